```python
import jax, jax.numpy as jnp
from jax import lax
import numpy as np

D_MODEL = 1024
BATCH = 32
SEQ = 2048
DEPTH = 2

N_Q_HEADS = 8
N_KV_HEADS = 2
HEAD_DIM = 64
WINDOW = 128
ATTN_BLOCK = 128
ATTN_WIDTH = N_Q_HEADS * HEAD_DIM
KV_WIDTH = N_KV_HEADS * HEAD_DIM
CONV_WIDTH = D_MODEL - ATTN_WIDTH
CONV_KERNEL = 31
IN0_WIDTH = ATTN_WIDTH + 2 * KV_WIDTH + 2 * CONV_WIDTH
POOL_WINDOWS = (2, 4, 8, 16)
POOL_WIDTH = D_MODEL // 2
POOL_GROUP = POOL_WIDTH // len(POOL_WINDOWS)
SGU_WIDTH = D_MODEL - POOL_WIDTH
SGU_HEADS = 4
SGU_HEAD_DIM = SGU_WIDTH // SGU_HEADS
SGU_CHUNK = 128
IN1_WIDTH = POOL_WIDTH + 2 * SGU_WIDTH
D_FF = -(-(8 * D_MODEL) // (3 * 256)) * 256
N_EVEN = (DEPTH + 1) // 2
N_ODD = DEPTH // 2
EPS = 1e-5

kernel_name = "hybrid_swa_conformer_pool_sgu"


def rms_norm(x, g):
    xf = x.astype(jnp.float32)
    y = xf * lax.rsqrt(jnp.mean(xf * xf, axis=-1, keepdims=True) + EPS)
    return (y * g.astype(jnp.float32)).astype(x.dtype)


def layer_norm(x, g, b):
    xf = x.astype(jnp.float32)
    mu = jnp.mean(xf, axis=-1, keepdims=True)
    var = jnp.mean(jnp.square(xf - mu), axis=-1, keepdims=True)
    y = (xf - mu) * lax.rsqrt(var + EPS)
    return (y * g.astype(jnp.float32) + b.astype(jnp.float32)).astype(x.dtype)


def sliding_window_attention(q, k, v, sinks):
    B, S = q.shape[0], q.shape[1]
    nb = S // ATTN_BLOCK
    G = N_Q_HEADS // N_KV_HEADS
    qb = q.reshape(B, nb, ATTN_BLOCK, N_KV_HEADS, G, HEAD_DIM)
    kb = k.reshape(B, nb, ATTN_BLOCK, N_KV_HEADS, HEAD_DIM)
    vb = v.reshape(B, nb, ATTN_BLOCK, N_KV_HEADS, HEAD_DIM)

    def with_prev(t):
        prev = jnp.pad(t, ((0, 0), (1, 0), (0, 0), (0, 0), (0, 0)))[:, :-1]
        return jnp.concatenate([prev, t], axis=2)

    kw, vw = with_prev(kb), with_prev(vb)
    logits = jnp.einsum('bnqkgd,bnskd->bnkgqs', qb, kw).astype(jnp.float32) * (HEAD_DIM ** -0.5)
    qi = jnp.arange(ATTN_BLOCK)[:, None]
    r = jnp.arange(2 * ATTN_BLOCK)[None, :]
    dist = qi + ATTN_BLOCK - r
    band = (dist >= 0) & (dist < WINDOW)
    key_pos = jnp.arange(nb)[:, None, None] * ATTN_BLOCK + r[None] - ATTN_BLOCK
    mask = band[None] & (key_pos >= 0)
    logits = jnp.where(mask[None, :, None, None], logits, -jnp.inf)
    sink = sinks.astype(jnp.float32).reshape(1, 1, N_KV_HEADS, G, 1, 1)
    m = jnp.maximum(jnp.max(logits, axis=-1, keepdims=True), sink)
    p = jnp.exp(logits - m)
    probs = p / (jnp.sum(p, axis=-1, keepdims=True) + jnp.exp(sink - m))
    out = jnp.einsum('bnkgqs,bnskd->bnqkgd', probs.astype(v.dtype), vw)
    return out.reshape(B, S, ATTN_WIDTH)


def conformer_conv(c, conv_w, conv_b, ln_g, ln_b):
    a, gate = jnp.split(c, 2, axis=-1)
    h = a * jax.nn.sigmoid(gate)
    h = lax.conv_general_dilated(
        h, conv_w[:, None, :].astype(h.dtype), window_strides=(1,),
        padding=[(CONV_KERNEL - 1, 0)],
        dimension_numbers=('NWC', 'WIO', 'NWC'),
        feature_group_count=CONV_WIDTH) + conv_b
    h = layer_norm(h, ln_g, ln_b)
    return jax.nn.silu(h)


def multiscale_pool(z, w_pool, scale):
    S = z.shape[1]
    zf = z.astype(jnp.float32)
    cs = jnp.cumsum(zf, axis=1)
    t = jnp.arange(S)
    outs = []
    for g, w in enumerate(POOL_WINDOWS):
        lo, hi = g * POOL_GROUP, (g + 1) * POOL_GROUP
        c = cs[..., lo:hi]
        prev = jnp.pad(c, ((0, 0), (w, 0), (0, 0)))[:, :S]
        cnt = jnp.minimum(t + 1, w).astype(jnp.float32)[:, None]
        pooled = (c - prev) / cnt - zf[..., lo:hi]
        outs.append(jnp.einsum('bsc,cd->bsd', pooled.astype(z.dtype), w_pool[g]))
    return jnp.concatenate(outs, axis=-1) * scale


def chunked_spatial_gating(z, ln_g, ln_b, w_s, b_s):
    B, S = z.shape[0], z.shape[1]
    u, v = jnp.split(jax.nn.gelu(z), 2, axis=-1)
    v = layer_norm(v, ln_g, ln_b)
    nc = S // SGU_CHUNK
    vc = v.reshape(B, nc, SGU_CHUNK, SGU_HEADS, SGU_HEAD_DIM)
    causal = jnp.tril(jnp.ones((SGU_CHUNK, SGU_CHUNK), dtype=bool))
    w = jnp.where(causal[None], w_s, jnp.zeros_like(w_s))
    mixed = jnp.einsum('gts,bcsgh->bctgh', w, vc) + b_s.T[None, None, :, :, None]
    return u * mixed.reshape(B, S, SGU_WIDTH)


def swiglu(h, w_gate, w_up, w_down):
    return (jax.nn.silu(h @ w_gate) * (h @ w_up)) @ w_down


def _fwd_setup_inputs(seed: int = 0) -> dict:
    key = jax.random.key(seed)
    ks = jax.random.split(key, 24)
    f32 = jnp.float32

    def nrm(k, shape, s):
        return jax.random.normal(k, shape, f32) * s

    def gain(k, shape):
        return 1.0 + 0.02 * jax.random.normal(k, shape, f32)

    return {
        'x': jax.random.normal(ks[0], (BATCH, SEQ, D_MODEL), f32),
        'mix_norm': gain(ks[1], (DEPTH, D_MODEL)),
        'a_w_in': nrm(ks[2], (N_EVEN, D_MODEL, IN0_WIDTH), D_MODEL ** -0.5),
        'a_b_in': nrm(ks[3], (N_EVEN, IN0_WIDTH), 0.02),
        'a_sinks': nrm(ks[4], (N_EVEN, N_Q_HEADS), 1.0),
        'a_conv_w': nrm(ks[5], (N_EVEN, CONV_KERNEL, CONV_WIDTH), CONV_KERNEL ** -0.5),
        'a_conv_b': nrm(ks[6], (N_EVEN, CONV_WIDTH), 0.02),
        'a_cln_g': gain(ks[7], (N_EVEN, CONV_WIDTH)),
        'a_cln_b': nrm(ks[8], (N_EVEN, CONV_WIDTH), 0.02),
        'a_w_out': nrm(ks[9], (N_EVEN, D_MODEL, D_MODEL), D_MODEL ** -0.5),
        'c_w_in': nrm(ks[10], (N_ODD, D_MODEL, IN1_WIDTH), D_MODEL ** -0.5),
        'c_w_pool': nrm(ks[11], (N_ODD, len(POOL_WINDOWS), POOL_GROUP, POOL_GROUP), POOL_GROUP ** -0.5),
        'c_pool_scale': gain(ks[12], (N_ODD, POOL_WIDTH)),
        'c_sln_g': gain(ks[13], (N_ODD, SGU_WIDTH)),
        'c_sln_b': nrm(ks[14], (N_ODD, SGU_WIDTH), 0.02),
        'c_w_s': nrm(ks[15], (N_ODD, SGU_HEADS, SGU_CHUNK, SGU_CHUNK), SGU_CHUNK ** -0.5),
        'c_b_s': gain(ks[16], (N_ODD, SGU_HEADS, SGU_CHUNK)),
        'c_w_out': nrm(ks[17], (N_ODD, D_MODEL, D_MODEL), D_MODEL ** -0.5),
        'ffn_norm': gain(ks[18], (DEPTH, D_MODEL)),
        'ffn_w_gate': nrm(ks[19], (DEPTH, D_MODEL, D_FF), D_MODEL ** -0.5),
        'ffn_w_up': nrm(ks[20], (DEPTH, D_MODEL, D_FF), D_MODEL ** -0.5),
        'ffn_w_down': nrm(ks[21], (DEPTH, D_FF, D_MODEL), D_FF ** -0.5),
        'final_norm': gain(ks[22], (D_MODEL,)),
    }


def _fwd_reference(x, mix_norm, a_w_in, a_b_in, a_sinks, a_conv_w, a_conv_b, a_cln_g, a_cln_b, a_w_out,
              c_w_in, c_w_pool, c_pool_scale, c_sln_g, c_sln_b, c_w_s, c_b_s, c_w_out,
              ffn_norm, ffn_w_gate, ffn_w_up, ffn_w_down, final_norm):
    B, S = x.shape[0], x.shape[1]
    h = x
    for i in range(DEPTH):
        j = i // 2
        hn = rms_norm(h, mix_norm[i])
        if i % 2 == 0:
            z = hn @ a_w_in[j] + a_b_in[j]
            q = z[..., :ATTN_WIDTH].reshape(B, S, N_Q_HEADS, HEAD_DIM)
            k = z[..., ATTN_WIDTH:ATTN_WIDTH + KV_WIDTH].reshape(B, S, N_KV_HEADS, HEAD_DIM)
            v = z[..., ATTN_WIDTH + KV_WIDTH:ATTN_WIDTH + 2 * KV_WIDTH].reshape(B, S, N_KV_HEADS, HEAD_DIM)
            c = z[..., ATTN_WIDTH + 2 * KV_WIDTH:]
            attn = sliding_window_attention(q, k, v, a_sinks[j])
            conv = conformer_conv(c, a_conv_w[j], a_conv_b[j], a_cln_g[j], a_cln_b[j])
            h = h + jnp.concatenate([attn, conv], axis=-1) @ a_w_out[j]
        else:
            z = hn @ c_w_in[j]
            pool = multiscale_pool(z[..., :POOL_WIDTH], c_w_pool[j], c_pool_scale[j])
            sgu = chunked_spatial_gating(z[..., POOL_WIDTH:], c_sln_g[j], c_sln_b[j], c_w_s[j], c_b_s[j])
            h = h + jnp.concatenate([pool, sgu], axis=-1) @ c_w_out[j]
        h = h + swiglu(rms_norm(h, ffn_norm[i]), ffn_w_gate[i], ffn_w_up[i], ffn_w_down[i])
    return rms_norm(h, final_norm)


import jax as _jax
import jax.numpy as _jnp

TWIN_FORMAT = 'train_step'
FWD_PARAMS = ['x', 'mix_norm', 'a_w_in', 'a_b_in', 'a_sinks', 'a_conv_w', 'a_conv_b', 'a_cln_g', 'a_cln_b', 'a_w_out', 'c_w_in', 'c_w_pool', 'c_pool_scale', 'c_sln_g', 'c_sln_b', 'c_w_s', 'c_b_s', 'c_w_out', 'ffn_norm', 'ffn_w_gate', 'ffn_w_up', 'ffn_w_down', 'final_norm']
TWIN_WEIGHTS = ['mix_norm', 'a_w_in', 'a_b_in', 'a_sinks', 'a_conv_w', 'a_conv_b', 'a_cln_g', 'a_cln_b', 'a_w_out', 'c_w_in', 'c_w_pool', 'c_pool_scale', 'c_sln_g', 'c_sln_b', 'c_w_s', 'c_b_s', 'c_w_out', 'ffn_norm', 'ffn_w_gate', 'ffn_w_up', 'ffn_w_down', 'final_norm']
TWIN_DIFF_INPUT = 'x'
TWIN_INPUTS = ['x', 'mix_norm', 'a_w_in', 'a_b_in', 'a_sinks', 'a_conv_w', 'a_conv_b', 'a_cln_g', 'a_cln_b', 'a_w_out', 'c_w_in', 'c_w_pool', 'c_pool_scale', 'c_sln_g', 'c_sln_b', 'c_w_s', 'c_b_s', 'c_w_out', 'ffn_norm', 'ffn_w_gate', 'ffn_w_up', 'ffn_w_down', 'final_norm', 'loss_target', 'm_mix_norm', 'm_a_w_in', 'm_a_b_in', 'm_a_sinks', 'm_a_conv_w', 'm_a_conv_b', 'm_a_cln_g', 'm_a_cln_b', 'm_a_w_out', 'm_c_w_in', 'm_c_w_pool', 'm_c_pool_scale', 'm_c_sln_g', 'm_c_sln_b', 'm_c_w_s', 'm_c_b_s', 'm_c_w_out', 'm_ffn_norm', 'm_ffn_w_gate', 'm_ffn_w_up', 'm_ffn_w_down', 'm_final_norm', 'v_mix_norm', 'v_a_w_in', 'v_a_b_in', 'v_a_sinks', 'v_a_conv_w', 'v_a_conv_b', 'v_a_cln_g', 'v_a_cln_b', 'v_a_w_out', 'v_c_w_in', 'v_c_w_pool', 'v_c_pool_scale', 'v_c_sln_g', 'v_c_sln_b', 'v_c_w_s', 'v_c_b_s', 'v_c_w_out', 'v_ffn_norm', 'v_ffn_w_gate', 'v_ffn_w_up', 'v_ffn_w_down', 'v_final_norm']
TWIN_OUTPUTS = ['loss', 'grad_x', 'grad_mix_norm', 'grad_a_w_in', 'grad_a_b_in', 'grad_a_sinks', 'grad_a_conv_w', 'grad_a_conv_b', 'grad_a_cln_g', 'grad_a_cln_b', 'grad_a_w_out', 'grad_c_w_in', 'grad_c_w_pool', 'grad_c_pool_scale', 'grad_c_sln_g', 'grad_c_sln_b', 'grad_c_w_s', 'grad_c_b_s', 'grad_c_w_out', 'grad_ffn_norm', 'grad_ffn_w_gate', 'grad_ffn_w_up', 'grad_ffn_w_down', 'grad_final_norm', 'delta_mix_norm', 'delta_a_w_in', 'delta_a_b_in', 'delta_a_sinks', 'delta_a_conv_w', 'delta_a_conv_b', 'delta_a_cln_g', 'delta_a_cln_b', 'delta_a_w_out', 'delta_c_w_in', 'delta_c_w_pool', 'delta_c_pool_scale', 'delta_c_sln_g', 'delta_c_sln_b', 'delta_c_w_s', 'delta_c_b_s', 'delta_c_w_out', 'delta_ffn_norm', 'delta_ffn_w_gate', 'delta_ffn_w_up', 'delta_ffn_w_down', 'delta_final_norm', 'new_m_mix_norm', 'new_m_a_w_in', 'new_m_a_b_in', 'new_m_a_sinks', 'new_m_a_conv_w', 'new_m_a_conv_b', 'new_m_a_cln_g', 'new_m_a_cln_b', 'new_m_a_w_out', 'new_m_c_w_in', 'new_m_c_w_pool', 'new_m_c_pool_scale', 'new_m_c_sln_g', 'new_m_c_sln_b', 'new_m_c_w_s', 'new_m_c_b_s', 'new_m_c_w_out', 'new_m_ffn_norm', 'new_m_ffn_w_gate', 'new_m_ffn_w_up', 'new_m_ffn_w_down', 'new_m_final_norm', 'new_v_mix_norm', 'new_v_a_w_in', 'new_v_a_b_in', 'new_v_a_sinks', 'new_v_a_conv_w', 'new_v_a_conv_b', 'new_v_a_cln_g', 'new_v_a_cln_b', 'new_v_a_w_out', 'new_v_c_w_in', 'new_v_c_w_pool', 'new_v_c_pool_scale', 'new_v_c_sln_g', 'new_v_c_sln_b', 'new_v_c_w_s', 'new_v_c_b_s', 'new_v_c_w_out', 'new_v_ffn_norm', 'new_v_ffn_w_gate', 'new_v_ffn_w_up', 'new_v_ffn_w_down', 'new_v_final_norm']
TWIN_LEAF_KINDS = {'loss': 'loss', 'grad_x': 'grad_x', 'grad_mix_norm': 'grad_w', 'grad_a_w_in': 'grad_w', 'grad_a_b_in': 'grad_w', 'grad_a_sinks': 'grad_w', 'grad_a_conv_w': 'grad_w', 'grad_a_conv_b': 'grad_w', 'grad_a_cln_g': 'grad_w', 'grad_a_cln_b': 'grad_w', 'grad_a_w_out': 'grad_w', 'grad_c_w_in': 'grad_w', 'grad_c_w_pool': 'grad_w', 'grad_c_pool_scale': 'grad_w', 'grad_c_sln_g': 'grad_w', 'grad_c_sln_b': 'grad_w', 'grad_c_w_s': 'grad_w', 'grad_c_b_s': 'grad_w', 'grad_c_w_out': 'grad_w', 'grad_ffn_norm': 'grad_w', 'grad_ffn_w_gate': 'grad_w', 'grad_ffn_w_up': 'grad_w', 'grad_ffn_w_down': 'grad_w', 'grad_final_norm': 'grad_w', 'delta_mix_norm': 'delta_w', 'delta_a_w_in': 'delta_w', 'delta_a_b_in': 'delta_w', 'delta_a_sinks': 'delta_w', 'delta_a_conv_w': 'delta_w', 'delta_a_conv_b': 'delta_w', 'delta_a_cln_g': 'delta_w', 'delta_a_cln_b': 'delta_w', 'delta_a_w_out': 'delta_w', 'delta_c_w_in': 'delta_w', 'delta_c_w_pool': 'delta_w', 'delta_c_pool_scale': 'delta_w', 'delta_c_sln_g': 'delta_w', 'delta_c_sln_b': 'delta_w', 'delta_c_w_s': 'delta_w', 'delta_c_b_s': 'delta_w', 'delta_c_w_out': 'delta_w', 'delta_ffn_norm': 'delta_w', 'delta_ffn_w_gate': 'delta_w', 'delta_ffn_w_up': 'delta_w', 'delta_ffn_w_down': 'delta_w', 'delta_final_norm': 'delta_w', 'new_m_mix_norm': 'new_m', 'new_m_a_w_in': 'new_m', 'new_m_a_b_in': 'new_m', 'new_m_a_sinks': 'new_m', 'new_m_a_conv_w': 'new_m', 'new_m_a_conv_b': 'new_m', 'new_m_a_cln_g': 'new_m', 'new_m_a_cln_b': 'new_m', 'new_m_a_w_out': 'new_m', 'new_m_c_w_in': 'new_m', 'new_m_c_w_pool': 'new_m', 'new_m_c_pool_scale': 'new_m', 'new_m_c_sln_g': 'new_m', 'new_m_c_sln_b': 'new_m', 'new_m_c_w_s': 'new_m', 'new_m_c_b_s': 'new_m', 'new_m_c_w_out': 'new_m', 'new_m_ffn_norm': 'new_m', 'new_m_ffn_w_gate': 'new_m', 'new_m_ffn_w_up': 'new_m', 'new_m_ffn_w_down': 'new_m', 'new_m_final_norm': 'new_m', 'new_v_mix_norm': 'new_v', 'new_v_a_w_in': 'new_v', 'new_v_a_b_in': 'new_v', 'new_v_a_sinks': 'new_v', 'new_v_a_conv_w': 'new_v', 'new_v_a_conv_b': 'new_v', 'new_v_a_cln_g': 'new_v', 'new_v_a_cln_b': 'new_v', 'new_v_a_w_out': 'new_v', 'new_v_c_w_in': 'new_v', 'new_v_c_w_pool': 'new_v', 'new_v_c_pool_scale': 'new_v', 'new_v_c_sln_g': 'new_v', 'new_v_c_sln_b': 'new_v', 'new_v_c_w_s': 'new_v', 'new_v_c_b_s': 'new_v', 'new_v_c_w_out': 'new_v', 'new_v_ffn_norm': 'new_v', 'new_v_ffn_w_gate': 'new_v', 'new_v_ffn_w_up': 'new_v', 'new_v_ffn_w_down': 'new_v', 'new_v_final_norm': 'new_v'}


def _forward(args):
    return _fwd_reference(*[args[k] for k in FWD_PARAMS])


def _output_shape():
    out = _jax.eval_shape(lambda: _forward(_fwd_setup_inputs(0)))
    return out.shape, out.dtype

N_MICROBATCH = 1
ADAM_LR = 0.001
ADAM_B1 = 0.9
ADAM_B2 = 0.999
ADAM_EPS = 1e-08
ADAM_WD = 0.01
ADAM_STEP = 10
PER_EXAMPLE_BATCH_AXIS = {'x': 0, 'loss_target': 0}
SHARED_INPUTS = []
_WEIGHT_DTYPES = {'mix_norm': _jnp.float32, 'a_w_in': _jnp.float32, 'a_b_in': _jnp.float32, 'a_sinks': _jnp.float32, 'a_conv_w': _jnp.float32, 'a_conv_b': _jnp.float32, 'a_cln_g': _jnp.float32, 'a_cln_b': _jnp.float32, 'a_w_out': _jnp.float32, 'c_w_in': _jnp.float32, 'c_w_pool': _jnp.float32, 'c_pool_scale': _jnp.float32, 'c_sln_g': _jnp.float32, 'c_sln_b': _jnp.float32, 'c_w_s': _jnp.float32, 'c_b_s': _jnp.float32, 'c_w_out': _jnp.float32, 'ffn_norm': _jnp.float32, 'ffn_w_gate': _jnp.float32, 'ffn_w_up': _jnp.float32, 'ffn_w_down': _jnp.float32, 'final_norm': _jnp.float32}
MOMENT_SCALE = {'mix_norm': 1.544062e-01, 'a_w_in': 1.122750e-01, 'a_b_in': 2.412249e-01, 'a_sinks': 4.877466e-02, 'a_conv_w': 1.787235e-01, 'a_conv_b': 3.670412e-01, 'a_cln_g': 2.319417e-01, 'a_cln_b': 2.183773e-01, 'a_w_out': 1.266714e-01, 'c_w_in': 1.354557e-01, 'c_w_pool': 1.559776e-01, 'c_pool_scale': 1.586128e-01, 'c_sln_g': 9.048025e-02, 'c_sln_b': 7.976698e-02, 'c_w_s': 8.212031e-02, 'c_b_s': 1.165150e-01, 'c_w_out': 1.521159e-01, 'ffn_norm': 1.709047e-01, 'ffn_w_gate': 7.163569e-02, 'ffn_w_up': 6.959382e-02, 'ffn_w_down': 1.151749e-01, 'final_norm': 6.395599e+01}


def _to_microbatches(a, axis):
    t = _jnp.moveaxis(a, axis, 0)
    t = t.reshape((N_MICROBATCH, t.shape[0] // N_MICROBATCH) + t.shape[1:])
    return _jnp.moveaxis(t, 1, axis + 1)


def setup_inputs(seed: int = 0) -> dict:
    inp = _fwd_setup_inputs(seed)
    key = _jax.random.fold_in(_jax.random.key(seed), 7919)
    shape, _ = _output_shape()
    out = dict(inp)
    out["loss_target"] = _jax.random.normal(_jax.random.fold_in(key, 0), shape, _jnp.float32)
    for i, name in enumerate(TWIN_WEIGHTS):
        w = inp[name].astype(_jnp.float32)
        if MOMENT_SCALE is None:
            s = _jnp.sqrt(_jnp.mean(_jnp.square(w)) + 1e-30)
        else:
            s = MOMENT_SCALE[name]
        km, kv = _jax.random.split(_jax.random.fold_in(key, i + 1))
        out[name] = w
        out["m_" + name] = s * _jax.random.normal(km, w.shape, _jnp.float32)
        out["v_" + name] = (s * s) * _jax.random.uniform(kv, w.shape, _jnp.float32, 0.5, 1.5)
    if N_MICROBATCH > 1:
        for name, axis in PER_EXAMPLE_BATCH_AXIS.items():
            out[name] = _to_microbatches(out[name], axis)
    return {'x': out['x'], 'mix_norm': out['mix_norm'], 'a_w_in': out['a_w_in'], 'a_b_in': out['a_b_in'], 'a_sinks': out['a_sinks'], 'a_conv_w': out['a_conv_w'], 'a_conv_b': out['a_conv_b'], 'a_cln_g': out['a_cln_g'], 'a_cln_b': out['a_cln_b'], 'a_w_out': out['a_w_out'], 'c_w_in': out['c_w_in'], 'c_w_pool': out['c_w_pool'], 'c_pool_scale': out['c_pool_scale'], 'c_sln_g': out['c_sln_g'], 'c_sln_b': out['c_sln_b'], 'c_w_s': out['c_w_s'], 'c_b_s': out['c_b_s'], 'c_w_out': out['c_w_out'], 'ffn_norm': out['ffn_norm'], 'ffn_w_gate': out['ffn_w_gate'], 'ffn_w_up': out['ffn_w_up'], 'ffn_w_down': out['ffn_w_down'], 'final_norm': out['final_norm'], 'loss_target': out['loss_target'], 'm_mix_norm': out['m_mix_norm'], 'm_a_w_in': out['m_a_w_in'], 'm_a_b_in': out['m_a_b_in'], 'm_a_sinks': out['m_a_sinks'], 'm_a_conv_w': out['m_a_conv_w'], 'm_a_conv_b': out['m_a_conv_b'], 'm_a_cln_g': out['m_a_cln_g'], 'm_a_cln_b': out['m_a_cln_b'], 'm_a_w_out': out['m_a_w_out'], 'm_c_w_in': out['m_c_w_in'], 'm_c_w_pool': out['m_c_w_pool'], 'm_c_pool_scale': out['m_c_pool_scale'], 'm_c_sln_g': out['m_c_sln_g'], 'm_c_sln_b': out['m_c_sln_b'], 'm_c_w_s': out['m_c_w_s'], 'm_c_b_s': out['m_c_b_s'], 'm_c_w_out': out['m_c_w_out'], 'm_ffn_norm': out['m_ffn_norm'], 'm_ffn_w_gate': out['m_ffn_w_gate'], 'm_ffn_w_up': out['m_ffn_w_up'], 'm_ffn_w_down': out['m_ffn_w_down'], 'm_final_norm': out['m_final_norm'], 'v_mix_norm': out['v_mix_norm'], 'v_a_w_in': out['v_a_w_in'], 'v_a_b_in': out['v_a_b_in'], 'v_a_sinks': out['v_a_sinks'], 'v_a_conv_w': out['v_a_conv_w'], 'v_a_conv_b': out['v_a_conv_b'], 'v_a_cln_g': out['v_a_cln_g'], 'v_a_cln_b': out['v_a_cln_b'], 'v_a_w_out': out['v_a_w_out'], 'v_c_w_in': out['v_c_w_in'], 'v_c_w_pool': out['v_c_w_pool'], 'v_c_pool_scale': out['v_c_pool_scale'], 'v_c_sln_g': out['v_c_sln_g'], 'v_c_sln_b': out['v_c_sln_b'], 'v_c_w_s': out['v_c_w_s'], 'v_c_b_s': out['v_c_b_s'], 'v_c_w_out': out['v_c_w_out'], 'v_ffn_norm': out['v_ffn_norm'], 'v_ffn_w_gate': out['v_ffn_w_gate'], 'v_ffn_w_up': out['v_ffn_w_up'], 'v_ffn_w_down': out['v_ffn_w_down'], 'v_final_norm': out['v_final_norm']}


def _loss(weights, diff, rest, loss_target):
    with _jax.named_scope("forward"):
        args = {**rest, TWIN_DIFF_INPUT: diff, **{k: w.astype(_WEIGHT_DTYPES[k]) for k, w in weights.items()}}
        y = _forward(args)
    with _jax.named_scope("loss_head"):
        err = _jnp.square(y.astype(_jnp.float32) - loss_target)
        return 0.5 * _jnp.sum(_jnp.mean(err, axis=-1)) if err.ndim else 0.5 * err


def _adamw(w, g, m, v):
    m = ADAM_B1 * m + (1.0 - ADAM_B1) * g
    v = ADAM_B2 * v + (1.0 - ADAM_B2) * _jnp.square(g)
    m_hat = m / (1.0 - ADAM_B1 ** ADAM_STEP)
    v_hat = v / (1.0 - ADAM_B2 ** ADAM_STEP)
    delta = -ADAM_LR * (m_hat / (_jnp.sqrt(v_hat) + ADAM_EPS) + ADAM_WD * w)
    return delta, m, v


def reference(x, mix_norm, a_w_in, a_b_in, a_sinks, a_conv_w, a_conv_b, a_cln_g, a_cln_b, a_w_out, c_w_in, c_w_pool, c_pool_scale, c_sln_g, c_sln_b, c_w_s, c_b_s, c_w_out, ffn_norm, ffn_w_gate, ffn_w_up, ffn_w_down, final_norm, loss_target, m_mix_norm, m_a_w_in, m_a_b_in, m_a_sinks, m_a_conv_w, m_a_conv_b, m_a_cln_g, m_a_cln_b, m_a_w_out, m_c_w_in, m_c_w_pool, m_c_pool_scale, m_c_sln_g, m_c_sln_b, m_c_w_s, m_c_b_s, m_c_w_out, m_ffn_norm, m_ffn_w_gate, m_ffn_w_up, m_ffn_w_down, m_final_norm, v_mix_norm, v_a_w_in, v_a_b_in, v_a_sinks, v_a_conv_w, v_a_conv_b, v_a_cln_g, v_a_cln_b, v_a_w_out, v_c_w_in, v_c_w_pool, v_c_pool_scale, v_c_sln_g, v_c_sln_b, v_c_w_s, v_c_b_s, v_c_w_out, v_ffn_norm, v_ffn_w_gate, v_ffn_w_up, v_ffn_w_down, v_final_norm):
    given = dict(x=x, mix_norm=mix_norm, a_w_in=a_w_in, a_b_in=a_b_in, a_sinks=a_sinks, a_conv_w=a_conv_w, a_conv_b=a_conv_b, a_cln_g=a_cln_g, a_cln_b=a_cln_b, a_w_out=a_w_out, c_w_in=c_w_in, c_w_pool=c_w_pool, c_pool_scale=c_pool_scale, c_sln_g=c_sln_g, c_sln_b=c_sln_b, c_w_s=c_w_s, c_b_s=c_b_s, c_w_out=c_w_out, ffn_norm=ffn_norm, ffn_w_gate=ffn_w_gate, ffn_w_up=ffn_w_up, ffn_w_down=ffn_w_down, final_norm=final_norm, loss_target=loss_target, m_mix_norm=m_mix_norm, m_a_w_in=m_a_w_in, m_a_b_in=m_a_b_in, m_a_sinks=m_a_sinks, m_a_conv_w=m_a_conv_w, m_a_conv_b=m_a_conv_b, m_a_cln_g=m_a_cln_g, m_a_cln_b=m_a_cln_b, m_a_w_out=m_a_w_out, m_c_w_in=m_c_w_in, m_c_w_pool=m_c_w_pool, m_c_pool_scale=m_c_pool_scale, m_c_sln_g=m_c_sln_g, m_c_sln_b=m_c_sln_b, m_c_w_s=m_c_w_s, m_c_b_s=m_c_b_s, m_c_w_out=m_c_w_out, m_ffn_norm=m_ffn_norm, m_ffn_w_gate=m_ffn_w_gate, m_ffn_w_up=m_ffn_w_up, m_ffn_w_down=m_ffn_w_down, m_final_norm=m_final_norm, v_mix_norm=v_mix_norm, v_a_w_in=v_a_w_in, v_a_b_in=v_a_b_in, v_a_sinks=v_a_sinks, v_a_conv_w=v_a_conv_w, v_a_conv_b=v_a_conv_b, v_a_cln_g=v_a_cln_g, v_a_cln_b=v_a_cln_b, v_a_w_out=v_a_w_out, v_c_w_in=v_c_w_in, v_c_w_pool=v_c_w_pool, v_c_pool_scale=v_c_pool_scale, v_c_sln_g=v_c_sln_g, v_c_sln_b=v_c_sln_b, v_c_w_s=v_c_w_s, v_c_b_s=v_c_b_s, v_c_w_out=v_c_w_out, v_ffn_norm=v_ffn_norm, v_ffn_w_gate=v_ffn_w_gate, v_ffn_w_up=v_ffn_w_up, v_ffn_w_down=v_ffn_w_down, v_final_norm=v_final_norm)
    weights = {n: given[n] for n in TWIN_WEIGHTS}
    shared = {n: given[n] for n in SHARED_INPUTS}
    per_example = {n: given[n] for n in ['x']}
    grad_fn = _jax.value_and_grad(_loss, argnums=(0, 1))

    def one_microbatch(ex, loss_target):
        ex = dict(ex)
        diff = ex.pop(TWIN_DIFF_INPUT)
        return grad_fn(weights, diff, {**shared, **ex}, loss_target)

    if N_MICROBATCH == 1:
        loss, (grad_w, grad_x) = one_microbatch(per_example, given["loss_target"])
    else:
        def body(carry, xs):
            loss_sum, grad_sum = carry
            l_k, (gw_k, gx_k) = one_microbatch(xs[0], xs[1])
            with _jax.named_scope("update"):
                return (loss_sum + l_k, _jax.tree.map(_jnp.add, grad_sum, gw_k)), gx_k

        init = (_jnp.zeros((), _jnp.float32), _jax.tree.map(_jnp.zeros_like, weights))
        (loss, grad_w), grad_x = _jax.lax.scan(body, init, (per_example, given["loss_target"]))
    with _jax.named_scope("update"):
        delta_w, new_m, new_v = {}, {}, {}
        for n in TWIN_WEIGHTS:
            delta_w[n], new_m[n], new_v[n] = _adamw(weights[n], grad_w[n], given["m_" + n], given["v_" + n])
    return (loss, grad_x, *[grad_w[n] for n in TWIN_WEIGHTS], *[delta_w[n] for n in TWIN_WEIGHTS],
            *[new_m[n] for n in TWIN_WEIGHTS], *[new_v[n] for n in TWIN_WEIGHTS])
```

```python
import functools

import jax
import jax.numpy as jnp
from jax import lax
from jax.experimental import pallas as pl
from jax.experimental.pallas import tpu as pltpu

F32 = jnp.float32
BF16 = jnp.bfloat16

D_MODEL = 1024
EPS = 1e-5
N_Q_HEADS, N_KV_HEADS, HEAD_DIM = 8, 2, 64
ATTN_BLOCK = 128
ATTN_WIDTH = N_Q_HEADS * HEAD_DIM
KV_WIDTH = N_KV_HEADS * HEAD_DIM
CONV_WIDTH = 512
CONV_KERNEL = 31
CONV_HALO = 32
POOL_WINDOWS = (2, 4, 8, 16)
POOL_WIDTH = 512
POOL_HALO = 16
SGU_WIDTH = 512
SGU_CHUNK = 128
D_FF = 2816
LANES = 128
N_CHIPS = 4
N_DEV = 8

ADAM_LR, ADAM_B1, ADAM_B2, ADAM_EPS, ADAM_WD, ADAM_STEP = 0.001, 0.9, 0.999, 1e-08, 0.01, 10

VMEM_LIMIT = 56 * 2**20

WEIGHTS = ['mix_norm', 'a_w_in', 'a_b_in', 'a_sinks', 'a_conv_w', 'a_conv_b', 'a_cln_g', 'a_cln_b', 'a_w_out',
           'c_w_in', 'c_w_pool', 'c_pool_scale', 'c_sln_g', 'c_sln_b', 'c_w_s', 'c_b_s', 'c_w_out',
           'ffn_norm', 'ffn_w_gate', 'ffn_w_up', 'ffn_w_down', 'final_norm']
IN_NAMES = (['x'] + WEIGHTS + ['loss_target'] + ['m_' + n for n in WEIGHTS] + ['v_' + n for n in WEIGHTS])


def _params(*sem):
    return pltpu.CompilerParams(dimension_semantics=sem, vmem_limit_bytes=VMEM_LIMIT)


def _dot(a, b):
    return jnp.dot(a, b, preferred_element_type=F32)


def _dot_nt(a, b):
    return lax.dot_general(a, b, (((1,), (1,)), ((), ())), preferred_element_type=F32)


def _dot_tn(a, b):
    return lax.dot_general(a, b, (((0,), (0,)), ((), ())), preferred_element_type=F32)


def _sigmoid(v):
    return 1.0 / (1.0 + jnp.exp(-v))


def _row_tile(n, pref):
    t = min(n, pref)
    while n % t:
        t //= 2
    return t


def _col_tile(m, rows, budget=6 * 2**20):
    best = LANES
    for t in range(LANES, m + 1, LANES):
        if m % t == 0 and rows * t * 4 <= budget:
            best = t
    return best


def norm_inproj(x, gain, w, bias, splits, dtypes, name):
    n = x.shape[0]
    m = w.shape[1]
    tm = _row_tile(n, 256)

    def body(x_ref, g_ref, w_ref, b_ref, hn_ref, *outs):
        xv = x_ref[...]
        r = lax.rsqrt(jnp.mean(xv * xv, axis=-1, keepdims=True) + EPS)
        hn = ((xv * r) * g_ref[...]).astype(BF16)
        hn_ref[...] = hn
        z = _dot(hn, w_ref[...]) + b_ref[...]
        for o, (lo, hi) in zip(outs, splits):
            o[...] = z[:, lo:hi].astype(o.dtype)

    out_shape = [jax.ShapeDtypeStruct((n, D_MODEL), BF16)]
    out_specs = [pl.BlockSpec((tm, D_MODEL), lambda i: (i, 0))]
    for (lo, hi), dt in zip(splits, dtypes):
        out_shape.append(jax.ShapeDtypeStruct((n, hi - lo), dt))
        out_specs.append(pl.BlockSpec((tm, hi - lo), lambda i: (i, 0)))
    return pl.pallas_call(
        body, name=name, grid=(n // tm,),
        in_specs=[pl.BlockSpec((tm, D_MODEL), lambda i: (i, 0)),
                  pl.BlockSpec((1, D_MODEL), lambda i: (0, 0)),
                  pl.BlockSpec((D_MODEL, m), lambda i: (0, 0)),
                  pl.BlockSpec((1, m), lambda i: (0, 0))],
        out_specs=out_specs, out_shape=out_shape,
        compiler_params=_params("parallel"),
    )(x, gain, w, bias)


def out_proj(res, m1, m2, w, name):
    n = res.shape[0]
    k1, k2 = m1.shape[1], m2.shape[1]
    assert k1 == k2
    tm = _row_tile(n, 512)

    def body(r_ref, a_ref, b_ref, w1_ref, w2_ref, o_ref):
        o_ref[...] = r_ref[...] + _dot(a_ref[...], w1_ref[...]) + _dot(b_ref[...], w2_ref[...])

    return pl.pallas_call(
        body, name=name, grid=(n // tm,),
        in_specs=[pl.BlockSpec((tm, D_MODEL), lambda i: (i, 0)),
                  pl.BlockSpec((tm, k1), lambda i: (i, 0)),
                  pl.BlockSpec((tm, k2), lambda i: (i, 0)),
                  pl.BlockSpec((k1, D_MODEL), lambda i: (0, 0)),
                  pl.BlockSpec((k2, D_MODEL), lambda i: (1, 0))],
        out_specs=pl.BlockSpec((tm, D_MODEL), lambda i: (i, 0)),
        out_shape=jax.ShapeDtypeStruct((n, D_MODEL), F32),
        compiler_params=_params("parallel"),
    )(res, m1, m2, w, w)


def ffn_gate_up(h, gain, wg, wu, name):
    n = h.shape[0]
    tm = _row_tile(n, 512)
    th = D_FF // 2

    def body(h_ref, g_ref, wg_ref, wu_ref, hn_ref, go_ref, uo_ref):
        @pl.when(pl.program_id(1) == 0)
        def _():
            xv = h_ref[...]
            r = lax.rsqrt(jnp.mean(xv * xv, axis=-1, keepdims=True) + EPS)
            hn_ref[...] = ((xv * r) * g_ref[...]).astype(BF16)

        hn = hn_ref[...]
        go_ref[...] = _dot(hn, wg_ref[...]).astype(BF16)
        uo_ref[...] = _dot(hn, wu_ref[...]).astype(BF16)

    return pl.pallas_call(
        body, name=name, grid=(n // tm, D_FF // th),
        in_specs=[pl.BlockSpec((tm, D_MODEL), lambda i, j: (i, 0)),
                  pl.BlockSpec((1, D_MODEL), lambda i, j: (0, 0)),
                  pl.BlockSpec((D_MODEL, th), lambda i, j: (0, j)),
                  pl.BlockSpec((D_MODEL, th), lambda i, j: (0, j))],
        out_specs=[pl.BlockSpec((tm, D_MODEL), lambda i, j: (i, 0)),
                   pl.BlockSpec((tm, th), lambda i, j: (i, j)),
                   pl.BlockSpec((tm, th), lambda i, j: (i, j))],
        out_shape=[jax.ShapeDtypeStruct((n, D_MODEL), BF16),
                   jax.ShapeDtypeStruct((n, D_FF), BF16),
                   jax.ShapeDtypeStruct((n, D_FF), BF16)],
        compiler_params=_params("parallel", "arbitrary"),
    )(h, gain, wg, wu)


def ffn_down(h, g, u, wd, name):
    n = h.shape[0]
    tm = _row_tile(n, 512)
    tk = D_FF // 2
    nk = D_FF // tk

    def body(h_ref, g_ref, u_ref, w_ref, o_ref, acc_ref):
        k = pl.program_id(1)
        gv = g_ref[...].astype(F32)
        a = (gv * _sigmoid(gv) * u_ref[...].astype(F32)).astype(BF16)
        part = _dot(a, w_ref[...])

        @pl.when(k == 0)
        def _():
            acc_ref[...] = part

        @pl.when(k > 0)
        def _():
            acc_ref[...] += part

        @pl.when(k == nk - 1)
        def _():
            o_ref[...] = h_ref[...] + acc_ref[...]

    return pl.pallas_call(
        body, name=name, grid=(n // tm, nk),
        in_specs=[pl.BlockSpec((tm, D_MODEL), lambda i, k: (i, 0)),
                  pl.BlockSpec((tm, tk), lambda i, k: (i, k)),
                  pl.BlockSpec((tm, tk), lambda i, k: (i, k)),
                  pl.BlockSpec((tk, D_MODEL), lambda i, k: (k, 0))],
        out_specs=pl.BlockSpec((tm, D_MODEL), lambda i, k: (i, 0)),
        out_shape=jax.ShapeDtypeStruct((n, D_MODEL), F32),
        scratch_shapes=[pltpu.VMEM((tm, D_MODEL), F32)],
        compiler_params=_params("parallel", "arbitrary"),
    )(h, g, u, wd)


def ffn_down_bwd(dh, g, u, wd, name):
    n = dh.shape[0]
    tm = _row_tile(n, 512)
    th = D_FF // 2

    def body(dh_ref, g_ref, u_ref, w_ref, dg_ref, du_ref, a_ref):
        da = _dot_nt(dh_ref[...].astype(BF16), w_ref[...])
        gv = g_ref[...].astype(F32)
        uv = u_ref[...].astype(F32)
        sg = _sigmoid(gv)
        act = gv * sg
        dg_ref[...] = (da * uv * (sg * (1.0 + gv * (1.0 - sg)))).astype(BF16)
        du_ref[...] = (da * act).astype(BF16)
        a_ref[...] = (act * uv).astype(BF16)

    spec_h = pl.BlockSpec((tm, th), lambda i, j: (i, j))
    return pl.pallas_call(
        body, name=name, grid=(n // tm, D_FF // th),
        in_specs=[pl.BlockSpec((tm, D_MODEL), lambda i, j: (i, 0)), spec_h, spec_h,
                  pl.BlockSpec((th, D_MODEL), lambda i, j: (j, 0))],
        out_specs=[spec_h, spec_h, spec_h],
        out_shape=[jax.ShapeDtypeStruct((n, D_FF), BF16)] * 3,
        compiler_params=_params("parallel", "arbitrary"),
    )(dh, g, u, wd)


def mm_tn(x, dy, name, colsum=False):
    n, k = x.shape
    m = dy.shape[1]
    tn = _col_tile(m, k)
    tt = _row_tile(n, 512)

    def body(x_ref, dy_ref, o_ref, *rest):
        t = pl.program_id(1)
        dyv = dy_ref[...]
        part = _dot_tn(x_ref[...], dyv.astype(BF16))

        @pl.when(t == 0)
        def _():
            o_ref[...] = part

        @pl.when(t > 0)
        def _():
            o_ref[...] += part

        if colsum:
            cs = jnp.sum(dyv.astype(F32), axis=0, keepdims=True)

            @pl.when(t == 0)
            def _():
                rest[0][...] = jnp.broadcast_to(cs, rest[0].shape)

            @pl.when(t > 0)
            def _():
                rest[0][...] += jnp.broadcast_to(cs, rest[0].shape)

    out_shape = [jax.ShapeDtypeStruct((k, m), F32)]
    out_specs = [pl.BlockSpec((k, tn), lambda j, t: (0, j))]
    if colsum:
        out_shape.append(jax.ShapeDtypeStruct((8, m), F32))
        out_specs.append(pl.BlockSpec((8, tn), lambda j, t: (0, j)))
    res = pl.pallas_call(
        body, name=name, grid=(m // tn, n // tt),
        in_specs=[pl.BlockSpec((tt, k), lambda j, t: (t, 0)),
                  pl.BlockSpec((tt, tn), lambda j, t: (t, j))],
        out_specs=out_specs, out_shape=out_shape,
        compiler_params=_params("parallel", "arbitrary"),
    )(x, dy)
    return (res[0], res[1][0]) if colsum else res[0]


def out_proj_bwd(dh, w, dtypes, name):
    n = dh.shape[0]
    k = w.shape[0]
    half = k // 2
    tm = _row_tile(n, 512)

    def body(dh_ref, w_ref, a_ref, b_ref):
        dm = _dot_nt(dh_ref[...].astype(BF16), w_ref[...])
        a_ref[...] = dm[:, :half].astype(a_ref.dtype)
        b_ref[...] = dm[:, half:].astype(b_ref.dtype)

    return pl.pallas_call(
        body, name=name, grid=(n // tm,),
        in_specs=[pl.BlockSpec((tm, D_MODEL), lambda i: (i, 0)),
                  pl.BlockSpec((k, D_MODEL), lambda i: (0, 0))],
        out_specs=[pl.BlockSpec((tm, half), lambda i: (i, 0))] * 2,
        out_shape=[jax.ShapeDtypeStruct((n, half), dtypes[0]), jax.ShapeDtypeStruct((n, half), dtypes[1])],
        compiler_params=_params("parallel"),
    )(dh, w)


def proj_rms_bwd(dys, ws, h_in, gain, dres, nk, name):
    n = h_in.shape[0]
    npair = len(dys)
    tm = _row_tile(n, 256)
    tks = [dy.shape[1] // nk for dy in dys]

    def body(*refs):
        dy_refs = refs[:npair]
        w_refs = refs[npair:2 * npair]
        h_ref, g_ref, dr_ref, o_ref, dg_ref, acc_ref = refs[2 * npair:]
        i, k = pl.program_id(0), pl.program_id(1)
        part = _dot_nt(dy_refs[0][...], w_refs[0][...])
        for p in range(1, npair):
            part = part + _dot_nt(dy_refs[p][...], w_refs[p][...])

        @pl.when(k == 0)
        def _():
            acc_ref[...] = part

        @pl.when(k > 0)
        def _():
            acc_ref[...] += part

        @pl.when(k == nk - 1)
        def _():
            dhn = acc_ref[...]
            xv = h_ref[...]
            r = lax.rsqrt(jnp.mean(xv * xv, axis=-1, keepdims=True) + EPS)
            xh = xv * r
            uv = dhn * g_ref[...]
            o_ref[...] = dr_ref[...] + r * (uv - xh * jnp.mean(uv * xh, axis=-1, keepdims=True))
            dgp = jnp.broadcast_to(jnp.sum(dhn * xh, axis=0, keepdims=True), dg_ref.shape)

            @pl.when(i == 0)
            def _():
                dg_ref[...] = dgp

            @pl.when(i > 0)
            def _():
                dg_ref[...] += dgp

    row = pl.BlockSpec((tm, D_MODEL), lambda i, k: (i, 0))
    in_specs = [pl.BlockSpec((tm, tk), lambda i, k: (i, k)) for tk in tks]
    in_specs += [pl.BlockSpec((D_MODEL, tk), lambda i, k: (0, k)) for tk in tks]
    in_specs += [row, pl.BlockSpec((1, D_MODEL), lambda i, k: (0, 0)), row]
    dh, dgain = pl.pallas_call(
        body, name=name, grid=(n // tm, nk),
        in_specs=in_specs,
        out_specs=[row, pl.BlockSpec((8, D_MODEL), lambda i, k: (0, 0))],
        out_shape=[jax.ShapeDtypeStruct((n, D_MODEL), F32), jax.ShapeDtypeStruct((8, D_MODEL), F32)],
        scratch_shapes=[pltpu.VMEM((tm, D_MODEL), F32)],
        compiler_params=_params("arbitrary", "arbitrary"),
    )(*dys, *ws, h_in, gain, dres)
    return dh, dgain[0]


def loss_head(h, gain, target, name):
    n = h.shape[0]
    tm = _row_tile(n, 512)

    def body(h_ref, g_ref, t_ref, dh_ref, dg_ref, l_ref):
        i = pl.program_id(0)
        xv = h_ref[...]
        r = lax.rsqrt(jnp.mean(xv * xv, axis=-1, keepdims=True) + EPS)
        xh = xv * r
        err = xh * g_ref[...] - t_ref[...]
        dy = err * (1.0 / D_MODEL)
        uv = dy * g_ref[...]
        dh_ref[...] = r * (uv - xh * jnp.mean(uv * xh, axis=-1, keepdims=True))
        dgp = jnp.broadcast_to(jnp.sum(dy * xh, axis=0, keepdims=True), dg_ref.shape)
        lp = jnp.sum(jnp.sum(err * err, axis=-1, keepdims=True), axis=0, keepdims=True) * (0.5 / D_MODEL)
        lp = jnp.broadcast_to(lp, l_ref.shape)

        @pl.when(i == 0)
        def _():
            dg_ref[...] = dgp
            l_ref[...] = lp

        @pl.when(i > 0)
        def _():
            dg_ref[...] += dgp
            l_ref[...] += lp

    row = pl.BlockSpec((tm, D_MODEL), lambda i: (i, 0))
    dh, dg, l = pl.pallas_call(
        body, name=name, grid=(n // tm,),
        in_specs=[row, pl.BlockSpec((1, D_MODEL), lambda i: (0, 0)), row],
        out_specs=[row, pl.BlockSpec((8, D_MODEL), lambda i: (0, 0)), pl.BlockSpec((8, LANES), lambda i: (0, 0))],
        out_shape=[jax.ShapeDtypeStruct((n, D_MODEL), F32), jax.ShapeDtypeStruct((8, D_MODEL), F32),
                   jax.ShapeDtypeStruct((8, LANES), F32)],
        compiler_params=_params("arbitrary"),
    )(h, gain, target)
    return dh, dg[0], l[0, 0]


def _attn_mask(n):
    qi = lax.broadcasted_iota(jnp.int32, (ATTN_BLOCK, 2 * ATTN_BLOCK), 0)
    r = lax.broadcasted_iota(jnp.int32, (ATTN_BLOCK, 2 * ATTN_BLOCK), 1)
    band = (r > qi) & (r <= qi + ATTN_BLOCK)
    return band & ((r >= ATTN_BLOCK) | (n > 0))


def _attn_probs(q, k, mask, sink):
    s = _dot_nt(q, k) * (HEAD_DIM ** -0.5)
    s = jnp.where(mask, s, -1e30)
    m = jnp.maximum(jnp.max(s, axis=-1, keepdims=True), sink)
    p = jnp.exp(s - m)
    esink = jnp.exp(sink - m)
    inv = 1.0 / (jnp.sum(p, axis=-1, keepdims=True) + esink)
    return p * inv, esink * inv


def attn_fwd(q, kv, sinks, nseq, seq, name):
    nb = seq // ATTN_BLOCK

    def body(q_ref, kv_ref, s_ref, o_ref, kvp):
        kvp[0:ATTN_BLOCK, :] = jnp.zeros((ATTN_BLOCK, 2 * KV_WIDTH), BF16)
        kvp[ATTN_BLOCK:, :] = kv_ref[...]

        def blk(n, carry):
            st = pl.multiple_of(n * ATTN_BLOCK, ATTN_BLOCK)
            qb = q_ref[pl.ds(st, ATTN_BLOCK), :]
            kw = kvp[pl.ds(st, 2 * ATTN_BLOCK), :]
            mask = _attn_mask(n)
            for kh in range(N_KV_HEADS):
                kk = kw[:, kh * HEAD_DIM:(kh + 1) * HEAD_DIM]
                vv = kw[:, KV_WIDTH + kh * HEAD_DIM:KV_WIDTH + (kh + 1) * HEAD_DIM]
                for g in range(N_Q_HEADS // N_KV_HEADS):
                    h = kh * (N_Q_HEADS // N_KV_HEADS) + g
                    probs, _ = _attn_probs(qb[:, h * HEAD_DIM:(h + 1) * HEAD_DIM], kk, mask, s_ref[h:h + 1, 0:1])
                    o = _dot(probs.astype(BF16), vv)
                    o_ref[pl.ds(st, ATTN_BLOCK), h * HEAD_DIM:(h + 1) * HEAD_DIM] = o.astype(o_ref.dtype)
            return carry

        lax.fori_loop(0, nb, blk, 0)

    return pl.pallas_call(
        body, name=name, grid=(nseq,),
        in_specs=[pl.BlockSpec((seq, ATTN_WIDTH), lambda b: (b, 0)),
                  pl.BlockSpec((seq, 2 * KV_WIDTH), lambda b: (b, 0)),
                  pl.BlockSpec((N_Q_HEADS, LANES), lambda b: (0, 0))],
        out_specs=pl.BlockSpec((seq, ATTN_WIDTH), lambda b: (b, 0)),
        out_shape=jax.ShapeDtypeStruct((nseq * seq, ATTN_WIDTH), BF16),
        scratch_shapes=[pltpu.VMEM((ATTN_BLOCK + seq, 2 * KV_WIDTH), BF16)],
        compiler_params=_params("parallel"),
    )(q, kv, sinks)


def attn_bwd(q, kv, sinks, do, nseq, seq, name):
    nb = seq // ATTN_BLOCK
    grp = N_Q_HEADS // N_KV_HEADS

    def body(q_ref, kv_ref, s_ref, do_ref, dq_ref, dkv_ref, ds_ref, kvp, dkvp):
        @pl.when(pl.program_id(0) == 0)
        def _():
            ds_ref[...] = jnp.zeros(ds_ref.shape, F32)

        kvp[0:ATTN_BLOCK, :] = jnp.zeros((ATTN_BLOCK, 2 * KV_WIDTH), BF16)
        kvp[ATTN_BLOCK:, :] = kv_ref[...]
        dkvp[...] = jnp.zeros(dkvp.shape, F32)

        def blk(n, carry):
            st = pl.multiple_of(n * ATTN_BLOCK, ATTN_BLOCK)
            qb = q_ref[pl.ds(st, ATTN_BLOCK), :]
            dob = do_ref[pl.ds(st, ATTN_BLOCK), :]
            kw = kvp[pl.ds(st, 2 * ATTN_BLOCK), :]
            mask = _attn_mask(n)
            for kh in range(N_KV_HEADS):
                kk = kw[:, kh * HEAD_DIM:(kh + 1) * HEAD_DIM]
                vv = kw[:, KV_WIDTH + kh * HEAD_DIM:KV_WIDTH + (kh + 1) * HEAD_DIM]
                dk = jnp.zeros((2 * ATTN_BLOCK, HEAD_DIM), F32)
                dv = jnp.zeros((2 * ATTN_BLOCK, HEAD_DIM), F32)
                for g in range(grp):
                    h = kh * grp + g
                    qh = qb[:, h * HEAD_DIM:(h + 1) * HEAD_DIM]
                    doh = dob[:, h * HEAD_DIM:(h + 1) * HEAD_DIM]
                    probs, psink = _attn_probs(qh, kk, mask, s_ref[h:h + 1, 0:1])
                    dp = _dot_nt(doh, vv)
                    dv = dv + _dot_tn(probs.astype(BF16), doh)
                    rowdot = jnp.sum(probs * dp, axis=-1, keepdims=True)
                    dsc = (probs * (dp - rowdot) * (HEAD_DIM ** -0.5)).astype(BF16)
                    dsink = jnp.sum(-psink * rowdot, axis=0, keepdims=True)
                    ds_ref[h:h + 1, :] += jnp.broadcast_to(dsink, (1, LANES))
                    dq_ref[pl.ds(st, ATTN_BLOCK), h * HEAD_DIM:(h + 1) * HEAD_DIM] = _dot(dsc, kk).astype(dq_ref.dtype)
                    dk = dk + _dot_tn(dsc, qh)
                dkvp[pl.ds(st, 2 * ATTN_BLOCK), kh * HEAD_DIM:(kh + 1) * HEAD_DIM] += dk
                dkvp[pl.ds(st, 2 * ATTN_BLOCK), KV_WIDTH + kh * HEAD_DIM:KV_WIDTH + (kh + 1) * HEAD_DIM] += dv
            return carry

        lax.fori_loop(0, nb, blk, 0)
        dkv_ref[...] = dkvp[ATTN_BLOCK:, :].astype(dkv_ref.dtype)

    seq_q = pl.BlockSpec((seq, ATTN_WIDTH), lambda b: (b, 0))
    seq_kv = pl.BlockSpec((seq, 2 * KV_WIDTH), lambda b: (b, 0))
    sink_spec = pl.BlockSpec((N_Q_HEADS, LANES), lambda b: (0, 0))
    return pl.pallas_call(
        body, name=name, grid=(nseq,),
        in_specs=[seq_q, seq_kv, sink_spec, seq_q],
        out_specs=[seq_q, seq_kv, sink_spec],
        out_shape=[jax.ShapeDtypeStruct((nseq * seq, ATTN_WIDTH), BF16),
                   jax.ShapeDtypeStruct((nseq * seq, 2 * KV_WIDTH), BF16),
                   jax.ShapeDtypeStruct((N_Q_HEADS, LANES), F32)],
        scratch_shapes=[pltpu.VMEM((ATTN_BLOCK + seq, 2 * KV_WIDTH), BF16),
                        pltpu.VMEM((ATTN_BLOCK + seq, 2 * KV_WIDTH), F32)],
        compiler_params=_params("arbitrary"),
    )(q, kv, sinks, do)


CONV_T = 128


def _conv_taps(win, w_ref, lanes, init):
    acc = init
    for j in range(CONV_KERNEL):
        sh = win if j == CONV_KERNEL - 1 else pltpu.roll(win, CONV_KERNEL - 1 - j, 0)
        acc = acc + w_ref[j:j + 1, lanes] * sh[CONV_HALO:CONV_HALO + CONV_T]
    return acc


def _conv_block(h0p, w_ref, vec_ref, st):
    cols = []
    for cs in range(CONV_WIDTH // LANES):
        lanes = slice(cs * LANES, (cs + 1) * LANES)
        win = h0p[pl.ds(st, CONV_T + CONV_HALO), lanes]
        init = jnp.broadcast_to(vec_ref[0:1, lanes], (CONV_T, LANES))
        cols.append(_conv_taps(win, w_ref, lanes, init))
    return jnp.concatenate(cols, axis=-1)


def _glu_store(c_ref, h0p, st):
    cb = c_ref[pl.ds(st, CONV_T), :]
    h0p[pl.ds(pl.multiple_of(st + CONV_HALO, CONV_HALO), CONV_T), :] = cb[:, :CONV_WIDTH] * _sigmoid(cb[:, CONV_WIDTH:])


def conv_fwd(c, w, vec, nseq, seq, name):
    nb = seq // CONV_T

    def body(c_ref, w_ref, vec_ref, o_ref, h0p):
        h0p[0:CONV_HALO, :] = jnp.zeros((CONV_HALO, CONV_WIDTH), F32)

        def blk(n, carry):
            st = pl.multiple_of(n * CONV_T, CONV_T)
            _glu_store(c_ref, h0p, st)
            h1 = _conv_block(h0p, w_ref, vec_ref, st)
            mu = jnp.mean(h1, axis=-1, keepdims=True)
            xc = h1 - mu
            rstd = lax.rsqrt(jnp.mean(xc * xc, axis=-1, keepdims=True) + EPS)
            y = xc * rstd * vec_ref[1:2, :] + vec_ref[2:3, :]
            o_ref[pl.ds(st, CONV_T), :] = (y * _sigmoid(y)).astype(o_ref.dtype)
            return carry

        lax.fori_loop(0, nb, blk, 0)

    return pl.pallas_call(
        body, name=name, grid=(nseq,),
        in_specs=[pl.BlockSpec((seq, 2 * CONV_WIDTH), lambda b: (b, 0)),
                  pl.BlockSpec((32, CONV_WIDTH), lambda b: (0, 0)),
                  pl.BlockSpec((8, CONV_WIDTH), lambda b: (0, 0))],
        out_specs=pl.BlockSpec((seq, CONV_WIDTH), lambda b: (b, 0)),
        out_shape=jax.ShapeDtypeStruct((nseq * seq, CONV_WIDTH), BF16),
        scratch_shapes=[pltpu.VMEM((CONV_HALO + seq, CONV_WIDTH), F32)],
        compiler_params=_params("parallel"),
    )(c, w, vec)


def conv_bwd(c, w, vec, dout, nseq, seq, name):
    nb = seq // CONV_T

    def body(c_ref, w_ref, vec_ref, do_ref, dc_ref, dw_ref, dvec_ref, h0p, dh1p):
        @pl.when(pl.program_id(0) == 0)
        def _():
            dw_ref[...] = jnp.zeros(dw_ref.shape, F32)
            dvec_ref[...] = jnp.zeros(dvec_ref.shape, F32)

        h0p[0:CONV_HALO, :] = jnp.zeros((CONV_HALO, CONV_WIDTH), F32)
        dh1p[seq:seq + CONV_HALO, :] = jnp.zeros((CONV_HALO, CONV_WIDTH), F32)

        def pass_a(n, carry):
            st = pl.multiple_of(n * CONV_T, CONV_T)
            _glu_store(c_ref, h0p, st)
            h1 = _conv_block(h0p, w_ref, vec_ref, st)
            mu = jnp.mean(h1, axis=-1, keepdims=True)
            xc = h1 - mu
            rstd = lax.rsqrt(jnp.mean(xc * xc, axis=-1, keepdims=True) + EPS)
            xh = xc * rstd
            y = xh * vec_ref[1:2, :] + vec_ref[2:3, :]
            sg = _sigmoid(y)
            dy = do_ref[pl.ds(st, CONV_T), :] * (sg * (1.0 + y * (1.0 - sg)))
            dvec_ref[1:2, :] += jnp.sum(dy * xh, axis=0, keepdims=True)
            dvec_ref[2:3, :] += jnp.sum(dy, axis=0, keepdims=True)
            dxh = dy * vec_ref[1:2, :]
            dh1 = rstd * (dxh - jnp.mean(dxh, axis=-1, keepdims=True)
                          - xh * jnp.mean(dxh * xh, axis=-1, keepdims=True))
            dvec_ref[0:1, :] += jnp.sum(dh1, axis=0, keepdims=True)
            dh1p[pl.ds(st, CONV_T), :] = dh1
            return carry

        lax.fori_loop(0, nb, pass_a, 0)

        def pass_b(n, carry):
            st = pl.multiple_of(n * CONV_T, CONV_T)
            cols = []
            for cs in range(CONV_WIDTH // LANES):
                lanes = slice(cs * LANES, (cs + 1) * LANES)
                wind = dh1p[pl.ds(st, CONV_T + CONV_HALO), lanes]
                winh = h0p[pl.ds(st, CONV_T + CONV_HALO), lanes]
                d1 = wind[0:CONV_T]
                acc = jnp.zeros((CONV_T, LANES), F32)
                for j in range(CONV_KERNEL):
                    acc = acc + w_ref[j:j + 1, lanes] * pltpu.roll(wind, 2 + j, 0)[CONV_HALO:CONV_HALO + CONV_T]
                    hs = winh if j == CONV_KERNEL - 1 else pltpu.roll(winh, CONV_KERNEL - 1 - j, 0)
                    dw_ref[j:j + 1, lanes] += jnp.sum(d1 * hs[CONV_HALO:CONV_HALO + CONV_T], axis=0, keepdims=True)
                cols.append(acc)
            dh0 = jnp.concatenate(cols, axis=-1)
            cb = c_ref[pl.ds(st, CONV_T), :]
            av, gt = cb[:, :CONV_WIDTH], cb[:, CONV_WIDTH:]
            sg = _sigmoid(gt)
            dc_ref[pl.ds(st, CONV_T), :] = jnp.concatenate(
                [dh0 * sg, dh0 * av * sg * (1.0 - sg)], axis=-1).astype(dc_ref.dtype)
            return carry

        lax.fori_loop(0, nb, pass_b, 0)

    return pl.pallas_call(
        body, name=name, grid=(nseq,),
        in_specs=[pl.BlockSpec((seq, 2 * CONV_WIDTH), lambda b: (b, 0)),
                  pl.BlockSpec((32, CONV_WIDTH), lambda b: (0, 0)),
                  pl.BlockSpec((8, CONV_WIDTH), lambda b: (0, 0)),
                  pl.BlockSpec((seq, CONV_WIDTH), lambda b: (b, 0))],
        out_specs=[pl.BlockSpec((seq, 2 * CONV_WIDTH), lambda b: (b, 0)),
                   pl.BlockSpec((32, CONV_WIDTH), lambda b: (0, 0)),
                   pl.BlockSpec((8, CONV_WIDTH), lambda b: (0, 0))],
        out_shape=[jax.ShapeDtypeStruct((nseq * seq, 2 * CONV_WIDTH), BF16),
                   jax.ShapeDtypeStruct((32, CONV_WIDTH), F32),
                   jax.ShapeDtypeStruct((8, CONV_WIDTH), F32)],
        scratch_shapes=[pltpu.VMEM((CONV_HALO + seq, CONV_WIDTH), F32),
                        pltpu.VMEM((seq + CONV_HALO, CONV_WIDTH), F32)],
        compiler_params=_params("arbitrary"),
    )(c, w, vec, dout)


POOL_T = 128


def _pooled_block(zpp, st, grp):
    lanes = slice(grp * LANES, (grp + 1) * LANES)
    win = zpp[pl.ds(st, POOL_T + POOL_HALO), lanes]
    acc = win
    for lvl in range(grp + 1):
        acc = acc + pltpu.roll(acc, 1 << lvl, 0)
    t = st + lax.broadcasted_iota(jnp.int32, (POOL_T, 1), 0)
    inv = 1.0 / jnp.minimum(t + 1, POOL_WINDOWS[grp]).astype(F32)
    return acc[POOL_HALO:] * inv - win[POOL_HALO:], inv


def pool_fwd(zp, wp, scale, nseq, seq, name):
    nb = seq // POOL_T

    def body(z_ref, wp_ref, sc_ref, o_ref, zpp):
        zpp[0:POOL_HALO, :] = jnp.zeros((POOL_HALO, POOL_WIDTH), F32)
        zpp[POOL_HALO:, :] = z_ref[...]

        def blk(n, carry):
            st = pl.multiple_of(n * POOL_T, POOL_T)
            for grp in range(len(POOL_WINDOWS)):
                lanes = slice(grp * LANES, (grp + 1) * LANES)
                pooled, _ = _pooled_block(zpp, st, grp)
                o_ref[pl.ds(st, POOL_T), lanes] = (
                    _dot(pooled.astype(BF16), wp_ref[grp]) * sc_ref[0:1, lanes]).astype(o_ref.dtype)
            return carry

        lax.fori_loop(0, nb, blk, 0)

    return pl.pallas_call(
        body, name=name, grid=(nseq,),
        in_specs=[pl.BlockSpec((seq, POOL_WIDTH), lambda b: (b, 0)),
                  pl.BlockSpec((4, LANES, LANES), lambda b: (0, 0, 0)),
                  pl.BlockSpec((1, POOL_WIDTH), lambda b: (0, 0))],
        out_specs=pl.BlockSpec((seq, POOL_WIDTH), lambda b: (b, 0)),
        out_shape=jax.ShapeDtypeStruct((nseq * seq, POOL_WIDTH), BF16),
        scratch_shapes=[pltpu.VMEM((POOL_HALO + seq, POOL_WIDTH), F32)],
        compiler_params=_params("parallel"),
    )(zp, wp, scale)


def pool_bwd(zp, wp, scale, dout, nseq, seq, name):
    nb = seq // POOL_T

    def body(z_ref, wp_ref, sc_ref, do_ref, dz_ref, dwp_ref, dsc_ref, zpp, dpcp, negd):
        @pl.when(pl.program_id(0) == 0)
        def _():
            dwp_ref[...] = jnp.zeros(dwp_ref.shape, F32)
            dsc_ref[...] = jnp.zeros(dsc_ref.shape, F32)

        zpp[0:POOL_HALO, :] = jnp.zeros((POOL_HALO, POOL_WIDTH), F32)
        zpp[POOL_HALO:, :] = z_ref[...]
        dpcp[seq:seq + POOL_HALO, :] = jnp.zeros((POOL_HALO, POOL_WIDTH), F32)

        def pass_a(n, carry):
            st = pl.multiple_of(n * POOL_T, POOL_T)
            for grp in range(len(POOL_WINDOWS)):
                lanes = slice(grp * LANES, (grp + 1) * LANES)
                pooled, inv = _pooled_block(zpp, st, grp)
                pb = pooled.astype(BF16)
                dob = do_ref[pl.ds(st, POOL_T), lanes]
                dsc_ref[0:1, lanes] += jnp.sum(dob * _dot(pb, wp_ref[grp]), axis=0, keepdims=True)
                dpm = (dob * sc_ref[0:1, lanes]).astype(BF16)
                dwp_ref[grp] += _dot_tn(pb, dpm)
                dpooled = _dot_nt(dpm, wp_ref[grp])
                negd[pl.ds(st, POOL_T), lanes] = -dpooled
                dpcp[pl.ds(st, POOL_T), lanes] = dpooled * inv
            return carry

        lax.fori_loop(0, nb, pass_a, 0)

        def pass_b(n, carry):
            st = pl.multiple_of(n * POOL_T, POOL_T)
            rows = POOL_T + POOL_HALO
            for grp in range(len(POOL_WINDOWS)):
                lanes = slice(grp * LANES, (grp + 1) * LANES)
                acc = dpcp[pl.ds(st, rows), lanes]
                for lvl in range(grp + 1):
                    acc = acc + pltpu.roll(acc, rows - (1 << lvl), 0)
                dz_ref[pl.ds(st, POOL_T), lanes] = (acc[0:POOL_T] + negd[pl.ds(st, POOL_T), lanes]).astype(dz_ref.dtype)
            return carry

        lax.fori_loop(0, nb, pass_b, 0)

    seq_spec = pl.BlockSpec((seq, POOL_WIDTH), lambda b: (b, 0))
    return pl.pallas_call(
        body, name=name, grid=(nseq,),
        in_specs=[seq_spec, pl.BlockSpec((4, LANES, LANES), lambda b: (0, 0, 0)),
                  pl.BlockSpec((1, POOL_WIDTH), lambda b: (0, 0)), seq_spec],
        out_specs=[seq_spec, pl.BlockSpec((4, LANES, LANES), lambda b: (0, 0, 0)),
                   pl.BlockSpec((8, POOL_WIDTH), lambda b: (0, 0))],
        out_shape=[jax.ShapeDtypeStruct((nseq * seq, POOL_WIDTH), BF16),
                   jax.ShapeDtypeStruct((4, LANES, LANES), F32),
                   jax.ShapeDtypeStruct((8, POOL_WIDTH), F32)],
        scratch_shapes=[pltpu.VMEM((POOL_HALO + seq, POOL_WIDTH), F32),
                        pltpu.VMEM((seq + POOL_HALO, POOL_WIDTH), F32),
                        pltpu.VMEM((seq, POOL_WIDTH), F32)],
        compiler_params=_params("arbitrary"),
    )(zp, wp, scale, dout)


GELU_C0 = 0.7978845608028654
GELU_C1 = 0.044715


def _gelu(xv):
    return xv * (0.5 * (1.0 + jnp.tanh(GELU_C0 * (xv + GELU_C1 * (xv * xv * xv)))))


def _gelu_grad(xv):
    t = jnp.tanh(GELU_C0 * (xv + GELU_C1 * (xv * xv * xv)))
    return 0.5 * (1.0 + t) + 0.5 * xv * (1.0 - t * t) * (GELU_C0 * (1.0 + 3.0 * GELU_C1 * xv * xv))


def _tril():
    ti = lax.broadcasted_iota(jnp.int32, (SGU_CHUNK, SGU_CHUNK), 0)
    si = lax.broadcasted_iota(jnp.int32, (SGU_CHUNK, SGU_CHUNK), 1)
    return si <= ti


def sgu_fwd(zs, ws, bst, ln, nseq, seq, name):
    nc = seq // SGU_CHUNK

    def body(z_ref, ws_ref, bs_ref, ln_ref, o_ref):
        tril = _tril()

        def blk(n, carry):
            st = pl.multiple_of(n * SGU_CHUNK, SGU_CHUNK)
            ge = _gelu(z_ref[pl.ds(st, SGU_CHUNK), :])
            uu, vv = ge[:, :SGU_WIDTH], ge[:, SGU_WIDTH:]
            mu = jnp.mean(vv, axis=-1, keepdims=True)
            xc = vv - mu
            rstd = lax.rsqrt(jnp.mean(xc * xc, axis=-1, keepdims=True) + EPS)
            vn = (xc * rstd * ln_ref[0:1, :] + ln_ref[1:2, :]).astype(BF16)
            for g in range(4):
                lanes = slice(g * LANES, (g + 1) * LANES)
                wm = jnp.where(tril, ws_ref[g], 0.0).astype(BF16)
                mixed = _dot(wm, vn[:, lanes]) + bs_ref[:, g:g + 1]
                o_ref[pl.ds(st, SGU_CHUNK), lanes] = (uu[:, lanes] * mixed).astype(o_ref.dtype)
            return carry

        lax.fori_loop(0, nc, blk, 0)

    return pl.pallas_call(
        body, name=name, grid=(nseq,),
        in_specs=[pl.BlockSpec((seq, 2 * SGU_WIDTH), lambda b: (b, 0)),
                  pl.BlockSpec((4, LANES, LANES), lambda b: (0, 0, 0)),
                  pl.BlockSpec((SGU_CHUNK, 4), lambda b: (0, 0)),
                  pl.BlockSpec((8, SGU_WIDTH), lambda b: (0, 0))],
        out_specs=pl.BlockSpec((seq, SGU_WIDTH), lambda b: (b, 0)),
        out_shape=jax.ShapeDtypeStruct((nseq * seq, SGU_WIDTH), BF16),
        compiler_params=_params("parallel"),
    )(zs, ws, bst, ln)


def sgu_bwd(zs, ws, bst, ln, dout, nseq, seq, name):
    nc = seq // SGU_CHUNK

    def body(z_ref, ws_ref, bs_ref, ln_ref, do_ref, dz_ref, dws_ref, dbs_ref, dln_ref):
        @pl.when(pl.program_id(0) == 0)
        def _():
            dws_ref[...] = jnp.zeros(dws_ref.shape, F32)
            dbs_ref[...] = jnp.zeros(dbs_ref.shape, F32)
            dln_ref[...] = jnp.zeros(dln_ref.shape, F32)

        tril = _tril()

        def blk(n, carry):
            st = pl.multiple_of(n * SGU_CHUNK, SGU_CHUNK)
            zv = z_ref[pl.ds(st, SGU_CHUNK), :]
            ge = _gelu(zv)
            uu, vv = ge[:, :SGU_WIDTH], ge[:, SGU_WIDTH:]
            mu = jnp.mean(vv, axis=-1, keepdims=True)
            xc = vv - mu
            rstd = lax.rsqrt(jnp.mean(xc * xc, axis=-1, keepdims=True) + EPS)
            xh = xc * rstd
            vn = (xh * ln_ref[0:1, :] + ln_ref[1:2, :]).astype(BF16)
            dob = do_ref[pl.ds(st, SGU_CHUNK), :]
            du_cols, dvn_cols = [], []
            for g in range(4):
                lanes = slice(g * LANES, (g + 1) * LANES)
                wm = jnp.where(tril, ws_ref[g], 0.0).astype(BF16)
                mixed = _dot(wm, vn[:, lanes]) + bs_ref[:, g:g + 1]
                du_cols.append(dob[:, lanes] * mixed)
                dmix = dob[:, lanes] * uu[:, lanes]
                dbs_ref[g] += jnp.broadcast_to(jnp.sum(dmix, axis=-1, keepdims=True), (SGU_CHUNK, LANES))
                dmb = dmix.astype(BF16)
                dws_ref[g] += jnp.where(tril, _dot_nt(dmb, vn[:, lanes]), 0.0)
                dvn_cols.append(_dot_tn(wm, dmb))
            dvn = jnp.concatenate(dvn_cols, axis=-1)
            dln_ref[0:1, :] += jnp.sum(dvn * xh, axis=0, keepdims=True)
            dln_ref[1:2, :] += jnp.sum(dvn, axis=0, keepdims=True)
            dxh = dvn * ln_ref[0:1, :]
            dv = rstd * (dxh - jnp.mean(dxh, axis=-1, keepdims=True)
                         - xh * jnp.mean(dxh * xh, axis=-1, keepdims=True))
            dge = jnp.concatenate(du_cols + [dv], axis=-1)
            dz_ref[pl.ds(st, SGU_CHUNK), :] = (dge * _gelu_grad(zv)).astype(dz_ref.dtype)
            return carry

        lax.fori_loop(0, nc, blk, 0)

    w_spec = pl.BlockSpec((4, LANES, LANES), lambda b: (0, 0, 0))
    ln_spec = pl.BlockSpec((8, SGU_WIDTH), lambda b: (0, 0))
    return pl.pallas_call(
        body, name=name, grid=(nseq,),
        in_specs=[pl.BlockSpec((seq, 2 * SGU_WIDTH), lambda b: (b, 0)), w_spec,
                  pl.BlockSpec((SGU_CHUNK, 4), lambda b: (0, 0)), ln_spec,
                  pl.BlockSpec((seq, SGU_WIDTH), lambda b: (b, 0))],
        out_specs=[pl.BlockSpec((seq, 2 * SGU_WIDTH), lambda b: (b, 0)), w_spec, w_spec, ln_spec],
        out_shape=[jax.ShapeDtypeStruct((nseq * seq, 2 * SGU_WIDTH), BF16),
                   jax.ShapeDtypeStruct((4, LANES, LANES), F32),
                   jax.ShapeDtypeStruct((4, LANES, LANES), F32),
                   jax.ShapeDtypeStruct((8, SGU_WIDTH), F32)],
        compiler_params=_params("arbitrary"),
    )(zs, ws, bst, ln, dout)


def _ew_rows(rows, cols, nbuf):
    t = _row_tile(rows, 1024)
    while t > 8 and t * cols * 4 * nbuf * 2 > 24 * 2**20:
        t //= 2
    return t


def adamw(w, g, m, v, name):
    rows, cols = w.shape
    tr = _ew_rows(rows, cols, 7)

    def body(w_ref, g_ref, m_ref, v_ref, d_ref, mo_ref, vo_ref):
        gv = g_ref[...]
        mn = ADAM_B1 * m_ref[...] + (1.0 - ADAM_B1) * gv
        vn = ADAM_B2 * v_ref[...] + (1.0 - ADAM_B2) * (gv * gv)
        m_hat = mn / (1.0 - ADAM_B1 ** ADAM_STEP)
        v_hat = vn / (1.0 - ADAM_B2 ** ADAM_STEP)
        d_ref[...] = -ADAM_LR * (m_hat / (jnp.sqrt(v_hat) + ADAM_EPS) + ADAM_WD * w_ref[...])
        mo_ref[...] = mn
        vo_ref[...] = vn

    spec = pl.BlockSpec((tr, cols), lambda i: (i, 0))
    return pl.pallas_call(
        body, name=name, grid=(rows // tr,),
        in_specs=[spec] * 4, out_specs=[spec] * 3,
        out_shape=[jax.ShapeDtypeStruct((rows, cols), F32)] * 3,
        compiler_params=_params("parallel"),
    )(w, g, m, v)


def add_cast(a, b, name):
    nchip, rows, cols = a.shape
    tr = _ew_rows(rows, cols, 3)

    def body(a_ref, b_ref, o_ref):
        o_ref[...] = (a_ref[...] + b_ref[...]).astype(BF16)

    spec = pl.BlockSpec((1, tr, cols), lambda k, i: (k, i, 0))
    return pl.pallas_call(
        body, name=name, grid=(nchip, rows // tr),
        in_specs=[spec, spec], out_specs=spec,
        out_shape=jax.ShapeDtypeStruct(a.shape, BF16),
        compiler_params=_params("parallel", "parallel"),
    )(a, b)


def sum_parts(parts, name):
    npart, rows, cols = parts.shape
    tr = _ew_rows(rows, cols, npart + 1)

    def body(p_ref, o_ref):
        acc = p_ref[0].astype(F32)
        for j in range(1, npart):
            acc = acc + p_ref[j].astype(F32)
        o_ref[...] = acc

    return pl.pallas_call(
        body, name=name, grid=(rows // tr,),
        in_specs=[pl.BlockSpec((npart, tr, cols), lambda i: (0, i, 0))],
        out_specs=pl.BlockSpec((tr, cols), lambda i: (i, 0)),
        out_shape=jax.ShapeDtypeStruct((rows, cols), F32),
        compiler_params=_params("parallel"),
    )(parts)


ANY = pl.BlockSpec(memory_space=pl.ANY)
MESH = pl.DeviceIdType.MESH


def _me():
    return lax.axis_index("x"), lax.axis_index("y"), lax.axis_index("c")


def _flip(pos, rel):
    return tuple(1 - p if f else p for p, f in zip(pos, rel))


def _exchange(name, ins, out_shapes, plan):
    n_in, n_out = len(ins), len(out_shapes)

    def body(*refs):
        in_refs, out_refs = refs[:n_in], refs[n_in:n_in + n_out]
        send_sems, recv_sems, loc_sems = refs[n_in + n_out:]
        pos = _me()
        remote, local = plan(in_refs, out_refs, pos)
        loc = [pltpu.make_async_copy(s, d, loc_sems.at[i]) for i, (s, d) in enumerate(local)]
        for cp in loc:
            cp.start()
        rem = [pltpu.make_async_remote_copy(src_ref=s, dst_ref=d, send_sem=send_sems.at[i], recv_sem=recv_sems.at[i],
                                            device_id=_flip(pos, rel), device_id_type=MESH)
               for i, (rel, s, d) in enumerate(remote)]
        for cp in rem:
            cp.start()
        for cp in rem:
            cp.wait()
        for cp in loc:
            cp.wait()

    n_rem, n_loc = plan.counts
    return pl.pallas_call(
        body, name=name,
        in_specs=[ANY] * n_in, out_specs=[ANY] * n_out,
        out_shape=[jax.ShapeDtypeStruct(s, d) for s, d in out_shapes],
        scratch_shapes=[pltpu.SemaphoreType.DMA((n_rem,)), pltpu.SemaphoreType.DMA((n_rem,)),
                        pltpu.SemaphoreType.DMA((max(n_loc, 1),))],
    )(*ins)


SIBLING = (0, 0, 1)
OTHER_CHIPS = ((1, 0, 0), (0, 1, 0), (1, 1, 0))


def _chip_of(pos, rel=(0, 0, 0)):
    px, py, _ = _flip(pos, rel)
    return 2 * px + py


def allgather_blocks(shards, name):
    nt = len(shards)
    hs = [s.shape[0] // 2 for s in shards]

    def body(*refs):
        ins, outs = refs[:nt], refs[nt:2 * nt]
        send_sems, recv_sems, loc_sems = refs[2 * nt:]
        pos = _me()
        x, y, c = pos

        def block_id(rel):
            px, py, pc = _flip(pos, rel)
            return 4 * px + 2 * py + pc

        def copy(t, k, block_rel, to_rel, src=None):
            dst = outs[t].at[block_id(block_rel)]
            return pltpu.make_async_remote_copy(
                src_ref=dst if src is None else src, dst_ref=dst,
                send_sem=send_sems.at[t * 7 + k], recv_sem=recv_sems.at[t * 7 + k],
                device_id=_flip(pos, to_rel), device_id_type=MESH)

        own = [ins[t].at[pl.ds(c * hs[t], hs[t])] for t in range(nt)]
        mine = [pltpu.make_async_copy(own[t], outs[t].at[block_id((0, 0, 0))], loc_sems.at[t]) for t in range(nt)]
        for cp in mine:
            cp.start()
        first = []
        for t in range(nt):
            first.append(copy(t, 0, (0, 0, 0), SIBLING, src=own[t]))
            first += [copy(t, 1 + j, (0, 0, 0), rel, src=own[t]) for j, rel in enumerate(OTHER_CHIPS)]
        for cp in first:
            cp.start()
        passed = []
        for j, rel in enumerate(OTHER_CHIPS):
            for t in range(nt):
                copy(t, 1 + j, rel, (0, 0, 0)).wait_recv()
                fwd = copy(t, 4 + j, rel, SIBLING)
                fwd.start()
                passed.append(fwd)
        for t in range(nt):
            copy(t, 0, SIBLING, (0, 0, 0)).wait_recv()
            for j, rel in enumerate(OTHER_CHIPS):
                copy(t, 4 + j, (rel[0], rel[1], 1), (0, 0, 0)).wait_recv()
        for cp in first + passed:
            cp.wait_send()
        for cp in mine:
            cp.wait()

    return pl.pallas_call(
        body, name=name,
        in_specs=[ANY] * nt, out_specs=[ANY] * nt,
        out_shape=[jax.ShapeDtypeStruct((N_DEV, h, s.shape[1]), s.dtype) for h, s in zip(hs, shards)],
        scratch_shapes=[pltpu.SemaphoreType.DMA((7 * nt,)), pltpu.SemaphoreType.DMA((7 * nt,)),
                        pltpu.SemaphoreType.DMA((nt,))],
    )(*shards)


def chip_gather(block, name):
    def plan(ins, outs, pos):
        me = _chip_of(pos)
        remote = [(rel, ins[0], outs[0].at[me]) for rel in OTHER_CHIPS]
        return remote, [(ins[0], outs[0].at[me])]

    plan.counts = (3, 1)
    return _exchange(name, [block], [((N_CHIPS,) + block.shape, block.dtype)], plan)[0]


def sibling_swap(xs, name):
    def plan(ins, outs, pos):
        return [(SIBLING, i, o) for i, o in zip(ins, outs)], []

    plan.counts = (len(xs), 0)
    return _exchange(name, xs, [(v.shape, v.dtype) for v in xs], plan)


def chip_scatter(xs, name):
    def plan(ins, outs, pos):
        me = _chip_of(pos)
        remote, local = [], []
        for i, o in zip(ins, outs):
            remote += [(rel, i.at[_chip_of(pos, rel)], o.at[me]) for rel in OTHER_CHIPS]
            local.append((i.at[me], o.at[me]))
        return remote, local

    plan.counts = (3 * len(xs), len(xs))
    return _exchange(name, xs, [(v.shape, v.dtype) for v in xs], plan)


def sibling_join(halves, name):
    def plan(ins, outs, pos):
        c = pos[2]
        remote, local = [], []
        for i, o in zip(ins, outs):
            h = i.shape[0]
            remote.append((SIBLING, i, o.at[pl.ds(c * h, h)]))
            local.append((i, o.at[pl.ds(c * h, h)]))
        return remote, local

    plan.counts = (len(halves), len(halves))
    return _exchange(name, halves, [((2 * v.shape[0],) + v.shape[1:], v.dtype) for v in halves], plan)


def gather_all(block, name):
    rels = [(0, 0, 1), (1, 0, 0), (0, 1, 0), (1, 1, 0), (1, 0, 1), (0, 1, 1), (1, 1, 1)]

    def plan(ins, outs, pos):
        me = 4 * pos[0] + 2 * pos[1] + pos[2]
        return [(rel, ins[0], outs[0].at[me]) for rel in rels], [(ins[0], outs[0].at[me])]

    plan.counts = (7, 1)
    return _exchange(name, [block], [((N_DEV,) + block.shape, block.dtype)], plan)[0]


PACK_ROWS = 256


def _pack(arrs):
    parts, layout = [], []
    row = 0
    for a in arrs:
        flat = a.reshape(-1).astype(F32)
        size = flat.shape[0]
        rows = -(-size // (8 * LANES)) * 8
        flat = jnp.pad(flat, (0, rows * LANES - size))
        parts.append(flat.reshape(rows, LANES))
        layout.append((row, rows, size, a.shape))
        row += rows
    if row % PACK_ROWS:
        parts.append(jnp.zeros((PACK_ROWS - row % PACK_ROWS, LANES), F32))
    return jnp.concatenate(parts, axis=0), layout


def _unpack(packed, layout):
    return [packed[r0:r0 + rows].reshape(-1)[:size].reshape(shape) for r0, rows, size, shape in layout]


SMALL_REPL = ['mix_norm', 'a_b_in', 'a_sinks', 'a_conv_b', 'a_cln_g', 'a_cln_b', 'c_w_pool', 'c_w_s', 'c_b_s',
              'ffn_norm', 'final_norm']
SMALL_SHARD = ['a_conv_w', 'c_pool_scale', 'c_sln_g', 'c_sln_b']
BIG = ['a_w_in', 'a_w_out', 'c_w_in', 'c_w_out', 'ffn_w_gate', 'ffn_w_up', 'ffn_w_down']
BIG_ROW_SHARDED = {'a_w_out', 'c_w_out', 'ffn_w_down'}


def _shard_view(name, a):
    return a.reshape(-1, a.shape[-1])


def _from_shard_major(name, g8):
    _, h, cols = g8.shape
    g4 = g8.reshape(N_CHIPS, 2 * h, cols)
    layers = 2 if name.startswith('ffn') else 1
    g4 = g4.reshape(N_CHIPS, layers, (2 * h) // layers, cols)
    out = []
    for l in range(layers):
        blk = g4[:, l]
        if name in BIG_ROW_SHARDED:
            out.append(blk.reshape(-1, cols))
        else:
            out.append(jnp.transpose(blk, (1, 0, 2)).reshape(blk.shape[1], N_CHIPS * cols))
    return out


def _to_shard_major(name, fulls):
    per_layer = []
    for f in fulls:
        if name in BIG_ROW_SHARDED:
            per_layer.append(f.reshape(N_CHIPS, f.shape[0] // N_CHIPS, f.shape[1]))
        else:
            r, cfull = f.shape
            per_layer.append(jnp.transpose(f.reshape(r, N_CHIPS, cfull // N_CHIPS), (1, 0, 2)))
    t = jnp.stack(per_layer, axis=1)
    return t.reshape(N_CHIPS, -1, t.shape[-1])


def kernel(*args):
    a = dict(zip(IN_NAMES, args))
    bl, seq, _ = a['x'].shape
    n = bl * seq
    x = a['x'].reshape(n, D_MODEL)
    target = a['loss_target'].reshape(n, D_MODEL)
    xi, yi, ci = _me()
    chip = 2 * xi + yi

    gathered = allgather_blocks([_shard_view(k, a[k]).astype(BF16) for k in BIG], "gather_weights")
    full = {k: _from_shard_major(k, g) for k, g in zip(BIG, gathered)}
    a_w_in, a_w_out = full['a_w_in'][0], full['a_w_out'][0]
    c_w_in, c_w_out = full['c_w_in'][0], full['c_w_out'][0]

    small_shard_pack, small_shard_layout = _pack([a[k] for k in SMALL_SHARD])
    ss = chip_gather(small_shard_pack, "gather_small")
    ss_full = []
    for r0, rows, size, shape in small_shard_layout:
        per_chip = ss[:, r0:r0 + rows].reshape(N_CHIPS, -1)[:, :size].reshape((N_CHIPS,) + shape)
        ss_full.append(jnp.concatenate([per_chip[k] for k in range(N_CHIPS)], axis=-1))
    a_conv_w, c_pool_scale, c_sln_g, c_sln_b = [v[0] for v in ss_full]

    conv_taps = jnp.pad(a_conv_w, ((0, 32 - CONV_KERNEL), (0, 0)))
    conv_vec = jnp.pad(jnp.stack([a['a_conv_b'][0], a['a_cln_g'][0], a['a_cln_b'][0]]), ((0, 5), (0, 0)))
    sinks_b = jnp.broadcast_to(a['a_sinks'][0].reshape(N_Q_HEADS, 1), (N_Q_HEADS, LANES))
    w_pool_bf = a['c_w_pool'][0].astype(BF16)
    pool_scale = c_pool_scale.reshape(1, POOL_WIDTH)
    w_s = a['c_w_s'][0]
    b_s_t = a['c_b_s'][0].T
    sgu_ln = jnp.pad(jnp.stack([c_sln_g, c_sln_b]), ((0, 6), (0, 0)))
    mix_norm, ffn_norm = a['mix_norm'], a['ffn_norm']
    final_norm = a['final_norm'].reshape(1, D_MODEL)

    hn0, q, kv, cc = norm_inproj(
        x, mix_norm[0:1], a_w_in, a['a_b_in'],
        [(0, ATTN_WIDTH), (ATTN_WIDTH, ATTN_WIDTH + 2 * KV_WIDTH), (ATTN_WIDTH + 2 * KV_WIDTH, a_w_in.shape[1])],
        [BF16, BF16, F32], "in_proj0")
    attn = attn_fwd(q, kv, sinks_b, bl, seq, "attn_fwd")
    conv = conv_fwd(cc, conv_taps, conv_vec, bl, seq, "conv_fwd")
    h1 = out_proj(x, attn, conv, a_w_out, "out_proj0")
    hnf0, g0, u0 = ffn_gate_up(h1, ffn_norm[0:1], full['ffn_w_gate'][0], full['ffn_w_up'][0], "ffn_gate_up0")
    h2 = ffn_down(h1, g0, u0, full['ffn_w_down'][0], "ffn_down0")

    hn1, zp, zs = norm_inproj(
        h2, mix_norm[1:2], c_w_in, jnp.zeros((1, c_w_in.shape[1]), F32),
        [(0, POOL_WIDTH), (POOL_WIDTH, c_w_in.shape[1])], [F32, F32], "in_proj1")
    pool = pool_fwd(zp, w_pool_bf, pool_scale, bl, seq, "pool_fwd")
    sgu = sgu_fwd(zs, w_s, b_s_t, sgu_ln, bl, seq, "sgu_fwd")
    h3 = out_proj(h2, pool, sgu, c_w_out, "out_proj1")
    hnf1, g1, u1 = ffn_gate_up(h3, ffn_norm[1:2], full['ffn_w_gate'][1], full['ffn_w_up'][1], "ffn_gate_up1")
    h4 = ffn_down(h3, g1, u1, full['ffn_w_down'][1], "ffn_down1")

    dh4, d_final_norm, loss_local = loss_head(h4, final_norm, target, "loss_head")
    loss = lax.psum(loss_local, ("x", "y", "c"))

    grads = {}

    def ffn_bwd(layer, dh_out, h_in, hnf, g, u, tag):
        wg, wu, wd = full['ffn_w_gate'][layer], full['ffn_w_up'][layer], full['ffn_w_down'][layer]
        dg, du, act = ffn_down_bwd(dh_out, g, u, wd, "ffn_down_bwd" + tag)
        d_wd = mm_tn(act, dh_out, "dw_down" + tag)
        d_wg = mm_tn(hnf, dg, "dw_gate" + tag)
        d_wu = mm_tn(hnf, du, "dw_up" + tag)
        dh_in, d_gain = proj_rms_bwd([dg, du], [wg, wu], h_in, ffn_norm[layer:layer + 1], dh_out, 2,
                                     "ffn_up_bwd" + tag)
        return dh_in, d_gain, d_wg, d_wu, d_wd

    dh3, d_ffn_norm1, d_wg1, d_wu1, d_wd1 = ffn_bwd(1, dh4, h3, hnf1, g1, u1, "1")

    d_pool, d_sgu = out_proj_bwd(dh3, c_w_out, [F32, F32], "out_proj_bwd1")
    d_c_w_out = jnp.concatenate([mm_tn(pool, dh3, "dw_out1_pool"), mm_tn(sgu, dh3, "dw_out1_sgu")], axis=0)
    dzp, d_w_pool, d_pool_scale = pool_bwd(zp, w_pool_bf, pool_scale, d_pool, bl, seq, "pool_bwd")
    dzs, d_w_s, d_b_s_b, d_sgu_ln = sgu_bwd(zs, w_s, b_s_t, sgu_ln, d_sgu, bl, seq, "sgu_bwd")
    d_c_w_in = jnp.concatenate([mm_tn(hn1, dzp, "dw_in1_pool"), mm_tn(hn1, dzs, "dw_in1_sgu")], axis=1)
    dh2, d_mix_norm1 = proj_rms_bwd([dzp, dzs], [c_w_in[:, :POOL_WIDTH], c_w_in[:, POOL_WIDTH:]], h2,
                                    mix_norm[1:2], dh3, 1, "in_proj_bwd1")

    dh1, d_ffn_norm0, d_wg0, d_wu0, d_wd0 = ffn_bwd(0, dh2, h1, hnf0, g0, u0, "0")

    d_attn, d_conv = out_proj_bwd(dh1, a_w_out, [BF16, F32], "out_proj_bwd0")
    d_a_w_out = jnp.concatenate([mm_tn(attn, dh1, "dw_out0_attn"), mm_tn(conv, dh1, "dw_out0_conv")], axis=0)
    dq, dkv, d_sinks_b = attn_bwd(q, kv, sinks_b, d_attn, bl, seq, "attn_bwd")
    dcc, d_conv_taps, d_conv_vec = conv_bwd(cc, conv_taps, conv_vec, d_conv, bl, seq, "conv_bwd")
    dw_q, db_q = mm_tn(hn0, dq, "dw_in0_q", colsum=True)
    dw_kv, db_kv = mm_tn(hn0, dkv, "dw_in0_kv", colsum=True)
    dw_c, db_c = mm_tn(hn0, dcc, "dw_in0_c", colsum=True)
    d_a_w_in = jnp.concatenate([dw_q, dw_kv, dw_c], axis=1)
    d_a_b_in = jnp.concatenate([db_q, db_kv, db_c], axis=0)
    kq, kk = ATTN_WIDTH, ATTN_WIDTH + 2 * KV_WIDTH
    grad_x, d_mix_norm0 = proj_rms_bwd([dq, dkv, dcc], [a_w_in[:, :kq], a_w_in[:, kq:kk], a_w_in[:, kk:]], x,
                                       mix_norm[0:1], dh1, 1, "in_proj_bwd0")

    big_full = {'a_w_in': [d_a_w_in], 'a_w_out': [d_a_w_out], 'c_w_in': [d_c_w_in], 'c_w_out': [d_c_w_out],
                'ffn_w_gate': [d_wg0, d_wg1], 'ffn_w_up': [d_wu0, d_wu1], 'ffn_w_down': [d_wd0, d_wd1]}
    keep, give = [], []
    for k in BIG:
        t = _to_shard_major(k, big_full[k])
        h = t.shape[1] // 2
        keep.append(lax.dynamic_slice_in_dim(t, ci * h, h, axis=1))
        give.append(lax.dynamic_slice_in_dim(t, (1 - ci) * h, h, axis=1))
    got = sibling_swap(give, "reduce_pair")
    pair_sums = [add_cast(kp, gt, "pair_sum_" + k) for k, kp, gt in zip(BIG, keep, got)]
    from_chips = chip_scatter(pair_sums, "reduce_chips")
    halves = [sum_parts(p, "chip_sum_" + k) for k, p in zip(BIG, from_chips)]
    shard_grads = sibling_join(halves, "reduce_join")
    for k, g in zip(BIG, shard_grads):
        grads[k] = g.reshape(a[k].shape)

    small_full = {
        'mix_norm': jnp.stack([d_mix_norm0, d_mix_norm1]), 'a_b_in': d_a_b_in[None], 'a_sinks': d_sinks_b[:, 0][None],
        'a_conv_w': d_conv_taps[:CONV_KERNEL][None], 'a_conv_b': d_conv_vec[0][None], 'a_cln_g': d_conv_vec[1][None],
        'a_cln_b': d_conv_vec[2][None], 'c_w_pool': d_w_pool[None], 'c_pool_scale': d_pool_scale[0][None],
        'c_sln_g': d_sgu_ln[0][None], 'c_sln_b': d_sgu_ln[1][None], 'c_w_s': d_w_s[None],
        'c_b_s': d_b_s_b[:, :, 0][None], 'ffn_norm': jnp.stack([d_ffn_norm0, d_ffn_norm1]),
        'final_norm': d_final_norm}
    small_names = SMALL_REPL + SMALL_SHARD
    small_pack, small_layout = _pack([small_full[k] for k in small_names])
    small_sum = sum_parts(gather_all(small_pack, "allreduce_small"), "small_sum")
    for k, g in zip(small_names, _unpack(small_sum, small_layout)):
        if k in SMALL_SHARD:
            width = a[k].shape[-1]
            g = lax.dynamic_slice_in_dim(g, chip * width, width, axis=g.ndim - 1)
        grads[k] = g

    delta, new_m, new_v = {}, {}, {}
    for k in BIG:
        shp = a[k].shape
        v2 = lambda t: t.reshape(-1, shp[-1])
        d, m, v = adamw(v2(a[k]), v2(grads[k]), v2(a['m_' + k]), v2(a['v_' + k]), "adamw_" + k)
        delta[k], new_m[k], new_v[k] = d.reshape(shp), m.reshape(shp), v.reshape(shp)
    packs = [_pack([src[k] for k in small_names])
             for src in (a, grads, {k: a['m_' + k] for k in small_names}, {k: a['v_' + k] for k in small_names})]
    d, m, v = adamw(packs[0][0], packs[1][0], packs[2][0], packs[3][0], "adamw_small")
    lay = packs[0][1]
    for k, dv, mv, vv in zip(small_names, _unpack(d, lay), _unpack(m, lay), _unpack(v, lay)):
        delta[k], new_m[k], new_v[k] = dv, mv, vv

    return (loss, grad_x.reshape(a['x'].shape), *[grads[k] for k in WEIGHTS], *[delta[k] for k in WEIGHTS],
            *[new_m[k] for k in WEIGHTS], *[new_v[k] for k in WEIGHTS])
```

```python
import functools

import jax
import jax.numpy as jnp
from jax import lax
from jax.experimental import pallas as pl
from jax.experimental.pallas import tpu as pltpu

F32 = jnp.float32
BF16 = jnp.bfloat16

D_MODEL = 1024
EPS = 1e-5
N_Q_HEADS, N_KV_HEADS, HEAD_DIM = 8, 2, 64
ATTN_BLOCK = 128
ATTN_WIDTH = N_Q_HEADS * HEAD_DIM
KV_WIDTH = N_KV_HEADS * HEAD_DIM
CONV_WIDTH = 512
CONV_KERNEL = 31
CONV_HALO = 32
POOL_WINDOWS = (2, 4, 8, 16)
POOL_WIDTH = 512
POOL_HALO = 16
SGU_WIDTH = 512
SGU_CHUNK = 128
D_FF = 2816
FF_CHUNK = 128
LANES = 128
N_CHIPS = 4
N_DEV = 8

ADAM_LR, ADAM_B1, ADAM_B2, ADAM_EPS, ADAM_WD, ADAM_STEP = 0.001, 0.9, 0.999, 1e-08, 0.01, 10

VMEM_LIMIT = 56 * 2**20

WEIGHTS = ['mix_norm', 'a_w_in', 'a_b_in', 'a_sinks', 'a_conv_w', 'a_conv_b', 'a_cln_g', 'a_cln_b', 'a_w_out',
           'c_w_in', 'c_w_pool', 'c_pool_scale', 'c_sln_g', 'c_sln_b', 'c_w_s', 'c_b_s', 'c_w_out',
           'ffn_norm', 'ffn_w_gate', 'ffn_w_up', 'ffn_w_down', 'final_norm']
IN_NAMES = (['x'] + WEIGHTS + ['loss_target'] + ['m_' + n for n in WEIGHTS] + ['v_' + n for n in WEIGHTS])


def _params(*sem):
    return pltpu.CompilerParams(dimension_semantics=sem, vmem_limit_bytes=VMEM_LIMIT)


def _dot(a, b):
    return jnp.dot(a, b, preferred_element_type=F32)


def _dot_nt(a, b):
    return lax.dot_general(a, b, (((1,), (1,)), ((), ())), preferred_element_type=F32)


def _dot_tn(a, b):
    return lax.dot_general(a, b, (((0,), (0,)), ((), ())), preferred_element_type=F32)


def _sigmoid(v):
    return 1.0 / (1.0 + jnp.exp(-v))


def _row_tile(n, pref):
    t = min(n, pref)
    while n % t:
        t //= 2
    return t


def _col_tile(m, rows, budget=6 * 2**20):
    best = LANES
    for t in range(LANES, m + 1, LANES):
        if m % t == 0 and rows * t * 4 <= budget:
            best = t
    return best


def norm_inproj(x, gain, w, bias, splits, dtypes, name):
    n = x.shape[0]
    m = w.shape[1]
    tm = _row_tile(n, 512)

    def body(x_ref, g_ref, w_ref, b_ref, hn_ref, *outs):
        xv = x_ref[...]
        r = lax.rsqrt(jnp.mean(xv * xv, axis=-1, keepdims=True) + EPS)
        hn = ((xv * r) * g_ref[...]).astype(BF16)
        hn_ref[...] = hn
        z = _dot(hn, w_ref[...]) + b_ref[...]
        for o, (lo, hi) in zip(outs, splits):
            o[...] = z[:, lo:hi].astype(o.dtype)

    out_shape = [jax.ShapeDtypeStruct((n, D_MODEL), BF16)]
    out_specs = [pl.BlockSpec((tm, D_MODEL), lambda i: (i, 0))]
    for (lo, hi), dt in zip(splits, dtypes):
        out_shape.append(jax.ShapeDtypeStruct((n, hi - lo), dt))
        out_specs.append(pl.BlockSpec((tm, hi - lo), lambda i: (i, 0)))
    return pl.pallas_call(
        body, name=name, grid=(n // tm,),
        in_specs=[pl.BlockSpec((tm, D_MODEL), lambda i: (i, 0)),
                  pl.BlockSpec((1, D_MODEL), lambda i: (0, 0)),
                  pl.BlockSpec((D_MODEL, m), lambda i: (0, 0)),
                  pl.BlockSpec((1, m), lambda i: (0, 0))],
        out_specs=out_specs, out_shape=out_shape,
        compiler_params=_params("parallel"),
    )(x, gain, w, bias)


def out_proj(res, m1, m2, w, name):
    n = res.shape[0]
    k1, k2 = m1.shape[1], m2.shape[1]
    assert k1 == k2
    tm = _row_tile(n, 512)

    def body(r_ref, a_ref, b_ref, w1_ref, w2_ref, o_ref):
        o_ref[...] = r_ref[...] + _dot(a_ref[...], w1_ref[...]) + _dot(b_ref[...], w2_ref[...])

    return pl.pallas_call(
        body, name=name, grid=(n // tm,),
        in_specs=[pl.BlockSpec((tm, D_MODEL), lambda i: (i, 0)),
                  pl.BlockSpec((tm, k1), lambda i: (i, 0)),
                  pl.BlockSpec((tm, k2), lambda i: (i, 0)),
                  pl.BlockSpec((k1, D_MODEL), lambda i: (0, 0)),
                  pl.BlockSpec((k2, D_MODEL), lambda i: (1, 0))],
        out_specs=pl.BlockSpec((tm, D_MODEL), lambda i: (i, 0)),
        out_shape=jax.ShapeDtypeStruct((n, D_MODEL), F32),
        compiler_params=_params("parallel"),
    )(res, m1, m2, w, w)


def ffn_gate_up(h, gain, wg, wu, name):
    n = h.shape[0]
    tm = _row_tile(n, 1024)
    th = D_FF // 2

    def body(h_ref, g_ref, wg_ref, wu_ref, hn_ref, go_ref, uo_ref):
        @pl.when(pl.program_id(1) == 0)
        def _():
            xv = h_ref[...]
            r = lax.rsqrt(jnp.mean(xv * xv, axis=-1, keepdims=True) + EPS)
            hn_ref[...] = ((xv * r) * g_ref[...]).astype(BF16)

        hn = hn_ref[...]
        go_ref[...] = _dot(hn, wg_ref[...]).astype(BF16)
        uo_ref[...] = _dot(hn, wu_ref[...]).astype(BF16)

    return pl.pallas_call(
        body, name=name, grid=(n // tm, D_FF // th),
        in_specs=[pl.BlockSpec((tm, D_MODEL), lambda i, j: (i, 0)),
                  pl.BlockSpec((1, D_MODEL), lambda i, j: (0, 0)),
                  pl.BlockSpec((D_MODEL, th), lambda i, j: (0, j)),
                  pl.BlockSpec((D_MODEL, th), lambda i, j: (0, j))],
        out_specs=[pl.BlockSpec((tm, D_MODEL), lambda i, j: (i, 0)),
                   pl.BlockSpec((tm, th), lambda i, j: (i, j)),
                   pl.BlockSpec((tm, th), lambda i, j: (i, j))],
        out_shape=[jax.ShapeDtypeStruct((n, D_MODEL), BF16),
                   jax.ShapeDtypeStruct((n, D_FF), BF16),
                   jax.ShapeDtypeStruct((n, D_FF), BF16)],
        compiler_params=_params("parallel", "arbitrary"),
    )(h, gain, wg, wu)


def ffn_down(h, g, u, wd, name):
    n = h.shape[0]
    tm = _row_tile(n, 512)

    def body(h_ref, g_ref, u_ref, w_ref, o_ref, a_ref):
        for c0 in range(0, D_FF, FF_CHUNK):
            gv = g_ref[:, c0:c0 + FF_CHUNK].astype(F32)
            a_ref[:, c0:c0 + FF_CHUNK] = (gv * _sigmoid(gv) * u_ref[:, c0:c0 + FF_CHUNK].astype(F32)).astype(BF16)
        o_ref[...] = h_ref[...] + _dot(a_ref[...], w_ref[...])

    return pl.pallas_call(
        body, name=name, grid=(n // tm,),
        in_specs=[pl.BlockSpec((tm, D_MODEL), lambda i: (i, 0)),
                  pl.BlockSpec((tm, D_FF), lambda i: (i, 0)),
                  pl.BlockSpec((tm, D_FF), lambda i: (i, 0)),
                  pl.BlockSpec((D_FF, D_MODEL), lambda i: (0, 0))],
        out_specs=pl.BlockSpec((tm, D_MODEL), lambda i: (i, 0)),
        out_shape=jax.ShapeDtypeStruct((n, D_MODEL), F32),
        scratch_shapes=[pltpu.VMEM((tm, D_FF), BF16)],
        compiler_params=_params("parallel"),
    )(h, g, u, wd)


def ffn_down_bwd(dh, g, u, wd, name):
    n = dh.shape[0]
    tm = _row_tile(n, 512)
    th = D_FF // 2

    def body(dh_ref, g_ref, u_ref, w_ref, dg_ref, du_ref, a_ref, da_ref):
        da_ref[...] = _dot_nt(dh_ref[...].astype(BF16), w_ref[...])
        for c0 in range(0, th, FF_CHUNK):
            cols = slice(c0, c0 + FF_CHUNK)
            da = da_ref[:, cols]
            gv = g_ref[:, cols].astype(F32)
            uv = u_ref[:, cols].astype(F32)
            sg = _sigmoid(gv)
            act = gv * sg
            dg_ref[:, cols] = (da * uv * (sg * (1.0 + gv * (1.0 - sg)))).astype(BF16)
            du_ref[:, cols] = (da * act).astype(BF16)
            a_ref[:, cols] = (act * uv).astype(BF16)

    spec_h = pl.BlockSpec((tm, th), lambda i, j: (i, j))
    return pl.pallas_call(
        body, name=name, grid=(n // tm, D_FF // th),
        in_specs=[pl.BlockSpec((tm, D_MODEL), lambda i, j: (i, 0)), spec_h, spec_h,
                  pl.BlockSpec((th, D_MODEL), lambda i, j: (j, 0))],
        out_specs=[spec_h, spec_h, spec_h],
        out_shape=[jax.ShapeDtypeStruct((n, D_FF), BF16)] * 3,
        scratch_shapes=[pltpu.VMEM((tm, th), F32)],
        compiler_params=_params("parallel", "arbitrary"),
    )(dh, g, u, wd)


def mm_tn(x, dy, name, colsum=False):
    n, k = x.shape
    m = dy.shape[1]
    tn = _col_tile(m, k)
    tt = _row_tile(n, 1024)

    def body(x_ref, dy_ref, o_ref, *rest):
        t = pl.program_id(1)
        dyv = dy_ref[...]
        part = _dot_tn(x_ref[...], dyv.astype(BF16))

        @pl.when(t == 0)
        def _():
            o_ref[...] = part

        @pl.when(t > 0)
        def _():
            o_ref[...] += part

        if colsum:
            cs = jnp.sum(dyv.astype(F32), axis=0, keepdims=True)

            @pl.when(t == 0)
            def _():
                rest[0][...] = jnp.broadcast_to(cs, rest[0].shape)

            @pl.when(t > 0)
            def _():
                rest[0][...] += jnp.broadcast_to(cs, rest[0].shape)

    out_shape = [jax.ShapeDtypeStruct((k, m), F32)]
    out_specs = [pl.BlockSpec((k, tn), lambda j, t: (0, j))]
    if colsum:
        out_shape.append(jax.ShapeDtypeStruct((8, m), F32))
        out_specs.append(pl.BlockSpec((8, tn), lambda j, t: (0, j)))
    res = pl.pallas_call(
        body, name=name, grid=(m // tn, n // tt),
        in_specs=[pl.BlockSpec((tt, k), lambda j, t: (t, 0)),
                  pl.BlockSpec((tt, tn), lambda j, t: (t, j))],
        out_specs=out_specs, out_shape=out_shape,
        compiler_params=_params("parallel", "arbitrary"),
    )(x, dy)
    return (res[0], res[1][0]) if colsum else res[0]


def out_proj_bwd(dh, w, dtypes, name):
    n = dh.shape[0]
    k = w.shape[0]
    half = k // 2
    tm = _row_tile(n, 512)

    def body(dh_ref, w_ref, a_ref, b_ref):
        dm = _dot_nt(dh_ref[...].astype(BF16), w_ref[...])
        a_ref[...] = dm[:, :half].astype(a_ref.dtype)
        b_ref[...] = dm[:, half:].astype(b_ref.dtype)

    return pl.pallas_call(
        body, name=name, grid=(n // tm,),
        in_specs=[pl.BlockSpec((tm, D_MODEL), lambda i: (i, 0)),
                  pl.BlockSpec((k, D_MODEL), lambda i: (0, 0))],
        out_specs=[pl.BlockSpec((tm, half), lambda i: (i, 0))] * 2,
        out_shape=[jax.ShapeDtypeStruct((n, half), dtypes[0]), jax.ShapeDtypeStruct((n, half), dtypes[1])],
        compiler_params=_params("parallel"),
    )(dh, w)


def proj_rms_bwd(dys, ws, h_in, gain, dres, nk, name, tm_pref=512):
    n = h_in.shape[0]
    npair = len(dys)
    tm = _row_tile(n, tm_pref)
    tks = [dy.shape[1] // nk for dy in dys]

    def body(*refs):
        dy_refs = refs[:npair]
        w_refs = refs[npair:2 * npair]
        h_ref, g_ref, dr_ref, o_ref, dg_ref, acc_ref = refs[2 * npair:]
        i, k = pl.program_id(0), pl.program_id(1)
        part = _dot_nt(dy_refs[0][...], w_refs[0][...])
        for p in range(1, npair):
            part = part + _dot_nt(dy_refs[p][...], w_refs[p][...])

        @pl.when(k == 0)
        def _():
            acc_ref[...] = part

        @pl.when(k > 0)
        def _():
            acc_ref[...] += part

        @pl.when(k == nk - 1)
        def _():
            dhn = acc_ref[...]
            xv = h_ref[...]
            r = lax.rsqrt(jnp.mean(xv * xv, axis=-1, keepdims=True) + EPS)
            xh = xv * r
            uv = dhn * g_ref[...]
            o_ref[...] = dr_ref[...] + r * (uv - xh * jnp.mean(uv * xh, axis=-1, keepdims=True))
            dgp = jnp.broadcast_to(jnp.sum(dhn * xh, axis=0, keepdims=True), dg_ref.shape)

            @pl.when(i == 0)
            def _():
                dg_ref[...] = dgp

            @pl.when(i > 0)
            def _():
                dg_ref[...] += dgp

    row = pl.BlockSpec((tm, D_MODEL), lambda i, k: (i, 0))
    in_specs = [pl.BlockSpec((tm, tk), lambda i, k: (i, k)) for tk in tks]
    in_specs += [pl.BlockSpec((D_MODEL, tk), lambda i, k: (0, k)) for tk in tks]
    in_specs += [row, pl.BlockSpec((1, D_MODEL), lambda i, k: (0, 0)), row]
    dh, dgain = pl.pallas_call(
        body, name=name, grid=(n // tm, nk),
        in_specs=in_specs,
        out_specs=[row, pl.BlockSpec((8, D_MODEL), lambda i, k: (0, 0))],
        out_shape=[jax.ShapeDtypeStruct((n, D_MODEL), F32), jax.ShapeDtypeStruct((8, D_MODEL), F32)],
        scratch_shapes=[pltpu.VMEM((tm, D_MODEL), F32)],
        compiler_params=_params("arbitrary", "arbitrary"),
    )(*dys, *ws, h_in, gain, dres)
    return dh, dgain[0]


def loss_head(h, gain, target, name):
    n = h.shape[0]
    tm = _row_tile(n, 512)

    def body(h_ref, g_ref, t_ref, dh_ref, dg_ref, l_ref):
        i = pl.program_id(0)
        xv = h_ref[...]
        r = lax.rsqrt(jnp.mean(xv * xv, axis=-1, keepdims=True) + EPS)
        xh = xv * r
        err = xh * g_ref[...] - t_ref[...]
        dy = err * (1.0 / D_MODEL)
        uv = dy * g_ref[...]
        dh_ref[...] = r * (uv - xh * jnp.mean(uv * xh, axis=-1, keepdims=True))
        dgp = jnp.broadcast_to(jnp.sum(dy * xh, axis=0, keepdims=True), dg_ref.shape)
        lp = jnp.sum(jnp.sum(err * err, axis=-1, keepdims=True), axis=0, keepdims=True) * (0.5 / D_MODEL)
        lp = jnp.broadcast_to(lp, l_ref.shape)

        @pl.when(i == 0)
        def _():
            dg_ref[...] = dgp
            l_ref[...] = lp

        @pl.when(i > 0)
        def _():
            dg_ref[...] += dgp
            l_ref[...] += lp

    row = pl.BlockSpec((tm, D_MODEL), lambda i: (i, 0))
    dh, dg, l = pl.pallas_call(
        body, name=name, grid=(n // tm,),
        in_specs=[row, pl.BlockSpec((1, D_MODEL), lambda i: (0, 0)), row],
        out_specs=[row, pl.BlockSpec((8, D_MODEL), lambda i: (0, 0)), pl.BlockSpec((8, LANES), lambda i: (0, 0))],
        out_shape=[jax.ShapeDtypeStruct((n, D_MODEL), F32), jax.ShapeDtypeStruct((8, D_MODEL), F32),
                   jax.ShapeDtypeStruct((8, LANES), F32)],
        compiler_params=_params("arbitrary"),
    )(h, gain, target)
    return dh, dg[0], l[0, 0]


GROUP = N_Q_HEADS // N_KV_HEADS
GQ = GROUP * ATTN_BLOCK


def _attn_mask_t(n):
    r = lax.broadcasted_iota(jnp.int32, (2 * ATTN_BLOCK, GQ), 0)
    qi = lax.broadcasted_iota(jnp.int32, (2 * ATTN_BLOCK, GQ), 1) & (ATTN_BLOCK - 1)
    band = (r > qi) & (r <= qi + ATTN_BLOCK)
    return band & ((r >= ATTN_BLOCK) | (n > 0))


def _stack_heads(blk, kh):
    return jnp.concatenate([blk[:, (kh * GROUP + g) * HEAD_DIM:(kh * GROUP + g + 1) * HEAD_DIM]
                            for g in range(GROUP)], axis=0)


def _attn_probs_t(kk, qs, mask, sink):
    s = _dot_nt(kk, qs) * (HEAD_DIM ** -0.5)
    s = jnp.where(mask, s, -1e30)
    m = jnp.maximum(jnp.max(s, axis=0, keepdims=True), sink)
    p = jnp.exp(s - m)
    esink = jnp.exp(sink - m)
    inv = 1.0 / (jnp.sum(p, axis=0, keepdims=True) + esink)
    return p * inv, esink * inv


def attn_fwd(q, kv, sinks_t, nseq, seq, name):
    nb = seq // ATTN_BLOCK

    def body(q_ref, kv_ref, s_ref, o_ref, kvp):
        kvp[0:ATTN_BLOCK, :] = jnp.zeros((ATTN_BLOCK, 2 * KV_WIDTH), BF16)
        kvp[ATTN_BLOCK:, :] = kv_ref[...]

        def blk(n, carry):
            st = pl.multiple_of(n * ATTN_BLOCK, ATTN_BLOCK)
            qb = q_ref[pl.ds(st, ATTN_BLOCK), :]
            kw = kvp[pl.ds(st, 2 * ATTN_BLOCK), :]
            mask = _attn_mask_t(n)
            for kh in range(N_KV_HEADS):
                kk = kw[:, kh * HEAD_DIM:(kh + 1) * HEAD_DIM]
                vv = kw[:, KV_WIDTH + kh * HEAD_DIM:KV_WIDTH + (kh + 1) * HEAD_DIM]
                probs, _ = _attn_probs_t(kk, _stack_heads(qb, kh), mask, s_ref[kh:kh + 1, :])
                ot = _dot_tn(vv, probs.astype(BF16))
                for pair in range(GROUP // 2):
                    two = jnp.concatenate([ot[:, (2 * pair) * ATTN_BLOCK:(2 * pair + 1) * ATTN_BLOCK],
                                           ot[:, (2 * pair + 1) * ATTN_BLOCK:(2 * pair + 2) * ATTN_BLOCK]], axis=0)
                    col = (kh * GROUP + 2 * pair) * HEAD_DIM
                    o_ref[pl.ds(st, ATTN_BLOCK), col:col + 2 * HEAD_DIM] = two.T.astype(o_ref.dtype)
            return carry

        lax.fori_loop(0, nb, blk, 0)

    return pl.pallas_call(
        body, name=name, grid=(nseq,),
        in_specs=[pl.BlockSpec((seq, ATTN_WIDTH), lambda b: (b, 0)),
                  pl.BlockSpec((seq, 2 * KV_WIDTH), lambda b: (b, 0)),
                  pl.BlockSpec((8, GQ), lambda b: (0, 0))],
        out_specs=pl.BlockSpec((seq, ATTN_WIDTH), lambda b: (b, 0)),
        out_shape=jax.ShapeDtypeStruct((nseq * seq, ATTN_WIDTH), BF16),
        scratch_shapes=[pltpu.VMEM((ATTN_BLOCK + seq, 2 * KV_WIDTH), BF16)],
        compiler_params=_params("parallel"),
    )(q, kv, sinks_t)


def attn_bwd(q, kv, sinks_t, do, nseq, seq, name):
    nb = seq // ATTN_BLOCK

    def body(q_ref, kv_ref, s_ref, do_ref, dq_ref, dkv_ref, ds_ref, kvp, dkvp, dsacc):
        @pl.when(pl.program_id(0) == 0)
        def _():
            dsacc[...] = jnp.zeros(dsacc.shape, F32)

        kvp[0:ATTN_BLOCK, :] = jnp.zeros((ATTN_BLOCK, 2 * KV_WIDTH), BF16)
        kvp[ATTN_BLOCK:, :] = kv_ref[...]
        dkvp[...] = jnp.zeros(dkvp.shape, F32)

        def blk(n, carry):
            st = pl.multiple_of(n * ATTN_BLOCK, ATTN_BLOCK)
            qb = q_ref[pl.ds(st, ATTN_BLOCK), :]
            dob = do_ref[pl.ds(st, ATTN_BLOCK), :]
            kw = kvp[pl.ds(st, 2 * ATTN_BLOCK), :]
            mask = _attn_mask_t(n)
            for kh in range(N_KV_HEADS):
                kk = kw[:, kh * HEAD_DIM:(kh + 1) * HEAD_DIM]
                vv = kw[:, KV_WIDTH + kh * HEAD_DIM:KV_WIDTH + (kh + 1) * HEAD_DIM]
                qs = _stack_heads(qb, kh)
                dos = _stack_heads(dob, kh)
                probs, psink = _attn_probs_t(kk, qs, mask, s_ref[kh:kh + 1, :])
                dp = _dot_nt(vv, dos)
                dv = _dot(probs.astype(BF16), dos)
                rowdot = jnp.sum(probs * dp, axis=0, keepdims=True)
                dsc = (probs * (dp - rowdot) * (HEAD_DIM ** -0.5)).astype(BF16)
                dsacc[kh:kh + 1, :] += -psink * rowdot
                dk = _dot(dsc, qs)
                dqs = _dot_tn(dsc, kk)
                for g in range(GROUP):
                    col = (kh * GROUP + g) * HEAD_DIM
                    dq_ref[pl.ds(st, ATTN_BLOCK), col:col + HEAD_DIM] = (
                        dqs[g * ATTN_BLOCK:(g + 1) * ATTN_BLOCK].astype(dq_ref.dtype))
                dkvp[pl.ds(st, 2 * ATTN_BLOCK), kh * HEAD_DIM:(kh + 1) * HEAD_DIM] += dk
                dkvp[pl.ds(st, 2 * ATTN_BLOCK), KV_WIDTH + kh * HEAD_DIM:KV_WIDTH + (kh + 1) * HEAD_DIM] += dv
            return carry

        lax.fori_loop(0, nb, blk, 0)
        dkv_ref[...] = dkvp[ATTN_BLOCK:, :].astype(dkv_ref.dtype)

        @pl.when(pl.program_id(0) == nseq - 1)
        def _():
            for kh in range(N_KV_HEADS):
                for g in range(GROUP):
                    tot = jnp.sum(dsacc[kh:kh + 1, g * ATTN_BLOCK:(g + 1) * ATTN_BLOCK], axis=1, keepdims=True)
                    ds_ref[kh * GROUP + g:kh * GROUP + g + 1, :] = jnp.broadcast_to(tot, (1, LANES))

    seq_q = pl.BlockSpec((seq, ATTN_WIDTH), lambda b: (b, 0))
    seq_kv = pl.BlockSpec((seq, 2 * KV_WIDTH), lambda b: (b, 0))
    return pl.pallas_call(
        body, name=name, grid=(nseq,),
        in_specs=[seq_q, seq_kv, pl.BlockSpec((8, GQ), lambda b: (0, 0)), seq_q],
        out_specs=[seq_q, seq_kv, pl.BlockSpec((N_Q_HEADS, LANES), lambda b: (0, 0))],
        out_shape=[jax.ShapeDtypeStruct((nseq * seq, ATTN_WIDTH), BF16),
                   jax.ShapeDtypeStruct((nseq * seq, 2 * KV_WIDTH), BF16),
                   jax.ShapeDtypeStruct((N_Q_HEADS, LANES), F32)],
        scratch_shapes=[pltpu.VMEM((ATTN_BLOCK + seq, 2 * KV_WIDTH), BF16),
                        pltpu.VMEM((ATTN_BLOCK + seq, 2 * KV_WIDTH), F32),
                        pltpu.VMEM((8, GQ), F32)],
        compiler_params=_params("arbitrary"),
    )(q, kv, sinks_t, do)


CONV_T = 128


def _conv_taps(win, w_ref, lanes, init):
    acc = init
    for j in range(CONV_KERNEL):
        sh = win if j == CONV_KERNEL - 1 else pltpu.roll(win, CONV_KERNEL - 1 - j, 0)
        acc = acc + w_ref[j:j + 1, lanes] * sh[CONV_HALO:CONV_HALO + CONV_T]
    return acc


def _conv_block(h0p, w_ref, vec_ref, st):
    cols = []
    for cs in range(CONV_WIDTH // LANES):
        lanes = slice(cs * LANES, (cs + 1) * LANES)
        win = h0p[pl.ds(st, CONV_T + CONV_HALO), lanes]
        init = jnp.broadcast_to(vec_ref[0:1, lanes], (CONV_T, LANES))
        cols.append(_conv_taps(win, w_ref, lanes, init))
    return jnp.concatenate(cols, axis=-1)


def _glu_store(c_ref, h0p, st):
    cb = c_ref[pl.ds(st, CONV_T), :]
    h0p[pl.ds(pl.multiple_of(st + CONV_HALO, CONV_HALO), CONV_T), :] = cb[:, :CONV_WIDTH] * _sigmoid(cb[:, CONV_WIDTH:])


def conv_fwd(c, w, vec, nseq, seq, name):
    nb = seq // CONV_T

    def body(c_ref, w_ref, vec_ref, o_ref, h0p):
        h0p[0:CONV_HALO, :] = jnp.zeros((CONV_HALO, CONV_WIDTH), F32)

        def blk(n, carry):
            st = pl.multiple_of(n * CONV_T, CONV_T)
            _glu_store(c_ref, h0p, st)
            h1 = _conv_block(h0p, w_ref, vec_ref, st)
            mu = jnp.mean(h1, axis=-1, keepdims=True)
            xc = h1 - mu
            rstd = lax.rsqrt(jnp.mean(xc * xc, axis=-1, keepdims=True) + EPS)
            y = xc * rstd * vec_ref[1:2, :] + vec_ref[2:3, :]
            o_ref[pl.ds(st, CONV_T), :] = (y * _sigmoid(y)).astype(o_ref.dtype)
            return carry

        lax.fori_loop(0, nb, blk, 0)

    return pl.pallas_call(
        body, name=name, grid=(nseq,),
        in_specs=[pl.BlockSpec((seq, 2 * CONV_WIDTH), lambda b: (b, 0)),
                  pl.BlockSpec((32, CONV_WIDTH), lambda b: (0, 0)),
                  pl.BlockSpec((8, CONV_WIDTH), lambda b: (0, 0))],
        out_specs=pl.BlockSpec((seq, CONV_WIDTH), lambda b: (b, 0)),
        out_shape=jax.ShapeDtypeStruct((nseq * seq, CONV_WIDTH), BF16),
        scratch_shapes=[pltpu.VMEM((CONV_HALO + seq, CONV_WIDTH), F32)],
        compiler_params=_params("parallel"),
    )(c, w, vec)


def conv_bwd(c, w, vec, dout, nseq, seq, name):
    nb = seq // CONV_T

    def body(c_ref, w_ref, vec_ref, do_ref, dc_ref, dw_ref, dvec_ref, h0p, dh1p):
        @pl.when(pl.program_id(0) == 0)
        def _():
            dw_ref[...] = jnp.zeros(dw_ref.shape, F32)
            dvec_ref[...] = jnp.zeros(dvec_ref.shape, F32)

        h0p[0:CONV_HALO, :] = jnp.zeros((CONV_HALO, CONV_WIDTH), F32)
        dh1p[seq:seq + CONV_HALO, :] = jnp.zeros((CONV_HALO, CONV_WIDTH), F32)

        def pass_a(n, carry):
            st = pl.multiple_of(n * CONV_T, CONV_T)
            _glu_store(c_ref, h0p, st)
            h1 = _conv_block(h0p, w_ref, vec_ref, st)
            mu = jnp.mean(h1, axis=-1, keepdims=True)
            xc = h1 - mu
            rstd = lax.rsqrt(jnp.mean(xc * xc, axis=-1, keepdims=True) + EPS)
            xh = xc * rstd
            y = xh * vec_ref[1:2, :] + vec_ref[2:3, :]
            sg = _sigmoid(y)
            dy = do_ref[pl.ds(st, CONV_T), :] * (sg * (1.0 + y * (1.0 - sg)))
            dvec_ref[1:2, :] += jnp.sum(dy * xh, axis=0, keepdims=True)
            dvec_ref[2:3, :] += jnp.sum(dy, axis=0, keepdims=True)
            dxh = dy * vec_ref[1:2, :]
            dh1 = rstd * (dxh - jnp.mean(dxh, axis=-1, keepdims=True)
                          - xh * jnp.mean(dxh * xh, axis=-1, keepdims=True))
            dvec_ref[0:1, :] += jnp.sum(dh1, axis=0, keepdims=True)
            dh1p[pl.ds(st, CONV_T), :] = dh1
            return carry

        lax.fori_loop(0, nb, pass_a, 0)

        def pass_b(n, carry):
            st = pl.multiple_of(n * CONV_T, CONV_T)
            cols = []
            for cs in range(CONV_WIDTH // LANES):
                lanes = slice(cs * LANES, (cs + 1) * LANES)
                wind = dh1p[pl.ds(st, CONV_T + CONV_HALO), lanes]
                winh = h0p[pl.ds(st, CONV_T + CONV_HALO), lanes]
                d1 = wind[0:CONV_T]
                acc = jnp.zeros((CONV_T, LANES), F32)
                for j in range(CONV_KERNEL):
                    acc = acc + w_ref[j:j + 1, lanes] * pltpu.roll(wind, 2 + j, 0)[CONV_HALO:CONV_HALO + CONV_T]
                    hs = winh if j == CONV_KERNEL - 1 else pltpu.roll(winh, CONV_KERNEL - 1 - j, 0)
                    dw_ref[j:j + 1, lanes] += jnp.sum(d1 * hs[CONV_HALO:CONV_HALO + CONV_T], axis=0, keepdims=True)
                cols.append(acc)
            dh0 = jnp.concatenate(cols, axis=-1)
            cb = c_ref[pl.ds(st, CONV_T), :]
            av, gt = cb[:, :CONV_WIDTH], cb[:, CONV_WIDTH:]
            sg = _sigmoid(gt)
            dc_ref[pl.ds(st, CONV_T), :] = jnp.concatenate(
                [dh0 * sg, dh0 * av * sg * (1.0 - sg)], axis=-1).astype(dc_ref.dtype)
            return carry

        lax.fori_loop(0, nb, pass_b, 0)

    return pl.pallas_call(
        body, name=name, grid=(nseq,),
        in_specs=[pl.BlockSpec((seq, 2 * CONV_WIDTH), lambda b: (b, 0)),
                  pl.BlockSpec((32, CONV_WIDTH), lambda b: (0, 0)),
                  pl.BlockSpec((8, CONV_WIDTH), lambda b: (0, 0)),
                  pl.BlockSpec((seq, CONV_WIDTH), lambda b: (b, 0))],
        out_specs=[pl.BlockSpec((seq, 2 * CONV_WIDTH), lambda b: (b, 0)),
                   pl.BlockSpec((32, CONV_WIDTH), lambda b: (0, 0)),
                   pl.BlockSpec((8, CONV_WIDTH), lambda b: (0, 0))],
        out_shape=[jax.ShapeDtypeStruct((nseq * seq, 2 * CONV_WIDTH), BF16),
                   jax.ShapeDtypeStruct((32, CONV_WIDTH), F32),
                   jax.ShapeDtypeStruct((8, CONV_WIDTH), F32)],
        scratch_shapes=[pltpu.VMEM((CONV_HALO + seq, CONV_WIDTH), F32),
                        pltpu.VMEM((seq + CONV_HALO, CONV_WIDTH), F32)],
        compiler_params=_params("arbitrary"),
    )(c, w, vec, dout)


POOL_T = 128


def _pooled_block(zpp, st, grp):
    lanes = slice(grp * LANES, (grp + 1) * LANES)
    win = zpp[pl.ds(st, POOL_T + POOL_HALO), lanes]
    acc = win
    for lvl in range(grp + 1):
        acc = acc + pltpu.roll(acc, 1 << lvl, 0)
    t = st + lax.broadcasted_iota(jnp.int32, (POOL_T, 1), 0)
    inv = 1.0 / jnp.minimum(t + 1, POOL_WINDOWS[grp]).astype(F32)
    return acc[POOL_HALO:] * inv - win[POOL_HALO:], inv


def pool_fwd(zp, wp, scale, nseq, seq, name):
    nb = seq // POOL_T

    def body(z_ref, wp_ref, sc_ref, o_ref, zpp):
        zpp[0:POOL_HALO, :] = jnp.zeros((POOL_HALO, POOL_WIDTH), F32)
        zpp[POOL_HALO:, :] = z_ref[...]

        def blk(n, carry):
            st = pl.multiple_of(n * POOL_T, POOL_T)
            for grp in range(len(POOL_WINDOWS)):
                lanes = slice(grp * LANES, (grp + 1) * LANES)
                pooled, _ = _pooled_block(zpp, st, grp)
                o_ref[pl.ds(st, POOL_T), lanes] = (
                    _dot(pooled.astype(BF16), wp_ref[grp]) * sc_ref[0:1, lanes]).astype(o_ref.dtype)
            return carry

        lax.fori_loop(0, nb, blk, 0)

    return pl.pallas_call(
        body, name=name, grid=(nseq,),
        in_specs=[pl.BlockSpec((seq, POOL_WIDTH), lambda b: (b, 0)),
                  pl.BlockSpec((4, LANES, LANES), lambda b: (0, 0, 0)),
                  pl.BlockSpec((1, POOL_WIDTH), lambda b: (0, 0))],
        out_specs=pl.BlockSpec((seq, POOL_WIDTH), lambda b: (b, 0)),
        out_shape=jax.ShapeDtypeStruct((nseq * seq, POOL_WIDTH), BF16),
        scratch_shapes=[pltpu.VMEM((POOL_HALO + seq, POOL_WIDTH), F32)],
        compiler_params=_params("parallel"),
    )(zp, wp, scale)


def pool_bwd(zp, wp, scale, dout, nseq, seq, name):
    nb = seq // POOL_T

    def body(z_ref, wp_ref, sc_ref, do_ref, dz_ref, dwp_ref, dsc_ref, zpp, dpcp, negd):
        @pl.when(pl.program_id(0) == 0)
        def _():
            dwp_ref[...] = jnp.zeros(dwp_ref.shape, F32)
            dsc_ref[...] = jnp.zeros(dsc_ref.shape, F32)

        zpp[0:POOL_HALO, :] = jnp.zeros((POOL_HALO, POOL_WIDTH), F32)
        zpp[POOL_HALO:, :] = z_ref[...]
        dpcp[seq:seq + POOL_HALO, :] = jnp.zeros((POOL_HALO, POOL_WIDTH), F32)

        def pass_a(n, carry):
            st = pl.multiple_of(n * POOL_T, POOL_T)
            for grp in range(len(POOL_WINDOWS)):
                lanes = slice(grp * LANES, (grp + 1) * LANES)
                pooled, inv = _pooled_block(zpp, st, grp)
                pb = pooled.astype(BF16)
                dob = do_ref[pl.ds(st, POOL_T), lanes]
                dsc_ref[0:1, lanes] += jnp.sum(dob * _dot(pb, wp_ref[grp]), axis=0, keepdims=True)
                dpm = (dob * sc_ref[0:1, lanes]).astype(BF16)
                dwp_ref[grp] += _dot_tn(pb, dpm)
                dpooled = _dot_nt(dpm, wp_ref[grp])
                negd[pl.ds(st, POOL_T), lanes] = -dpooled
                dpcp[pl.ds(st, POOL_T), lanes] = dpooled * inv
            return carry

        lax.fori_loop(0, nb, pass_a, 0)

        def pass_b(n, carry):
            st = pl.multiple_of(n * POOL_T, POOL_T)
            rows = POOL_T + POOL_HALO
            for grp in range(len(POOL_WINDOWS)):
                lanes = slice(grp * LANES, (grp + 1) * LANES)
                acc = dpcp[pl.ds(st, rows), lanes]
                for lvl in range(grp + 1):
                    acc = acc + pltpu.roll(acc, rows - (1 << lvl), 0)
                dz_ref[pl.ds(st, POOL_T), lanes] = (acc[0:POOL_T] + negd[pl.ds(st, POOL_T), lanes]).astype(dz_ref.dtype)
            return carry

        lax.fori_loop(0, nb, pass_b, 0)

    seq_spec = pl.BlockSpec((seq, POOL_WIDTH), lambda b: (b, 0))
    return pl.pallas_call(
        body, name=name, grid=(nseq,),
        in_specs=[seq_spec, pl.BlockSpec((4, LANES, LANES), lambda b: (0, 0, 0)),
                  pl.BlockSpec((1, POOL_WIDTH), lambda b: (0, 0)), seq_spec],
        out_specs=[seq_spec, pl.BlockSpec((4, LANES, LANES), lambda b: (0, 0, 0)),
                   pl.BlockSpec((8, POOL_WIDTH), lambda b: (0, 0))],
        out_shape=[jax.ShapeDtypeStruct((nseq * seq, POOL_WIDTH), BF16),
                   jax.ShapeDtypeStruct((4, LANES, LANES), F32),
                   jax.ShapeDtypeStruct((8, POOL_WIDTH), F32)],
        scratch_shapes=[pltpu.VMEM((POOL_HALO + seq, POOL_WIDTH), F32),
                        pltpu.VMEM((seq + POOL_HALO, POOL_WIDTH), F32),
                        pltpu.VMEM((seq, POOL_WIDTH), F32)],
        compiler_params=_params("arbitrary"),
    )(zp, wp, scale, dout)


GELU_C0 = 0.7978845608028654
GELU_C1 = 0.044715


def _gelu(xv):
    return xv * (0.5 * (1.0 + jnp.tanh(GELU_C0 * (xv + GELU_C1 * (xv * xv * xv)))))


def _gelu_grad(xv):
    t = jnp.tanh(GELU_C0 * (xv + GELU_C1 * (xv * xv * xv)))
    return 0.5 * (1.0 + t) + 0.5 * xv * (1.0 - t * t) * (GELU_C0 * (1.0 + 3.0 * GELU_C1 * xv * xv))


def _tril():
    ti = lax.broadcasted_iota(jnp.int32, (SGU_CHUNK, SGU_CHUNK), 0)
    si = lax.broadcasted_iota(jnp.int32, (SGU_CHUNK, SGU_CHUNK), 1)
    return si <= ti


def sgu_fwd(zs, ws, bst, ln, nseq, seq, name):
    nc = seq // SGU_CHUNK

    def body(z_ref, ws_ref, bs_ref, ln_ref, o_ref):
        tril = _tril()

        def blk(n, carry):
            st = pl.multiple_of(n * SGU_CHUNK, SGU_CHUNK)
            ge = _gelu(z_ref[pl.ds(st, SGU_CHUNK), :])
            uu, vv = ge[:, :SGU_WIDTH], ge[:, SGU_WIDTH:]
            mu = jnp.mean(vv, axis=-1, keepdims=True)
            xc = vv - mu
            rstd = lax.rsqrt(jnp.mean(xc * xc, axis=-1, keepdims=True) + EPS)
            vn = (xc * rstd * ln_ref[0:1, :] + ln_ref[1:2, :]).astype(BF16)
            for g in range(4):
                lanes = slice(g * LANES, (g + 1) * LANES)
                wm = jnp.where(tril, ws_ref[g], 0.0).astype(BF16)
                mixed = _dot(wm, vn[:, lanes]) + bs_ref[:, g:g + 1]
                o_ref[pl.ds(st, SGU_CHUNK), lanes] = (uu[:, lanes] * mixed).astype(o_ref.dtype)
            return carry

        lax.fori_loop(0, nc, blk, 0)

    return pl.pallas_call(
        body, name=name, grid=(nseq,),
        in_specs=[pl.BlockSpec((seq, 2 * SGU_WIDTH), lambda b: (b, 0)),
                  pl.BlockSpec((4, LANES, LANES), lambda b: (0, 0, 0)),
                  pl.BlockSpec((SGU_CHUNK, 4), lambda b: (0, 0)),
                  pl.BlockSpec((8, SGU_WIDTH), lambda b: (0, 0))],
        out_specs=pl.BlockSpec((seq, SGU_WIDTH), lambda b: (b, 0)),
        out_shape=jax.ShapeDtypeStruct((nseq * seq, SGU_WIDTH), BF16),
        compiler_params=_params("parallel"),
    )(zs, ws, bst, ln)


def sgu_bwd(zs, ws, bst, ln, dout, nseq, seq, name):
    nc = seq // SGU_CHUNK

    def body(z_ref, ws_ref, bs_ref, ln_ref, do_ref, dz_ref, dws_ref, dbs_ref, dln_ref):
        @pl.when(pl.program_id(0) == 0)
        def _():
            dws_ref[...] = jnp.zeros(dws_ref.shape, F32)
            dbs_ref[...] = jnp.zeros(dbs_ref.shape, F32)
            dln_ref[...] = jnp.zeros(dln_ref.shape, F32)

        tril = _tril()

        def blk(n, carry):
            st = pl.multiple_of(n * SGU_CHUNK, SGU_CHUNK)
            zv = z_ref[pl.ds(st, SGU_CHUNK), :]
            ge = _gelu(zv)
            uu, vv = ge[:, :SGU_WIDTH], ge[:, SGU_WIDTH:]
            mu = jnp.mean(vv, axis=-1, keepdims=True)
            xc = vv - mu
            rstd = lax.rsqrt(jnp.mean(xc * xc, axis=-1, keepdims=True) + EPS)
            xh = xc * rstd
            vn = (xh * ln_ref[0:1, :] + ln_ref[1:2, :]).astype(BF16)
            dob = do_ref[pl.ds(st, SGU_CHUNK), :]
            du_cols, dvn_cols = [], []
            for g in range(4):
                lanes = slice(g * LANES, (g + 1) * LANES)
                wm = jnp.where(tril, ws_ref[g], 0.0).astype(BF16)
                mixed = _dot(wm, vn[:, lanes]) + bs_ref[:, g:g + 1]
                du_cols.append(dob[:, lanes] * mixed)
                dmix = dob[:, lanes] * uu[:, lanes]
                dbs_ref[g] += jnp.broadcast_to(jnp.sum(dmix, axis=-1, keepdims=True), (SGU_CHUNK, LANES))
                dmb = dmix.astype(BF16)
                dws_ref[g] += jnp.where(tril, _dot_nt(dmb, vn[:, lanes]), 0.0)
                dvn_cols.append(_dot_tn(wm, dmb))
            dvn = jnp.concatenate(dvn_cols, axis=-1)
            dln_ref[0:1, :] += jnp.sum(dvn * xh, axis=0, keepdims=True)
            dln_ref[1:2, :] += jnp.sum(dvn, axis=0, keepdims=True)
            dxh = dvn * ln_ref[0:1, :]
            dv = rstd * (dxh - jnp.mean(dxh, axis=-1, keepdims=True)
                         - xh * jnp.mean(dxh * xh, axis=-1, keepdims=True))
            dge = jnp.concatenate(du_cols + [dv], axis=-1)
            dz_ref[pl.ds(st, SGU_CHUNK), :] = (dge * _gelu_grad(zv)).astype(dz_ref.dtype)
            return carry

        lax.fori_loop(0, nc, blk, 0)

    w_spec = pl.BlockSpec((4, LANES, LANES), lambda b: (0, 0, 0))
    ln_spec = pl.BlockSpec((8, SGU_WIDTH), lambda b: (0, 0))
    return pl.pallas_call(
        body, name=name, grid=(nseq,),
        in_specs=[pl.BlockSpec((seq, 2 * SGU_WIDTH), lambda b: (b, 0)), w_spec,
                  pl.BlockSpec((SGU_CHUNK, 4), lambda b: (0, 0)), ln_spec,
                  pl.BlockSpec((seq, SGU_WIDTH), lambda b: (b, 0))],
        out_specs=[pl.BlockSpec((seq, 2 * SGU_WIDTH), lambda b: (b, 0)), w_spec, w_spec, ln_spec],
        out_shape=[jax.ShapeDtypeStruct((nseq * seq, 2 * SGU_WIDTH), BF16),
                   jax.ShapeDtypeStruct((4, LANES, LANES), F32),
                   jax.ShapeDtypeStruct((4, LANES, LANES), F32),
                   jax.ShapeDtypeStruct((8, SGU_WIDTH), F32)],
        compiler_params=_params("arbitrary"),
    )(zs, ws, bst, ln, dout)


def _ew_rows(rows, cols, nbuf):
    t = _row_tile(rows, 1024)
    while t > 8 and t * cols * 4 * nbuf * 2 > 24 * 2**20:
        t //= 2
    return t


def adamw(w, g, m, v, name):
    rows, cols = w.shape
    tr = _ew_rows(rows, cols, 7)

    def body(w_ref, g_ref, m_ref, v_ref, d_ref, mo_ref, vo_ref):
        gv = g_ref[...]
        mn = ADAM_B1 * m_ref[...] + (1.0 - ADAM_B1) * gv
        vn = ADAM_B2 * v_ref[...] + (1.0 - ADAM_B2) * (gv * gv)
        m_hat = mn / (1.0 - ADAM_B1 ** ADAM_STEP)
        v_hat = vn / (1.0 - ADAM_B2 ** ADAM_STEP)
        d_ref[...] = -ADAM_LR * (m_hat / (jnp.sqrt(v_hat) + ADAM_EPS) + ADAM_WD * w_ref[...])
        mo_ref[...] = mn
        vo_ref[...] = vn

    spec = pl.BlockSpec((tr, cols), lambda i: (i, 0))
    return pl.pallas_call(
        body, name=name, grid=(rows // tr,),
        in_specs=[spec] * 4, out_specs=[spec] * 3,
        out_shape=[jax.ShapeDtypeStruct((rows, cols), F32)] * 3,
        compiler_params=_params("parallel"),
    )(w, g, m, v)


def add_cast(a, b, name, dtype=BF16):
    nslab, rows, cols = a.shape
    tr = _ew_rows(rows, cols, 3)

    def body(a_ref, b_ref, o_ref):
        o_ref[...] = (a_ref[...] + b_ref[...]).astype(dtype)

    spec = pl.BlockSpec((1, tr, cols), lambda k, i: (k, i, 0))
    return pl.pallas_call(
        body, name=name, grid=(nslab, rows // tr),
        in_specs=[spec, spec], out_specs=spec,
        out_shape=jax.ShapeDtypeStruct(a.shape, dtype),
        compiler_params=_params("parallel", "parallel"),
    )(a, b)


def sum_parts(parts, name, first=None):
    npart, rows, cols = parts.shape
    tr = _ew_rows(rows, cols, npart + 2)

    def body(*refs):
        p_ref, o_ref = refs[-2], refs[-1]
        acc = p_ref[0].astype(F32) if first is None else refs[0][...].astype(F32) + p_ref[0].astype(F32)
        for j in range(1, npart):
            acc = acc + p_ref[j].astype(F32)
        o_ref[...] = acc

    row = pl.BlockSpec((tr, cols), lambda i: (i, 0))
    ins = [parts] if first is None else [first, parts]
    return pl.pallas_call(
        body, name=name, grid=(rows // tr,),
        in_specs=([] if first is None else [row]) + [pl.BlockSpec((npart, tr, cols), lambda i: (0, i, 0))],
        out_specs=row,
        out_shape=jax.ShapeDtypeStruct((rows, cols), F32),
        compiler_params=_params("parallel"),
    )(*ins)


ANY = pl.BlockSpec(memory_space=pl.ANY)
MESH = pl.DeviceIdType.MESH


def _me():
    return lax.axis_index("x"), lax.axis_index("y"), lax.axis_index("c")


def _flip(pos, rel):
    return tuple(1 - p if f else p for p, f in zip(pos, rel))


def _exchange(name, ins, out_shapes, plan):
    n_in, n_out = len(ins), len(out_shapes)

    def body(*refs):
        in_refs, out_refs = refs[:n_in], refs[n_in:n_in + n_out]
        send_sems, recv_sems, loc_sems = refs[n_in + n_out:]
        pos = _me()
        remote, local = plan(in_refs, out_refs, pos)
        loc = [pltpu.make_async_copy(s, d, loc_sems.at[i]) for i, (s, d) in enumerate(local)]
        for cp in loc:
            cp.start()
        rem = [pltpu.make_async_remote_copy(src_ref=s, dst_ref=d, send_sem=send_sems.at[i], recv_sem=recv_sems.at[i],
                                            device_id=_flip(pos, rel), device_id_type=MESH)
               for i, (rel, s, d) in enumerate(remote)]
        for cp in rem:
            cp.start()
        for cp in rem:
            cp.wait()
        for cp in loc:
            cp.wait()

    n_rem, n_loc = plan.counts
    return pl.pallas_call(
        body, name=name,
        in_specs=[ANY] * n_in, out_specs=[ANY] * n_out,
        out_shape=[jax.ShapeDtypeStruct(s, d) for s, d in out_shapes],
        scratch_shapes=[pltpu.SemaphoreType.DMA((n_rem,)), pltpu.SemaphoreType.DMA((n_rem,)),
                        pltpu.SemaphoreType.DMA((max(n_loc, 1),))],
    )(*ins)


SIBLING = (0, 0, 1)
OTHER_CHIPS = ((1, 0, 0), (0, 1, 0), (1, 1, 0))


def _chip_of(pos, rel=(0, 0, 0)):
    px, py, _ = _flip(pos, rel)
    return 2 * px + py


def allgather_blocks(shards, name):
    nt = len(shards)
    hs = [s.shape[0] // 2 for s in shards]

    def body(*refs):
        ins, outs = refs[:nt], refs[nt:2 * nt]
        send_sems, recv_sems, loc_sems = refs[2 * nt:]
        pos = _me()
        x, y, c = pos

        def block_id(rel):
            px, py, pc = _flip(pos, rel)
            return 4 * px + 2 * py + pc

        def copy(t, k, block_rel, to_rel, src=None):
            dst = outs[t].at[block_id(block_rel)]
            return pltpu.make_async_remote_copy(
                src_ref=dst if src is None else src, dst_ref=dst,
                send_sem=send_sems.at[t * 7 + k], recv_sem=recv_sems.at[t * 7 + k],
                device_id=_flip(pos, to_rel), device_id_type=MESH)

        own = [ins[t].at[pl.ds(c * hs[t], hs[t])] for t in range(nt)]
        mine = [pltpu.make_async_copy(own[t], outs[t].at[block_id((0, 0, 0))], loc_sems.at[t]) for t in range(nt)]
        for cp in mine:
            cp.start()
        first = []
        for t in range(nt):
            first.append(copy(t, 0, (0, 0, 0), SIBLING, src=own[t]))
            first += [copy(t, 1 + j, (0, 0, 0), rel, src=own[t]) for j, rel in enumerate(OTHER_CHIPS)]
        for cp in first:
            cp.start()
        passed = []
        for j, rel in enumerate(OTHER_CHIPS):
            for t in range(nt):
                copy(t, 1 + j, rel, (0, 0, 0)).wait_recv()
                fwd = copy(t, 4 + j, rel, SIBLING)
                fwd.start()
                passed.append(fwd)
        for t in range(nt):
            copy(t, 0, SIBLING, (0, 0, 0)).wait_recv()
            for j, rel in enumerate(OTHER_CHIPS):
                copy(t, 4 + j, (rel[0], rel[1], 1), (0, 0, 0)).wait_recv()
        for cp in first + passed:
            cp.wait_send()
        for cp in mine:
            cp.wait()

    return pl.pallas_call(
        body, name=name,
        in_specs=[ANY] * nt, out_specs=[ANY] * nt,
        out_shape=[jax.ShapeDtypeStruct((N_DEV, h, s.shape[1]), s.dtype) for h, s in zip(hs, shards)],
        scratch_shapes=[pltpu.SemaphoreType.DMA((7 * nt,)), pltpu.SemaphoreType.DMA((7 * nt,)),
                        pltpu.SemaphoreType.DMA((nt,))],
    )(*shards)


def chip_gather(block, name):
    def plan(ins, outs, pos):
        me = _chip_of(pos)
        remote = [(rel, ins[0], outs[0].at[me]) for rel in OTHER_CHIPS]
        return remote, [(ins[0], outs[0].at[me])]

    plan.counts = (3, 1)
    return _exchange(name, [block], [((N_CHIPS,) + block.shape, block.dtype)], plan)[0]


def sibling_swap(xs, name):
    def plan(ins, outs, pos):
        return [(SIBLING, i, o) for i, o in zip(ins, outs)], []

    plan.counts = (len(xs), 0)
    return _exchange(name, xs, [(v.shape, v.dtype) for v in xs], plan)


def chip_scatter(xs, shared, name):
    nx = len(xs)

    def plan(ins, outs, pos):
        me = _chip_of(pos)
        remote = []
        for i, o in zip(ins[:nx], outs[:nx]):
            remote += [(rel, i.at[_chip_of(pos, rel)], o.at[j]) for j, rel in enumerate(OTHER_CHIPS)]
        remote += [(rel, ins[nx], outs[nx].at[me]) for rel in OTHER_CHIPS]
        return remote, [(ins[nx], outs[nx].at[me])]

    plan.counts = (3 * nx + 3, 1)
    shapes = [((3,) + v.shape[1:], v.dtype) for v in xs] + [((N_CHIPS,) + shared.shape, shared.dtype)]
    res = _exchange(name, list(xs) + [shared], shapes, plan)
    return res[:nx], res[nx]


PACK_ROWS = 256


def _pack(arrs):
    parts, layout = [], []
    row = 0
    for a in arrs:
        flat = a.reshape(-1).astype(F32)
        size = flat.shape[0]
        rows = -(-size // (8 * LANES)) * 8
        flat = jnp.pad(flat, (0, rows * LANES - size))
        parts.append(flat.reshape(rows, LANES))
        layout.append((row, rows, size, a.shape))
        row += rows
    if row % PACK_ROWS:
        parts.append(jnp.zeros((PACK_ROWS - row % PACK_ROWS, LANES), F32))
    return jnp.concatenate(parts, axis=0), layout


def _unpack(packed, layout):
    return [packed[r0:r0 + rows].reshape(-1)[:size].reshape(shape) for r0, rows, size, shape in layout]


SMALL_REPL = ['mix_norm', 'a_b_in', 'a_sinks', 'a_conv_b', 'a_cln_g', 'a_cln_b', 'c_w_pool', 'c_w_s', 'c_b_s',
              'ffn_norm', 'final_norm']
SMALL_SHARD = ['a_conv_w', 'c_pool_scale', 'c_sln_g', 'c_sln_b']
BIG = ['a_w_in', 'a_w_out', 'c_w_in', 'c_w_out', 'ffn_w_gate', 'ffn_w_up', 'ffn_w_down']
BIG_ROW_SHARDED = {'a_w_out', 'c_w_out', 'ffn_w_down'}


def _shard_view(name, a):
    return a.reshape(-1, a.shape[-1])


def _from_shard_major(name, g8):
    _, h, cols = g8.shape
    g4 = g8.reshape(N_CHIPS, 2 * h, cols)
    layers = 2 if name.startswith('ffn') else 1
    g4 = g4.reshape(N_CHIPS, layers, (2 * h) // layers, cols)
    out = []
    for l in range(layers):
        blk = g4[:, l]
        if name in BIG_ROW_SHARDED:
            out.append(blk.reshape(-1, cols))
        else:
            out.append(jnp.transpose(blk, (1, 0, 2)).reshape(blk.shape[1], N_CHIPS * cols))
    return out


def _to_shard_major(name, fulls):
    per_layer = []
    for f in fulls:
        if name in BIG_ROW_SHARDED:
            per_layer.append(f.reshape(N_CHIPS, f.shape[0] // N_CHIPS, f.shape[1]))
        else:
            r, cfull = f.shape
            per_layer.append(jnp.transpose(f.reshape(r, N_CHIPS, cfull // N_CHIPS), (1, 0, 2)))
    t = jnp.stack(per_layer, axis=1)
    return t.reshape(N_CHIPS, -1, t.shape[-1])


def kernel(*args):
    a = dict(zip(IN_NAMES, args))
    bl, seq, _ = a['x'].shape
    n = bl * seq
    x = a['x'].reshape(n, D_MODEL)
    target = a['loss_target'].reshape(n, D_MODEL)
    xi, yi, ci = _me()
    chip = 2 * xi + yi

    gathered = allgather_blocks([_shard_view(k, a[k]).astype(BF16) for k in BIG], "gather_weights")
    full = {k: _from_shard_major(k, g) for k, g in zip(BIG, gathered)}
    a_w_in, a_w_out = full['a_w_in'][0], full['a_w_out'][0]
    c_w_in, c_w_out = full['c_w_in'][0], full['c_w_out'][0]

    small_shard_pack, small_shard_layout = _pack([a[k] for k in SMALL_SHARD])
    ss = chip_gather(small_shard_pack, "gather_small")
    ss_full = []
    for r0, rows, size, shape in small_shard_layout:
        per_chip = ss[:, r0:r0 + rows].reshape(N_CHIPS, -1)[:, :size].reshape((N_CHIPS,) + shape)
        ss_full.append(jnp.concatenate([per_chip[k] for k in range(N_CHIPS)], axis=-1))
    a_conv_w, c_pool_scale, c_sln_g, c_sln_b = [v[0] for v in ss_full]

    conv_taps = jnp.pad(a_conv_w, ((0, 32 - CONV_KERNEL), (0, 0)))
    conv_vec = jnp.pad(jnp.stack([a['a_conv_b'][0], a['a_cln_g'][0], a['a_cln_b'][0]]), ((0, 5), (0, 0)))
    sinks_b = jnp.pad(jnp.repeat(a['a_sinks'][0].reshape(N_KV_HEADS, GROUP), ATTN_BLOCK, axis=1), ((0, 6), (0, 0)))
    w_pool_bf = a['c_w_pool'][0].astype(BF16)
    pool_scale = c_pool_scale.reshape(1, POOL_WIDTH)
    w_s = a['c_w_s'][0]
    b_s_t = a['c_b_s'][0].T
    sgu_ln = jnp.pad(jnp.stack([c_sln_g, c_sln_b]), ((0, 6), (0, 0)))
    mix_norm, ffn_norm = a['mix_norm'], a['ffn_norm']
    final_norm = a['final_norm'].reshape(1, D_MODEL)

    hn0, q, kv, cc = norm_inproj(
        x, mix_norm[0:1], a_w_in, a['a_b_in'],
        [(0, ATTN_WIDTH), (ATTN_WIDTH, ATTN_WIDTH + 2 * KV_WIDTH), (ATTN_WIDTH + 2 * KV_WIDTH, a_w_in.shape[1])],
        [BF16, BF16, F32], "in_proj0")
    attn = attn_fwd(q, kv, sinks_b, bl, seq, "attn_fwd")
    conv = conv_fwd(cc, conv_taps, conv_vec, bl, seq, "conv_fwd")
    h1 = out_proj(x, attn, conv, a_w_out, "out_proj0")
    hnf0, g0, u0 = ffn_gate_up(h1, ffn_norm[0:1], full['ffn_w_gate'][0], full['ffn_w_up'][0], "ffn_gate_up0")
    h2 = ffn_down(h1, g0, u0, full['ffn_w_down'][0], "ffn_down0")

    hn1, zp, zs = norm_inproj(
        h2, mix_norm[1:2], c_w_in, jnp.zeros((1, c_w_in.shape[1]), F32),
        [(0, POOL_WIDTH), (POOL_WIDTH, c_w_in.shape[1])], [F32, F32], "in_proj1")
    pool = pool_fwd(zp, w_pool_bf, pool_scale, bl, seq, "pool_fwd")
    sgu = sgu_fwd(zs, w_s, b_s_t, sgu_ln, bl, seq, "sgu_fwd")
    h3 = out_proj(h2, pool, sgu, c_w_out, "out_proj1")
    hnf1, g1, u1 = ffn_gate_up(h3, ffn_norm[1:2], full['ffn_w_gate'][1], full['ffn_w_up'][1], "ffn_gate_up1")
    h4 = ffn_down(h3, g1, u1, full['ffn_w_down'][1], "ffn_down1")

    dh4, d_final_norm, loss_local = loss_head(h4, final_norm, target, "loss_head")
    loss = lax.psum(loss_local, ("x", "y", "c"))

    grads = {}

    def ffn_bwd(layer, dh_out, h_in, hnf, g, u, tag):
        wg, wu, wd = full['ffn_w_gate'][layer], full['ffn_w_up'][layer], full['ffn_w_down'][layer]
        dg, du, act = ffn_down_bwd(dh_out, g, u, wd, "ffn_down_bwd" + tag)
        d_wd = mm_tn(act, dh_out, "dw_down" + tag)
        d_wg = mm_tn(hnf, dg, "dw_gate" + tag)
        d_wu = mm_tn(hnf, du, "dw_up" + tag)
        dh_in, d_gain = proj_rms_bwd([dg, du], [wg, wu], h_in, ffn_norm[layer:layer + 1], dh_out, 1,
                                     "ffn_up_bwd" + tag, tm_pref=256)
        return dh_in, d_gain, d_wg, d_wu, d_wd

    dh3, d_ffn_norm1, d_wg1, d_wu1, d_wd1 = ffn_bwd(1, dh4, h3, hnf1, g1, u1, "1")

    d_pool, d_sgu = out_proj_bwd(dh3, c_w_out, [F32, F32], "out_proj_bwd1")
    d_c_w_out = jnp.concatenate([mm_tn(pool, dh3, "dw_out1_pool"), mm_tn(sgu, dh3, "dw_out1_sgu")], axis=0)
    dzp, d_w_pool, d_pool_scale = pool_bwd(zp, w_pool_bf, pool_scale, d_pool, bl, seq, "pool_bwd")
    dzs, d_w_s, d_b_s_b, d_sgu_ln = sgu_bwd(zs, w_s, b_s_t, sgu_ln, d_sgu, bl, seq, "sgu_bwd")
    d_c_w_in = jnp.concatenate([mm_tn(hn1, dzp, "dw_in1_pool"), mm_tn(hn1, dzs, "dw_in1_sgu")], axis=1)
    dh2, d_mix_norm1 = proj_rms_bwd([dzp, dzs], [c_w_in[:, :POOL_WIDTH], c_w_in[:, POOL_WIDTH:]], h2,
                                    mix_norm[1:2], dh3, 1, "in_proj_bwd1")

    dh1, d_ffn_norm0, d_wg0, d_wu0, d_wd0 = ffn_bwd(0, dh2, h1, hnf0, g0, u0, "0")

    d_attn, d_conv = out_proj_bwd(dh1, a_w_out, [BF16, F32], "out_proj_bwd0")
    d_a_w_out = jnp.concatenate([mm_tn(attn, dh1, "dw_out0_attn"), mm_tn(conv, dh1, "dw_out0_conv")], axis=0)
    dq, dkv, d_sinks_b = attn_bwd(q, kv, sinks_b, d_attn, bl, seq, "attn_bwd")
    dcc, d_conv_taps, d_conv_vec = conv_bwd(cc, conv_taps, conv_vec, d_conv, bl, seq, "conv_bwd")
    dw_q, db_q = mm_tn(hn0, dq, "dw_in0_q", colsum=True)
    dw_kv, db_kv = mm_tn(hn0, dkv, "dw_in0_kv", colsum=True)
    dw_c, db_c = mm_tn(hn0, dcc, "dw_in0_c", colsum=True)
    d_a_w_in = jnp.concatenate([dw_q, dw_kv, dw_c], axis=1)
    d_a_b_in = jnp.concatenate([db_q, db_kv, db_c], axis=0)
    kq, kk = ATTN_WIDTH, ATTN_WIDTH + 2 * KV_WIDTH
    grad_x, d_mix_norm0 = proj_rms_bwd([dq, dkv, dcc], [a_w_in[:, :kq], a_w_in[:, kq:kk], a_w_in[:, kk:]], x,
                                       mix_norm[0:1], dh1, 1, "in_proj_bwd0")

    big_full = {'a_w_in': [d_a_w_in], 'a_w_out': [d_a_w_out], 'c_w_in': [d_c_w_in], 'c_w_out': [d_c_w_out],
                'ffn_w_gate': [d_wg0, d_wg1], 'ffn_w_up': [d_wu0, d_wu1], 'ffn_w_down': [d_wd0, d_wd1]}
    keep, give = [], []
    for k in BIG:
        t = _to_shard_major(k, big_full[k])
        h = t.shape[1] // 2
        keep.append(lax.dynamic_slice_in_dim(t, ci * h, h, axis=1))
        give.append(lax.dynamic_slice_in_dim(t, (1 - ci) * h, h, axis=1))
    small_full = {
        'mix_norm': jnp.stack([d_mix_norm0, d_mix_norm1]), 'a_b_in': d_a_b_in[None], 'a_sinks': d_sinks_b[:, 0][None],
        'a_conv_w': d_conv_taps[:CONV_KERNEL][None], 'a_conv_b': d_conv_vec[0][None], 'a_cln_g': d_conv_vec[1][None],
        'a_cln_b': d_conv_vec[2][None], 'c_w_pool': d_w_pool[None], 'c_pool_scale': d_pool_scale[0][None],
        'c_sln_g': d_sgu_ln[0][None], 'c_sln_b': d_sgu_ln[1][None], 'c_w_s': d_w_s[None],
        'c_b_s': d_b_s_b[:, :, 0][None], 'ffn_norm': jnp.stack([d_ffn_norm0, d_ffn_norm1]),
        'final_norm': d_final_norm}
    small_names = SMALL_REPL + SMALL_SHARD
    small_pack, small_layout = _pack([small_full[k] for k in small_names])

    got = sibling_swap(give + [small_pack], "reduce_pair")
    pair_sums = [add_cast(kp, gt, "pair_sum_" + k) for k, kp, gt in zip(BIG, keep, got)]
    small_pair = add_cast(small_pack[None], got[-1][None], "pair_sum_small", dtype=F32)[0]
    from_chips, small_chips = chip_scatter(pair_sums, small_pair, "reduce_chips")
    own = [lax.dynamic_index_in_dim(p, chip, axis=0, keepdims=False) for p in pair_sums]
    halves = [sum_parts(p, "chip_sum_" + k, first=o) for k, p, o in zip(BIG, from_chips, own)]
    others = sibling_swap(halves, "reduce_join")
    for k, mine, theirs in zip(BIG, halves, others):
        low = jnp.where(ci == 0, mine, theirs)
        high = jnp.where(ci == 0, theirs, mine)
        grads[k] = jnp.concatenate([low, high], axis=0).reshape(a[k].shape)

    small_sum = sum_parts(small_chips, "small_sum")
    for k, g in zip(small_names, _unpack(small_sum, small_layout)):
        if k in SMALL_SHARD:
            width = a[k].shape[-1]
            g = lax.dynamic_slice_in_dim(g, chip * width, width, axis=g.ndim - 1)
        grads[k] = g

    delta, new_m, new_v = {}, {}, {}
    for k in BIG:
        shp = a[k].shape
        v2 = lambda t: t.reshape(-1, shp[-1])
        d, m, v = adamw(v2(a[k]), v2(grads[k]), v2(a['m_' + k]), v2(a['v_' + k]), "adamw_" + k)
        delta[k], new_m[k], new_v[k] = d.reshape(shp), m.reshape(shp), v.reshape(shp)
    packs = [_pack([src[k] for k in small_names])
             for src in (a, grads, {k: a['m_' + k] for k in small_names}, {k: a['v_' + k] for k in small_names})]
    d, m, v = adamw(packs[0][0], packs[1][0], packs[2][0], packs[3][0], "adamw_small")
    lay = packs[0][1]
    for k, dv, mv, vv in zip(small_names, _unpack(d, lay), _unpack(m, lay), _unpack(v, lay)):
        delta[k], new_m[k], new_v[k] = dv, mv, vv

    return (loss, grad_x.reshape(a['x'].shape), *[grads[k] for k in WEIGHTS], *[delta[k] for k in WEIGHTS],
            *[new_m[k] for k in WEIGHTS], *[new_v[k] for k in WEIGHTS])
```

```python
import functools

import jax
import jax.numpy as jnp
from jax import lax
from jax.experimental import pallas as pl
from jax.experimental.pallas import tpu as pltpu

F32 = jnp.float32
BF16 = jnp.bfloat16

D_MODEL = 1024
EPS = 1e-5
N_Q_HEADS, N_KV_HEADS, HEAD_DIM = 8, 2, 64
ATTN_BLOCK = 128
ATTN_WIDTH = N_Q_HEADS * HEAD_DIM
KV_WIDTH = N_KV_HEADS * HEAD_DIM
CONV_WIDTH = 512
CONV_KERNEL = 31
CONV_HALO = 32
POOL_WINDOWS = (2, 4, 8, 16)
POOL_WIDTH = 512
POOL_HALO = 16
SGU_WIDTH = 512
SGU_CHUNK = 128
D_FF = 2816
FF_CHUNK = 128
LANES = 128
N_CHIPS = 4
N_DEV = 8

ADAM_LR, ADAM_B1, ADAM_B2, ADAM_EPS, ADAM_WD, ADAM_STEP = 0.001, 0.9, 0.999, 1e-08, 0.01, 10

VMEM_LIMIT = 56 * 2**20

WEIGHTS = ['mix_norm', 'a_w_in', 'a_b_in', 'a_sinks', 'a_conv_w', 'a_conv_b', 'a_cln_g', 'a_cln_b', 'a_w_out',
           'c_w_in', 'c_w_pool', 'c_pool_scale', 'c_sln_g', 'c_sln_b', 'c_w_s', 'c_b_s', 'c_w_out',
           'ffn_norm', 'ffn_w_gate', 'ffn_w_up', 'ffn_w_down', 'final_norm']
IN_NAMES = (['x'] + WEIGHTS + ['loss_target'] + ['m_' + n for n in WEIGHTS] + ['v_' + n for n in WEIGHTS])


def _params(*sem):
    return pltpu.CompilerParams(dimension_semantics=sem, vmem_limit_bytes=VMEM_LIMIT)


def _dot(a, b):
    return jnp.dot(a, b, preferred_element_type=F32)


def _dot_nt(a, b):
    return lax.dot_general(a, b, (((1,), (1,)), ((), ())), preferred_element_type=F32)


def _dot_tn(a, b):
    return lax.dot_general(a, b, (((0,), (0,)), ((), ())), preferred_element_type=F32)


def _sigmoid(v):
    return 1.0 / (1.0 + jnp.exp(-v))


def _row_tile(n, pref):
    t = min(n, pref)
    while n % t:
        t //= 2
    return t


def _col_tile(m, rows, budget=6 * 2**20):
    best = LANES
    for t in range(LANES, m + 1, LANES):
        if m % t == 0 and rows * t * 4 <= budget:
            best = t
    return best


class Comm:
    def __init__(self, ins, out_shapes, plan, count, aliases=None):
        self.ins, self.out_shapes, self.plan, self.count, self.aliases = ins, out_shapes, plan, count, aliases or {}


def _pcall(body, name, grid, in_specs, out_specs, out_shape, scratch_shapes, args, sem, comm=None):
    single = not isinstance(out_shape, (list, tuple))
    if single:
        out_specs, out_shape = [out_specs], [out_shape]
    if comm is None:
        res = pl.pallas_call(body, name=name, grid=grid, in_specs=in_specs, out_specs=list(out_specs),
                             out_shape=list(out_shape), scratch_shapes=list(scratch_shapes),
                             compiler_params=_params(*sem))(*args)
        return (res[0] if single else res), []
    na, nci, no, nco, ns = len(args), len(comm.ins), len(out_shape), len(comm.out_shapes), len(scratch_shapes)

    def wrapped(*refs):
        a_refs, ci_refs = refs[:na], refs[na:na + nci]
        o_refs, co_refs = refs[na + nci:na + nci + no], refs[na + nci + no:na + nci + no + nco]
        s_refs = refs[na + nci + no + nco:na + nci + no + nco + ns]
        send_sems, recv_sems = refs[-2], refs[-1]
        pos = _me()

        def copies():
            return [pltpu.make_async_remote_copy(src_ref=s, dst_ref=d, send_sem=send_sems.at[i],
                                                 recv_sem=recv_sems.at[i], device_id=_flip(pos, rel),
                                                 device_id_type=MESH)
                    for i, (rel, s, d) in enumerate(comm.plan(ci_refs, co_refs, pos))]

        first, last = None, None
        for d, size in enumerate(grid):
            f, l = pl.program_id(d) == 0, pl.program_id(d) == size - 1
            first = f if first is None else first & f
            last = l if last is None else last & l

        @pl.when(first)
        def _():
            for cp in copies():
                cp.start()

        body(*a_refs, *o_refs, *s_refs)

        @pl.when(last)
        def _():
            for cp in copies():
                cp.wait()

    res = pl.pallas_call(
        wrapped, name=name, grid=grid,
        in_specs=list(in_specs) + [ANY] * nci, out_specs=list(out_specs) + [ANY] * nco,
        out_shape=list(out_shape) + [jax.ShapeDtypeStruct(s, d) for s, d in comm.out_shapes],
        scratch_shapes=list(scratch_shapes) + [pltpu.SemaphoreType.DMA((comm.count,)),
                                               pltpu.SemaphoreType.DMA((comm.count,))],
        input_output_aliases={na + i: no + o for i, o in comm.aliases.items()},
        compiler_params=_params(*(["arbitrary"] * len(grid))),
    )(*args, *comm.ins)
    outs = res[:no]
    return (outs[0] if single else outs), list(res[no:])


def norm_inproj(x, gain, w, bias, splits, dtypes, name):
    n = x.shape[0]
    m = w.shape[1]
    tm = _row_tile(n, 512)

    def body(x_ref, g_ref, w_ref, b_ref, hn_ref, *outs):
        xv = x_ref[...]
        r = lax.rsqrt(jnp.mean(xv * xv, axis=-1, keepdims=True) + EPS)
        hn = ((xv * r) * g_ref[...]).astype(BF16)
        hn_ref[...] = hn
        z = _dot(hn, w_ref[...]) + b_ref[...]
        for o, (lo, hi) in zip(outs, splits):
            o[...] = z[:, lo:hi].astype(o.dtype)

    out_shape = [jax.ShapeDtypeStruct((n, D_MODEL), BF16)]
    out_specs = [pl.BlockSpec((tm, D_MODEL), lambda i: (i, 0))]
    for (lo, hi), dt in zip(splits, dtypes):
        out_shape.append(jax.ShapeDtypeStruct((n, hi - lo), dt))
        out_specs.append(pl.BlockSpec((tm, hi - lo), lambda i: (i, 0)))
    return pl.pallas_call(
        body, name=name, grid=(n // tm,),
        in_specs=[pl.BlockSpec((tm, D_MODEL), lambda i: (i, 0)),
                  pl.BlockSpec((1, D_MODEL), lambda i: (0, 0)),
                  pl.BlockSpec((D_MODEL, m), lambda i: (0, 0)),
                  pl.BlockSpec((1, m), lambda i: (0, 0))],
        out_specs=out_specs, out_shape=out_shape,
        compiler_params=_params("parallel"),
    )(x, gain, w, bias)


def out_proj(res, m1, m2, w, name):
    n = res.shape[0]
    k1, k2 = m1.shape[1], m2.shape[1]
    assert k1 == k2
    tm = _row_tile(n, 512)

    def body(r_ref, a_ref, b_ref, w1_ref, w2_ref, o_ref):
        o_ref[...] = r_ref[...] + _dot(a_ref[...], w1_ref[...]) + _dot(b_ref[...], w2_ref[...])

    return pl.pallas_call(
        body, name=name, grid=(n // tm,),
        in_specs=[pl.BlockSpec((tm, D_MODEL), lambda i: (i, 0)),
                  pl.BlockSpec((tm, k1), lambda i: (i, 0)),
                  pl.BlockSpec((tm, k2), lambda i: (i, 0)),
                  pl.BlockSpec((k1, D_MODEL), lambda i: (0, 0)),
                  pl.BlockSpec((k2, D_MODEL), lambda i: (1, 0))],
        out_specs=pl.BlockSpec((tm, D_MODEL), lambda i: (i, 0)),
        out_shape=jax.ShapeDtypeStruct((n, D_MODEL), F32),
        compiler_params=_params("parallel"),
    )(res, m1, m2, w, w)


def ffn_gate_up(h, gain, wg, wu, name, comm=None):
    n = h.shape[0]
    tm = _row_tile(n, 1024)
    th = D_FF // 2

    def body(h_ref, g_ref, wg_ref, wu_ref, hn_ref, go_ref, uo_ref):
        @pl.when(pl.program_id(1) == 0)
        def _():
            xv = h_ref[...]
            r = lax.rsqrt(jnp.mean(xv * xv, axis=-1, keepdims=True) + EPS)
            hn_ref[...] = ((xv * r) * g_ref[...]).astype(BF16)

        hn = hn_ref[...]
        go_ref[...] = _dot(hn, wg_ref[...]).astype(BF16)
        uo_ref[...] = _dot(hn, wu_ref[...]).astype(BF16)

    return _pcall(
        body, name, (n // tm, D_FF // th),
        [pl.BlockSpec((tm, D_MODEL), lambda i, j: (i, 0)),
         pl.BlockSpec((1, D_MODEL), lambda i, j: (0, 0)),
         pl.BlockSpec((D_MODEL, th), lambda i, j: (0, j)),
         pl.BlockSpec((D_MODEL, th), lambda i, j: (0, j))],
        [pl.BlockSpec((tm, D_MODEL), lambda i, j: (i, 0)),
         pl.BlockSpec((tm, th), lambda i, j: (i, j)),
         pl.BlockSpec((tm, th), lambda i, j: (i, j))],
        [jax.ShapeDtypeStruct((n, D_MODEL), BF16),
         jax.ShapeDtypeStruct((n, D_FF), BF16),
         jax.ShapeDtypeStruct((n, D_FF), BF16)],
        [], (h, gain, wg, wu), ("parallel", "arbitrary"), comm)


def ffn_down(h, g, u, wd, name, comm=None):
    n = h.shape[0]
    tm = _row_tile(n, 512)

    def body(h_ref, g_ref, u_ref, w_ref, o_ref, a_ref):
        for c0 in range(0, D_FF, FF_CHUNK):
            gv = g_ref[:, c0:c0 + FF_CHUNK].astype(F32)
            a_ref[:, c0:c0 + FF_CHUNK] = (gv * _sigmoid(gv) * u_ref[:, c0:c0 + FF_CHUNK].astype(F32)).astype(BF16)
        o_ref[...] = h_ref[...] + _dot(a_ref[...], w_ref[...])

    return _pcall(
        body, name, (n // tm,),
        [pl.BlockSpec((tm, D_MODEL), lambda i: (i, 0)),
         pl.BlockSpec((tm, D_FF), lambda i: (i, 0)),
         pl.BlockSpec((tm, D_FF), lambda i: (i, 0)),
         pl.BlockSpec((D_FF, D_MODEL), lambda i: (0, 0))],
        pl.BlockSpec((tm, D_MODEL), lambda i: (i, 0)),
        jax.ShapeDtypeStruct((n, D_MODEL), F32),
        [pltpu.VMEM((tm, D_FF), BF16)], (h, g, u, wd), ("parallel",), comm)


def ffn_down_bwd(dh, g, u, wd, name):
    n = dh.shape[0]
    tm = _row_tile(n, 512)
    th = D_FF // 2

    def body(dh_ref, g_ref, u_ref, w_ref, dg_ref, du_ref, a_ref, da_ref):
        da_ref[...] = _dot_nt(dh_ref[...].astype(BF16), w_ref[...])
        for c0 in range(0, th, FF_CHUNK):
            cols = slice(c0, c0 + FF_CHUNK)
            da = da_ref[:, cols]
            gv = g_ref[:, cols].astype(F32)
            uv = u_ref[:, cols].astype(F32)
            sg = _sigmoid(gv)
            act = gv * sg
            dg_ref[:, cols] = (da * uv * (sg * (1.0 + gv * (1.0 - sg)))).astype(BF16)
            du_ref[:, cols] = (da * act).astype(BF16)
            a_ref[:, cols] = (act * uv).astype(BF16)

    spec_h = pl.BlockSpec((tm, th), lambda i, j: (i, j))
    return pl.pallas_call(
        body, name=name, grid=(n // tm, D_FF // th),
        in_specs=[pl.BlockSpec((tm, D_MODEL), lambda i, j: (i, 0)), spec_h, spec_h,
                  pl.BlockSpec((th, D_MODEL), lambda i, j: (j, 0))],
        out_specs=[spec_h, spec_h, spec_h],
        out_shape=[jax.ShapeDtypeStruct((n, D_FF), BF16)] * 3,
        scratch_shapes=[pltpu.VMEM((tm, th), F32)],
        compiler_params=_params("parallel", "arbitrary"),
    )(dh, g, u, wd)


def mm_tn(x, dy, name, colsum=False):
    n, k = x.shape
    m = dy.shape[1]
    tn = _col_tile(m, k)
    tt = _row_tile(n, 1024)

    def body(x_ref, dy_ref, o_ref, *rest):
        t = pl.program_id(1)
        dyv = dy_ref[...]
        part = _dot_tn(x_ref[...], dyv.astype(BF16))

        @pl.when(t == 0)
        def _():
            o_ref[...] = part

        @pl.when(t > 0)
        def _():
            o_ref[...] += part

        if colsum:
            cs = jnp.sum(dyv.astype(F32), axis=0, keepdims=True)

            @pl.when(t == 0)
            def _():
                rest[0][...] = jnp.broadcast_to(cs, rest[0].shape)

            @pl.when(t > 0)
            def _():
                rest[0][...] += jnp.broadcast_to(cs, rest[0].shape)

    out_shape = [jax.ShapeDtypeStruct((k, m), F32)]
    out_specs = [pl.BlockSpec((k, tn), lambda j, t: (0, j))]
    if colsum:
        out_shape.append(jax.ShapeDtypeStruct((8, m), F32))
        out_specs.append(pl.BlockSpec((8, tn), lambda j, t: (0, j)))
    res = pl.pallas_call(
        body, name=name, grid=(m // tn, n // tt),
        in_specs=[pl.BlockSpec((tt, k), lambda j, t: (t, 0)),
                  pl.BlockSpec((tt, tn), lambda j, t: (t, j))],
        out_specs=out_specs, out_shape=out_shape,
        compiler_params=_params("parallel", "arbitrary"),
    )(x, dy)
    return (res[0], res[1][0]) if colsum else res[0]


def out_proj_bwd(dh, w, dtypes, name):
    n = dh.shape[0]
    k = w.shape[0]
    half = k // 2
    tm = _row_tile(n, 512)

    def body(dh_ref, w_ref, a_ref, b_ref):
        dm = _dot_nt(dh_ref[...].astype(BF16), w_ref[...])
        a_ref[...] = dm[:, :half].astype(a_ref.dtype)
        b_ref[...] = dm[:, half:].astype(b_ref.dtype)

    return pl.pallas_call(
        body, name=name, grid=(n // tm,),
        in_specs=[pl.BlockSpec((tm, D_MODEL), lambda i: (i, 0)),
                  pl.BlockSpec((k, D_MODEL), lambda i: (0, 0))],
        out_specs=[pl.BlockSpec((tm, half), lambda i: (i, 0))] * 2,
        out_shape=[jax.ShapeDtypeStruct((n, half), dtypes[0]), jax.ShapeDtypeStruct((n, half), dtypes[1])],
        compiler_params=_params("parallel"),
    )(dh, w)


def proj_rms_bwd(dys, ws, h_in, gain, dres, nk, name, tm_pref=512):
    n = h_in.shape[0]
    npair = len(dys)
    tm = _row_tile(n, tm_pref)
    tks = [dy.shape[1] // nk for dy in dys]

    def body(*refs):
        dy_refs = refs[:npair]
        w_refs = refs[npair:2 * npair]
        h_ref, g_ref, dr_ref, o_ref, dg_ref, acc_ref = refs[2 * npair:]
        i, k = pl.program_id(0), pl.program_id(1)
        part = _dot_nt(dy_refs[0][...], w_refs[0][...])
        for p in range(1, npair):
            part = part + _dot_nt(dy_refs[p][...], w_refs[p][...])

        @pl.when(k == 0)
        def _():
            acc_ref[...] = part

        @pl.when(k > 0)
        def _():
            acc_ref[...] += part

        @pl.when(k == nk - 1)
        def _():
            dhn = acc_ref[...]
            xv = h_ref[...]
            r = lax.rsqrt(jnp.mean(xv * xv, axis=-1, keepdims=True) + EPS)
            xh = xv * r
            uv = dhn * g_ref[...]
            o_ref[...] = dr_ref[...] + r * (uv - xh * jnp.mean(uv * xh, axis=-1, keepdims=True))
            dgp = jnp.broadcast_to(jnp.sum(dhn * xh, axis=0, keepdims=True), dg_ref.shape)

            @pl.when(i == 0)
            def _():
                dg_ref[...] = dgp

            @pl.when(i > 0)
            def _():
                dg_ref[...] += dgp

    row = pl.BlockSpec((tm, D_MODEL), lambda i, k: (i, 0))
    in_specs = [pl.BlockSpec((tm, tk), lambda i, k: (i, k)) for tk in tks]
    in_specs += [pl.BlockSpec((D_MODEL, tk), lambda i, k: (0, k)) for tk in tks]
    in_specs += [row, pl.BlockSpec((1, D_MODEL), lambda i, k: (0, 0)), row]
    dh, dgain = pl.pallas_call(
        body, name=name, grid=(n // tm, nk),
        in_specs=in_specs,
        out_specs=[row, pl.BlockSpec((8, D_MODEL), lambda i, k: (0, 0))],
        out_shape=[jax.ShapeDtypeStruct((n, D_MODEL), F32), jax.ShapeDtypeStruct((8, D_MODEL), F32)],
        scratch_shapes=[pltpu.VMEM((tm, D_MODEL), F32)],
        compiler_params=_params("arbitrary", "arbitrary"),
    )(*dys, *ws, h_in, gain, dres)
    return dh, dgain[0]


def loss_head(h, gain, target, name):
    n = h.shape[0]
    tm = _row_tile(n, 512)

    def body(h_ref, g_ref, t_ref, dh_ref, dg_ref, l_ref):
        i = pl.program_id(0)
        xv = h_ref[...]
        r = lax.rsqrt(jnp.mean(xv * xv, axis=-1, keepdims=True) + EPS)
        xh = xv * r
        err = xh * g_ref[...] - t_ref[...]
        dy = err * (1.0 / D_MODEL)
        uv = dy * g_ref[...]
        dh_ref[...] = r * (uv - xh * jnp.mean(uv * xh, axis=-1, keepdims=True))
        dgp = jnp.broadcast_to(jnp.sum(dy * xh, axis=0, keepdims=True), dg_ref.shape)
        lp = jnp.sum(jnp.sum(err * err, axis=-1, keepdims=True), axis=0, keepdims=True) * (0.5 / D_MODEL)
        lp = jnp.broadcast_to(lp, l_ref.shape)

        @pl.when(i == 0)
        def _():
            dg_ref[...] = dgp
            l_ref[...] = lp

        @pl.when(i > 0)
        def _():
            dg_ref[...] += dgp
            l_ref[...] += lp

    row = pl.BlockSpec((tm, D_MODEL), lambda i: (i, 0))
    dh, dg, l = pl.pallas_call(
        body, name=name, grid=(n // tm,),
        in_specs=[row, pl.BlockSpec((1, D_MODEL), lambda i: (0, 0)), row],
        out_specs=[row, pl.BlockSpec((8, D_MODEL), lambda i: (0, 0)), pl.BlockSpec((8, LANES), lambda i: (0, 0))],
        out_shape=[jax.ShapeDtypeStruct((n, D_MODEL), F32), jax.ShapeDtypeStruct((8, D_MODEL), F32),
                   jax.ShapeDtypeStruct((8, LANES), F32)],
        compiler_params=_params("arbitrary"),
    )(h, gain, target)
    return dh, dg[0], l[0, 0]


GROUP = N_Q_HEADS // N_KV_HEADS
GQ = GROUP * ATTN_BLOCK


def _attn_mask_t(n):
    r = lax.broadcasted_iota(jnp.int32, (2 * ATTN_BLOCK, GQ), 0)
    qi = lax.broadcasted_iota(jnp.int32, (2 * ATTN_BLOCK, GQ), 1) & (ATTN_BLOCK - 1)
    band = (r > qi) & (r <= qi + ATTN_BLOCK)
    return band & ((r >= ATTN_BLOCK) | (n > 0))


def _stack_heads(blk, kh):
    return jnp.concatenate([blk[:, (kh * GROUP + g) * HEAD_DIM:(kh * GROUP + g + 1) * HEAD_DIM]
                            for g in range(GROUP)], axis=0)


def _attn_probs_t(kk, qs, mask, sink):
    s = _dot_nt(kk, qs) * (HEAD_DIM ** -0.5)
    s = jnp.where(mask, s, -1e30)
    m = jnp.maximum(jnp.max(s, axis=0, keepdims=True), sink)
    p = jnp.exp(s - m)
    esink = jnp.exp(sink - m)
    inv = 1.0 / (jnp.sum(p, axis=0, keepdims=True) + esink)
    return p * inv, esink * inv


def attn_fwd(q, kv, sinks_t, nseq, seq, name, comm=None):
    nb = seq // ATTN_BLOCK

    def body(q_ref, kv_ref, s_ref, o_ref, kvp):
        kvp[0:ATTN_BLOCK, :] = jnp.zeros((ATTN_BLOCK, 2 * KV_WIDTH), BF16)
        kvp[ATTN_BLOCK:, :] = kv_ref[...]

        def blk(n, carry):
            st = pl.multiple_of(n * ATTN_BLOCK, ATTN_BLOCK)
            qb = q_ref[pl.ds(st, ATTN_BLOCK), :]
            kw = kvp[pl.ds(st, 2 * ATTN_BLOCK), :]
            mask = _attn_mask_t(n)
            for kh in range(N_KV_HEADS):
                kk = kw[:, kh * HEAD_DIM:(kh + 1) * HEAD_DIM]
                vv = kw[:, KV_WIDTH + kh * HEAD_DIM:KV_WIDTH + (kh + 1) * HEAD_DIM]
                probs, _ = _attn_probs_t(kk, _stack_heads(qb, kh), mask, s_ref[kh:kh + 1, :])
                ot = _dot_tn(vv, probs.astype(BF16))
                for pair in range(GROUP // 2):
                    two = jnp.concatenate([ot[:, (2 * pair) * ATTN_BLOCK:(2 * pair + 1) * ATTN_BLOCK],
                                           ot[:, (2 * pair + 1) * ATTN_BLOCK:(2 * pair + 2) * ATTN_BLOCK]], axis=0)
                    col = (kh * GROUP + 2 * pair) * HEAD_DIM
                    o_ref[pl.ds(st, ATTN_BLOCK), col:col + 2 * HEAD_DIM] = two.T.astype(o_ref.dtype)
            return carry

        lax.fori_loop(0, nb, blk, 0)

    return _pcall(
        body, name, (nseq,),
        [pl.BlockSpec((seq, ATTN_WIDTH), lambda b: (b, 0)),
         pl.BlockSpec((seq, 2 * KV_WIDTH), lambda b: (b, 0)),
         pl.BlockSpec((8, GQ), lambda b: (0, 0))],
        pl.BlockSpec((seq, ATTN_WIDTH), lambda b: (b, 0)),
        jax.ShapeDtypeStruct((nseq * seq, ATTN_WIDTH), BF16),
        [pltpu.VMEM((ATTN_BLOCK + seq, 2 * KV_WIDTH), BF16)], (q, kv, sinks_t), ("parallel",), comm)


def attn_bwd(q, kv, sinks_t, do, nseq, seq, name):
    nb = seq // ATTN_BLOCK

    def body(q_ref, kv_ref, s_ref, do_ref, dq_ref, dkv_ref, ds_ref, kvp, dkvp, dsacc):
        @pl.when(pl.program_id(0) == 0)
        def _():
            dsacc[...] = jnp.zeros(dsacc.shape, F32)

        kvp[0:ATTN_BLOCK, :] = jnp.zeros((ATTN_BLOCK, 2 * KV_WIDTH), BF16)
        kvp[ATTN_BLOCK:, :] = kv_ref[...]
        dkvp[...] = jnp.zeros(dkvp.shape, F32)

        def blk(n, carry):
            st = pl.multiple_of(n * ATTN_BLOCK, ATTN_BLOCK)
            qb = q_ref[pl.ds(st, ATTN_BLOCK), :]
            dob = do_ref[pl.ds(st, ATTN_BLOCK), :]
            kw = kvp[pl.ds(st, 2 * ATTN_BLOCK), :]
            mask = _attn_mask_t(n)
            for kh in range(N_KV_HEADS):
                kk = kw[:, kh * HEAD_DIM:(kh + 1) * HEAD_DIM]
                vv = kw[:, KV_WIDTH + kh * HEAD_DIM:KV_WIDTH + (kh + 1) * HEAD_DIM]
                qs = _stack_heads(qb, kh)
                dos = _stack_heads(dob, kh)
                probs, psink = _attn_probs_t(kk, qs, mask, s_ref[kh:kh + 1, :])
                dp = _dot_nt(vv, dos)
                dv = _dot(probs.astype(BF16), dos)
                rowdot = jnp.sum(probs * dp, axis=0, keepdims=True)
                dsc = (probs * (dp - rowdot) * (HEAD_DIM ** -0.5)).astype(BF16)
                dsacc[kh:kh + 1, :] += -psink * rowdot
                dk = _dot(dsc, qs)
                dqs = _dot_tn(dsc, kk)
                for g in range(GROUP):
                    col = (kh * GROUP + g) * HEAD_DIM
                    dq_ref[pl.ds(st, ATTN_BLOCK), col:col + HEAD_DIM] = (
                        dqs[g * ATTN_BLOCK:(g + 1) * ATTN_BLOCK].astype(dq_ref.dtype))
                dkvp[pl.ds(st, 2 * ATTN_BLOCK), kh * HEAD_DIM:(kh + 1) * HEAD_DIM] += dk
                dkvp[pl.ds(st, 2 * ATTN_BLOCK), KV_WIDTH + kh * HEAD_DIM:KV_WIDTH + (kh + 1) * HEAD_DIM] += dv
            return carry

        lax.fori_loop(0, nb, blk, 0)
        dkv_ref[...] = dkvp[ATTN_BLOCK:, :].astype(dkv_ref.dtype)

        @pl.when(pl.program_id(0) == nseq - 1)
        def _():
            for kh in range(N_KV_HEADS):
                for g in range(GROUP):
                    tot = jnp.sum(dsacc[kh:kh + 1, g * ATTN_BLOCK:(g + 1) * ATTN_BLOCK], axis=1, keepdims=True)
                    ds_ref[kh * GROUP + g:kh * GROUP + g + 1, :] = jnp.broadcast_to(tot, (1, LANES))

    seq_q = pl.BlockSpec((seq, ATTN_WIDTH), lambda b: (b, 0))
    seq_kv = pl.BlockSpec((seq, 2 * KV_WIDTH), lambda b: (b, 0))
    return pl.pallas_call(
        body, name=name, grid=(nseq,),
        in_specs=[seq_q, seq_kv, pl.BlockSpec((8, GQ), lambda b: (0, 0)), seq_q],
        out_specs=[seq_q, seq_kv, pl.BlockSpec((N_Q_HEADS, LANES), lambda b: (0, 0))],
        out_shape=[jax.ShapeDtypeStruct((nseq * seq, ATTN_WIDTH), BF16),
                   jax.ShapeDtypeStruct((nseq * seq, 2 * KV_WIDTH), BF16),
                   jax.ShapeDtypeStruct((N_Q_HEADS, LANES), F32)],
        scratch_shapes=[pltpu.VMEM((ATTN_BLOCK + seq, 2 * KV_WIDTH), BF16),
                        pltpu.VMEM((ATTN_BLOCK + seq, 2 * KV_WIDTH), F32),
                        pltpu.VMEM((8, GQ), F32)],
        compiler_params=_params("arbitrary"),
    )(q, kv, sinks_t, do)


CONV_T = 128


def _conv_taps(win, w_ref, lanes, init):
    acc = init
    for j in range(CONV_KERNEL):
        sh = win if j == CONV_KERNEL - 1 else pltpu.roll(win, CONV_KERNEL - 1 - j, 0)
        acc = acc + w_ref[j:j + 1, lanes] * sh[CONV_HALO:CONV_HALO + CONV_T]
    return acc


def _conv_block(h0p, w_ref, vec_ref, st):
    cols = []
    for cs in range(CONV_WIDTH // LANES):
        lanes = slice(cs * LANES, (cs + 1) * LANES)
        win = h0p[pl.ds(st, CONV_T + CONV_HALO), lanes]
        init = jnp.broadcast_to(vec_ref[0:1, lanes], (CONV_T, LANES))
        cols.append(_conv_taps(win, w_ref, lanes, init))
    return jnp.concatenate(cols, axis=-1)


def _glu_store(c_ref, h0p, st):
    cb = c_ref[pl.ds(st, CONV_T), :]
    h0p[pl.ds(pl.multiple_of(st + CONV_HALO, CONV_HALO), CONV_T), :] = cb[:, :CONV_WIDTH] * _sigmoid(cb[:, CONV_WIDTH:])


def conv_fwd(c, w, vec, nseq, seq, name):
    nb = seq // CONV_T

    def body(c_ref, w_ref, vec_ref, o_ref, h0p):
        h0p[0:CONV_HALO, :] = jnp.zeros((CONV_HALO, CONV_WIDTH), F32)

        def blk(n, carry):
            st = pl.multiple_of(n * CONV_T, CONV_T)
            _glu_store(c_ref, h0p, st)
            h1 = _conv_block(h0p, w_ref, vec_ref, st)
            mu = jnp.mean(h1, axis=-1, keepdims=True)
            xc = h1 - mu
            rstd = lax.rsqrt(jnp.mean(xc * xc, axis=-1, keepdims=True) + EPS)
            y = xc * rstd * vec_ref[1:2, :] + vec_ref[2:3, :]
            o_ref[pl.ds(st, CONV_T), :] = (y * _sigmoid(y)).astype(o_ref.dtype)
            return carry

        lax.fori_loop(0, nb, blk, 0)

    return pl.pallas_call(
        body, name=name, grid=(nseq,),
        in_specs=[pl.BlockSpec((seq, 2 * CONV_WIDTH), lambda b: (b, 0)),
                  pl.BlockSpec((32, CONV_WIDTH), lambda b: (0, 0)),
                  pl.BlockSpec((8, CONV_WIDTH), lambda b: (0, 0))],
        out_specs=pl.BlockSpec((seq, CONV_WIDTH), lambda b: (b, 0)),
        out_shape=jax.ShapeDtypeStruct((nseq * seq, CONV_WIDTH), BF16),
        scratch_shapes=[pltpu.VMEM((CONV_HALO + seq, CONV_WIDTH), F32)],
        compiler_params=_params("parallel"),
    )(c, w, vec)


def conv_bwd(c, w, vec, dout, nseq, seq, name):
    nb = seq // CONV_T

    def body(c_ref, w_ref, vec_ref, do_ref, dc_ref, dw_ref, dvec_ref, h0p, dh1p):
        @pl.when(pl.program_id(0) == 0)
        def _():
            dw_ref[...] = jnp.zeros(dw_ref.shape, F32)
            dvec_ref[...] = jnp.zeros(dvec_ref.shape, F32)

        h0p[0:CONV_HALO, :] = jnp.zeros((CONV_HALO, CONV_WIDTH), F32)
        dh1p[seq:seq + CONV_HALO, :] = jnp.zeros((CONV_HALO, CONV_WIDTH), F32)

        def pass_a(n, carry):
            st = pl.multiple_of(n * CONV_T, CONV_T)
            _glu_store(c_ref, h0p, st)
            h1 = _conv_block(h0p, w_ref, vec_ref, st)
            mu = jnp.mean(h1, axis=-1, keepdims=True)
            xc = h1 - mu
            rstd = lax.rsqrt(jnp.mean(xc * xc, axis=-1, keepdims=True) + EPS)
            xh = xc * rstd
            y = xh * vec_ref[1:2, :] + vec_ref[2:3, :]
            sg = _sigmoid(y)
            dy = do_ref[pl.ds(st, CONV_T), :] * (sg * (1.0 + y * (1.0 - sg)))
            dvec_ref[1:2, :] += jnp.sum(dy * xh, axis=0, keepdims=True)
            dvec_ref[2:3, :] += jnp.sum(dy, axis=0, keepdims=True)
            dxh = dy * vec_ref[1:2, :]
            dh1 = rstd * (dxh - jnp.mean(dxh, axis=-1, keepdims=True)
                          - xh * jnp.mean(dxh * xh, axis=-1, keepdims=True))
            dvec_ref[0:1, :] += jnp.sum(dh1, axis=0, keepdims=True)
            dh1p[pl.ds(st, CONV_T), :] = dh1
            return carry

        lax.fori_loop(0, nb, pass_a, 0)

        def pass_b(n, carry):
            st = pl.multiple_of(n * CONV_T, CONV_T)
            cols = []
            for cs in range(CONV_WIDTH // LANES):
                lanes = slice(cs * LANES, (cs + 1) * LANES)
                wind = dh1p[pl.ds(st, CONV_T + CONV_HALO), lanes]
                winh = h0p[pl.ds(st, CONV_T + CONV_HALO), lanes]
                d1 = wind[0:CONV_T]
                acc = jnp.zeros((CONV_T, LANES), F32)
                for j in range(CONV_KERNEL):
                    acc = acc + w_ref[j:j + 1, lanes] * pltpu.roll(wind, 2 + j, 0)[CONV_HALO:CONV_HALO + CONV_T]
                    hs = winh if j == CONV_KERNEL - 1 else pltpu.roll(winh, CONV_KERNEL - 1 - j, 0)
                    dw_ref[j:j + 1, lanes] += jnp.sum(d1 * hs[CONV_HALO:CONV_HALO + CONV_T], axis=0, keepdims=True)
                cols.append(acc)
            dh0 = jnp.concatenate(cols, axis=-1)
            cb = c_ref[pl.ds(st, CONV_T), :]
            av, gt = cb[:, :CONV_WIDTH], cb[:, CONV_WIDTH:]
            sg = _sigmoid(gt)
            dc_ref[pl.ds(st, CONV_T), :] = jnp.concatenate(
                [dh0 * sg, dh0 * av * sg * (1.0 - sg)], axis=-1).astype(dc_ref.dtype)
            return carry

        lax.fori_loop(0, nb, pass_b, 0)

    return pl.pallas_call(
        body, name=name, grid=(nseq,),
        in_specs=[pl.BlockSpec((seq, 2 * CONV_WIDTH), lambda b: (b, 0)),
                  pl.BlockSpec((32, CONV_WIDTH), lambda b: (0, 0)),
                  pl.BlockSpec((8, CONV_WIDTH), lambda b: (0, 0)),
                  pl.BlockSpec((seq, CONV_WIDTH), lambda b: (b, 0))],
        out_specs=[pl.BlockSpec((seq, 2 * CONV_WIDTH), lambda b: (b, 0)),
                   pl.BlockSpec((32, CONV_WIDTH), lambda b: (0, 0)),
                   pl.BlockSpec((8, CONV_WIDTH), lambda b: (0, 0))],
        out_shape=[jax.ShapeDtypeStruct((nseq * seq, 2 * CONV_WIDTH), BF16),
                   jax.ShapeDtypeStruct((32, CONV_WIDTH), F32),
                   jax.ShapeDtypeStruct((8, CONV_WIDTH), F32)],
        scratch_shapes=[pltpu.VMEM((CONV_HALO + seq, CONV_WIDTH), F32),
                        pltpu.VMEM((seq + CONV_HALO, CONV_WIDTH), F32)],
        compiler_params=_params("arbitrary"),
    )(c, w, vec, dout)


POOL_T = 128


def _pooled_block(zpp, st, grp):
    lanes = slice(grp * LANES, (grp + 1) * LANES)
    win = zpp[pl.ds(st, POOL_T + POOL_HALO), lanes]
    acc = win
    for lvl in range(grp + 1):
        acc = acc + pltpu.roll(acc, 1 << lvl, 0)
    t = st + lax.broadcasted_iota(jnp.int32, (POOL_T, 1), 0)
    inv = 1.0 / jnp.minimum(t + 1, POOL_WINDOWS[grp]).astype(F32)
    return acc[POOL_HALO:] * inv - win[POOL_HALO:], inv


def pool_fwd(zp, wp, scale, nseq, seq, name):
    nb = seq // POOL_T

    def body(z_ref, wp_ref, sc_ref, o_ref, zpp):
        zpp[0:POOL_HALO, :] = jnp.zeros((POOL_HALO, POOL_WIDTH), F32)
        zpp[POOL_HALO:, :] = z_ref[...]

        def blk(n, carry):
            st = pl.multiple_of(n * POOL_T, POOL_T)
            for grp in range(len(POOL_WINDOWS)):
                lanes = slice(grp * LANES, (grp + 1) * LANES)
                pooled, _ = _pooled_block(zpp, st, grp)
                o_ref[pl.ds(st, POOL_T), lanes] = (
                    _dot(pooled.astype(BF16), wp_ref[grp]) * sc_ref[0:1, lanes]).astype(o_ref.dtype)
            return carry

        lax.fori_loop(0, nb, blk, 0)

    return pl.pallas_call(
        body, name=name, grid=(nseq,),
        in_specs=[pl.BlockSpec((seq, POOL_WIDTH), lambda b: (b, 0)),
                  pl.BlockSpec((4, LANES, LANES), lambda b: (0, 0, 0)),
                  pl.BlockSpec((1, POOL_WIDTH), lambda b: (0, 0))],
        out_specs=pl.BlockSpec((seq, POOL_WIDTH), lambda b: (b, 0)),
        out_shape=jax.ShapeDtypeStruct((nseq * seq, POOL_WIDTH), BF16),
        scratch_shapes=[pltpu.VMEM((POOL_HALO + seq, POOL_WIDTH), F32)],
        compiler_params=_params("parallel"),
    )(zp, wp, scale)


def pool_bwd(zp, wp, scale, dout, nseq, seq, name):
    nb = seq // POOL_T

    def body(z_ref, wp_ref, sc_ref, do_ref, dz_ref, dwp_ref, dsc_ref, zpp, dpcp, negd):
        @pl.when(pl.program_id(0) == 0)
        def _():
            dwp_ref[...] = jnp.zeros(dwp_ref.shape, F32)
            dsc_ref[...] = jnp.zeros(dsc_ref.shape, F32)

        zpp[0:POOL_HALO, :] = jnp.zeros((POOL_HALO, POOL_WIDTH), F32)
        zpp[POOL_HALO:, :] = z_ref[...]
        dpcp[seq:seq + POOL_HALO, :] = jnp.zeros((POOL_HALO, POOL_WIDTH), F32)

        def pass_a(n, carry):
            st = pl.multiple_of(n * POOL_T, POOL_T)
            for grp in range(len(POOL_WINDOWS)):
                lanes = slice(grp * LANES, (grp + 1) * LANES)
                pooled, inv = _pooled_block(zpp, st, grp)
                pb = pooled.astype(BF16)
                dob = do_ref[pl.ds(st, POOL_T), lanes]
                dsc_ref[0:1, lanes] += jnp.sum(dob * _dot(pb, wp_ref[grp]), axis=0, keepdims=True)
                dpm = (dob * sc_ref[0:1, lanes]).astype(BF16)
                dwp_ref[grp] += _dot_tn(pb, dpm)
                dpooled = _dot_nt(dpm, wp_ref[grp])
                negd[pl.ds(st, POOL_T), lanes] = -dpooled
                dpcp[pl.ds(st, POOL_T), lanes] = dpooled * inv
            return carry

        lax.fori_loop(0, nb, pass_a, 0)

        def pass_b(n, carry):
            st = pl.multiple_of(n * POOL_T, POOL_T)
            rows = POOL_T + POOL_HALO
            for grp in range(len(POOL_WINDOWS)):
                lanes = slice(grp * LANES, (grp + 1) * LANES)
                acc = dpcp[pl.ds(st, rows), lanes]
                for lvl in range(grp + 1):
                    acc = acc + pltpu.roll(acc, rows - (1 << lvl), 0)
                dz_ref[pl.ds(st, POOL_T), lanes] = (acc[0:POOL_T] + negd[pl.ds(st, POOL_T), lanes]).astype(dz_ref.dtype)
            return carry

        lax.fori_loop(0, nb, pass_b, 0)

    seq_spec = pl.BlockSpec((seq, POOL_WIDTH), lambda b: (b, 0))
    return pl.pallas_call(
        body, name=name, grid=(nseq,),
        in_specs=[seq_spec, pl.BlockSpec((4, LANES, LANES), lambda b: (0, 0, 0)),
                  pl.BlockSpec((1, POOL_WIDTH), lambda b: (0, 0)), seq_spec],
        out_specs=[seq_spec, pl.BlockSpec((4, LANES, LANES), lambda b: (0, 0, 0)),
                   pl.BlockSpec((8, POOL_WIDTH), lambda b: (0, 0))],
        out_shape=[jax.ShapeDtypeStruct((nseq * seq, POOL_WIDTH), BF16),
                   jax.ShapeDtypeStruct((4, LANES, LANES), F32),
                   jax.ShapeDtypeStruct((8, POOL_WIDTH), F32)],
        scratch_shapes=[pltpu.VMEM((POOL_HALO + seq, POOL_WIDTH), F32),
                        pltpu.VMEM((seq + POOL_HALO, POOL_WIDTH), F32),
                        pltpu.VMEM((seq, POOL_WIDTH), F32)],
        compiler_params=_params("arbitrary"),
    )(zp, wp, scale, dout)


GELU_C0 = 0.7978845608028654
GELU_C1 = 0.044715


def _gelu(xv):
    return xv * (0.5 * (1.0 + jnp.tanh(GELU_C0 * (xv + GELU_C1 * (xv * xv * xv)))))


def _gelu_grad(xv):
    t = jnp.tanh(GELU_C0 * (xv + GELU_C1 * (xv * xv * xv)))
    return 0.5 * (1.0 + t) + 0.5 * xv * (1.0 - t * t) * (GELU_C0 * (1.0 + 3.0 * GELU_C1 * xv * xv))


def _tril():
    ti = lax.broadcasted_iota(jnp.int32, (SGU_CHUNK, SGU_CHUNK), 0)
    si = lax.broadcasted_iota(jnp.int32, (SGU_CHUNK, SGU_CHUNK), 1)
    return si <= ti


def sgu_fwd(zs, ws, bst, ln, nseq, seq, name):
    nc = seq // SGU_CHUNK

    def body(z_ref, ws_ref, bs_ref, ln_ref, o_ref):
        tril = _tril()

        def blk(n, carry):
            st = pl.multiple_of(n * SGU_CHUNK, SGU_CHUNK)
            ge = _gelu(z_ref[pl.ds(st, SGU_CHUNK), :])
            uu, vv = ge[:, :SGU_WIDTH], ge[:, SGU_WIDTH:]
            mu = jnp.mean(vv, axis=-1, keepdims=True)
            xc = vv - mu
            rstd = lax.rsqrt(jnp.mean(xc * xc, axis=-1, keepdims=True) + EPS)
            vn = (xc * rstd * ln_ref[0:1, :] + ln_ref[1:2, :]).astype(BF16)
            for g in range(4):
                lanes = slice(g * LANES, (g + 1) * LANES)
                wm = jnp.where(tril, ws_ref[g], 0.0).astype(BF16)
                mixed = _dot(wm, vn[:, lanes]) + bs_ref[:, g:g + 1]
                o_ref[pl.ds(st, SGU_CHUNK), lanes] = (uu[:, lanes] * mixed).astype(o_ref.dtype)
            return carry

        lax.fori_loop(0, nc, blk, 0)

    return pl.pallas_call(
        body, name=name, grid=(nseq,),
        in_specs=[pl.BlockSpec((seq, 2 * SGU_WIDTH), lambda b: (b, 0)),
                  pl.BlockSpec((4, LANES, LANES), lambda b: (0, 0, 0)),
                  pl.BlockSpec((SGU_CHUNK, 4), lambda b: (0, 0)),
                  pl.BlockSpec((8, SGU_WIDTH), lambda b: (0, 0))],
        out_specs=pl.BlockSpec((seq, SGU_WIDTH), lambda b: (b, 0)),
        out_shape=jax.ShapeDtypeStruct((nseq * seq, SGU_WIDTH), BF16),
        compiler_params=_params("parallel"),
    )(zs, ws, bst, ln)


def sgu_bwd(zs, ws, bst, ln, dout, nseq, seq, name):
    nc = seq // SGU_CHUNK

    def body(z_ref, ws_ref, bs_ref, ln_ref, do_ref, dz_ref, dws_ref, dbs_ref, dln_ref):
        @pl.when(pl.program_id(0) == 0)
        def _():
            dws_ref[...] = jnp.zeros(dws_ref.shape, F32)
            dbs_ref[...] = jnp.zeros(dbs_ref.shape, F32)
            dln_ref[...] = jnp.zeros(dln_ref.shape, F32)

        tril = _tril()

        def blk(n, carry):
            st = pl.multiple_of(n * SGU_CHUNK, SGU_CHUNK)
            zv = z_ref[pl.ds(st, SGU_CHUNK), :]
            ge = _gelu(zv)
            uu, vv = ge[:, :SGU_WIDTH], ge[:, SGU_WIDTH:]
            mu = jnp.mean(vv, axis=-1, keepdims=True)
            xc = vv - mu
            rstd = lax.rsqrt(jnp.mean(xc * xc, axis=-1, keepdims=True) + EPS)
            xh = xc * rstd
            vn = (xh * ln_ref[0:1, :] + ln_ref[1:2, :]).astype(BF16)
            dob = do_ref[pl.ds(st, SGU_CHUNK), :]
            du_cols, dvn_cols = [], []
            for g in range(4):
                lanes = slice(g * LANES, (g + 1) * LANES)
                wm = jnp.where(tril, ws_ref[g], 0.0).astype(BF16)
                mixed = _dot(wm, vn[:, lanes]) + bs_ref[:, g:g + 1]
                du_cols.append(dob[:, lanes] * mixed)
                dmix = dob[:, lanes] * uu[:, lanes]
                dbs_ref[g] += jnp.broadcast_to(jnp.sum(dmix, axis=-1, keepdims=True), (SGU_CHUNK, LANES))
                dmb = dmix.astype(BF16)
                dws_ref[g] += jnp.where(tril, _dot_nt(dmb, vn[:, lanes]), 0.0)
                dvn_cols.append(_dot_tn(wm, dmb))
            dvn = jnp.concatenate(dvn_cols, axis=-1)
            dln_ref[0:1, :] += jnp.sum(dvn * xh, axis=0, keepdims=True)
            dln_ref[1:2, :] += jnp.sum(dvn, axis=0, keepdims=True)
            dxh = dvn * ln_ref[0:1, :]
            dv = rstd * (dxh - jnp.mean(dxh, axis=-1, keepdims=True)
                         - xh * jnp.mean(dxh * xh, axis=-1, keepdims=True))
            dge = jnp.concatenate(du_cols + [dv], axis=-1)
            dz_ref[pl.ds(st, SGU_CHUNK), :] = (dge * _gelu_grad(zv)).astype(dz_ref.dtype)
            return carry

        lax.fori_loop(0, nc, blk, 0)

    w_spec = pl.BlockSpec((4, LANES, LANES), lambda b: (0, 0, 0))
    ln_spec = pl.BlockSpec((8, SGU_WIDTH), lambda b: (0, 0))
    return pl.pallas_call(
        body, name=name, grid=(nseq,),
        in_specs=[pl.BlockSpec((seq, 2 * SGU_WIDTH), lambda b: (b, 0)), w_spec,
                  pl.BlockSpec((SGU_CHUNK, 4), lambda b: (0, 0)), ln_spec,
                  pl.BlockSpec((seq, SGU_WIDTH), lambda b: (b, 0))],
        out_specs=[pl.BlockSpec((seq, 2 * SGU_WIDTH), lambda b: (b, 0)), w_spec, w_spec, ln_spec],
        out_shape=[jax.ShapeDtypeStruct((nseq * seq, 2 * SGU_WIDTH), BF16),
                   jax.ShapeDtypeStruct((4, LANES, LANES), F32),
                   jax.ShapeDtypeStruct((4, LANES, LANES), F32),
                   jax.ShapeDtypeStruct((8, SGU_WIDTH), F32)],
        compiler_params=_params("arbitrary"),
    )(zs, ws, bst, ln, dout)


def _ew_rows(rows, cols, nbuf):
    t = _row_tile(rows, 1024)
    while t > 8 and t * cols * 4 * nbuf * 2 > 24 * 2**20:
        t //= 2
    return t


def adamw(w, g, m, v, name):
    layers, rows, cols = w.shape
    tr = _ew_rows(rows, cols, 7)

    def body(w_ref, g_ref, m_ref, v_ref, d_ref, mo_ref, vo_ref):
        gv = g_ref[...]
        mn = ADAM_B1 * m_ref[...] + (1.0 - ADAM_B1) * gv
        vn = ADAM_B2 * v_ref[...] + (1.0 - ADAM_B2) * (gv * gv)
        m_hat = mn / (1.0 - ADAM_B1 ** ADAM_STEP)
        v_hat = vn / (1.0 - ADAM_B2 ** ADAM_STEP)
        d_ref[...] = -ADAM_LR * (m_hat / (jnp.sqrt(v_hat) + ADAM_EPS) + ADAM_WD * w_ref[...])
        mo_ref[...] = mn
        vo_ref[...] = vn

    spec = pl.BlockSpec((1, tr, cols), lambda l, i: (l, i, 0))
    return pl.pallas_call(
        body, name=name, grid=(layers, rows // tr),
        in_specs=[spec] * 4, out_specs=[spec] * 3,
        out_shape=[jax.ShapeDtypeStruct(w.shape, F32)] * 3,
        compiler_params=_params("parallel", "parallel"),
    )(w, g, m, v)


def add_cast(a, b, name, dtype=BF16):
    nslab, rows, cols = a.shape
    tr = _ew_rows(rows, cols, 3)

    def body(a_ref, b_ref, o_ref):
        o_ref[...] = (a_ref[...] + b_ref[...]).astype(dtype)

    spec = pl.BlockSpec((1, tr, cols), lambda k, i: (k, i, 0))
    return pl.pallas_call(
        body, name=name, grid=(nslab, rows // tr),
        in_specs=[spec, spec], out_specs=spec,
        out_shape=jax.ShapeDtypeStruct(a.shape, dtype),
        compiler_params=_params("parallel", "parallel"),
    )(a, b)


def sum_parts(parts, name, first=None):
    npart, rows, cols = parts.shape
    tr = _ew_rows(rows, cols, npart + 2)

    def body(*refs):
        p_ref, o_ref = refs[-2], refs[-1]
        acc = p_ref[0].astype(F32) if first is None else refs[0][...].astype(F32) + p_ref[0].astype(F32)
        for j in range(1, npart):
            acc = acc + p_ref[j].astype(F32)
        o_ref[...] = acc

    row = pl.BlockSpec((tr, cols), lambda i: (i, 0))
    ins = [parts] if first is None else [first, parts]
    return pl.pallas_call(
        body, name=name, grid=(rows // tr,),
        in_specs=([] if first is None else [row]) + [pl.BlockSpec((npart, tr, cols), lambda i: (0, i, 0))],
        out_specs=row,
        out_shape=jax.ShapeDtypeStruct((rows, cols), F32),
        compiler_params=_params("parallel"),
    )(*ins)


ANY = pl.BlockSpec(memory_space=pl.ANY)
MESH = pl.DeviceIdType.MESH


def _me():
    return lax.axis_index("x"), lax.axis_index("y"), lax.axis_index("c")


def _flip(pos, rel):
    return tuple(1 - p if f else p for p, f in zip(pos, rel))


def _exchange(name, ins, out_shapes, plan):
    n_in, n_out = len(ins), len(out_shapes)

    def body(*refs):
        in_refs, out_refs = refs[:n_in], refs[n_in:n_in + n_out]
        send_sems, recv_sems, loc_sems = refs[n_in + n_out:]
        pos = _me()
        remote, local = plan(in_refs, out_refs, pos)
        loc = [pltpu.make_async_copy(s, d, loc_sems.at[i]) for i, (s, d) in enumerate(local)]
        for cp in loc:
            cp.start()
        rem = [pltpu.make_async_remote_copy(src_ref=s, dst_ref=d, send_sem=send_sems.at[i], recv_sem=recv_sems.at[i],
                                            device_id=_flip(pos, rel), device_id_type=MESH)
               for i, (rel, s, d) in enumerate(remote)]
        for cp in rem:
            cp.start()
        for cp in rem:
            cp.wait()
        for cp in loc:
            cp.wait()

    n_rem, n_loc = plan.counts
    return pl.pallas_call(
        body, name=name,
        in_specs=[ANY] * n_in, out_specs=[ANY] * n_out,
        out_shape=[jax.ShapeDtypeStruct(s, d) for s, d in out_shapes],
        scratch_shapes=[pltpu.SemaphoreType.DMA((n_rem,)), pltpu.SemaphoreType.DMA((n_rem,)),
                        pltpu.SemaphoreType.DMA((max(n_loc, 1),))],
    )(*ins)


SIBLING = (0, 0, 1)
OTHER_CHIPS = ((1, 0, 0), (0, 1, 0), (1, 1, 0))


def _chip_of(pos, rel=(0, 0, 0)):
    px, py, _ = _flip(pos, rel)
    return 2 * px + py


def allgather_blocks(shards, name):
    nt = len(shards)
    hs = [s.shape[0] // 2 for s in shards]

    def body(*refs):
        ins, outs = refs[:nt], refs[nt:2 * nt]
        send_sems, recv_sems, loc_sems = refs[2 * nt:]
        pos = _me()
        x, y, c = pos

        def block_id(rel):
            px, py, pc = _flip(pos, rel)
            return 4 * px + 2 * py + pc

        def copy(t, k, block_rel, to_rel, src=None):
            dst = outs[t].at[block_id(block_rel)]
            return pltpu.make_async_remote_copy(
                src_ref=dst if src is None else src, dst_ref=dst,
                send_sem=send_sems.at[t * 7 + k], recv_sem=recv_sems.at[t * 7 + k],
                device_id=_flip(pos, to_rel), device_id_type=MESH)

        own = [ins[t].at[pl.ds(c * hs[t], hs[t])] for t in range(nt)]
        mine = [pltpu.make_async_copy(own[t], outs[t].at[block_id((0, 0, 0))], loc_sems.at[t]) for t in range(nt)]
        for cp in mine:
            cp.start()
        first = []
        for t in range(nt):
            first.append(copy(t, 0, (0, 0, 0), SIBLING, src=own[t]))
            first += [copy(t, 1 + j, (0, 0, 0), rel, src=own[t]) for j, rel in enumerate(OTHER_CHIPS)]
        for cp in first:
            cp.start()
        passed = []
        for j, rel in enumerate(OTHER_CHIPS):
            for t in range(nt):
                copy(t, 1 + j, rel, (0, 0, 0)).wait_recv()
                fwd = copy(t, 4 + j, rel, SIBLING)
                fwd.start()
                passed.append(fwd)
        for t in range(nt):
            copy(t, 0, SIBLING, (0, 0, 0)).wait_recv()
            for j, rel in enumerate(OTHER_CHIPS):
                copy(t, 4 + j, (rel[0], rel[1], 1), (0, 0, 0)).wait_recv()
        for cp in first + passed:
            cp.wait_send()
        for cp in mine:
            cp.wait()

    return pl.pallas_call(
        body, name=name,
        in_specs=[ANY] * nt, out_specs=[ANY] * nt,
        out_shape=[jax.ShapeDtypeStruct((N_DEV, h, s.shape[1]), s.dtype) for h, s in zip(hs, shards)],
        scratch_shapes=[pltpu.SemaphoreType.DMA((7 * nt,)), pltpu.SemaphoreType.DMA((7 * nt,)),
                        pltpu.SemaphoreType.DMA((nt,))],
    )(*shards)


def chip_gather(block, name):
    def plan(ins, outs, pos):
        me = _chip_of(pos)
        remote = [(rel, ins[0], outs[0].at[me]) for rel in OTHER_CHIPS]
        return remote, [(ins[0], outs[0].at[me])]

    plan.counts = (3, 1)
    return _exchange(name, [block], [((N_CHIPS,) + block.shape, block.dtype)], plan)[0]


def _block_id(pos, rel=(0, 0, 0)):
    px, py, pc = _flip(pos, rel)
    return 4 * px + 2 * py + pc


def gather_first_hop(shards):
    hs = [s.shape[0] // 2 for s in shards]

    def plan(ins, outs, pos):
        me = _block_id(pos)
        remote = []
        for i, o, h in zip(ins, outs, hs):
            own = i.at[pl.ds(pos[2] * h, h)]
            remote += [(rel, own, o.at[me]) for rel in (SIBLING,) + OTHER_CHIPS]
        return remote

    return Comm(shards, [((N_DEV, h, s.shape[1]), s.dtype) for h, s in zip(hs, shards)], plan, 4 * len(shards))


def gather_second_hop(gathered):
    def plan(ins, outs, pos):
        remote = []
        for i, o in zip(ins, outs):
            for rel in OTHER_CHIPS:
                blk = _block_id(pos, rel)
                remote.append((SIBLING, i.at[blk], o.at[blk]))
        return remote

    return Comm(gathered, [(g.shape, g.dtype) for g in gathered], plan, 3 * len(gathered),
                aliases={i: i for i in range(len(gathered))})


def sibling_swap(xs, name):
    def plan(ins, outs, pos):
        return [(SIBLING, i, o) for i, o in zip(ins, outs)], []

    plan.counts = (len(xs), 0)
    return _exchange(name, xs, [(v.shape, v.dtype) for v in xs], plan)


def chip_scatter(xs, shared, name):
    nx = len(xs)

    def plan(ins, outs, pos):
        me = _chip_of(pos)
        remote = []
        for i, o in zip(ins[:nx], outs[:nx]):
            remote += [(rel, i.at[_chip_of(pos, rel)], o.at[j]) for j, rel in enumerate(OTHER_CHIPS)]
        remote += [(rel, ins[nx], outs[nx].at[me]) for rel in OTHER_CHIPS]
        return remote, [(ins[nx], outs[nx].at[me])]

    plan.counts = (3 * nx + 3, 1)
    shapes = [((3,) + v.shape[1:], v.dtype) for v in xs] + [((N_CHIPS,) + shared.shape, shared.dtype)]
    res = _exchange(name, list(xs) + [shared], shapes, plan)
    return res[:nx], res[nx]


PACK_ROWS = 256


def _pack(arrs):
    parts, layout = [], []
    row = 0
    for a in arrs:
        flat = a.reshape(-1).astype(F32)
        size = flat.shape[0]
        rows = -(-size // (8 * LANES)) * 8
        flat = jnp.pad(flat, (0, rows * LANES - size))
        parts.append(flat.reshape(rows, LANES))
        layout.append((row, rows, size, a.shape))
        row += rows
    if row % PACK_ROWS:
        parts.append(jnp.zeros((PACK_ROWS - row % PACK_ROWS, LANES), F32))
    return jnp.concatenate(parts, axis=0), layout


def _unpack(packed, layout):
    return [packed[r0:r0 + rows].reshape(-1)[:size].reshape(shape) for r0, rows, size, shape in layout]


SMALL_REPL = ['mix_norm', 'a_b_in', 'a_sinks', 'a_conv_b', 'a_cln_g', 'a_cln_b', 'c_w_pool', 'c_w_s', 'c_b_s',
              'ffn_norm', 'final_norm']
SMALL_SHARD = ['a_conv_w', 'c_pool_scale', 'c_sln_g', 'c_sln_b']
BIG = ['a_w_in', 'a_w_out', 'c_w_in', 'c_w_out', 'ffn_w_gate', 'ffn_w_up', 'ffn_w_down']
BIG_ROW_SHARDED = {'a_w_out', 'c_w_out', 'ffn_w_down'}


def _full_weight(name, g8):
    _, h, cols = g8.shape
    g4 = g8.reshape(N_CHIPS, 2 * h, cols)
    if name in BIG_ROW_SHARDED:
        return g4.reshape(-1, cols)
    return jnp.transpose(g4, (1, 0, 2)).reshape(2 * h, N_CHIPS * cols)


def _to_shard_major(name, f):
    if name in BIG_ROW_SHARDED:
        return f.reshape(N_CHIPS, f.shape[0] // N_CHIPS, f.shape[1])
    r, cfull = f.shape
    return jnp.transpose(f.reshape(r, N_CHIPS, cfull // N_CHIPS), (1, 0, 2))


def kernel(*args):
    a = dict(zip(IN_NAMES, args))
    bl, seq, _ = a['x'].shape
    n = bl * seq
    x = a['x'].reshape(n, D_MODEL)
    target = a['loss_target'].reshape(n, D_MODEL)
    xi, yi, ci = _me()
    chip = 2 * xi + yi

    layer0 = [('a_w_in', 0), ('a_w_out', 0), ('ffn_w_gate', 0), ('ffn_w_up', 0), ('ffn_w_down', 0)]
    layer1 = [('c_w_in', 0), ('c_w_out', 0), ('ffn_w_gate', 1), ('ffn_w_up', 1), ('ffn_w_down', 1)]
    shards0 = [a[k][l].astype(BF16) for k, l in layer0]
    shards1 = [a[k][l].astype(BF16) for k, l in layer1]
    gathered0 = allgather_blocks(shards0, "gather_layer0")
    a_w_in, a_w_out, wg0, wu0, wd0 = [_full_weight(k, g) for (k, _), g in zip(layer0, gathered0)]

    small_shard_pack, small_shard_layout = _pack([a[k] for k in SMALL_SHARD])
    ss = chip_gather(small_shard_pack, "gather_small")
    ss_full = []
    for r0, rows, size, shape in small_shard_layout:
        per_chip = ss[:, r0:r0 + rows].reshape(N_CHIPS, -1)[:, :size].reshape((N_CHIPS,) + shape)
        ss_full.append(jnp.concatenate([per_chip[k] for k in range(N_CHIPS)], axis=-1))
    a_conv_w, c_pool_scale, c_sln_g, c_sln_b = [v[0] for v in ss_full]

    conv_taps = jnp.pad(a_conv_w, ((0, 32 - CONV_KERNEL), (0, 0)))
    conv_vec = jnp.pad(jnp.stack([a['a_conv_b'][0], a['a_cln_g'][0], a['a_cln_b'][0]]), ((0, 5), (0, 0)))
    sinks_b = jnp.pad(jnp.repeat(a['a_sinks'][0].reshape(N_KV_HEADS, GROUP), ATTN_BLOCK, axis=1), ((0, 6), (0, 0)))
    w_pool_bf = a['c_w_pool'][0].astype(BF16)
    pool_scale = c_pool_scale.reshape(1, POOL_WIDTH)
    w_s = a['c_w_s'][0]
    b_s_t = a['c_b_s'][0].T
    sgu_ln = jnp.pad(jnp.stack([c_sln_g, c_sln_b]), ((0, 6), (0, 0)))
    mix_norm, ffn_norm = a['mix_norm'], a['ffn_norm']
    final_norm = a['final_norm'].reshape(1, D_MODEL)

    hn0, q, kv, cc = norm_inproj(
        x, mix_norm[0:1], a_w_in, a['a_b_in'],
        [(0, ATTN_WIDTH), (ATTN_WIDTH, ATTN_WIDTH + 2 * KV_WIDTH), (ATTN_WIDTH + 2 * KV_WIDTH, a_w_in.shape[1])],
        [BF16, BF16, F32], "in_proj0")
    attn, hop_a = attn_fwd(q, kv, sinks_b, bl, seq, "attn_fwd", comm=gather_first_hop(shards1[:2]))
    conv = conv_fwd(cc, conv_taps, conv_vec, bl, seq, "conv_fwd")
    h1 = out_proj(x, attn, conv, a_w_out, "out_proj0")
    (hnf0, g0, u0), hop_b = ffn_gate_up(h1, ffn_norm[0:1], wg0, wu0, "ffn_gate_up0",
                                        comm=gather_first_hop(shards1[2:]))
    h2, gathered1 = ffn_down(h1, g0, u0, wd0, "ffn_down0", comm=gather_second_hop(hop_a + hop_b))
    weights1 = []
    for (k, _), g, s in zip(layer1, gathered1, shards1):
        h = s.shape[0] // 2
        own = lax.dynamic_slice_in_dim(s, ci * h, h, axis=0)
        g = lax.dynamic_update_slice_in_dim(g, own[None], 4 * xi + 2 * yi + ci, axis=0)
        weights1.append(_full_weight(k, g))
    c_w_in, c_w_out, wg1, wu1, wd1 = weights1
    wg, wu, wd = [wg0, wg1], [wu0, wu1], [wd0, wd1]

    hn1, zp, zs = norm_inproj(
        h2, mix_norm[1:2], c_w_in, jnp.zeros((1, c_w_in.shape[1]), F32),
        [(0, POOL_WIDTH), (POOL_WIDTH, c_w_in.shape[1])], [F32, F32], "in_proj1")
    pool = pool_fwd(zp, w_pool_bf, pool_scale, bl, seq, "pool_fwd")
    sgu = sgu_fwd(zs, w_s, b_s_t, sgu_ln, bl, seq, "sgu_fwd")
    h3 = out_proj(h2, pool, sgu, c_w_out, "out_proj1")
    (hnf1, g1, u1), _ = ffn_gate_up(h3, ffn_norm[1:2], wg1, wu1, "ffn_gate_up1")
    h4, _ = ffn_down(h3, g1, u1, wd1, "ffn_down1")

    dh4, d_final_norm, loss_local = loss_head(h4, final_norm, target, "loss_head")

    grads = {}

    def ffn_bwd(layer, dh_out, h_in, hnf, g, u, tag):
        dg, du, act = ffn_down_bwd(dh_out, g, u, wd[layer], "ffn_down_bwd" + tag)
        d_wd = mm_tn(act, dh_out, "dw_down" + tag)
        d_wg = mm_tn(hnf, dg, "dw_gate" + tag)
        d_wu = mm_tn(hnf, du, "dw_up" + tag)
        dh_in, d_gain = proj_rms_bwd([dg, du], [wg[layer], wu[layer]], h_in, ffn_norm[layer:layer + 1], dh_out, 1,
                                     "ffn_up_bwd" + tag, tm_pref=256)
        return dh_in, d_gain, d_wg, d_wu, d_wd

    dh3, d_ffn_norm1, d_wg1, d_wu1, d_wd1 = ffn_bwd(1, dh4, h3, hnf1, g1, u1, "1")

    d_pool, d_sgu = out_proj_bwd(dh3, c_w_out, [F32, F32], "out_proj_bwd1")
    d_c_w_out = jnp.concatenate([mm_tn(pool, dh3, "dw_out1_pool"), mm_tn(sgu, dh3, "dw_out1_sgu")], axis=0)
    dzp, d_w_pool, d_pool_scale = pool_bwd(zp, w_pool_bf, pool_scale, d_pool, bl, seq, "pool_bwd")
    dzs, d_w_s, d_b_s_b, d_sgu_ln = sgu_bwd(zs, w_s, b_s_t, sgu_ln, d_sgu, bl, seq, "sgu_bwd")
    d_c_w_in = jnp.concatenate([mm_tn(hn1, dzp, "dw_in1_pool"), mm_tn(hn1, dzs, "dw_in1_sgu")], axis=1)
    dh2, d_mix_norm1 = proj_rms_bwd([dzp, dzs], [c_w_in[:, :POOL_WIDTH], c_w_in[:, POOL_WIDTH:]], h2,
                                    mix_norm[1:2], dh3, 1, "in_proj_bwd1")

    dh1, d_ffn_norm0, d_wg0, d_wu0, d_wd0 = ffn_bwd(0, dh2, h1, hnf0, g0, u0, "0")

    d_attn, d_conv = out_proj_bwd(dh1, a_w_out, [BF16, F32], "out_proj_bwd0")
    d_a_w_out = jnp.concatenate([mm_tn(attn, dh1, "dw_out0_attn"), mm_tn(conv, dh1, "dw_out0_conv")], axis=0)
    dq, dkv, d_sinks_b = attn_bwd(q, kv, sinks_b, d_attn, bl, seq, "attn_bwd")
    dcc, d_conv_taps, d_conv_vec = conv_bwd(cc, conv_taps, conv_vec, d_conv, bl, seq, "conv_bwd")
    dw_q, db_q = mm_tn(hn0, dq, "dw_in0_q", colsum=True)
    dw_kv, db_kv = mm_tn(hn0, dkv, "dw_in0_kv", colsum=True)
    dw_c, db_c = mm_tn(hn0, dcc, "dw_in0_c", colsum=True)
    d_a_w_in = jnp.concatenate([dw_q, dw_kv, dw_c], axis=1)
    d_a_b_in = jnp.concatenate([db_q, db_kv, db_c], axis=0)
    kq, kk = ATTN_WIDTH, ATTN_WIDTH + 2 * KV_WIDTH
    grad_x, d_mix_norm0 = proj_rms_bwd([dq, dkv, dcc], [a_w_in[:, :kq], a_w_in[:, kq:kk], a_w_in[:, kk:]], x,
                                       mix_norm[0:1], dh1, 1, "in_proj_bwd0")

    big_full = {'a_w_in': [d_a_w_in], 'a_w_out': [d_a_w_out], 'c_w_in': [d_c_w_in], 'c_w_out': [d_c_w_out],
                'ffn_w_gate': [d_wg0, d_wg1], 'ffn_w_up': [d_wu0, d_wu1], 'ffn_w_down': [d_wd0, d_wd1]}
    keep, give = [], []
    for k in BIG:
        if len(big_full[k]) == 2:
            t0, t1 = [_to_shard_major(k, f) for f in big_full[k]]
            keep.append(jnp.where(ci == 0, t0, t1))
            give.append(jnp.where(ci == 0, t1, t0))
        else:
            t = _to_shard_major(k, big_full[k][0])
            h = t.shape[1] // 2
            keep.append(lax.dynamic_slice_in_dim(t, ci * h, h, axis=1))
            give.append(lax.dynamic_slice_in_dim(t, (1 - ci) * h, h, axis=1))
    small_full = {
        'mix_norm': jnp.stack([d_mix_norm0, d_mix_norm1]), 'a_b_in': d_a_b_in[None], 'a_sinks': d_sinks_b[:, 0][None],
        'a_conv_w': d_conv_taps[:CONV_KERNEL][None], 'a_conv_b': d_conv_vec[0][None], 'a_cln_g': d_conv_vec[1][None],
        'a_cln_b': d_conv_vec[2][None], 'c_w_pool': d_w_pool[None], 'c_pool_scale': d_pool_scale[0][None],
        'c_sln_g': d_sgu_ln[0][None], 'c_sln_b': d_sgu_ln[1][None], 'c_w_s': d_w_s[None],
        'c_b_s': d_b_s_b[:, :, 0][None], 'ffn_norm': jnp.stack([d_ffn_norm0, d_ffn_norm1]),
        'final_norm': d_final_norm, 'loss': loss_local.reshape(1)}
    small_names = SMALL_REPL + SMALL_SHARD
    small_pack, small_layout = _pack([small_full[k] for k in small_names + ['loss']])

    got = sibling_swap(give + [small_pack], "reduce_pair")
    pair_sums = [add_cast(kp, gt, "pair_sum_" + k) for k, kp, gt in zip(BIG, keep, got)]
    small_pair = add_cast(small_pack[None], got[-1][None], "pair_sum_small", dtype=F32)[0]
    from_chips, small_chips = chip_scatter(pair_sums, small_pair, "reduce_chips")
    own = [lax.dynamic_index_in_dim(p, chip, axis=0, keepdims=False) for p in pair_sums]
    halves = [sum_parts(p, "chip_sum_" + k, first=o) for k, p, o in zip(BIG, from_chips, own)]
    others = sibling_swap(halves, "reduce_join")
    for k, mine, theirs in zip(BIG, halves, others):
        low = jnp.where(ci == 0, mine, theirs)
        high = jnp.where(ci == 0, theirs, mine)
        grads[k] = jnp.concatenate([low, high], axis=0).reshape(a[k].shape)

    small_sum = sum_parts(small_chips, "small_sum")
    for k, g in zip(small_names + ['loss'], _unpack(small_sum, small_layout)):
        if k in SMALL_SHARD:
            width = a[k].shape[-1]
            g = lax.dynamic_slice_in_dim(g, chip * width, width, axis=g.ndim - 1)
        grads[k] = g
    loss = grads.pop('loss')[0]

    delta, new_m, new_v = {}, {}, {}
    for k in BIG:
        delta[k], new_m[k], new_v[k] = adamw(a[k], grads[k], a['m_' + k], a['v_' + k], "adamw_" + k)
    packs = [_pack([src[k] for k in small_names])
             for src in (a, grads, {k: a['m_' + k] for k in small_names}, {k: a['v_' + k] for k in small_names})]
    d, m, v = adamw(packs[0][0][None], packs[1][0][None], packs[2][0][None], packs[3][0][None], "adamw_small")
    d, m, v = d[0], m[0], v[0]
    lay = packs[0][1]
    for k, dv, mv, vv in zip(small_names, _unpack(d, lay), _unpack(m, lay), _unpack(v, lay)):
        delta[k], new_m[k], new_v[k] = dv, mv, vv

    return (loss, grad_x.reshape(a['x'].shape), *[grads[k] for k in WEIGHTS], *[delta[k] for k in WEIGHTS],
            *[new_m[k] for k in WEIGHTS], *[new_v[k] for k in WEIGHTS])
```

```python
import functools

import jax
import jax.numpy as jnp
from jax import lax
from jax.experimental import pallas as pl
from jax.experimental.pallas import tpu as pltpu

F32 = jnp.float32
BF16 = jnp.bfloat16

D_MODEL = 1024
EPS = 1e-5
N_Q_HEADS, N_KV_HEADS, HEAD_DIM = 8, 2, 64
ATTN_BLOCK = 128
ATTN_WIDTH = N_Q_HEADS * HEAD_DIM
KV_WIDTH = N_KV_HEADS * HEAD_DIM
CONV_WIDTH = 512
CONV_KERNEL = 31
CONV_HALO = 32
POOL_WINDOWS = (2, 4, 8, 16)
POOL_WIDTH = 512
POOL_HALO = 16
SGU_WIDTH = 512
SGU_CHUNK = 128
D_FF = 2816
FF_CHUNK = 128
LANES = 128
N_CHIPS = 4
N_DEV = 8

ADAM_LR, ADAM_B1, ADAM_B2, ADAM_EPS, ADAM_WD, ADAM_STEP = 0.001, 0.9, 0.999, 1e-08, 0.01, 10

VMEM_LIMIT = 56 * 2**20

WEIGHTS = ['mix_norm', 'a_w_in', 'a_b_in', 'a_sinks', 'a_conv_w', 'a_conv_b', 'a_cln_g', 'a_cln_b', 'a_w_out',
           'c_w_in', 'c_w_pool', 'c_pool_scale', 'c_sln_g', 'c_sln_b', 'c_w_s', 'c_b_s', 'c_w_out',
           'ffn_norm', 'ffn_w_gate', 'ffn_w_up', 'ffn_w_down', 'final_norm']
IN_NAMES = (['x'] + WEIGHTS + ['loss_target'] + ['m_' + n for n in WEIGHTS] + ['v_' + n for n in WEIGHTS])


def _params(*sem):
    return pltpu.CompilerParams(dimension_semantics=sem, vmem_limit_bytes=VMEM_LIMIT)


def _dot(a, b):
    return jnp.dot(a, b, preferred_element_type=F32)


def _dot_nt(a, b):
    return lax.dot_general(a, b, (((1,), (1,)), ((), ())), preferred_element_type=F32)


def _dot_tn(a, b):
    return lax.dot_general(a, b, (((0,), (0,)), ((), ())), preferred_element_type=F32)


def _sigmoid(v):
    return 1.0 / (1.0 + jnp.exp(-v))


def _row_tile(n, pref):
    t = min(n, pref)
    while n % t:
        t //= 2
    return t


def _col_tile(m, rows, budget=6 * 2**20):
    best = LANES
    for t in range(LANES, m + 1, LANES):
        if m % t == 0 and rows * t * 4 <= budget:
            best = t
    return best


class Comm:
    def __init__(self, ins, out_shapes, plan, count, aliases=None):
        self.ins, self.out_shapes, self.plan, self.count, self.aliases = ins, out_shapes, plan, count, aliases or {}

    def __add__(self, other):
        ni, no = len(self.ins), len(self.out_shapes)

        def plan(ins, outs, pos):
            return self.plan(ins[:ni], outs[:no], pos) + other.plan(ins[ni:], outs[no:], pos)

        aliases = dict(self.aliases)
        aliases.update({ni + i: no + o for i, o in other.aliases.items()})
        return Comm(list(self.ins) + list(other.ins), list(self.out_shapes) + list(other.out_shapes), plan,
                    self.count + other.count, aliases)

    def split(self, outs, other):
        return outs[:len(self.out_shapes)], outs[len(self.out_shapes):]


def _pcall(body, name, grid, in_specs, out_specs, out_shape, scratch_shapes, args, sem, comm=None):
    single = not isinstance(out_shape, (list, tuple))
    if single:
        out_specs, out_shape = [out_specs], [out_shape]
    if comm is None:
        res = pl.pallas_call(body, name=name, grid=grid, in_specs=in_specs, out_specs=list(out_specs),
                             out_shape=list(out_shape), scratch_shapes=list(scratch_shapes),
                             compiler_params=_params(*sem))(*args)
        return (res[0] if single else res), []
    na, nci, no, nco, ns = len(args), len(comm.ins), len(out_shape), len(comm.out_shapes), len(scratch_shapes)

    def wrapped(*refs):
        a_refs, ci_refs = refs[:na], refs[na:na + nci]
        o_refs, co_refs = refs[na + nci:na + nci + no], refs[na + nci + no:na + nci + no + nco]
        s_refs = refs[na + nci + no + nco:na + nci + no + nco + ns]
        send_sems, recv_sems = refs[-2], refs[-1]
        pos = _me()

        def copies():
            return [pltpu.make_async_remote_copy(src_ref=s, dst_ref=d, send_sem=send_sems.at[i],
                                                 recv_sem=recv_sems.at[i], device_id=_flip(pos, rel),
                                                 device_id_type=MESH)
                    for i, (rel, s, d) in enumerate(comm.plan(ci_refs, co_refs, pos))]

        first, last = None, None
        for d, size in enumerate(grid):
            f, l = pl.program_id(d) == 0, pl.program_id(d) == size - 1
            first = f if first is None else first & f
            last = l if last is None else last & l

        @pl.when(first)
        def _():
            for cp in copies():
                cp.start()

        body(*a_refs, *o_refs, *s_refs)

        @pl.when(last)
        def _():
            for cp in copies():
                cp.wait()

    res = pl.pallas_call(
        wrapped, name=name, grid=grid,
        in_specs=list(in_specs) + [ANY] * nci, out_specs=list(out_specs) + [ANY] * nco,
        out_shape=list(out_shape) + [jax.ShapeDtypeStruct(s, d) for s, d in comm.out_shapes],
        scratch_shapes=list(scratch_shapes) + [pltpu.SemaphoreType.DMA((comm.count,)),
                                               pltpu.SemaphoreType.DMA((comm.count,))],
        input_output_aliases={na + i: no + o for i, o in comm.aliases.items()},
        compiler_params=_params(*(["arbitrary"] * len(grid))),
    )(*args, *comm.ins)
    outs = res[:no]
    return (outs[0] if single else outs), list(res[no:])


def norm_inproj(x, gain, w, bias, splits, dtypes, name, comm=None):
    n = x.shape[0]
    m = w.shape[1]
    tm = _row_tile(n, 512)

    def body(x_ref, g_ref, w_ref, b_ref, hn_ref, *outs):
        xv = x_ref[...]
        r = lax.rsqrt(jnp.mean(xv * xv, axis=-1, keepdims=True) + EPS)
        hn = ((xv * r) * g_ref[...]).astype(BF16)
        hn_ref[...] = hn
        z = _dot(hn, w_ref[...]) + b_ref[...]
        for o, (lo, hi) in zip(outs, splits):
            o[...] = z[:, lo:hi].astype(o.dtype)

    out_shape = [jax.ShapeDtypeStruct((n, D_MODEL), BF16)]
    out_specs = [pl.BlockSpec((tm, D_MODEL), lambda i: (i, 0))]
    for (lo, hi), dt in zip(splits, dtypes):
        out_shape.append(jax.ShapeDtypeStruct((n, hi - lo), dt))
        out_specs.append(pl.BlockSpec((tm, hi - lo), lambda i: (i, 0)))
    return _pcall(
        body, name, (n // tm,),
        [pl.BlockSpec((tm, D_MODEL), lambda i: (i, 0)),
         pl.BlockSpec((1, D_MODEL), lambda i: (0, 0)),
         pl.BlockSpec((D_MODEL, m), lambda i: (0, 0)),
         pl.BlockSpec((1, m), lambda i: (0, 0))],
        out_specs, out_shape, [], (x, gain, w, bias), ("parallel",), comm)


def out_proj(res, m1, m2, w, name, comm=None):
    n = res.shape[0]
    k1, k2 = m1.shape[1], m2.shape[1]
    assert k1 == k2
    tm = _row_tile(n, 512)

    def body(r_ref, a_ref, b_ref, w1_ref, w2_ref, o_ref):
        o_ref[...] = r_ref[...] + _dot(a_ref[...], w1_ref[...]) + _dot(b_ref[...], w2_ref[...])

    return _pcall(
        body, name, (n // tm,),
        [pl.BlockSpec((tm, D_MODEL), lambda i: (i, 0)),
         pl.BlockSpec((tm, k1), lambda i: (i, 0)),
         pl.BlockSpec((tm, k2), lambda i: (i, 0)),
         pl.BlockSpec((k1, D_MODEL), lambda i: (0, 0)),
         pl.BlockSpec((k2, D_MODEL), lambda i: (1, 0))],
        pl.BlockSpec((tm, D_MODEL), lambda i: (i, 0)),
        jax.ShapeDtypeStruct((n, D_MODEL), F32), [], (res, m1, m2, w, w), ("parallel",), comm)


def ffn_gate_up(h, gain, wg, wu, name, comm=None):
    n = h.shape[0]
    tm = _row_tile(n, 1024)
    th = D_FF // 2

    def body(h_ref, g_ref, wg_ref, wu_ref, hn_ref, go_ref, uo_ref):
        @pl.when(pl.program_id(1) == 0)
        def _():
            xv = h_ref[...]
            r = lax.rsqrt(jnp.mean(xv * xv, axis=-1, keepdims=True) + EPS)
            hn_ref[...] = ((xv * r) * g_ref[...]).astype(BF16)

        hn = hn_ref[...]
        go_ref[...] = _dot(hn, wg_ref[...]).astype(BF16)
        uo_ref[...] = _dot(hn, wu_ref[...]).astype(BF16)

    return _pcall(
        body, name, (n // tm, D_FF // th),
        [pl.BlockSpec((tm, D_MODEL), lambda i, j: (i, 0)),
         pl.BlockSpec((1, D_MODEL), lambda i, j: (0, 0)),
         pl.BlockSpec((D_MODEL, th), lambda i, j: (0, j)),
         pl.BlockSpec((D_MODEL, th), lambda i, j: (0, j))],
        [pl.BlockSpec((tm, D_MODEL), lambda i, j: (i, 0)),
         pl.BlockSpec((tm, th), lambda i, j: (i, j)),
         pl.BlockSpec((tm, th), lambda i, j: (i, j))],
        [jax.ShapeDtypeStruct((n, D_MODEL), BF16),
         jax.ShapeDtypeStruct((n, D_FF), BF16),
         jax.ShapeDtypeStruct((n, D_FF), BF16)],
        [], (h, gain, wg, wu), ("parallel", "arbitrary"), comm)


def ffn_down(h, g, u, wd, name, comm=None):
    n = h.shape[0]
    tm = _row_tile(n, 512)

    def body(h_ref, g_ref, u_ref, w_ref, o_ref, a_ref):
        for c0 in range(0, D_FF, FF_CHUNK):
            gv = g_ref[:, c0:c0 + FF_CHUNK].astype(F32)
            a_ref[:, c0:c0 + FF_CHUNK] = (gv * _sigmoid(gv) * u_ref[:, c0:c0 + FF_CHUNK].astype(F32)).astype(BF16)
        o_ref[...] = h_ref[...] + _dot(a_ref[...], w_ref[...])

    return _pcall(
        body, name, (n // tm,),
        [pl.BlockSpec((tm, D_MODEL), lambda i: (i, 0)),
         pl.BlockSpec((tm, D_FF), lambda i: (i, 0)),
         pl.BlockSpec((tm, D_FF), lambda i: (i, 0)),
         pl.BlockSpec((D_FF, D_MODEL), lambda i: (0, 0))],
        pl.BlockSpec((tm, D_MODEL), lambda i: (i, 0)),
        jax.ShapeDtypeStruct((n, D_MODEL), F32),
        [pltpu.VMEM((tm, D_FF), BF16)], (h, g, u, wd), ("parallel",), comm)


def ffn_down_bwd(dh, g, u, wd, name, comm=None):
    n = dh.shape[0]
    tm = _row_tile(n, 512)
    th = D_FF // 2

    def body(dh_ref, g_ref, u_ref, w_ref, dg_ref, du_ref, a_ref, da_ref):
        da_ref[...] = _dot_nt(dh_ref[...].astype(BF16), w_ref[...])
        for c0 in range(0, th, FF_CHUNK):
            cols = slice(c0, c0 + FF_CHUNK)
            da = da_ref[:, cols]
            gv = g_ref[:, cols].astype(F32)
            uv = u_ref[:, cols].astype(F32)
            sg = _sigmoid(gv)
            act = gv * sg
            dg_ref[:, cols] = (da * uv * (sg * (1.0 + gv * (1.0 - sg)))).astype(BF16)
            du_ref[:, cols] = (da * act).astype(BF16)
            a_ref[:, cols] = (act * uv).astype(BF16)

    spec_h = pl.BlockSpec((tm, th), lambda i, j: (i, j))
    return _pcall(
        body, name, (n // tm, D_FF // th),
        [pl.BlockSpec((tm, D_MODEL), lambda i, j: (i, 0)), spec_h, spec_h,
         pl.BlockSpec((th, D_MODEL), lambda i, j: (j, 0))],
        [spec_h, spec_h, spec_h], [jax.ShapeDtypeStruct((n, D_FF), BF16)] * 3,
        [pltpu.VMEM((tm, th), F32)], (dh, g, u, wd), ("parallel", "arbitrary"), comm)


def mm_tn(x, dy, name, colsum=False):
    n, k = x.shape
    m = dy.shape[1]
    tn = _col_tile(m, k)
    tt = _row_tile(n, 1024)

    def body(x_ref, dy_ref, o_ref, *rest):
        t = pl.program_id(1)
        dyv = dy_ref[...]
        part = _dot_tn(x_ref[...], dyv.astype(BF16))

        @pl.when(t == 0)
        def _():
            o_ref[...] = part

        @pl.when(t > 0)
        def _():
            o_ref[...] += part

        if colsum:
            cs = jnp.sum(dyv.astype(F32), axis=0, keepdims=True)

            @pl.when(t == 0)
            def _():
                rest[0][...] = jnp.broadcast_to(cs, rest[0].shape)

            @pl.when(t > 0)
            def _():
                rest[0][...] += jnp.broadcast_to(cs, rest[0].shape)

    out_shape = [jax.ShapeDtypeStruct((k, m), F32)]
    out_specs = [pl.BlockSpec((k, tn), lambda j, t: (0, j))]
    if colsum:
        out_shape.append(jax.ShapeDtypeStruct((8, m), F32))
        out_specs.append(pl.BlockSpec((8, tn), lambda j, t: (0, j)))
    res = pl.pallas_call(
        body, name=name, grid=(m // tn, n // tt),
        in_specs=[pl.BlockSpec((tt, k), lambda j, t: (t, 0)),
                  pl.BlockSpec((tt, tn), lambda j, t: (t, j))],
        out_specs=out_specs, out_shape=out_shape,
        compiler_params=_params("parallel", "arbitrary"),
    )(x, dy)
    return (res[0], res[1][0]) if colsum else res[0]


def out_proj_bwd(dh, w, dtypes, name):
    n = dh.shape[0]
    k = w.shape[0]
    half = k // 2
    tm = _row_tile(n, 512)

    def body(dh_ref, w_ref, a_ref, b_ref):
        dm = _dot_nt(dh_ref[...].astype(BF16), w_ref[...])
        a_ref[...] = dm[:, :half].astype(a_ref.dtype)
        b_ref[...] = dm[:, half:].astype(b_ref.dtype)

    return pl.pallas_call(
        body, name=name, grid=(n // tm,),
        in_specs=[pl.BlockSpec((tm, D_MODEL), lambda i: (i, 0)),
                  pl.BlockSpec((k, D_MODEL), lambda i: (0, 0))],
        out_specs=[pl.BlockSpec((tm, half), lambda i: (i, 0))] * 2,
        out_shape=[jax.ShapeDtypeStruct((n, half), dtypes[0]), jax.ShapeDtypeStruct((n, half), dtypes[1])],
        compiler_params=_params("parallel"),
    )(dh, w)


def proj_rms_bwd(dys, ws, h_in, gain, dres, nk, name, tm_pref=512, comm=None):
    n = h_in.shape[0]
    npair = len(dys)
    tm = _row_tile(n, tm_pref)
    tks = [dy.shape[1] // nk for dy in dys]

    def body(*refs):
        dy_refs = refs[:npair]
        w_refs = refs[npair:2 * npair]
        h_ref, g_ref, dr_ref, o_ref, dg_ref, acc_ref = refs[2 * npair:]
        i, k = pl.program_id(0), pl.program_id(1)
        part = _dot_nt(dy_refs[0][...], w_refs[0][...])
        for p in range(1, npair):
            part = part + _dot_nt(dy_refs[p][...], w_refs[p][...])

        @pl.when(k == 0)
        def _():
            acc_ref[...] = part

        @pl.when(k > 0)
        def _():
            acc_ref[...] += part

        @pl.when(k == nk - 1)
        def _():
            dhn = acc_ref[...]
            xv = h_ref[...]
            r = lax.rsqrt(jnp.mean(xv * xv, axis=-1, keepdims=True) + EPS)
            xh = xv * r
            uv = dhn * g_ref[...]
            o_ref[...] = dr_ref[...] + r * (uv - xh * jnp.mean(uv * xh, axis=-1, keepdims=True))
            dgp = jnp.broadcast_to(jnp.sum(dhn * xh, axis=0, keepdims=True), dg_ref.shape)

            @pl.when(i == 0)
            def _():
                dg_ref[...] = dgp

            @pl.when(i > 0)
            def _():
                dg_ref[...] += dgp

    row = pl.BlockSpec((tm, D_MODEL), lambda i, k: (i, 0))
    in_specs = [pl.BlockSpec((tm, tk), lambda i, k: (i, k)) for tk in tks]
    in_specs += [pl.BlockSpec((D_MODEL, tk), lambda i, k: (0, k)) for tk in tks]
    in_specs += [row, pl.BlockSpec((1, D_MODEL), lambda i, k: (0, 0)), row]
    (dh, dgain), comm_outs = _pcall(
        body, name, (n // tm, nk), in_specs,
        [row, pl.BlockSpec((8, D_MODEL), lambda i, k: (0, 0))],
        [jax.ShapeDtypeStruct((n, D_MODEL), F32), jax.ShapeDtypeStruct((8, D_MODEL), F32)],
        [pltpu.VMEM((tm, D_MODEL), F32)], (*dys, *ws, h_in, gain, dres), ("arbitrary", "arbitrary"), comm)
    return dh, dgain[0], comm_outs


def loss_head(h, gain, target, name):
    n = h.shape[0]
    tm = _row_tile(n, 512)

    def body(h_ref, g_ref, t_ref, dh_ref, dg_ref, l_ref):
        i = pl.program_id(0)
        xv = h_ref[...]
        r = lax.rsqrt(jnp.mean(xv * xv, axis=-1, keepdims=True) + EPS)
        xh = xv * r
        err = xh * g_ref[...] - t_ref[...]
        dy = err * (1.0 / D_MODEL)
        uv = dy * g_ref[...]
        dh_ref[...] = r * (uv - xh * jnp.mean(uv * xh, axis=-1, keepdims=True))
        dgp = jnp.broadcast_to(jnp.sum(dy * xh, axis=0, keepdims=True), dg_ref.shape)
        lp = jnp.sum(jnp.sum(err * err, axis=-1, keepdims=True), axis=0, keepdims=True) * (0.5 / D_MODEL)
        lp = jnp.broadcast_to(lp, l_ref.shape)

        @pl.when(i == 0)
        def _():
            dg_ref[...] = dgp
            l_ref[...] = lp

        @pl.when(i > 0)
        def _():
            dg_ref[...] += dgp
            l_ref[...] += lp

    row = pl.BlockSpec((tm, D_MODEL), lambda i: (i, 0))
    dh, dg, l = pl.pallas_call(
        body, name=name, grid=(n // tm,),
        in_specs=[row, pl.BlockSpec((1, D_MODEL), lambda i: (0, 0)), row],
        out_specs=[row, pl.BlockSpec((8, D_MODEL), lambda i: (0, 0)), pl.BlockSpec((8, LANES), lambda i: (0, 0))],
        out_shape=[jax.ShapeDtypeStruct((n, D_MODEL), F32), jax.ShapeDtypeStruct((8, D_MODEL), F32),
                   jax.ShapeDtypeStruct((8, LANES), F32)],
        compiler_params=_params("arbitrary"),
    )(h, gain, target)
    return dh, dg[0], l[0, 0]


GROUP = N_Q_HEADS // N_KV_HEADS
GQ = GROUP * ATTN_BLOCK


def _attn_mask_t(n):
    r = lax.broadcasted_iota(jnp.int32, (2 * ATTN_BLOCK, GQ), 0)
    qi = lax.broadcasted_iota(jnp.int32, (2 * ATTN_BLOCK, GQ), 1) & (ATTN_BLOCK - 1)
    band = (r > qi) & (r <= qi + ATTN_BLOCK)
    return band & ((r >= ATTN_BLOCK) | (n > 0))


def _stack_heads(blk, kh):
    return jnp.concatenate([blk[:, (kh * GROUP + g) * HEAD_DIM:(kh * GROUP + g + 1) * HEAD_DIM]
                            for g in range(GROUP)], axis=0)


def _attn_probs_t(kk, qs, mask, sink):
    s = _dot_nt(kk, qs) * (HEAD_DIM ** -0.5)
    s = jnp.where(mask, s, -1e30)
    m = jnp.maximum(jnp.max(s, axis=0, keepdims=True), sink)
    p = jnp.exp(s - m)
    esink = jnp.exp(sink - m)
    inv = 1.0 / (jnp.sum(p, axis=0, keepdims=True) + esink)
    return p * inv, esink * inv


def attn_fwd(q, kv, sinks_t, nseq, seq, name, comm=None):
    nb = seq // ATTN_BLOCK

    def body(q_ref, kv_ref, s_ref, o_ref, kvp):
        kvp[0:ATTN_BLOCK, :] = jnp.zeros((ATTN_BLOCK, 2 * KV_WIDTH), BF16)
        kvp[ATTN_BLOCK:, :] = kv_ref[...]

        def blk(n, carry):
            st = pl.multiple_of(n * ATTN_BLOCK, ATTN_BLOCK)
            qb = q_ref[pl.ds(st, ATTN_BLOCK), :]
            kw = kvp[pl.ds(st, 2 * ATTN_BLOCK), :]
            mask = _attn_mask_t(n)
            for kh in range(N_KV_HEADS):
                kk = kw[:, kh * HEAD_DIM:(kh + 1) * HEAD_DIM]
                vv = kw[:, KV_WIDTH + kh * HEAD_DIM:KV_WIDTH + (kh + 1) * HEAD_DIM]
                probs, _ = _attn_probs_t(kk, _stack_heads(qb, kh), mask, s_ref[kh:kh + 1, :])
                ot = _dot_tn(vv, probs.astype(BF16))
                for pair in range(GROUP // 2):
                    two = jnp.concatenate([ot[:, (2 * pair) * ATTN_BLOCK:(2 * pair + 1) * ATTN_BLOCK],
                                           ot[:, (2 * pair + 1) * ATTN_BLOCK:(2 * pair + 2) * ATTN_BLOCK]], axis=0)
                    col = (kh * GROUP + 2 * pair) * HEAD_DIM
                    o_ref[pl.ds(st, ATTN_BLOCK), col:col + 2 * HEAD_DIM] = two.T.astype(o_ref.dtype)
            return carry

        lax.fori_loop(0, nb, blk, 0)

    return _pcall(
        body, name, (nseq,),
        [pl.BlockSpec((seq, ATTN_WIDTH), lambda b: (b, 0)),
         pl.BlockSpec((seq, 2 * KV_WIDTH), lambda b: (b, 0)),
         pl.BlockSpec((8, GQ), lambda b: (0, 0))],
        pl.BlockSpec((seq, ATTN_WIDTH), lambda b: (b, 0)),
        jax.ShapeDtypeStruct((nseq * seq, ATTN_WIDTH), BF16),
        [pltpu.VMEM((ATTN_BLOCK + seq, 2 * KV_WIDTH), BF16)], (q, kv, sinks_t), ("parallel",), comm)


def attn_bwd(q, kv, sinks_t, do, nseq, seq, name, comm=None):
    nb = seq // ATTN_BLOCK

    def body(q_ref, kv_ref, s_ref, do_ref, dq_ref, dkv_ref, ds_ref, kvp, dkvp, dsacc):
        @pl.when(pl.program_id(0) == 0)
        def _():
            dsacc[...] = jnp.zeros(dsacc.shape, F32)

        kvp[0:ATTN_BLOCK, :] = jnp.zeros((ATTN_BLOCK, 2 * KV_WIDTH), BF16)
        kvp[ATTN_BLOCK:, :] = kv_ref[...]
        dkvp[...] = jnp.zeros(dkvp.shape, F32)

        def blk(n, carry):
            st = pl.multiple_of(n * ATTN_BLOCK, ATTN_BLOCK)
            qb = q_ref[pl.ds(st, ATTN_BLOCK), :]
            dob = do_ref[pl.ds(st, ATTN_BLOCK), :]
            kw = kvp[pl.ds(st, 2 * ATTN_BLOCK), :]
            mask = _attn_mask_t(n)
            for kh in range(N_KV_HEADS):
                kk = kw[:, kh * HEAD_DIM:(kh + 1) * HEAD_DIM]
                vv = kw[:, KV_WIDTH + kh * HEAD_DIM:KV_WIDTH + (kh + 1) * HEAD_DIM]
                qs = _stack_heads(qb, kh)
                dos = _stack_heads(dob, kh)
                probs, psink = _attn_probs_t(kk, qs, mask, s_ref[kh:kh + 1, :])
                dp = _dot_nt(vv, dos)
                dv = _dot(probs.astype(BF16), dos)
                rowdot = jnp.sum(probs * dp, axis=0, keepdims=True)
                dsc = (probs * (dp - rowdot) * (HEAD_DIM ** -0.5)).astype(BF16)
                dsacc[kh:kh + 1, :] += -psink * rowdot
                dk = _dot(dsc, qs)
                dqs = _dot_tn(dsc, kk)
                for g in range(GROUP):
                    col = (kh * GROUP + g) * HEAD_DIM
                    dq_ref[pl.ds(st, ATTN_BLOCK), col:col + HEAD_DIM] = (
                        dqs[g * ATTN_BLOCK:(g + 1) * ATTN_BLOCK].astype(dq_ref.dtype))
                dkvp[pl.ds(st, 2 * ATTN_BLOCK), kh * HEAD_DIM:(kh + 1) * HEAD_DIM] += dk
                dkvp[pl.ds(st, 2 * ATTN_BLOCK), KV_WIDTH + kh * HEAD_DIM:KV_WIDTH + (kh + 1) * HEAD_DIM] += dv
            return carry

        lax.fori_loop(0, nb, blk, 0)
        dkv_ref[...] = dkvp[ATTN_BLOCK:, :].astype(dkv_ref.dtype)

        @pl.when(pl.program_id(0) == nseq - 1)
        def _():
            for kh in range(N_KV_HEADS):
                for g in range(GROUP):
                    tot = jnp.sum(dsacc[kh:kh + 1, g * ATTN_BLOCK:(g + 1) * ATTN_BLOCK], axis=1, keepdims=True)
                    ds_ref[kh * GROUP + g:kh * GROUP + g + 1, :] = jnp.broadcast_to(tot, (1, LANES))

    seq_q = pl.BlockSpec((seq, ATTN_WIDTH), lambda b: (b, 0))
    seq_kv = pl.BlockSpec((seq, 2 * KV_WIDTH), lambda b: (b, 0))
    return _pcall(
        body, name, (nseq,),
        [seq_q, seq_kv, pl.BlockSpec((8, GQ), lambda b: (0, 0)), seq_q],
        [seq_q, seq_kv, pl.BlockSpec((N_Q_HEADS, LANES), lambda b: (0, 0))],
        [jax.ShapeDtypeStruct((nseq * seq, ATTN_WIDTH), BF16),
         jax.ShapeDtypeStruct((nseq * seq, 2 * KV_WIDTH), BF16),
         jax.ShapeDtypeStruct((N_Q_HEADS, LANES), F32)],
        [pltpu.VMEM((ATTN_BLOCK + seq, 2 * KV_WIDTH), BF16),
         pltpu.VMEM((ATTN_BLOCK + seq, 2 * KV_WIDTH), F32),
         pltpu.VMEM((8, GQ), F32)], (q, kv, sinks_t, do), ("arbitrary",), comm)


CONV_T = 128


def _conv_taps(win, w_ref, lanes, init):
    acc = init
    for j in range(CONV_KERNEL):
        sh = win if j == CONV_KERNEL - 1 else pltpu.roll(win, CONV_KERNEL - 1 - j, 0)
        acc = acc + w_ref[j:j + 1, lanes] * sh[CONV_HALO:CONV_HALO + CONV_T]
    return acc


def _conv_block(h0p, w_ref, vec_ref, st):
    cols = []
    for cs in range(CONV_WIDTH // LANES):
        lanes = slice(cs * LANES, (cs + 1) * LANES)
        win = h0p[pl.ds(st, CONV_T + CONV_HALO), lanes]
        init = jnp.broadcast_to(vec_ref[0:1, lanes], (CONV_T, LANES))
        cols.append(_conv_taps(win, w_ref, lanes, init))
    return jnp.concatenate(cols, axis=-1)


def _glu_store(c_ref, h0p, st):
    cb = c_ref[pl.ds(st, CONV_T), :]
    h0p[pl.ds(pl.multiple_of(st + CONV_HALO, CONV_HALO), CONV_T), :] = cb[:, :CONV_WIDTH] * _sigmoid(cb[:, CONV_WIDTH:])


def conv_fwd(c, w, vec, nseq, seq, name, comm=None):
    nb = seq // CONV_T

    def body(c_ref, w_ref, vec_ref, o_ref, h0p):
        h0p[0:CONV_HALO, :] = jnp.zeros((CONV_HALO, CONV_WIDTH), F32)

        def blk(n, carry):
            st = pl.multiple_of(n * CONV_T, CONV_T)
            _glu_store(c_ref, h0p, st)
            h1 = _conv_block(h0p, w_ref, vec_ref, st)
            mu = jnp.mean(h1, axis=-1, keepdims=True)
            xc = h1 - mu
            rstd = lax.rsqrt(jnp.mean(xc * xc, axis=-1, keepdims=True) + EPS)
            y = xc * rstd * vec_ref[1:2, :] + vec_ref[2:3, :]
            o_ref[pl.ds(st, CONV_T), :] = (y * _sigmoid(y)).astype(o_ref.dtype)
            return carry

        lax.fori_loop(0, nb, blk, 0)

    return _pcall(
        body, name, (nseq,),
        [pl.BlockSpec((seq, 2 * CONV_WIDTH), lambda b: (b, 0)),
         pl.BlockSpec((32, CONV_WIDTH), lambda b: (0, 0)),
         pl.BlockSpec((8, CONV_WIDTH), lambda b: (0, 0))],
        pl.BlockSpec((seq, CONV_WIDTH), lambda b: (b, 0)),
        jax.ShapeDtypeStruct((nseq * seq, CONV_WIDTH), BF16),
        [pltpu.VMEM((CONV_HALO + seq, CONV_WIDTH), F32)], (c, w, vec), ("parallel",), comm)


def conv_bwd(c, w, vec, dout, nseq, seq, name, comm=None):
    nb = seq // CONV_T

    def body(c_ref, w_ref, vec_ref, do_ref, dc_ref, dw_ref, dvec_ref, h0p, dh1p):
        @pl.when(pl.program_id(0) == 0)
        def _():
            dw_ref[...] = jnp.zeros(dw_ref.shape, F32)
            dvec_ref[...] = jnp.zeros(dvec_ref.shape, F32)

        h0p[0:CONV_HALO, :] = jnp.zeros((CONV_HALO, CONV_WIDTH), F32)
        dh1p[seq:seq + CONV_HALO, :] = jnp.zeros((CONV_HALO, CONV_WIDTH), F32)

        def pass_a(n, carry):
            st = pl.multiple_of(n * CONV_T, CONV_T)
            _glu_store(c_ref, h0p, st)
            h1 = _conv_block(h0p, w_ref, vec_ref, st)
            mu = jnp.mean(h1, axis=-1, keepdims=True)
            xc = h1 - mu
            rstd = lax.rsqrt(jnp.mean(xc * xc, axis=-1, keepdims=True) + EPS)
            xh = xc * rstd
            y = xh * vec_ref[1:2, :] + vec_ref[2:3, :]
            sg = _sigmoid(y)
            dy = do_ref[pl.ds(st, CONV_T), :] * (sg * (1.0 + y * (1.0 - sg)))
            dvec_ref[1:2, :] += jnp.sum(dy * xh, axis=0, keepdims=True)
            dvec_ref[2:3, :] += jnp.sum(dy, axis=0, keepdims=True)
            dxh = dy * vec_ref[1:2, :]
            dh1 = rstd * (dxh - jnp.mean(dxh, axis=-1, keepdims=True)
                          - xh * jnp.mean(dxh * xh, axis=-1, keepdims=True))
            dvec_ref[0:1, :] += jnp.sum(dh1, axis=0, keepdims=True)
            dh1p[pl.ds(st, CONV_T), :] = dh1
            return carry

        lax.fori_loop(0, nb, pass_a, 0)

        def pass_b(n, carry):
            st = pl.multiple_of(n * CONV_T, CONV_T)
            cols = []
            for cs in range(CONV_WIDTH // LANES):
                lanes = slice(cs * LANES, (cs + 1) * LANES)
                wind = dh1p[pl.ds(st, CONV_T + CONV_HALO), lanes]
                winh = h0p[pl.ds(st, CONV_T + CONV_HALO), lanes]
                d1 = wind[0:CONV_T]
                acc = jnp.zeros((CONV_T, LANES), F32)
                for j in range(CONV_KERNEL):
                    acc = acc + w_ref[j:j + 1, lanes] * pltpu.roll(wind, 2 + j, 0)[CONV_HALO:CONV_HALO + CONV_T]
                    hs = winh if j == CONV_KERNEL - 1 else pltpu.roll(winh, CONV_KERNEL - 1 - j, 0)
                    dw_ref[j:j + 1, lanes] += jnp.sum(d1 * hs[CONV_HALO:CONV_HALO + CONV_T], axis=0, keepdims=True)
                cols.append(acc)
            dh0 = jnp.concatenate(cols, axis=-1)
            cb = c_ref[pl.ds(st, CONV_T), :]
            av, gt = cb[:, :CONV_WIDTH], cb[:, CONV_WIDTH:]
            sg = _sigmoid(gt)
            dc_ref[pl.ds(st, CONV_T), :] = jnp.concatenate(
                [dh0 * sg, dh0 * av * sg * (1.0 - sg)], axis=-1).astype(dc_ref.dtype)
            return carry

        lax.fori_loop(0, nb, pass_b, 0)

    return _pcall(
        body, name, (nseq,),
        [pl.BlockSpec((seq, 2 * CONV_WIDTH), lambda b: (b, 0)),
         pl.BlockSpec((32, CONV_WIDTH), lambda b: (0, 0)),
         pl.BlockSpec((8, CONV_WIDTH), lambda b: (0, 0)),
         pl.BlockSpec((seq, CONV_WIDTH), lambda b: (b, 0))],
        [pl.BlockSpec((seq, 2 * CONV_WIDTH), lambda b: (b, 0)),
         pl.BlockSpec((32, CONV_WIDTH), lambda b: (0, 0)),
         pl.BlockSpec((8, CONV_WIDTH), lambda b: (0, 0))],
        [jax.ShapeDtypeStruct((nseq * seq, 2 * CONV_WIDTH), BF16),
         jax.ShapeDtypeStruct((32, CONV_WIDTH), F32),
         jax.ShapeDtypeStruct((8, CONV_WIDTH), F32)],
        [pltpu.VMEM((CONV_HALO + seq, CONV_WIDTH), F32),
         pltpu.VMEM((seq + CONV_HALO, CONV_WIDTH), F32)], (c, w, vec, dout), ("arbitrary",), comm)


POOL_T = 128


def _pooled_block(zpp, st, grp):
    lanes = slice(grp * LANES, (grp + 1) * LANES)
    win = zpp[pl.ds(st, POOL_T + POOL_HALO), lanes]
    acc = win
    for lvl in range(grp + 1):
        acc = acc + pltpu.roll(acc, 1 << lvl, 0)
    t = st + lax.broadcasted_iota(jnp.int32, (POOL_T, 1), 0)
    inv = 1.0 / jnp.minimum(t + 1, POOL_WINDOWS[grp]).astype(F32)
    return acc[POOL_HALO:] * inv - win[POOL_HALO:], inv


def pool_fwd(zp, wp, scale, nseq, seq, name):
    nb = seq // POOL_T

    def body(z_ref, wp_ref, sc_ref, o_ref, zpp):
        zpp[0:POOL_HALO, :] = jnp.zeros((POOL_HALO, POOL_WIDTH), F32)
        zpp[POOL_HALO:, :] = z_ref[...]

        def blk(n, carry):
            st = pl.multiple_of(n * POOL_T, POOL_T)
            for grp in range(len(POOL_WINDOWS)):
                lanes = slice(grp * LANES, (grp + 1) * LANES)
                pooled, _ = _pooled_block(zpp, st, grp)
                o_ref[pl.ds(st, POOL_T), lanes] = (
                    _dot(pooled.astype(BF16), wp_ref[grp]) * sc_ref[0:1, lanes]).astype(o_ref.dtype)
            return carry

        lax.fori_loop(0, nb, blk, 0)

    return pl.pallas_call(
        body, name=name, grid=(nseq,),
        in_specs=[pl.BlockSpec((seq, POOL_WIDTH), lambda b: (b, 0)),
                  pl.BlockSpec((4, LANES, LANES), lambda b: (0, 0, 0)),
                  pl.BlockSpec((1, POOL_WIDTH), lambda b: (0, 0))],
        out_specs=pl.BlockSpec((seq, POOL_WIDTH), lambda b: (b, 0)),
        out_shape=jax.ShapeDtypeStruct((nseq * seq, POOL_WIDTH), BF16),
        scratch_shapes=[pltpu.VMEM((POOL_HALO + seq, POOL_WIDTH), F32)],
        compiler_params=_params("parallel"),
    )(zp, wp, scale)


def pool_bwd(zp, wp, scale, dout, nseq, seq, name):
    nb = seq // POOL_T

    def body(z_ref, wp_ref, sc_ref, do_ref, dz_ref, dwp_ref, dsc_ref, zpp, dpcp, negd):
        @pl.when(pl.program_id(0) == 0)
        def _():
            dwp_ref[...] = jnp.zeros(dwp_ref.shape, F32)
            dsc_ref[...] = jnp.zeros(dsc_ref.shape, F32)

        zpp[0:POOL_HALO, :] = jnp.zeros((POOL_HALO, POOL_WIDTH), F32)
        zpp[POOL_HALO:, :] = z_ref[...]
        dpcp[seq:seq + POOL_HALO, :] = jnp.zeros((POOL_HALO, POOL_WIDTH), F32)

        def pass_a(n, carry):
            st = pl.multiple_of(n * POOL_T, POOL_T)
            for grp in range(len(POOL_WINDOWS)):
                lanes = slice(grp * LANES, (grp + 1) * LANES)
                pooled, inv = _pooled_block(zpp, st, grp)
                pb = pooled.astype(BF16)
                dob = do_ref[pl.ds(st, POOL_T), lanes]
                dsc_ref[0:1, lanes] += jnp.sum(dob * _dot(pb, wp_ref[grp]), axis=0, keepdims=True)
                dpm = (dob * sc_ref[0:1, lanes]).astype(BF16)
                dwp_ref[grp] += _dot_tn(pb, dpm)
                dpooled = _dot_nt(dpm, wp_ref[grp])
                negd[pl.ds(st, POOL_T), lanes] = -dpooled
                dpcp[pl.ds(st, POOL_T), lanes] = dpooled * inv
            return carry

        lax.fori_loop(0, nb, pass_a, 0)

        def pass_b(n, carry):
            st = pl.multiple_of(n * POOL_T, POOL_T)
            rows = POOL_T + POOL_HALO
            for grp in range(len(POOL_WINDOWS)):
                lanes = slice(grp * LANES, (grp + 1) * LANES)
                acc = dpcp[pl.ds(st, rows), lanes]
                for lvl in range(grp + 1):
                    acc = acc + pltpu.roll(acc, rows - (1 << lvl), 0)
                dz_ref[pl.ds(st, POOL_T), lanes] = (acc[0:POOL_T] + negd[pl.ds(st, POOL_T), lanes]).astype(dz_ref.dtype)
            return carry

        lax.fori_loop(0, nb, pass_b, 0)

    seq_spec = pl.BlockSpec((seq, POOL_WIDTH), lambda b: (b, 0))
    return pl.pallas_call(
        body, name=name, grid=(nseq,),
        in_specs=[seq_spec, pl.BlockSpec((4, LANES, LANES), lambda b: (0, 0, 0)),
                  pl.BlockSpec((1, POOL_WIDTH), lambda b: (0, 0)), seq_spec],
        out_specs=[seq_spec, pl.BlockSpec((4, LANES, LANES), lambda b: (0, 0, 0)),
                   pl.BlockSpec((8, POOL_WIDTH), lambda b: (0, 0))],
        out_shape=[jax.ShapeDtypeStruct((nseq * seq, POOL_WIDTH), BF16),
                   jax.ShapeDtypeStruct((4, LANES, LANES), F32),
                   jax.ShapeDtypeStruct((8, POOL_WIDTH), F32)],
        scratch_shapes=[pltpu.VMEM((POOL_HALO + seq, POOL_WIDTH), F32),
                        pltpu.VMEM((seq + POOL_HALO, POOL_WIDTH), F32),
                        pltpu.VMEM((seq, POOL_WIDTH), F32)],
        compiler_params=_params("arbitrary"),
    )(zp, wp, scale, dout)


GELU_C0 = 0.7978845608028654
GELU_C1 = 0.044715


def _gelu(xv):
    return xv * (0.5 * (1.0 + jnp.tanh(GELU_C0 * (xv + GELU_C1 * (xv * xv * xv)))))


def _gelu_grad(xv):
    t = jnp.tanh(GELU_C0 * (xv + GELU_C1 * (xv * xv * xv)))
    return 0.5 * (1.0 + t) + 0.5 * xv * (1.0 - t * t) * (GELU_C0 * (1.0 + 3.0 * GELU_C1 * xv * xv))


def _tril():
    ti = lax.broadcasted_iota(jnp.int32, (SGU_CHUNK, SGU_CHUNK), 0)
    si = lax.broadcasted_iota(jnp.int32, (SGU_CHUNK, SGU_CHUNK), 1)
    return si <= ti


def sgu_fwd(zs, ws, bst, ln, nseq, seq, name):
    nc = seq // SGU_CHUNK

    def body(z_ref, ws_ref, bs_ref, ln_ref, o_ref):
        tril = _tril()

        def blk(n, carry):
            st = pl.multiple_of(n * SGU_CHUNK, SGU_CHUNK)
            ge = _gelu(z_ref[pl.ds(st, SGU_CHUNK), :])
            uu, vv = ge[:, :SGU_WIDTH], ge[:, SGU_WIDTH:]
            mu = jnp.mean(vv, axis=-1, keepdims=True)
            xc = vv - mu
            rstd = lax.rsqrt(jnp.mean(xc * xc, axis=-1, keepdims=True) + EPS)
            vn = (xc * rstd * ln_ref[0:1, :] + ln_ref[1:2, :]).astype(BF16)
            for g in range(4):
                lanes = slice(g * LANES, (g + 1) * LANES)
                wm = jnp.where(tril, ws_ref[g], 0.0).astype(BF16)
                mixed = _dot(wm, vn[:, lanes]) + bs_ref[:, g:g + 1]
                o_ref[pl.ds(st, SGU_CHUNK), lanes] = (uu[:, lanes] * mixed).astype(o_ref.dtype)
            return carry

        lax.fori_loop(0, nc, blk, 0)

    return pl.pallas_call(
        body, name=name, grid=(nseq,),
        in_specs=[pl.BlockSpec((seq, 2 * SGU_WIDTH), lambda b: (b, 0)),
                  pl.BlockSpec((4, LANES, LANES), lambda b: (0, 0, 0)),
                  pl.BlockSpec((SGU_CHUNK, 4), lambda b: (0, 0)),
                  pl.BlockSpec((8, SGU_WIDTH), lambda b: (0, 0))],
        out_specs=pl.BlockSpec((seq, SGU_WIDTH), lambda b: (b, 0)),
        out_shape=jax.ShapeDtypeStruct((nseq * seq, SGU_WIDTH), BF16),
        compiler_params=_params("parallel"),
    )(zs, ws, bst, ln)


def sgu_bwd(zs, ws, bst, ln, dout, nseq, seq, name):
    nc = seq // SGU_CHUNK

    def body(z_ref, ws_ref, bs_ref, ln_ref, do_ref, dz_ref, dws_ref, dbs_ref, dln_ref):
        @pl.when(pl.program_id(0) == 0)
        def _():
            dws_ref[...] = jnp.zeros(dws_ref.shape, F32)
            dbs_ref[...] = jnp.zeros(dbs_ref.shape, F32)
            dln_ref[...] = jnp.zeros(dln_ref.shape, F32)

        tril = _tril()

        def blk(n, carry):
            st = pl.multiple_of(n * SGU_CHUNK, SGU_CHUNK)
            zv = z_ref[pl.ds(st, SGU_CHUNK), :]
            ge = _gelu(zv)
            uu, vv = ge[:, :SGU_WIDTH], ge[:, SGU_WIDTH:]
            mu = jnp.mean(vv, axis=-1, keepdims=True)
            xc = vv - mu
            rstd = lax.rsqrt(jnp.mean(xc * xc, axis=-1, keepdims=True) + EPS)
            xh = xc * rstd
            vn = (xh * ln_ref[0:1, :] + ln_ref[1:2, :]).astype(BF16)
            dob = do_ref[pl.ds(st, SGU_CHUNK), :]
            du_cols, dvn_cols = [], []
            for g in range(4):
                lanes = slice(g * LANES, (g + 1) * LANES)
                wm = jnp.where(tril, ws_ref[g], 0.0).astype(BF16)
                mixed = _dot(wm, vn[:, lanes]) + bs_ref[:, g:g + 1]
                du_cols.append(dob[:, lanes] * mixed)
                dmix = dob[:, lanes] * uu[:, lanes]
                dbs_ref[g] += jnp.broadcast_to(jnp.sum(dmix, axis=-1, keepdims=True), (SGU_CHUNK, LANES))
                dmb = dmix.astype(BF16)
                dws_ref[g] += jnp.where(tril, _dot_nt(dmb, vn[:, lanes]), 0.0)
                dvn_cols.append(_dot_tn(wm, dmb))
            dvn = jnp.concatenate(dvn_cols, axis=-1)
            dln_ref[0:1, :] += jnp.sum(dvn * xh, axis=0, keepdims=True)
            dln_ref[1:2, :] += jnp.sum(dvn, axis=0, keepdims=True)
            dxh = dvn * ln_ref[0:1, :]
            dv = rstd * (dxh - jnp.mean(dxh, axis=-1, keepdims=True)
                         - xh * jnp.mean(dxh * xh, axis=-1, keepdims=True))
            dge = jnp.concatenate(du_cols + [dv], axis=-1)
            dz_ref[pl.ds(st, SGU_CHUNK), :] = (dge * _gelu_grad(zv)).astype(dz_ref.dtype)
            return carry

        lax.fori_loop(0, nc, blk, 0)

    w_spec = pl.BlockSpec((4, LANES, LANES), lambda b: (0, 0, 0))
    ln_spec = pl.BlockSpec((8, SGU_WIDTH), lambda b: (0, 0))
    return pl.pallas_call(
        body, name=name, grid=(nseq,),
        in_specs=[pl.BlockSpec((seq, 2 * SGU_WIDTH), lambda b: (b, 0)), w_spec,
                  pl.BlockSpec((SGU_CHUNK, 4), lambda b: (0, 0)), ln_spec,
                  pl.BlockSpec((seq, SGU_WIDTH), lambda b: (b, 0))],
        out_specs=[pl.BlockSpec((seq, 2 * SGU_WIDTH), lambda b: (b, 0)), w_spec, w_spec, ln_spec],
        out_shape=[jax.ShapeDtypeStruct((nseq * seq, 2 * SGU_WIDTH), BF16),
                   jax.ShapeDtypeStruct((4, LANES, LANES), F32),
                   jax.ShapeDtypeStruct((4, LANES, LANES), F32),
                   jax.ShapeDtypeStruct((8, SGU_WIDTH), F32)],
        compiler_params=_params("arbitrary"),
    )(zs, ws, bst, ln, dout)


def _ew_rows(rows, cols, nbuf):
    t = _row_tile(rows, 1024)
    while t > 8 and t * cols * 4 * nbuf * 2 > 24 * 2**20:
        t //= 2
    return t


def adamw(w, g, m, v, name):
    layers, rows, cols = w.shape
    tr = _ew_rows(rows, cols, 7)

    def body(w_ref, g_ref, m_ref, v_ref, d_ref, mo_ref, vo_ref):
        gv = g_ref[...]
        mn = ADAM_B1 * m_ref[...] + (1.0 - ADAM_B1) * gv
        vn = ADAM_B2 * v_ref[...] + (1.0 - ADAM_B2) * (gv * gv)
        m_hat = mn / (1.0 - ADAM_B1 ** ADAM_STEP)
        v_hat = vn / (1.0 - ADAM_B2 ** ADAM_STEP)
        d_ref[...] = -ADAM_LR * (m_hat / (jnp.sqrt(v_hat) + ADAM_EPS) + ADAM_WD * w_ref[...])
        mo_ref[...] = mn
        vo_ref[...] = vn

    spec = pl.BlockSpec((1, tr, cols), lambda l, i: (l, i, 0))
    return pl.pallas_call(
        body, name=name, grid=(layers, rows // tr),
        in_specs=[spec] * 4, out_specs=[spec] * 3,
        out_shape=[jax.ShapeDtypeStruct(w.shape, F32)] * 3,
        compiler_params=_params("parallel", "parallel"),
    )(w, g, m, v)


def add_cast(a, b, name, dtype=BF16):
    nslab, rows, cols = a.shape
    tr = _ew_rows(rows, cols, 3)

    def body(a_ref, b_ref, o_ref):
        o_ref[...] = (a_ref[...] + b_ref[...]).astype(dtype)

    spec = pl.BlockSpec((1, tr, cols), lambda k, i: (k, i, 0))
    return pl.pallas_call(
        body, name=name, grid=(nslab, rows // tr),
        in_specs=[spec, spec], out_specs=spec,
        out_shape=jax.ShapeDtypeStruct(a.shape, dtype),
        compiler_params=_params("parallel", "parallel"),
    )(a, b)


def sum_parts(parts, name, first=None):
    npart, rows, cols = parts.shape
    tr = _ew_rows(rows, cols, npart + 2)

    def body(*refs):
        p_ref, o_ref = refs[-2], refs[-1]
        acc = p_ref[0].astype(F32) if first is None else refs[0][...].astype(F32) + p_ref[0].astype(F32)
        for j in range(1, npart):
            acc = acc + p_ref[j].astype(F32)
        o_ref[...] = acc

    row = pl.BlockSpec((tr, cols), lambda i: (i, 0))
    ins = [parts] if first is None else [first, parts]
    return pl.pallas_call(
        body, name=name, grid=(rows // tr,),
        in_specs=([] if first is None else [row]) + [pl.BlockSpec((npart, tr, cols), lambda i: (0, i, 0))],
        out_specs=row,
        out_shape=jax.ShapeDtypeStruct((rows, cols), F32),
        compiler_params=_params("parallel"),
    )(*ins)


ANY = pl.BlockSpec(memory_space=pl.ANY)
MESH = pl.DeviceIdType.MESH


def _me():
    return lax.axis_index("x"), lax.axis_index("y"), lax.axis_index("c")


def _flip(pos, rel):
    return tuple(1 - p if f else p for p, f in zip(pos, rel))


SIBLING = (0, 0, 1)
OTHER_CHIPS = ((1, 0, 0), (0, 1, 0), (1, 1, 0))


def _chip_of(pos, rel=(0, 0, 0)):
    px, py, _ = _flip(pos, rel)
    return 2 * px + py


def allgather_blocks(shards, name):
    nt = len(shards)
    hs = [s.shape[0] // 2 for s in shards]

    def body(*refs):
        ins, outs = refs[:nt], refs[nt:2 * nt]
        send_sems, recv_sems, loc_sems = refs[2 * nt:]
        pos = _me()
        x, y, c = pos

        def block_id(rel):
            px, py, pc = _flip(pos, rel)
            return 4 * px + 2 * py + pc

        def copy(t, k, block_rel, to_rel, src=None):
            dst = outs[t].at[block_id(block_rel)]
            return pltpu.make_async_remote_copy(
                src_ref=dst if src is None else src, dst_ref=dst,
                send_sem=send_sems.at[t * 7 + k], recv_sem=recv_sems.at[t * 7 + k],
                device_id=_flip(pos, to_rel), device_id_type=MESH)

        own = [ins[t].at[pl.ds(c * hs[t], hs[t])] for t in range(nt)]
        mine = [pltpu.make_async_copy(own[t], outs[t].at[block_id((0, 0, 0))], loc_sems.at[t]) for t in range(nt)]
        for cp in mine:
            cp.start()
        first = []
        for t in range(nt):
            first.append(copy(t, 0, (0, 0, 0), SIBLING, src=own[t]))
            first += [copy(t, 1 + j, (0, 0, 0), rel, src=own[t]) for j, rel in enumerate(OTHER_CHIPS)]
        for cp in first:
            cp.start()
        passed = []
        for j, rel in enumerate(OTHER_CHIPS):
            for t in range(nt):
                copy(t, 1 + j, rel, (0, 0, 0)).wait_recv()
                fwd = copy(t, 4 + j, rel, SIBLING)
                fwd.start()
                passed.append(fwd)
        for t in range(nt):
            copy(t, 0, SIBLING, (0, 0, 0)).wait_recv()
            for j, rel in enumerate(OTHER_CHIPS):
                copy(t, 4 + j, (rel[0], rel[1], 1), (0, 0, 0)).wait_recv()
        for cp in first + passed:
            cp.wait_send()
        for cp in mine:
            cp.wait()

    return pl.pallas_call(
        body, name=name,
        in_specs=[ANY] * nt, out_specs=[ANY] * nt,
        out_shape=[jax.ShapeDtypeStruct((N_DEV, h, s.shape[1]), s.dtype) for h, s in zip(hs, shards)],
        scratch_shapes=[pltpu.SemaphoreType.DMA((7 * nt,)), pltpu.SemaphoreType.DMA((7 * nt,)),
                        pltpu.SemaphoreType.DMA((nt,))],
    )(*shards)


def _block_id(pos, rel=(0, 0, 0)):
    px, py, pc = _flip(pos, rel)
    return 4 * px + 2 * py + pc


def gather_first_hop(shards):
    hs = [s.shape[0] // 2 for s in shards]

    def plan(ins, outs, pos):
        me = _block_id(pos)
        remote = []
        for i, o, h in zip(ins, outs, hs):
            own = i.at[pl.ds(pos[2] * h, h)]
            remote += [(rel, own, o.at[me]) for rel in (SIBLING,) + OTHER_CHIPS]
        return remote

    return Comm(shards, [((N_DEV, h, s.shape[1]), s.dtype) for h, s in zip(hs, shards)], plan, 4 * len(shards))


def gather_second_hop(gathered):
    def plan(ins, outs, pos):
        remote = []
        for i, o in zip(ins, outs):
            for rel in OTHER_CHIPS:
                blk = _block_id(pos, rel)
                remote.append((SIBLING, i.at[blk], o.at[blk]))
        return remote

    return Comm(gathered, [(g.shape, g.dtype) for g in gathered], plan, 3 * len(gathered),
                aliases={i: i for i in range(len(gathered))})


def swap_comm(xs):
    def plan(ins, outs, pos):
        return [(SIBLING, i, o) for i, o in zip(ins, outs)]

    return Comm(list(xs), [(v.shape, v.dtype) for v in xs], plan, len(xs))


def chip_scatter_comm(xs, shared=None):
    nx = len(xs)

    def plan(ins, outs, pos):
        me = _chip_of(pos)
        remote = []
        for i, o in zip(ins[:nx], outs[:nx]):
            remote += [(rel, i.at[_chip_of(pos, rel)], o.at[j]) for j, rel in enumerate(OTHER_CHIPS)]
        if shared is not None:
            remote += [(rel, ins[nx], outs[nx].at[me]) for rel in OTHER_CHIPS]
        return remote

    shapes = [((3,) + v.shape[1:], v.dtype) for v in xs]
    if shared is not None:
        shapes.append(((N_CHIPS,) + shared.shape, shared.dtype))
    return Comm(list(xs) + ([] if shared is None else [shared]), shapes, plan, 3 * nx + (0 if shared is None else 3))


def run_comm(comm, name):
    return _pcall(lambda: None, name, (1,), [], [], [], [], (), ("arbitrary",), comm)[1]


PACK_ROWS = 256


def _pack(arrs):
    parts, layout = [], []
    row = 0
    for a in arrs:
        flat = a.reshape(-1).astype(F32)
        size = flat.shape[0]
        rows = -(-size // (8 * LANES)) * 8
        flat = jnp.pad(flat, (0, rows * LANES - size))
        parts.append(flat.reshape(rows, LANES))
        layout.append((row, rows, size, a.shape))
        row += rows
    if row % PACK_ROWS:
        parts.append(jnp.zeros((PACK_ROWS - row % PACK_ROWS, LANES), F32))
    return jnp.concatenate(parts, axis=0), layout


def _unpack(packed, layout):
    return [packed[r0:r0 + rows].reshape(-1)[:size].reshape(shape) for r0, rows, size, shape in layout]


SMALL_REPL = ['mix_norm', 'a_b_in', 'a_sinks', 'a_conv_b', 'a_cln_g', 'a_cln_b', 'c_w_pool', 'c_w_s', 'c_b_s',
              'ffn_norm', 'final_norm']
SMALL_SHARD = ['a_conv_w', 'c_pool_scale', 'c_sln_g', 'c_sln_b']
BIG = ['a_w_in', 'a_w_out', 'c_w_in', 'c_w_out', 'ffn_w_gate', 'ffn_w_up', 'ffn_w_down']
BIG_ROW_SHARDED = {'a_w_out', 'c_w_out', 'down0', 'down1'}


def _full_weight(name, g8):
    _, h, cols = g8.shape
    g4 = g8.reshape(N_CHIPS, 2 * h, cols)
    if name in BIG_ROW_SHARDED:
        return g4.reshape(-1, cols)
    return jnp.transpose(g4, (1, 0, 2)).reshape(2 * h, N_CHIPS * cols)


def _to_shard_major(name, f):
    if name in BIG_ROW_SHARDED:
        return f.reshape(N_CHIPS, f.shape[0] // N_CHIPS, f.shape[1])
    r, cfull = f.shape
    return jnp.transpose(f.reshape(r, N_CHIPS, cfull // N_CHIPS), (1, 0, 2))


def kernel(*args):
    a = dict(zip(IN_NAMES, args))
    bl, seq, _ = a['x'].shape
    n = bl * seq
    x = a['x'].reshape(n, D_MODEL)
    target = a['loss_target'].reshape(n, D_MODEL)
    xi, yi, ci = _me()
    chip = 2 * xi + yi

    shard = {'a_w_in': a['a_w_in'][0], 'a_w_out': a['a_w_out'][0], 'c_w_in': a['c_w_in'][0], 'c_w_out': a['c_w_out'][0]}
    for layer in range(2):
        for short, key in (('gate', 'ffn_w_gate'), ('up', 'ffn_w_up'), ('down', 'ffn_w_down')):
            shard[short + str(layer)] = a[key][layer]
    shard = {k: v.astype(BF16) for k, v in shard.items()}
    block_id = 4 * xi + 2 * yi + ci

    def first_hop(*names):
        return gather_first_hop([shard[k] for k in names])

    def finish(name, g8):
        h = shard[name].shape[0] // 2
        own = lax.dynamic_slice_in_dim(shard[name], ci * h, h, axis=0)
        return _full_weight(name, lax.dynamic_update_slice_in_dim(g8, own[None], block_id, axis=0))

    a_w_in = _full_weight('a_w_in', allgather_blocks([shard['a_w_in']], "gather_a_w_in")[0])
    small_shard_pack, small_shard_layout = _pack([a[k] for k in SMALL_SHARD])
    hop_a = first_hop('a_w_out', 'gate0')
    hop_s = chip_scatter_comm([], shared=small_shard_pack)
    mix_norm, ffn_norm = a['mix_norm'], a['ffn_norm']
    (hn0, q, kv, cc), outs = norm_inproj(
        x, mix_norm[0:1], a_w_in, a['a_b_in'],
        [(0, ATTN_WIDTH), (ATTN_WIDTH, ATTN_WIDTH + 2 * KV_WIDTH), (ATTN_WIDTH + 2 * KV_WIDTH, a_w_in.shape[1])],
        [BF16, BF16, F32], "in_proj0", comm=hop_a + hop_s)
    got_a, (ss,) = hop_a.split(outs, hop_s)
    ss = lax.dynamic_update_slice_in_dim(ss, small_shard_pack[None], chip, axis=0)
    ss_full = []
    for r0, rows, size, shape in small_shard_layout:
        per_chip = ss[:, r0:r0 + rows].reshape(N_CHIPS, -1)[:, :size].reshape((N_CHIPS,) + shape)
        ss_full.append(jnp.concatenate([per_chip[k] for k in range(N_CHIPS)], axis=-1))
    a_conv_w, c_pool_scale, c_sln_g, c_sln_b = [v[0] for v in ss_full]

    conv_taps = jnp.pad(a_conv_w, ((0, 32 - CONV_KERNEL), (0, 0)))
    conv_vec = jnp.pad(jnp.stack([a['a_conv_b'][0], a['a_cln_g'][0], a['a_cln_b'][0]]), ((0, 5), (0, 0)))
    sinks_b = jnp.pad(jnp.repeat(a['a_sinks'][0].reshape(N_KV_HEADS, GROUP), ATTN_BLOCK, axis=1), ((0, 6), (0, 0)))
    w_pool_bf = a['c_w_pool'][0].astype(BF16)
    pool_scale = c_pool_scale.reshape(1, POOL_WIDTH)
    w_s = a['c_w_s'][0]
    b_s_t = a['c_b_s'][0].T
    sgu_ln = jnp.pad(jnp.stack([c_sln_g, c_sln_b]), ((0, 6), (0, 0)))
    final_norm = a['final_norm'].reshape(1, D_MODEL)

    hop_b, pass_a = first_hop('up0', 'down0'), gather_second_hop(got_a)
    attn, outs = attn_fwd(q, kv, sinks_b, bl, seq, "attn_fwd", comm=hop_b + pass_a)
    got_b, done = hop_b.split(outs, pass_a)
    a_w_out, wg0 = finish('a_w_out', done[0]), finish('gate0', done[1])

    hop_c, pass_b = first_hop('c_w_in', 'c_w_out', 'gate1'), gather_second_hop(got_b)
    conv, outs = conv_fwd(cc, conv_taps, conv_vec, bl, seq, "conv_fwd", comm=hop_c + pass_b)
    got_c, done = hop_c.split(outs, pass_b)
    wu0, wd0 = finish('up0', done[0]), finish('down0', done[1])

    h1, got_d = out_proj(x, attn, conv, a_w_out, "out_proj0", comm=first_hop('up1'))

    hop_e, pass_c = first_hop('down1'), gather_second_hop(got_c + got_d)
    (hnf0, g0, u0), outs = ffn_gate_up(h1, ffn_norm[0:1], wg0, wu0, "ffn_gate_up0", comm=hop_e + pass_c)
    got_e, done = hop_e.split(outs, pass_c)
    c_w_in, c_w_out = finish('c_w_in', done[0]), finish('c_w_out', done[1])
    wg1, wu1 = finish('gate1', done[2]), finish('up1', done[3])

    h2, done = ffn_down(h1, g0, u0, wd0, "ffn_down0", comm=gather_second_hop(got_e))
    wd1 = finish('down1', done[0])
    wg, wu, wd = [wg0, wg1], [wu0, wu1], [wd0, wd1]

    (hn1, zp, zs), _ = norm_inproj(
        h2, mix_norm[1:2], c_w_in, jnp.zeros((1, c_w_in.shape[1]), F32),
        [(0, POOL_WIDTH), (POOL_WIDTH, c_w_in.shape[1])], [F32, F32], "in_proj1")
    pool = pool_fwd(zp, w_pool_bf, pool_scale, bl, seq, "pool_fwd")
    sgu = sgu_fwd(zs, w_s, b_s_t, sgu_ln, bl, seq, "sgu_fwd")
    h3, _ = out_proj(h2, pool, sgu, c_w_out, "out_proj1")
    (hnf1, g1, u1), _ = ffn_gate_up(h3, ffn_norm[1:2], wg1, wu1, "ffn_gate_up1")
    h4, _ = ffn_down(h3, g1, u1, wd1, "ffn_down1")

    dh4, d_final_norm, loss_local = loss_head(h4, final_norm, target, "loss_head")

    grads = {}
    pieces = {}

    def halves_of(name, full_grad):
        t = _to_shard_major(name, full_grad)
        h = t.shape[1] // 2
        return (lax.dynamic_slice_in_dim(t, ci * h, h, axis=1), lax.dynamic_slice_in_dim(t, (1 - ci) * h, h, axis=1))

    def pair_sums_of(names, keeps, gots):
        return [add_cast(kp, gt, "pair_sum_" + k) for k, kp, gt in zip(names, keeps, gots)]

    def chip_sums_of(names, sums, from_chips):
        own = [lax.dynamic_index_in_dim(p, chip, axis=0, keepdims=False) for p in sums]
        return [sum_parts(p, "chip_sum_" + k, first=o) for k, p, o in zip(names, from_chips, own)]

    (dg, du, act), _ = ffn_down_bwd(dh4, g1, u1, wd[1], "ffn_down_bwd1")
    full1 = {'down1': mm_tn(act, dh4, "dw_down1"), 'gate1': mm_tn(hnf1, dg, "dw_gate1"),
             'up1': mm_tn(hnf1, du, "dw_up1")}
    dh3, d_ffn_norm1, _ = proj_rms_bwd([dg, du], [wg[1], wu[1]], h3, ffn_norm[1:2], dh4, 1, "ffn_up_bwd1",
                                       tm_pref=256)
    d_pool, d_sgu = out_proj_bwd(dh3, c_w_out, [F32, F32], "out_proj_bwd1")
    full1['c_w_out'] = jnp.concatenate([mm_tn(pool, dh3, "dw_out1_pool"), mm_tn(sgu, dh3, "dw_out1_sgu")], axis=0)
    dzp, d_w_pool, d_pool_scale = pool_bwd(zp, w_pool_bf, pool_scale, d_pool, bl, seq, "pool_bwd")
    dzs, d_w_s, d_b_s_b, d_sgu_ln = sgu_bwd(zs, w_s, b_s_t, sgu_ln, d_sgu, bl, seq, "sgu_bwd")
    full1['c_w_in'] = jnp.concatenate([mm_tn(hn1, dzp, "dw_in1_pool"), mm_tn(hn1, dzs, "dw_in1_sgu")], axis=1)
    names1 = ['gate1', 'up1', 'down1', 'c_w_out', 'c_w_in']
    keep1, give1 = zip(*[halves_of(k, full1[k]) for k in names1])
    dh2, d_mix_norm1, got1 = proj_rms_bwd([dzp, dzs], [c_w_in[:, :POOL_WIDTH], c_w_in[:, POOL_WIDTH:]], h2,
                                          mix_norm[1:2], dh3, 1, "in_proj_bwd1", comm=swap_comm(give1))
    sums1 = pair_sums_of(names1, keep1, got1)

    (dg, du, act), from_chips1 = ffn_down_bwd(dh2, g0, u0, wd[0], "ffn_down_bwd0", comm=chip_scatter_comm(sums1))
    mine1 = chip_sums_of(names1, sums1, from_chips1)
    full0 = {'down0': mm_tn(act, dh2, "dw_down0"), 'gate0': mm_tn(hnf0, dg, "dw_gate0"),
             'up0': mm_tn(hnf0, du, "dw_up0")}
    names0 = ['gate0', 'up0', 'down0']
    keep0, give0 = zip(*[halves_of(k, full0[k]) for k in names0])
    join1, pair0 = swap_comm(mine1), swap_comm(give0)
    dh1, d_ffn_norm0, outs = proj_rms_bwd([dg, du], [wg[0], wu[0]], h1, ffn_norm[0:1], dh2, 1, "ffn_up_bwd0",
                                          tm_pref=256, comm=join1 + pair0)
    theirs1, got0 = join1.split(outs, pair0)
    pieces.update({k: (m, t) for k, m, t in zip(names1, mine1, theirs1)})
    sums0 = pair_sums_of(names0, keep0, got0)

    d_attn, d_conv = out_proj_bwd(dh1, a_w_out, [BF16, F32], "out_proj_bwd0")
    d_a_w_out = jnp.concatenate([mm_tn(attn, dh1, "dw_out0_attn"), mm_tn(conv, dh1, "dw_out0_conv")], axis=0)
    keep_o, give_o = halves_of('a_w_out', d_a_w_out)
    chips0, pair_o = chip_scatter_comm(sums0), swap_comm([give_o])
    (dq, dkv, d_sinks_b), outs = attn_bwd(q, kv, sinks_b, d_attn, bl, seq, "attn_bwd", comm=chips0 + pair_o)
    from_chips0, got_o = chips0.split(outs, pair_o)
    mine0 = chip_sums_of(names0, sums0, from_chips0)
    sums_o = pair_sums_of(['a_w_out'], [keep_o], got_o)
    join0, chips_o = swap_comm(mine0), chip_scatter_comm(sums_o)
    (dcc, d_conv_taps, d_conv_vec), outs = conv_bwd(cc, conv_taps, conv_vec, d_conv, bl, seq, "conv_bwd",
                                                    comm=join0 + chips_o)
    theirs0, from_chips_o = join0.split(outs, chips_o)
    pieces.update({k: (m, t) for k, m, t in zip(names0, mine0, theirs0)})
    mine_o = chip_sums_of(['a_w_out'], sums_o, from_chips_o)
    dw_q, db_q = mm_tn(hn0, dq, "dw_in0_q", colsum=True)
    dw_kv, db_kv = mm_tn(hn0, dkv, "dw_in0_kv", colsum=True)
    dw_c, db_c = mm_tn(hn0, dcc, "dw_in0_c", colsum=True)
    d_a_w_in = jnp.concatenate([dw_q, dw_kv, dw_c], axis=1)
    d_a_b_in = jnp.concatenate([db_q, db_kv, db_c], axis=0)
    keep_i, give_i = halves_of('a_w_in', d_a_w_in)
    kq, kk = ATTN_WIDTH, ATTN_WIDTH + 2 * KV_WIDTH
    join_o, pair_i = swap_comm(mine_o), swap_comm([give_i])
    grad_x, d_mix_norm0, outs = proj_rms_bwd([dq, dkv, dcc], [a_w_in[:, :kq], a_w_in[:, kq:kk], a_w_in[:, kk:]], x,
                                             mix_norm[0:1], dh1, 1, "in_proj_bwd0", comm=join_o + pair_i)
    theirs_o, got_i = join_o.split(outs, pair_i)
    pieces['a_w_out'] = (mine_o[0], theirs_o[0])
    sums_i = pair_sums_of(['a_w_in'], [keep_i], got_i)

    small_full = {
        'mix_norm': jnp.stack([d_mix_norm0, d_mix_norm1]), 'a_b_in': d_a_b_in[None], 'a_sinks': d_sinks_b[:, 0][None],
        'a_conv_w': d_conv_taps[:CONV_KERNEL][None], 'a_conv_b': d_conv_vec[0][None], 'a_cln_g': d_conv_vec[1][None],
        'a_cln_b': d_conv_vec[2][None], 'c_w_pool': d_w_pool[None], 'c_pool_scale': d_pool_scale[0][None],
        'c_sln_g': d_sgu_ln[0][None], 'c_sln_b': d_sgu_ln[1][None], 'c_w_s': d_w_s[None],
        'c_b_s': d_b_s_b[:, :, 0][None], 'ffn_norm': jnp.stack([d_ffn_norm0, d_ffn_norm1]),
        'final_norm': d_final_norm, 'loss': loss_local.reshape(1)}
    small_names = SMALL_REPL + SMALL_SHARD
    small_pack, small_layout = _pack([small_full[k] for k in small_names + ['loss']])

    got_s = run_comm(swap_comm([small_pack]), "tail_pair")
    small_pair = add_cast(small_pack[None], got_s[0][None], "pair_sum_small", dtype=F32)[0]
    outs = run_comm(chip_scatter_comm(sums_i, shared=small_pair), "tail_chips")
    mine_i = chip_sums_of(['a_w_in'], sums_i, outs[:1])
    small_chips = lax.dynamic_update_slice_in_dim(outs[1], small_pair[None], chip, axis=0)
    theirs_i = run_comm(swap_comm(mine_i), "tail_join")
    pieces['a_w_in'] = (mine_i[0], theirs_i[0])

    def whole(name):
        mine, theirs = pieces[name]
        return jnp.concatenate([jnp.where(ci == 0, mine, theirs), jnp.where(ci == 0, theirs, mine)], axis=0)

    for k in ('a_w_in', 'a_w_out', 'c_w_in', 'c_w_out'):
        grads[k] = whole(k)[None]
    for short, key in (('gate', 'ffn_w_gate'), ('up', 'ffn_w_up'), ('down', 'ffn_w_down')):
        grads[key] = jnp.stack([whole(short + '0'), whole(short + '1')])

    small_sum = sum_parts(small_chips, "small_sum")
    for k, g in zip(small_names + ['loss'], _unpack(small_sum, small_layout)):
        if k in SMALL_SHARD:
            width = a[k].shape[-1]
            g = lax.dynamic_slice_in_dim(g, chip * width, width, axis=g.ndim - 1)
        grads[k] = g
    loss = grads.pop('loss')[0]

    delta, new_m, new_v = {}, {}, {}
    for k in BIG:
        delta[k], new_m[k], new_v[k] = adamw(a[k], grads[k], a['m_' + k], a['v_' + k], "adamw_" + k)
    packs = [_pack([src[k] for k in small_names])
             for src in (a, grads, {k: a['m_' + k] for k in small_names}, {k: a['v_' + k] for k in small_names})]
    d, m, v = adamw(packs[0][0][None], packs[1][0][None], packs[2][0][None], packs[3][0][None], "adamw_small")
    d, m, v = d[0], m[0], v[0]
    lay = packs[0][1]
    for k, dv, mv, vv in zip(small_names, _unpack(d, lay), _unpack(m, lay), _unpack(v, lay)):
        delta[k], new_m[k], new_v[k] = dv, mv, vv

    return (loss, grad_x.reshape(a['x'].shape), *[grads[k] for k in WEIGHTS], *[delta[k] for k in WEIGHTS],
            *[new_m[k] for k in WEIGHTS], *[new_v[k] for k in WEIGHTS])
```

```python
import functools

import jax
import jax.numpy as jnp
from jax import lax
from jax.experimental import pallas as pl
from jax.experimental.pallas import tpu as pltpu

F32 = jnp.float32
BF16 = jnp.bfloat16

D_MODEL = 1024
EPS = 1e-5
N_Q_HEADS, N_KV_HEADS, HEAD_DIM = 8, 2, 64
ATTN_BLOCK = 128
ATTN_WIDTH = N_Q_HEADS * HEAD_DIM
KV_WIDTH = N_KV_HEADS * HEAD_DIM
CONV_WIDTH = 512
CONV_KERNEL = 31
CONV_HALO = 32
POOL_WINDOWS = (2, 4, 8, 16)
POOL_WIDTH = 512
POOL_HALO = 16
SGU_WIDTH = 512
SGU_CHUNK = 128
D_FF = 2816
FF_CHUNK = 128
LANES = 128
N_CHIPS = 4
N_DEV = 8

ADAM_LR, ADAM_B1, ADAM_B2, ADAM_EPS, ADAM_WD, ADAM_STEP = 0.001, 0.9, 0.999, 1e-08, 0.01, 10

VMEM_LIMIT = 56 * 2**20

WEIGHTS = ['mix_norm', 'a_w_in', 'a_b_in', 'a_sinks', 'a_conv_w', 'a_conv_b', 'a_cln_g', 'a_cln_b', 'a_w_out',
           'c_w_in', 'c_w_pool', 'c_pool_scale', 'c_sln_g', 'c_sln_b', 'c_w_s', 'c_b_s', 'c_w_out',
           'ffn_norm', 'ffn_w_gate', 'ffn_w_up', 'ffn_w_down', 'final_norm']
IN_NAMES = (['x'] + WEIGHTS + ['loss_target'] + ['m_' + n for n in WEIGHTS] + ['v_' + n for n in WEIGHTS])


def _params(*sem):
    return pltpu.CompilerParams(dimension_semantics=sem, vmem_limit_bytes=VMEM_LIMIT)


def _dot(a, b):
    return jnp.dot(a, b, preferred_element_type=F32)


def _dot_nt(a, b):
    return lax.dot_general(a, b, (((1,), (1,)), ((), ())), preferred_element_type=F32)


def _dot_tn(a, b):
    return lax.dot_general(a, b, (((0,), (0,)), ((), ())), preferred_element_type=F32)


def _sigmoid(v):
    return 1.0 / (1.0 + jnp.exp(-v))


def _row_tile(n, pref):
    t = min(n, pref)
    while n % t:
        t //= 2
    return t


def _col_tile(m, rows, budget=6 * 2**20):
    best = LANES
    for t in range(LANES, m + 1, LANES):
        if m % t == 0 and rows * t * 4 <= budget:
            best = t
    return best


class Comm:
    def __init__(self, ins, out_shapes, plan, count, aliases=None):
        self.ins, self.out_shapes, self.plan, self.count, self.aliases = ins, out_shapes, plan, count, aliases or {}

    def __add__(self, other):
        ni, no = len(self.ins), len(self.out_shapes)

        def plan(ins, outs, pos):
            return self.plan(ins[:ni], outs[:no], pos) + other.plan(ins[ni:], outs[no:], pos)

        aliases = dict(self.aliases)
        aliases.update({ni + i: no + o for i, o in other.aliases.items()})
        return Comm(list(self.ins) + list(other.ins), list(self.out_shapes) + list(other.out_shapes), plan,
                    self.count + other.count, aliases)

    def split(self, outs, other):
        return outs[:len(self.out_shapes)], outs[len(self.out_shapes):]


def _pcall(body, name, grid, in_specs, out_specs, out_shape, scratch_shapes, args, sem, comm=None):
    single = not isinstance(out_shape, (list, tuple))
    if single:
        out_specs, out_shape = [out_specs], [out_shape]
    if comm is None:
        res = pl.pallas_call(body, name=name, grid=grid, in_specs=in_specs, out_specs=list(out_specs),
                             out_shape=list(out_shape), scratch_shapes=list(scratch_shapes),
                             compiler_params=_params(*sem))(*args)
        return (res[0] if single else res), []
    na, nci, no, nco, ns = len(args), len(comm.ins), len(out_shape), len(comm.out_shapes), len(scratch_shapes)

    def wrapped(*refs):
        a_refs, ci_refs = refs[:na], refs[na:na + nci]
        o_refs, co_refs = refs[na + nci:na + nci + no], refs[na + nci + no:na + nci + no + nco]
        s_refs = refs[na + nci + no + nco:na + nci + no + nco + ns]
        send_sems, recv_sems = refs[-2], refs[-1]
        pos = _me()

        def copies():
            return [pltpu.make_async_remote_copy(src_ref=s, dst_ref=d, send_sem=send_sems.at[i],
                                                 recv_sem=recv_sems.at[i], device_id=_flip(pos, rel),
                                                 device_id_type=MESH)
                    for i, (rel, s, d) in enumerate(comm.plan(ci_refs, co_refs, pos))]

        first, last = None, None
        for d, size in enumerate(grid):
            f, l = pl.program_id(d) == 0, pl.program_id(d) == size - 1
            first = f if first is None else first & f
            last = l if last is None else last & l

        @pl.when(first)
        def _():
            for cp in copies():
                cp.start()

        body(*a_refs, *o_refs, *s_refs)

        @pl.when(last)
        def _():
            for cp in copies():
                cp.wait()

    res = pl.pallas_call(
        wrapped, name=name, grid=grid,
        in_specs=list(in_specs) + [ANY] * nci, out_specs=list(out_specs) + [ANY] * nco,
        out_shape=list(out_shape) + [jax.ShapeDtypeStruct(s, d) for s, d in comm.out_shapes],
        scratch_shapes=list(scratch_shapes) + [pltpu.SemaphoreType.DMA((comm.count,)),
                                               pltpu.SemaphoreType.DMA((comm.count,))],
        input_output_aliases={na + i: no + o for i, o in comm.aliases.items()},
        compiler_params=_params(*(["arbitrary"] * len(grid))),
    )(*args, *comm.ins)
    outs = res[:no]
    return (outs[0] if single else outs), list(res[no:])


def norm_inproj(x, gain, w, bias, splits, dtypes, name, comm=None, w_transposed=False):
    n = x.shape[0]
    m = w.shape[0] if w_transposed else w.shape[1]
    tm = _row_tile(n, 512)

    def body(x_ref, g_ref, w_ref, b_ref, hn_ref, *outs):
        xv = x_ref[...]
        r = lax.rsqrt(jnp.mean(xv * xv, axis=-1, keepdims=True) + EPS)
        hn = ((xv * r) * g_ref[...]).astype(BF16)
        hn_ref[...] = hn
        z = (_dot_nt if w_transposed else _dot)(hn, w_ref[...]) + b_ref[...]
        for o, (lo, hi) in zip(outs, splits):
            o[...] = z[:, lo:hi].astype(o.dtype)

    out_shape = [jax.ShapeDtypeStruct((n, D_MODEL), BF16)]
    out_specs = [pl.BlockSpec((tm, D_MODEL), lambda i: (i, 0))]
    for (lo, hi), dt in zip(splits, dtypes):
        out_shape.append(jax.ShapeDtypeStruct((n, hi - lo), dt))
        out_specs.append(pl.BlockSpec((tm, hi - lo), lambda i: (i, 0)))
    return _pcall(
        body, name, (n // tm,),
        [pl.BlockSpec((tm, D_MODEL), lambda i: (i, 0)),
         pl.BlockSpec((1, D_MODEL), lambda i: (0, 0)),
         pl.BlockSpec(w.shape, lambda i: (0, 0)),
         pl.BlockSpec((1, m), lambda i: (0, 0))],
        out_specs, out_shape, [], (x, gain, w, bias), ("parallel",), comm)


def out_proj(res, m1, m2, w, name, comm=None):
    n = res.shape[0]
    k1, k2 = m1.shape[1], m2.shape[1]
    assert k1 == k2
    tm = _row_tile(n, 512)

    def body(r_ref, a_ref, b_ref, w1_ref, w2_ref, o_ref):
        o_ref[...] = r_ref[...] + _dot(a_ref[...], w1_ref[...]) + _dot(b_ref[...], w2_ref[...])

    return _pcall(
        body, name, (n // tm,),
        [pl.BlockSpec((tm, D_MODEL), lambda i: (i, 0)),
         pl.BlockSpec((tm, k1), lambda i: (i, 0)),
         pl.BlockSpec((tm, k2), lambda i: (i, 0)),
         pl.BlockSpec((k1, D_MODEL), lambda i: (0, 0)),
         pl.BlockSpec((k2, D_MODEL), lambda i: (1, 0))],
        pl.BlockSpec((tm, D_MODEL), lambda i: (i, 0)),
        jax.ShapeDtypeStruct((n, D_MODEL), F32), [], (res, m1, m2, w, w), ("parallel",), comm)


def ffn_gate_up(h, gain, wg_t, wu_t, name, comm=None):
    n = h.shape[0]
    tm = _row_tile(n, 1024)
    th = D_FF // 2

    def body(h_ref, g_ref, wg_ref, wu_ref, hn_ref, go_ref, uo_ref):
        @pl.when(pl.program_id(1) == 0)
        def _():
            xv = h_ref[...]
            r = lax.rsqrt(jnp.mean(xv * xv, axis=-1, keepdims=True) + EPS)
            hn_ref[...] = ((xv * r) * g_ref[...]).astype(BF16)

        hn = hn_ref[...]
        go_ref[...] = _dot_nt(hn, wg_ref[...]).astype(BF16)
        uo_ref[...] = _dot_nt(hn, wu_ref[...]).astype(BF16)

    return _pcall(
        body, name, (n // tm, D_FF // th),
        [pl.BlockSpec((tm, D_MODEL), lambda i, j: (i, 0)),
         pl.BlockSpec((1, D_MODEL), lambda i, j: (0, 0)),
         pl.BlockSpec((th, D_MODEL), lambda i, j: (j, 0)),
         pl.BlockSpec((th, D_MODEL), lambda i, j: (j, 0))],
        [pl.BlockSpec((tm, D_MODEL), lambda i, j: (i, 0)),
         pl.BlockSpec((tm, th), lambda i, j: (i, j)),
         pl.BlockSpec((tm, th), lambda i, j: (i, j))],
        [jax.ShapeDtypeStruct((n, D_MODEL), BF16),
         jax.ShapeDtypeStruct((n, D_FF), BF16),
         jax.ShapeDtypeStruct((n, D_FF), BF16)],
        [], (h, gain, wg_t, wu_t), ("parallel", "arbitrary"), comm)


def ffn_down(h, g, u, wd, name, comm=None):
    n = h.shape[0]
    tm = _row_tile(n, 512)

    def body(h_ref, g_ref, u_ref, w_ref, o_ref, a_ref):
        for c0 in range(0, D_FF, FF_CHUNK):
            gv = g_ref[:, c0:c0 + FF_CHUNK].astype(F32)
            a_ref[:, c0:c0 + FF_CHUNK] = (gv * _sigmoid(gv) * u_ref[:, c0:c0 + FF_CHUNK].astype(F32)).astype(BF16)
        o_ref[...] = h_ref[...] + _dot(a_ref[...], w_ref[...])

    return _pcall(
        body, name, (n // tm,),
        [pl.BlockSpec((tm, D_MODEL), lambda i: (i, 0)),
         pl.BlockSpec((tm, D_FF), lambda i: (i, 0)),
         pl.BlockSpec((tm, D_FF), lambda i: (i, 0)),
         pl.BlockSpec((D_FF, D_MODEL), lambda i: (0, 0))],
        pl.BlockSpec((tm, D_MODEL), lambda i: (i, 0)),
        jax.ShapeDtypeStruct((n, D_MODEL), F32),
        [pltpu.VMEM((tm, D_FF), BF16)], (h, g, u, wd), ("parallel",), comm)


def ffn_down_bwd(dh, g, u, wd, name, comm=None):
    n = dh.shape[0]
    tm = _row_tile(n, 512)
    th = D_FF // 2

    def body(dh_ref, g_ref, u_ref, w_ref, dg_ref, du_ref, a_ref, da_ref):
        da_ref[...] = _dot_nt(dh_ref[...].astype(BF16), w_ref[...])
        for c0 in range(0, th, FF_CHUNK):
            cols = slice(c0, c0 + FF_CHUNK)
            da = da_ref[:, cols]
            gv = g_ref[:, cols].astype(F32)
            uv = u_ref[:, cols].astype(F32)
            sg = _sigmoid(gv)
            act = gv * sg
            dg_ref[:, cols] = (da * uv * (sg * (1.0 + gv * (1.0 - sg)))).astype(BF16)
            du_ref[:, cols] = (da * act).astype(BF16)
            a_ref[:, cols] = (act * uv).astype(BF16)

    spec_h = pl.BlockSpec((tm, th), lambda i, j: (i, j))
    return _pcall(
        body, name, (n // tm, D_FF // th),
        [pl.BlockSpec((tm, D_MODEL), lambda i, j: (i, 0)), spec_h, spec_h,
         pl.BlockSpec((th, D_MODEL), lambda i, j: (j, 0))],
        [spec_h, spec_h, spec_h], [jax.ShapeDtypeStruct((n, D_FF), BF16)] * 3,
        [pltpu.VMEM((tm, th), F32)], (dh, g, u, wd), ("parallel", "arbitrary"), comm)


def mm_tn(x, dy, name, xsum=False, comm=None):
    n, k = x.shape
    m = dy.shape[1]
    tn = _col_tile(m, k)
    tt = _row_tile(n, 1024)

    def body(x_ref, dy_ref, o_ref, *rest):
        j, t = pl.program_id(0), pl.program_id(1)
        xv = x_ref[...]
        part = _dot_tn(xv.astype(BF16), dy_ref[...].astype(BF16))

        @pl.when(t == 0)
        def _():
            o_ref[...] = part

        @pl.when(t > 0)
        def _():
            o_ref[...] += part

        if xsum:
            @pl.when(j == 0)
            def _():
                cs = jnp.broadcast_to(jnp.sum(xv.astype(F32), axis=0, keepdims=True), rest[0].shape)

                @pl.when(t == 0)
                def _():
                    rest[0][...] = cs

                @pl.when(t > 0)
                def _():
                    rest[0][...] += cs

    out_shape = [jax.ShapeDtypeStruct((k, m), F32)]
    out_specs = [pl.BlockSpec((k, tn), lambda j, t: (0, j))]
    if xsum:
        out_shape.append(jax.ShapeDtypeStruct((8, k), F32))
        out_specs.append(pl.BlockSpec((8, k), lambda j, t: (0, 0)))
    res, comm_outs = _pcall(
        body, name, (m // tn, n // tt),
        [pl.BlockSpec((tt, k), lambda j, t: (t, 0)),
         pl.BlockSpec((tt, tn), lambda j, t: (t, j))],
        out_specs, out_shape, [], (x, dy), ("arbitrary", "arbitrary"), comm)
    res = (res[0], res[1][0]) if xsum else res[0]
    return res if comm is None else (res, comm_outs)


def out_proj_bwd(dh, w, dtypes, name):
    n = dh.shape[0]
    k = w.shape[0]
    half = k // 2
    tm = _row_tile(n, 512)

    def body(dh_ref, w_ref, a_ref, b_ref):
        dm = _dot_nt(dh_ref[...].astype(BF16), w_ref[...])
        a_ref[...] = dm[:, :half].astype(a_ref.dtype)
        b_ref[...] = dm[:, half:].astype(b_ref.dtype)

    return pl.pallas_call(
        body, name=name, grid=(n // tm,),
        in_specs=[pl.BlockSpec((tm, D_MODEL), lambda i: (i, 0)),
                  pl.BlockSpec((k, D_MODEL), lambda i: (0, 0))],
        out_specs=[pl.BlockSpec((tm, half), lambda i: (i, 0))] * 2,
        out_shape=[jax.ShapeDtypeStruct((n, half), dtypes[0]), jax.ShapeDtypeStruct((n, half), dtypes[1])],
        compiler_params=_params("parallel"),
    )(dh, w)


def proj_rms_bwd(dys, ws, h_in, gain, dres, nk, name, tm_pref=512, comm=None, w_transposed=False):
    n = h_in.shape[0]
    npair = len(dys)
    tm = _row_tile(n, tm_pref)
    tks = [dy.shape[1] // nk for dy in dys]
    mm = _dot if w_transposed else _dot_nt

    def body(*refs):
        dy_refs = refs[:npair]
        w_refs = refs[npair:2 * npair]
        h_ref, g_ref, dr_ref, o_ref, dg_ref, acc_ref = refs[2 * npair:]
        i, k = pl.program_id(0), pl.program_id(1)
        part = mm(dy_refs[0][...], w_refs[0][...])
        for p in range(1, npair):
            part = part + mm(dy_refs[p][...], w_refs[p][...])

        @pl.when(k == 0)
        def _():
            acc_ref[...] = part

        @pl.when(k > 0)
        def _():
            acc_ref[...] += part

        @pl.when(k == nk - 1)
        def _():
            dhn = acc_ref[...]
            xv = h_ref[...]
            r = lax.rsqrt(jnp.mean(xv * xv, axis=-1, keepdims=True) + EPS)
            xh = xv * r
            uv = dhn * g_ref[...]
            o_ref[...] = dr_ref[...] + r * (uv - xh * jnp.mean(uv * xh, axis=-1, keepdims=True))
            dgp = jnp.broadcast_to(jnp.sum(dhn * xh, axis=0, keepdims=True), dg_ref.shape)

            @pl.when(i == 0)
            def _():
                dg_ref[...] = dgp

            @pl.when(i > 0)
            def _():
                dg_ref[...] += dgp

    row = pl.BlockSpec((tm, D_MODEL), lambda i, k: (i, 0))
    in_specs = [pl.BlockSpec((tm, tk), lambda i, k: (i, k)) for tk in tks]
    if w_transposed:
        in_specs += [pl.BlockSpec((tk, D_MODEL), lambda i, k: (k, 0)) for tk in tks]
    else:
        in_specs += [pl.BlockSpec((D_MODEL, tk), lambda i, k: (0, k)) for tk in tks]
    in_specs += [row, pl.BlockSpec((1, D_MODEL), lambda i, k: (0, 0)), row]
    (dh, dgain), comm_outs = _pcall(
        body, name, (n // tm, nk), in_specs,
        [row, pl.BlockSpec((8, D_MODEL), lambda i, k: (0, 0))],
        [jax.ShapeDtypeStruct((n, D_MODEL), F32), jax.ShapeDtypeStruct((8, D_MODEL), F32)],
        [pltpu.VMEM((tm, D_MODEL), F32)], (*dys, *ws, h_in, gain, dres), ("arbitrary", "arbitrary"), comm)
    return dh, dgain[0], comm_outs


def loss_head(h, gain, target, name):
    n = h.shape[0]
    tm = _row_tile(n, 512)

    def body(h_ref, g_ref, t_ref, dh_ref, dg_ref, l_ref):
        i = pl.program_id(0)
        xv = h_ref[...]
        r = lax.rsqrt(jnp.mean(xv * xv, axis=-1, keepdims=True) + EPS)
        xh = xv * r
        err = xh * g_ref[...] - t_ref[...]
        dy = err * (1.0 / D_MODEL)
        uv = dy * g_ref[...]
        dh_ref[...] = r * (uv - xh * jnp.mean(uv * xh, axis=-1, keepdims=True))
        dgp = jnp.broadcast_to(jnp.sum(dy * xh, axis=0, keepdims=True), dg_ref.shape)
        lp = jnp.sum(jnp.sum(err * err, axis=-1, keepdims=True), axis=0, keepdims=True) * (0.5 / D_MODEL)
        lp = jnp.broadcast_to(lp, l_ref.shape)

        @pl.when(i == 0)
        def _():
            dg_ref[...] = dgp
            l_ref[...] = lp

        @pl.when(i > 0)
        def _():
            dg_ref[...] += dgp
            l_ref[...] += lp

    row = pl.BlockSpec((tm, D_MODEL), lambda i: (i, 0))
    dh, dg, l = pl.pallas_call(
        body, name=name, grid=(n // tm,),
        in_specs=[row, pl.BlockSpec((1, D_MODEL), lambda i: (0, 0)), row],
        out_specs=[row, pl.BlockSpec((8, D_MODEL), lambda i: (0, 0)), pl.BlockSpec((8, LANES), lambda i: (0, 0))],
        out_shape=[jax.ShapeDtypeStruct((n, D_MODEL), F32), jax.ShapeDtypeStruct((8, D_MODEL), F32),
                   jax.ShapeDtypeStruct((8, LANES), F32)],
        compiler_params=_params("arbitrary"),
    )(h, gain, target)
    return dh, dg[0], l[0, 0]


GROUP = N_Q_HEADS // N_KV_HEADS
GQ = GROUP * ATTN_BLOCK


def _attn_mask_t(n):
    r = lax.broadcasted_iota(jnp.int32, (2 * ATTN_BLOCK, GQ), 0)
    qi = lax.broadcasted_iota(jnp.int32, (2 * ATTN_BLOCK, GQ), 1) & (ATTN_BLOCK - 1)
    band = (r > qi) & (r <= qi + ATTN_BLOCK)
    return band & ((r >= ATTN_BLOCK) | (n > 0))


def _stack_heads(blk, kh):
    return jnp.concatenate([blk[:, (kh * GROUP + g) * HEAD_DIM:(kh * GROUP + g + 1) * HEAD_DIM]
                            for g in range(GROUP)], axis=0)


def _attn_probs_t(kk, qs, mask, sink):
    s = _dot_nt(kk, qs) * (HEAD_DIM ** -0.5)
    s = jnp.where(mask, s, -1e30)
    m = jnp.maximum(jnp.max(s, axis=0, keepdims=True), sink)
    p = jnp.exp(s - m)
    esink = jnp.exp(sink - m)
    inv = 1.0 / (jnp.sum(p, axis=0, keepdims=True) + esink)
    return p * inv, esink * inv


def attn_fwd(q, kv, sinks_t, nseq, seq, name, comm=None):
    nb = seq // ATTN_BLOCK

    def body(q_ref, kv_ref, s_ref, o_ref, kvp):
        kvp[0:ATTN_BLOCK, :] = jnp.zeros((ATTN_BLOCK, 2 * KV_WIDTH), BF16)
        kvp[ATTN_BLOCK:, :] = kv_ref[...]

        def blk(n, carry):
            st = pl.multiple_of(n * ATTN_BLOCK, ATTN_BLOCK)
            qb = q_ref[pl.ds(st, ATTN_BLOCK), :]
            kw = kvp[pl.ds(st, 2 * ATTN_BLOCK), :]
            mask = _attn_mask_t(n)
            for kh in range(N_KV_HEADS):
                kk = kw[:, kh * HEAD_DIM:(kh + 1) * HEAD_DIM]
                vv = kw[:, KV_WIDTH + kh * HEAD_DIM:KV_WIDTH + (kh + 1) * HEAD_DIM]
                probs, _ = _attn_probs_t(kk, _stack_heads(qb, kh), mask, s_ref[kh:kh + 1, :])
                ot = _dot_tn(vv, probs.astype(BF16))
                for pair in range(GROUP // 2):
                    two = jnp.concatenate([ot[:, (2 * pair) * ATTN_BLOCK:(2 * pair + 1) * ATTN_BLOCK],
                                           ot[:, (2 * pair + 1) * ATTN_BLOCK:(2 * pair + 2) * ATTN_BLOCK]], axis=0)
                    col = (kh * GROUP + 2 * pair) * HEAD_DIM
                    o_ref[pl.ds(st, ATTN_BLOCK), col:col + 2 * HEAD_DIM] = two.T.astype(o_ref.dtype)
            return carry

        lax.fori_loop(0, nb, blk, 0)

    return _pcall(
        body, name, (nseq,),
        [pl.BlockSpec((seq, ATTN_WIDTH), lambda b: (b, 0)),
         pl.BlockSpec((seq, 2 * KV_WIDTH), lambda b: (b, 0)),
         pl.BlockSpec((8, GQ), lambda b: (0, 0))],
        pl.BlockSpec((seq, ATTN_WIDTH), lambda b: (b, 0)),
        jax.ShapeDtypeStruct((nseq * seq, ATTN_WIDTH), BF16),
        [pltpu.VMEM((ATTN_BLOCK + seq, 2 * KV_WIDTH), BF16)], (q, kv, sinks_t), ("parallel",), comm)


def attn_bwd(q, kv, sinks_t, do, nseq, seq, name, comm=None):
    nb = seq // ATTN_BLOCK

    def body(q_ref, kv_ref, s_ref, do_ref, dq_ref, dkv_ref, ds_ref, kvp, dkvp, dsacc):
        @pl.when(pl.program_id(0) == 0)
        def _():
            dsacc[...] = jnp.zeros(dsacc.shape, F32)

        kvp[0:ATTN_BLOCK, :] = jnp.zeros((ATTN_BLOCK, 2 * KV_WIDTH), BF16)
        kvp[ATTN_BLOCK:, :] = kv_ref[...]
        dkvp[...] = jnp.zeros(dkvp.shape, F32)

        def blk(n, carry):
            st = pl.multiple_of(n * ATTN_BLOCK, ATTN_BLOCK)
            qb = q_ref[pl.ds(st, ATTN_BLOCK), :]
            dob = do_ref[pl.ds(st, ATTN_BLOCK), :]
            kw = kvp[pl.ds(st, 2 * ATTN_BLOCK), :]
            mask = _attn_mask_t(n)
            for kh in range(N_KV_HEADS):
                kk = kw[:, kh * HEAD_DIM:(kh + 1) * HEAD_DIM]
                vv = kw[:, KV_WIDTH + kh * HEAD_DIM:KV_WIDTH + (kh + 1) * HEAD_DIM]
                qs = _stack_heads(qb, kh)
                dos = _stack_heads(dob, kh)
                probs, psink = _attn_probs_t(kk, qs, mask, s_ref[kh:kh + 1, :])
                dp = _dot_nt(vv, dos)
                dv = _dot(probs.astype(BF16), dos)
                rowdot = jnp.sum(probs * dp, axis=0, keepdims=True)
                dsc = (probs * (dp - rowdot) * (HEAD_DIM ** -0.5)).astype(BF16)
                dsacc[kh:kh + 1, :] += -psink * rowdot
                dk = _dot(dsc, qs)
                dqs = _dot_tn(dsc, kk)
                for g in range(GROUP):
                    col = (kh * GROUP + g) * HEAD_DIM
                    dq_ref[pl.ds(st, ATTN_BLOCK), col:col + HEAD_DIM] = (
                        dqs[g * ATTN_BLOCK:(g + 1) * ATTN_BLOCK].astype(dq_ref.dtype))
                dkvp[pl.ds(st, 2 * ATTN_BLOCK), kh * HEAD_DIM:(kh + 1) * HEAD_DIM] += dk
                dkvp[pl.ds(st, 2 * ATTN_BLOCK), KV_WIDTH + kh * HEAD_DIM:KV_WIDTH + (kh + 1) * HEAD_DIM] += dv
            return carry

        lax.fori_loop(0, nb, blk, 0)
        dkv_ref[...] = dkvp[ATTN_BLOCK:, :].astype(dkv_ref.dtype)

        @pl.when(pl.program_id(0) == nseq - 1)
        def _():
            for kh in range(N_KV_HEADS):
                for g in range(GROUP):
                    tot = jnp.sum(dsacc[kh:kh + 1, g * ATTN_BLOCK:(g + 1) * ATTN_BLOCK], axis=1, keepdims=True)
                    ds_ref[kh * GROUP + g:kh * GROUP + g + 1, :] = jnp.broadcast_to(tot, (1, LANES))

    seq_q = pl.BlockSpec((seq, ATTN_WIDTH), lambda b: (b, 0))
    seq_kv = pl.BlockSpec((seq, 2 * KV_WIDTH), lambda b: (b, 0))
    return _pcall(
        body, name, (nseq,),
        [seq_q, seq_kv, pl.BlockSpec((8, GQ), lambda b: (0, 0)), seq_q],
        [seq_q, seq_kv, pl.BlockSpec((N_Q_HEADS, LANES), lambda b: (0, 0))],
        [jax.ShapeDtypeStruct((nseq * seq, ATTN_WIDTH), BF16),
         jax.ShapeDtypeStruct((nseq * seq, 2 * KV_WIDTH), BF16),
         jax.ShapeDtypeStruct((N_Q_HEADS, LANES), F32)],
        [pltpu.VMEM((ATTN_BLOCK + seq, 2 * KV_WIDTH), BF16),
         pltpu.VMEM((ATTN_BLOCK + seq, 2 * KV_WIDTH), F32),
         pltpu.VMEM((8, GQ), F32)], (q, kv, sinks_t, do), ("arbitrary",), comm)


CONV_T = 128


def _conv_taps(win, w_ref, lanes, init):
    acc = init
    for j in range(CONV_KERNEL):
        sh = win if j == CONV_KERNEL - 1 else pltpu.roll(win, CONV_KERNEL - 1 - j, 0)
        acc = acc + w_ref[j:j + 1, lanes] * sh[CONV_HALO:CONV_HALO + CONV_T]
    return acc


def _conv_block(h0p, w_ref, vec_ref, st):
    cols = []
    for cs in range(CONV_WIDTH // LANES):
        lanes = slice(cs * LANES, (cs + 1) * LANES)
        win = h0p[pl.ds(st, CONV_T + CONV_HALO), lanes]
        init = jnp.broadcast_to(vec_ref[0:1, lanes], (CONV_T, LANES))
        cols.append(_conv_taps(win, w_ref, lanes, init))
    return jnp.concatenate(cols, axis=-1)


def _glu_store(c_ref, h0p, st):
    cb = c_ref[pl.ds(st, CONV_T), :]
    h0p[pl.ds(pl.multiple_of(st + CONV_HALO, CONV_HALO), CONV_T), :] = cb[:, :CONV_WIDTH] * _sigmoid(cb[:, CONV_WIDTH:])


def conv_fwd(c, w, vec, nseq, seq, name, comm=None):
    nb = seq // CONV_T

    def body(c_ref, w_ref, vec_ref, o_ref, h0p):
        h0p[0:CONV_HALO, :] = jnp.zeros((CONV_HALO, CONV_WIDTH), F32)

        def blk(n, carry):
            st = pl.multiple_of(n * CONV_T, CONV_T)
            _glu_store(c_ref, h0p, st)
            h1 = _conv_block(h0p, w_ref, vec_ref, st)
            mu = jnp.mean(h1, axis=-1, keepdims=True)
            xc = h1 - mu
            rstd = lax.rsqrt(jnp.mean(xc * xc, axis=-1, keepdims=True) + EPS)
            y = xc * rstd * vec_ref[1:2, :] + vec_ref[2:3, :]
            o_ref[pl.ds(st, CONV_T), :] = (y * _sigmoid(y)).astype(o_ref.dtype)
            return carry

        lax.fori_loop(0, nb, blk, 0)

    return _pcall(
        body, name, (nseq,),
        [pl.BlockSpec((seq, 2 * CONV_WIDTH), lambda b: (b, 0)),
         pl.BlockSpec((32, CONV_WIDTH), lambda b: (0, 0)),
         pl.BlockSpec((8, CONV_WIDTH), lambda b: (0, 0))],
        pl.BlockSpec((seq, CONV_WIDTH), lambda b: (b, 0)),
        jax.ShapeDtypeStruct((nseq * seq, CONV_WIDTH), BF16),
        [pltpu.VMEM((CONV_HALO + seq, CONV_WIDTH), F32)], (c, w, vec), ("parallel",), comm)


def conv_bwd(c, w, vec, dout, nseq, seq, name, comm=None):
    nb = seq // CONV_T

    def body(c_ref, w_ref, vec_ref, do_ref, dc_ref, dw_ref, dvec_ref, h0p, dh1p):
        @pl.when(pl.program_id(0) == 0)
        def _():
            dw_ref[...] = jnp.zeros(dw_ref.shape, F32)
            dvec_ref[...] = jnp.zeros(dvec_ref.shape, F32)

        h0p[0:CONV_HALO, :] = jnp.zeros((CONV_HALO, CONV_WIDTH), F32)
        dh1p[seq:seq + CONV_HALO, :] = jnp.zeros((CONV_HALO, CONV_WIDTH), F32)

        def pass_a(n, carry):
            st = pl.multiple_of(n * CONV_T, CONV_T)
            _glu_store(c_ref, h0p, st)
            h1 = _conv_block(h0p, w_ref, vec_ref, st)
            mu = jnp.mean(h1, axis=-1, keepdims=True)
            xc = h1 - mu
            rstd = lax.rsqrt(jnp.mean(xc * xc, axis=-1, keepdims=True) + EPS)
            xh = xc * rstd
            y = xh * vec_ref[1:2, :] + vec_ref[2:3, :]
            sg = _sigmoid(y)
            dy = do_ref[pl.ds(st, CONV_T), :] * (sg * (1.0 + y * (1.0 - sg)))
            dvec_ref[1:2, :] += jnp.sum(dy * xh, axis=0, keepdims=True)
            dvec_ref[2:3, :] += jnp.sum(dy, axis=0, keepdims=True)
            dxh = dy * vec_ref[1:2, :]
            dh1 = rstd * (dxh - jnp.mean(dxh, axis=-1, keepdims=True)
                          - xh * jnp.mean(dxh * xh, axis=-1, keepdims=True))
            dvec_ref[0:1, :] += jnp.sum(dh1, axis=0, keepdims=True)
            dh1p[pl.ds(st, CONV_T), :] = dh1
            return carry

        lax.fori_loop(0, nb, pass_a, 0)

        def pass_b(n, carry):
            st = pl.multiple_of(n * CONV_T, CONV_T)
            cols = []
            for cs in range(CONV_WIDTH // LANES):
                lanes = slice(cs * LANES, (cs + 1) * LANES)
                wind = dh1p[pl.ds(st, CONV_T + CONV_HALO), lanes]
                winh = h0p[pl.ds(st, CONV_T + CONV_HALO), lanes]
                d1 = wind[0:CONV_T]
                acc = jnp.zeros((CONV_T, LANES), F32)
                for j in range(CONV_KERNEL):
                    acc = acc + w_ref[j:j + 1, lanes] * pltpu.roll(wind, 2 + j, 0)[CONV_HALO:CONV_HALO + CONV_T]
                    hs = winh if j == CONV_KERNEL - 1 else pltpu.roll(winh, CONV_KERNEL - 1 - j, 0)
                    dw_ref[j:j + 1, lanes] += jnp.sum(d1 * hs[CONV_HALO:CONV_HALO + CONV_T], axis=0, keepdims=True)
                cols.append(acc)
            dh0 = jnp.concatenate(cols, axis=-1)
            cb = c_ref[pl.ds(st, CONV_T), :]
            av, gt = cb[:, :CONV_WIDTH], cb[:, CONV_WIDTH:]
            sg = _sigmoid(gt)
            dc_ref[pl.ds(st, CONV_T), :] = jnp.concatenate(
                [dh0 * sg, dh0 * av * sg * (1.0 - sg)], axis=-1).astype(dc_ref.dtype)
            return carry

        lax.fori_loop(0, nb, pass_b, 0)

    return _pcall(
        body, name, (nseq,),
        [pl.BlockSpec((seq, 2 * CONV_WIDTH), lambda b: (b, 0)),
         pl.BlockSpec((32, CONV_WIDTH), lambda b: (0, 0)),
         pl.BlockSpec((8, CONV_WIDTH), lambda b: (0, 0)),
         pl.BlockSpec((seq, CONV_WIDTH), lambda b: (b, 0))],
        [pl.BlockSpec((seq, 2 * CONV_WIDTH), lambda b: (b, 0)),
         pl.BlockSpec((32, CONV_WIDTH), lambda b: (0, 0)),
         pl.BlockSpec((8, CONV_WIDTH), lambda b: (0, 0))],
        [jax.ShapeDtypeStruct((nseq * seq, 2 * CONV_WIDTH), BF16),
         jax.ShapeDtypeStruct((32, CONV_WIDTH), F32),
         jax.ShapeDtypeStruct((8, CONV_WIDTH), F32)],
        [pltpu.VMEM((CONV_HALO + seq, CONV_WIDTH), F32),
         pltpu.VMEM((seq + CONV_HALO, CONV_WIDTH), F32)], (c, w, vec, dout), ("arbitrary",), comm)


POOL_T = 128


def _pooled_block(zpp, st, grp):
    lanes = slice(grp * LANES, (grp + 1) * LANES)
    win = zpp[pl.ds(st, POOL_T + POOL_HALO), lanes]
    acc = win
    for lvl in range(grp + 1):
        acc = acc + pltpu.roll(acc, 1 << lvl, 0)
    t = st + lax.broadcasted_iota(jnp.int32, (POOL_T, 1), 0)
    inv = 1.0 / jnp.minimum(t + 1, POOL_WINDOWS[grp]).astype(F32)
    return acc[POOL_HALO:] * inv - win[POOL_HALO:], inv


def pool_fwd(zp, wp, scale, nseq, seq, name):
    nb = seq // POOL_T

    def body(z_ref, wp_ref, sc_ref, o_ref, zpp):
        zpp[0:POOL_HALO, :] = jnp.zeros((POOL_HALO, POOL_WIDTH), F32)
        zpp[POOL_HALO:, :] = z_ref[...]

        def blk(n, carry):
            st = pl.multiple_of(n * POOL_T, POOL_T)
            for grp in range(len(POOL_WINDOWS)):
                lanes = slice(grp * LANES, (grp + 1) * LANES)
                pooled, _ = _pooled_block(zpp, st, grp)
                o_ref[pl.ds(st, POOL_T), lanes] = (
                    _dot(pooled.astype(BF16), wp_ref[grp]) * sc_ref[0:1, lanes]).astype(o_ref.dtype)
            return carry

        lax.fori_loop(0, nb, blk, 0)

    return pl.pallas_call(
        body, name=name, grid=(nseq,),
        in_specs=[pl.BlockSpec((seq, POOL_WIDTH), lambda b: (b, 0)),
                  pl.BlockSpec((4, LANES, LANES), lambda b: (0, 0, 0)),
                  pl.BlockSpec((1, POOL_WIDTH), lambda b: (0, 0))],
        out_specs=pl.BlockSpec((seq, POOL_WIDTH), lambda b: (b, 0)),
        out_shape=jax.ShapeDtypeStruct((nseq * seq, POOL_WIDTH), BF16),
        scratch_shapes=[pltpu.VMEM((POOL_HALO + seq, POOL_WIDTH), F32)],
        compiler_params=_params("parallel"),
    )(zp, wp, scale)


def pool_bwd(zp, wp, scale, dout, nseq, seq, name):
    nb = seq // POOL_T

    def body(z_ref, wp_ref, sc_ref, do_ref, dz_ref, dwp_ref, dsc_ref, zpp, dpcp, negd):
        @pl.when(pl.program_id(0) == 0)
        def _():
            dwp_ref[...] = jnp.zeros(dwp_ref.shape, F32)
            dsc_ref[...] = jnp.zeros(dsc_ref.shape, F32)

        zpp[0:POOL_HALO, :] = jnp.zeros((POOL_HALO, POOL_WIDTH), F32)
        zpp[POOL_HALO:, :] = z_ref[...]
        dpcp[seq:seq + POOL_HALO, :] = jnp.zeros((POOL_HALO, POOL_WIDTH), F32)

        def pass_a(n, carry):
            st = pl.multiple_of(n * POOL_T, POOL_T)
            for grp in range(len(POOL_WINDOWS)):
                lanes = slice(grp * LANES, (grp + 1) * LANES)
                pooled, inv = _pooled_block(zpp, st, grp)
                pb = pooled.astype(BF16)
                dob = do_ref[pl.ds(st, POOL_T), lanes]
                dsc_ref[0:1, lanes] += jnp.sum(dob * _dot(pb, wp_ref[grp]), axis=0, keepdims=True)
                dpm = (dob * sc_ref[0:1, lanes]).astype(BF16)
                dwp_ref[grp] += _dot_tn(pb, dpm)
                dpooled = _dot_nt(dpm, wp_ref[grp])
                negd[pl.ds(st, POOL_T), lanes] = -dpooled
                dpcp[pl.ds(st, POOL_T), lanes] = dpooled * inv
            return carry

        lax.fori_loop(0, nb, pass_a, 0)

        def pass_b(n, carry):
            st = pl.multiple_of(n * POOL_T, POOL_T)
            rows = POOL_T + POOL_HALO
            for grp in range(len(POOL_WINDOWS)):
                lanes = slice(grp * LANES, (grp + 1) * LANES)
                acc = dpcp[pl.ds(st, rows), lanes]
                for lvl in range(grp + 1):
                    acc = acc + pltpu.roll(acc, rows - (1 << lvl), 0)
                dz_ref[pl.ds(st, POOL_T), lanes] = (acc[0:POOL_T] + negd[pl.ds(st, POOL_T), lanes]).astype(dz_ref.dtype)
            return carry

        lax.fori_loop(0, nb, pass_b, 0)

    seq_spec = pl.BlockSpec((seq, POOL_WIDTH), lambda b: (b, 0))
    return pl.pallas_call(
        body, name=name, grid=(nseq,),
        in_specs=[seq_spec, pl.BlockSpec((4, LANES, LANES), lambda b: (0, 0, 0)),
                  pl.BlockSpec((1, POOL_WIDTH), lambda b: (0, 0)), seq_spec],
        out_specs=[seq_spec, pl.BlockSpec((4, LANES, LANES), lambda b: (0, 0, 0)),
                   pl.BlockSpec((8, POOL_WIDTH), lambda b: (0, 0))],
        out_shape=[jax.ShapeDtypeStruct((nseq * seq, POOL_WIDTH), BF16),
                   jax.ShapeDtypeStruct((4, LANES, LANES), F32),
                   jax.ShapeDtypeStruct((8, POOL_WIDTH), F32)],
        scratch_shapes=[pltpu.VMEM((POOL_HALO + seq, POOL_WIDTH), F32),
                        pltpu.VMEM((seq + POOL_HALO, POOL_WIDTH), F32),
                        pltpu.VMEM((seq, POOL_WIDTH), F32)],
        compiler_params=_params("arbitrary"),
    )(zp, wp, scale, dout)


GELU_C0 = 0.7978845608028654
GELU_C1 = 0.044715


def _gelu(xv):
    return xv * (0.5 * (1.0 + jnp.tanh(GELU_C0 * (xv + GELU_C1 * (xv * xv * xv)))))


def _gelu_grad(xv):
    t = jnp.tanh(GELU_C0 * (xv + GELU_C1 * (xv * xv * xv)))
    return 0.5 * (1.0 + t) + 0.5 * xv * (1.0 - t * t) * (GELU_C0 * (1.0 + 3.0 * GELU_C1 * xv * xv))


def _tril():
    ti = lax.broadcasted_iota(jnp.int32, (SGU_CHUNK, SGU_CHUNK), 0)
    si = lax.broadcasted_iota(jnp.int32, (SGU_CHUNK, SGU_CHUNK), 1)
    return si <= ti


def sgu_fwd(zs, ws, bst, ln, nseq, seq, name):
    nc = seq // SGU_CHUNK

    def body(z_ref, ws_ref, bs_ref, ln_ref, o_ref):
        tril = _tril()

        def blk(n, carry):
            st = pl.multiple_of(n * SGU_CHUNK, SGU_CHUNK)
            ge = _gelu(z_ref[pl.ds(st, SGU_CHUNK), :])
            uu, vv = ge[:, :SGU_WIDTH], ge[:, SGU_WIDTH:]
            mu = jnp.mean(vv, axis=-1, keepdims=True)
            xc = vv - mu
            rstd = lax.rsqrt(jnp.mean(xc * xc, axis=-1, keepdims=True) + EPS)
            vn = (xc * rstd * ln_ref[0:1, :] + ln_ref[1:2, :]).astype(BF16)
            for g in range(4):
                lanes = slice(g * LANES, (g + 1) * LANES)
                wm = jnp.where(tril, ws_ref[g], 0.0).astype(BF16)
                mixed = _dot(wm, vn[:, lanes]) + bs_ref[:, g:g + 1]
                o_ref[pl.ds(st, SGU_CHUNK), lanes] = (uu[:, lanes] * mixed).astype(o_ref.dtype)
            return carry

        lax.fori_loop(0, nc, blk, 0)

    return pl.pallas_call(
        body, name=name, grid=(nseq,),
        in_specs=[pl.BlockSpec((seq, 2 * SGU_WIDTH), lambda b: (b, 0)),
                  pl.BlockSpec((4, LANES, LANES), lambda b: (0, 0, 0)),
                  pl.BlockSpec((SGU_CHUNK, 4), lambda b: (0, 0)),
                  pl.BlockSpec((8, SGU_WIDTH), lambda b: (0, 0))],
        out_specs=pl.BlockSpec((seq, SGU_WIDTH), lambda b: (b, 0)),
        out_shape=jax.ShapeDtypeStruct((nseq * seq, SGU_WIDTH), BF16),
        compiler_params=_params("parallel"),
    )(zs, ws, bst, ln)


def sgu_bwd(zs, ws, bst, ln, dout, nseq, seq, name):
    nc = seq // SGU_CHUNK

    def body(z_ref, ws_ref, bs_ref, ln_ref, do_ref, dz_ref, dws_ref, dbs_ref, dln_ref):
        @pl.when(pl.program_id(0) == 0)
        def _():
            dws_ref[...] = jnp.zeros(dws_ref.shape, F32)
            dbs_ref[...] = jnp.zeros(dbs_ref.shape, F32)
            dln_ref[...] = jnp.zeros(dln_ref.shape, F32)

        tril = _tril()

        def blk(n, carry):
            st = pl.multiple_of(n * SGU_CHUNK, SGU_CHUNK)
            zv = z_ref[pl.ds(st, SGU_CHUNK), :]
            ge = _gelu(zv)
            uu, vv = ge[:, :SGU_WIDTH], ge[:, SGU_WIDTH:]
            mu = jnp.mean(vv, axis=-1, keepdims=True)
            xc = vv - mu
            rstd = lax.rsqrt(jnp.mean(xc * xc, axis=-1, keepdims=True) + EPS)
            xh = xc * rstd
            vn = (xh * ln_ref[0:1, :] + ln_ref[1:2, :]).astype(BF16)
            dob = do_ref[pl.ds(st, SGU_CHUNK), :]
            du_cols, dvn_cols = [], []
            for g in range(4):
                lanes = slice(g * LANES, (g + 1) * LANES)
                wm = jnp.where(tril, ws_ref[g], 0.0).astype(BF16)
                mixed = _dot(wm, vn[:, lanes]) + bs_ref[:, g:g + 1]
                du_cols.append(dob[:, lanes] * mixed)
                dmix = dob[:, lanes] * uu[:, lanes]
                dbs_ref[g] += jnp.broadcast_to(jnp.sum(dmix, axis=-1, keepdims=True), (SGU_CHUNK, LANES))
                dmb = dmix.astype(BF16)
                dws_ref[g] += jnp.where(tril, _dot_nt(dmb, vn[:, lanes]), 0.0)
                dvn_cols.append(_dot_tn(wm, dmb))
            dvn = jnp.concatenate(dvn_cols, axis=-1)
            dln_ref[0:1, :] += jnp.sum(dvn * xh, axis=0, keepdims=True)
            dln_ref[1:2, :] += jnp.sum(dvn, axis=0, keepdims=True)
            dxh = dvn * ln_ref[0:1, :]
            dv = rstd * (dxh - jnp.mean(dxh, axis=-1, keepdims=True)
                         - xh * jnp.mean(dxh * xh, axis=-1, keepdims=True))
            dge = jnp.concatenate(du_cols + [dv], axis=-1)
            dz_ref[pl.ds(st, SGU_CHUNK), :] = (dge * _gelu_grad(zv)).astype(dz_ref.dtype)
            return carry

        lax.fori_loop(0, nc, blk, 0)

    w_spec = pl.BlockSpec((4, LANES, LANES), lambda b: (0, 0, 0))
    ln_spec = pl.BlockSpec((8, SGU_WIDTH), lambda b: (0, 0))
    return pl.pallas_call(
        body, name=name, grid=(nseq,),
        in_specs=[pl.BlockSpec((seq, 2 * SGU_WIDTH), lambda b: (b, 0)), w_spec,
                  pl.BlockSpec((SGU_CHUNK, 4), lambda b: (0, 0)), ln_spec,
                  pl.BlockSpec((seq, SGU_WIDTH), lambda b: (b, 0))],
        out_specs=[pl.BlockSpec((seq, 2 * SGU_WIDTH), lambda b: (b, 0)), w_spec, w_spec, ln_spec],
        out_shape=[jax.ShapeDtypeStruct((nseq * seq, 2 * SGU_WIDTH), BF16),
                   jax.ShapeDtypeStruct((4, LANES, LANES), F32),
                   jax.ShapeDtypeStruct((4, LANES, LANES), F32),
                   jax.ShapeDtypeStruct((8, SGU_WIDTH), F32)],
        compiler_params=_params("arbitrary"),
    )(zs, ws, bst, ln, dout)


def _ew_rows(rows, cols, nbuf):
    t = _row_tile(rows, 1024)
    while t > 8 and t * cols * 4 * nbuf * 2 > 24 * 2**20:
        t //= 2
    return t


def adamw(w, g, m, v, name):
    layers, rows, cols = w.shape
    tr = _ew_rows(rows, cols, 7)

    def body(w_ref, g_ref, m_ref, v_ref, d_ref, mo_ref, vo_ref):
        gv = g_ref[...]
        mn = ADAM_B1 * m_ref[...] + (1.0 - ADAM_B1) * gv
        vn = ADAM_B2 * v_ref[...] + (1.0 - ADAM_B2) * (gv * gv)
        m_hat = mn / (1.0 - ADAM_B1 ** ADAM_STEP)
        v_hat = vn / (1.0 - ADAM_B2 ** ADAM_STEP)
        d_ref[...] = -ADAM_LR * (m_hat / (jnp.sqrt(v_hat) + ADAM_EPS) + ADAM_WD * w_ref[...])
        mo_ref[...] = mn
        vo_ref[...] = vn

    spec = pl.BlockSpec((1, tr, cols), lambda l, i: (l, i, 0))
    return pl.pallas_call(
        body, name=name, grid=(layers, rows // tr),
        in_specs=[spec] * 4, out_specs=[spec] * 3,
        out_shape=[jax.ShapeDtypeStruct(w.shape, F32)] * 3,
        compiler_params=_params("parallel", "parallel"),
    )(w, g, m, v)


def add_cast(a, b, name, dtype=BF16):
    nslab, rows, cols = a.shape
    tr = _ew_rows(rows, cols, 3)

    def body(a_ref, b_ref, o_ref):
        o_ref[...] = (a_ref[...] + b_ref[...]).astype(dtype)

    spec = pl.BlockSpec((1, tr, cols), lambda k, i: (k, i, 0))
    return pl.pallas_call(
        body, name=name, grid=(nslab, rows // tr),
        in_specs=[spec, spec], out_specs=spec,
        out_shape=jax.ShapeDtypeStruct(a.shape, dtype),
        compiler_params=_params("parallel", "parallel"),
    )(a, b)


def pair_sum(t, got, core, name):
    nslab, h, cols = got.shape
    tr = _ew_rows(h, cols, 3)
    nb = h // tr

    def body(c_ref, a_ref, b_ref, o_ref):
        o_ref[...] = (a_ref[...] + b_ref[...]).astype(BF16)

    spec = pl.BlockSpec((1, tr, cols), lambda k, i, c: (k, i, 0))
    return pl.pallas_call(
        body, name=name,
        grid_spec=pltpu.PrefetchScalarGridSpec(
            num_scalar_prefetch=1, grid=(nslab, nb),
            in_specs=[pl.BlockSpec((1, tr, cols), lambda k, i, c: (k, c[0] * nb + i, 0)), spec],
            out_specs=spec),
        out_shape=jax.ShapeDtypeStruct(got.shape, BF16),
        compiler_params=_params("parallel", "parallel"),
    )(core, t, got)


def sum_parts(parts, name, first=None):
    npart, rows, cols = parts.shape
    tr = _ew_rows(rows, cols, npart + 2)

    def body(*refs):
        p_ref, o_ref = refs[-2], refs[-1]
        acc = p_ref[0].astype(F32) if first is None else refs[0][...].astype(F32) + p_ref[0].astype(F32)
        for j in range(1, npart):
            acc = acc + p_ref[j].astype(F32)
        o_ref[...] = acc

    row = pl.BlockSpec((tr, cols), lambda i: (i, 0))
    ins = [parts] if first is None else [first, parts]
    return pl.pallas_call(
        body, name=name, grid=(rows // tr,),
        in_specs=([] if first is None else [row]) + [pl.BlockSpec((npart, tr, cols), lambda i: (0, i, 0))],
        out_specs=row,
        out_shape=jax.ShapeDtypeStruct((rows, cols), F32),
        compiler_params=_params("parallel"),
    )(*ins)


ANY = pl.BlockSpec(memory_space=pl.ANY)
MESH = pl.DeviceIdType.MESH


def _me():
    return lax.axis_index("x"), lax.axis_index("y"), lax.axis_index("c")


def _flip(pos, rel):
    return tuple(1 - p if f else p for p, f in zip(pos, rel))


SIBLING = (0, 0, 1)
OTHER_CHIPS = ((1, 0, 0), (0, 1, 0), (1, 1, 0))


def _chip_of(pos, rel=(0, 0, 0)):
    px, py, _ = _flip(pos, rel)
    return 2 * px + py


def allgather_blocks(shards, name):
    nt = len(shards)
    hs = [s.shape[0] // 2 for s in shards]

    def body(*refs):
        ins, outs = refs[:nt], refs[nt:2 * nt]
        send_sems, recv_sems, loc_sems = refs[2 * nt:]
        pos = _me()
        x, y, c = pos

        def block_id(rel):
            px, py, pc = _flip(pos, rel)
            return 4 * px + 2 * py + pc

        def copy(t, k, block_rel, to_rel, src=None):
            dst = outs[t].at[block_id(block_rel)]
            return pltpu.make_async_remote_copy(
                src_ref=dst if src is None else src, dst_ref=dst,
                send_sem=send_sems.at[t * 7 + k], recv_sem=recv_sems.at[t * 7 + k],
                device_id=_flip(pos, to_rel), device_id_type=MESH)

        own = [ins[t].at[pl.ds(c * hs[t], hs[t])] for t in range(nt)]
        mine = [pltpu.make_async_copy(own[t], outs[t].at[block_id((0, 0, 0))], loc_sems.at[t]) for t in range(nt)]
        for cp in mine:
            cp.start()
        first = []
        for t in range(nt):
            first.append(copy(t, 0, (0, 0, 0), SIBLING, src=own[t]))
            first += [copy(t, 1 + j, (0, 0, 0), rel, src=own[t]) for j, rel in enumerate(OTHER_CHIPS)]
        for cp in first:
            cp.start()
        passed = []
        for j, rel in enumerate(OTHER_CHIPS):
            for t in range(nt):
                copy(t, 1 + j, rel, (0, 0, 0)).wait_recv()
                fwd = copy(t, 4 + j, rel, SIBLING)
                fwd.start()
                passed.append(fwd)
        for t in range(nt):
            copy(t, 0, SIBLING, (0, 0, 0)).wait_recv()
            for j, rel in enumerate(OTHER_CHIPS):
                copy(t, 4 + j, (rel[0], rel[1], 1), (0, 0, 0)).wait_recv()
        for cp in first + passed:
            cp.wait_send()
        for cp in mine:
            cp.wait()

    return pl.pallas_call(
        body, name=name,
        in_specs=[ANY] * nt, out_specs=[ANY] * nt,
        out_shape=[jax.ShapeDtypeStruct((N_DEV, h, s.shape[1]), s.dtype) for h, s in zip(hs, shards)],
        scratch_shapes=[pltpu.SemaphoreType.DMA((7 * nt,)), pltpu.SemaphoreType.DMA((7 * nt,)),
                        pltpu.SemaphoreType.DMA((nt,))],
    )(*shards)


def _block_id(pos, rel=(0, 0, 0)):
    px, py, pc = _flip(pos, rel)
    return 4 * px + 2 * py + pc


def gather_first_hop(shards):
    hs = [s.shape[0] // 2 for s in shards]

    def plan(ins, outs, pos):
        me = _block_id(pos)
        remote = []
        for i, o, h in zip(ins, outs, hs):
            own = i.at[pl.ds(pos[2] * h, h)]
            remote += [(rel, own, o.at[me]) for rel in (SIBLING,) + OTHER_CHIPS]
        return remote

    return Comm(shards, [((N_DEV, h, s.shape[1]), s.dtype) for h, s in zip(hs, shards)], plan, 4 * len(shards))


def gather_second_hop(gathered):
    def plan(ins, outs, pos):
        remote = []
        for i, o in zip(ins, outs):
            for rel in OTHER_CHIPS:
                blk = _block_id(pos, rel)
                remote.append((SIBLING, i.at[blk], o.at[blk]))
        return remote

    return Comm(gathered, [(g.shape, g.dtype) for g in gathered], plan, 3 * len(gathered),
                aliases={i: i for i in range(len(gathered))})


def swap_comm(xs):
    def plan(ins, outs, pos):
        return [(SIBLING, i, o) for i, o in zip(ins, outs)]

    return Comm(list(xs), [(v.shape, v.dtype) for v in xs], plan, len(xs))


def give_half_comm(ts, plain=()):
    nt = len(ts)

    def plan(ins, outs, pos):
        remote = []
        for i, o in zip(ins[:nt], outs[:nt]):
            h = o.shape[1]
            remote.append((SIBLING, i.at[:, pl.ds((1 - pos[2]) * h, h)], o))
        return remote + [(SIBLING, i, o) for i, o in zip(ins[nt:], outs[nt:])]

    shapes = [((t.shape[0], t.shape[1] // 2, t.shape[2]), t.dtype) for t in ts] + [(v.shape, v.dtype) for v in plain]
    return Comm(list(ts) + list(plain), shapes, plan, nt + len(plain))


def chip_scatter_comm(xs, shared=None):
    nx = len(xs)

    def plan(ins, outs, pos):
        me = _chip_of(pos)
        remote = []
        for i, o in zip(ins[:nx], outs[:nx]):
            remote += [(rel, i.at[_chip_of(pos, rel)], o.at[j]) for j, rel in enumerate(OTHER_CHIPS)]
        if shared is not None:
            remote += [(rel, ins[nx], outs[nx].at[me]) for rel in OTHER_CHIPS]
        return remote

    shapes = [((3,) + v.shape[1:], v.dtype) for v in xs]
    if shared is not None:
        shapes.append(((N_CHIPS,) + shared.shape, shared.dtype))
    return Comm(list(xs) + ([] if shared is None else [shared]), shapes, plan, 3 * nx + (0 if shared is None else 3))


def run_comm(comm, name):
    return _pcall(lambda: None, name, (1,), [], [], [], [], (), ("arbitrary",), comm)[1]


PACK_ROWS = 256


def _pack(arrs):
    parts, layout = [], []
    row = 0
    for a in arrs:
        flat = a.reshape(-1).astype(F32)
        size = flat.shape[0]
        rows = -(-size // (8 * LANES)) * 8
        flat = jnp.pad(flat, (0, rows * LANES - size))
        parts.append(flat.reshape(rows, LANES))
        layout.append((row, rows, size, a.shape))
        row += rows
    if row % PACK_ROWS:
        parts.append(jnp.zeros((PACK_ROWS - row % PACK_ROWS, LANES), F32))
    return jnp.concatenate(parts, axis=0), layout


def _unpack(packed, layout):
    return [packed[r0:r0 + rows].reshape(-1)[:size].reshape(shape) for r0, rows, size, shape in layout]


SMALL_REPL = ['mix_norm', 'a_b_in', 'a_sinks', 'a_conv_b', 'a_cln_g', 'a_cln_b', 'c_w_pool', 'c_w_s', 'c_b_s',
              'ffn_norm', 'final_norm']
SMALL_SHARD = ['a_conv_w', 'c_pool_scale', 'c_sln_g', 'c_sln_b']
BIG = ['a_w_in', 'a_w_out', 'c_w_in', 'c_w_out', 'ffn_w_gate', 'ffn_w_up', 'ffn_w_down']
TRANSPOSED = ('a_w_in', 'ffn_w_gate', 'ffn_w_up')
BIG_COL_SHARDED = {'c_w_in'}


def _full_weight(name, g8):
    _, h, cols = g8.shape
    g4 = g8.reshape(N_CHIPS, 2 * h, cols)
    if name not in BIG_COL_SHARDED:
        return g4.reshape(-1, cols)
    return jnp.transpose(g4, (1, 0, 2)).reshape(2 * h, N_CHIPS * cols)


def _to_shard_major(name, f):
    if name not in BIG_COL_SHARDED:
        return f.reshape(N_CHIPS, f.shape[0] // N_CHIPS, f.shape[1])
    r, cfull = f.shape
    return jnp.transpose(f.reshape(r, N_CHIPS, cfull // N_CHIPS), (1, 0, 2))


def kernel(*args):
    a = dict(zip(IN_NAMES, args))
    bl, seq, _ = a['x'].shape
    n = bl * seq
    x = a['x'].reshape(n, D_MODEL)
    target = a['loss_target'].reshape(n, D_MODEL)
    xi, yi, ci = _me()
    chip = 2 * xi + yi

    shard = {'a_w_in': a['a_w_in'][0].T, 'a_w_out': a['a_w_out'][0], 'c_w_in': a['c_w_in'][0], 'c_w_out': a['c_w_out'][0]}
    for layer in range(2):
        shard['gate' + str(layer)] = a['ffn_w_gate'][layer].T
        shard['up' + str(layer)] = a['ffn_w_up'][layer].T
        shard['down' + str(layer)] = a['ffn_w_down'][layer]
    shard = {k: v.astype(BF16) for k, v in shard.items()}
    core = ci.astype(jnp.int32).reshape(1)
    block_id = 4 * xi + 2 * yi + ci

    def first_hop(*names):
        return gather_first_hop([shard[k] for k in names])

    def finish(name, g8):
        h = shard[name].shape[0] // 2
        own = lax.dynamic_slice_in_dim(shard[name], ci * h, h, axis=0)
        return _full_weight(name, lax.dynamic_update_slice_in_dim(g8, own[None], block_id, axis=0))

    a_w_in_t = _full_weight('a_w_in', allgather_blocks([shard['a_w_in']], "gather_a_w_in")[0])
    in0_width = a_w_in_t.shape[0]
    small_shard_pack, small_shard_layout = _pack([a[k] for k in SMALL_SHARD])
    hop_a = first_hop('a_w_out', 'gate0')
    hop_s = chip_scatter_comm([], shared=small_shard_pack)
    mix_norm, ffn_norm = a['mix_norm'], a['ffn_norm']
    (hn0, q, kv, cc), outs = norm_inproj(
        x, mix_norm[0:1], a_w_in_t, a['a_b_in'],
        [(0, ATTN_WIDTH), (ATTN_WIDTH, ATTN_WIDTH + 2 * KV_WIDTH), (ATTN_WIDTH + 2 * KV_WIDTH, in0_width)],
        [BF16, BF16, F32], "in_proj0", comm=hop_a + hop_s, w_transposed=True)
    got_a, (ss,) = hop_a.split(outs, hop_s)
    ss = lax.dynamic_update_slice_in_dim(ss, small_shard_pack[None], chip, axis=0)
    ss_full = []
    for r0, rows, size, shape in small_shard_layout:
        per_chip = ss[:, r0:r0 + rows].reshape(N_CHIPS, -1)[:, :size].reshape((N_CHIPS,) + shape)
        ss_full.append(jnp.concatenate([per_chip[k] for k in range(N_CHIPS)], axis=-1))
    a_conv_w, c_pool_scale, c_sln_g, c_sln_b = [v[0] for v in ss_full]

    conv_taps = jnp.pad(a_conv_w, ((0, 32 - CONV_KERNEL), (0, 0)))
    conv_vec = jnp.pad(jnp.stack([a['a_conv_b'][0], a['a_cln_g'][0], a['a_cln_b'][0]]), ((0, 5), (0, 0)))
    sinks_b = jnp.pad(jnp.repeat(a['a_sinks'][0].reshape(N_KV_HEADS, GROUP), ATTN_BLOCK, axis=1), ((0, 6), (0, 0)))
    w_pool_bf = a['c_w_pool'][0].astype(BF16)
    pool_scale = c_pool_scale.reshape(1, POOL_WIDTH)
    w_s = a['c_w_s'][0]
    b_s_t = a['c_b_s'][0].T
    sgu_ln = jnp.pad(jnp.stack([c_sln_g, c_sln_b]), ((0, 6), (0, 0)))
    final_norm = a['final_norm'].reshape(1, D_MODEL)

    hop_b, pass_a = first_hop('up0', 'down0'), gather_second_hop(got_a)
    attn, outs = attn_fwd(q, kv, sinks_b, bl, seq, "attn_fwd", comm=hop_b + pass_a)
    got_b, done = hop_b.split(outs, pass_a)
    a_w_out, wg0 = finish('a_w_out', done[0]), finish('gate0', done[1])

    hop_c, pass_b = first_hop('c_w_in', 'c_w_out', 'gate1'), gather_second_hop(got_b)
    conv, outs = conv_fwd(cc, conv_taps, conv_vec, bl, seq, "conv_fwd", comm=hop_c + pass_b)
    got_c, done = hop_c.split(outs, pass_b)
    wu0, wd0 = finish('up0', done[0]), finish('down0', done[1])

    h1, got_d = out_proj(x, attn, conv, a_w_out, "out_proj0", comm=first_hop('up1'))

    hop_e, pass_c = first_hop('down1'), gather_second_hop(got_c + got_d)
    (hnf0, g0, u0), outs = ffn_gate_up(h1, ffn_norm[0:1], wg0, wu0, "ffn_gate_up0", comm=hop_e + pass_c)
    got_e, done = hop_e.split(outs, pass_c)
    c_w_in, c_w_out = finish('c_w_in', done[0]), finish('c_w_out', done[1])
    wg1, wu1 = finish('gate1', done[2]), finish('up1', done[3])

    h2, done = ffn_down(h1, g0, u0, wd0, "ffn_down0", comm=gather_second_hop(got_e))
    wd1 = finish('down1', done[0])
    wg, wu, wd = [wg0, wg1], [wu0, wu1], [wd0, wd1]

    (hn1, zp, zs), _ = norm_inproj(
        h2, mix_norm[1:2], c_w_in, jnp.zeros((1, c_w_in.shape[1]), F32),
        [(0, POOL_WIDTH), (POOL_WIDTH, c_w_in.shape[1])], [F32, F32], "in_proj1")
    pool = pool_fwd(zp, w_pool_bf, pool_scale, bl, seq, "pool_fwd")
    sgu = sgu_fwd(zs, w_s, b_s_t, sgu_ln, bl, seq, "sgu_fwd")
    h3, _ = out_proj(h2, pool, sgu, c_w_out, "out_proj1")
    (hnf1, g1, u1), _ = ffn_gate_up(h3, ffn_norm[1:2], wg1, wu1, "ffn_gate_up1")
    h4, _ = ffn_down(h3, g1, u1, wd1, "ffn_down1")

    dh4, d_final_norm, loss_local = loss_head(h4, final_norm, target, "loss_head")

    grads = {}
    pieces = {}

    def slabs_of(names, fulls):
        return [_to_shard_major(k, fulls[k]) for k in names]

    def pair_sums_of(names, slabs, gots):
        return [pair_sum(t, gt, core, "pair_sum_" + k) for k, t, gt in zip(names, slabs, gots)]

    def chip_sums_of(names, sums, from_chips):
        own = [lax.dynamic_index_in_dim(p, chip, axis=0, keepdims=False) for p in sums]
        return [sum_parts(p, "chip_sum_" + k, first=o) for k, p, o in zip(names, from_chips, own)]

    (dg, du, act), _ = ffn_down_bwd(dh4, g1, u1, wd[1], "ffn_down_bwd1")
    full1 = {'down1': mm_tn(act, dh4, "dw_down1"), 'gate1': mm_tn(dg, hnf1, "dw_gate1"),
             'up1': mm_tn(du, hnf1, "dw_up1")}
    dh3, d_ffn_norm1, _ = proj_rms_bwd([dg, du], [wg[1], wu[1]], h3, ffn_norm[1:2], dh4, 1, "ffn_up_bwd1",
                                       tm_pref=256, w_transposed=True)
    d_pool, d_sgu = out_proj_bwd(dh3, c_w_out, [F32, F32], "out_proj_bwd1")
    full1['c_w_out'] = jnp.concatenate([mm_tn(pool, dh3, "dw_out1_pool"), mm_tn(sgu, dh3, "dw_out1_sgu")], axis=0)
    dzp, d_w_pool, d_pool_scale = pool_bwd(zp, w_pool_bf, pool_scale, d_pool, bl, seq, "pool_bwd")
    dzs, d_w_s, d_b_s_b, d_sgu_ln = sgu_bwd(zs, w_s, b_s_t, sgu_ln, d_sgu, bl, seq, "sgu_bwd")
    full1['c_w_in'] = jnp.concatenate([mm_tn(hn1, dzp, "dw_in1_pool"), mm_tn(hn1, dzs, "dw_in1_sgu")], axis=1)
    names1 = ['gate1', 'up1', 'down1', 'c_w_out', 'c_w_in']
    slabs1 = slabs_of(names1, full1)
    dh2, d_mix_norm1, got1 = proj_rms_bwd([dzp, dzs], [c_w_in[:, :POOL_WIDTH], c_w_in[:, POOL_WIDTH:]], h2,
                                          mix_norm[1:2], dh3, 1, "in_proj_bwd1", comm=give_half_comm(slabs1))
    sums1 = pair_sums_of(names1, slabs1, got1)

    (dg, du, act), from_chips1 = ffn_down_bwd(dh2, g0, u0, wd[0], "ffn_down_bwd0", comm=chip_scatter_comm(sums1))
    mine1 = chip_sums_of(names1, sums1, from_chips1)
    full0 = {'down0': mm_tn(act, dh2, "dw_down0"), 'gate0': mm_tn(dg, hnf0, "dw_gate0"),
             'up0': mm_tn(du, hnf0, "dw_up0")}
    names0 = ['gate0', 'up0', 'down0']
    slabs0 = slabs_of(names0, full0)
    join1, pair0 = swap_comm(mine1), give_half_comm(slabs0)
    dh1, d_ffn_norm0, outs = proj_rms_bwd([dg, du], [wg[0], wu[0]], h1, ffn_norm[0:1], dh2, 1, "ffn_up_bwd0",
                                          tm_pref=256, comm=join1 + pair0, w_transposed=True)
    theirs1, got0 = join1.split(outs, pair0)
    pieces.update({k: (m, t) for k, m, t in zip(names1, mine1, theirs1)})
    sums0 = pair_sums_of(names0, slabs0, got0)

    d_attn, d_conv = out_proj_bwd(dh1, a_w_out, [BF16, F32], "out_proj_bwd0")
    full_o = {'a_w_out': jnp.concatenate([mm_tn(attn, dh1, "dw_out0_attn"), mm_tn(conv, dh1, "dw_out0_conv")], axis=0)}
    slabs_o = slabs_of(['a_w_out'], full_o)
    chips0, pair_o = chip_scatter_comm(sums0), give_half_comm(slabs_o)
    (dq, dkv, d_sinks_b), outs = attn_bwd(q, kv, sinks_b, d_attn, bl, seq, "attn_bwd", comm=chips0 + pair_o)
    from_chips0, got_o = chips0.split(outs, pair_o)
    mine0 = chip_sums_of(names0, sums0, from_chips0)
    sums_o = pair_sums_of(['a_w_out'], slabs_o, got_o)
    join0, chips_o = swap_comm(mine0), chip_scatter_comm(sums_o)
    (dcc, d_conv_taps, d_conv_vec), outs = conv_bwd(cc, conv_taps, conv_vec, d_conv, bl, seq, "conv_bwd",
                                                    comm=join0 + chips_o)
    theirs0, from_chips_o = join0.split(outs, chips_o)
    pieces.update({k: (m, t) for k, m, t in zip(names0, mine0, theirs0)})
    mine_o = chip_sums_of(['a_w_out'], sums_o, from_chips_o)
    kq, kk = ATTN_WIDTH, ATTN_WIDTH + 2 * KV_WIDTH
    grad_x, d_mix_norm0, _ = proj_rms_bwd([dq, dkv, dcc], [a_w_in_t[:kq], a_w_in_t[kq:kk], a_w_in_t[kk:]], x,
                                          mix_norm[0:1], dh1, 1, "in_proj_bwd0", w_transposed=True)
    dw_q, db_q = mm_tn(dq, hn0, "dw_in0_q", xsum=True)
    dw_kv, db_kv = mm_tn(dkv, hn0, "dw_in0_kv", xsum=True)
    (dw_c, db_c), theirs_o = mm_tn(dcc, hn0, "dw_in0_c", xsum=True, comm=swap_comm(mine_o))
    pieces['a_w_out'] = (mine_o[0], theirs_o[0])
    d_a_b_in = jnp.concatenate([db_q, db_kv, db_c], axis=0)
    slabs_i = slabs_of(['a_w_in'], {'a_w_in': jnp.concatenate([dw_q, dw_kv, dw_c], axis=0)})

    small_full = {
        'mix_norm': jnp.stack([d_mix_norm0, d_mix_norm1]), 'a_b_in': d_a_b_in[None], 'a_sinks': d_sinks_b[:, 0][None],
        'a_conv_w': d_conv_taps[:CONV_KERNEL][None], 'a_conv_b': d_conv_vec[0][None], 'a_cln_g': d_conv_vec[1][None],
        'a_cln_b': d_conv_vec[2][None], 'c_w_pool': d_w_pool[None], 'c_pool_scale': d_pool_scale[0][None],
        'c_sln_g': d_sgu_ln[0][None], 'c_sln_b': d_sgu_ln[1][None], 'c_w_s': d_w_s[None],
        'c_b_s': d_b_s_b[:, :, 0][None], 'ffn_norm': jnp.stack([d_ffn_norm0, d_ffn_norm1]),
        'final_norm': d_final_norm, 'loss': loss_local.reshape(1)}
    small_names = SMALL_REPL + SMALL_SHARD
    small_pack, small_layout = _pack([small_full[k] for k in small_names + ['loss']])

    got_i, got_s = run_comm(give_half_comm(slabs_i, plain=[small_pack]), "tail_pair")
    sums_i = pair_sums_of(['a_w_in'], slabs_i, [got_i])
    small_pair = add_cast(small_pack[None], got_s[None], "pair_sum_small", dtype=F32)[0]
    outs = run_comm(chip_scatter_comm(sums_i, shared=small_pair), "tail_chips")
    mine_i = chip_sums_of(['a_w_in'], sums_i, outs[:1])
    small_chips = lax.dynamic_update_slice_in_dim(outs[1], small_pair[None], chip, axis=0)
    theirs_i = run_comm(swap_comm(mine_i), "tail_join")
    pieces['a_w_in'] = (mine_i[0], theirs_i[0])

    def whole(name):
        mine, theirs = pieces[name]
        return jnp.concatenate([jnp.where(ci == 0, mine, theirs), jnp.where(ci == 0, theirs, mine)], axis=0)

    for k in ('a_w_in', 'a_w_out', 'c_w_in', 'c_w_out'):
        grads[k] = whole(k)[None]
    for short, key in (('gate', 'ffn_w_gate'), ('up', 'ffn_w_up'), ('down', 'ffn_w_down')):
        grads[key] = jnp.stack([whole(short + '0'), whole(short + '1')])

    small_sum = sum_parts(small_chips, "small_sum")
    for k, g in zip(small_names + ['loss'], _unpack(small_sum, small_layout)):
        if k in SMALL_SHARD:
            width = a[k].shape[-1]
            g = lax.dynamic_slice_in_dim(g, chip * width, width, axis=g.ndim - 1)
        grads[k] = g
    loss = grads.pop('loss')[0]

    delta, new_m, new_v = {}, {}, {}
    for k in BIG:
        if k in TRANSPOSED:
            flip = lambda t: jnp.swapaxes(t, 1, 2)
            d, m, v = adamw(flip(a[k]), grads[k], flip(a['m_' + k]), flip(a['v_' + k]), "adamw_" + k)
            grads[k], delta[k], new_m[k], new_v[k] = flip(grads[k]), flip(d), flip(m), flip(v)
        else:
            delta[k], new_m[k], new_v[k] = adamw(a[k], grads[k], a['m_' + k], a['v_' + k], "adamw_" + k)
    packs = [_pack([src[k] for k in small_names])
             for src in (a, grads, {k: a['m_' + k] for k in small_names}, {k: a['v_' + k] for k in small_names})]
    d, m, v = adamw(packs[0][0][None], packs[1][0][None], packs[2][0][None], packs[3][0][None], "adamw_small")
    d, m, v = d[0], m[0], v[0]
    lay = packs[0][1]
    for k, dv, mv, vv in zip(small_names, _unpack(d, lay), _unpack(m, lay), _unpack(v, lay)):
        delta[k], new_m[k], new_v[k] = dv, mv, vv

    return (loss, grad_x.reshape(a['x'].shape), *[grads[k] for k in WEIGHTS], *[delta[k] for k in WEIGHTS],
            *[new_m[k] for k in WEIGHTS], *[new_v[k] for k in WEIGHTS])
```

```python
import functools

import jax
import jax.numpy as jnp
from jax import lax
from jax.experimental import pallas as pl
from jax.experimental.pallas import tpu as pltpu

F32 = jnp.float32
BF16 = jnp.bfloat16

D_MODEL = 1024
EPS = 1e-5
N_Q_HEADS, N_KV_HEADS, HEAD_DIM = 8, 2, 64
ATTN_BLOCK = 128
ATTN_WIDTH = N_Q_HEADS * HEAD_DIM
KV_WIDTH = N_KV_HEADS * HEAD_DIM
CONV_WIDTH = 512
CONV_KERNEL = 31
CONV_HALO = 32
POOL_WINDOWS = (2, 4, 8, 16)
POOL_WIDTH = 512
POOL_HALO = 16
SGU_WIDTH = 512
SGU_CHUNK = 128
D_FF = 2816
FF_CHUNK = 128
MXU_COLS = 256
LANES = 128
N_CHIPS = 4
N_DEV = 8

ADAM_LR, ADAM_B1, ADAM_B2, ADAM_EPS, ADAM_WD, ADAM_STEP = 0.001, 0.9, 0.999, 1e-08, 0.01, 10

VMEM_LIMIT = 56 * 2**20

WEIGHTS = ['mix_norm', 'a_w_in', 'a_b_in', 'a_sinks', 'a_conv_w', 'a_conv_b', 'a_cln_g', 'a_cln_b', 'a_w_out',
           'c_w_in', 'c_w_pool', 'c_pool_scale', 'c_sln_g', 'c_sln_b', 'c_w_s', 'c_b_s', 'c_w_out',
           'ffn_norm', 'ffn_w_gate', 'ffn_w_up', 'ffn_w_down', 'final_norm']
IN_NAMES = (['x'] + WEIGHTS + ['loss_target'] + ['m_' + n for n in WEIGHTS] + ['v_' + n for n in WEIGHTS])


def _params(*sem):
    return pltpu.CompilerParams(dimension_semantics=sem, vmem_limit_bytes=VMEM_LIMIT)


def _dot(a, b):
    return jnp.dot(a, b, preferred_element_type=F32)


def _dot_nt(a, b):
    return lax.dot_general(a, b, (((1,), (1,)), ((), ())), preferred_element_type=F32)


def _dot_tn(a, b):
    return lax.dot_general(a, b, (((0,), (0,)), ((), ())), preferred_element_type=F32)


def _sigmoid(v):
    return 0.5 * jnp.tanh(0.5 * v) + 0.5


def _row_tile(n, pref):
    t = min(n, pref)
    while n % t:
        t //= 2
    return t


def _col_tile(m, rows, budget=6 * 2**20):
    best = LANES
    for t in range(LANES, m + 1, LANES):
        if m % t == 0 and rows * t * 4 <= budget:
            best = t
    return best


class Comm:
    def __init__(self, ins, out_shapes, plan, count, aliases=None):
        self.ins, self.out_shapes, self.plan, self.count, self.aliases = ins, out_shapes, plan, count, aliases or {}

    def __add__(self, other):
        ni, no = len(self.ins), len(self.out_shapes)

        def plan(ins, outs, pos):
            return self.plan(ins[:ni], outs[:no], pos) + other.plan(ins[ni:], outs[no:], pos)

        aliases = dict(self.aliases)
        aliases.update({ni + i: no + o for i, o in other.aliases.items()})
        return Comm(list(self.ins) + list(other.ins), list(self.out_shapes) + list(other.out_shapes), plan,
                    self.count + other.count, aliases)

    def split(self, outs, other):
        return outs[:len(self.out_shapes)], outs[len(self.out_shapes):]


def _pcall(body, name, grid, in_specs, out_specs, out_shape, scratch_shapes, args, sem, comm=None):
    single = not isinstance(out_shape, (list, tuple))
    if single:
        out_specs, out_shape = [out_specs], [out_shape]
    if comm is None:
        res = pl.pallas_call(body, name=name, grid=grid, in_specs=in_specs, out_specs=list(out_specs),
                             out_shape=list(out_shape), scratch_shapes=list(scratch_shapes),
                             compiler_params=_params(*sem))(*args)
        return (res[0] if single else res), []
    na, nci, no, nco, ns = len(args), len(comm.ins), len(out_shape), len(comm.out_shapes), len(scratch_shapes)

    def wrapped(*refs):
        a_refs, ci_refs = refs[:na], refs[na:na + nci]
        o_refs, co_refs = refs[na + nci:na + nci + no], refs[na + nci + no:na + nci + no + nco]
        s_refs = refs[na + nci + no + nco:na + nci + no + nco + ns]
        send_sems, recv_sems = refs[-2], refs[-1]
        pos = _me()

        def copies():
            return [pltpu.make_async_remote_copy(src_ref=s, dst_ref=d, send_sem=send_sems.at[i],
                                                 recv_sem=recv_sems.at[i], device_id=_flip(pos, rel),
                                                 device_id_type=MESH)
                    for i, (rel, s, d) in enumerate(comm.plan(ci_refs, co_refs, pos))]

        first, last = None, None
        for d, size in enumerate(grid):
            f, l = pl.program_id(d) == 0, pl.program_id(d) == size - 1
            first = f if first is None else first & f
            last = l if last is None else last & l

        @pl.when(first)
        def _():
            for cp in copies():
                cp.start()

        body(*a_refs, *o_refs, *s_refs)

        @pl.when(last)
        def _():
            for cp in copies():
                cp.wait()

    res = pl.pallas_call(
        wrapped, name=name, grid=grid,
        in_specs=list(in_specs) + [ANY] * nci, out_specs=list(out_specs) + [ANY] * nco,
        out_shape=list(out_shape) + [jax.ShapeDtypeStruct(s, d) for s, d in comm.out_shapes],
        scratch_shapes=list(scratch_shapes) + [pltpu.SemaphoreType.DMA((comm.count,)),
                                               pltpu.SemaphoreType.DMA((comm.count,))],
        input_output_aliases={na + i: no + o for i, o in comm.aliases.items()},
        compiler_params=_params(*(["arbitrary"] * len(grid))),
    )(*args, *comm.ins)
    outs = res[:no]
    return (outs[0] if single else outs), list(res[no:])


def norm_inproj(x, gain, w, bias, splits, dtypes, name, comm=None, w_transposed=False):
    n = x.shape[0]
    m = w.shape[0] if w_transposed else w.shape[1]
    tm = _row_tile(n, 512)

    def body(x_ref, g_ref, w_ref, b_ref, hn_ref, *outs):
        xv = x_ref[...]
        r = lax.rsqrt(jnp.mean(xv * xv, axis=-1, keepdims=True) + EPS)
        hn = ((xv * r) * g_ref[...]).astype(BF16)
        hn_ref[...] = hn
        z = (_dot_nt if w_transposed else _dot)(hn, w_ref[...]) + b_ref[...]
        for o, (lo, hi) in zip(outs, splits):
            o[...] = z[:, lo:hi].astype(o.dtype)

    out_shape = [jax.ShapeDtypeStruct((n, D_MODEL), BF16)]
    out_specs = [pl.BlockSpec((tm, D_MODEL), lambda i: (i, 0))]
    for (lo, hi), dt in zip(splits, dtypes):
        out_shape.append(jax.ShapeDtypeStruct((n, hi - lo), dt))
        out_specs.append(pl.BlockSpec((tm, hi - lo), lambda i: (i, 0)))
    return _pcall(
        body, name, (n // tm,),
        [pl.BlockSpec((tm, D_MODEL), lambda i: (i, 0)),
         pl.BlockSpec((1, D_MODEL), lambda i: (0, 0)),
         pl.BlockSpec(w.shape, lambda i: (0, 0)),
         pl.BlockSpec((1, m), lambda i: (0, 0))],
        out_specs, out_shape, [], (x, gain, w, bias), ("parallel",), comm)


def out_proj(res, m1, m2, w, name, comm=None):
    n = res.shape[0]
    k1, k2 = m1.shape[1], m2.shape[1]
    assert k1 == k2
    tm = _row_tile(n, 512)

    def body(r_ref, a_ref, b_ref, w1_ref, w2_ref, o_ref):
        o_ref[...] = r_ref[...] + _dot(a_ref[...], w1_ref[...]) + _dot(b_ref[...], w2_ref[...])

    return _pcall(
        body, name, (n // tm,),
        [pl.BlockSpec((tm, D_MODEL), lambda i: (i, 0)),
         pl.BlockSpec((tm, k1), lambda i: (i, 0)),
         pl.BlockSpec((tm, k2), lambda i: (i, 0)),
         pl.BlockSpec((k1, D_MODEL), lambda i: (0, 0)),
         pl.BlockSpec((k2, D_MODEL), lambda i: (1, 0))],
        pl.BlockSpec((tm, D_MODEL), lambda i: (i, 0)),
        jax.ShapeDtypeStruct((n, D_MODEL), F32), [], (res, m1, m2, w, w), ("parallel",), comm)


def ffn_gate_up(h, gain, wg_t, wu_t, name, comm=None):
    n = h.shape[0]
    tm = _row_tile(n, 1024)
    th = D_FF // 2

    def body(h_ref, g_ref, wg_ref, wu_ref, hn_ref, go_ref, uo_ref):
        @pl.when(pl.program_id(1) == 0)
        def _():
            xv = h_ref[...]
            r = lax.rsqrt(jnp.mean(xv * xv, axis=-1, keepdims=True) + EPS)
            hn_ref[...] = ((xv * r) * g_ref[...]).astype(BF16)

        hn = hn_ref[...]
        go_ref[...] = _dot_nt(hn, wg_ref[...]).astype(BF16)
        uo_ref[...] = _dot_nt(hn, wu_ref[...]).astype(BF16)

    return _pcall(
        body, name, (n // tm, D_FF // th),
        [pl.BlockSpec((tm, D_MODEL), lambda i, j: (i, 0)),
         pl.BlockSpec((1, D_MODEL), lambda i, j: (0, 0)),
         pl.BlockSpec((th, D_MODEL), lambda i, j: (j, 0)),
         pl.BlockSpec((th, D_MODEL), lambda i, j: (j, 0))],
        [pl.BlockSpec((tm, D_MODEL), lambda i, j: (i, 0)),
         pl.BlockSpec((tm, th), lambda i, j: (i, j)),
         pl.BlockSpec((tm, th), lambda i, j: (i, j))],
        [jax.ShapeDtypeStruct((n, D_MODEL), BF16),
         jax.ShapeDtypeStruct((n, D_FF), BF16),
         jax.ShapeDtypeStruct((n, D_FF), BF16)],
        [], (h, gain, wg_t, wu_t), ("parallel", "arbitrary"), comm)


def ffn_down(h, g, u, wd, name, comm=None):
    n = h.shape[0]
    tm = _row_tile(n, 512)

    def body(h_ref, g_ref, u_ref, w_ref, o_ref, a_ref):
        for c0 in range(0, D_FF, FF_CHUNK):
            gv = g_ref[:, c0:c0 + FF_CHUNK]
            a_ref[:, c0:c0 + FF_CHUNK] = gv * _sigmoid(gv) * u_ref[:, c0:c0 + FF_CHUNK]
        o_ref[...] = h_ref[...] + _dot(a_ref[...], w_ref[...])

    return _pcall(
        body, name, (n // tm,),
        [pl.BlockSpec((tm, D_MODEL), lambda i: (i, 0)),
         pl.BlockSpec((tm, D_FF), lambda i: (i, 0)),
         pl.BlockSpec((tm, D_FF), lambda i: (i, 0)),
         pl.BlockSpec((D_FF, D_MODEL), lambda i: (0, 0))],
        pl.BlockSpec((tm, D_MODEL), lambda i: (i, 0)),
        jax.ShapeDtypeStruct((n, D_MODEL), F32),
        [pltpu.VMEM((tm, D_FF), BF16)], (h, g, u, wd), ("parallel",), comm)


def ffn_down_bwd(dh, g, u, wd, name, comm=None):
    n = dh.shape[0]
    tm = _row_tile(n, 512)

    def body(dh_ref, g_ref, u_ref, w_ref, dg_ref, du_ref, a_ref):
        dhb = dh_ref[...].astype(BF16)
        for c0 in range(0, D_FF, MXU_COLS):
            cols = slice(c0, c0 + MXU_COLS)
            da = _dot_nt(dhb, w_ref[cols, :]).astype(BF16)
            gv, uv = g_ref[:, cols], u_ref[:, cols]
            sg = _sigmoid(gv)
            act = gv * sg
            dg_ref[:, cols] = (da * uv) * (sg + act * (1.0 - sg))
            du_ref[:, cols] = da * act
            a_ref[:, cols] = act * uv

    spec_h = pl.BlockSpec((tm, D_FF), lambda i: (i, 0))
    return _pcall(
        body, name, (n // tm,),
        [pl.BlockSpec((tm, D_MODEL), lambda i: (i, 0)), spec_h, spec_h,
         pl.BlockSpec((D_FF, D_MODEL), lambda i: (0, 0))],
        [spec_h, spec_h, spec_h], [jax.ShapeDtypeStruct((n, D_FF), BF16)] * 3,
        [], (dh, g, u, wd), ("parallel",), comm)


def mm_tn(x, dy, name, xsum=False, comm=None):
    n, k = x.shape
    m = dy.shape[1]
    tn = _col_tile(m, k)
    tt = _row_tile(n, 1024)

    def body(x_ref, dy_ref, o_ref, *rest):
        j, t = pl.program_id(0), pl.program_id(1)
        xv = x_ref[...]
        part = _dot_tn(xv.astype(BF16), dy_ref[...].astype(BF16))

        @pl.when(t == 0)
        def _():
            o_ref[...] = part

        @pl.when(t > 0)
        def _():
            o_ref[...] += part

        if xsum:
            @pl.when(j == 0)
            def _():
                cs = jnp.broadcast_to(jnp.sum(xv.astype(F32), axis=0, keepdims=True), rest[0].shape)

                @pl.when(t == 0)
                def _():
                    rest[0][...] = cs

                @pl.when(t > 0)
                def _():
                    rest[0][...] += cs

    out_shape = [jax.ShapeDtypeStruct((k, m), F32)]
    out_specs = [pl.BlockSpec((k, tn), lambda j, t: (0, j))]
    if xsum:
        out_shape.append(jax.ShapeDtypeStruct((8, k), F32))
        out_specs.append(pl.BlockSpec((8, k), lambda j, t: (0, 0)))
    res, comm_outs = _pcall(
        body, name, (m // tn, n // tt),
        [pl.BlockSpec((tt, k), lambda j, t: (t, 0)),
         pl.BlockSpec((tt, tn), lambda j, t: (t, j))],
        out_specs, out_shape, [], (x, dy), ("arbitrary", "arbitrary"), comm)
    res = (res[0], res[1][0]) if xsum else res[0]
    return res if comm is None else (res, comm_outs)


def out_proj_bwd(dh, w, dtypes, name):
    n = dh.shape[0]
    k = w.shape[0]
    half = k // 2
    tm = _row_tile(n, 512)

    def body(dh_ref, w_ref, a_ref, b_ref):
        dm = _dot_nt(dh_ref[...].astype(BF16), w_ref[...])
        a_ref[...] = dm[:, :half].astype(a_ref.dtype)
        b_ref[...] = dm[:, half:].astype(b_ref.dtype)

    return pl.pallas_call(
        body, name=name, grid=(n // tm,),
        in_specs=[pl.BlockSpec((tm, D_MODEL), lambda i: (i, 0)),
                  pl.BlockSpec((k, D_MODEL), lambda i: (0, 0))],
        out_specs=[pl.BlockSpec((tm, half), lambda i: (i, 0))] * 2,
        out_shape=[jax.ShapeDtypeStruct((n, half), dtypes[0]), jax.ShapeDtypeStruct((n, half), dtypes[1])],
        compiler_params=_params("parallel"),
    )(dh, w)


def proj_rms_bwd(dys, ws, h_in, gain, dres, nk, name, tm_pref=512, comm=None, w_transposed=False):
    n = h_in.shape[0]
    npair = len(dys)
    tm = _row_tile(n, tm_pref)
    tks = [dy.shape[1] // nk for dy in dys]
    mm = _dot if w_transposed else _dot_nt

    def body(*refs):
        dy_refs = refs[:npair]
        w_refs = refs[npair:2 * npair]
        h_ref, g_ref, dr_ref, o_ref, dg_ref, acc_ref = refs[2 * npair:]
        i, k = pl.program_id(0), pl.program_id(1)
        part = mm(dy_refs[0][...], w_refs[0][...])
        for p in range(1, npair):
            part = part + mm(dy_refs[p][...], w_refs[p][...])

        @pl.when(k == 0)
        def _():
            acc_ref[...] = part

        @pl.when(k > 0)
        def _():
            acc_ref[...] += part

        @pl.when(k == nk - 1)
        def _():
            dhn = acc_ref[...]
            xv = h_ref[...]
            r = lax.rsqrt(jnp.mean(xv * xv, axis=-1, keepdims=True) + EPS)
            xh = xv * r
            uv = dhn * g_ref[...]
            o_ref[...] = dr_ref[...] + r * (uv - xh * jnp.mean(uv * xh, axis=-1, keepdims=True))
            dgp = jnp.broadcast_to(jnp.sum(dhn * xh, axis=0, keepdims=True), dg_ref.shape)

            @pl.when(i == 0)
            def _():
                dg_ref[...] = dgp

            @pl.when(i > 0)
            def _():
                dg_ref[...] += dgp

    row = pl.BlockSpec((tm, D_MODEL), lambda i, k: (i, 0))
    in_specs = [pl.BlockSpec((tm, tk), lambda i, k: (i, k)) for tk in tks]
    if w_transposed:
        in_specs += [pl.BlockSpec((tk, D_MODEL), lambda i, k: (k, 0)) for tk in tks]
    else:
        in_specs += [pl.BlockSpec((D_MODEL, tk), lambda i, k: (0, k)) for tk in tks]
    in_specs += [row, pl.BlockSpec((1, D_MODEL), lambda i, k: (0, 0)), row]
    (dh, dgain), comm_outs = _pcall(
        body, name, (n // tm, nk), in_specs,
        [row, pl.BlockSpec((8, D_MODEL), lambda i, k: (0, 0))],
        [jax.ShapeDtypeStruct((n, D_MODEL), F32), jax.ShapeDtypeStruct((8, D_MODEL), F32)],
        [pltpu.VMEM((tm, D_MODEL), F32)], (*dys, *ws, h_in, gain, dres), ("arbitrary", "arbitrary"), comm)
    return dh, dgain[0], comm_outs


def loss_head(h, gain, target, name):
    n = h.shape[0]
    tm = _row_tile(n, 512)

    def body(h_ref, g_ref, t_ref, dh_ref, dg_ref, l_ref):
        i = pl.program_id(0)
        xv = h_ref[...]
        r = lax.rsqrt(jnp.mean(xv * xv, axis=-1, keepdims=True) + EPS)
        xh = xv * r
        err = xh * g_ref[...] - t_ref[...]
        dy = err * (1.0 / D_MODEL)
        uv = dy * g_ref[...]
        dh_ref[...] = r * (uv - xh * jnp.mean(uv * xh, axis=-1, keepdims=True))
        dgp = jnp.broadcast_to(jnp.sum(dy * xh, axis=0, keepdims=True), dg_ref.shape)
        lp = jnp.sum(jnp.sum(err * err, axis=-1, keepdims=True), axis=0, keepdims=True) * (0.5 / D_MODEL)
        lp = jnp.broadcast_to(lp, l_ref.shape)

        @pl.when(i == 0)
        def _():
            dg_ref[...] = dgp
            l_ref[...] = lp

        @pl.when(i > 0)
        def _():
            dg_ref[...] += dgp
            l_ref[...] += lp

    row = pl.BlockSpec((tm, D_MODEL), lambda i: (i, 0))
    dh, dg, l = pl.pallas_call(
        body, name=name, grid=(n // tm,),
        in_specs=[row, pl.BlockSpec((1, D_MODEL), lambda i: (0, 0)), row],
        out_specs=[row, pl.BlockSpec((8, D_MODEL), lambda i: (0, 0)), pl.BlockSpec((8, LANES), lambda i: (0, 0))],
        out_shape=[jax.ShapeDtypeStruct((n, D_MODEL), F32), jax.ShapeDtypeStruct((8, D_MODEL), F32),
                   jax.ShapeDtypeStruct((8, LANES), F32)],
        compiler_params=_params("arbitrary"),
    )(h, gain, target)
    return dh, dg[0], l[0, 0]


GROUP = N_Q_HEADS // N_KV_HEADS
GQ = GROUP * ATTN_BLOCK


def _attn_mask_t(n):
    r = lax.broadcasted_iota(jnp.int32, (2 * ATTN_BLOCK, GQ), 0)
    qi = lax.broadcasted_iota(jnp.int32, (2 * ATTN_BLOCK, GQ), 1) & (ATTN_BLOCK - 1)
    band = (r > qi) & (r <= qi + ATTN_BLOCK)
    return band & ((r >= ATTN_BLOCK) | (n > 0))


def _stack_heads(blk, kh):
    return jnp.concatenate([blk[:, (kh * GROUP + g) * HEAD_DIM:(kh * GROUP + g + 1) * HEAD_DIM]
                            for g in range(GROUP)], axis=0)


def _attn_probs_t(kk, qs, mask, sink):
    s = _dot_nt(kk, qs) * (HEAD_DIM ** -0.5)
    s = jnp.where(mask, s, -1e30)
    m = jnp.maximum(jnp.max(s, axis=0, keepdims=True), sink)
    p = jnp.exp(s - m)
    esink = jnp.exp(sink - m)
    inv = 1.0 / (jnp.sum(p, axis=0, keepdims=True) + esink)
    return p * inv, esink * inv


def attn_fwd(q, kv, sinks_t, nseq, seq, name, comm=None):
    nb = seq // ATTN_BLOCK

    def body(q_ref, kv_ref, s_ref, o_ref, kvp):
        kvp[0:ATTN_BLOCK, :] = jnp.zeros((ATTN_BLOCK, 2 * KV_WIDTH), BF16)
        kvp[ATTN_BLOCK:, :] = kv_ref[...]

        def blk(n, carry):
            st = pl.multiple_of(n * ATTN_BLOCK, ATTN_BLOCK)
            qb = q_ref[pl.ds(st, ATTN_BLOCK), :]
            kw = kvp[pl.ds(st, 2 * ATTN_BLOCK), :]
            mask = _attn_mask_t(n)
            for kh in range(N_KV_HEADS):
                kk = kw[:, kh * HEAD_DIM:(kh + 1) * HEAD_DIM]
                vv = kw[:, KV_WIDTH + kh * HEAD_DIM:KV_WIDTH + (kh + 1) * HEAD_DIM]
                probs, _ = _attn_probs_t(kk, _stack_heads(qb, kh), mask, s_ref[kh:kh + 1, :])
                ot = _dot_tn(vv, probs.astype(BF16))
                for pair in range(GROUP // 2):
                    two = jnp.concatenate([ot[:, (2 * pair) * ATTN_BLOCK:(2 * pair + 1) * ATTN_BLOCK],
                                           ot[:, (2 * pair + 1) * ATTN_BLOCK:(2 * pair + 2) * ATTN_BLOCK]], axis=0)
                    col = (kh * GROUP + 2 * pair) * HEAD_DIM
                    o_ref[pl.ds(st, ATTN_BLOCK), col:col + 2 * HEAD_DIM] = two.T.astype(o_ref.dtype)
            return carry

        lax.fori_loop(0, nb, blk, 0)

    return _pcall(
        body, name, (nseq,),
        [pl.BlockSpec((seq, ATTN_WIDTH), lambda b: (b, 0)),
         pl.BlockSpec((seq, 2 * KV_WIDTH), lambda b: (b, 0)),
         pl.BlockSpec((8, GQ), lambda b: (0, 0))],
        pl.BlockSpec((seq, ATTN_WIDTH), lambda b: (b, 0)),
        jax.ShapeDtypeStruct((nseq * seq, ATTN_WIDTH), BF16),
        [pltpu.VMEM((ATTN_BLOCK + seq, 2 * KV_WIDTH), BF16)], (q, kv, sinks_t), ("parallel",), comm)


def attn_bwd(q, kv, sinks_t, do, nseq, seq, name, comm=None):
    nb = seq // ATTN_BLOCK

    def body(q_ref, kv_ref, s_ref, do_ref, dq_ref, dkv_ref, ds_ref, kvp, dkvp, dsacc):
        @pl.when(pl.program_id(0) == 0)
        def _():
            dsacc[...] = jnp.zeros(dsacc.shape, F32)

        kvp[0:ATTN_BLOCK, :] = jnp.zeros((ATTN_BLOCK, 2 * KV_WIDTH), BF16)
        kvp[ATTN_BLOCK:, :] = kv_ref[...]
        dkvp[...] = jnp.zeros(dkvp.shape, F32)

        def blk(n, carry):
            st = pl.multiple_of(n * ATTN_BLOCK, ATTN_BLOCK)
            qb = q_ref[pl.ds(st, ATTN_BLOCK), :]
            dob = do_ref[pl.ds(st, ATTN_BLOCK), :]
            kw = kvp[pl.ds(st, 2 * ATTN_BLOCK), :]
            mask = _attn_mask_t(n)
            for kh in range(N_KV_HEADS):
                kk = kw[:, kh * HEAD_DIM:(kh + 1) * HEAD_DIM]
                vv = kw[:, KV_WIDTH + kh * HEAD_DIM:KV_WIDTH + (kh + 1) * HEAD_DIM]
                qs = _stack_heads(qb, kh)
                dos = _stack_heads(dob, kh)
                probs, psink = _attn_probs_t(kk, qs, mask, s_ref[kh:kh + 1, :])
                dp = _dot_nt(vv, dos)
                dv = _dot(probs.astype(BF16), dos)
                rowdot = jnp.sum(probs * dp, axis=0, keepdims=True)
                dsc = (probs * (dp - rowdot) * (HEAD_DIM ** -0.5)).astype(BF16)
                dsacc[kh:kh + 1, :] += -psink * rowdot
                dk = _dot(dsc, qs)
                dqs = _dot_tn(dsc, kk)
                for g in range(GROUP):
                    col = (kh * GROUP + g) * HEAD_DIM
                    dq_ref[pl.ds(st, ATTN_BLOCK), col:col + HEAD_DIM] = (
                        dqs[g * ATTN_BLOCK:(g + 1) * ATTN_BLOCK].astype(dq_ref.dtype))
                dkvp[pl.ds(st, 2 * ATTN_BLOCK), kh * HEAD_DIM:(kh + 1) * HEAD_DIM] += dk
                dkvp[pl.ds(st, 2 * ATTN_BLOCK), KV_WIDTH + kh * HEAD_DIM:KV_WIDTH + (kh + 1) * HEAD_DIM] += dv
            return carry

        lax.fori_loop(0, nb, blk, 0)
        dkv_ref[...] = dkvp[ATTN_BLOCK:, :].astype(dkv_ref.dtype)

        @pl.when(pl.program_id(0) == nseq - 1)
        def _():
            for kh in range(N_KV_HEADS):
                for g in range(GROUP):
                    tot = jnp.sum(dsacc[kh:kh + 1, g * ATTN_BLOCK:(g + 1) * ATTN_BLOCK], axis=1, keepdims=True)
                    ds_ref[kh * GROUP + g:kh * GROUP + g + 1, :] = jnp.broadcast_to(tot, (1, LANES))

    seq_q = pl.BlockSpec((seq, ATTN_WIDTH), lambda b: (b, 0))
    seq_kv = pl.BlockSpec((seq, 2 * KV_WIDTH), lambda b: (b, 0))
    return _pcall(
        body, name, (nseq,),
        [seq_q, seq_kv, pl.BlockSpec((8, GQ), lambda b: (0, 0)), seq_q],
        [seq_q, seq_kv, pl.BlockSpec((N_Q_HEADS, LANES), lambda b: (0, 0))],
        [jax.ShapeDtypeStruct((nseq * seq, ATTN_WIDTH), BF16),
         jax.ShapeDtypeStruct((nseq * seq, 2 * KV_WIDTH), BF16),
         jax.ShapeDtypeStruct((N_Q_HEADS, LANES), F32)],
        [pltpu.VMEM((ATTN_BLOCK + seq, 2 * KV_WIDTH), BF16),
         pltpu.VMEM((ATTN_BLOCK + seq, 2 * KV_WIDTH), F32),
         pltpu.VMEM((8, GQ), F32)], (q, kv, sinks_t, do), ("arbitrary",), comm)


CONV_T = 128


def _conv_taps(win, w_ref, lanes, init):
    acc = init
    for j in range(CONV_KERNEL):
        sh = win if j == CONV_KERNEL - 1 else pltpu.roll(win, CONV_KERNEL - 1 - j, 0)
        acc = acc + w_ref[j:j + 1, lanes] * sh[CONV_HALO:CONV_HALO + CONV_T]
    return acc


def _conv_block(h0p, w_ref, vec_ref, st):
    cols = []
    for cs in range(CONV_WIDTH // LANES):
        lanes = slice(cs * LANES, (cs + 1) * LANES)
        win = h0p[pl.ds(st, CONV_T + CONV_HALO), lanes]
        init = jnp.broadcast_to(vec_ref[0:1, lanes], (CONV_T, LANES))
        cols.append(_conv_taps(win, w_ref, lanes, init))
    return jnp.concatenate(cols, axis=-1)


def _glu_store(c_ref, h0p, st):
    cb = c_ref[pl.ds(st, CONV_T), :]
    h0p[pl.ds(pl.multiple_of(st + CONV_HALO, CONV_HALO), CONV_T), :] = cb[:, :CONV_WIDTH] * _sigmoid(cb[:, CONV_WIDTH:])


def conv_fwd(c, w, vec, nseq, seq, name, comm=None):
    nb = seq // CONV_T

    def body(c_ref, w_ref, vec_ref, o_ref, h0p):
        h0p[0:CONV_HALO, :] = jnp.zeros((CONV_HALO, CONV_WIDTH), F32)

        def blk(n, carry):
            st = pl.multiple_of(n * CONV_T, CONV_T)
            _glu_store(c_ref, h0p, st)
            h1 = _conv_block(h0p, w_ref, vec_ref, st)
            mu = jnp.mean(h1, axis=-1, keepdims=True)
            xc = h1 - mu
            rstd = lax.rsqrt(jnp.mean(xc * xc, axis=-1, keepdims=True) + EPS)
            y = xc * rstd * vec_ref[1:2, :] + vec_ref[2:3, :]
            o_ref[pl.ds(st, CONV_T), :] = (y * _sigmoid(y)).astype(o_ref.dtype)
            return carry

        lax.fori_loop(0, nb, blk, 0)

    return _pcall(
        body, name, (nseq,),
        [pl.BlockSpec((seq, 2 * CONV_WIDTH), lambda b: (b, 0)),
         pl.BlockSpec((32, CONV_WIDTH), lambda b: (0, 0)),
         pl.BlockSpec((8, CONV_WIDTH), lambda b: (0, 0))],
        pl.BlockSpec((seq, CONV_WIDTH), lambda b: (b, 0)),
        jax.ShapeDtypeStruct((nseq * seq, CONV_WIDTH), BF16),
        [pltpu.VMEM((CONV_HALO + seq, CONV_WIDTH), F32)], (c, w, vec), ("parallel",), comm)


def conv_bwd(c, w, vec, dout, nseq, seq, name, comm=None):
    nb = seq // CONV_T

    def body(c_ref, w_ref, vec_ref, do_ref, dc_ref, dw_ref, dvec_ref, h0p, dh1p):
        @pl.when(pl.program_id(0) == 0)
        def _():
            dw_ref[...] = jnp.zeros(dw_ref.shape, F32)
            dvec_ref[...] = jnp.zeros(dvec_ref.shape, F32)

        h0p[0:CONV_HALO, :] = jnp.zeros((CONV_HALO, CONV_WIDTH), F32)
        dh1p[seq:seq + CONV_HALO, :] = jnp.zeros((CONV_HALO, CONV_WIDTH), F32)

        def pass_a(n, carry):
            st = pl.multiple_of(n * CONV_T, CONV_T)
            _glu_store(c_ref, h0p, st)
            h1 = _conv_block(h0p, w_ref, vec_ref, st)
            mu = jnp.mean(h1, axis=-1, keepdims=True)
            xc = h1 - mu
            rstd = lax.rsqrt(jnp.mean(xc * xc, axis=-1, keepdims=True) + EPS)
            xh = xc * rstd
            y = xh * vec_ref[1:2, :] + vec_ref[2:3, :]
            sg = _sigmoid(y)
            dy = do_ref[pl.ds(st, CONV_T), :] * (sg * (1.0 + y * (1.0 - sg)))
            dvec_ref[1:2, :] += jnp.sum(dy * xh, axis=0, keepdims=True)
            dvec_ref[2:3, :] += jnp.sum(dy, axis=0, keepdims=True)
            dxh = dy * vec_ref[1:2, :]
            dh1 = rstd * (dxh - jnp.mean(dxh, axis=-1, keepdims=True)
                          - xh * jnp.mean(dxh * xh, axis=-1, keepdims=True))
            dvec_ref[0:1, :] += jnp.sum(dh1, axis=0, keepdims=True)
            dh1p[pl.ds(st, CONV_T), :] = dh1
            return carry

        lax.fori_loop(0, nb, pass_a, 0)

        def pass_b(n, carry):
            st = pl.multiple_of(n * CONV_T, CONV_T)
            cols = []
            for cs in range(CONV_WIDTH // LANES):
                lanes = slice(cs * LANES, (cs + 1) * LANES)
                wind = dh1p[pl.ds(st, CONV_T + CONV_HALO), lanes]
                winh = h0p[pl.ds(st, CONV_T + CONV_HALO), lanes]
                d1 = wind[0:CONV_T]
                acc = jnp.zeros((CONV_T, LANES), F32)
                for j in range(CONV_KERNEL):
                    acc = acc + w_ref[j:j + 1, lanes] * pltpu.roll(wind, 2 + j, 0)[CONV_HALO:CONV_HALO + CONV_T]
                    hs = winh if j == CONV_KERNEL - 1 else pltpu.roll(winh, CONV_KERNEL - 1 - j, 0)
                    dw_ref[j:j + 1, lanes] += jnp.sum(d1 * hs[CONV_HALO:CONV_HALO + CONV_T], axis=0, keepdims=True)
                cols.append(acc)
            dh0 = jnp.concatenate(cols, axis=-1)
            cb = c_ref[pl.ds(st, CONV_T), :]
            av, gt = cb[:, :CONV_WIDTH], cb[:, CONV_WIDTH:]
            sg = _sigmoid(gt)
            dc_ref[pl.ds(st, CONV_T), :] = jnp.concatenate(
                [dh0 * sg, dh0 * av * sg * (1.0 - sg)], axis=-1).astype(dc_ref.dtype)
            return carry

        lax.fori_loop(0, nb, pass_b, 0)

    return _pcall(
        body, name, (nseq,),
        [pl.BlockSpec((seq, 2 * CONV_WIDTH), lambda b: (b, 0)),
         pl.BlockSpec((32, CONV_WIDTH), lambda b: (0, 0)),
         pl.BlockSpec((8, CONV_WIDTH), lambda b: (0, 0)),
         pl.BlockSpec((seq, CONV_WIDTH), lambda b: (b, 0))],
        [pl.BlockSpec((seq, 2 * CONV_WIDTH), lambda b: (b, 0)),
         pl.BlockSpec((32, CONV_WIDTH), lambda b: (0, 0)),
         pl.BlockSpec((8, CONV_WIDTH), lambda b: (0, 0))],
        [jax.ShapeDtypeStruct((nseq * seq, 2 * CONV_WIDTH), BF16),
         jax.ShapeDtypeStruct((32, CONV_WIDTH), F32),
         jax.ShapeDtypeStruct((8, CONV_WIDTH), F32)],
        [pltpu.VMEM((CONV_HALO + seq, CONV_WIDTH), F32),
         pltpu.VMEM((seq + CONV_HALO, CONV_WIDTH), F32)], (c, w, vec, dout), ("arbitrary",), comm)


POOL_T = 128


def _pooled_block(zpp, st, grp):
    lanes = slice(grp * LANES, (grp + 1) * LANES)
    win = zpp[pl.ds(st, POOL_T + POOL_HALO), lanes]
    acc = win
    for lvl in range(grp + 1):
        acc = acc + pltpu.roll(acc, 1 << lvl, 0)
    t = st + lax.broadcasted_iota(jnp.int32, (POOL_T, 1), 0)
    inv = 1.0 / jnp.minimum(t + 1, POOL_WINDOWS[grp]).astype(F32)
    return acc[POOL_HALO:] * inv - win[POOL_HALO:], inv


def pool_fwd(zp, wp, scale, nseq, seq, name):
    nb = seq // POOL_T

    def body(z_ref, wp_ref, sc_ref, o_ref, zpp):
        zpp[0:POOL_HALO, :] = jnp.zeros((POOL_HALO, POOL_WIDTH), F32)
        zpp[POOL_HALO:, :] = z_ref[...]

        def blk(n, carry):
            st = pl.multiple_of(n * POOL_T, POOL_T)
            for grp in range(len(POOL_WINDOWS)):
                lanes = slice(grp * LANES, (grp + 1) * LANES)
                pooled, _ = _pooled_block(zpp, st, grp)
                o_ref[pl.ds(st, POOL_T), lanes] = (
                    _dot(pooled.astype(BF16), wp_ref[grp]) * sc_ref[0:1, lanes]).astype(o_ref.dtype)
            return carry

        lax.fori_loop(0, nb, blk, 0)

    return pl.pallas_call(
        body, name=name, grid=(nseq,),
        in_specs=[pl.BlockSpec((seq, POOL_WIDTH), lambda b: (b, 0)),
                  pl.BlockSpec((4, LANES, LANES), lambda b: (0, 0, 0)),
                  pl.BlockSpec((1, POOL_WIDTH), lambda b: (0, 0))],
        out_specs=pl.BlockSpec((seq, POOL_WIDTH), lambda b: (b, 0)),
        out_shape=jax.ShapeDtypeStruct((nseq * seq, POOL_WIDTH), BF16),
        scratch_shapes=[pltpu.VMEM((POOL_HALO + seq, POOL_WIDTH), F32)],
        compiler_params=_params("parallel"),
    )(zp, wp, scale)


def pool_bwd(zp, wp, scale, dout, nseq, seq, name):
    nb = seq // POOL_T

    def body(z_ref, wp_ref, sc_ref, do_ref, dz_ref, dwp_ref, dsc_ref, zpp, dpcp, negd):
        @pl.when(pl.program_id(0) == 0)
        def _():
            dwp_ref[...] = jnp.zeros(dwp_ref.shape, F32)
            dsc_ref[...] = jnp.zeros(dsc_ref.shape, F32)

        zpp[0:POOL_HALO, :] = jnp.zeros((POOL_HALO, POOL_WIDTH), F32)
        zpp[POOL_HALO:, :] = z_ref[...]
        dpcp[seq:seq + POOL_HALO, :] = jnp.zeros((POOL_HALO, POOL_WIDTH), F32)

        def pass_a(n, carry):
            st = pl.multiple_of(n * POOL_T, POOL_T)
            for grp in range(len(POOL_WINDOWS)):
                lanes = slice(grp * LANES, (grp + 1) * LANES)
                pooled, inv = _pooled_block(zpp, st, grp)
                pb = pooled.astype(BF16)
                dob = do_ref[pl.ds(st, POOL_T), lanes]
                dsc_ref[0:1, lanes] += jnp.sum(dob * _dot(pb, wp_ref[grp]), axis=0, keepdims=True)
                dpm = (dob * sc_ref[0:1, lanes]).astype(BF16)
                dwp_ref[grp] += _dot_tn(pb, dpm)
                dpooled = _dot_nt(dpm, wp_ref[grp])
                negd[pl.ds(st, POOL_T), lanes] = -dpooled
                dpcp[pl.ds(st, POOL_T), lanes] = dpooled * inv
            return carry

        lax.fori_loop(0, nb, pass_a, 0)

        def pass_b(n, carry):
            st = pl.multiple_of(n * POOL_T, POOL_T)
            rows = POOL_T + POOL_HALO
            for grp in range(len(POOL_WINDOWS)):
                lanes = slice(grp * LANES, (grp + 1) * LANES)
                acc = dpcp[pl.ds(st, rows), lanes]
                for lvl in range(grp + 1):
                    acc = acc + pltpu.roll(acc, rows - (1 << lvl), 0)
                dz_ref[pl.ds(st, POOL_T), lanes] = (acc[0:POOL_T] + negd[pl.ds(st, POOL_T), lanes]).astype(dz_ref.dtype)
            return carry

        lax.fori_loop(0, nb, pass_b, 0)

    seq_spec = pl.BlockSpec((seq, POOL_WIDTH), lambda b: (b, 0))
    return pl.pallas_call(
        body, name=name, grid=(nseq,),
        in_specs=[seq_spec, pl.BlockSpec((4, LANES, LANES), lambda b: (0, 0, 0)),
                  pl.BlockSpec((1, POOL_WIDTH), lambda b: (0, 0)), seq_spec],
        out_specs=[seq_spec, pl.BlockSpec((4, LANES, LANES), lambda b: (0, 0, 0)),
                   pl.BlockSpec((8, POOL_WIDTH), lambda b: (0, 0))],
        out_shape=[jax.ShapeDtypeStruct((nseq * seq, POOL_WIDTH), BF16),
                   jax.ShapeDtypeStruct((4, LANES, LANES), F32),
                   jax.ShapeDtypeStruct((8, POOL_WIDTH), F32)],
        scratch_shapes=[pltpu.VMEM((POOL_HALO + seq, POOL_WIDTH), F32),
                        pltpu.VMEM((seq + POOL_HALO, POOL_WIDTH), F32),
                        pltpu.VMEM((seq, POOL_WIDTH), F32)],
        compiler_params=_params("arbitrary"),
    )(zp, wp, scale, dout)


GELU_C0 = 0.7978845608028654
GELU_C1 = 0.044715


def _gelu(xv):
    return xv * (0.5 * (1.0 + jnp.tanh(GELU_C0 * (xv + GELU_C1 * (xv * xv * xv)))))


def _gelu_grad(xv):
    t = jnp.tanh(GELU_C0 * (xv + GELU_C1 * (xv * xv * xv)))
    return 0.5 * (1.0 + t) + 0.5 * xv * (1.0 - t * t) * (GELU_C0 * (1.0 + 3.0 * GELU_C1 * xv * xv))


def _tril():
    ti = lax.broadcasted_iota(jnp.int32, (SGU_CHUNK, SGU_CHUNK), 0)
    si = lax.broadcasted_iota(jnp.int32, (SGU_CHUNK, SGU_CHUNK), 1)
    return si <= ti


def sgu_fwd(zs, ws, bst, ln, nseq, seq, name):
    nc = seq // SGU_CHUNK

    def body(z_ref, ws_ref, bs_ref, ln_ref, o_ref):
        tril = _tril()

        def blk(n, carry):
            st = pl.multiple_of(n * SGU_CHUNK, SGU_CHUNK)
            ge = _gelu(z_ref[pl.ds(st, SGU_CHUNK), :])
            uu, vv = ge[:, :SGU_WIDTH], ge[:, SGU_WIDTH:]
            mu = jnp.mean(vv, axis=-1, keepdims=True)
            xc = vv - mu
            rstd = lax.rsqrt(jnp.mean(xc * xc, axis=-1, keepdims=True) + EPS)
            vn = (xc * rstd * ln_ref[0:1, :] + ln_ref[1:2, :]).astype(BF16)
            for g in range(4):
                lanes = slice(g * LANES, (g + 1) * LANES)
                wm = jnp.where(tril, ws_ref[g], 0.0).astype(BF16)
                mixed = _dot(wm, vn[:, lanes]) + bs_ref[:, g:g + 1]
                o_ref[pl.ds(st, SGU_CHUNK), lanes] = (uu[:, lanes] * mixed).astype(o_ref.dtype)
            return carry

        lax.fori_loop(0, nc, blk, 0)

    return pl.pallas_call(
        body, name=name, grid=(nseq,),
        in_specs=[pl.BlockSpec((seq, 2 * SGU_WIDTH), lambda b: (b, 0)),
                  pl.BlockSpec((4, LANES, LANES), lambda b: (0, 0, 0)),
                  pl.BlockSpec((SGU_CHUNK, 4), lambda b: (0, 0)),
                  pl.BlockSpec((8, SGU_WIDTH), lambda b: (0, 0))],
        out_specs=pl.BlockSpec((seq, SGU_WIDTH), lambda b: (b, 0)),
        out_shape=jax.ShapeDtypeStruct((nseq * seq, SGU_WIDTH), BF16),
        compiler_params=_params("parallel"),
    )(zs, ws, bst, ln)


def sgu_bwd(zs, ws, bst, ln, dout, nseq, seq, name):
    nc = seq // SGU_CHUNK

    def body(z_ref, ws_ref, bs_ref, ln_ref, do_ref, dz_ref, dws_ref, dbs_ref, dln_ref):
        @pl.when(pl.program_id(0) == 0)
        def _():
            dws_ref[...] = jnp.zeros(dws_ref.shape, F32)
            dbs_ref[...] = jnp.zeros(dbs_ref.shape, F32)
            dln_ref[...] = jnp.zeros(dln_ref.shape, F32)

        tril = _tril()

        def blk(n, carry):
            st = pl.multiple_of(n * SGU_CHUNK, SGU_CHUNK)
            zv = z_ref[pl.ds(st, SGU_CHUNK), :]
            ge = _gelu(zv)
            uu, vv = ge[:, :SGU_WIDTH], ge[:, SGU_WIDTH:]
            mu = jnp.mean(vv, axis=-1, keepdims=True)
            xc = vv - mu
            rstd = lax.rsqrt(jnp.mean(xc * xc, axis=-1, keepdims=True) + EPS)
            xh = xc * rstd
            vn = (xh * ln_ref[0:1, :] + ln_ref[1:2, :]).astype(BF16)
            dob = do_ref[pl.ds(st, SGU_CHUNK), :]
            du_cols, dvn_cols = [], []
            for g in range(4):
                lanes = slice(g * LANES, (g + 1) * LANES)
                wm = jnp.where(tril, ws_ref[g], 0.0).astype(BF16)
                mixed = _dot(wm, vn[:, lanes]) + bs_ref[:, g:g + 1]
                du_cols.append(dob[:, lanes] * mixed)
                dmix = dob[:, lanes] * uu[:, lanes]
                dbs_ref[g] += jnp.broadcast_to(jnp.sum(dmix, axis=-1, keepdims=True), (SGU_CHUNK, LANES))
                dmb = dmix.astype(BF16)
                dws_ref[g] += jnp.where(tril, _dot_nt(dmb, vn[:, lanes]), 0.0)
                dvn_cols.append(_dot_tn(wm, dmb))
            dvn = jnp.concatenate(dvn_cols, axis=-1)
            dln_ref[0:1, :] += jnp.sum(dvn * xh, axis=0, keepdims=True)
            dln_ref[1:2, :] += jnp.sum(dvn, axis=0, keepdims=True)
            dxh = dvn * ln_ref[0:1, :]
            dv = rstd * (dxh - jnp.mean(dxh, axis=-1, keepdims=True)
                         - xh * jnp.mean(dxh * xh, axis=-1, keepdims=True))
            dge = jnp.concatenate(du_cols + [dv], axis=-1)
            dz_ref[pl.ds(st, SGU_CHUNK), :] = (dge * _gelu_grad(zv)).astype(dz_ref.dtype)
            return carry

        lax.fori_loop(0, nc, blk, 0)

    w_spec = pl.BlockSpec((4, LANES, LANES), lambda b: (0, 0, 0))
    ln_spec = pl.BlockSpec((8, SGU_WIDTH), lambda b: (0, 0))
    return pl.pallas_call(
        body, name=name, grid=(nseq,),
        in_specs=[pl.BlockSpec((seq, 2 * SGU_WIDTH), lambda b: (b, 0)), w_spec,
                  pl.BlockSpec((SGU_CHUNK, 4), lambda b: (0, 0)), ln_spec,
                  pl.BlockSpec((seq, SGU_WIDTH), lambda b: (b, 0))],
        out_specs=[pl.BlockSpec((seq, 2 * SGU_WIDTH), lambda b: (b, 0)), w_spec, w_spec, ln_spec],
        out_shape=[jax.ShapeDtypeStruct((nseq * seq, 2 * SGU_WIDTH), BF16),
                   jax.ShapeDtypeStruct((4, LANES, LANES), F32),
                   jax.ShapeDtypeStruct((4, LANES, LANES), F32),
                   jax.ShapeDtypeStruct((8, SGU_WIDTH), F32)],
        compiler_params=_params("arbitrary"),
    )(zs, ws, bst, ln, dout)


def _ew_rows(rows, cols, nbuf):
    t = _row_tile(rows, 1024)
    while t > 8 and t * cols * 4 * nbuf * 2 > 24 * 2**20:
        t //= 2
    return t


def adamw(w, g, m, v, name):
    layers, rows, cols = w.shape
    tr = _ew_rows(rows, cols, 7)

    def body(w_ref, g_ref, m_ref, v_ref, d_ref, mo_ref, vo_ref):
        gv = g_ref[...]
        mn = ADAM_B1 * m_ref[...] + (1.0 - ADAM_B1) * gv
        vn = ADAM_B2 * v_ref[...] + (1.0 - ADAM_B2) * (gv * gv)
        m_hat = mn / (1.0 - ADAM_B1 ** ADAM_STEP)
        v_hat = vn / (1.0 - ADAM_B2 ** ADAM_STEP)
        d_ref[...] = -ADAM_LR * (m_hat / (jnp.sqrt(v_hat) + ADAM_EPS) + ADAM_WD * w_ref[...])
        mo_ref[...] = mn
        vo_ref[...] = vn

    spec = pl.BlockSpec((1, tr, cols), lambda l, i: (l, i, 0))
    return pl.pallas_call(
        body, name=name, grid=(layers, rows // tr),
        in_specs=[spec] * 4, out_specs=[spec] * 3,
        out_shape=[jax.ShapeDtypeStruct(w.shape, F32)] * 3,
        compiler_params=_params("parallel", "parallel"),
    )(w, g, m, v)


def add_cast(a, b, name, dtype=BF16):
    nslab, rows, cols = a.shape
    tr = _ew_rows(rows, cols, 3)

    def body(a_ref, b_ref, o_ref):
        o_ref[...] = (a_ref[...] + b_ref[...]).astype(dtype)

    spec = pl.BlockSpec((1, tr, cols), lambda k, i: (k, i, 0))
    return pl.pallas_call(
        body, name=name, grid=(nslab, rows // tr),
        in_specs=[spec, spec], out_specs=spec,
        out_shape=jax.ShapeDtypeStruct(a.shape, dtype),
        compiler_params=_params("parallel", "parallel"),
    )(a, b)


def pair_sum(t, got, core, name):
    nslab, h, cols = got.shape
    tr = _ew_rows(h, cols, 3)
    nb = h // tr

    def body(c_ref, a_ref, b_ref, o_ref):
        o_ref[...] = (a_ref[...] + b_ref[...]).astype(BF16)

    spec = pl.BlockSpec((1, tr, cols), lambda k, i, c: (k, i, 0))
    return pl.pallas_call(
        body, name=name,
        grid_spec=pltpu.PrefetchScalarGridSpec(
            num_scalar_prefetch=1, grid=(nslab, nb),
            in_specs=[pl.BlockSpec((1, tr, cols), lambda k, i, c: (k, c[0] * nb + i, 0)), spec],
            out_specs=spec),
        out_shape=jax.ShapeDtypeStruct(got.shape, BF16),
        compiler_params=_params("parallel", "parallel"),
    )(core, t, got)


def sum_parts(parts, name, first=None):
    npart, rows, cols = parts.shape
    tr = _ew_rows(rows, cols, npart + 2)

    def body(*refs):
        p_ref, o_ref = refs[-2], refs[-1]
        acc = p_ref[0].astype(F32) if first is None else refs[0][...].astype(F32) + p_ref[0].astype(F32)
        for j in range(1, npart):
            acc = acc + p_ref[j].astype(F32)
        o_ref[...] = acc

    row = pl.BlockSpec((tr, cols), lambda i: (i, 0))
    ins = [parts] if first is None else [first, parts]
    return pl.pallas_call(
        body, name=name, grid=(rows // tr,),
        in_specs=([] if first is None else [row]) + [pl.BlockSpec((npart, tr, cols), lambda i: (0, i, 0))],
        out_specs=row,
        out_shape=jax.ShapeDtypeStruct((rows, cols), F32),
        compiler_params=_params("parallel"),
    )(*ins)


ANY = pl.BlockSpec(memory_space=pl.ANY)
MESH = pl.DeviceIdType.MESH


def _me():
    return lax.axis_index("x"), lax.axis_index("y"), lax.axis_index("c")


def _flip(pos, rel):
    return tuple(1 - p if f else p for p, f in zip(pos, rel))


SIBLING = (0, 0, 1)
OTHER_CHIPS = ((1, 0, 0), (0, 1, 0), (1, 1, 0))


def _chip_of(pos, rel=(0, 0, 0)):
    px, py, _ = _flip(pos, rel)
    return 2 * px + py


def allgather_blocks(shards, name):
    nt = len(shards)
    hs = [s.shape[0] // 2 for s in shards]

    def body(*refs):
        ins, outs = refs[:nt], refs[nt:2 * nt]
        send_sems, recv_sems, loc_sems = refs[2 * nt:]
        pos = _me()
        x, y, c = pos

        def block_id(rel):
            px, py, pc = _flip(pos, rel)
            return 4 * px + 2 * py + pc

        def copy(t, k, block_rel, to_rel, src=None):
            dst = outs[t].at[block_id(block_rel)]
            return pltpu.make_async_remote_copy(
                src_ref=dst if src is None else src, dst_ref=dst,
                send_sem=send_sems.at[t * 7 + k], recv_sem=recv_sems.at[t * 7 + k],
                device_id=_flip(pos, to_rel), device_id_type=MESH)

        own = [ins[t].at[pl.ds(c * hs[t], hs[t])] for t in range(nt)]
        mine = [pltpu.make_async_copy(own[t], outs[t].at[block_id((0, 0, 0))], loc_sems.at[t]) for t in range(nt)]
        for cp in mine:
            cp.start()
        first = []
        for t in range(nt):
            first.append(copy(t, 0, (0, 0, 0), SIBLING, src=own[t]))
            first += [copy(t, 1 + j, (0, 0, 0), rel, src=own[t]) for j, rel in enumerate(OTHER_CHIPS)]
        for cp in first:
            cp.start()
        passed = []
        for j, rel in enumerate(OTHER_CHIPS):
            for t in range(nt):
                copy(t, 1 + j, rel, (0, 0, 0)).wait_recv()
                fwd = copy(t, 4 + j, rel, SIBLING)
                fwd.start()
                passed.append(fwd)
        for t in range(nt):
            copy(t, 0, SIBLING, (0, 0, 0)).wait_recv()
            for j, rel in enumerate(OTHER_CHIPS):
                copy(t, 4 + j, (rel[0], rel[1], 1), (0, 0, 0)).wait_recv()
        for cp in first + passed:
            cp.wait_send()
        for cp in mine:
            cp.wait()

    return pl.pallas_call(
        body, name=name,
        in_specs=[ANY] * nt, out_specs=[ANY] * nt,
        out_shape=[jax.ShapeDtypeStruct((N_DEV, h, s.shape[1]), s.dtype) for h, s in zip(hs, shards)],
        scratch_shapes=[pltpu.SemaphoreType.DMA((7 * nt,)), pltpu.SemaphoreType.DMA((7 * nt,)),
                        pltpu.SemaphoreType.DMA((nt,))],
    )(*shards)


def _block_id(pos, rel=(0, 0, 0)):
    px, py, pc = _flip(pos, rel)
    return 4 * px + 2 * py + pc


def gather_first_hop(shards):
    hs = [s.shape[0] // 2 for s in shards]

    def plan(ins, outs, pos):
        me = _block_id(pos)
        remote = []
        for i, o, h in zip(ins, outs, hs):
            own = i.at[pl.ds(pos[2] * h, h)]
            remote += [(rel, own, o.at[me]) for rel in (SIBLING,) + OTHER_CHIPS]
        return remote

    return Comm(shards, [((N_DEV, h, s.shape[1]), s.dtype) for h, s in zip(hs, shards)], plan, 4 * len(shards))


def gather_second_hop(gathered):
    def plan(ins, outs, pos):
        remote = []
        for i, o in zip(ins, outs):
            for rel in OTHER_CHIPS:
                blk = _block_id(pos, rel)
                remote.append((SIBLING, i.at[blk], o.at[blk]))
        return remote

    return Comm(gathered, [(g.shape, g.dtype) for g in gathered], plan, 3 * len(gathered),
                aliases={i: i for i in range(len(gathered))})


def swap_comm(xs):
    def plan(ins, outs, pos):
        return [(SIBLING, i, o) for i, o in zip(ins, outs)]

    return Comm(list(xs), [(v.shape, v.dtype) for v in xs], plan, len(xs))


def give_half_comm(ts, plain=()):
    nt = len(ts)

    def plan(ins, outs, pos):
        remote = []
        for i, o in zip(ins[:nt], outs[:nt]):
            h = o.shape[1]
            remote.append((SIBLING, i.at[:, pl.ds((1 - pos[2]) * h, h)], o))
        return remote + [(SIBLING, i, o) for i, o in zip(ins[nt:], outs[nt:])]

    shapes = [((t.shape[0], t.shape[1] // 2, t.shape[2]), t.dtype) for t in ts] + [(v.shape, v.dtype) for v in plain]
    return Comm(list(ts) + list(plain), shapes, plan, nt + len(plain))


def chip_scatter_comm(xs, shared=None):
    nx = len(xs)

    def plan(ins, outs, pos):
        me = _chip_of(pos)
        remote = []
        for i, o in zip(ins[:nx], outs[:nx]):
            remote += [(rel, i.at[_chip_of(pos, rel)], o.at[j]) for j, rel in enumerate(OTHER_CHIPS)]
        if shared is not None:
            remote += [(rel, ins[nx], outs[nx].at[me]) for rel in OTHER_CHIPS]
        return remote

    shapes = [((3,) + v.shape[1:], v.dtype) for v in xs]
    if shared is not None:
        shapes.append(((N_CHIPS,) + shared.shape, shared.dtype))
    return Comm(list(xs) + ([] if shared is None else [shared]), shapes, plan, 3 * nx + (0 if shared is None else 3))


def run_comm(comm, name):
    return _pcall(lambda: None, name, (1,), [], [], [], [], (), ("arbitrary",), comm)[1]


PACK_ROWS = 256


def _pack(arrs):
    parts, layout = [], []
    row = 0
    for a in arrs:
        flat = a.reshape(-1).astype(F32)
        size = flat.shape[0]
        rows = -(-size // (8 * LANES)) * 8
        flat = jnp.pad(flat, (0, rows * LANES - size))
        parts.append(flat.reshape(rows, LANES))
        layout.append((row, rows, size, a.shape))
        row += rows
    if row % PACK_ROWS:
        parts.append(jnp.zeros((PACK_ROWS - row % PACK_ROWS, LANES), F32))
    return jnp.concatenate(parts, axis=0), layout


def _unpack(packed, layout):
    return [packed[r0:r0 + rows].reshape(-1)[:size].reshape(shape) for r0, rows, size, shape in layout]


SMALL_REPL = ['mix_norm', 'a_b_in', 'a_sinks', 'a_conv_b', 'a_cln_g', 'a_cln_b', 'c_w_pool', 'c_w_s', 'c_b_s',
              'ffn_norm', 'final_norm']
SMALL_SHARD = ['a_conv_w', 'c_pool_scale', 'c_sln_g', 'c_sln_b']
BIG = ['a_w_in', 'a_w_out', 'c_w_in', 'c_w_out', 'ffn_w_gate', 'ffn_w_up', 'ffn_w_down']
TRANSPOSED = ('a_w_in', 'ffn_w_gate', 'ffn_w_up')
BIG_COL_SHARDED = {'c_w_in'}


def _full_weight(name, g8):
    _, h, cols = g8.shape
    g4 = g8.reshape(N_CHIPS, 2 * h, cols)
    if name not in BIG_COL_SHARDED:
        return g4.reshape(-1, cols)
    return jnp.transpose(g4, (1, 0, 2)).reshape(2 * h, N_CHIPS * cols)


def _to_shard_major(name, f):
    if name not in BIG_COL_SHARDED:
        return f.reshape(N_CHIPS, f.shape[0] // N_CHIPS, f.shape[1])
    r, cfull = f.shape
    return jnp.transpose(f.reshape(r, N_CHIPS, cfull // N_CHIPS), (1, 0, 2))


def kernel(*args):
    a = dict(zip(IN_NAMES, args))
    bl, seq, _ = a['x'].shape
    n = bl * seq
    x = a['x'].reshape(n, D_MODEL)
    target = a['loss_target'].reshape(n, D_MODEL)
    xi, yi, ci = _me()
    chip = 2 * xi + yi

    shard = {'a_w_in': a['a_w_in'][0].T, 'a_w_out': a['a_w_out'][0], 'c_w_in': a['c_w_in'][0], 'c_w_out': a['c_w_out'][0]}
    for layer in range(2):
        shard['gate' + str(layer)] = a['ffn_w_gate'][layer].T
        shard['up' + str(layer)] = a['ffn_w_up'][layer].T
        shard['down' + str(layer)] = a['ffn_w_down'][layer]
    shard = {k: v.astype(BF16) for k, v in shard.items()}
    core = ci.astype(jnp.int32).reshape(1)
    block_id = 4 * xi + 2 * yi + ci

    def first_hop(*names):
        return gather_first_hop([shard[k] for k in names])

    def finish(name, g8):
        h = shard[name].shape[0] // 2
        own = lax.dynamic_slice_in_dim(shard[name], ci * h, h, axis=0)
        return _full_weight(name, lax.dynamic_update_slice_in_dim(g8, own[None], block_id, axis=0))

    a_w_in_t = _full_weight('a_w_in', allgather_blocks([shard['a_w_in']], "gather_a_w_in")[0])
    in0_width = a_w_in_t.shape[0]
    small_shard_pack, small_shard_layout = _pack([a[k] for k in SMALL_SHARD])
    hop_a = first_hop('a_w_out', 'gate0')
    hop_s = chip_scatter_comm([], shared=small_shard_pack)
    mix_norm, ffn_norm = a['mix_norm'], a['ffn_norm']
    (hn0, q, kv, cc), outs = norm_inproj(
        x, mix_norm[0:1], a_w_in_t, a['a_b_in'],
        [(0, ATTN_WIDTH), (ATTN_WIDTH, ATTN_WIDTH + 2 * KV_WIDTH), (ATTN_WIDTH + 2 * KV_WIDTH, in0_width)],
        [BF16, BF16, F32], "in_proj0", comm=hop_a + hop_s, w_transposed=True)
    got_a, (ss,) = hop_a.split(outs, hop_s)
    ss = lax.dynamic_update_slice_in_dim(ss, small_shard_pack[None], chip, axis=0)
    ss_full = []
    for r0, rows, size, shape in small_shard_layout:
        per_chip = ss[:, r0:r0 + rows].reshape(N_CHIPS, -1)[:, :size].reshape((N_CHIPS,) + shape)
        ss_full.append(jnp.concatenate([per_chip[k] for k in range(N_CHIPS)], axis=-1))
    a_conv_w, c_pool_scale, c_sln_g, c_sln_b = [v[0] for v in ss_full]

    conv_taps = jnp.pad(a_conv_w, ((0, 32 - CONV_KERNEL), (0, 0)))
    conv_vec = jnp.pad(jnp.stack([a['a_conv_b'][0], a['a_cln_g'][0], a['a_cln_b'][0]]), ((0, 5), (0, 0)))
    sinks_b = jnp.pad(jnp.repeat(a['a_sinks'][0].reshape(N_KV_HEADS, GROUP), ATTN_BLOCK, axis=1), ((0, 6), (0, 0)))
    w_pool_bf = a['c_w_pool'][0].astype(BF16)
    pool_scale = c_pool_scale.reshape(1, POOL_WIDTH)
    w_s = a['c_w_s'][0]
    b_s_t = a['c_b_s'][0].T
    sgu_ln = jnp.pad(jnp.stack([c_sln_g, c_sln_b]), ((0, 6), (0, 0)))
    final_norm = a['final_norm'].reshape(1, D_MODEL)

    hop_b, pass_a = first_hop('up0', 'down0'), gather_second_hop(got_a)
    attn, outs = attn_fwd(q, kv, sinks_b, bl, seq, "attn_fwd", comm=hop_b + pass_a)
    got_b, done = hop_b.split(outs, pass_a)
    a_w_out, wg0 = finish('a_w_out', done[0]), finish('gate0', done[1])

    hop_c, pass_b = first_hop('c_w_in', 'c_w_out', 'gate1'), gather_second_hop(got_b)
    conv, outs = conv_fwd(cc, conv_taps, conv_vec, bl, seq, "conv_fwd", comm=hop_c + pass_b)
    got_c, done = hop_c.split(outs, pass_b)
    wu0, wd0 = finish('up0', done[0]), finish('down0', done[1])

    h1, got_d = out_proj(x, attn, conv, a_w_out, "out_proj0", comm=first_hop('up1'))

    hop_e, pass_c = first_hop('down1'), gather_second_hop(got_c + got_d)
    (hnf0, g0, u0), outs = ffn_gate_up(h1, ffn_norm[0:1], wg0, wu0, "ffn_gate_up0", comm=hop_e + pass_c)
    got_e, done = hop_e.split(outs, pass_c)
    c_w_in, c_w_out = finish('c_w_in', done[0]), finish('c_w_out', done[1])
    wg1, wu1 = finish('gate1', done[2]), finish('up1', done[3])

    h2, done = ffn_down(h1, g0, u0, wd0, "ffn_down0", comm=gather_second_hop(got_e))
    wd1 = finish('down1', done[0])
    wg, wu, wd = [wg0, wg1], [wu0, wu1], [wd0, wd1]

    (hn1, zp, zs), _ = norm_inproj(
        h2, mix_norm[1:2], c_w_in, jnp.zeros((1, c_w_in.shape[1]), F32),
        [(0, POOL_WIDTH), (POOL_WIDTH, c_w_in.shape[1])], [F32, F32], "in_proj1")
    pool = pool_fwd(zp, w_pool_bf, pool_scale, bl, seq, "pool_fwd")
    sgu = sgu_fwd(zs, w_s, b_s_t, sgu_ln, bl, seq, "sgu_fwd")
    h3, _ = out_proj(h2, pool, sgu, c_w_out, "out_proj1")
    (hnf1, g1, u1), _ = ffn_gate_up(h3, ffn_norm[1:2], wg1, wu1, "ffn_gate_up1")
    h4, _ = ffn_down(h3, g1, u1, wd1, "ffn_down1")

    dh4, d_final_norm, loss_local = loss_head(h4, final_norm, target, "loss_head")

    grads = {}
    pieces = {}

    def slabs_of(names, fulls):
        return [_to_shard_major(k, fulls[k]) for k in names]

    def pair_sums_of(names, slabs, gots):
        return [pair_sum(t, gt, core, "pair_sum_" + k) for k, t, gt in zip(names, slabs, gots)]

    def chip_sums_of(names, sums, from_chips):
        own = [lax.dynamic_index_in_dim(p, chip, axis=0, keepdims=False) for p in sums]
        return [sum_parts(p, "chip_sum_" + k, first=o) for k, p, o in zip(names, from_chips, own)]

    (dg, du, act), _ = ffn_down_bwd(dh4, g1, u1, wd[1], "ffn_down_bwd1")
    full1 = {'down1': mm_tn(act, dh4, "dw_down1"), 'gate1': mm_tn(dg, hnf1, "dw_gate1"),
             'up1': mm_tn(du, hnf1, "dw_up1")}
    dh3, d_ffn_norm1, _ = proj_rms_bwd([dg, du], [wg[1], wu[1]], h3, ffn_norm[1:2], dh4, 1, "ffn_up_bwd1",
                                       tm_pref=256, w_transposed=True)
    d_pool, d_sgu = out_proj_bwd(dh3, c_w_out, [F32, F32], "out_proj_bwd1")
    full1['c_w_out'] = jnp.concatenate([mm_tn(pool, dh3, "dw_out1_pool"), mm_tn(sgu, dh3, "dw_out1_sgu")], axis=0)
    dzp, d_w_pool, d_pool_scale = pool_bwd(zp, w_pool_bf, pool_scale, d_pool, bl, seq, "pool_bwd")
    dzs, d_w_s, d_b_s_b, d_sgu_ln = sgu_bwd(zs, w_s, b_s_t, sgu_ln, d_sgu, bl, seq, "sgu_bwd")
    full1['c_w_in'] = jnp.concatenate([mm_tn(hn1, dzp, "dw_in1_pool"), mm_tn(hn1, dzs, "dw_in1_sgu")], axis=1)
    names1 = ['gate1', 'up1', 'down1', 'c_w_out', 'c_w_in']
    slabs1 = slabs_of(names1, full1)
    dh2, d_mix_norm1, got1 = proj_rms_bwd([dzp, dzs], [c_w_in[:, :POOL_WIDTH], c_w_in[:, POOL_WIDTH:]], h2,
                                          mix_norm[1:2], dh3, 1, "in_proj_bwd1", comm=give_half_comm(slabs1))
    sums1 = pair_sums_of(names1, slabs1, got1)

    (dg, du, act), from_chips1 = ffn_down_bwd(dh2, g0, u0, wd[0], "ffn_down_bwd0", comm=chip_scatter_comm(sums1))
    mine1 = chip_sums_of(names1, sums1, from_chips1)
    full0 = {'down0': mm_tn(act, dh2, "dw_down0"), 'gate0': mm_tn(dg, hnf0, "dw_gate0"),
             'up0': mm_tn(du, hnf0, "dw_up0")}
    names0 = ['gate0', 'up0', 'down0']
    slabs0 = slabs_of(names0, full0)
    join1, pair0 = swap_comm(mine1), give_half_comm(slabs0)
    dh1, d_ffn_norm0, outs = proj_rms_bwd([dg, du], [wg[0], wu[0]], h1, ffn_norm[0:1], dh2, 1, "ffn_up_bwd0",
                                          tm_pref=256, comm=join1 + pair0, w_transposed=True)
    theirs1, got0 = join1.split(outs, pair0)
    pieces.update({k: (m, t) for k, m, t in zip(names1, mine1, theirs1)})
    sums0 = pair_sums_of(names0, slabs0, got0)

    d_attn, d_conv = out_proj_bwd(dh1, a_w_out, [BF16, F32], "out_proj_bwd0")
    full_o = {'a_w_out': jnp.concatenate([mm_tn(attn, dh1, "dw_out0_attn"), mm_tn(conv, dh1, "dw_out0_conv")], axis=0)}
    slabs_o = slabs_of(['a_w_out'], full_o)
    chips0, pair_o = chip_scatter_comm(sums0), give_half_comm(slabs_o)
    (dq, dkv, d_sinks_b), outs = attn_bwd(q, kv, sinks_b, d_attn, bl, seq, "attn_bwd", comm=chips0 + pair_o)
    from_chips0, got_o = chips0.split(outs, pair_o)
    mine0 = chip_sums_of(names0, sums0, from_chips0)
    sums_o = pair_sums_of(['a_w_out'], slabs_o, got_o)
    join0, chips_o = swap_comm(mine0), chip_scatter_comm(sums_o)
    (dcc, d_conv_taps, d_conv_vec), outs = conv_bwd(cc, conv_taps, conv_vec, d_conv, bl, seq, "conv_bwd",
                                                    comm=join0 + chips_o)
    theirs0, from_chips_o = join0.split(outs, chips_o)
    pieces.update({k: (m, t) for k, m, t in zip(names0, mine0, theirs0)})
    mine_o = chip_sums_of(['a_w_out'], sums_o, from_chips_o)
    kq, kk = ATTN_WIDTH, ATTN_WIDTH + 2 * KV_WIDTH
    grad_x, d_mix_norm0, _ = proj_rms_bwd([dq, dkv, dcc], [a_w_in_t[:kq], a_w_in_t[kq:kk], a_w_in_t[kk:]], x,
                                          mix_norm[0:1], dh1, 1, "in_proj_bwd0", w_transposed=True)
    dw_q, db_q = mm_tn(dq, hn0, "dw_in0_q", xsum=True)
    dw_kv, db_kv = mm_tn(dkv, hn0, "dw_in0_kv", xsum=True)
    (dw_c, db_c), theirs_o = mm_tn(dcc, hn0, "dw_in0_c", xsum=True, comm=swap_comm(mine_o))
    pieces['a_w_out'] = (mine_o[0], theirs_o[0])
    d_a_b_in = jnp.concatenate([db_q, db_kv, db_c], axis=0)
    slabs_i = slabs_of(['a_w_in'], {'a_w_in': jnp.concatenate([dw_q, dw_kv, dw_c], axis=0)})

    small_full = {
        'mix_norm': jnp.stack([d_mix_norm0, d_mix_norm1]), 'a_b_in': d_a_b_in[None], 'a_sinks': d_sinks_b[:, 0][None],
        'a_conv_w': d_conv_taps[:CONV_KERNEL][None], 'a_conv_b': d_conv_vec[0][None], 'a_cln_g': d_conv_vec[1][None],
        'a_cln_b': d_conv_vec[2][None], 'c_w_pool': d_w_pool[None], 'c_pool_scale': d_pool_scale[0][None],
        'c_sln_g': d_sgu_ln[0][None], 'c_sln_b': d_sgu_ln[1][None], 'c_w_s': d_w_s[None],
        'c_b_s': d_b_s_b[:, :, 0][None], 'ffn_norm': jnp.stack([d_ffn_norm0, d_ffn_norm1]),
        'final_norm': d_final_norm, 'loss': loss_local.reshape(1)}
    small_names = SMALL_REPL + SMALL_SHARD
    small_pack, small_layout = _pack([small_full[k] for k in small_names + ['loss']])

    got_i, got_s = run_comm(give_half_comm(slabs_i, plain=[small_pack]), "tail_pair")
    sums_i = pair_sums_of(['a_w_in'], slabs_i, [got_i])
    small_pair = add_cast(small_pack[None], got_s[None], "pair_sum_small", dtype=F32)[0]
    outs = run_comm(chip_scatter_comm(sums_i, shared=small_pair), "tail_chips")
    mine_i = chip_sums_of(['a_w_in'], sums_i, outs[:1])
    small_chips = lax.dynamic_update_slice_in_dim(outs[1], small_pair[None], chip, axis=0)
    theirs_i = run_comm(swap_comm(mine_i), "tail_join")
    pieces['a_w_in'] = (mine_i[0], theirs_i[0])

    def whole(name):
        mine, theirs = pieces[name]
        return jnp.concatenate([jnp.where(ci == 0, mine, theirs), jnp.where(ci == 0, theirs, mine)], axis=0)

    for k in ('a_w_in', 'a_w_out', 'c_w_in', 'c_w_out'):
        grads[k] = whole(k)[None]
    for short, key in (('gate', 'ffn_w_gate'), ('up', 'ffn_w_up'), ('down', 'ffn_w_down')):
        grads[key] = jnp.stack([whole(short + '0'), whole(short + '1')])

    small_sum = sum_parts(small_chips, "small_sum")
    for k, g in zip(small_names + ['loss'], _unpack(small_sum, small_layout)):
        if k in SMALL_SHARD:
            width = a[k].shape[-1]
            g = lax.dynamic_slice_in_dim(g, chip * width, width, axis=g.ndim - 1)
        grads[k] = g
    loss = grads.pop('loss')[0]

    delta, new_m, new_v = {}, {}, {}
    for k in BIG:
        if k in TRANSPOSED:
            flip = lambda t: jnp.swapaxes(t, 1, 2)
            d, m, v = adamw(flip(a[k]), grads[k], flip(a['m_' + k]), flip(a['v_' + k]), "adamw_" + k)
            grads[k], delta[k], new_m[k], new_v[k] = flip(grads[k]), flip(d), flip(m), flip(v)
        else:
            delta[k], new_m[k], new_v[k] = adamw(a[k], grads[k], a['m_' + k], a['v_' + k], "adamw_" + k)
    packs = [_pack([src[k] for k in small_names])
             for src in (a, grads, {k: a['m_' + k] for k in small_names}, {k: a['v_' + k] for k in small_names})]
    d, m, v = adamw(packs[0][0][None], packs[1][0][None], packs[2][0][None], packs[3][0][None], "adamw_small")
    d, m, v = d[0], m[0], v[0]
    lay = packs[0][1]
    for k, dv, mv, vv in zip(small_names, _unpack(d, lay), _unpack(m, lay), _unpack(v, lay)):
        delta[k], new_m[k], new_v[k] = dv, mv, vv

    return (loss, grad_x.reshape(a['x'].shape), *[grads[k] for k in WEIGHTS], *[delta[k] for k in WEIGHTS],
            *[new_m[k] for k in WEIGHTS], *[new_v[k] for k in WEIGHTS])
```

```python
import functools

import jax
import jax.numpy as jnp
from jax import lax
from jax.experimental import pallas as pl
from jax.experimental.pallas import tpu as pltpu

F32 = jnp.float32
BF16 = jnp.bfloat16

D_MODEL = 1024
EPS = 1e-5
N_Q_HEADS, N_KV_HEADS, HEAD_DIM = 8, 2, 64
ATTN_BLOCK = 128
ATTN_WIDTH = N_Q_HEADS * HEAD_DIM
KV_WIDTH = N_KV_HEADS * HEAD_DIM
CONV_WIDTH = 512
CONV_KERNEL = 31
CONV_HALO = 32
POOL_WINDOWS = (2, 4, 8, 16)
POOL_WIDTH = 512
POOL_HALO = 16
SGU_WIDTH = 512
SGU_CHUNK = 128
D_FF = 2816
FF_CHUNK = 128
MXU_COLS = 256
LANES = 128
N_CHIPS = 4
N_DEV = 8

ADAM_LR, ADAM_B1, ADAM_B2, ADAM_EPS, ADAM_WD, ADAM_STEP = 0.001, 0.9, 0.999, 1e-08, 0.01, 10

VMEM_LIMIT = 56 * 2**20

WEIGHTS = ['mix_norm', 'a_w_in', 'a_b_in', 'a_sinks', 'a_conv_w', 'a_conv_b', 'a_cln_g', 'a_cln_b', 'a_w_out',
           'c_w_in', 'c_w_pool', 'c_pool_scale', 'c_sln_g', 'c_sln_b', 'c_w_s', 'c_b_s', 'c_w_out',
           'ffn_norm', 'ffn_w_gate', 'ffn_w_up', 'ffn_w_down', 'final_norm']
IN_NAMES = (['x'] + WEIGHTS + ['loss_target'] + ['m_' + n for n in WEIGHTS] + ['v_' + n for n in WEIGHTS])


def _params(*sem):
    return pltpu.CompilerParams(dimension_semantics=sem, vmem_limit_bytes=VMEM_LIMIT)


def _dot(a, b):
    return jnp.dot(a, b, preferred_element_type=F32)


def _dot_nt(a, b):
    return lax.dot_general(a, b, (((1,), (1,)), ((), ())), preferred_element_type=F32)


def _dot_tn(a, b):
    return lax.dot_general(a, b, (((0,), (0,)), ((), ())), preferred_element_type=F32)


def _sigmoid(v):
    return 0.5 * jnp.tanh(0.5 * v) + 0.5


def _row_tile(n, pref):
    t = min(n, pref)
    while n % t:
        t //= 2
    return t


def _col_tile(m, rows, budget=6 * 2**20):
    best = LANES
    for t in range(LANES, m + 1, LANES):
        if m % t == 0 and rows * t * 4 <= budget:
            best = t
    return best


class Comm:
    def __init__(self, ins, out_shapes, plan, count, aliases=None):
        self.ins, self.out_shapes, self.plan, self.count, self.aliases = ins, out_shapes, plan, count, aliases or {}

    def __add__(self, other):
        ni, no = len(self.ins), len(self.out_shapes)

        def plan(ins, outs, pos):
            return self.plan(ins[:ni], outs[:no], pos) + other.plan(ins[ni:], outs[no:], pos)

        aliases = dict(self.aliases)
        aliases.update({ni + i: no + o for i, o in other.aliases.items()})
        return Comm(list(self.ins) + list(other.ins), list(self.out_shapes) + list(other.out_shapes), plan,
                    self.count + other.count, aliases)

    def split(self, outs, other):
        return outs[:len(self.out_shapes)], outs[len(self.out_shapes):]


def _pcall(body, name, grid, in_specs, out_specs, out_shape, scratch_shapes, args, sem, comm=None):
    single = not isinstance(out_shape, (list, tuple))
    if single:
        out_specs, out_shape = [out_specs], [out_shape]
    if comm is None:
        res = pl.pallas_call(body, name=name, grid=grid, in_specs=in_specs, out_specs=list(out_specs),
                             out_shape=list(out_shape), scratch_shapes=list(scratch_shapes),
                             compiler_params=_params(*sem))(*args)
        return (res[0] if single else res), []
    na, nci, no, nco, ns = len(args), len(comm.ins), len(out_shape), len(comm.out_shapes), len(scratch_shapes)

    def wrapped(*refs):
        a_refs, ci_refs = refs[:na], refs[na:na + nci]
        o_refs, co_refs = refs[na + nci:na + nci + no], refs[na + nci + no:na + nci + no + nco]
        s_refs = refs[na + nci + no + nco:na + nci + no + nco + ns]
        send_sems, recv_sems = refs[-2], refs[-1]
        pos = _me()

        def copies():
            return [pltpu.make_async_remote_copy(src_ref=s, dst_ref=d, send_sem=send_sems.at[i],
                                                 recv_sem=recv_sems.at[i], device_id=_flip(pos, rel),
                                                 device_id_type=MESH)
                    for i, (rel, s, d) in enumerate(comm.plan(ci_refs, co_refs, pos))]

        first, last = None, None
        for d, size in enumerate(grid):
            f, l = pl.program_id(d) == 0, pl.program_id(d) == size - 1
            first = f if first is None else first & f
            last = l if last is None else last & l

        @pl.when(first)
        def _():
            for cp in copies():
                cp.start()

        body(*a_refs, *o_refs, *s_refs)

        @pl.when(last)
        def _():
            for cp in copies():
                cp.wait()

    res = pl.pallas_call(
        wrapped, name=name, grid=grid,
        in_specs=list(in_specs) + [ANY] * nci, out_specs=list(out_specs) + [ANY] * nco,
        out_shape=list(out_shape) + [jax.ShapeDtypeStruct(s, d) for s, d in comm.out_shapes],
        scratch_shapes=list(scratch_shapes) + [pltpu.SemaphoreType.DMA((comm.count,)),
                                               pltpu.SemaphoreType.DMA((comm.count,))],
        input_output_aliases={na + i: no + o for i, o in comm.aliases.items()},
        compiler_params=_params(*(["arbitrary"] * len(grid))),
    )(*args, *comm.ins)
    outs = res[:no]
    return (outs[0] if single else outs), list(res[no:])


def norm_inproj(x, gain, w, bias, splits, dtypes, name, comm=None, w_transposed=False):
    n = x.shape[0]
    m = w.shape[0] if w_transposed else w.shape[1]
    tm = _row_tile(n, 512)

    def body(x_ref, g_ref, w_ref, b_ref, hn_ref, *outs):
        xv = x_ref[...]
        r = lax.rsqrt(jnp.mean(xv * xv, axis=-1, keepdims=True) + EPS)
        hn = ((xv * r) * g_ref[...]).astype(BF16)
        hn_ref[...] = hn
        z = (_dot_nt if w_transposed else _dot)(hn, w_ref[...]) + b_ref[...]
        for o, (lo, hi) in zip(outs, splits):
            o[...] = z[:, lo:hi].astype(o.dtype)

    out_shape = [jax.ShapeDtypeStruct((n, D_MODEL), BF16)]
    out_specs = [pl.BlockSpec((tm, D_MODEL), lambda i: (i, 0))]
    for (lo, hi), dt in zip(splits, dtypes):
        out_shape.append(jax.ShapeDtypeStruct((n, hi - lo), dt))
        out_specs.append(pl.BlockSpec((tm, hi - lo), lambda i: (i, 0)))
    return _pcall(
        body, name, (n // tm,),
        [pl.BlockSpec((tm, D_MODEL), lambda i: (i, 0)),
         pl.BlockSpec((1, D_MODEL), lambda i: (0, 0)),
         pl.BlockSpec(w.shape, lambda i: (0, 0)),
         pl.BlockSpec((1, m), lambda i: (0, 0))],
        out_specs, out_shape, [], (x, gain, w, bias), ("parallel",), comm)


def out_proj(res, m1, m2, w, name, comm=None):
    n = res.shape[0]
    k1, k2 = m1.shape[1], m2.shape[1]
    assert k1 == k2
    tm = _row_tile(n, 512)

    def body(r_ref, a_ref, b_ref, w1_ref, w2_ref, o_ref):
        o_ref[...] = r_ref[...] + _dot(a_ref[...], w1_ref[...]) + _dot(b_ref[...], w2_ref[...])

    return _pcall(
        body, name, (n // tm,),
        [pl.BlockSpec((tm, D_MODEL), lambda i: (i, 0)),
         pl.BlockSpec((tm, k1), lambda i: (i, 0)),
         pl.BlockSpec((tm, k2), lambda i: (i, 0)),
         pl.BlockSpec((k1, D_MODEL), lambda i: (0, 0)),
         pl.BlockSpec((k2, D_MODEL), lambda i: (1, 0))],
        pl.BlockSpec((tm, D_MODEL), lambda i: (i, 0)),
        jax.ShapeDtypeStruct((n, D_MODEL), F32), [], (res, m1, m2, w, w), ("parallel",), comm)


def ffn_gate_up(h, gain, wg_t, wu_t, name, comm=None):
    n = h.shape[0]
    tm = _row_tile(n, 1024)
    th = D_FF // 2

    def body(h_ref, g_ref, wg_ref, wu_ref, hn_ref, go_ref, uo_ref):
        @pl.when(pl.program_id(1) == 0)
        def _():
            xv = h_ref[...]
            r = lax.rsqrt(jnp.mean(xv * xv, axis=-1, keepdims=True) + EPS)
            hn_ref[...] = ((xv * r) * g_ref[...]).astype(BF16)

        hn = hn_ref[...]
        go_ref[...] = _dot_nt(hn, wg_ref[...]).astype(BF16)
        uo_ref[...] = _dot_nt(hn, wu_ref[...]).astype(BF16)

    return _pcall(
        body, name, (n // tm, D_FF // th),
        [pl.BlockSpec((tm, D_MODEL), lambda i, j: (i, 0)),
         pl.BlockSpec((1, D_MODEL), lambda i, j: (0, 0)),
         pl.BlockSpec((th, D_MODEL), lambda i, j: (j, 0)),
         pl.BlockSpec((th, D_MODEL), lambda i, j: (j, 0))],
        [pl.BlockSpec((tm, D_MODEL), lambda i, j: (i, 0)),
         pl.BlockSpec((tm, th), lambda i, j: (i, j)),
         pl.BlockSpec((tm, th), lambda i, j: (i, j))],
        [jax.ShapeDtypeStruct((n, D_MODEL), BF16),
         jax.ShapeDtypeStruct((n, D_FF), BF16),
         jax.ShapeDtypeStruct((n, D_FF), BF16)],
        [], (h, gain, wg_t, wu_t), ("parallel", "arbitrary"), comm)


def ffn_down(h, g, u, wd, name, comm=None):
    n = h.shape[0]
    tm = _row_tile(n, 512)

    def body(h_ref, g_ref, u_ref, w_ref, o_ref, a_ref):
        for c0 in range(0, D_FF, FF_CHUNK):
            gv = g_ref[:, c0:c0 + FF_CHUNK]
            a_ref[:, c0:c0 + FF_CHUNK] = gv * _sigmoid(gv) * u_ref[:, c0:c0 + FF_CHUNK]
        o_ref[...] = h_ref[...] + _dot(a_ref[...], w_ref[...])

    return _pcall(
        body, name, (n // tm,),
        [pl.BlockSpec((tm, D_MODEL), lambda i: (i, 0)),
         pl.BlockSpec((tm, D_FF), lambda i: (i, 0)),
         pl.BlockSpec((tm, D_FF), lambda i: (i, 0)),
         pl.BlockSpec((D_FF, D_MODEL), lambda i: (0, 0))],
        pl.BlockSpec((tm, D_MODEL), lambda i: (i, 0)),
        jax.ShapeDtypeStruct((n, D_MODEL), F32),
        [pltpu.VMEM((tm, D_FF), BF16)], (h, g, u, wd), ("parallel",), comm)


def ffn_down_bwd(dh, g, u, wd, name, comm=None):
    n = dh.shape[0]
    tm = _row_tile(n, 512)

    def body(dh_ref, g_ref, u_ref, w_ref, dg_ref, du_ref, a_ref):
        dhb = dh_ref[...].astype(BF16)
        for c0 in range(0, D_FF, MXU_COLS):
            cols = slice(c0, c0 + MXU_COLS)
            da = _dot_nt(dhb, w_ref[cols, :]).astype(BF16)
            gv, uv = g_ref[:, cols], u_ref[:, cols]
            sg = _sigmoid(gv)
            act = gv * sg
            dg_ref[:, cols] = (da * uv) * (sg + act * (1.0 - sg))
            du_ref[:, cols] = da * act
            a_ref[:, cols] = act * uv

    spec_h = pl.BlockSpec((tm, D_FF), lambda i: (i, 0))
    return _pcall(
        body, name, (n // tm,),
        [pl.BlockSpec((tm, D_MODEL), lambda i: (i, 0)), spec_h, spec_h,
         pl.BlockSpec((D_FF, D_MODEL), lambda i: (0, 0))],
        [spec_h, spec_h, spec_h], [jax.ShapeDtypeStruct((n, D_FF), BF16)] * 3,
        [], (dh, g, u, wd), ("parallel",), comm)


def mm_tn(x, dy, name, xsum=False, comm=None):
    n, k = x.shape
    m = dy.shape[1]
    tn = _col_tile(m, k)
    tt = _row_tile(n, 1024)

    def body(x_ref, dy_ref, o_ref, *rest):
        j, t = pl.program_id(0), pl.program_id(1)
        xv = x_ref[...]
        part = _dot_tn(xv.astype(BF16), dy_ref[...].astype(BF16))

        @pl.when(t == 0)
        def _():
            o_ref[...] = part

        @pl.when(t > 0)
        def _():
            o_ref[...] += part

        if xsum:
            @pl.when(j == 0)
            def _():
                cs = jnp.broadcast_to(jnp.sum(xv.astype(F32), axis=0, keepdims=True), rest[0].shape)

                @pl.when(t == 0)
                def _():
                    rest[0][...] = cs

                @pl.when(t > 0)
                def _():
                    rest[0][...] += cs

    out_shape = [jax.ShapeDtypeStruct((k, m), F32)]
    out_specs = [pl.BlockSpec((k, tn), lambda j, t: (0, j))]
    if xsum:
        out_shape.append(jax.ShapeDtypeStruct((8, k), F32))
        out_specs.append(pl.BlockSpec((8, k), lambda j, t: (0, 0)))
    res, comm_outs = _pcall(
        body, name, (m // tn, n // tt),
        [pl.BlockSpec((tt, k), lambda j, t: (t, 0)),
         pl.BlockSpec((tt, tn), lambda j, t: (t, j))],
        out_specs, out_shape, [], (x, dy), ("arbitrary", "arbitrary"), comm)
    res = (res[0], res[1][0]) if xsum else res[0]
    return res if comm is None else (res, comm_outs)


def out_proj_bwd(dh, w, dtypes, name):
    n = dh.shape[0]
    k = w.shape[0]
    half = k // 2
    tm = _row_tile(n, 512)

    def body(dh_ref, w_ref, a_ref, b_ref):
        dm = _dot_nt(dh_ref[...].astype(BF16), w_ref[...])
        a_ref[...] = dm[:, :half].astype(a_ref.dtype)
        b_ref[...] = dm[:, half:].astype(b_ref.dtype)

    return pl.pallas_call(
        body, name=name, grid=(n // tm,),
        in_specs=[pl.BlockSpec((tm, D_MODEL), lambda i: (i, 0)),
                  pl.BlockSpec((k, D_MODEL), lambda i: (0, 0))],
        out_specs=[pl.BlockSpec((tm, half), lambda i: (i, 0))] * 2,
        out_shape=[jax.ShapeDtypeStruct((n, half), dtypes[0]), jax.ShapeDtypeStruct((n, half), dtypes[1])],
        compiler_params=_params("parallel"),
    )(dh, w)


def proj_rms_bwd(dys, ws, h_in, gain, dres, nk, name, tm_pref=512, comm=None, w_transposed=False):
    n = h_in.shape[0]
    npair = len(dys)
    tm = _row_tile(n, tm_pref)
    tks = [dy.shape[1] // nk for dy in dys]
    mm = _dot if w_transposed else _dot_nt

    def body(*refs):
        dy_refs = refs[:npair]
        w_refs = refs[npair:2 * npair]
        h_ref, g_ref, dr_ref, o_ref, dg_ref, acc_ref = refs[2 * npair:]
        i, k = pl.program_id(0), pl.program_id(1)
        part = mm(dy_refs[0][...], w_refs[0][...])
        for p in range(1, npair):
            part = part + mm(dy_refs[p][...], w_refs[p][...])

        @pl.when(k == 0)
        def _():
            acc_ref[...] = part

        @pl.when(k > 0)
        def _():
            acc_ref[...] += part

        @pl.when(k == nk - 1)
        def _():
            dhn = acc_ref[...]
            xv = h_ref[...]
            r = lax.rsqrt(jnp.mean(xv * xv, axis=-1, keepdims=True) + EPS)
            xh = xv * r
            uv = dhn * g_ref[...]
            o_ref[...] = dr_ref[...] + r * (uv - xh * jnp.mean(uv * xh, axis=-1, keepdims=True))
            dgp = jnp.broadcast_to(jnp.sum(dhn * xh, axis=0, keepdims=True), dg_ref.shape)

            @pl.when(i == 0)
            def _():
                dg_ref[...] = dgp

            @pl.when(i > 0)
            def _():
                dg_ref[...] += dgp

    row = pl.BlockSpec((tm, D_MODEL), lambda i, k: (i, 0))
    in_specs = [pl.BlockSpec((tm, tk), lambda i, k: (i, k)) for tk in tks]
    if w_transposed:
        in_specs += [pl.BlockSpec((tk, D_MODEL), lambda i, k: (k, 0)) for tk in tks]
    else:
        in_specs += [pl.BlockSpec((D_MODEL, tk), lambda i, k: (0, k)) for tk in tks]
    in_specs += [row, pl.BlockSpec((1, D_MODEL), lambda i, k: (0, 0)), row]
    (dh, dgain), comm_outs = _pcall(
        body, name, (n // tm, nk), in_specs,
        [row, pl.BlockSpec((8, D_MODEL), lambda i, k: (0, 0))],
        [jax.ShapeDtypeStruct((n, D_MODEL), F32), jax.ShapeDtypeStruct((8, D_MODEL), F32)],
        [pltpu.VMEM((tm, D_MODEL), F32)], (*dys, *ws, h_in, gain, dres), ("arbitrary", "arbitrary"), comm)
    return dh, dgain[0], comm_outs


def loss_head(h, gain, target, name):
    n = h.shape[0]
    tm = _row_tile(n, 512)

    def body(h_ref, g_ref, t_ref, dh_ref, dg_ref, l_ref):
        i = pl.program_id(0)
        xv = h_ref[...]
        r = lax.rsqrt(jnp.mean(xv * xv, axis=-1, keepdims=True) + EPS)
        xh = xv * r
        err = xh * g_ref[...] - t_ref[...]
        dy = err * (1.0 / D_MODEL)
        uv = dy * g_ref[...]
        dh_ref[...] = r * (uv - xh * jnp.mean(uv * xh, axis=-1, keepdims=True))
        dgp = jnp.broadcast_to(jnp.sum(dy * xh, axis=0, keepdims=True), dg_ref.shape)
        lp = jnp.sum(jnp.sum(err * err, axis=-1, keepdims=True), axis=0, keepdims=True) * (0.5 / D_MODEL)
        lp = jnp.broadcast_to(lp, l_ref.shape)

        @pl.when(i == 0)
        def _():
            dg_ref[...] = dgp
            l_ref[...] = lp

        @pl.when(i > 0)
        def _():
            dg_ref[...] += dgp
            l_ref[...] += lp

    row = pl.BlockSpec((tm, D_MODEL), lambda i: (i, 0))
    dh, dg, l = pl.pallas_call(
        body, name=name, grid=(n // tm,),
        in_specs=[row, pl.BlockSpec((1, D_MODEL), lambda i: (0, 0)), row],
        out_specs=[row, pl.BlockSpec((8, D_MODEL), lambda i: (0, 0)), pl.BlockSpec((8, LANES), lambda i: (0, 0))],
        out_shape=[jax.ShapeDtypeStruct((n, D_MODEL), F32), jax.ShapeDtypeStruct((8, D_MODEL), F32),
                   jax.ShapeDtypeStruct((8, LANES), F32)],
        compiler_params=_params("arbitrary"),
    )(h, gain, target)
    return dh, dg[0], l[0, 0]


GROUP = N_Q_HEADS // N_KV_HEADS
GQ = GROUP * ATTN_BLOCK


def _attn_mask_t(n):
    r = lax.broadcasted_iota(jnp.int32, (2 * ATTN_BLOCK, GQ), 0)
    qi = lax.broadcasted_iota(jnp.int32, (2 * ATTN_BLOCK, GQ), 1) & (ATTN_BLOCK - 1)
    band = (r > qi) & (r <= qi + ATTN_BLOCK)
    return band & ((r >= ATTN_BLOCK) | (n > 0))


def _stack_heads(blk, kh):
    return jnp.concatenate([blk[:, (kh * GROUP + g) * HEAD_DIM:(kh * GROUP + g + 1) * HEAD_DIM]
                            for g in range(GROUP)], axis=0)


def _attn_probs_t(kk, qs, mask, sink):
    s = _dot_nt(kk, qs) * (HEAD_DIM ** -0.5)
    s = jnp.where(mask, s, -1e30)
    m = jnp.maximum(jnp.max(s, axis=0, keepdims=True), sink)
    p = jnp.exp(s - m)
    esink = jnp.exp(sink - m)
    inv = 1.0 / (jnp.sum(p, axis=0, keepdims=True) + esink)
    return p * inv, esink * inv


def attn_fwd(q, kv, sinks_t, nseq, seq, name, comm=None):
    nb = seq // ATTN_BLOCK

    def body(q_ref, kv_ref, s_ref, o_ref, kvp):
        kvp[0:ATTN_BLOCK, :] = jnp.zeros((ATTN_BLOCK, 2 * KV_WIDTH), BF16)
        kvp[ATTN_BLOCK:, :] = kv_ref[...]

        def blk(n, carry):
            st = pl.multiple_of(n * ATTN_BLOCK, ATTN_BLOCK)
            qb = q_ref[pl.ds(st, ATTN_BLOCK), :]
            kw = kvp[pl.ds(st, 2 * ATTN_BLOCK), :]
            mask = _attn_mask_t(n)
            for kh in range(N_KV_HEADS):
                kk = kw[:, kh * HEAD_DIM:(kh + 1) * HEAD_DIM]
                vv = kw[:, KV_WIDTH + kh * HEAD_DIM:KV_WIDTH + (kh + 1) * HEAD_DIM]
                probs, _ = _attn_probs_t(kk, _stack_heads(qb, kh), mask, s_ref[kh:kh + 1, :])
                ot = _dot_tn(vv, probs.astype(BF16))
                for pair in range(GROUP // 2):
                    two = jnp.concatenate([ot[:, (2 * pair) * ATTN_BLOCK:(2 * pair + 1) * ATTN_BLOCK],
                                           ot[:, (2 * pair + 1) * ATTN_BLOCK:(2 * pair + 2) * ATTN_BLOCK]], axis=0)
                    col = (kh * GROUP + 2 * pair) * HEAD_DIM
                    o_ref[pl.ds(st, ATTN_BLOCK), col:col + 2 * HEAD_DIM] = two.T.astype(o_ref.dtype)
            return carry

        lax.fori_loop(0, nb, blk, 0, unroll=2)

    return _pcall(
        body, name, (nseq,),
        [pl.BlockSpec((seq, ATTN_WIDTH), lambda b: (b, 0)),
         pl.BlockSpec((seq, 2 * KV_WIDTH), lambda b: (b, 0)),
         pl.BlockSpec((8, GQ), lambda b: (0, 0))],
        pl.BlockSpec((seq, ATTN_WIDTH), lambda b: (b, 0)),
        jax.ShapeDtypeStruct((nseq * seq, ATTN_WIDTH), BF16),
        [pltpu.VMEM((ATTN_BLOCK + seq, 2 * KV_WIDTH), BF16)], (q, kv, sinks_t), ("parallel",), comm)


def attn_bwd(q, kv, sinks_t, do, nseq, seq, name, comm=None):
    nb = seq // ATTN_BLOCK

    def body(q_ref, kv_ref, s_ref, do_ref, dq_ref, dkv_ref, ds_ref, kvp, dkvp, dsacc):
        @pl.when(pl.program_id(0) == 0)
        def _():
            dsacc[...] = jnp.zeros(dsacc.shape, F32)

        kvp[0:ATTN_BLOCK, :] = jnp.zeros((ATTN_BLOCK, 2 * KV_WIDTH), BF16)
        kvp[ATTN_BLOCK:, :] = kv_ref[...]
        dkvp[...] = jnp.zeros(dkvp.shape, F32)

        def blk(n, carry):
            st = pl.multiple_of(n * ATTN_BLOCK, ATTN_BLOCK)
            qb = q_ref[pl.ds(st, ATTN_BLOCK), :]
            dob = do_ref[pl.ds(st, ATTN_BLOCK), :]
            kw = kvp[pl.ds(st, 2 * ATTN_BLOCK), :]
            mask = _attn_mask_t(n)
            for kh in range(N_KV_HEADS):
                kk = kw[:, kh * HEAD_DIM:(kh + 1) * HEAD_DIM]
                vv = kw[:, KV_WIDTH + kh * HEAD_DIM:KV_WIDTH + (kh + 1) * HEAD_DIM]
                qs = _stack_heads(qb, kh)
                dos = _stack_heads(dob, kh)
                probs, psink = _attn_probs_t(kk, qs, mask, s_ref[kh:kh + 1, :])
                dp = _dot_nt(vv, dos)
                dv = _dot(probs.astype(BF16), dos)
                rowdot = jnp.sum(probs * dp, axis=0, keepdims=True)
                dsc = (probs * (dp - rowdot) * (HEAD_DIM ** -0.5)).astype(BF16)
                dsacc[kh:kh + 1, :] += -psink * rowdot
                dk = _dot(dsc, qs)
                dqs = _dot_tn(dsc, kk)
                for g in range(GROUP):
                    col = (kh * GROUP + g) * HEAD_DIM
                    dq_ref[pl.ds(st, ATTN_BLOCK), col:col + HEAD_DIM] = (
                        dqs[g * ATTN_BLOCK:(g + 1) * ATTN_BLOCK].astype(dq_ref.dtype))
                dkvp[pl.ds(st, 2 * ATTN_BLOCK), kh * HEAD_DIM:(kh + 1) * HEAD_DIM] += dk
                dkvp[pl.ds(st, 2 * ATTN_BLOCK), KV_WIDTH + kh * HEAD_DIM:KV_WIDTH + (kh + 1) * HEAD_DIM] += dv
            return carry

        lax.fori_loop(0, nb, blk, 0, unroll=2)
        dkv_ref[...] = dkvp[ATTN_BLOCK:, :].astype(dkv_ref.dtype)

        @pl.when(pl.program_id(0) == nseq - 1)
        def _():
            for kh in range(N_KV_HEADS):
                for g in range(GROUP):
                    tot = jnp.sum(dsacc[kh:kh + 1, g * ATTN_BLOCK:(g + 1) * ATTN_BLOCK], axis=1, keepdims=True)
                    ds_ref[kh * GROUP + g:kh * GROUP + g + 1, :] = jnp.broadcast_to(tot, (1, LANES))

    seq_q = pl.BlockSpec((seq, ATTN_WIDTH), lambda b: (b, 0))
    seq_kv = pl.BlockSpec((seq, 2 * KV_WIDTH), lambda b: (b, 0))
    return _pcall(
        body, name, (nseq,),
        [seq_q, seq_kv, pl.BlockSpec((8, GQ), lambda b: (0, 0)), seq_q],
        [seq_q, seq_kv, pl.BlockSpec((N_Q_HEADS, LANES), lambda b: (0, 0))],
        [jax.ShapeDtypeStruct((nseq * seq, ATTN_WIDTH), BF16),
         jax.ShapeDtypeStruct((nseq * seq, 2 * KV_WIDTH), BF16),
         jax.ShapeDtypeStruct((N_Q_HEADS, LANES), F32)],
        [pltpu.VMEM((ATTN_BLOCK + seq, 2 * KV_WIDTH), BF16),
         pltpu.VMEM((ATTN_BLOCK + seq, 2 * KV_WIDTH), F32),
         pltpu.VMEM((8, GQ), F32)], (q, kv, sinks_t, do), ("arbitrary",), comm)


CONV_T = 128


def _conv_taps(win, w_ref, lanes, init):
    acc = init
    for j in range(CONV_KERNEL):
        sh = win if j == CONV_KERNEL - 1 else pltpu.roll(win, CONV_KERNEL - 1 - j, 0)
        acc = acc + w_ref[j:j + 1, lanes] * sh[CONV_HALO:CONV_HALO + CONV_T]
    return acc


def _conv_block(h0p, w_ref, vec_ref, st):
    cols = []
    for cs in range(CONV_WIDTH // LANES):
        lanes = slice(cs * LANES, (cs + 1) * LANES)
        win = h0p[pl.ds(st, CONV_T + CONV_HALO), lanes]
        init = jnp.broadcast_to(vec_ref[0:1, lanes], (CONV_T, LANES))
        cols.append(_conv_taps(win, w_ref, lanes, init))
    return jnp.concatenate(cols, axis=-1)


def _glu_store(c_ref, h0p, st):
    cb = c_ref[pl.ds(st, CONV_T), :]
    h0p[pl.ds(pl.multiple_of(st + CONV_HALO, CONV_HALO), CONV_T), :] = cb[:, :CONV_WIDTH] * _sigmoid(cb[:, CONV_WIDTH:])


def conv_fwd(c, w, vec, nseq, seq, name, comm=None):
    nb = seq // CONV_T

    def body(c_ref, w_ref, vec_ref, o_ref, h1_ref, h0p):
        h0p[0:CONV_HALO, :] = jnp.zeros((CONV_HALO, CONV_WIDTH), F32)

        def blk(n, carry):
            st = pl.multiple_of(n * CONV_T, CONV_T)
            _glu_store(c_ref, h0p, st)
            h1 = _conv_block(h0p, w_ref, vec_ref, st)
            h1_ref[pl.ds(st, CONV_T), :] = h1
            mu = jnp.mean(h1, axis=-1, keepdims=True)
            xc = h1 - mu
            rstd = lax.rsqrt(jnp.mean(xc * xc, axis=-1, keepdims=True) + EPS)
            y = xc * rstd * vec_ref[1:2, :] + vec_ref[2:3, :]
            o_ref[pl.ds(st, CONV_T), :] = (y * _sigmoid(y)).astype(o_ref.dtype)
            return carry

        lax.fori_loop(0, nb, blk, 0)

    return _pcall(
        body, name, (nseq,),
        [pl.BlockSpec((seq, 2 * CONV_WIDTH), lambda b: (b, 0)),
         pl.BlockSpec((32, CONV_WIDTH), lambda b: (0, 0)),
         pl.BlockSpec((8, CONV_WIDTH), lambda b: (0, 0))],
        [pl.BlockSpec((seq, CONV_WIDTH), lambda b: (b, 0))] * 2,
        [jax.ShapeDtypeStruct((nseq * seq, CONV_WIDTH), BF16), jax.ShapeDtypeStruct((nseq * seq, CONV_WIDTH), F32)],
        [pltpu.VMEM((CONV_HALO + seq, CONV_WIDTH), F32)], (c, w, vec), ("parallel",), comm)


def conv_bwd(c, h1_saved, w, vec, dout, nseq, seq, name, comm=None):
    nb = seq // CONV_T

    def body(c_ref, h1_ref, w_ref, vec_ref, do_ref, dc_ref, dw_ref, dvec_ref, h0p, dh1p):
        @pl.when(pl.program_id(0) == 0)
        def _():
            dw_ref[...] = jnp.zeros(dw_ref.shape, F32)
            dvec_ref[...] = jnp.zeros(dvec_ref.shape, F32)

        h0p[0:CONV_HALO, :] = jnp.zeros((CONV_HALO, CONV_WIDTH), F32)
        dh1p[seq:seq + CONV_HALO, :] = jnp.zeros((CONV_HALO, CONV_WIDTH), F32)

        def pass_a(n, carry):
            st = pl.multiple_of(n * CONV_T, CONV_T)
            _glu_store(c_ref, h0p, st)
            h1 = h1_ref[pl.ds(st, CONV_T), :]
            mu = jnp.mean(h1, axis=-1, keepdims=True)
            xc = h1 - mu
            rstd = lax.rsqrt(jnp.mean(xc * xc, axis=-1, keepdims=True) + EPS)
            xh = xc * rstd
            y = xh * vec_ref[1:2, :] + vec_ref[2:3, :]
            sg = _sigmoid(y)
            dy = do_ref[pl.ds(st, CONV_T), :] * (sg * (1.0 + y * (1.0 - sg)))
            dvec_ref[1:2, :] += jnp.sum(dy * xh, axis=0, keepdims=True)
            dvec_ref[2:3, :] += jnp.sum(dy, axis=0, keepdims=True)
            dxh = dy * vec_ref[1:2, :]
            dh1 = rstd * (dxh - jnp.mean(dxh, axis=-1, keepdims=True)
                          - xh * jnp.mean(dxh * xh, axis=-1, keepdims=True))
            dvec_ref[0:1, :] += jnp.sum(dh1, axis=0, keepdims=True)
            dh1p[pl.ds(st, CONV_T), :] = dh1
            return carry

        lax.fori_loop(0, nb, pass_a, 0)

        def pass_b(n, carry):
            st = pl.multiple_of(n * CONV_T, CONV_T)
            cols = []
            for cs in range(CONV_WIDTH // LANES):
                lanes = slice(cs * LANES, (cs + 1) * LANES)
                wind = dh1p[pl.ds(st, CONV_T + CONV_HALO), lanes]
                winh = h0p[pl.ds(st, CONV_T + CONV_HALO), lanes]
                d1 = wind[0:CONV_T]
                acc = jnp.zeros((CONV_T, LANES), F32)
                for j in range(CONV_KERNEL):
                    acc = acc + w_ref[j:j + 1, lanes] * pltpu.roll(wind, 2 + j, 0)[CONV_HALO:CONV_HALO + CONV_T]
                    hs = winh if j == CONV_KERNEL - 1 else pltpu.roll(winh, CONV_KERNEL - 1 - j, 0)
                    dw_ref[j:j + 1, lanes] += jnp.sum(d1 * hs[CONV_HALO:CONV_HALO + CONV_T], axis=0, keepdims=True)
                cols.append(acc)
            dh0 = jnp.concatenate(cols, axis=-1)
            cb = c_ref[pl.ds(st, CONV_T), :]
            av, gt = cb[:, :CONV_WIDTH], cb[:, CONV_WIDTH:]
            sg = _sigmoid(gt)
            dc_ref[pl.ds(st, CONV_T), :] = jnp.concatenate(
                [dh0 * sg, dh0 * av * sg * (1.0 - sg)], axis=-1).astype(dc_ref.dtype)
            return carry

        lax.fori_loop(0, nb, pass_b, 0)

    return _pcall(
        body, name, (nseq,),
        [pl.BlockSpec((seq, 2 * CONV_WIDTH), lambda b: (b, 0)),
         pl.BlockSpec((seq, CONV_WIDTH), lambda b: (b, 0)),
         pl.BlockSpec((32, CONV_WIDTH), lambda b: (0, 0)),
         pl.BlockSpec((8, CONV_WIDTH), lambda b: (0, 0)),
         pl.BlockSpec((seq, CONV_WIDTH), lambda b: (b, 0))],
        [pl.BlockSpec((seq, 2 * CONV_WIDTH), lambda b: (b, 0)),
         pl.BlockSpec((32, CONV_WIDTH), lambda b: (0, 0)),
         pl.BlockSpec((8, CONV_WIDTH), lambda b: (0, 0))],
        [jax.ShapeDtypeStruct((nseq * seq, 2 * CONV_WIDTH), BF16),
         jax.ShapeDtypeStruct((32, CONV_WIDTH), F32),
         jax.ShapeDtypeStruct((8, CONV_WIDTH), F32)],
        [pltpu.VMEM((CONV_HALO + seq, CONV_WIDTH), F32),
         pltpu.VMEM((seq + CONV_HALO, CONV_WIDTH), F32)], (c, h1_saved, w, vec, dout), ("arbitrary",), comm)


POOL_T = 128


def _pooled_block(zpp, st, grp):
    lanes = slice(grp * LANES, (grp + 1) * LANES)
    win = zpp[pl.ds(st, POOL_T + POOL_HALO), lanes]
    acc = win
    for lvl in range(grp + 1):
        acc = acc + pltpu.roll(acc, 1 << lvl, 0)
    t = st + lax.broadcasted_iota(jnp.int32, (POOL_T, 1), 0)
    inv = 1.0 / jnp.minimum(t + 1, POOL_WINDOWS[grp]).astype(F32)
    return acc[POOL_HALO:] * inv - win[POOL_HALO:], inv


def pool_fwd(zp, wp, scale, nseq, seq, name):
    nb = seq // POOL_T

    def body(z_ref, wp_ref, sc_ref, o_ref, zpp):
        zpp[0:POOL_HALO, :] = jnp.zeros((POOL_HALO, POOL_WIDTH), F32)
        zpp[POOL_HALO:, :] = z_ref[...]

        def blk(n, carry):
            st = pl.multiple_of(n * POOL_T, POOL_T)
            for grp in range(len(POOL_WINDOWS)):
                lanes = slice(grp * LANES, (grp + 1) * LANES)
                pooled, _ = _pooled_block(zpp, st, grp)
                o_ref[pl.ds(st, POOL_T), lanes] = (
                    _dot(pooled.astype(BF16), wp_ref[grp]) * sc_ref[0:1, lanes]).astype(o_ref.dtype)
            return carry

        lax.fori_loop(0, nb, blk, 0)

    return pl.pallas_call(
        body, name=name, grid=(nseq,),
        in_specs=[pl.BlockSpec((seq, POOL_WIDTH), lambda b: (b, 0)),
                  pl.BlockSpec((4, LANES, LANES), lambda b: (0, 0, 0)),
                  pl.BlockSpec((1, POOL_WIDTH), lambda b: (0, 0))],
        out_specs=pl.BlockSpec((seq, POOL_WIDTH), lambda b: (b, 0)),
        out_shape=jax.ShapeDtypeStruct((nseq * seq, POOL_WIDTH), BF16),
        scratch_shapes=[pltpu.VMEM((POOL_HALO + seq, POOL_WIDTH), F32)],
        compiler_params=_params("parallel"),
    )(zp, wp, scale)


def pool_bwd(zp, wp, scale, dout, nseq, seq, name, comm=None):
    nb = seq // POOL_T

    def body(z_ref, wp_ref, sc_ref, do_ref, dz_ref, dwp_ref, dsc_ref, zpp, dpcp, negd):
        @pl.when(pl.program_id(0) == 0)
        def _():
            dwp_ref[...] = jnp.zeros(dwp_ref.shape, F32)
            dsc_ref[...] = jnp.zeros(dsc_ref.shape, F32)

        zpp[0:POOL_HALO, :] = jnp.zeros((POOL_HALO, POOL_WIDTH), F32)
        zpp[POOL_HALO:, :] = z_ref[...]
        dpcp[seq:seq + POOL_HALO, :] = jnp.zeros((POOL_HALO, POOL_WIDTH), F32)

        def pass_a(n, carry):
            st = pl.multiple_of(n * POOL_T, POOL_T)
            for grp in range(len(POOL_WINDOWS)):
                lanes = slice(grp * LANES, (grp + 1) * LANES)
                pooled, inv = _pooled_block(zpp, st, grp)
                pb = pooled.astype(BF16)
                dob = do_ref[pl.ds(st, POOL_T), lanes]
                dsc_ref[0:1, lanes] += jnp.sum(dob * _dot(pb, wp_ref[grp]), axis=0, keepdims=True)
                dpm = (dob * sc_ref[0:1, lanes]).astype(BF16)
                dwp_ref[grp] += _dot_tn(pb, dpm)
                dpooled = _dot_nt(dpm, wp_ref[grp])
                negd[pl.ds(st, POOL_T), lanes] = -dpooled
                dpcp[pl.ds(st, POOL_T), lanes] = dpooled * inv
            return carry

        lax.fori_loop(0, nb, pass_a, 0)

        def pass_b(n, carry):
            st = pl.multiple_of(n * POOL_T, POOL_T)
            rows = POOL_T + POOL_HALO
            for grp in range(len(POOL_WINDOWS)):
                lanes = slice(grp * LANES, (grp + 1) * LANES)
                acc = dpcp[pl.ds(st, rows), lanes]
                for lvl in range(grp + 1):
                    acc = acc + pltpu.roll(acc, rows - (1 << lvl), 0)
                dz_ref[pl.ds(st, POOL_T), lanes] = (acc[0:POOL_T] + negd[pl.ds(st, POOL_T), lanes]).astype(dz_ref.dtype)
            return carry

        lax.fori_loop(0, nb, pass_b, 0)

    seq_spec = pl.BlockSpec((seq, POOL_WIDTH), lambda b: (b, 0))
    return _pcall(
        body, name, (nseq,),
        [seq_spec, pl.BlockSpec((4, LANES, LANES), lambda b: (0, 0, 0)),
         pl.BlockSpec((1, POOL_WIDTH), lambda b: (0, 0)), seq_spec],
        [seq_spec, pl.BlockSpec((4, LANES, LANES), lambda b: (0, 0, 0)),
         pl.BlockSpec((8, POOL_WIDTH), lambda b: (0, 0))],
        [jax.ShapeDtypeStruct((nseq * seq, POOL_WIDTH), BF16),
         jax.ShapeDtypeStruct((4, LANES, LANES), F32),
         jax.ShapeDtypeStruct((8, POOL_WIDTH), F32)],
        [pltpu.VMEM((POOL_HALO + seq, POOL_WIDTH), F32),
         pltpu.VMEM((seq + POOL_HALO, POOL_WIDTH), F32),
         pltpu.VMEM((seq, POOL_WIDTH), F32)], (zp, wp, scale, dout), ("arbitrary",), comm)


GELU_C0 = 0.7978845608028654
GELU_C1 = 0.044715


def _gelu(xv):
    return xv * (0.5 * (1.0 + jnp.tanh(GELU_C0 * (xv + GELU_C1 * (xv * xv * xv)))))


def _gelu_grad(xv):
    t = jnp.tanh(GELU_C0 * (xv + GELU_C1 * (xv * xv * xv)))
    return 0.5 * (1.0 + t) + 0.5 * xv * (1.0 - t * t) * (GELU_C0 * (1.0 + 3.0 * GELU_C1 * xv * xv))


def _tril():
    ti = lax.broadcasted_iota(jnp.int32, (SGU_CHUNK, SGU_CHUNK), 0)
    si = lax.broadcasted_iota(jnp.int32, (SGU_CHUNK, SGU_CHUNK), 1)
    return si <= ti


def sgu_fwd(zs, ws, bst, ln, nseq, seq, name):
    nc = seq // SGU_CHUNK

    def body(z_ref, ws_ref, bs_ref, ln_ref, o_ref):
        tril = _tril()

        def blk(n, carry):
            st = pl.multiple_of(n * SGU_CHUNK, SGU_CHUNK)
            ge = _gelu(z_ref[pl.ds(st, SGU_CHUNK), :])
            uu, vv = ge[:, :SGU_WIDTH], ge[:, SGU_WIDTH:]
            mu = jnp.mean(vv, axis=-1, keepdims=True)
            xc = vv - mu
            rstd = lax.rsqrt(jnp.mean(xc * xc, axis=-1, keepdims=True) + EPS)
            vn = (xc * rstd * ln_ref[0:1, :] + ln_ref[1:2, :]).astype(BF16)
            for g in range(4):
                lanes = slice(g * LANES, (g + 1) * LANES)
                wm = jnp.where(tril, ws_ref[g], 0.0).astype(BF16)
                mixed = _dot(wm, vn[:, lanes]) + bs_ref[:, g:g + 1]
                o_ref[pl.ds(st, SGU_CHUNK), lanes] = (uu[:, lanes] * mixed).astype(o_ref.dtype)
            return carry

        lax.fori_loop(0, nc, blk, 0)

    return pl.pallas_call(
        body, name=name, grid=(nseq,),
        in_specs=[pl.BlockSpec((seq, 2 * SGU_WIDTH), lambda b: (b, 0)),
                  pl.BlockSpec((4, LANES, LANES), lambda b: (0, 0, 0)),
                  pl.BlockSpec((SGU_CHUNK, 4), lambda b: (0, 0)),
                  pl.BlockSpec((8, SGU_WIDTH), lambda b: (0, 0))],
        out_specs=pl.BlockSpec((seq, SGU_WIDTH), lambda b: (b, 0)),
        out_shape=jax.ShapeDtypeStruct((nseq * seq, SGU_WIDTH), BF16),
        compiler_params=_params("parallel"),
    )(zs, ws, bst, ln)


def sgu_bwd(zs, ws, bst, ln, dout, nseq, seq, name, comm=None):
    nc = seq // SGU_CHUNK

    def body(z_ref, ws_ref, bs_ref, ln_ref, do_ref, dz_ref, dws_ref, dbs_ref, dln_ref):
        @pl.when(pl.program_id(0) == 0)
        def _():
            dws_ref[...] = jnp.zeros(dws_ref.shape, F32)
            dbs_ref[...] = jnp.zeros(dbs_ref.shape, F32)
            dln_ref[...] = jnp.zeros(dln_ref.shape, F32)

        tril = _tril()

        def blk(n, carry):
            st = pl.multiple_of(n * SGU_CHUNK, SGU_CHUNK)
            zv = z_ref[pl.ds(st, SGU_CHUNK), :]
            ge = _gelu(zv)
            uu, vv = ge[:, :SGU_WIDTH], ge[:, SGU_WIDTH:]
            mu = jnp.mean(vv, axis=-1, keepdims=True)
            xc = vv - mu
            rstd = lax.rsqrt(jnp.mean(xc * xc, axis=-1, keepdims=True) + EPS)
            xh = xc * rstd
            vn = (xh * ln_ref[0:1, :] + ln_ref[1:2, :]).astype(BF16)
            dob = do_ref[pl.ds(st, SGU_CHUNK), :]
            du_cols, dvn_cols = [], []
            for g in range(4):
                lanes = slice(g * LANES, (g + 1) * LANES)
                wm = jnp.where(tril, ws_ref[g], 0.0).astype(BF16)
                mixed = _dot(wm, vn[:, lanes]) + bs_ref[:, g:g + 1]
                du_cols.append(dob[:, lanes] * mixed)
                dmix = dob[:, lanes] * uu[:, lanes]
                dbs_ref[g] += jnp.broadcast_to(jnp.sum(dmix, axis=-1, keepdims=True), (SGU_CHUNK, LANES))
                dmb = dmix.astype(BF16)
                dws_ref[g] += jnp.where(tril, _dot_nt(dmb, vn[:, lanes]), 0.0)
                dvn_cols.append(_dot_tn(wm, dmb))
            dvn = jnp.concatenate(dvn_cols, axis=-1)
            dln_ref[0:1, :] += jnp.sum(dvn * xh, axis=0, keepdims=True)
            dln_ref[1:2, :] += jnp.sum(dvn, axis=0, keepdims=True)
            dxh = dvn * ln_ref[0:1, :]
            dv = rstd * (dxh - jnp.mean(dxh, axis=-1, keepdims=True)
                         - xh * jnp.mean(dxh * xh, axis=-1, keepdims=True))
            dge = jnp.concatenate(du_cols + [dv], axis=-1)
            dz_ref[pl.ds(st, SGU_CHUNK), :] = (dge * _gelu_grad(zv)).astype(dz_ref.dtype)
            return carry

        lax.fori_loop(0, nc, blk, 0)

    w_spec = pl.BlockSpec((4, LANES, LANES), lambda b: (0, 0, 0))
    ln_spec = pl.BlockSpec((8, SGU_WIDTH), lambda b: (0, 0))
    return _pcall(
        body, name, (nseq,),
        [pl.BlockSpec((seq, 2 * SGU_WIDTH), lambda b: (b, 0)), w_spec,
         pl.BlockSpec((SGU_CHUNK, 4), lambda b: (0, 0)), ln_spec,
         pl.BlockSpec((seq, SGU_WIDTH), lambda b: (b, 0))],
        [pl.BlockSpec((seq, 2 * SGU_WIDTH), lambda b: (b, 0)), w_spec, w_spec, ln_spec],
        [jax.ShapeDtypeStruct((nseq * seq, 2 * SGU_WIDTH), BF16),
         jax.ShapeDtypeStruct((4, LANES, LANES), F32),
         jax.ShapeDtypeStruct((4, LANES, LANES), F32),
         jax.ShapeDtypeStruct((8, SGU_WIDTH), F32)],
        [], (zs, ws, bst, ln, dout), ("arbitrary",), comm)


def _ew_rows(rows, cols, nbuf):
    t = _row_tile(rows, 1024)
    while t > 8 and t * cols * 4 * nbuf * 2 > 24 * 2**20:
        t //= 2
    return t


def adamw(w, g, m, v, name):
    layers, rows, cols = w.shape
    tr = _ew_rows(rows, cols, 7)

    def body(w_ref, g_ref, m_ref, v_ref, d_ref, mo_ref, vo_ref):
        gv = g_ref[...]
        mn = ADAM_B1 * m_ref[...] + (1.0 - ADAM_B1) * gv
        vn = ADAM_B2 * v_ref[...] + (1.0 - ADAM_B2) * (gv * gv)
        m_hat = mn / (1.0 - ADAM_B1 ** ADAM_STEP)
        v_hat = vn / (1.0 - ADAM_B2 ** ADAM_STEP)
        d_ref[...] = -ADAM_LR * (m_hat / (jnp.sqrt(v_hat) + ADAM_EPS) + ADAM_WD * w_ref[...])
        mo_ref[...] = mn
        vo_ref[...] = vn

    spec = pl.BlockSpec((1, tr, cols), lambda l, i: (l, i, 0))
    return pl.pallas_call(
        body, name=name, grid=(layers, rows // tr),
        in_specs=[spec] * 4, out_specs=[spec] * 3,
        out_shape=[jax.ShapeDtypeStruct(w.shape, F32)] * 3,
        compiler_params=_params("parallel", "parallel"),
    )(w, g, m, v)


def add_cast(a, b, name, dtype=BF16):
    nslab, rows, cols = a.shape
    tr = _ew_rows(rows, cols, 3)

    def body(a_ref, b_ref, o_ref):
        o_ref[...] = (a_ref[...] + b_ref[...]).astype(dtype)

    spec = pl.BlockSpec((1, tr, cols), lambda k, i: (k, i, 0))
    return pl.pallas_call(
        body, name=name, grid=(nslab, rows // tr),
        in_specs=[spec, spec], out_specs=spec,
        out_shape=jax.ShapeDtypeStruct(a.shape, dtype),
        compiler_params=_params("parallel", "parallel"),
    )(a, b)


def pair_sum(t, got, core, name):
    nslab, h, cols = got.shape
    tr = _ew_rows(h, cols, 3)
    nb = h // tr

    def body(c_ref, a_ref, b_ref, o_ref):
        o_ref[...] = (a_ref[...] + b_ref[...]).astype(BF16)

    spec = pl.BlockSpec((1, tr, cols), lambda k, i, c: (k, i, 0))
    return pl.pallas_call(
        body, name=name,
        grid_spec=pltpu.PrefetchScalarGridSpec(
            num_scalar_prefetch=1, grid=(nslab, nb),
            in_specs=[pl.BlockSpec((1, tr, cols), lambda k, i, c: (k, c[0] * nb + i, 0)), spec],
            out_specs=spec),
        out_shape=jax.ShapeDtypeStruct(got.shape, BF16),
        compiler_params=_params("parallel", "parallel"),
    )(core, t, got)


def sum_parts(parts, name, first=None):
    npart, rows, cols = parts.shape
    tr = _ew_rows(rows, cols, npart + 2)

    def body(*refs):
        p_ref, o_ref = refs[-2], refs[-1]
        acc = p_ref[0].astype(F32) if first is None else refs[0][...].astype(F32) + p_ref[0].astype(F32)
        for j in range(1, npart):
            acc = acc + p_ref[j].astype(F32)
        o_ref[...] = acc

    row = pl.BlockSpec((tr, cols), lambda i: (i, 0))
    ins = [parts] if first is None else [first, parts]
    return pl.pallas_call(
        body, name=name, grid=(rows // tr,),
        in_specs=([] if first is None else [row]) + [pl.BlockSpec((npart, tr, cols), lambda i: (0, i, 0))],
        out_specs=row,
        out_shape=jax.ShapeDtypeStruct((rows, cols), F32),
        compiler_params=_params("parallel"),
    )(*ins)


ANY = pl.BlockSpec(memory_space=pl.ANY)
MESH = pl.DeviceIdType.MESH


def _me():
    return lax.axis_index("x"), lax.axis_index("y"), lax.axis_index("c")


def _flip(pos, rel):
    return tuple(1 - p if f else p for p, f in zip(pos, rel))


SIBLING = (0, 0, 1)
OTHER_CHIPS = ((1, 0, 0), (0, 1, 0), (1, 1, 0))


def _chip_of(pos, rel=(0, 0, 0)):
    px, py, _ = _flip(pos, rel)
    return 2 * px + py


def allgather_blocks(shards, name):
    nt = len(shards)
    hs = [s.shape[0] // 2 for s in shards]

    def body(*refs):
        ins, outs = refs[:nt], refs[nt:2 * nt]
        send_sems, recv_sems, loc_sems = refs[2 * nt:]
        pos = _me()
        x, y, c = pos

        def block_id(rel):
            px, py, pc = _flip(pos, rel)
            return 4 * px + 2 * py + pc

        def copy(t, k, block_rel, to_rel, src=None):
            dst = outs[t].at[block_id(block_rel)]
            return pltpu.make_async_remote_copy(
                src_ref=dst if src is None else src, dst_ref=dst,
                send_sem=send_sems.at[t * 7 + k], recv_sem=recv_sems.at[t * 7 + k],
                device_id=_flip(pos, to_rel), device_id_type=MESH)

        own = [ins[t].at[pl.ds(c * hs[t], hs[t])] for t in range(nt)]
        mine = [pltpu.make_async_copy(own[t], outs[t].at[block_id((0, 0, 0))], loc_sems.at[t]) for t in range(nt)]
        for cp in mine:
            cp.start()
        first = []
        for t in range(nt):
            first.append(copy(t, 0, (0, 0, 0), SIBLING, src=own[t]))
            first += [copy(t, 1 + j, (0, 0, 0), rel, src=own[t]) for j, rel in enumerate(OTHER_CHIPS)]
        for cp in first:
            cp.start()
        passed = []
        for j, rel in enumerate(OTHER_CHIPS):
            for t in range(nt):
                copy(t, 1 + j, rel, (0, 0, 0)).wait_recv()
                fwd = copy(t, 4 + j, rel, SIBLING)
                fwd.start()
                passed.append(fwd)
        for t in range(nt):
            copy(t, 0, SIBLING, (0, 0, 0)).wait_recv()
            for j, rel in enumerate(OTHER_CHIPS):
                copy(t, 4 + j, (rel[0], rel[1], 1), (0, 0, 0)).wait_recv()
        for cp in first + passed:
            cp.wait_send()
        for cp in mine:
            cp.wait()

    return pl.pallas_call(
        body, name=name,
        in_specs=[ANY] * nt, out_specs=[ANY] * nt,
        out_shape=[jax.ShapeDtypeStruct((N_DEV, h, s.shape[1]), s.dtype) for h, s in zip(hs, shards)],
        scratch_shapes=[pltpu.SemaphoreType.DMA((7 * nt,)), pltpu.SemaphoreType.DMA((7 * nt,)),
                        pltpu.SemaphoreType.DMA((nt,))],
    )(*shards)


def _block_id(pos, rel=(0, 0, 0)):
    px, py, pc = _flip(pos, rel)
    return 4 * px + 2 * py + pc


def gather_first_hop(shards):
    hs = [s.shape[0] // 2 for s in shards]

    def plan(ins, outs, pos):
        me = _block_id(pos)
        remote = []
        for i, o, h in zip(ins, outs, hs):
            own = i.at[pl.ds(pos[2] * h, h)]
            remote += [(rel, own, o.at[me]) for rel in (SIBLING,) + OTHER_CHIPS]
        return remote

    return Comm(shards, [((N_DEV, h, s.shape[1]), s.dtype) for h, s in zip(hs, shards)], plan, 4 * len(shards))


def gather_second_hop(gathered):
    def plan(ins, outs, pos):
        remote = []
        for i, o in zip(ins, outs):
            for rel in OTHER_CHIPS:
                blk = _block_id(pos, rel)
                remote.append((SIBLING, i.at[blk], o.at[blk]))
        return remote

    return Comm(gathered, [(g.shape, g.dtype) for g in gathered], plan, 3 * len(gathered),
                aliases={i: i for i in range(len(gathered))})


def swap_comm(xs):
    def plan(ins, outs, pos):
        return [(SIBLING, i, o) for i, o in zip(ins, outs)]

    return Comm(list(xs), [(v.shape, v.dtype) for v in xs], plan, len(xs))


def give_half_comm(ts, plain=()):
    nt = len(ts)

    def plan(ins, outs, pos):
        remote = []
        for i, o in zip(ins[:nt], outs[:nt]):
            h = o.shape[1]
            remote.append((SIBLING, i.at[:, pl.ds((1 - pos[2]) * h, h)], o))
        return remote + [(SIBLING, i, o) for i, o in zip(ins[nt:], outs[nt:])]

    shapes = [((t.shape[0], t.shape[1] // 2, t.shape[2]), t.dtype) for t in ts] + [(v.shape, v.dtype) for v in plain]
    return Comm(list(ts) + list(plain), shapes, plan, nt + len(plain))


def chip_scatter_comm(xs, shared=None):
    nx = len(xs)

    def plan(ins, outs, pos):
        me = _chip_of(pos)
        remote = []
        for i, o in zip(ins[:nx], outs[:nx]):
            remote += [(rel, i.at[_chip_of(pos, rel)], o.at[j]) for j, rel in enumerate(OTHER_CHIPS)]
        if shared is not None:
            remote += [(rel, ins[nx], outs[nx].at[me]) for rel in OTHER_CHIPS]
        return remote

    shapes = [((3,) + v.shape[1:], v.dtype) for v in xs]
    if shared is not None:
        shapes.append(((N_CHIPS,) + shared.shape, shared.dtype))
    return Comm(list(xs) + ([] if shared is None else [shared]), shapes, plan, 3 * nx + (0 if shared is None else 3))


def run_comm(comm, name):
    return _pcall(lambda: None, name, (1,), [], [], [], [], (), ("arbitrary",), comm)[1]


PACK_ROWS = 256


def _pack(arrs):
    parts, layout = [], []
    row = 0
    for a in arrs:
        flat = a.reshape(-1).astype(F32)
        size = flat.shape[0]
        rows = -(-size // (8 * LANES)) * 8
        flat = jnp.pad(flat, (0, rows * LANES - size))
        parts.append(flat.reshape(rows, LANES))
        layout.append((row, rows, size, a.shape))
        row += rows
    if row % PACK_ROWS:
        parts.append(jnp.zeros((PACK_ROWS - row % PACK_ROWS, LANES), F32))
    return jnp.concatenate(parts, axis=0), layout


def _unpack(packed, layout):
    return [packed[r0:r0 + rows].reshape(-1)[:size].reshape(shape) for r0, rows, size, shape in layout]


SMALL_REPL = ['mix_norm', 'a_b_in', 'a_sinks', 'a_conv_b', 'a_cln_g', 'a_cln_b', 'c_w_pool', 'c_w_s', 'c_b_s',
              'ffn_norm', 'final_norm']
SMALL_SHARD = ['a_conv_w', 'c_pool_scale', 'c_sln_g', 'c_sln_b']
BIG = ['a_w_in', 'a_w_out', 'c_w_in', 'c_w_out', 'ffn_w_gate', 'ffn_w_up', 'ffn_w_down']
TRANSPOSED = ('a_w_in', 'ffn_w_gate', 'ffn_w_up')
BIG_COL_SHARDED = {'c_w_in'}


def _full_weight(name, g8):
    _, h, cols = g8.shape
    g4 = g8.reshape(N_CHIPS, 2 * h, cols)
    if name not in BIG_COL_SHARDED:
        return g4.reshape(-1, cols)
    return jnp.transpose(g4, (1, 0, 2)).reshape(2 * h, N_CHIPS * cols)


def _to_shard_major(name, f):
    if name not in BIG_COL_SHARDED:
        return f.reshape(N_CHIPS, f.shape[0] // N_CHIPS, f.shape[1])
    r, cfull = f.shape
    return jnp.transpose(f.reshape(r, N_CHIPS, cfull // N_CHIPS), (1, 0, 2))


def kernel(*args):
    a = dict(zip(IN_NAMES, args))
    bl, seq, _ = a['x'].shape
    n = bl * seq
    x = a['x'].reshape(n, D_MODEL)
    target = a['loss_target'].reshape(n, D_MODEL)
    xi, yi, ci = _me()
    chip = 2 * xi + yi

    shard = {'a_w_in': a['a_w_in'][0].T, 'a_w_out': a['a_w_out'][0], 'c_w_in': a['c_w_in'][0], 'c_w_out': a['c_w_out'][0]}
    for layer in range(2):
        shard['gate' + str(layer)] = a['ffn_w_gate'][layer].T
        shard['up' + str(layer)] = a['ffn_w_up'][layer].T
        shard['down' + str(layer)] = a['ffn_w_down'][layer]
    shard = {k: v.astype(BF16) for k, v in shard.items()}
    core = ci.astype(jnp.int32).reshape(1)
    block_id = 4 * xi + 2 * yi + ci

    def first_hop(*names):
        return gather_first_hop([shard[k] for k in names])

    def finish(name, g8):
        h = shard[name].shape[0] // 2
        own = lax.dynamic_slice_in_dim(shard[name], ci * h, h, axis=0)
        return _full_weight(name, lax.dynamic_update_slice_in_dim(g8, own[None], block_id, axis=0))

    a_w_in_t = _full_weight('a_w_in', allgather_blocks([shard['a_w_in']], "gather_a_w_in")[0])
    in0_width = a_w_in_t.shape[0]
    small_shard_pack, small_shard_layout = _pack([a[k] for k in SMALL_SHARD])
    hop_a = first_hop('a_w_out', 'gate0')
    hop_s = chip_scatter_comm([], shared=small_shard_pack)
    mix_norm, ffn_norm = a['mix_norm'], a['ffn_norm']
    (hn0, q, kv, cc), outs = norm_inproj(
        x, mix_norm[0:1], a_w_in_t, a['a_b_in'],
        [(0, ATTN_WIDTH), (ATTN_WIDTH, ATTN_WIDTH + 2 * KV_WIDTH), (ATTN_WIDTH + 2 * KV_WIDTH, in0_width)],
        [BF16, BF16, F32], "in_proj0", comm=hop_a + hop_s, w_transposed=True)
    got_a, (ss,) = hop_a.split(outs, hop_s)
    ss = lax.dynamic_update_slice_in_dim(ss, small_shard_pack[None], chip, axis=0)
    ss_full = []
    for r0, rows, size, shape in small_shard_layout:
        per_chip = ss[:, r0:r0 + rows].reshape(N_CHIPS, -1)[:, :size].reshape((N_CHIPS,) + shape)
        ss_full.append(jnp.concatenate([per_chip[k] for k in range(N_CHIPS)], axis=-1))
    a_conv_w, c_pool_scale, c_sln_g, c_sln_b = [v[0] for v in ss_full]

    conv_taps = jnp.pad(a_conv_w, ((0, 32 - CONV_KERNEL), (0, 0)))
    conv_vec = jnp.pad(jnp.stack([a['a_conv_b'][0], a['a_cln_g'][0], a['a_cln_b'][0]]), ((0, 5), (0, 0)))
    sinks_b = jnp.pad(jnp.repeat(a['a_sinks'][0].reshape(N_KV_HEADS, GROUP), ATTN_BLOCK, axis=1), ((0, 6), (0, 0)))
    w_pool_bf = a['c_w_pool'][0].astype(BF16)
    pool_scale = c_pool_scale.reshape(1, POOL_WIDTH)
    w_s = a['c_w_s'][0]
    b_s_t = a['c_b_s'][0].T
    sgu_ln = jnp.pad(jnp.stack([c_sln_g, c_sln_b]), ((0, 6), (0, 0)))
    final_norm = a['final_norm'].reshape(1, D_MODEL)

    hop_b, pass_a = first_hop('up0', 'down0'), gather_second_hop(got_a)
    attn, outs = attn_fwd(q, kv, sinks_b, bl, seq, "attn_fwd", comm=hop_b + pass_a)
    got_b, done = hop_b.split(outs, pass_a)
    a_w_out, wg0 = finish('a_w_out', done[0]), finish('gate0', done[1])

    hop_c, pass_b = first_hop('c_w_in', 'c_w_out', 'gate1'), gather_second_hop(got_b)
    (conv, conv_h1), outs = conv_fwd(cc, conv_taps, conv_vec, bl, seq, "conv_fwd", comm=hop_c + pass_b)
    got_c, done = hop_c.split(outs, pass_b)
    wu0, wd0 = finish('up0', done[0]), finish('down0', done[1])

    h1, got_d = out_proj(x, attn, conv, a_w_out, "out_proj0", comm=first_hop('up1'))

    hop_e, pass_c = first_hop('down1'), gather_second_hop(got_c + got_d)
    (hnf0, g0, u0), outs = ffn_gate_up(h1, ffn_norm[0:1], wg0, wu0, "ffn_gate_up0", comm=hop_e + pass_c)
    got_e, done = hop_e.split(outs, pass_c)
    c_w_in, c_w_out = finish('c_w_in', done[0]), finish('c_w_out', done[1])
    wg1, wu1 = finish('gate1', done[2]), finish('up1', done[3])

    h2, done = ffn_down(h1, g0, u0, wd0, "ffn_down0", comm=gather_second_hop(got_e))
    wd1 = finish('down1', done[0])
    wg, wu, wd = [wg0, wg1], [wu0, wu1], [wd0, wd1]

    (hn1, zp, zs), _ = norm_inproj(
        h2, mix_norm[1:2], c_w_in, jnp.zeros((1, c_w_in.shape[1]), F32),
        [(0, POOL_WIDTH), (POOL_WIDTH, c_w_in.shape[1])], [F32, F32], "in_proj1")
    pool = pool_fwd(zp, w_pool_bf, pool_scale, bl, seq, "pool_fwd")
    sgu = sgu_fwd(zs, w_s, b_s_t, sgu_ln, bl, seq, "sgu_fwd")
    h3, _ = out_proj(h2, pool, sgu, c_w_out, "out_proj1")
    (hnf1, g1, u1), _ = ffn_gate_up(h3, ffn_norm[1:2], wg1, wu1, "ffn_gate_up1")
    h4, _ = ffn_down(h3, g1, u1, wd1, "ffn_down1")

    dh4, d_final_norm, loss_local = loss_head(h4, final_norm, target, "loss_head")

    grads = {}
    pieces = {}

    def slabs_of(names, fulls):
        return [_to_shard_major(k, fulls[k]) for k in names]

    def pair_sums_of(names, slabs, gots):
        return [pair_sum(t, gt, core, "pair_sum_" + k) for k, t, gt in zip(names, slabs, gots)]

    def chip_sums_of(names, sums, from_chips):
        own = [lax.dynamic_index_in_dim(p, chip, axis=0, keepdims=False) for p in sums]
        return [sum_parts(p, "chip_sum_" + k, first=o) for k, p, o in zip(names, from_chips, own)]

    (dg, du, act), _ = ffn_down_bwd(dh4, g1, u1, wd[1], "ffn_down_bwd1")
    full1 = {'down1': mm_tn(act, dh4, "dw_down1"), 'gate1': mm_tn(dg, hnf1, "dw_gate1"),
             'up1': mm_tn(du, hnf1, "dw_up1")}
    names1 = ['gate1', 'up1', 'down1']
    slabs1 = slabs_of(names1, full1)
    dh3, d_ffn_norm1, got1 = proj_rms_bwd([dg, du], [wg[1], wu[1]], h3, ffn_norm[1:2], dh4, 1, "ffn_up_bwd1",
                                          tm_pref=256, w_transposed=True, comm=give_half_comm(slabs1))
    sums1 = pair_sums_of(names1, slabs1, got1)
    d_pool, d_sgu = out_proj_bwd(dh3, c_w_out, [F32, F32], "out_proj_bwd1")
    full1['c_w_out'] = jnp.concatenate([mm_tn(pool, dh3, "dw_out1_pool"), mm_tn(sgu, dh3, "dw_out1_sgu")], axis=0)
    (dzp, d_w_pool, d_pool_scale), from_gate = pool_bwd(zp, w_pool_bf, pool_scale, d_pool, bl, seq, "pool_bwd",
                                                        comm=chip_scatter_comm(sums1[0:1]))
    (dzs, d_w_s, d_b_s_b, d_sgu_ln), from_up = sgu_bwd(zs, w_s, b_s_t, sgu_ln, d_sgu, bl, seq, "sgu_bwd",
                                                       comm=chip_scatter_comm(sums1[1:2]))
    full1['c_w_in'] = jnp.concatenate([mm_tn(hn1, dzp, "dw_in1_pool"), mm_tn(hn1, dzs, "dw_in1_sgu")], axis=1)
    names1b = ['c_w_out', 'c_w_in']
    slabs1b = slabs_of(names1b, full1)
    chips_down, pair1b = chip_scatter_comm(sums1[2:3]), give_half_comm(slabs1b)
    dh2, d_mix_norm1, outs = proj_rms_bwd([dzp, dzs], [c_w_in[:, :POOL_WIDTH], c_w_in[:, POOL_WIDTH:]], h2,
                                          mix_norm[1:2], dh3, 1, "in_proj_bwd1", comm=chips_down + pair1b)
    from_down, got1b = chips_down.split(outs, pair1b)
    mine1 = chip_sums_of(names1, sums1, from_gate + from_up + from_down)
    sums1b = pair_sums_of(names1b, slabs1b, got1b)

    join1, chips1b = swap_comm(mine1), chip_scatter_comm(sums1b)
    (dg, du, act), outs = ffn_down_bwd(dh2, g0, u0, wd[0], "ffn_down_bwd0", comm=join1 + chips1b)
    theirs1, from_chips1b = join1.split(outs, chips1b)
    pieces.update({k: (m, t) for k, m, t in zip(names1, mine1, theirs1)})
    mine1b = chip_sums_of(names1b, sums1b, from_chips1b)
    full0 = {'down0': mm_tn(act, dh2, "dw_down0"), 'gate0': mm_tn(dg, hnf0, "dw_gate0"),
             'up0': mm_tn(du, hnf0, "dw_up0")}
    names0 = ['gate0', 'up0', 'down0']
    slabs0 = slabs_of(names0, full0)
    join1b, pair0 = swap_comm(mine1b), give_half_comm(slabs0)
    dh1, d_ffn_norm0, outs = proj_rms_bwd([dg, du], [wg[0], wu[0]], h1, ffn_norm[0:1], dh2, 1, "ffn_up_bwd0",
                                          tm_pref=256, comm=join1b + pair0, w_transposed=True)
    theirs1b, got0 = join1b.split(outs, pair0)
    pieces.update({k: (m, t) for k, m, t in zip(names1b, mine1b, theirs1b)})
    sums0 = pair_sums_of(names0, slabs0, got0)

    d_attn, d_conv = out_proj_bwd(dh1, a_w_out, [BF16, F32], "out_proj_bwd0")
    full_o = {'a_w_out': jnp.concatenate([mm_tn(attn, dh1, "dw_out0_attn"), mm_tn(conv, dh1, "dw_out0_conv")], axis=0)}
    slabs_o = slabs_of(['a_w_out'], full_o)
    chips0, pair_o = chip_scatter_comm(sums0), give_half_comm(slabs_o)
    (dq, dkv, d_sinks_b), outs = attn_bwd(q, kv, sinks_b, d_attn, bl, seq, "attn_bwd", comm=chips0 + pair_o)
    from_chips0, got_o = chips0.split(outs, pair_o)
    mine0 = chip_sums_of(names0, sums0, from_chips0)
    sums_o = pair_sums_of(['a_w_out'], slabs_o, got_o)
    join0, chips_o = swap_comm(mine0), chip_scatter_comm(sums_o)
    (dcc, d_conv_taps, d_conv_vec), outs = conv_bwd(cc, conv_h1, conv_taps, conv_vec, d_conv, bl, seq, "conv_bwd",
                                                    comm=join0 + chips_o)
    theirs0, from_chips_o = join0.split(outs, chips_o)
    pieces.update({k: (m, t) for k, m, t in zip(names0, mine0, theirs0)})
    mine_o = chip_sums_of(['a_w_out'], sums_o, from_chips_o)
    kq, kk = ATTN_WIDTH, ATTN_WIDTH + 2 * KV_WIDTH
    grad_x, d_mix_norm0, _ = proj_rms_bwd([dq, dkv, dcc], [a_w_in_t[:kq], a_w_in_t[kq:kk], a_w_in_t[kk:]], x,
                                          mix_norm[0:1], dh1, 1, "in_proj_bwd0", w_transposed=True)
    dw_q, db_q = mm_tn(dq, hn0, "dw_in0_q", xsum=True)
    dw_kv, db_kv = mm_tn(dkv, hn0, "dw_in0_kv", xsum=True)
    (dw_c, db_c), theirs_o = mm_tn(dcc, hn0, "dw_in0_c", xsum=True, comm=swap_comm(mine_o))
    pieces['a_w_out'] = (mine_o[0], theirs_o[0])
    d_a_b_in = jnp.concatenate([db_q, db_kv, db_c], axis=0)
    slabs_i = slabs_of(['a_w_in'], {'a_w_in': jnp.concatenate([dw_q, dw_kv, dw_c], axis=0)})

    small_full = {
        'mix_norm': jnp.stack([d_mix_norm0, d_mix_norm1]), 'a_b_in': d_a_b_in[None], 'a_sinks': d_sinks_b[:, 0][None],
        'a_conv_w': d_conv_taps[:CONV_KERNEL][None], 'a_conv_b': d_conv_vec[0][None], 'a_cln_g': d_conv_vec[1][None],
        'a_cln_b': d_conv_vec[2][None], 'c_w_pool': d_w_pool[None], 'c_pool_scale': d_pool_scale[0][None],
        'c_sln_g': d_sgu_ln[0][None], 'c_sln_b': d_sgu_ln[1][None], 'c_w_s': d_w_s[None],
        'c_b_s': d_b_s_b[:, :, 0][None], 'ffn_norm': jnp.stack([d_ffn_norm0, d_ffn_norm1]),
        'final_norm': d_final_norm, 'loss': loss_local.reshape(1)}
    small_names = SMALL_REPL + SMALL_SHARD
    small_pack, small_layout = _pack([small_full[k] for k in small_names + ['loss']])

    got_i, got_s = run_comm(give_half_comm(slabs_i, plain=[small_pack]), "tail_pair")
    sums_i = pair_sums_of(['a_w_in'], slabs_i, [got_i])
    small_pair = add_cast(small_pack[None], got_s[None], "pair_sum_small", dtype=F32)[0]
    outs = run_comm(chip_scatter_comm(sums_i, shared=small_pair), "tail_chips")
    mine_i = chip_sums_of(['a_w_in'], sums_i, outs[:1])
    small_chips = lax.dynamic_update_slice_in_dim(outs[1], small_pair[None], chip, axis=0)
    theirs_i = run_comm(swap_comm(mine_i), "tail_join")
    pieces['a_w_in'] = (mine_i[0], theirs_i[0])

    def whole(name):
        mine, theirs = pieces[name]
        return jnp.concatenate([jnp.where(ci == 0, mine, theirs), jnp.where(ci == 0, theirs, mine)], axis=0)

    for k in ('a_w_in', 'a_w_out', 'c_w_in', 'c_w_out'):
        grads[k] = whole(k)[None]
    for short, key in (('gate', 'ffn_w_gate'), ('up', 'ffn_w_up'), ('down', 'ffn_w_down')):
        grads[key] = jnp.stack([whole(short + '0'), whole(short + '1')])

    small_sum = sum_parts(small_chips, "small_sum")
    for k, g in zip(small_names + ['loss'], _unpack(small_sum, small_layout)):
        if k in SMALL_SHARD:
            width = a[k].shape[-1]
            g = lax.dynamic_slice_in_dim(g, chip * width, width, axis=g.ndim - 1)
        grads[k] = g
    loss = grads.pop('loss')[0]

    delta, new_m, new_v = {}, {}, {}
    for k in BIG:
        if k in TRANSPOSED:
            flip = lambda t: jnp.swapaxes(t, 1, 2)
            d, m, v = adamw(flip(a[k]), grads[k], flip(a['m_' + k]), flip(a['v_' + k]), "adamw_" + k)
            grads[k], delta[k], new_m[k], new_v[k] = flip(grads[k]), flip(d), flip(m), flip(v)
        else:
            delta[k], new_m[k], new_v[k] = adamw(a[k], grads[k], a['m_' + k], a['v_' + k], "adamw_" + k)
    packs = [_pack([src[k] for k in small_names])
             for src in (a, grads, {k: a['m_' + k] for k in small_names}, {k: a['v_' + k] for k in small_names})]
    d, m, v = adamw(packs[0][0][None], packs[1][0][None], packs[2][0][None], packs[3][0][None], "adamw_small")
    d, m, v = d[0], m[0], v[0]
    lay = packs[0][1]
    for k, dv, mv, vv in zip(small_names, _unpack(d, lay), _unpack(m, lay), _unpack(v, lay)):
        delta[k], new_m[k], new_v[k] = dv, mv, vv

    return (loss, grad_x.reshape(a['x'].shape), *[grads[k] for k in WEIGHTS], *[delta[k] for k in WEIGHTS],
            *[new_m[k] for k in WEIGHTS], *[new_v[k] for k in WEIGHTS])
```

```python
import functools

import jax
import jax.numpy as jnp
from jax import lax
from jax.experimental import pallas as pl
from jax.experimental.pallas import tpu as pltpu

F32 = jnp.float32
BF16 = jnp.bfloat16

D_MODEL = 1024
EPS = 1e-5
N_Q_HEADS, N_KV_HEADS, HEAD_DIM = 8, 2, 64
ATTN_BLOCK = 128
ATTN_WIDTH = N_Q_HEADS * HEAD_DIM
KV_WIDTH = N_KV_HEADS * HEAD_DIM
CONV_WIDTH = 512
CONV_KERNEL = 31
CONV_HALO = 32
POOL_WINDOWS = (2, 4, 8, 16)
POOL_WIDTH = 512
POOL_HALO = 16
SGU_WIDTH = 512
SGU_CHUNK = 128
D_FF = 2816
FF_CHUNK = 128
MXU_COLS = 256
LANES = 128
N_CHIPS = 4
N_DEV = 8

ADAM_LR, ADAM_B1, ADAM_B2, ADAM_EPS, ADAM_WD, ADAM_STEP = 0.001, 0.9, 0.999, 1e-08, 0.01, 10

VMEM_LIMIT = 56 * 2**20

WEIGHTS = ['mix_norm', 'a_w_in', 'a_b_in', 'a_sinks', 'a_conv_w', 'a_conv_b', 'a_cln_g', 'a_cln_b', 'a_w_out',
           'c_w_in', 'c_w_pool', 'c_pool_scale', 'c_sln_g', 'c_sln_b', 'c_w_s', 'c_b_s', 'c_w_out',
           'ffn_norm', 'ffn_w_gate', 'ffn_w_up', 'ffn_w_down', 'final_norm']
IN_NAMES = (['x'] + WEIGHTS + ['loss_target'] + ['m_' + n for n in WEIGHTS] + ['v_' + n for n in WEIGHTS])


def _params(*sem):
    return pltpu.CompilerParams(dimension_semantics=sem, vmem_limit_bytes=VMEM_LIMIT)


def _dot(a, b):
    return jnp.dot(a, b, preferred_element_type=F32)


def _dot_nt(a, b):
    return lax.dot_general(a, b, (((1,), (1,)), ((), ())), preferred_element_type=F32)


def _dot_tn(a, b):
    return lax.dot_general(a, b, (((0,), (0,)), ((), ())), preferred_element_type=F32)


def _sigmoid(v):
    return 0.5 * jnp.tanh(0.5 * v) + 0.5


def _row_tile(n, pref):
    t = min(n, pref)
    while n % t:
        t //= 2
    return t


def _col_tile(m, rows, budget=6 * 2**20):
    best = LANES
    for t in range(LANES, m + 1, LANES):
        if m % t == 0 and rows * t * 4 <= budget:
            best = t
    return best


class Comm:
    def __init__(self, ins, out_shapes, plan, count, aliases=None):
        self.ins, self.out_shapes, self.plan, self.count, self.aliases = ins, out_shapes, plan, count, aliases or {}

    def __add__(self, other):
        ni, no = len(self.ins), len(self.out_shapes)

        def plan(ins, outs, pos):
            return self.plan(ins[:ni], outs[:no], pos) + other.plan(ins[ni:], outs[no:], pos)

        aliases = dict(self.aliases)
        aliases.update({ni + i: no + o for i, o in other.aliases.items()})
        return Comm(list(self.ins) + list(other.ins), list(self.out_shapes) + list(other.out_shapes), plan,
                    self.count + other.count, aliases)

    def split(self, outs, other):
        return outs[:len(self.out_shapes)], outs[len(self.out_shapes):]


def _pcall(body, name, grid, in_specs, out_specs, out_shape, scratch_shapes, args, sem, comm=None):
    single = not isinstance(out_shape, (list, tuple))
    if single:
        out_specs, out_shape = [out_specs], [out_shape]
    if comm is None:
        res = pl.pallas_call(body, name=name, grid=grid, in_specs=in_specs, out_specs=list(out_specs),
                             out_shape=list(out_shape), scratch_shapes=list(scratch_shapes),
                             compiler_params=_params(*sem))(*args)
        return (res[0] if single else res), []
    na, nci, no, nco, ns = len(args), len(comm.ins), len(out_shape), len(comm.out_shapes), len(scratch_shapes)

    def wrapped(*refs):
        a_refs, ci_refs = refs[:na], refs[na:na + nci]
        o_refs, co_refs = refs[na + nci:na + nci + no], refs[na + nci + no:na + nci + no + nco]
        s_refs = refs[na + nci + no + nco:na + nci + no + nco + ns]
        send_sems, recv_sems = refs[-2], refs[-1]
        pos = _me()

        def copies():
            return [pltpu.make_async_remote_copy(src_ref=s, dst_ref=d, send_sem=send_sems.at[i],
                                                 recv_sem=recv_sems.at[i], device_id=_flip(pos, rel),
                                                 device_id_type=MESH)
                    for i, (rel, s, d) in enumerate(comm.plan(ci_refs, co_refs, pos))]

        first, last = None, None
        for d, size in enumerate(grid):
            f, l = pl.program_id(d) == 0, pl.program_id(d) == size - 1
            first = f if first is None else first & f
            last = l if last is None else last & l

        @pl.when(first)
        def _():
            for cp in copies():
                cp.start()

        body(*a_refs, *o_refs, *s_refs)

        @pl.when(last)
        def _():
            for cp in copies():
                cp.wait()

    res = pl.pallas_call(
        wrapped, name=name, grid=grid,
        in_specs=list(in_specs) + [ANY] * nci, out_specs=list(out_specs) + [ANY] * nco,
        out_shape=list(out_shape) + [jax.ShapeDtypeStruct(s, d) for s, d in comm.out_shapes],
        scratch_shapes=list(scratch_shapes) + [pltpu.SemaphoreType.DMA((comm.count,)),
                                               pltpu.SemaphoreType.DMA((comm.count,))],
        input_output_aliases={na + i: no + o for i, o in comm.aliases.items()},
        compiler_params=_params(*(["arbitrary"] * len(grid))),
    )(*args, *comm.ins)
    outs = res[:no]
    return (outs[0] if single else outs), list(res[no:])


def norm_inproj(x, gain, w, bias, splits, dtypes, name, comm=None, w_transposed=False):
    n = x.shape[0]
    m = w.shape[0] if w_transposed else w.shape[1]
    tm = _row_tile(n, 512)

    def body(x_ref, g_ref, w_ref, b_ref, hn_ref, *outs):
        xv = x_ref[...]
        r = lax.rsqrt(jnp.mean(xv * xv, axis=-1, keepdims=True) + EPS)
        hn = ((xv * r) * g_ref[...]).astype(BF16)
        hn_ref[...] = hn
        z = (_dot_nt if w_transposed else _dot)(hn, w_ref[...]) + b_ref[...]
        for o, (lo, hi) in zip(outs, splits):
            o[...] = z[:, lo:hi].astype(o.dtype)

    out_shape = [jax.ShapeDtypeStruct((n, D_MODEL), BF16)]
    out_specs = [pl.BlockSpec((tm, D_MODEL), lambda i: (i, 0))]
    for (lo, hi), dt in zip(splits, dtypes):
        out_shape.append(jax.ShapeDtypeStruct((n, hi - lo), dt))
        out_specs.append(pl.BlockSpec((tm, hi - lo), lambda i: (i, 0)))
    return _pcall(
        body, name, (n // tm,),
        [pl.BlockSpec((tm, D_MODEL), lambda i: (i, 0)),
         pl.BlockSpec((1, D_MODEL), lambda i: (0, 0)),
         pl.BlockSpec(w.shape, lambda i: (0, 0)),
         pl.BlockSpec((1, m), lambda i: (0, 0))],
        out_specs, out_shape, [], (x, gain, w, bias), ("parallel",), comm)


def out_proj(res, m1, m2, w, name, comm=None):
    n = res.shape[0]
    k1, k2 = m1.shape[1], m2.shape[1]
    assert k1 == k2
    tm = _row_tile(n, 512)

    def body(r_ref, a_ref, b_ref, w1_ref, w2_ref, o_ref):
        o_ref[...] = r_ref[...] + _dot(a_ref[...], w1_ref[...]) + _dot(b_ref[...], w2_ref[...])

    return _pcall(
        body, name, (n // tm,),
        [pl.BlockSpec((tm, D_MODEL), lambda i: (i, 0)),
         pl.BlockSpec((tm, k1), lambda i: (i, 0)),
         pl.BlockSpec((tm, k2), lambda i: (i, 0)),
         pl.BlockSpec((k1, D_MODEL), lambda i: (0, 0)),
         pl.BlockSpec((k2, D_MODEL), lambda i: (1, 0))],
        pl.BlockSpec((tm, D_MODEL), lambda i: (i, 0)),
        jax.ShapeDtypeStruct((n, D_MODEL), F32), [], (res, m1, m2, w, w), ("parallel",), comm)


def ffn_gate_up(h, gain, wg_t, wu_t, name, comm=None):
    n = h.shape[0]
    tm = _row_tile(n, 1024)
    th = D_FF // 2

    def body(h_ref, g_ref, wg_ref, wu_ref, hn_ref, go_ref, uo_ref):
        @pl.when(pl.program_id(1) == 0)
        def _():
            xv = h_ref[...]
            r = lax.rsqrt(jnp.mean(xv * xv, axis=-1, keepdims=True) + EPS)
            hn_ref[...] = ((xv * r) * g_ref[...]).astype(BF16)

        hn = hn_ref[...]
        go_ref[...] = _dot_nt(hn, wg_ref[...]).astype(BF16)
        uo_ref[...] = _dot_nt(hn, wu_ref[...]).astype(BF16)

    return _pcall(
        body, name, (n // tm, D_FF // th),
        [pl.BlockSpec((tm, D_MODEL), lambda i, j: (i, 0)),
         pl.BlockSpec((1, D_MODEL), lambda i, j: (0, 0)),
         pl.BlockSpec((th, D_MODEL), lambda i, j: (j, 0)),
         pl.BlockSpec((th, D_MODEL), lambda i, j: (j, 0))],
        [pl.BlockSpec((tm, D_MODEL), lambda i, j: (i, 0)),
         pl.BlockSpec((tm, th), lambda i, j: (i, j)),
         pl.BlockSpec((tm, th), lambda i, j: (i, j))],
        [jax.ShapeDtypeStruct((n, D_MODEL), BF16),
         jax.ShapeDtypeStruct((n, D_FF), BF16),
         jax.ShapeDtypeStruct((n, D_FF), BF16)],
        [], (h, gain, wg_t, wu_t), ("parallel", "arbitrary"), comm)


def ffn_down(h, g, u, wd, name, comm=None):
    n = h.shape[0]
    tm = _row_tile(n, 512)

    def body(h_ref, g_ref, u_ref, w_ref, o_ref, a_ref):
        for c0 in range(0, D_FF, FF_CHUNK):
            gv = g_ref[:, c0:c0 + FF_CHUNK]
            a_ref[:, c0:c0 + FF_CHUNK] = gv * _sigmoid(gv) * u_ref[:, c0:c0 + FF_CHUNK]
        o_ref[...] = h_ref[...] + _dot(a_ref[...], w_ref[...])

    return _pcall(
        body, name, (n // tm,),
        [pl.BlockSpec((tm, D_MODEL), lambda i: (i, 0)),
         pl.BlockSpec((tm, D_FF), lambda i: (i, 0)),
         pl.BlockSpec((tm, D_FF), lambda i: (i, 0)),
         pl.BlockSpec((D_FF, D_MODEL), lambda i: (0, 0))],
        pl.BlockSpec((tm, D_MODEL), lambda i: (i, 0)),
        jax.ShapeDtypeStruct((n, D_MODEL), F32),
        [pltpu.VMEM((tm, D_FF), BF16)], (h, g, u, wd), ("parallel",), comm)


def ffn_down_bwd(dh, g, u, wd, name, comm=None):
    n = dh.shape[0]
    tm = _row_tile(n, 512)

    def body(dh_ref, g_ref, u_ref, w_ref, dg_ref, du_ref, a_ref):
        dhb = dh_ref[...].astype(BF16)
        for c0 in range(0, D_FF, MXU_COLS):
            cols = slice(c0, c0 + MXU_COLS)
            da = _dot_nt(dhb, w_ref[cols, :]).astype(BF16)
            gv, uv = g_ref[:, cols], u_ref[:, cols]
            sg = _sigmoid(gv)
            act = gv * sg
            dg_ref[:, cols] = (da * uv) * (sg + act * (1.0 - sg))
            du_ref[:, cols] = da * act
            a_ref[:, cols] = act * uv

    spec_h = pl.BlockSpec((tm, D_FF), lambda i: (i, 0))
    return _pcall(
        body, name, (n // tm,),
        [pl.BlockSpec((tm, D_MODEL), lambda i: (i, 0)), spec_h, spec_h,
         pl.BlockSpec((D_FF, D_MODEL), lambda i: (0, 0))],
        [spec_h, spec_h, spec_h], [jax.ShapeDtypeStruct((n, D_FF), BF16)] * 3,
        [], (dh, g, u, wd), ("parallel",), comm)


def mm_tn(x, dy, name, xsum=False, comm=None):
    n, k = x.shape
    m = dy.shape[1]
    tk = _col_tile(k, m)
    tt = _row_tile(n, 1024)

    def body(x_ref, dy_ref, o_ref, *rest):
        xt_ref = rest[-1]
        t = pl.program_id(1)
        xv = x_ref[...]
        xt_ref[...] = xv.astype(BF16).T
        part = _dot(xt_ref[...], dy_ref[...].astype(BF16))

        @pl.when(t == 0)
        def _():
            o_ref[...] = part

        @pl.when(t > 0)
        def _():
            o_ref[...] += part

        if xsum:
            cs = jnp.broadcast_to(jnp.sum(xv.astype(F32), axis=0, keepdims=True), rest[0].shape)

            @pl.when(t == 0)
            def _():
                rest[0][...] = cs

            @pl.when(t > 0)
            def _():
                rest[0][...] += cs

    out_shape = [jax.ShapeDtypeStruct((k, m), F32)]
    out_specs = [pl.BlockSpec((tk, m), lambda j, t: (j, 0))]
    if xsum:
        out_shape.append(jax.ShapeDtypeStruct((8, k), F32))
        out_specs.append(pl.BlockSpec((8, tk), lambda j, t: (0, j)))
    res, comm_outs = _pcall(
        body, name, (k // tk, n // tt),
        [pl.BlockSpec((tt, tk), lambda j, t: (t, j)),
         pl.BlockSpec((tt, m), lambda j, t: (t, 0))],
        out_specs, out_shape, [pltpu.VMEM((tk, tt), BF16)], (x, dy), ("arbitrary", "arbitrary"), comm)
    res = (res[0], res[1][0]) if xsum else res[0]
    return res if comm is None else (res, comm_outs)


def out_proj_bwd(dh, w, dtypes, name):
    n = dh.shape[0]
    k = w.shape[0]
    half = k // 2
    tm = _row_tile(n, 512)

    def body(dh_ref, w_ref, a_ref, b_ref):
        dm = _dot_nt(dh_ref[...].astype(BF16), w_ref[...])
        a_ref[...] = dm[:, :half].astype(a_ref.dtype)
        b_ref[...] = dm[:, half:].astype(b_ref.dtype)

    return pl.pallas_call(
        body, name=name, grid=(n // tm,),
        in_specs=[pl.BlockSpec((tm, D_MODEL), lambda i: (i, 0)),
                  pl.BlockSpec((k, D_MODEL), lambda i: (0, 0))],
        out_specs=[pl.BlockSpec((tm, half), lambda i: (i, 0))] * 2,
        out_shape=[jax.ShapeDtypeStruct((n, half), dtypes[0]), jax.ShapeDtypeStruct((n, half), dtypes[1])],
        compiler_params=_params("parallel"),
    )(dh, w)


def proj_rms_bwd(dys, ws, h_in, gain, dres, nk, name, tm_pref=512, comm=None, w_transposed=False):
    n = h_in.shape[0]
    npair = len(dys)
    tm = _row_tile(n, tm_pref)
    tks = [dy.shape[1] // nk for dy in dys]
    mm = _dot if w_transposed else _dot_nt

    def body(*refs):
        dy_refs = refs[:npair]
        w_refs = refs[npair:2 * npair]
        h_ref, g_ref, dr_ref, o_ref, dg_ref, acc_ref = refs[2 * npair:]
        i, k = pl.program_id(0), pl.program_id(1)
        part = mm(dy_refs[0][...], w_refs[0][...])
        for p in range(1, npair):
            part = part + mm(dy_refs[p][...], w_refs[p][...])

        @pl.when(k == 0)
        def _():
            acc_ref[...] = part

        @pl.when(k > 0)
        def _():
            acc_ref[...] += part

        @pl.when(k == nk - 1)
        def _():
            dhn = acc_ref[...]
            xv = h_ref[...]
            r = lax.rsqrt(jnp.mean(xv * xv, axis=-1, keepdims=True) + EPS)
            xh = xv * r
            uv = dhn * g_ref[...]
            o_ref[...] = dr_ref[...] + r * (uv - xh * jnp.mean(uv * xh, axis=-1, keepdims=True))
            dgp = jnp.broadcast_to(jnp.sum(dhn * xh, axis=0, keepdims=True), dg_ref.shape)

            @pl.when(i == 0)
            def _():
                dg_ref[...] = dgp

            @pl.when(i > 0)
            def _():
                dg_ref[...] += dgp

    row = pl.BlockSpec((tm, D_MODEL), lambda i, k: (i, 0))
    in_specs = [pl.BlockSpec((tm, tk), lambda i, k: (i, k)) for tk in tks]
    if w_transposed:
        in_specs += [pl.BlockSpec((tk, D_MODEL), lambda i, k: (k, 0)) for tk in tks]
    else:
        in_specs += [pl.BlockSpec((D_MODEL, tk), lambda i, k: (0, k)) for tk in tks]
    in_specs += [row, pl.BlockSpec((1, D_MODEL), lambda i, k: (0, 0)), row]
    (dh, dgain), comm_outs = _pcall(
        body, name, (n // tm, nk), in_specs,
        [row, pl.BlockSpec((8, D_MODEL), lambda i, k: (0, 0))],
        [jax.ShapeDtypeStruct((n, D_MODEL), F32), jax.ShapeDtypeStruct((8, D_MODEL), F32)],
        [pltpu.VMEM((tm, D_MODEL), F32)], (*dys, *ws, h_in, gain, dres), ("arbitrary", "arbitrary"), comm)
    return dh, dgain[0], comm_outs


def loss_head(h, gain, target, name):
    n = h.shape[0]
    tm = _row_tile(n, 512)

    def body(h_ref, g_ref, t_ref, dh_ref, dg_ref, l_ref):
        i = pl.program_id(0)
        xv = h_ref[...]
        r = lax.rsqrt(jnp.mean(xv * xv, axis=-1, keepdims=True) + EPS)
        xh = xv * r
        err = xh * g_ref[...] - t_ref[...]
        dy = err * (1.0 / D_MODEL)
        uv = dy * g_ref[...]
        dh_ref[...] = r * (uv - xh * jnp.mean(uv * xh, axis=-1, keepdims=True))
        dgp = jnp.broadcast_to(jnp.sum(dy * xh, axis=0, keepdims=True), dg_ref.shape)
        lp = jnp.sum(jnp.sum(err * err, axis=-1, keepdims=True), axis=0, keepdims=True) * (0.5 / D_MODEL)
        lp = jnp.broadcast_to(lp, l_ref.shape)

        @pl.when(i == 0)
        def _():
            dg_ref[...] = dgp
            l_ref[...] = lp

        @pl.when(i > 0)
        def _():
            dg_ref[...] += dgp
            l_ref[...] += lp

    row = pl.BlockSpec((tm, D_MODEL), lambda i: (i, 0))
    dh, dg, l = pl.pallas_call(
        body, name=name, grid=(n // tm,),
        in_specs=[row, pl.BlockSpec((1, D_MODEL), lambda i: (0, 0)), row],
        out_specs=[row, pl.BlockSpec((8, D_MODEL), lambda i: (0, 0)), pl.BlockSpec((8, LANES), lambda i: (0, 0))],
        out_shape=[jax.ShapeDtypeStruct((n, D_MODEL), F32), jax.ShapeDtypeStruct((8, D_MODEL), F32),
                   jax.ShapeDtypeStruct((8, LANES), F32)],
        compiler_params=_params("arbitrary"),
    )(h, gain, target)
    return dh, dg[0], l[0, 0]


GROUP = N_Q_HEADS // N_KV_HEADS
GQ = GROUP * ATTN_BLOCK


def _attn_mask_t(n):
    r = lax.broadcasted_iota(jnp.int32, (2 * ATTN_BLOCK, GQ), 0)
    qi = lax.broadcasted_iota(jnp.int32, (2 * ATTN_BLOCK, GQ), 1) & (ATTN_BLOCK - 1)
    band = (r > qi) & (r <= qi + ATTN_BLOCK)
    return band & ((r >= ATTN_BLOCK) | (n > 0))


def _stack_heads(blk, kh):
    return jnp.concatenate([blk[:, (kh * GROUP + g) * HEAD_DIM:(kh * GROUP + g + 1) * HEAD_DIM]
                            for g in range(GROUP)], axis=0)


def _attn_probs_t(kk, qs, mask, sink):
    s = _dot_nt(kk, qs) * (HEAD_DIM ** -0.5)
    s = jnp.where(mask, s, -1e30)
    m = jnp.maximum(jnp.max(s, axis=0, keepdims=True), sink)
    p = jnp.exp(s - m)
    esink = jnp.exp(sink - m)
    inv = 1.0 / (jnp.sum(p, axis=0, keepdims=True) + esink)
    return p * inv, esink * inv


def attn_fwd(q, kv, sinks_t, nseq, seq, name, comm=None):
    nb = seq // ATTN_BLOCK

    def body(q_ref, kv_ref, s_ref, o_ref, kvp):
        kvp[0:ATTN_BLOCK, :] = jnp.zeros((ATTN_BLOCK, 2 * KV_WIDTH), BF16)
        kvp[ATTN_BLOCK:, :] = kv_ref[...]

        def blk(n, carry):
            st = pl.multiple_of(n * ATTN_BLOCK, ATTN_BLOCK)
            qb = q_ref[pl.ds(st, ATTN_BLOCK), :]
            kw = kvp[pl.ds(st, 2 * ATTN_BLOCK), :]
            mask = _attn_mask_t(n)
            for kh in range(N_KV_HEADS):
                kk = kw[:, kh * HEAD_DIM:(kh + 1) * HEAD_DIM]
                vv = kw[:, KV_WIDTH + kh * HEAD_DIM:KV_WIDTH + (kh + 1) * HEAD_DIM]
                probs, _ = _attn_probs_t(kk, _stack_heads(qb, kh), mask, s_ref[kh:kh + 1, :])
                ot = _dot_tn(vv, probs.astype(BF16))
                for pair in range(GROUP // 2):
                    two = jnp.concatenate([ot[:, (2 * pair) * ATTN_BLOCK:(2 * pair + 1) * ATTN_BLOCK],
                                           ot[:, (2 * pair + 1) * ATTN_BLOCK:(2 * pair + 2) * ATTN_BLOCK]], axis=0)
                    col = (kh * GROUP + 2 * pair) * HEAD_DIM
                    o_ref[pl.ds(st, ATTN_BLOCK), col:col + 2 * HEAD_DIM] = two.T.astype(o_ref.dtype)
            return carry

        lax.fori_loop(0, nb, blk, 0, unroll=2)

    return _pcall(
        body, name, (nseq,),
        [pl.BlockSpec((seq, ATTN_WIDTH), lambda b: (b, 0)),
         pl.BlockSpec((seq, 2 * KV_WIDTH), lambda b: (b, 0)),
         pl.BlockSpec((8, GQ), lambda b: (0, 0))],
        pl.BlockSpec((seq, ATTN_WIDTH), lambda b: (b, 0)),
        jax.ShapeDtypeStruct((nseq * seq, ATTN_WIDTH), BF16),
        [pltpu.VMEM((ATTN_BLOCK + seq, 2 * KV_WIDTH), BF16)], (q, kv, sinks_t), ("parallel",), comm)


def attn_bwd(q, kv, sinks_t, do, nseq, seq, name, comm=None):
    nb = seq // ATTN_BLOCK

    def body(q_ref, kv_ref, s_ref, do_ref, dq_ref, dkv_ref, ds_ref, kvp, dkvp, dsacc):
        @pl.when(pl.program_id(0) == 0)
        def _():
            dsacc[...] = jnp.zeros(dsacc.shape, F32)

        kvp[0:ATTN_BLOCK, :] = jnp.zeros((ATTN_BLOCK, 2 * KV_WIDTH), BF16)
        kvp[ATTN_BLOCK:, :] = kv_ref[...]
        dkvp[...] = jnp.zeros(dkvp.shape, F32)

        def blk(n, carry):
            st = pl.multiple_of(n * ATTN_BLOCK, ATTN_BLOCK)
            qb = q_ref[pl.ds(st, ATTN_BLOCK), :]
            dob = do_ref[pl.ds(st, ATTN_BLOCK), :]
            kw = kvp[pl.ds(st, 2 * ATTN_BLOCK), :]
            mask = _attn_mask_t(n)
            for kh in range(N_KV_HEADS):
                kk = kw[:, kh * HEAD_DIM:(kh + 1) * HEAD_DIM]
                vv = kw[:, KV_WIDTH + kh * HEAD_DIM:KV_WIDTH + (kh + 1) * HEAD_DIM]
                qs = _stack_heads(qb, kh)
                dos = _stack_heads(dob, kh)
                probs, psink = _attn_probs_t(kk, qs, mask, s_ref[kh:kh + 1, :])
                dp = _dot_nt(vv, dos)
                dv = _dot(probs.astype(BF16), dos)
                rowdot = jnp.sum(probs * dp, axis=0, keepdims=True)
                dsc = (probs * (dp - rowdot) * (HEAD_DIM ** -0.5)).astype(BF16)
                dsacc[kh:kh + 1, :] += -psink * rowdot
                dk = _dot(dsc, qs)
                dqs = _dot_tn(dsc, kk)
                for g in range(GROUP):
                    col = (kh * GROUP + g) * HEAD_DIM
                    dq_ref[pl.ds(st, ATTN_BLOCK), col:col + HEAD_DIM] = (
                        dqs[g * ATTN_BLOCK:(g + 1) * ATTN_BLOCK].astype(dq_ref.dtype))
                dkvp[pl.ds(st, 2 * ATTN_BLOCK), kh * HEAD_DIM:(kh + 1) * HEAD_DIM] += dk
                dkvp[pl.ds(st, 2 * ATTN_BLOCK), KV_WIDTH + kh * HEAD_DIM:KV_WIDTH + (kh + 1) * HEAD_DIM] += dv
            return carry

        lax.fori_loop(0, nb, blk, 0, unroll=2)
        dkv_ref[...] = dkvp[ATTN_BLOCK:, :].astype(dkv_ref.dtype)

        @pl.when(pl.program_id(0) == nseq - 1)
        def _():
            for kh in range(N_KV_HEADS):
                for g in range(GROUP):
                    tot = jnp.sum(dsacc[kh:kh + 1, g * ATTN_BLOCK:(g + 1) * ATTN_BLOCK], axis=1, keepdims=True)
                    ds_ref[kh * GROUP + g:kh * GROUP + g + 1, :] = jnp.broadcast_to(tot, (1, LANES))

    seq_q = pl.BlockSpec((seq, ATTN_WIDTH), lambda b: (b, 0))
    seq_kv = pl.BlockSpec((seq, 2 * KV_WIDTH), lambda b: (b, 0))
    return _pcall(
        body, name, (nseq,),
        [seq_q, seq_kv, pl.BlockSpec((8, GQ), lambda b: (0, 0)), seq_q],
        [seq_q, seq_kv, pl.BlockSpec((N_Q_HEADS, LANES), lambda b: (0, 0))],
        [jax.ShapeDtypeStruct((nseq * seq, ATTN_WIDTH), BF16),
         jax.ShapeDtypeStruct((nseq * seq, 2 * KV_WIDTH), BF16),
         jax.ShapeDtypeStruct((N_Q_HEADS, LANES), F32)],
        [pltpu.VMEM((ATTN_BLOCK + seq, 2 * KV_WIDTH), BF16),
         pltpu.VMEM((ATTN_BLOCK + seq, 2 * KV_WIDTH), F32),
         pltpu.VMEM((8, GQ), F32)], (q, kv, sinks_t, do), ("arbitrary",), comm)


CONV_T = 128


def _conv_taps(win, w_ref, lanes, init):
    acc = init
    for j in range(CONV_KERNEL):
        sh = win if j == CONV_KERNEL - 1 else pltpu.roll(win, CONV_KERNEL - 1 - j, 0)
        acc = acc + w_ref[j:j + 1, lanes] * sh[CONV_HALO:CONV_HALO + CONV_T]
    return acc


def _conv_block(h0p, w_ref, vec_ref, st):
    cols = []
    for cs in range(CONV_WIDTH // LANES):
        lanes = slice(cs * LANES, (cs + 1) * LANES)
        win = h0p[pl.ds(st, CONV_T + CONV_HALO), lanes]
        init = jnp.broadcast_to(vec_ref[0:1, lanes], (CONV_T, LANES))
        cols.append(_conv_taps(win, w_ref, lanes, init))
    return jnp.concatenate(cols, axis=-1)


def _glu_store(c_ref, h0p, st):
    cb = c_ref[pl.ds(st, CONV_T), :]
    h0p[pl.ds(pl.multiple_of(st + CONV_HALO, CONV_HALO), CONV_T), :] = cb[:, :CONV_WIDTH] * _sigmoid(cb[:, CONV_WIDTH:])


def conv_fwd(c, w, vec, nseq, seq, name, comm=None):
    nb = seq // CONV_T

    def body(c_ref, w_ref, vec_ref, o_ref, h1_ref, h0p):
        h0p[0:CONV_HALO, :] = jnp.zeros((CONV_HALO, CONV_WIDTH), F32)

        def blk(n, carry):
            st = pl.multiple_of(n * CONV_T, CONV_T)
            _glu_store(c_ref, h0p, st)
            h1 = _conv_block(h0p, w_ref, vec_ref, st)
            h1_ref[pl.ds(st, CONV_T), :] = h1
            mu = jnp.mean(h1, axis=-1, keepdims=True)
            xc = h1 - mu
            rstd = lax.rsqrt(jnp.mean(xc * xc, axis=-1, keepdims=True) + EPS)
            y = xc * rstd * vec_ref[1:2, :] + vec_ref[2:3, :]
            o_ref[pl.ds(st, CONV_T), :] = (y * _sigmoid(y)).astype(o_ref.dtype)
            return carry

        lax.fori_loop(0, nb, blk, 0)

    return _pcall(
        body, name, (nseq,),
        [pl.BlockSpec((seq, 2 * CONV_WIDTH), lambda b: (b, 0)),
         pl.BlockSpec((32, CONV_WIDTH), lambda b: (0, 0)),
         pl.BlockSpec((8, CONV_WIDTH), lambda b: (0, 0))],
        [pl.BlockSpec((seq, CONV_WIDTH), lambda b: (b, 0))] * 2,
        [jax.ShapeDtypeStruct((nseq * seq, CONV_WIDTH), BF16), jax.ShapeDtypeStruct((nseq * seq, CONV_WIDTH), F32)],
        [pltpu.VMEM((CONV_HALO + seq, CONV_WIDTH), F32)], (c, w, vec), ("parallel",), comm)


def conv_bwd(c, h1_saved, w, vec, dout, nseq, seq, name, comm=None):
    nb = seq // CONV_T

    def body(c_ref, h1_ref, w_ref, vec_ref, do_ref, dc_ref, dw_ref, dvec_ref, h0p, dh1p, dwacc):
        @pl.when(pl.program_id(0) == 0)
        def _():
            dwacc[...] = jnp.zeros(dwacc.shape, F32)
            dvec_ref[...] = jnp.zeros(dvec_ref.shape, F32)

        h0p[0:CONV_HALO, :] = jnp.zeros((CONV_HALO, CONV_WIDTH), F32)
        dh1p[seq:seq + CONV_HALO, :] = jnp.zeros((CONV_HALO, CONV_WIDTH), F32)

        def pass_a(n, carry):
            st = pl.multiple_of(n * CONV_T, CONV_T)
            _glu_store(c_ref, h0p, st)
            h1 = h1_ref[pl.ds(st, CONV_T), :]
            mu = jnp.mean(h1, axis=-1, keepdims=True)
            xc = h1 - mu
            rstd = lax.rsqrt(jnp.mean(xc * xc, axis=-1, keepdims=True) + EPS)
            xh = xc * rstd
            y = xh * vec_ref[1:2, :] + vec_ref[2:3, :]
            sg = _sigmoid(y)
            dy = do_ref[pl.ds(st, CONV_T), :] * (sg * (1.0 + y * (1.0 - sg)))
            dvec_ref[1:2, :] += jnp.sum(dy * xh, axis=0, keepdims=True)
            dvec_ref[2:3, :] += jnp.sum(dy, axis=0, keepdims=True)
            dxh = dy * vec_ref[1:2, :]
            dh1 = rstd * (dxh - jnp.mean(dxh, axis=-1, keepdims=True)
                          - xh * jnp.mean(dxh * xh, axis=-1, keepdims=True))
            dvec_ref[0:1, :] += jnp.sum(dh1, axis=0, keepdims=True)
            dh1p[pl.ds(st, CONV_T), :] = dh1
            return carry

        lax.fori_loop(0, nb, pass_a, 0)

        def pass_b(n, carry):
            st = pl.multiple_of(n * CONV_T, CONV_T)
            cols = []
            for cs in range(CONV_WIDTH // LANES):
                lanes = slice(cs * LANES, (cs + 1) * LANES)
                wind = dh1p[pl.ds(st, CONV_T + CONV_HALO), lanes]
                winh = h0p[pl.ds(st, CONV_T + CONV_HALO), lanes]
                d1 = wind[0:CONV_T]
                acc = jnp.zeros((CONV_T, LANES), F32)
                for j in range(CONV_KERNEL):
                    acc = acc + w_ref[j:j + 1, lanes] * pltpu.roll(wind, 2 + j, 0)[CONV_HALO:CONV_HALO + CONV_T]
                    hs = winh if j == CONV_KERNEL - 1 else pltpu.roll(winh, CONV_KERNEL - 1 - j, 0)
                    prod = d1 * hs[CONV_HALO:CONV_HALO + CONV_T]
                    part = prod[0:8]
                    for r in range(8, CONV_T, 8):
                        part = part + prod[r:r + 8]
                    dwacc[8 * j:8 * j + 8, lanes] += part
                cols.append(acc)
            dh0 = jnp.concatenate(cols, axis=-1)
            cb = c_ref[pl.ds(st, CONV_T), :]
            av, gt = cb[:, :CONV_WIDTH], cb[:, CONV_WIDTH:]
            sg = _sigmoid(gt)
            dc_ref[pl.ds(st, CONV_T), :] = jnp.concatenate(
                [dh0 * sg, dh0 * av * sg * (1.0 - sg)], axis=-1).astype(dc_ref.dtype)
            return carry

        lax.fori_loop(0, nb, pass_b, 0)

        @pl.when(pl.program_id(0) == nseq - 1)
        def _():
            dw_ref[...] = jnp.zeros(dw_ref.shape, F32)
            for j in range(CONV_KERNEL):
                dw_ref[j:j + 1, :] = jnp.sum(dwacc[8 * j:8 * j + 8, :], axis=0, keepdims=True)

    return _pcall(
        body, name, (nseq,),
        [pl.BlockSpec((seq, 2 * CONV_WIDTH), lambda b: (b, 0)),
         pl.BlockSpec((seq, CONV_WIDTH), lambda b: (b, 0)),
         pl.BlockSpec((32, CONV_WIDTH), lambda b: (0, 0)),
         pl.BlockSpec((8, CONV_WIDTH), lambda b: (0, 0)),
         pl.BlockSpec((seq, CONV_WIDTH), lambda b: (b, 0))],
        [pl.BlockSpec((seq, 2 * CONV_WIDTH), lambda b: (b, 0)),
         pl.BlockSpec((32, CONV_WIDTH), lambda b: (0, 0)),
         pl.BlockSpec((8, CONV_WIDTH), lambda b: (0, 0))],
        [jax.ShapeDtypeStruct((nseq * seq, 2 * CONV_WIDTH), BF16),
         jax.ShapeDtypeStruct((32, CONV_WIDTH), F32),
         jax.ShapeDtypeStruct((8, CONV_WIDTH), F32)],
        [pltpu.VMEM((CONV_HALO + seq, CONV_WIDTH), F32),
         pltpu.VMEM((seq + CONV_HALO, CONV_WIDTH), F32),
         pltpu.VMEM((8 * 32, CONV_WIDTH), F32)], (c, h1_saved, w, vec, dout), ("arbitrary",), comm)


POOL_T = 128


def _pooled_block(zpp, st, grp):
    lanes = slice(grp * LANES, (grp + 1) * LANES)
    win = zpp[pl.ds(st, POOL_T + POOL_HALO), lanes]
    acc = win
    for lvl in range(grp + 1):
        acc = acc + pltpu.roll(acc, 1 << lvl, 0)
    t = st + lax.broadcasted_iota(jnp.int32, (POOL_T, 1), 0)
    inv = 1.0 / jnp.minimum(t + 1, POOL_WINDOWS[grp]).astype(F32)
    return acc[POOL_HALO:] * inv - win[POOL_HALO:], inv


def pool_fwd(zp, wp, scale, nseq, seq, name):
    nb = seq // POOL_T

    def body(z_ref, wp_ref, sc_ref, o_ref, zpp):
        zpp[0:POOL_HALO, :] = jnp.zeros((POOL_HALO, POOL_WIDTH), F32)
        zpp[POOL_HALO:, :] = z_ref[...]

        def blk(n, carry):
            st = pl.multiple_of(n * POOL_T, POOL_T)
            for grp in range(len(POOL_WINDOWS)):
                lanes = slice(grp * LANES, (grp + 1) * LANES)
                pooled, _ = _pooled_block(zpp, st, grp)
                o_ref[pl.ds(st, POOL_T), lanes] = (
                    _dot(pooled.astype(BF16), wp_ref[grp]) * sc_ref[0:1, lanes]).astype(o_ref.dtype)
            return carry

        lax.fori_loop(0, nb, blk, 0)

    return pl.pallas_call(
        body, name=name, grid=(nseq,),
        in_specs=[pl.BlockSpec((seq, POOL_WIDTH), lambda b: (b, 0)),
                  pl.BlockSpec((4, LANES, LANES), lambda b: (0, 0, 0)),
                  pl.BlockSpec((1, POOL_WIDTH), lambda b: (0, 0))],
        out_specs=pl.BlockSpec((seq, POOL_WIDTH), lambda b: (b, 0)),
        out_shape=jax.ShapeDtypeStruct((nseq * seq, POOL_WIDTH), BF16),
        scratch_shapes=[pltpu.VMEM((POOL_HALO + seq, POOL_WIDTH), F32)],
        compiler_params=_params("parallel"),
    )(zp, wp, scale)


def pool_bwd(zp, wp, scale, dout, nseq, seq, name, comm=None):
    nb = seq // POOL_T

    def body(z_ref, wp_ref, sc_ref, do_ref, dz_ref, dwp_ref, dsc_ref, zpp, dpcp, negd):
        @pl.when(pl.program_id(0) == 0)
        def _():
            dwp_ref[...] = jnp.zeros(dwp_ref.shape, F32)
            dsc_ref[...] = jnp.zeros(dsc_ref.shape, F32)

        zpp[0:POOL_HALO, :] = jnp.zeros((POOL_HALO, POOL_WIDTH), F32)
        zpp[POOL_HALO:, :] = z_ref[...]
        dpcp[seq:seq + POOL_HALO, :] = jnp.zeros((POOL_HALO, POOL_WIDTH), F32)

        def pass_a(n, carry):
            st = pl.multiple_of(n * POOL_T, POOL_T)
            for grp in range(len(POOL_WINDOWS)):
                lanes = slice(grp * LANES, (grp + 1) * LANES)
                pooled, inv = _pooled_block(zpp, st, grp)
                pb = pooled.astype(BF16)
                dob = do_ref[pl.ds(st, POOL_T), lanes]
                dsc_ref[0:1, lanes] += jnp.sum(dob * _dot(pb, wp_ref[grp]), axis=0, keepdims=True)
                dpm = (dob * sc_ref[0:1, lanes]).astype(BF16)
                dwp_ref[grp] += _dot_tn(pb, dpm)
                dpooled = _dot_nt(dpm, wp_ref[grp])
                negd[pl.ds(st, POOL_T), lanes] = -dpooled
                dpcp[pl.ds(st, POOL_T), lanes] = dpooled * inv
            return carry

        lax.fori_loop(0, nb, pass_a, 0)

        def pass_b(n, carry):
            st = pl.multiple_of(n * POOL_T, POOL_T)
            rows = POOL_T + POOL_HALO
            for grp in range(len(POOL_WINDOWS)):
                lanes = slice(grp * LANES, (grp + 1) * LANES)
                acc = dpcp[pl.ds(st, rows), lanes]
                for lvl in range(grp + 1):
                    acc = acc + pltpu.roll(acc, rows - (1 << lvl), 0)
                dz_ref[pl.ds(st, POOL_T), lanes] = (acc[0:POOL_T] + negd[pl.ds(st, POOL_T), lanes]).astype(dz_ref.dtype)
            return carry

        lax.fori_loop(0, nb, pass_b, 0)

    seq_spec = pl.BlockSpec((seq, POOL_WIDTH), lambda b: (b, 0))
    return _pcall(
        body, name, (nseq,),
        [seq_spec, pl.BlockSpec((4, LANES, LANES), lambda b: (0, 0, 0)),
         pl.BlockSpec((1, POOL_WIDTH), lambda b: (0, 0)), seq_spec],
        [seq_spec, pl.BlockSpec((4, LANES, LANES), lambda b: (0, 0, 0)),
         pl.BlockSpec((8, POOL_WIDTH), lambda b: (0, 0))],
        [jax.ShapeDtypeStruct((nseq * seq, POOL_WIDTH), BF16),
         jax.ShapeDtypeStruct((4, LANES, LANES), F32),
         jax.ShapeDtypeStruct((8, POOL_WIDTH), F32)],
        [pltpu.VMEM((POOL_HALO + seq, POOL_WIDTH), F32),
         pltpu.VMEM((seq + POOL_HALO, POOL_WIDTH), F32),
         pltpu.VMEM((seq, POOL_WIDTH), F32)], (zp, wp, scale, dout), ("arbitrary",), comm)


GELU_C0 = 0.7978845608028654
GELU_C1 = 0.044715


def _gelu(xv):
    return xv * (0.5 * (1.0 + jnp.tanh(GELU_C0 * (xv + GELU_C1 * (xv * xv * xv)))))


def _gelu_grad(xv):
    t = jnp.tanh(GELU_C0 * (xv + GELU_C1 * (xv * xv * xv)))
    return 0.5 * (1.0 + t) + 0.5 * xv * (1.0 - t * t) * (GELU_C0 * (1.0 + 3.0 * GELU_C1 * xv * xv))


def _tril():
    ti = lax.broadcasted_iota(jnp.int32, (SGU_CHUNK, SGU_CHUNK), 0)
    si = lax.broadcasted_iota(jnp.int32, (SGU_CHUNK, SGU_CHUNK), 1)
    return si <= ti


def sgu_fwd(zs, ws, bst, ln, nseq, seq, name):
    nc = seq // SGU_CHUNK

    def body(z_ref, ws_ref, bs_ref, ln_ref, o_ref):
        tril = _tril()

        def blk(n, carry):
            st = pl.multiple_of(n * SGU_CHUNK, SGU_CHUNK)
            ge = _gelu(z_ref[pl.ds(st, SGU_CHUNK), :])
            uu, vv = ge[:, :SGU_WIDTH], ge[:, SGU_WIDTH:]
            mu = jnp.mean(vv, axis=-1, keepdims=True)
            xc = vv - mu
            rstd = lax.rsqrt(jnp.mean(xc * xc, axis=-1, keepdims=True) + EPS)
            vn = (xc * rstd * ln_ref[0:1, :] + ln_ref[1:2, :]).astype(BF16)
            for g in range(4):
                lanes = slice(g * LANES, (g + 1) * LANES)
                wm = jnp.where(tril, ws_ref[g], 0.0).astype(BF16)
                mixed = _dot(wm, vn[:, lanes]) + bs_ref[:, g:g + 1]
                o_ref[pl.ds(st, SGU_CHUNK), lanes] = (uu[:, lanes] * mixed).astype(o_ref.dtype)
            return carry

        lax.fori_loop(0, nc, blk, 0)

    return pl.pallas_call(
        body, name=name, grid=(nseq,),
        in_specs=[pl.BlockSpec((seq, 2 * SGU_WIDTH), lambda b: (b, 0)),
                  pl.BlockSpec((4, LANES, LANES), lambda b: (0, 0, 0)),
                  pl.BlockSpec((SGU_CHUNK, 4), lambda b: (0, 0)),
                  pl.BlockSpec((8, SGU_WIDTH), lambda b: (0, 0))],
        out_specs=pl.BlockSpec((seq, SGU_WIDTH), lambda b: (b, 0)),
        out_shape=jax.ShapeDtypeStruct((nseq * seq, SGU_WIDTH), BF16),
        compiler_params=_params("parallel"),
    )(zs, ws, bst, ln)


def sgu_bwd(zs, ws, bst, ln, dout, nseq, seq, name, comm=None):
    nc = seq // SGU_CHUNK

    def body(z_ref, ws_ref, bs_ref, ln_ref, do_ref, dz_ref, dws_ref, dbs_ref, dln_ref):
        @pl.when(pl.program_id(0) == 0)
        def _():
            dws_ref[...] = jnp.zeros(dws_ref.shape, F32)
            dbs_ref[...] = jnp.zeros(dbs_ref.shape, F32)
            dln_ref[...] = jnp.zeros(dln_ref.shape, F32)

        tril = _tril()

        def blk(n, carry):
            st = pl.multiple_of(n * SGU_CHUNK, SGU_CHUNK)
            zv = z_ref[pl.ds(st, SGU_CHUNK), :]
            ge = _gelu(zv)
            uu, vv = ge[:, :SGU_WIDTH], ge[:, SGU_WIDTH:]
            mu = jnp.mean(vv, axis=-1, keepdims=True)
            xc = vv - mu
            rstd = lax.rsqrt(jnp.mean(xc * xc, axis=-1, keepdims=True) + EPS)
            xh = xc * rstd
            vn = (xh * ln_ref[0:1, :] + ln_ref[1:2, :]).astype(BF16)
            dob = do_ref[pl.ds(st, SGU_CHUNK), :]
            du_cols, dvn_cols = [], []
            for g in range(4):
                lanes = slice(g * LANES, (g + 1) * LANES)
                wm = jnp.where(tril, ws_ref[g], 0.0).astype(BF16)
                mixed = _dot(wm, vn[:, lanes]) + bs_ref[:, g:g + 1]
                du_cols.append(dob[:, lanes] * mixed)
                dmix = dob[:, lanes] * uu[:, lanes]
                dbs_ref[g] += jnp.broadcast_to(jnp.sum(dmix, axis=-1, keepdims=True), (SGU_CHUNK, LANES))
                dmb = dmix.astype(BF16)
                dws_ref[g] += jnp.where(tril, _dot_nt(dmb, vn[:, lanes]), 0.0)
                dvn_cols.append(_dot_tn(wm, dmb))
            dvn = jnp.concatenate(dvn_cols, axis=-1)
            dln_ref[0:1, :] += jnp.sum(dvn * xh, axis=0, keepdims=True)
            dln_ref[1:2, :] += jnp.sum(dvn, axis=0, keepdims=True)
            dxh = dvn * ln_ref[0:1, :]
            dv = rstd * (dxh - jnp.mean(dxh, axis=-1, keepdims=True)
                         - xh * jnp.mean(dxh * xh, axis=-1, keepdims=True))
            dge = jnp.concatenate(du_cols + [dv], axis=-1)
            dz_ref[pl.ds(st, SGU_CHUNK), :] = (dge * _gelu_grad(zv)).astype(dz_ref.dtype)
            return carry

        lax.fori_loop(0, nc, blk, 0)

    w_spec = pl.BlockSpec((4, LANES, LANES), lambda b: (0, 0, 0))
    ln_spec = pl.BlockSpec((8, SGU_WIDTH), lambda b: (0, 0))
    return _pcall(
        body, name, (nseq,),
        [pl.BlockSpec((seq, 2 * SGU_WIDTH), lambda b: (b, 0)), w_spec,
         pl.BlockSpec((SGU_CHUNK, 4), lambda b: (0, 0)), ln_spec,
         pl.BlockSpec((seq, SGU_WIDTH), lambda b: (b, 0))],
        [pl.BlockSpec((seq, 2 * SGU_WIDTH), lambda b: (b, 0)), w_spec, w_spec, ln_spec],
        [jax.ShapeDtypeStruct((nseq * seq, 2 * SGU_WIDTH), BF16),
         jax.ShapeDtypeStruct((4, LANES, LANES), F32),
         jax.ShapeDtypeStruct((4, LANES, LANES), F32),
         jax.ShapeDtypeStruct((8, SGU_WIDTH), F32)],
        [], (zs, ws, bst, ln, dout), ("arbitrary",), comm)


def _ew_rows(rows, cols, nbuf):
    t = _row_tile(rows, 1024)
    while t > 8 and t * cols * 4 * nbuf * 2 > 24 * 2**20:
        t //= 2
    return t


def adamw(w, g, m, v, name):
    layers, rows, cols = w.shape
    tr = _ew_rows(rows, cols, 7)

    def body(w_ref, g_ref, m_ref, v_ref, d_ref, mo_ref, vo_ref):
        gv = g_ref[...]
        mn = ADAM_B1 * m_ref[...] + (1.0 - ADAM_B1) * gv
        vn = ADAM_B2 * v_ref[...] + (1.0 - ADAM_B2) * (gv * gv)
        m_hat = mn / (1.0 - ADAM_B1 ** ADAM_STEP)
        v_hat = vn / (1.0 - ADAM_B2 ** ADAM_STEP)
        d_ref[...] = -ADAM_LR * (m_hat / (jnp.sqrt(v_hat) + ADAM_EPS) + ADAM_WD * w_ref[...])
        mo_ref[...] = mn
        vo_ref[...] = vn

    spec = pl.BlockSpec((1, tr, cols), lambda l, i: (l, i, 0))
    return pl.pallas_call(
        body, name=name, grid=(layers, rows // tr),
        in_specs=[spec] * 4, out_specs=[spec] * 3,
        out_shape=[jax.ShapeDtypeStruct(w.shape, F32)] * 3,
        compiler_params=_params("parallel", "parallel"),
    )(w, g, m, v)


def add_cast(a, b, name, dtype=BF16):
    nslab, rows, cols = a.shape
    tr = _ew_rows(rows, cols, 3)

    def body(a_ref, b_ref, o_ref):
        o_ref[...] = (a_ref[...] + b_ref[...]).astype(dtype)

    spec = pl.BlockSpec((1, tr, cols), lambda k, i: (k, i, 0))
    return pl.pallas_call(
        body, name=name, grid=(nslab, rows // tr),
        in_specs=[spec, spec], out_specs=spec,
        out_shape=jax.ShapeDtypeStruct(a.shape, dtype),
        compiler_params=_params("parallel", "parallel"),
    )(a, b)


def pair_sum(t, got, core, name):
    nslab, h, cols = got.shape
    tr = _ew_rows(h, cols, 3)
    nb = h // tr

    def body(c_ref, a_ref, b_ref, o_ref):
        o_ref[...] = (a_ref[...] + b_ref[...]).astype(BF16)

    spec = pl.BlockSpec((1, tr, cols), lambda k, i, c: (k, i, 0))
    return pl.pallas_call(
        body, name=name,
        grid_spec=pltpu.PrefetchScalarGridSpec(
            num_scalar_prefetch=1, grid=(nslab, nb),
            in_specs=[pl.BlockSpec((1, tr, cols), lambda k, i, c: (k, c[0] * nb + i, 0)), spec],
            out_specs=spec),
        out_shape=jax.ShapeDtypeStruct(got.shape, BF16),
        compiler_params=_params("parallel", "parallel"),
    )(core, t, got)


def sum_parts(parts, name, first=None):
    npart, rows, cols = parts.shape
    tr = _ew_rows(rows, cols, npart + 2)

    def body(*refs):
        p_ref, o_ref = refs[-2], refs[-1]
        acc = p_ref[0].astype(F32) if first is None else refs[0][...].astype(F32) + p_ref[0].astype(F32)
        for j in range(1, npart):
            acc = acc + p_ref[j].astype(F32)
        o_ref[...] = acc

    row = pl.BlockSpec((tr, cols), lambda i: (i, 0))
    ins = [parts] if first is None else [first, parts]
    return pl.pallas_call(
        body, name=name, grid=(rows // tr,),
        in_specs=([] if first is None else [row]) + [pl.BlockSpec((npart, tr, cols), lambda i: (0, i, 0))],
        out_specs=row,
        out_shape=jax.ShapeDtypeStruct((rows, cols), F32),
        compiler_params=_params("parallel"),
    )(*ins)


ANY = pl.BlockSpec(memory_space=pl.ANY)
MESH = pl.DeviceIdType.MESH


def _me():
    return lax.axis_index("x"), lax.axis_index("y"), lax.axis_index("c")


def _flip(pos, rel):
    return tuple(1 - p if f else p for p, f in zip(pos, rel))


SIBLING = (0, 0, 1)
OTHER_CHIPS = ((1, 0, 0), (0, 1, 0), (1, 1, 0))


def _chip_of(pos, rel=(0, 0, 0)):
    px, py, _ = _flip(pos, rel)
    return 2 * px + py


def allgather_blocks(shards, name):
    nt = len(shards)
    hs = [s.shape[0] // 2 for s in shards]

    def body(*refs):
        ins, outs = refs[:nt], refs[nt:2 * nt]
        send_sems, recv_sems, loc_sems = refs[2 * nt:]
        pos = _me()
        x, y, c = pos

        def block_id(rel):
            px, py, pc = _flip(pos, rel)
            return 4 * px + 2 * py + pc

        def copy(t, k, block_rel, to_rel, src=None):
            dst = outs[t].at[block_id(block_rel)]
            return pltpu.make_async_remote_copy(
                src_ref=dst if src is None else src, dst_ref=dst,
                send_sem=send_sems.at[t * 7 + k], recv_sem=recv_sems.at[t * 7 + k],
                device_id=_flip(pos, to_rel), device_id_type=MESH)

        own = [ins[t].at[pl.ds(c * hs[t], hs[t])] for t in range(nt)]
        mine = [pltpu.make_async_copy(own[t], outs[t].at[block_id((0, 0, 0))], loc_sems.at[t]) for t in range(nt)]
        for cp in mine:
            cp.start()
        first = []
        for t in range(nt):
            first.append(copy(t, 0, (0, 0, 0), SIBLING, src=own[t]))
            first += [copy(t, 1 + j, (0, 0, 0), rel, src=own[t]) for j, rel in enumerate(OTHER_CHIPS)]
        for cp in first:
            cp.start()
        passed = []
        for j, rel in enumerate(OTHER_CHIPS):
            for t in range(nt):
                copy(t, 1 + j, rel, (0, 0, 0)).wait_recv()
                fwd = copy(t, 4 + j, rel, SIBLING)
                fwd.start()
                passed.append(fwd)
        for t in range(nt):
            copy(t, 0, SIBLING, (0, 0, 0)).wait_recv()
            for j, rel in enumerate(OTHER_CHIPS):
                copy(t, 4 + j, (rel[0], rel[1], 1), (0, 0, 0)).wait_recv()
        for cp in first + passed:
            cp.wait_send()
        for cp in mine:
            cp.wait()

    return pl.pallas_call(
        body, name=name,
        in_specs=[ANY] * nt, out_specs=[ANY] * nt,
        out_shape=[jax.ShapeDtypeStruct((N_DEV, h, s.shape[1]), s.dtype) for h, s in zip(hs, shards)],
        scratch_shapes=[pltpu.SemaphoreType.DMA((7 * nt,)), pltpu.SemaphoreType.DMA((7 * nt,)),
                        pltpu.SemaphoreType.DMA((nt,))],
    )(*shards)


def _block_id(pos, rel=(0, 0, 0)):
    px, py, pc = _flip(pos, rel)
    return 4 * px + 2 * py + pc


def gather_first_hop(shards):
    hs = [s.shape[0] // 2 for s in shards]

    def plan(ins, outs, pos):
        me = _block_id(pos)
        remote = []
        for i, o, h in zip(ins, outs, hs):
            own = i.at[pl.ds(pos[2] * h, h)]
            remote += [(rel, own, o.at[me]) for rel in (SIBLING,) + OTHER_CHIPS]
        return remote

    return Comm(shards, [((N_DEV, h, s.shape[1]), s.dtype) for h, s in zip(hs, shards)], plan, 4 * len(shards))


def gather_second_hop(gathered):
    def plan(ins, outs, pos):
        remote = []
        for i, o in zip(ins, outs):
            for rel in OTHER_CHIPS:
                blk = _block_id(pos, rel)
                remote.append((SIBLING, i.at[blk], o.at[blk]))
        return remote

    return Comm(gathered, [(g.shape, g.dtype) for g in gathered], plan, 3 * len(gathered),
                aliases={i: i for i in range(len(gathered))})


def swap_comm(xs):
    def plan(ins, outs, pos):
        return [(SIBLING, i, o) for i, o in zip(ins, outs)]

    return Comm(list(xs), [(v.shape, v.dtype) for v in xs], plan, len(xs))


def give_half_comm(ts, plain=()):
    nt = len(ts)

    def plan(ins, outs, pos):
        remote = []
        for i, o in zip(ins[:nt], outs[:nt]):
            h = o.shape[1]
            remote.append((SIBLING, i.at[:, pl.ds((1 - pos[2]) * h, h)], o))
        return remote + [(SIBLING, i, o) for i, o in zip(ins[nt:], outs[nt:])]

    shapes = [((t.shape[0], t.shape[1] // 2, t.shape[2]), t.dtype) for t in ts] + [(v.shape, v.dtype) for v in plain]
    return Comm(list(ts) + list(plain), shapes, plan, nt + len(plain))


def chip_scatter_comm(xs, shared=None):
    nx = len(xs)

    def plan(ins, outs, pos):
        me = _chip_of(pos)
        remote = []
        for i, o in zip(ins[:nx], outs[:nx]):
            remote += [(rel, i.at[_chip_of(pos, rel)], o.at[j]) for j, rel in enumerate(OTHER_CHIPS)]
        if shared is not None:
            remote += [(rel, ins[nx], outs[nx].at[me]) for rel in OTHER_CHIPS]
        return remote

    shapes = [((3,) + v.shape[1:], v.dtype) for v in xs]
    if shared is not None:
        shapes.append(((N_CHIPS,) + shared.shape, shared.dtype))
    return Comm(list(xs) + ([] if shared is None else [shared]), shapes, plan, 3 * nx + (0 if shared is None else 3))


def run_comm(comm, name):
    return _pcall(lambda: None, name, (1,), [], [], [], [], (), ("arbitrary",), comm)[1]


PACK_ROWS = 256


def _pack(arrs):
    parts, layout = [], []
    row = 0
    for a in arrs:
        flat = a.reshape(-1).astype(F32)
        size = flat.shape[0]
        rows = -(-size // (8 * LANES)) * 8
        flat = jnp.pad(flat, (0, rows * LANES - size))
        parts.append(flat.reshape(rows, LANES))
        layout.append((row, rows, size, a.shape))
        row += rows
    if row % PACK_ROWS:
        parts.append(jnp.zeros((PACK_ROWS - row % PACK_ROWS, LANES), F32))
    return jnp.concatenate(parts, axis=0), layout


def _unpack(packed, layout):
    return [packed[r0:r0 + rows].reshape(-1)[:size].reshape(shape) for r0, rows, size, shape in layout]


SMALL_REPL = ['mix_norm', 'a_b_in', 'a_sinks', 'a_conv_b', 'a_cln_g', 'a_cln_b', 'c_w_pool', 'c_w_s', 'c_b_s',
              'ffn_norm', 'final_norm']
SMALL_SHARD = ['a_conv_w', 'c_pool_scale', 'c_sln_g', 'c_sln_b']
BIG = ['a_w_in', 'a_w_out', 'c_w_in', 'c_w_out', 'ffn_w_gate', 'ffn_w_up', 'ffn_w_down']
TRANSPOSED = ('a_w_in', 'ffn_w_gate', 'ffn_w_up')
BIG_COL_SHARDED = {'c_w_in'}


def _full_weight(name, g8):
    _, h, cols = g8.shape
    g4 = g8.reshape(N_CHIPS, 2 * h, cols)
    if name not in BIG_COL_SHARDED:
        return g4.reshape(-1, cols)
    return jnp.transpose(g4, (1, 0, 2)).reshape(2 * h, N_CHIPS * cols)


def _to_shard_major(name, f):
    if name not in BIG_COL_SHARDED:
        return f.reshape(N_CHIPS, f.shape[0] // N_CHIPS, f.shape[1])
    r, cfull = f.shape
    return jnp.transpose(f.reshape(r, N_CHIPS, cfull // N_CHIPS), (1, 0, 2))


def kernel(*args):
    a = dict(zip(IN_NAMES, args))
    bl, seq, _ = a['x'].shape
    n = bl * seq
    x = a['x'].reshape(n, D_MODEL)
    target = a['loss_target'].reshape(n, D_MODEL)
    xi, yi, ci = _me()
    chip = 2 * xi + yi

    shard = {'a_w_in': a['a_w_in'][0].T, 'a_w_out': a['a_w_out'][0], 'c_w_in': a['c_w_in'][0], 'c_w_out': a['c_w_out'][0]}
    for layer in range(2):
        shard['gate' + str(layer)] = a['ffn_w_gate'][layer].T
        shard['up' + str(layer)] = a['ffn_w_up'][layer].T
        shard['down' + str(layer)] = a['ffn_w_down'][layer]
    shard = {k: v.astype(BF16) for k, v in shard.items()}
    core = ci.astype(jnp.int32).reshape(1)
    block_id = 4 * xi + 2 * yi + ci

    def first_hop(*names):
        return gather_first_hop([shard[k] for k in names])

    def finish(name, g8):
        h = shard[name].shape[0] // 2
        own = lax.dynamic_slice_in_dim(shard[name], ci * h, h, axis=0)
        return _full_weight(name, lax.dynamic_update_slice_in_dim(g8, own[None], block_id, axis=0))

    a_w_in_t = _full_weight('a_w_in', allgather_blocks([shard['a_w_in']], "gather_a_w_in")[0])
    in0_width = a_w_in_t.shape[0]
    small_shard_pack, small_shard_layout = _pack([a[k] for k in SMALL_SHARD])
    hop_a = first_hop('a_w_out', 'c_w_out')
    hop_s = chip_scatter_comm([], shared=small_shard_pack)
    mix_norm, ffn_norm = a['mix_norm'], a['ffn_norm']
    (hn0, q, kv, cc), outs = norm_inproj(
        x, mix_norm[0:1], a_w_in_t, a['a_b_in'],
        [(0, ATTN_WIDTH), (ATTN_WIDTH, ATTN_WIDTH + 2 * KV_WIDTH), (ATTN_WIDTH + 2 * KV_WIDTH, in0_width)],
        [BF16, BF16, F32], "in_proj0", comm=hop_a + hop_s, w_transposed=True)
    got_a, (ss,) = hop_a.split(outs, hop_s)
    ss = lax.dynamic_update_slice_in_dim(ss, small_shard_pack[None], chip, axis=0)
    ss_full = []
    for r0, rows, size, shape in small_shard_layout:
        per_chip = ss[:, r0:r0 + rows].reshape(N_CHIPS, -1)[:, :size].reshape((N_CHIPS,) + shape)
        ss_full.append(jnp.concatenate([per_chip[k] for k in range(N_CHIPS)], axis=-1))
    a_conv_w, c_pool_scale, c_sln_g, c_sln_b = [v[0] for v in ss_full]

    conv_taps = jnp.pad(a_conv_w, ((0, 32 - CONV_KERNEL), (0, 0)))
    conv_vec = jnp.pad(jnp.stack([a['a_conv_b'][0], a['a_cln_g'][0], a['a_cln_b'][0]]), ((0, 5), (0, 0)))
    sinks_b = jnp.pad(jnp.repeat(a['a_sinks'][0].reshape(N_KV_HEADS, GROUP), ATTN_BLOCK, axis=1), ((0, 6), (0, 0)))
    w_pool_bf = a['c_w_pool'][0].astype(BF16)
    pool_scale = c_pool_scale.reshape(1, POOL_WIDTH)
    w_s = a['c_w_s'][0]
    b_s_t = a['c_b_s'][0].T
    sgu_ln = jnp.pad(jnp.stack([c_sln_g, c_sln_b]), ((0, 6), (0, 0)))
    final_norm = a['final_norm'].reshape(1, D_MODEL)

    hop_b, pass_a = first_hop('gate0', 'c_w_in'), gather_second_hop(got_a)
    attn, outs = attn_fwd(q, kv, sinks_b, bl, seq, "attn_fwd", comm=hop_b + pass_a)
    got_b, done = hop_b.split(outs, pass_a)
    a_w_out, c_w_out = finish('a_w_out', done[0]), finish('c_w_out', done[1])

    hop_c, pass_b = first_hop('up0', 'down0'), gather_second_hop(got_b)
    (conv, conv_h1), outs = conv_fwd(cc, conv_taps, conv_vec, bl, seq, "conv_fwd", comm=hop_c + pass_b)
    got_c, done = hop_c.split(outs, pass_b)
    wg0, c_w_in = finish('gate0', done[0]), finish('c_w_in', done[1])

    h1, done = out_proj(x, attn, conv, a_w_out, "out_proj0", comm=gather_second_hop(got_c))
    wu0, wd0 = finish('up0', done[0]), finish('down0', done[1])

    (hnf0, g0, u0), got_e = ffn_gate_up(h1, ffn_norm[0:1], wg0, wu0, "ffn_gate_up0",
                                        comm=first_hop('gate1', 'up1', 'down1'))

    h2, done = ffn_down(h1, g0, u0, wd0, "ffn_down0", comm=gather_second_hop(got_e))
    wg1, wu1, wd1 = finish('gate1', done[0]), finish('up1', done[1]), finish('down1', done[2])
    wg, wu, wd = [wg0, wg1], [wu0, wu1], [wd0, wd1]

    (hn1, zp, zs), _ = norm_inproj(
        h2, mix_norm[1:2], c_w_in, jnp.zeros((1, c_w_in.shape[1]), F32),
        [(0, POOL_WIDTH), (POOL_WIDTH, c_w_in.shape[1])], [F32, F32], "in_proj1")
    pool = pool_fwd(zp, w_pool_bf, pool_scale, bl, seq, "pool_fwd")
    sgu = sgu_fwd(zs, w_s, b_s_t, sgu_ln, bl, seq, "sgu_fwd")
    h3, _ = out_proj(h2, pool, sgu, c_w_out, "out_proj1")
    (hnf1, g1, u1), _ = ffn_gate_up(h3, ffn_norm[1:2], wg1, wu1, "ffn_gate_up1")
    h4, _ = ffn_down(h3, g1, u1, wd1, "ffn_down1")

    dh4, d_final_norm, loss_local = loss_head(h4, final_norm, target, "loss_head")

    grads = {}
    pieces = {}

    def slabs_of(names, fulls):
        return [_to_shard_major(k, fulls[k]) for k in names]

    def pair_sums_of(names, slabs, gots):
        return [pair_sum(t, gt, core, "pair_sum_" + k) for k, t, gt in zip(names, slabs, gots)]

    def chip_sums_of(names, sums, from_chips):
        own = [lax.dynamic_index_in_dim(p, chip, axis=0, keepdims=False) for p in sums]
        return [sum_parts(p, "chip_sum_" + k, first=o) for k, p, o in zip(names, from_chips, own)]

    (dg, du, act), _ = ffn_down_bwd(dh4, g1, u1, wd[1], "ffn_down_bwd1")
    full1 = {'down1': mm_tn(act, dh4, "dw_down1"), 'gate1': mm_tn(dg, hnf1, "dw_gate1"),
             'up1': mm_tn(du, hnf1, "dw_up1")}
    names1 = ['gate1', 'up1', 'down1']
    slabs1 = slabs_of(names1, full1)
    dh3, d_ffn_norm1, got1 = proj_rms_bwd([dg, du], [wg[1], wu[1]], h3, ffn_norm[1:2], dh4, 1, "ffn_up_bwd1",
                                          tm_pref=256, w_transposed=True, comm=give_half_comm(slabs1))
    sums1 = pair_sums_of(names1, slabs1, got1)
    d_pool, d_sgu = out_proj_bwd(dh3, c_w_out, [F32, F32], "out_proj_bwd1")
    full1['c_w_out'] = jnp.concatenate([mm_tn(pool, dh3, "dw_out1_pool"), mm_tn(sgu, dh3, "dw_out1_sgu")], axis=0)
    (dzp, d_w_pool, d_pool_scale), from_gate = pool_bwd(zp, w_pool_bf, pool_scale, d_pool, bl, seq, "pool_bwd",
                                                        comm=chip_scatter_comm(sums1[0:1]))
    (dzs, d_w_s, d_b_s_b, d_sgu_ln), from_up = sgu_bwd(zs, w_s, b_s_t, sgu_ln, d_sgu, bl, seq, "sgu_bwd",
                                                       comm=chip_scatter_comm(sums1[1:2]))
    full1['c_w_in'] = jnp.concatenate([mm_tn(hn1, dzp, "dw_in1_pool"), mm_tn(hn1, dzs, "dw_in1_sgu")], axis=1)
    names1b = ['c_w_out', 'c_w_in']
    slabs1b = slabs_of(names1b, full1)
    heavy_pack, heavy_layout = _pack([d_w_pool[None], d_w_s[None]])
    chips_down, pair1b = chip_scatter_comm(sums1[2:3]), give_half_comm(slabs1b, plain=[heavy_pack])
    dh2, d_mix_norm1, outs = proj_rms_bwd([dzp, dzs], [c_w_in[:, :POOL_WIDTH], c_w_in[:, POOL_WIDTH:]], h2,
                                          mix_norm[1:2], dh3, 1, "in_proj_bwd1", comm=chips_down + pair1b)
    from_down, got1b = chips_down.split(outs, pair1b)
    mine1 = chip_sums_of(names1, sums1, from_gate + from_up + from_down)
    sums1b = pair_sums_of(names1b, slabs1b, got1b[:2])
    heavy_pair = add_cast(heavy_pack[None], got1b[2][None], "pair_sum_heavy", dtype=F32)[0]

    join1, chips1b = swap_comm(mine1), chip_scatter_comm(sums1b, shared=heavy_pair)
    (dg, du, act), outs = ffn_down_bwd(dh2, g0, u0, wd[0], "ffn_down_bwd0", comm=join1 + chips1b)
    theirs1, from_chips1b = join1.split(outs, chips1b)
    pieces.update({k: (m, t) for k, m, t in zip(names1, mine1, theirs1)})
    mine1b = chip_sums_of(names1b, sums1b, from_chips1b[:2])
    heavy_chips = lax.dynamic_update_slice_in_dim(from_chips1b[2], heavy_pair[None], chip, axis=0)
    grads['c_w_pool'], grads['c_w_s'] = _unpack(sum_parts(heavy_chips, "heavy_sum"), heavy_layout)
    full0 = {'down0': mm_tn(act, dh2, "dw_down0"), 'gate0': mm_tn(dg, hnf0, "dw_gate0"),
             'up0': mm_tn(du, hnf0, "dw_up0")}
    names0 = ['gate0', 'up0', 'down0']
    slabs0 = slabs_of(names0, full0)
    join1b, pair0 = swap_comm(mine1b), give_half_comm(slabs0)
    dh1, d_ffn_norm0, outs = proj_rms_bwd([dg, du], [wg[0], wu[0]], h1, ffn_norm[0:1], dh2, 1, "ffn_up_bwd0",
                                          tm_pref=256, comm=join1b + pair0, w_transposed=True)
    theirs1b, got0 = join1b.split(outs, pair0)
    pieces.update({k: (m, t) for k, m, t in zip(names1b, mine1b, theirs1b)})
    sums0 = pair_sums_of(names0, slabs0, got0)

    d_attn, d_conv = out_proj_bwd(dh1, a_w_out, [BF16, F32], "out_proj_bwd0")
    full_o = {'a_w_out': jnp.concatenate([mm_tn(attn, dh1, "dw_out0_attn"), mm_tn(conv, dh1, "dw_out0_conv")], axis=0)}
    slabs_o = slabs_of(['a_w_out'], full_o)
    chips0, pair_o = chip_scatter_comm(sums0), give_half_comm(slabs_o)
    (dq, dkv, d_sinks_b), outs = attn_bwd(q, kv, sinks_b, d_attn, bl, seq, "attn_bwd", comm=chips0 + pair_o)
    from_chips0, got_o = chips0.split(outs, pair_o)
    mine0 = chip_sums_of(names0, sums0, from_chips0)
    sums_o = pair_sums_of(['a_w_out'], slabs_o, got_o)
    join0, chips_o = swap_comm(mine0), chip_scatter_comm(sums_o)
    (dcc, d_conv_taps, d_conv_vec), outs = conv_bwd(cc, conv_h1, conv_taps, conv_vec, d_conv, bl, seq, "conv_bwd",
                                                    comm=join0 + chips_o)
    theirs0, from_chips_o = join0.split(outs, chips_o)
    pieces.update({k: (m, t) for k, m, t in zip(names0, mine0, theirs0)})
    mine_o = chip_sums_of(['a_w_out'], sums_o, from_chips_o)
    kq, kk = ATTN_WIDTH, ATTN_WIDTH + 2 * KV_WIDTH
    grad_x, d_mix_norm0, _ = proj_rms_bwd([dq, dkv, dcc], [a_w_in_t[:kq], a_w_in_t[kq:kk], a_w_in_t[kk:]], x,
                                          mix_norm[0:1], dh1, 1, "in_proj_bwd0", w_transposed=True)
    dw_q, db_q = mm_tn(dq, hn0, "dw_in0_q", xsum=True)
    dw_kv, db_kv = mm_tn(dkv, hn0, "dw_in0_kv", xsum=True)
    (dw_c, db_c), theirs_o = mm_tn(dcc, hn0, "dw_in0_c", xsum=True, comm=swap_comm(mine_o))
    pieces['a_w_out'] = (mine_o[0], theirs_o[0])
    d_a_b_in = jnp.concatenate([db_q, db_kv, db_c], axis=0)
    slabs_i = slabs_of(['a_w_in'], {'a_w_in': jnp.concatenate([dw_q, dw_kv, dw_c], axis=0)})

    small_full = {
        'mix_norm': jnp.stack([d_mix_norm0, d_mix_norm1]), 'a_b_in': d_a_b_in[None], 'a_sinks': d_sinks_b[:, 0][None],
        'a_conv_w': d_conv_taps[:CONV_KERNEL][None], 'a_conv_b': d_conv_vec[0][None], 'a_cln_g': d_conv_vec[1][None],
        'a_cln_b': d_conv_vec[2][None], 'c_pool_scale': d_pool_scale[0][None],
        'c_sln_g': d_sgu_ln[0][None], 'c_sln_b': d_sgu_ln[1][None],
        'c_b_s': d_b_s_b[:, :, 0][None], 'ffn_norm': jnp.stack([d_ffn_norm0, d_ffn_norm1]),
        'final_norm': d_final_norm, 'loss': loss_local.reshape(1)}
    small_names = SMALL_REPL + SMALL_SHARD
    tail_names = [k for k in small_names if k in small_full] + ['loss']
    small_pack, small_layout = _pack([small_full[k] for k in tail_names])

    got_i, got_s = run_comm(give_half_comm(slabs_i, plain=[small_pack]), "tail_pair")
    sums_i = pair_sums_of(['a_w_in'], slabs_i, [got_i])
    small_pair = add_cast(small_pack[None], got_s[None], "pair_sum_small", dtype=F32)[0]
    outs = run_comm(chip_scatter_comm(sums_i, shared=small_pair), "tail_chips")
    mine_i = chip_sums_of(['a_w_in'], sums_i, outs[:1])
    small_chips = lax.dynamic_update_slice_in_dim(outs[1], small_pair[None], chip, axis=0)
    theirs_i = run_comm(swap_comm(mine_i), "tail_join")
    pieces['a_w_in'] = (mine_i[0], theirs_i[0])

    def whole(name):
        mine, theirs = pieces[name]
        return jnp.concatenate([jnp.where(ci == 0, mine, theirs), jnp.where(ci == 0, theirs, mine)], axis=0)

    for k in ('a_w_in', 'a_w_out', 'c_w_in', 'c_w_out'):
        grads[k] = whole(k)[None]
    for short, key in (('gate', 'ffn_w_gate'), ('up', 'ffn_w_up'), ('down', 'ffn_w_down')):
        grads[key] = jnp.stack([whole(short + '0'), whole(short + '1')])

    small_sum = sum_parts(small_chips, "small_sum")
    for k, g in zip(tail_names, _unpack(small_sum, small_layout)):
        if k in SMALL_SHARD:
            width = a[k].shape[-1]
            g = lax.dynamic_slice_in_dim(g, chip * width, width, axis=g.ndim - 1)
        grads[k] = g
    loss = grads.pop('loss')[0]

    delta, new_m, new_v = {}, {}, {}
    for k in BIG:
        if k in TRANSPOSED:
            flip = lambda t: jnp.swapaxes(t, 1, 2)
            d, m, v = adamw(flip(a[k]), grads[k], flip(a['m_' + k]), flip(a['v_' + k]), "adamw_" + k)
            grads[k], delta[k], new_m[k], new_v[k] = flip(grads[k]), flip(d), flip(m), flip(v)
        else:
            delta[k], new_m[k], new_v[k] = adamw(a[k], grads[k], a['m_' + k], a['v_' + k], "adamw_" + k)
    packs = [_pack([src[k] for k in small_names])
             for src in (a, grads, {k: a['m_' + k] for k in small_names}, {k: a['v_' + k] for k in small_names})]
    d, m, v = adamw(packs[0][0][None], packs[1][0][None], packs[2][0][None], packs[3][0][None], "adamw_small")
    d, m, v = d[0], m[0], v[0]
    lay = packs[0][1]
    for k, dv, mv, vv in zip(small_names, _unpack(d, lay), _unpack(m, lay), _unpack(v, lay)):
        delta[k], new_m[k], new_v[k] = dv, mv, vv

    return (loss, grad_x.reshape(a['x'].shape), *[grads[k] for k in WEIGHTS], *[delta[k] for k in WEIGHTS],
            *[new_m[k] for k in WEIGHTS], *[new_v[k] for k in WEIGHTS])
```

```python
import functools

import jax
import jax.numpy as jnp
from jax import lax
from jax.experimental import pallas as pl
from jax.experimental.pallas import tpu as pltpu

F32 = jnp.float32
BF16 = jnp.bfloat16

D_MODEL = 1024
EPS = 1e-5
N_Q_HEADS, N_KV_HEADS, HEAD_DIM = 8, 2, 64
ATTN_BLOCK = 128
ATTN_WIDTH = N_Q_HEADS * HEAD_DIM
KV_WIDTH = N_KV_HEADS * HEAD_DIM
CONV_WIDTH = 512
CONV_KERNEL = 31
CONV_HALO = 32
POOL_WINDOWS = (2, 4, 8, 16)
POOL_WIDTH = 512
POOL_HALO = 16
SGU_WIDTH = 512
SGU_CHUNK = 128
D_FF = 2816
FF_CHUNK = 128
MXU_COLS = 256
LANES = 128
N_CHIPS = 4
N_DEV = 8

ADAM_LR, ADAM_B1, ADAM_B2, ADAM_EPS, ADAM_WD, ADAM_STEP = 0.001, 0.9, 0.999, 1e-08, 0.01, 10

VMEM_LIMIT = 56 * 2**20

WEIGHTS = ['mix_norm', 'a_w_in', 'a_b_in', 'a_sinks', 'a_conv_w', 'a_conv_b', 'a_cln_g', 'a_cln_b', 'a_w_out',
           'c_w_in', 'c_w_pool', 'c_pool_scale', 'c_sln_g', 'c_sln_b', 'c_w_s', 'c_b_s', 'c_w_out',
           'ffn_norm', 'ffn_w_gate', 'ffn_w_up', 'ffn_w_down', 'final_norm']
IN_NAMES = (['x'] + WEIGHTS + ['loss_target'] + ['m_' + n for n in WEIGHTS] + ['v_' + n for n in WEIGHTS])


def _params(*sem):
    return pltpu.CompilerParams(dimension_semantics=sem, vmem_limit_bytes=VMEM_LIMIT)


def _dot(a, b):
    return jnp.dot(a, b, preferred_element_type=F32)


def _dot_nt(a, b):
    return lax.dot_general(a, b, (((1,), (1,)), ((), ())), preferred_element_type=F32)


def _dot_tn(a, b):
    return lax.dot_general(a, b, (((0,), (0,)), ((), ())), preferred_element_type=F32)


def _sigmoid(v):
    return 0.5 * jnp.tanh(0.5 * v) + 0.5


def _row_tile(n, pref):
    t = min(n, pref)
    while n % t:
        t //= 2
    return t


def _col_tile(m, rows, budget=6 * 2**20):
    best = LANES
    for t in range(LANES, m + 1, LANES):
        if m % t == 0 and rows * t * 4 <= budget:
            best = t
    return best


class Comm:
    def __init__(self, ins, out_shapes, plan, count, aliases=None):
        self.ins, self.out_shapes, self.plan, self.count, self.aliases = ins, out_shapes, plan, count, aliases or {}

    def __add__(self, other):
        ni, no = len(self.ins), len(self.out_shapes)

        def plan(ins, outs, pos):
            return self.plan(ins[:ni], outs[:no], pos) + other.plan(ins[ni:], outs[no:], pos)

        aliases = dict(self.aliases)
        aliases.update({ni + i: no + o for i, o in other.aliases.items()})
        return Comm(list(self.ins) + list(other.ins), list(self.out_shapes) + list(other.out_shapes), plan,
                    self.count + other.count, aliases)

    def split(self, outs, other):
        return outs[:len(self.out_shapes)], outs[len(self.out_shapes):]


def _pcall(body, name, grid, in_specs, out_specs, out_shape, scratch_shapes, args, sem, comm=None):
    single = not isinstance(out_shape, (list, tuple))
    if single:
        out_specs, out_shape = [out_specs], [out_shape]
    if comm is None:
        res = pl.pallas_call(body, name=name, grid=grid, in_specs=in_specs, out_specs=list(out_specs),
                             out_shape=list(out_shape), scratch_shapes=list(scratch_shapes),
                             compiler_params=_params(*sem))(*args)
        return (res[0] if single else res), []
    na, nci, no, nco, ns = len(args), len(comm.ins), len(out_shape), len(comm.out_shapes), len(scratch_shapes)

    def wrapped(*refs):
        a_refs, ci_refs = refs[:na], refs[na:na + nci]
        o_refs, co_refs = refs[na + nci:na + nci + no], refs[na + nci + no:na + nci + no + nco]
        s_refs = refs[na + nci + no + nco:na + nci + no + nco + ns]
        send_sems, recv_sems = refs[-2], refs[-1]
        pos = _me()

        def copies():
            return [pltpu.make_async_remote_copy(src_ref=s, dst_ref=d, send_sem=send_sems.at[i],
                                                 recv_sem=recv_sems.at[i], device_id=_flip(pos, rel),
                                                 device_id_type=MESH)
                    for i, (rel, s, d) in enumerate(comm.plan(ci_refs, co_refs, pos))]

        first, last = None, None
        for d, size in enumerate(grid):
            f, l = pl.program_id(d) == 0, pl.program_id(d) == size - 1
            first = f if first is None else first & f
            last = l if last is None else last & l

        @pl.when(first)
        def _():
            for cp in copies():
                cp.start()

        body(*a_refs, *o_refs, *s_refs)

        @pl.when(last)
        def _():
            for cp in copies():
                cp.wait()

    res = pl.pallas_call(
        wrapped, name=name, grid=grid,
        in_specs=list(in_specs) + [ANY] * nci, out_specs=list(out_specs) + [ANY] * nco,
        out_shape=list(out_shape) + [jax.ShapeDtypeStruct(s, d) for s, d in comm.out_shapes],
        scratch_shapes=list(scratch_shapes) + [pltpu.SemaphoreType.DMA((comm.count,)),
                                               pltpu.SemaphoreType.DMA((comm.count,))],
        input_output_aliases={na + i: no + o for i, o in comm.aliases.items()},
        compiler_params=_params(*(["arbitrary"] * len(grid))),
    )(*args, *comm.ins)
    outs = res[:no]
    return (outs[0] if single else outs), list(res[no:])


def norm_inproj(x, gain, w, bias, splits, dtypes, name, comm=None, w_transposed=False):
    n = x.shape[0]
    m = w.shape[0] if w_transposed else w.shape[1]
    tm = _row_tile(n, 512)

    def body(x_ref, g_ref, w_ref, b_ref, hn_ref, *outs):
        xv = x_ref[...]
        r = lax.rsqrt(jnp.mean(xv * xv, axis=-1, keepdims=True) + EPS)
        hn = ((xv * r) * g_ref[...]).astype(BF16)
        hn_ref[...] = hn
        z = (_dot_nt if w_transposed else _dot)(hn, w_ref[...]) + b_ref[...]
        for o, (lo, hi) in zip(outs, splits):
            o[...] = z[:, lo:hi].astype(o.dtype)

    out_shape = [jax.ShapeDtypeStruct((n, D_MODEL), BF16)]
    out_specs = [pl.BlockSpec((tm, D_MODEL), lambda i: (i, 0))]
    for (lo, hi), dt in zip(splits, dtypes):
        out_shape.append(jax.ShapeDtypeStruct((n, hi - lo), dt))
        out_specs.append(pl.BlockSpec((tm, hi - lo), lambda i: (i, 0)))
    return _pcall(
        body, name, (n // tm,),
        [pl.BlockSpec((tm, D_MODEL), lambda i: (i, 0)),
         pl.BlockSpec((1, D_MODEL), lambda i: (0, 0)),
         pl.BlockSpec(w.shape, lambda i: (0, 0)),
         pl.BlockSpec((1, m), lambda i: (0, 0))],
        out_specs, out_shape, [], (x, gain, w, bias), ("parallel",), comm)


def out_proj(res, m1, m2, w, name, comm=None):
    n = res.shape[0]
    k1, k2 = m1.shape[1], m2.shape[1]
    assert k1 == k2
    tm = _row_tile(n, 512)

    def body(r_ref, a_ref, b_ref, w1_ref, w2_ref, o_ref):
        o_ref[...] = r_ref[...] + _dot(a_ref[...], w1_ref[...]) + _dot(b_ref[...], w2_ref[...])

    return _pcall(
        body, name, (n // tm,),
        [pl.BlockSpec((tm, D_MODEL), lambda i: (i, 0)),
         pl.BlockSpec((tm, k1), lambda i: (i, 0)),
         pl.BlockSpec((tm, k2), lambda i: (i, 0)),
         pl.BlockSpec((k1, D_MODEL), lambda i: (0, 0)),
         pl.BlockSpec((k2, D_MODEL), lambda i: (1, 0))],
        pl.BlockSpec((tm, D_MODEL), lambda i: (i, 0)),
        jax.ShapeDtypeStruct((n, D_MODEL), F32), [], (res, m1, m2, w, w), ("parallel",), comm)


def ffn_gate_up(h, gain, wg_t, wu_t, name, comm=None):
    n = h.shape[0]
    tm = _row_tile(n, 1024)
    th = D_FF // 2

    def body(h_ref, g_ref, wg_ref, wu_ref, hn_ref, go_ref, uo_ref):
        @pl.when(pl.program_id(1) == 0)
        def _():
            xv = h_ref[...]
            r = lax.rsqrt(jnp.mean(xv * xv, axis=-1, keepdims=True) + EPS)
            hn_ref[...] = ((xv * r) * g_ref[...]).astype(BF16)

        hn = hn_ref[...]
        go_ref[...] = _dot_nt(hn, wg_ref[...]).astype(BF16)
        uo_ref[...] = _dot_nt(hn, wu_ref[...]).astype(BF16)

    return _pcall(
        body, name, (n // tm, D_FF // th),
        [pl.BlockSpec((tm, D_MODEL), lambda i, j: (i, 0)),
         pl.BlockSpec((1, D_MODEL), lambda i, j: (0, 0)),
         pl.BlockSpec((th, D_MODEL), lambda i, j: (j, 0)),
         pl.BlockSpec((th, D_MODEL), lambda i, j: (j, 0))],
        [pl.BlockSpec((tm, D_MODEL), lambda i, j: (i, 0)),
         pl.BlockSpec((tm, th), lambda i, j: (i, j)),
         pl.BlockSpec((tm, th), lambda i, j: (i, j))],
        [jax.ShapeDtypeStruct((n, D_MODEL), BF16),
         jax.ShapeDtypeStruct((n, D_FF), BF16),
         jax.ShapeDtypeStruct((n, D_FF), BF16)],
        [], (h, gain, wg_t, wu_t), ("parallel", "arbitrary"), comm)


def ffn_down(h, g, u, wd, name, comm=None):
    n = h.shape[0]
    tm = _row_tile(n, 512)

    def body(h_ref, g_ref, u_ref, w_ref, o_ref, a_ref):
        for c0 in range(0, D_FF, FF_CHUNK):
            gv = g_ref[:, c0:c0 + FF_CHUNK]
            a_ref[:, c0:c0 + FF_CHUNK] = gv * _sigmoid(gv) * u_ref[:, c0:c0 + FF_CHUNK]
        o_ref[...] = h_ref[...] + _dot(a_ref[...], w_ref[...])

    return _pcall(
        body, name, (n // tm,),
        [pl.BlockSpec((tm, D_MODEL), lambda i: (i, 0)),
         pl.BlockSpec((tm, D_FF), lambda i: (i, 0)),
         pl.BlockSpec((tm, D_FF), lambda i: (i, 0)),
         pl.BlockSpec((D_FF, D_MODEL), lambda i: (0, 0))],
        pl.BlockSpec((tm, D_MODEL), lambda i: (i, 0)),
        jax.ShapeDtypeStruct((n, D_MODEL), F32),
        [pltpu.VMEM((tm, D_FF), BF16)], (h, g, u, wd), ("parallel",), comm)


def ffn_down_bwd(dh, g, u, wd, name, comm=None):
    n = dh.shape[0]
    tm = _row_tile(n, 512)

    def body(dh_ref, g_ref, u_ref, w_ref, dg_ref, du_ref, a_ref):
        dhb = dh_ref[...].astype(BF16)
        for c0 in range(0, D_FF, MXU_COLS):
            cols = slice(c0, c0 + MXU_COLS)
            da = _dot_nt(dhb, w_ref[cols, :]).astype(BF16)
            gv, uv = g_ref[:, cols], u_ref[:, cols]
            sg = _sigmoid(gv)
            act = gv * sg
            dg_ref[:, cols] = (da * uv) * (sg + act * (1.0 - sg))
            du_ref[:, cols] = da * act
            a_ref[:, cols] = act * uv

    spec_h = pl.BlockSpec((tm, D_FF), lambda i: (i, 0))
    return _pcall(
        body, name, (n // tm,),
        [pl.BlockSpec((tm, D_MODEL), lambda i: (i, 0)), spec_h, spec_h,
         pl.BlockSpec((D_FF, D_MODEL), lambda i: (0, 0))],
        [spec_h, spec_h, spec_h], [jax.ShapeDtypeStruct((n, D_FF), BF16)] * 3,
        [], (dh, g, u, wd), ("parallel",), comm)


def mm_tn(x, dy, name, xsum=False, comm=None):
    n, k = x.shape
    m = dy.shape[1]
    tk = _col_tile(k, m)
    tt = _row_tile(n, 1024)

    def body(x_ref, dy_ref, o_ref, *rest):
        xt_ref = rest[-1]
        t = pl.program_id(1)
        xv = x_ref[...]
        xt_ref[...] = xv.astype(BF16).T
        part = _dot(xt_ref[...], dy_ref[...].astype(BF16))

        @pl.when(t == 0)
        def _():
            o_ref[...] = part

        @pl.when(t > 0)
        def _():
            o_ref[...] += part

        if xsum:
            cs = jnp.broadcast_to(jnp.sum(xv.astype(F32), axis=0, keepdims=True), rest[0].shape)

            @pl.when(t == 0)
            def _():
                rest[0][...] = cs

            @pl.when(t > 0)
            def _():
                rest[0][...] += cs

    out_shape = [jax.ShapeDtypeStruct((k, m), F32)]
    out_specs = [pl.BlockSpec((tk, m), lambda j, t: (j, 0))]
    if xsum:
        out_shape.append(jax.ShapeDtypeStruct((8, k), F32))
        out_specs.append(pl.BlockSpec((8, tk), lambda j, t: (0, j)))
    res, comm_outs = _pcall(
        body, name, (k // tk, n // tt),
        [pl.BlockSpec((tt, tk), lambda j, t: (t, j)),
         pl.BlockSpec((tt, m), lambda j, t: (t, 0))],
        out_specs, out_shape, [pltpu.VMEM((tk, tt), BF16)], (x, dy), ("arbitrary", "arbitrary"), comm)
    res = (res[0], res[1][0]) if xsum else res[0]
    return res if comm is None else (res, comm_outs)


def out_proj_bwd(dh, w, dtypes, name):
    n = dh.shape[0]
    k = w.shape[0]
    half = k // 2
    tm = _row_tile(n, 512)

    def body(dh_ref, w_ref, a_ref, b_ref):
        dm = _dot_nt(dh_ref[...].astype(BF16), w_ref[...])
        a_ref[...] = dm[:, :half].astype(a_ref.dtype)
        b_ref[...] = dm[:, half:].astype(b_ref.dtype)

    return pl.pallas_call(
        body, name=name, grid=(n // tm,),
        in_specs=[pl.BlockSpec((tm, D_MODEL), lambda i: (i, 0)),
                  pl.BlockSpec((k, D_MODEL), lambda i: (0, 0))],
        out_specs=[pl.BlockSpec((tm, half), lambda i: (i, 0))] * 2,
        out_shape=[jax.ShapeDtypeStruct((n, half), dtypes[0]), jax.ShapeDtypeStruct((n, half), dtypes[1])],
        compiler_params=_params("parallel"),
    )(dh, w)


def proj_rms_bwd(dys, ws, h_in, gain, dres, nk, name, tm_pref=512, comm=None, w_transposed=False):
    n = h_in.shape[0]
    npair = len(dys)
    tm = _row_tile(n, tm_pref)
    tks = [dy.shape[1] // nk for dy in dys]
    mm = _dot if w_transposed else _dot_nt

    def body(*refs):
        dy_refs = refs[:npair]
        w_refs = refs[npair:2 * npair]
        h_ref, g_ref, dr_ref, o_ref, dg_ref, acc_ref = refs[2 * npair:]
        i, k = pl.program_id(0), pl.program_id(1)
        part = mm(dy_refs[0][...], w_refs[0][...])
        for p in range(1, npair):
            part = part + mm(dy_refs[p][...], w_refs[p][...])

        @pl.when(k == 0)
        def _():
            acc_ref[...] = part

        @pl.when(k > 0)
        def _():
            acc_ref[...] += part

        @pl.when(k == nk - 1)
        def _():
            dhn = acc_ref[...]
            xv = h_ref[...]
            r = lax.rsqrt(jnp.mean(xv * xv, axis=-1, keepdims=True) + EPS)
            xh = xv * r
            uv = dhn * g_ref[...]
            o_ref[...] = dr_ref[...] + r * (uv - xh * jnp.mean(uv * xh, axis=-1, keepdims=True))
            dgp = jnp.broadcast_to(jnp.sum(dhn * xh, axis=0, keepdims=True), dg_ref.shape)

            @pl.when(i == 0)
            def _():
                dg_ref[...] = dgp

            @pl.when(i > 0)
            def _():
                dg_ref[...] += dgp

    row = pl.BlockSpec((tm, D_MODEL), lambda i, k: (i, 0))
    in_specs = [pl.BlockSpec((tm, tk), lambda i, k: (i, k)) for tk in tks]
    if w_transposed:
        in_specs += [pl.BlockSpec((tk, D_MODEL), lambda i, k: (k, 0)) for tk in tks]
    else:
        in_specs += [pl.BlockSpec((D_MODEL, tk), lambda i, k: (0, k)) for tk in tks]
    in_specs += [row, pl.BlockSpec((1, D_MODEL), lambda i, k: (0, 0)), row]
    (dh, dgain), comm_outs = _pcall(
        body, name, (n // tm, nk), in_specs,
        [row, pl.BlockSpec((8, D_MODEL), lambda i, k: (0, 0))],
        [jax.ShapeDtypeStruct((n, D_MODEL), F32), jax.ShapeDtypeStruct((8, D_MODEL), F32)],
        [pltpu.VMEM((tm, D_MODEL), F32)], (*dys, *ws, h_in, gain, dres), ("arbitrary", "arbitrary"), comm)
    return dh, dgain[0], comm_outs


def loss_head(h, gain, target, name):
    n = h.shape[0]
    tm = _row_tile(n, 512)

    def body(h_ref, g_ref, t_ref, dh_ref, dg_ref, l_ref):
        i = pl.program_id(0)
        xv = h_ref[...]
        r = lax.rsqrt(jnp.mean(xv * xv, axis=-1, keepdims=True) + EPS)
        xh = xv * r
        err = xh * g_ref[...] - t_ref[...]
        dy = err * (1.0 / D_MODEL)
        uv = dy * g_ref[...]
        dh_ref[...] = r * (uv - xh * jnp.mean(uv * xh, axis=-1, keepdims=True))
        dgp = jnp.broadcast_to(jnp.sum(dy * xh, axis=0, keepdims=True), dg_ref.shape)
        lp = jnp.sum(jnp.sum(err * err, axis=-1, keepdims=True), axis=0, keepdims=True) * (0.5 / D_MODEL)
        lp = jnp.broadcast_to(lp, l_ref.shape)

        @pl.when(i == 0)
        def _():
            dg_ref[...] = dgp
            l_ref[...] = lp

        @pl.when(i > 0)
        def _():
            dg_ref[...] += dgp
            l_ref[...] += lp

    row = pl.BlockSpec((tm, D_MODEL), lambda i: (i, 0))
    dh, dg, l = pl.pallas_call(
        body, name=name, grid=(n // tm,),
        in_specs=[row, pl.BlockSpec((1, D_MODEL), lambda i: (0, 0)), row],
        out_specs=[row, pl.BlockSpec((8, D_MODEL), lambda i: (0, 0)), pl.BlockSpec((8, LANES), lambda i: (0, 0))],
        out_shape=[jax.ShapeDtypeStruct((n, D_MODEL), F32), jax.ShapeDtypeStruct((8, D_MODEL), F32),
                   jax.ShapeDtypeStruct((8, LANES), F32)],
        compiler_params=_params("arbitrary"),
    )(h, gain, target)
    return dh, dg[0], l[0, 0]


GROUP = N_Q_HEADS // N_KV_HEADS
GQ = GROUP * ATTN_BLOCK


def _attn_mask_t(n):
    r = lax.broadcasted_iota(jnp.int32, (2 * ATTN_BLOCK, GQ), 0)
    qi = lax.broadcasted_iota(jnp.int32, (2 * ATTN_BLOCK, GQ), 1) & (ATTN_BLOCK - 1)
    band = (r > qi) & (r <= qi + ATTN_BLOCK)
    return band & ((r >= ATTN_BLOCK) | (n > 0))


def _stack_heads(blk, kh):
    return jnp.concatenate([blk[:, (kh * GROUP + g) * HEAD_DIM:(kh * GROUP + g + 1) * HEAD_DIM]
                            for g in range(GROUP)], axis=0)


def _attn_probs_t(kk, qs, mask, sink):
    s = _dot_nt(kk, qs) * (HEAD_DIM ** -0.5)
    s = jnp.where(mask, s, -1e30)
    m = jnp.maximum(jnp.max(s, axis=0, keepdims=True), sink)
    p = jnp.exp(s - m)
    esink = jnp.exp(sink - m)
    inv = 1.0 / (jnp.sum(p, axis=0, keepdims=True) + esink)
    return p * inv, esink * inv


def attn_fwd(q, kv, sinks_t, nseq, seq, name, comm=None):
    nb = seq // ATTN_BLOCK

    def body(q_ref, kv_ref, s_ref, o_ref, kvp):
        kvp[0:ATTN_BLOCK, :] = jnp.zeros((ATTN_BLOCK, 2 * KV_WIDTH), BF16)
        kvp[ATTN_BLOCK:, :] = kv_ref[...]

        def blk(n, carry):
            st = pl.multiple_of(n * ATTN_BLOCK, ATTN_BLOCK)
            qb = q_ref[pl.ds(st, ATTN_BLOCK), :]
            kw = kvp[pl.ds(st, 2 * ATTN_BLOCK), :]
            mask = _attn_mask_t(n)
            for kh in range(N_KV_HEADS):
                kk = kw[:, kh * HEAD_DIM:(kh + 1) * HEAD_DIM]
                vv = kw[:, KV_WIDTH + kh * HEAD_DIM:KV_WIDTH + (kh + 1) * HEAD_DIM]
                probs, _ = _attn_probs_t(kk, _stack_heads(qb, kh), mask, s_ref[kh:kh + 1, :])
                ot = _dot_tn(vv, probs.astype(BF16))
                for pair in range(GROUP // 2):
                    two = jnp.concatenate([ot[:, (2 * pair) * ATTN_BLOCK:(2 * pair + 1) * ATTN_BLOCK],
                                           ot[:, (2 * pair + 1) * ATTN_BLOCK:(2 * pair + 2) * ATTN_BLOCK]], axis=0)
                    col = (kh * GROUP + 2 * pair) * HEAD_DIM
                    o_ref[pl.ds(st, ATTN_BLOCK), col:col + 2 * HEAD_DIM] = two.T.astype(o_ref.dtype)
            return carry

        lax.fori_loop(0, nb, blk, 0, unroll=4)

    return _pcall(
        body, name, (nseq,),
        [pl.BlockSpec((seq, ATTN_WIDTH), lambda b: (b, 0)),
         pl.BlockSpec((seq, 2 * KV_WIDTH), lambda b: (b, 0)),
         pl.BlockSpec((8, GQ), lambda b: (0, 0))],
        pl.BlockSpec((seq, ATTN_WIDTH), lambda b: (b, 0)),
        jax.ShapeDtypeStruct((nseq * seq, ATTN_WIDTH), BF16),
        [pltpu.VMEM((ATTN_BLOCK + seq, 2 * KV_WIDTH), BF16)], (q, kv, sinks_t), ("parallel",), comm)


def attn_bwd(q, kv, sinks_t, do, nseq, seq, name, comm=None):
    nb = seq // ATTN_BLOCK

    def body(q_ref, kv_ref, s_ref, do_ref, dq_ref, dkv_ref, ds_ref, kvp, dkvp, dsacc):
        @pl.when(pl.program_id(0) == 0)
        def _():
            dsacc[...] = jnp.zeros(dsacc.shape, F32)

        kvp[0:ATTN_BLOCK, :] = jnp.zeros((ATTN_BLOCK, 2 * KV_WIDTH), BF16)
        kvp[ATTN_BLOCK:, :] = kv_ref[...]
        dkvp[...] = jnp.zeros(dkvp.shape, F32)

        def blk(n, carry):
            st = pl.multiple_of(n * ATTN_BLOCK, ATTN_BLOCK)
            qb = q_ref[pl.ds(st, ATTN_BLOCK), :]
            dob = do_ref[pl.ds(st, ATTN_BLOCK), :]
            kw = kvp[pl.ds(st, 2 * ATTN_BLOCK), :]
            mask = _attn_mask_t(n)
            for kh in range(N_KV_HEADS):
                kk = kw[:, kh * HEAD_DIM:(kh + 1) * HEAD_DIM]
                vv = kw[:, KV_WIDTH + kh * HEAD_DIM:KV_WIDTH + (kh + 1) * HEAD_DIM]
                qs = _stack_heads(qb, kh)
                dos = _stack_heads(dob, kh)
                probs, psink = _attn_probs_t(kk, qs, mask, s_ref[kh:kh + 1, :])
                dp = _dot_nt(vv, dos)
                dv = _dot(probs.astype(BF16), dos)
                rowdot = jnp.sum(probs * dp, axis=0, keepdims=True)
                dsc = (probs * (dp - rowdot) * (HEAD_DIM ** -0.5)).astype(BF16)
                dsacc[kh:kh + 1, :] += -psink * rowdot
                dk = _dot(dsc, qs)
                dqs = _dot_tn(dsc, kk)
                for g in range(GROUP):
                    col = (kh * GROUP + g) * HEAD_DIM
                    dq_ref[pl.ds(st, ATTN_BLOCK), col:col + HEAD_DIM] = (
                        dqs[g * ATTN_BLOCK:(g + 1) * ATTN_BLOCK].astype(dq_ref.dtype))
                dkvp[pl.ds(st, 2 * ATTN_BLOCK), kh * HEAD_DIM:(kh + 1) * HEAD_DIM] += dk
                dkvp[pl.ds(st, 2 * ATTN_BLOCK), KV_WIDTH + kh * HEAD_DIM:KV_WIDTH + (kh + 1) * HEAD_DIM] += dv
            return carry

        lax.fori_loop(0, nb, blk, 0, unroll=2)
        dkv_ref[...] = dkvp[ATTN_BLOCK:, :].astype(dkv_ref.dtype)

        @pl.when(pl.program_id(0) == nseq - 1)
        def _():
            for kh in range(N_KV_HEADS):
                for g in range(GROUP):
                    tot = jnp.sum(dsacc[kh:kh + 1, g * ATTN_BLOCK:(g + 1) * ATTN_BLOCK], axis=1, keepdims=True)
                    ds_ref[kh * GROUP + g:kh * GROUP + g + 1, :] = jnp.broadcast_to(tot, (1, LANES))

    seq_q = pl.BlockSpec((seq, ATTN_WIDTH), lambda b: (b, 0))
    seq_kv = pl.BlockSpec((seq, 2 * KV_WIDTH), lambda b: (b, 0))
    return _pcall(
        body, name, (nseq,),
        [seq_q, seq_kv, pl.BlockSpec((8, GQ), lambda b: (0, 0)), seq_q],
        [seq_q, seq_kv, pl.BlockSpec((N_Q_HEADS, LANES), lambda b: (0, 0))],
        [jax.ShapeDtypeStruct((nseq * seq, ATTN_WIDTH), BF16),
         jax.ShapeDtypeStruct((nseq * seq, 2 * KV_WIDTH), BF16),
         jax.ShapeDtypeStruct((N_Q_HEADS, LANES), F32)],
        [pltpu.VMEM((ATTN_BLOCK + seq, 2 * KV_WIDTH), BF16),
         pltpu.VMEM((ATTN_BLOCK + seq, 2 * KV_WIDTH), F32),
         pltpu.VMEM((8, GQ), F32)], (q, kv, sinks_t, do), ("arbitrary",), comm)


CONV_T = 128


def _conv_taps(win, w_ref, lanes, init):
    acc = init
    for j in range(CONV_KERNEL):
        sh = win if j == CONV_KERNEL - 1 else pltpu.roll(win, CONV_KERNEL - 1 - j, 0)
        acc = acc + w_ref[j:j + 1, lanes] * sh[CONV_HALO:CONV_HALO + CONV_T]
    return acc


def _conv_block(h0p, w_ref, vec_ref, st):
    cols = []
    for cs in range(CONV_WIDTH // LANES):
        lanes = slice(cs * LANES, (cs + 1) * LANES)
        win = h0p[pl.ds(st, CONV_T + CONV_HALO), lanes]
        init = jnp.broadcast_to(vec_ref[0:1, lanes], (CONV_T, LANES))
        cols.append(_conv_taps(win, w_ref, lanes, init))
    return jnp.concatenate(cols, axis=-1)


def _glu_store(c_ref, h0p, st):
    cb = c_ref[pl.ds(st, CONV_T), :]
    h0p[pl.ds(pl.multiple_of(st + CONV_HALO, CONV_HALO), CONV_T), :] = cb[:, :CONV_WIDTH] * _sigmoid(cb[:, CONV_WIDTH:])


def conv_fwd(c, w, vec, nseq, seq, name, comm=None):
    nb = seq // CONV_T

    def body(c_ref, w_ref, vec_ref, o_ref, h1_ref, h0p):
        h0p[0:CONV_HALO, :] = jnp.zeros((CONV_HALO, CONV_WIDTH), F32)

        def blk(n, carry):
            st = pl.multiple_of(n * CONV_T, CONV_T)
            _glu_store(c_ref, h0p, st)
            h1 = _conv_block(h0p, w_ref, vec_ref, st)
            h1_ref[pl.ds(st, CONV_T), :] = h1
            mu = jnp.mean(h1, axis=-1, keepdims=True)
            xc = h1 - mu
            rstd = lax.rsqrt(jnp.mean(xc * xc, axis=-1, keepdims=True) + EPS)
            y = xc * rstd * vec_ref[1:2, :] + vec_ref[2:3, :]
            o_ref[pl.ds(st, CONV_T), :] = (y * _sigmoid(y)).astype(o_ref.dtype)
            return carry

        lax.fori_loop(0, nb, blk, 0)

    return _pcall(
        body, name, (nseq,),
        [pl.BlockSpec((seq, 2 * CONV_WIDTH), lambda b: (b, 0)),
         pl.BlockSpec((32, CONV_WIDTH), lambda b: (0, 0)),
         pl.BlockSpec((8, CONV_WIDTH), lambda b: (0, 0))],
        [pl.BlockSpec((seq, CONV_WIDTH), lambda b: (b, 0))] * 2,
        [jax.ShapeDtypeStruct((nseq * seq, CONV_WIDTH), BF16), jax.ShapeDtypeStruct((nseq * seq, CONV_WIDTH), F32)],
        [pltpu.VMEM((CONV_HALO + seq, CONV_WIDTH), F32)], (c, w, vec), ("parallel",), comm)


def conv_bwd(c, h1_saved, w, vec, dout, nseq, seq, name, comm=None):
    nb = seq // CONV_T

    def body(c_ref, h1_ref, w_ref, vec_ref, do_ref, dc_ref, dw_ref, dvec_ref, h0p, dh1p, dwacc):
        @pl.when(pl.program_id(0) == 0)
        def _():
            dwacc[...] = jnp.zeros(dwacc.shape, F32)
            dvec_ref[...] = jnp.zeros(dvec_ref.shape, F32)

        h0p[0:CONV_HALO, :] = jnp.zeros((CONV_HALO, CONV_WIDTH), F32)
        dh1p[seq:seq + CONV_HALO, :] = jnp.zeros((CONV_HALO, CONV_WIDTH), F32)

        def pass_a(n, carry):
            st = pl.multiple_of(n * CONV_T, CONV_T)
            _glu_store(c_ref, h0p, st)
            h1 = h1_ref[pl.ds(st, CONV_T), :]
            mu = jnp.mean(h1, axis=-1, keepdims=True)
            xc = h1 - mu
            rstd = lax.rsqrt(jnp.mean(xc * xc, axis=-1, keepdims=True) + EPS)
            xh = xc * rstd
            y = xh * vec_ref[1:2, :] + vec_ref[2:3, :]
            sg = _sigmoid(y)
            dy = do_ref[pl.ds(st, CONV_T), :] * (sg * (1.0 + y * (1.0 - sg)))
            dvec_ref[1:2, :] += jnp.sum(dy * xh, axis=0, keepdims=True)
            dvec_ref[2:3, :] += jnp.sum(dy, axis=0, keepdims=True)
            dxh = dy * vec_ref[1:2, :]
            dh1 = rstd * (dxh - jnp.mean(dxh, axis=-1, keepdims=True)
                          - xh * jnp.mean(dxh * xh, axis=-1, keepdims=True))
            dvec_ref[0:1, :] += jnp.sum(dh1, axis=0, keepdims=True)
            dh1p[pl.ds(st, CONV_T), :] = dh1
            return carry

        lax.fori_loop(0, nb, pass_a, 0)

        def pass_b(n, carry):
            st = pl.multiple_of(n * CONV_T, CONV_T)
            cols = []
            for cs in range(CONV_WIDTH // LANES):
                lanes = slice(cs * LANES, (cs + 1) * LANES)
                wind = dh1p[pl.ds(st, CONV_T + CONV_HALO), lanes]
                winh = h0p[pl.ds(st, CONV_T + CONV_HALO), lanes]
                d1 = wind[0:CONV_T]
                acc = jnp.zeros((CONV_T, LANES), F32)
                for j in range(CONV_KERNEL):
                    acc = acc + w_ref[j:j + 1, lanes] * pltpu.roll(wind, 2 + j, 0)[CONV_HALO:CONV_HALO + CONV_T]
                    hs = winh if j == CONV_KERNEL - 1 else pltpu.roll(winh, CONV_KERNEL - 1 - j, 0)
                    prod = d1 * hs[CONV_HALO:CONV_HALO + CONV_T]
                    part = prod[0:8]
                    for r in range(8, CONV_T, 8):
                        part = part + prod[r:r + 8]
                    dwacc[8 * j:8 * j + 8, lanes] += part
                cols.append(acc)
            dh0 = jnp.concatenate(cols, axis=-1)
            cb = c_ref[pl.ds(st, CONV_T), :]
            av, gt = cb[:, :CONV_WIDTH], cb[:, CONV_WIDTH:]
            sg = _sigmoid(gt)
            dc_ref[pl.ds(st, CONV_T), :] = jnp.concatenate(
                [dh0 * sg, dh0 * av * sg * (1.0 - sg)], axis=-1).astype(dc_ref.dtype)
            return carry

        lax.fori_loop(0, nb, pass_b, 0)

        @pl.when(pl.program_id(0) == nseq - 1)
        def _():
            dw_ref[...] = jnp.zeros(dw_ref.shape, F32)
            for j in range(CONV_KERNEL):
                dw_ref[j:j + 1, :] = jnp.sum(dwacc[8 * j:8 * j + 8, :], axis=0, keepdims=True)

    return _pcall(
        body, name, (nseq,),
        [pl.BlockSpec((seq, 2 * CONV_WIDTH), lambda b: (b, 0)),
         pl.BlockSpec((seq, CONV_WIDTH), lambda b: (b, 0)),
         pl.BlockSpec((32, CONV_WIDTH), lambda b: (0, 0)),
         pl.BlockSpec((8, CONV_WIDTH), lambda b: (0, 0)),
         pl.BlockSpec((seq, CONV_WIDTH), lambda b: (b, 0))],
        [pl.BlockSpec((seq, 2 * CONV_WIDTH), lambda b: (b, 0)),
         pl.BlockSpec((32, CONV_WIDTH), lambda b: (0, 0)),
         pl.BlockSpec((8, CONV_WIDTH), lambda b: (0, 0))],
        [jax.ShapeDtypeStruct((nseq * seq, 2 * CONV_WIDTH), BF16),
         jax.ShapeDtypeStruct((32, CONV_WIDTH), F32),
         jax.ShapeDtypeStruct((8, CONV_WIDTH), F32)],
        [pltpu.VMEM((CONV_HALO + seq, CONV_WIDTH), F32),
         pltpu.VMEM((seq + CONV_HALO, CONV_WIDTH), F32),
         pltpu.VMEM((8 * 32, CONV_WIDTH), F32)], (c, h1_saved, w, vec, dout), ("arbitrary",), comm)


POOL_T = 128


def _pooled_block(zpp, st, grp):
    lanes = slice(grp * LANES, (grp + 1) * LANES)
    win = zpp[pl.ds(st, POOL_T + POOL_HALO), lanes]
    acc = win
    for lvl in range(grp + 1):
        acc = acc + pltpu.roll(acc, 1 << lvl, 0)
    t = st + lax.broadcasted_iota(jnp.int32, (POOL_T, 1), 0)
    inv = 1.0 / jnp.minimum(t + 1, POOL_WINDOWS[grp]).astype(F32)
    return acc[POOL_HALO:] * inv - win[POOL_HALO:], inv


def pool_fwd(zp, wp, scale, nseq, seq, name):
    nb = seq // POOL_T

    def body(z_ref, wp_ref, sc_ref, o_ref, zpp):
        zpp[0:POOL_HALO, :] = jnp.zeros((POOL_HALO, POOL_WIDTH), F32)
        zpp[POOL_HALO:, :] = z_ref[...]

        def blk(n, carry):
            st = pl.multiple_of(n * POOL_T, POOL_T)
            for grp in range(len(POOL_WINDOWS)):
                lanes = slice(grp * LANES, (grp + 1) * LANES)
                pooled, _ = _pooled_block(zpp, st, grp)
                o_ref[pl.ds(st, POOL_T), lanes] = (
                    _dot(pooled.astype(BF16), wp_ref[grp]) * sc_ref[0:1, lanes]).astype(o_ref.dtype)
            return carry

        lax.fori_loop(0, nb, blk, 0)

    return pl.pallas_call(
        body, name=name, grid=(nseq,),
        in_specs=[pl.BlockSpec((seq, POOL_WIDTH), lambda b: (b, 0)),
                  pl.BlockSpec((4, LANES, LANES), lambda b: (0, 0, 0)),
                  pl.BlockSpec((1, POOL_WIDTH), lambda b: (0, 0))],
        out_specs=pl.BlockSpec((seq, POOL_WIDTH), lambda b: (b, 0)),
        out_shape=jax.ShapeDtypeStruct((nseq * seq, POOL_WIDTH), BF16),
        scratch_shapes=[pltpu.VMEM((POOL_HALO + seq, POOL_WIDTH), F32)],
        compiler_params=_params("parallel"),
    )(zp, wp, scale)


def pool_bwd(zp, wp, scale, dout, nseq, seq, name, comm=None):
    nb = seq // POOL_T

    def body(z_ref, wp_ref, sc_ref, do_ref, dz_ref, dwp_ref, dsc_ref, zpp, dpcp, negd):
        @pl.when(pl.program_id(0) == 0)
        def _():
            dwp_ref[...] = jnp.zeros(dwp_ref.shape, F32)
            dsc_ref[...] = jnp.zeros(dsc_ref.shape, F32)

        zpp[0:POOL_HALO, :] = jnp.zeros((POOL_HALO, POOL_WIDTH), F32)
        zpp[POOL_HALO:, :] = z_ref[...]
        dpcp[seq:seq + POOL_HALO, :] = jnp.zeros((POOL_HALO, POOL_WIDTH), F32)

        def pass_a(n, carry):
            st = pl.multiple_of(n * POOL_T, POOL_T)
            for grp in range(len(POOL_WINDOWS)):
                lanes = slice(grp * LANES, (grp + 1) * LANES)
                pooled, inv = _pooled_block(zpp, st, grp)
                pb = pooled.astype(BF16)
                dob = do_ref[pl.ds(st, POOL_T), lanes]
                dsc_ref[0:1, lanes] += jnp.sum(dob * _dot(pb, wp_ref[grp]), axis=0, keepdims=True)
                dpm = (dob * sc_ref[0:1, lanes]).astype(BF16)
                dwp_ref[grp] += _dot_tn(pb, dpm)
                dpooled = _dot_nt(dpm, wp_ref[grp])
                negd[pl.ds(st, POOL_T), lanes] = -dpooled
                dpcp[pl.ds(st, POOL_T), lanes] = dpooled * inv
            return carry

        lax.fori_loop(0, nb, pass_a, 0)

        def pass_b(n, carry):
            st = pl.multiple_of(n * POOL_T, POOL_T)
            rows = POOL_T + POOL_HALO
            for grp in range(len(POOL_WINDOWS)):
                lanes = slice(grp * LANES, (grp + 1) * LANES)
                acc = dpcp[pl.ds(st, rows), lanes]
                for lvl in range(grp + 1):
                    acc = acc + pltpu.roll(acc, rows - (1 << lvl), 0)
                dz_ref[pl.ds(st, POOL_T), lanes] = (acc[0:POOL_T] + negd[pl.ds(st, POOL_T), lanes]).astype(dz_ref.dtype)
            return carry

        lax.fori_loop(0, nb, pass_b, 0)

    seq_spec = pl.BlockSpec((seq, POOL_WIDTH), lambda b: (b, 0))
    return _pcall(
        body, name, (nseq,),
        [seq_spec, pl.BlockSpec((4, LANES, LANES), lambda b: (0, 0, 0)),
         pl.BlockSpec((1, POOL_WIDTH), lambda b: (0, 0)), seq_spec],
        [seq_spec, pl.BlockSpec((4, LANES, LANES), lambda b: (0, 0, 0)),
         pl.BlockSpec((8, POOL_WIDTH), lambda b: (0, 0))],
        [jax.ShapeDtypeStruct((nseq * seq, POOL_WIDTH), BF16),
         jax.ShapeDtypeStruct((4, LANES, LANES), F32),
         jax.ShapeDtypeStruct((8, POOL_WIDTH), F32)],
        [pltpu.VMEM((POOL_HALO + seq, POOL_WIDTH), F32),
         pltpu.VMEM((seq + POOL_HALO, POOL_WIDTH), F32),
         pltpu.VMEM((seq, POOL_WIDTH), F32)], (zp, wp, scale, dout), ("arbitrary",), comm)


GELU_C0 = 0.7978845608028654
GELU_C1 = 0.044715


def _gelu(xv):
    return xv * (0.5 * (1.0 + jnp.tanh(GELU_C0 * (xv + GELU_C1 * (xv * xv * xv)))))


def _gelu_grad(xv):
    t = jnp.tanh(GELU_C0 * (xv + GELU_C1 * (xv * xv * xv)))
    return 0.5 * (1.0 + t) + 0.5 * xv * (1.0 - t * t) * (GELU_C0 * (1.0 + 3.0 * GELU_C1 * xv * xv))


def _tril():
    ti = lax.broadcasted_iota(jnp.int32, (SGU_CHUNK, SGU_CHUNK), 0)
    si = lax.broadcasted_iota(jnp.int32, (SGU_CHUNK, SGU_CHUNK), 1)
    return si <= ti


def sgu_fwd(zs, ws, bst, ln, nseq, seq, name):
    nc = seq // SGU_CHUNK

    def body(z_ref, ws_ref, bs_ref, ln_ref, o_ref):
        tril = _tril()

        def blk(n, carry):
            st = pl.multiple_of(n * SGU_CHUNK, SGU_CHUNK)
            ge = _gelu(z_ref[pl.ds(st, SGU_CHUNK), :])
            uu, vv = ge[:, :SGU_WIDTH], ge[:, SGU_WIDTH:]
            mu = jnp.mean(vv, axis=-1, keepdims=True)
            xc = vv - mu
            rstd = lax.rsqrt(jnp.mean(xc * xc, axis=-1, keepdims=True) + EPS)
            vn = (xc * rstd * ln_ref[0:1, :] + ln_ref[1:2, :]).astype(BF16)
            for g in range(4):
                lanes = slice(g * LANES, (g + 1) * LANES)
                wm = jnp.where(tril, ws_ref[g], 0.0).astype(BF16)
                mixed = _dot(wm, vn[:, lanes]) + bs_ref[:, g:g + 1]
                o_ref[pl.ds(st, SGU_CHUNK), lanes] = (uu[:, lanes] * mixed).astype(o_ref.dtype)
            return carry

        lax.fori_loop(0, nc, blk, 0)

    return pl.pallas_call(
        body, name=name, grid=(nseq,),
        in_specs=[pl.BlockSpec((seq, 2 * SGU_WIDTH), lambda b: (b, 0)),
                  pl.BlockSpec((4, LANES, LANES), lambda b: (0, 0, 0)),
                  pl.BlockSpec((SGU_CHUNK, 4), lambda b: (0, 0)),
                  pl.BlockSpec((8, SGU_WIDTH), lambda b: (0, 0))],
        out_specs=pl.BlockSpec((seq, SGU_WIDTH), lambda b: (b, 0)),
        out_shape=jax.ShapeDtypeStruct((nseq * seq, SGU_WIDTH), BF16),
        compiler_params=_params("parallel"),
    )(zs, ws, bst, ln)


def sgu_bwd(zs, ws, bst, ln, dout, nseq, seq, name, comm=None):
    nc = seq // SGU_CHUNK

    def body(z_ref, ws_ref, bs_ref, ln_ref, do_ref, dz_ref, dws_ref, dbs_ref, dln_ref):
        @pl.when(pl.program_id(0) == 0)
        def _():
            dws_ref[...] = jnp.zeros(dws_ref.shape, F32)
            dbs_ref[...] = jnp.zeros(dbs_ref.shape, F32)
            dln_ref[...] = jnp.zeros(dln_ref.shape, F32)

        tril = _tril()

        def blk(n, carry):
            st = pl.multiple_of(n * SGU_CHUNK, SGU_CHUNK)
            zv = z_ref[pl.ds(st, SGU_CHUNK), :]
            ge = _gelu(zv)
            uu, vv = ge[:, :SGU_WIDTH], ge[:, SGU_WIDTH:]
            mu = jnp.mean(vv, axis=-1, keepdims=True)
            xc = vv - mu
            rstd = lax.rsqrt(jnp.mean(xc * xc, axis=-1, keepdims=True) + EPS)
            xh = xc * rstd
            vn = (xh * ln_ref[0:1, :] + ln_ref[1:2, :]).astype(BF16)
            dob = do_ref[pl.ds(st, SGU_CHUNK), :]
            du_cols, dvn_cols = [], []
            for g in range(4):
                lanes = slice(g * LANES, (g + 1) * LANES)
                wm = jnp.where(tril, ws_ref[g], 0.0).astype(BF16)
                mixed = _dot(wm, vn[:, lanes]) + bs_ref[:, g:g + 1]
                du_cols.append(dob[:, lanes] * mixed)
                dmix = dob[:, lanes] * uu[:, lanes]
                dbs_ref[g] += jnp.broadcast_to(jnp.sum(dmix, axis=-1, keepdims=True), (SGU_CHUNK, LANES))
                dmb = dmix.astype(BF16)
                dws_ref[g] += jnp.where(tril, _dot_nt(dmb, vn[:, lanes]), 0.0)
                dvn_cols.append(_dot_tn(wm, dmb))
            dvn = jnp.concatenate(dvn_cols, axis=-1)
            dln_ref[0:1, :] += jnp.sum(dvn * xh, axis=0, keepdims=True)
            dln_ref[1:2, :] += jnp.sum(dvn, axis=0, keepdims=True)
            dxh = dvn * ln_ref[0:1, :]
            dv = rstd * (dxh - jnp.mean(dxh, axis=-1, keepdims=True)
                         - xh * jnp.mean(dxh * xh, axis=-1, keepdims=True))
            dge = jnp.concatenate(du_cols + [dv], axis=-1)
            dz_ref[pl.ds(st, SGU_CHUNK), :] = (dge * _gelu_grad(zv)).astype(dz_ref.dtype)
            return carry

        lax.fori_loop(0, nc, blk, 0)

    w_spec = pl.BlockSpec((4, LANES, LANES), lambda b: (0, 0, 0))
    ln_spec = pl.BlockSpec((8, SGU_WIDTH), lambda b: (0, 0))
    return _pcall(
        body, name, (nseq,),
        [pl.BlockSpec((seq, 2 * SGU_WIDTH), lambda b: (b, 0)), w_spec,
         pl.BlockSpec((SGU_CHUNK, 4), lambda b: (0, 0)), ln_spec,
         pl.BlockSpec((seq, SGU_WIDTH), lambda b: (b, 0))],
        [pl.BlockSpec((seq, 2 * SGU_WIDTH), lambda b: (b, 0)), w_spec, w_spec, ln_spec],
        [jax.ShapeDtypeStruct((nseq * seq, 2 * SGU_WIDTH), BF16),
         jax.ShapeDtypeStruct((4, LANES, LANES), F32),
         jax.ShapeDtypeStruct((4, LANES, LANES), F32),
         jax.ShapeDtypeStruct((8, SGU_WIDTH), F32)],
        [], (zs, ws, bst, ln, dout), ("arbitrary",), comm)


def _ew_rows(rows, cols, nbuf):
    t = _row_tile(rows, 1024)
    while t > 8 and t * cols * 4 * nbuf * 2 > 24 * 2**20:
        t //= 2
    return t


def adamw(w, g, m, v, name):
    layers, rows, cols = w.shape
    tr = _ew_rows(rows, cols, 7)

    def body(w_ref, g_ref, m_ref, v_ref, d_ref, mo_ref, vo_ref):
        gv = g_ref[...]
        mn = ADAM_B1 * m_ref[...] + (1.0 - ADAM_B1) * gv
        vn = ADAM_B2 * v_ref[...] + (1.0 - ADAM_B2) * (gv * gv)
        m_hat = mn / (1.0 - ADAM_B1 ** ADAM_STEP)
        v_hat = vn / (1.0 - ADAM_B2 ** ADAM_STEP)
        d_ref[...] = -ADAM_LR * (m_hat / (jnp.sqrt(v_hat) + ADAM_EPS) + ADAM_WD * w_ref[...])
        mo_ref[...] = mn
        vo_ref[...] = vn

    spec = pl.BlockSpec((1, tr, cols), lambda l, i: (l, i, 0))
    return pl.pallas_call(
        body, name=name, grid=(layers, rows // tr),
        in_specs=[spec] * 4, out_specs=[spec] * 3,
        out_shape=[jax.ShapeDtypeStruct(w.shape, F32)] * 3,
        compiler_params=_params("parallel", "parallel"),
    )(w, g, m, v)


def adamw_many(ws, gs, ms, vs, name):
    n = len(ws)

    def body(*refs):
        w_refs, g_refs, m_refs, v_refs = refs[:n], refs[n:2 * n], refs[2 * n:3 * n], refs[3 * n:4 * n]
        d_refs, mo_refs, vo_refs = refs[4 * n:5 * n], refs[5 * n:6 * n], refs[6 * n:7 * n]
        for i in range(n):
            gv = g_refs[i][...]
            mn = ADAM_B1 * m_refs[i][...] + (1.0 - ADAM_B1) * gv
            vn = ADAM_B2 * v_refs[i][...] + (1.0 - ADAM_B2) * (gv * gv)
            m_hat = mn / (1.0 - ADAM_B1 ** ADAM_STEP)
            v_hat = vn / (1.0 - ADAM_B2 ** ADAM_STEP)
            d_refs[i][...] = -ADAM_LR * (m_hat / (jnp.sqrt(v_hat) + ADAM_EPS) + ADAM_WD * w_refs[i][...])
            mo_refs[i][...] = mn
            vo_refs[i][...] = vn

    vmem = pl.BlockSpec(memory_space=pltpu.VMEM)
    shapes = [jax.ShapeDtypeStruct(w.shape, F32) for w in ws]
    res = pl.pallas_call(
        body, name=name, in_specs=[vmem] * (4 * n), out_specs=[vmem] * (3 * n), out_shape=shapes * 3,
        compiler_params=pltpu.CompilerParams(vmem_limit_bytes=VMEM_LIMIT),
    )(*ws, *gs, *ms, *vs)
    return res[:n], res[n:2 * n], res[2 * n:]


def add_cast(a, b, name, dtype=BF16):
    nslab, rows, cols = a.shape
    tr = _ew_rows(rows, cols, 3)

    def body(a_ref, b_ref, o_ref):
        o_ref[...] = (a_ref[...] + b_ref[...]).astype(dtype)

    spec = pl.BlockSpec((1, tr, cols), lambda k, i: (k, i, 0))
    return pl.pallas_call(
        body, name=name, grid=(nslab, rows // tr),
        in_specs=[spec, spec], out_specs=spec,
        out_shape=jax.ShapeDtypeStruct(a.shape, dtype),
        compiler_params=_params("parallel", "parallel"),
    )(a, b)


def pair_sum(t, got, core, name):
    nslab, h, cols = got.shape
    tr = _ew_rows(h, cols, 3)
    nb = h // tr

    def body(c_ref, a_ref, b_ref, o_ref):
        o_ref[...] = (a_ref[...] + b_ref[...]).astype(BF16)

    spec = pl.BlockSpec((1, tr, cols), lambda k, i, c: (k, i, 0))
    return pl.pallas_call(
        body, name=name,
        grid_spec=pltpu.PrefetchScalarGridSpec(
            num_scalar_prefetch=1, grid=(nslab, nb),
            in_specs=[pl.BlockSpec((1, tr, cols), lambda k, i, c: (k, c[0] * nb + i, 0)), spec],
            out_specs=spec),
        out_shape=jax.ShapeDtypeStruct(got.shape, BF16),
        compiler_params=_params("parallel", "parallel"),
    )(core, t, got)


def sum_parts(parts, name, first=None):
    npart, rows, cols = parts.shape
    tr = _ew_rows(rows, cols, npart + 2)

    def body(*refs):
        p_ref, o_ref = refs[-2], refs[-1]
        acc = p_ref[0].astype(F32) if first is None else refs[0][...].astype(F32) + p_ref[0].astype(F32)
        for j in range(1, npart):
            acc = acc + p_ref[j].astype(F32)
        o_ref[...] = acc

    row = pl.BlockSpec((tr, cols), lambda i: (i, 0))
    ins = [parts] if first is None else [first, parts]
    return pl.pallas_call(
        body, name=name, grid=(rows // tr,),
        in_specs=([] if first is None else [row]) + [pl.BlockSpec((npart, tr, cols), lambda i: (0, i, 0))],
        out_specs=row,
        out_shape=jax.ShapeDtypeStruct((rows, cols), F32),
        compiler_params=_params("parallel"),
    )(*ins)


ANY = pl.BlockSpec(memory_space=pl.ANY)
MESH = pl.DeviceIdType.MESH


def _me():
    return lax.axis_index("x"), lax.axis_index("y"), lax.axis_index("c")


def _flip(pos, rel):
    return tuple(1 - p if f else p for p, f in zip(pos, rel))


SIBLING = (0, 0, 1)
OTHER_CHIPS = ((1, 0, 0), (0, 1, 0), (1, 1, 0))


def _chip_of(pos, rel=(0, 0, 0)):
    px, py, _ = _flip(pos, rel)
    return 2 * px + py


def allgather_blocks(shards, name):
    nt = len(shards)
    hs = [s.shape[0] // 2 for s in shards]

    def body(*refs):
        ins, outs = refs[:nt], refs[nt:2 * nt]
        send_sems, recv_sems, loc_sems = refs[2 * nt:]
        pos = _me()
        x, y, c = pos

        def block_id(rel):
            px, py, pc = _flip(pos, rel)
            return 4 * px + 2 * py + pc

        def copy(t, k, block_rel, to_rel, src=None):
            dst = outs[t].at[block_id(block_rel)]
            return pltpu.make_async_remote_copy(
                src_ref=dst if src is None else src, dst_ref=dst,
                send_sem=send_sems.at[t * 7 + k], recv_sem=recv_sems.at[t * 7 + k],
                device_id=_flip(pos, to_rel), device_id_type=MESH)

        own = [ins[t].at[pl.ds(c * hs[t], hs[t])] for t in range(nt)]
        mine = [pltpu.make_async_copy(own[t], outs[t].at[block_id((0, 0, 0))], loc_sems.at[t]) for t in range(nt)]
        for cp in mine:
            cp.start()
        first = []
        for t in range(nt):
            first.append(copy(t, 0, (0, 0, 0), SIBLING, src=own[t]))
            first += [copy(t, 1 + j, (0, 0, 0), rel, src=own[t]) for j, rel in enumerate(OTHER_CHIPS)]
        for cp in first:
            cp.start()
        passed = []
        for j, rel in enumerate(OTHER_CHIPS):
            for t in range(nt):
                copy(t, 1 + j, rel, (0, 0, 0)).wait_recv()
                fwd = copy(t, 4 + j, rel, SIBLING)
                fwd.start()
                passed.append(fwd)
        for t in range(nt):
            copy(t, 0, SIBLING, (0, 0, 0)).wait_recv()
            for j, rel in enumerate(OTHER_CHIPS):
                copy(t, 4 + j, (rel[0], rel[1], 1), (0, 0, 0)).wait_recv()
        for cp in first + passed:
            cp.wait_send()
        for cp in mine:
            cp.wait()

    return pl.pallas_call(
        body, name=name,
        in_specs=[ANY] * nt, out_specs=[ANY] * nt,
        out_shape=[jax.ShapeDtypeStruct((N_DEV, h, s.shape[1]), s.dtype) for h, s in zip(hs, shards)],
        scratch_shapes=[pltpu.SemaphoreType.DMA((7 * nt,)), pltpu.SemaphoreType.DMA((7 * nt,)),
                        pltpu.SemaphoreType.DMA((nt,))],
    )(*shards)


def _block_id(pos, rel=(0, 0, 0)):
    px, py, pc = _flip(pos, rel)
    return 4 * px + 2 * py + pc


def gather_first_hop(shards):
    hs = [s.shape[0] // 2 for s in shards]

    def plan(ins, outs, pos):
        me = _block_id(pos)
        remote = []
        for i, o, h in zip(ins, outs, hs):
            own = i.at[pl.ds(pos[2] * h, h)]
            remote += [(rel, own, o.at[me]) for rel in (SIBLING,) + OTHER_CHIPS]
        return remote

    return Comm(shards, [((N_DEV, h, s.shape[1]), s.dtype) for h, s in zip(hs, shards)], plan, 4 * len(shards))


def gather_second_hop(gathered):
    def plan(ins, outs, pos):
        remote = []
        for i, o in zip(ins, outs):
            for rel in OTHER_CHIPS:
                blk = _block_id(pos, rel)
                remote.append((SIBLING, i.at[blk], o.at[blk]))
        return remote

    return Comm(gathered, [(g.shape, g.dtype) for g in gathered], plan, 3 * len(gathered),
                aliases={i: i for i in range(len(gathered))})


def swap_comm(xs):
    def plan(ins, outs, pos):
        return [(SIBLING, i, o) for i, o in zip(ins, outs)]

    return Comm(list(xs), [(v.shape, v.dtype) for v in xs], plan, len(xs))


def give_half_comm(ts, plain=()):
    nt = len(ts)

    def plan(ins, outs, pos):
        remote = []
        for i, o in zip(ins[:nt], outs[:nt]):
            h = o.shape[1]
            remote.append((SIBLING, i.at[:, pl.ds((1 - pos[2]) * h, h)], o))
        return remote + [(SIBLING, i, o) for i, o in zip(ins[nt:], outs[nt:])]

    shapes = [((t.shape[0], t.shape[1] // 2, t.shape[2]), t.dtype) for t in ts] + [(v.shape, v.dtype) for v in plain]
    return Comm(list(ts) + list(plain), shapes, plan, nt + len(plain))


def chip_scatter_comm(xs, shared=None):
    nx = len(xs)

    def plan(ins, outs, pos):
        me = _chip_of(pos)
        remote = []
        for i, o in zip(ins[:nx], outs[:nx]):
            remote += [(rel, i.at[_chip_of(pos, rel)], o.at[j]) for j, rel in enumerate(OTHER_CHIPS)]
        if shared is not None:
            remote += [(rel, ins[nx], outs[nx].at[me]) for rel in OTHER_CHIPS]
        return remote

    shapes = [((3,) + v.shape[1:], v.dtype) for v in xs]
    if shared is not None:
        shapes.append(((N_CHIPS,) + shared.shape, shared.dtype))
    return Comm(list(xs) + ([] if shared is None else [shared]), shapes, plan, 3 * nx + (0 if shared is None else 3))


def run_comm(comm, name):
    return _pcall(lambda: None, name, (1,), [], [], [], [], (), ("arbitrary",), comm)[1]


PACK_ROWS = 256


def _pack(arrs):
    parts, layout = [], []
    row = 0
    for a in arrs:
        flat = a.reshape(-1).astype(F32)
        size = flat.shape[0]
        rows = -(-size // (8 * LANES)) * 8
        flat = jnp.pad(flat, (0, rows * LANES - size))
        parts.append(flat.reshape(rows, LANES))
        layout.append((row, rows, size, a.shape))
        row += rows
    if row % PACK_ROWS:
        parts.append(jnp.zeros((PACK_ROWS - row % PACK_ROWS, LANES), F32))
    return jnp.concatenate(parts, axis=0), layout


def _unpack(packed, layout):
    return [packed[r0:r0 + rows].reshape(-1)[:size].reshape(shape) for r0, rows, size, shape in layout]


SMALL_REPL = ['mix_norm', 'a_b_in', 'a_sinks', 'a_conv_b', 'a_cln_g', 'a_cln_b', 'c_w_pool', 'c_w_s', 'c_b_s',
              'ffn_norm', 'final_norm']
SMALL_SHARD = ['a_conv_w', 'c_pool_scale', 'c_sln_g', 'c_sln_b']
BIG = ['a_w_in', 'a_w_out', 'c_w_in', 'c_w_out', 'ffn_w_gate', 'ffn_w_up', 'ffn_w_down']
TRANSPOSED = ('a_w_in', 'ffn_w_gate', 'ffn_w_up')
BIG_COL_SHARDED = {'c_w_in'}


def _full_weight(name, g8):
    _, h, cols = g8.shape
    g4 = g8.reshape(N_CHIPS, 2 * h, cols)
    if name not in BIG_COL_SHARDED:
        return g4.reshape(-1, cols)
    return jnp.transpose(g4, (1, 0, 2)).reshape(2 * h, N_CHIPS * cols)


def _to_shard_major(name, f):
    if name not in BIG_COL_SHARDED:
        return f.reshape(N_CHIPS, f.shape[0] // N_CHIPS, f.shape[1])
    r, cfull = f.shape
    return jnp.transpose(f.reshape(r, N_CHIPS, cfull // N_CHIPS), (1, 0, 2))


def kernel(*args):
    a = dict(zip(IN_NAMES, args))
    bl, seq, _ = a['x'].shape
    n = bl * seq
    x = a['x'].reshape(n, D_MODEL)
    target = a['loss_target'].reshape(n, D_MODEL)
    xi, yi, ci = _me()
    chip = 2 * xi + yi

    shard = {'a_w_in': a['a_w_in'][0].T, 'a_w_out': a['a_w_out'][0], 'c_w_in': a['c_w_in'][0], 'c_w_out': a['c_w_out'][0]}
    for layer in range(2):
        shard['gate' + str(layer)] = a['ffn_w_gate'][layer].T
        shard['up' + str(layer)] = a['ffn_w_up'][layer].T
        shard['down' + str(layer)] = a['ffn_w_down'][layer]
    shard = {k: v.astype(BF16) for k, v in shard.items()}
    core = ci.astype(jnp.int32).reshape(1)
    block_id = 4 * xi + 2 * yi + ci

    def first_hop(*names):
        return gather_first_hop([shard[k] for k in names])

    def finish(name, g8):
        h = shard[name].shape[0] // 2
        own = lax.dynamic_slice_in_dim(shard[name], ci * h, h, axis=0)
        return _full_weight(name, lax.dynamic_update_slice_in_dim(g8, own[None], block_id, axis=0))

    a_w_in_t = _full_weight('a_w_in', allgather_blocks([shard['a_w_in']], "gather_a_w_in")[0])
    in0_width = a_w_in_t.shape[0]
    small_shard_pack, small_shard_layout = _pack([a[k] for k in SMALL_SHARD])
    hop_a = first_hop('a_w_out', 'c_w_out')
    hop_s = chip_scatter_comm([], shared=small_shard_pack)
    mix_norm, ffn_norm = a['mix_norm'], a['ffn_norm']
    (hn0, q, kv, cc), outs = norm_inproj(
        x, mix_norm[0:1], a_w_in_t, a['a_b_in'],
        [(0, ATTN_WIDTH), (ATTN_WIDTH, ATTN_WIDTH + 2 * KV_WIDTH), (ATTN_WIDTH + 2 * KV_WIDTH, in0_width)],
        [BF16, BF16, F32], "in_proj0", comm=hop_a + hop_s, w_transposed=True)
    got_a, (ss,) = hop_a.split(outs, hop_s)
    ss = lax.dynamic_update_slice_in_dim(ss, small_shard_pack[None], chip, axis=0)
    ss_full = []
    for r0, rows, size, shape in small_shard_layout:
        per_chip = ss[:, r0:r0 + rows].reshape(N_CHIPS, -1)[:, :size].reshape((N_CHIPS,) + shape)
        ss_full.append(jnp.concatenate([per_chip[k] for k in range(N_CHIPS)], axis=-1))
    a_conv_w, c_pool_scale, c_sln_g, c_sln_b = [v[0] for v in ss_full]

    conv_taps = jnp.pad(a_conv_w, ((0, 32 - CONV_KERNEL), (0, 0)))
    conv_vec = jnp.pad(jnp.stack([a['a_conv_b'][0], a['a_cln_g'][0], a['a_cln_b'][0]]), ((0, 5), (0, 0)))
    sinks_b = jnp.pad(jnp.repeat(a['a_sinks'][0].reshape(N_KV_HEADS, GROUP), ATTN_BLOCK, axis=1), ((0, 6), (0, 0)))
    w_pool_bf = a['c_w_pool'][0].astype(BF16)
    pool_scale = c_pool_scale.reshape(1, POOL_WIDTH)
    w_s = a['c_w_s'][0]
    b_s_t = a['c_b_s'][0].T
    sgu_ln = jnp.pad(jnp.stack([c_sln_g, c_sln_b]), ((0, 6), (0, 0)))
    final_norm = a['final_norm'].reshape(1, D_MODEL)

    hop_b, pass_a = first_hop('gate0', 'c_w_in'), gather_second_hop(got_a)
    attn, outs = attn_fwd(q, kv, sinks_b, bl, seq, "attn_fwd", comm=hop_b + pass_a)
    got_b, done = hop_b.split(outs, pass_a)
    a_w_out, c_w_out = finish('a_w_out', done[0]), finish('c_w_out', done[1])

    hop_c, pass_b = first_hop('up0', 'down0'), gather_second_hop(got_b)
    (conv, conv_h1), outs = conv_fwd(cc, conv_taps, conv_vec, bl, seq, "conv_fwd", comm=hop_c + pass_b)
    got_c, done = hop_c.split(outs, pass_b)
    wg0, c_w_in = finish('gate0', done[0]), finish('c_w_in', done[1])

    h1, done = out_proj(x, attn, conv, a_w_out, "out_proj0", comm=gather_second_hop(got_c))
    wu0, wd0 = finish('up0', done[0]), finish('down0', done[1])

    (hnf0, g0, u0), got_e = ffn_gate_up(h1, ffn_norm[0:1], wg0, wu0, "ffn_gate_up0",
                                        comm=first_hop('gate1', 'up1', 'down1'))

    h2, done = ffn_down(h1, g0, u0, wd0, "ffn_down0", comm=gather_second_hop(got_e))
    wg1, wu1, wd1 = finish('gate1', done[0]), finish('up1', done[1]), finish('down1', done[2])
    wg, wu, wd = [wg0, wg1], [wu0, wu1], [wd0, wd1]

    (hn1, zp, zs), _ = norm_inproj(
        h2, mix_norm[1:2], c_w_in, jnp.zeros((1, c_w_in.shape[1]), F32),
        [(0, POOL_WIDTH), (POOL_WIDTH, c_w_in.shape[1])], [F32, F32], "in_proj1")
    pool = pool_fwd(zp, w_pool_bf, pool_scale, bl, seq, "pool_fwd")
    sgu = sgu_fwd(zs, w_s, b_s_t, sgu_ln, bl, seq, "sgu_fwd")
    h3, _ = out_proj(h2, pool, sgu, c_w_out, "out_proj1")
    (hnf1, g1, u1), _ = ffn_gate_up(h3, ffn_norm[1:2], wg1, wu1, "ffn_gate_up1")
    h4, _ = ffn_down(h3, g1, u1, wd1, "ffn_down1")

    dh4, d_final_norm, loss_local = loss_head(h4, final_norm, target, "loss_head")

    grads = {}
    pieces = {}

    def slabs_of(names, fulls):
        return [_to_shard_major(k, fulls[k]) for k in names]

    def pair_sums_of(names, slabs, gots):
        return [pair_sum(t, gt, core, "pair_sum_" + k) for k, t, gt in zip(names, slabs, gots)]

    def chip_sums_of(names, sums, from_chips):
        own = [lax.dynamic_index_in_dim(p, chip, axis=0, keepdims=False) for p in sums]
        return [sum_parts(p, "chip_sum_" + k, first=o) for k, p, o in zip(names, from_chips, own)]

    (dg, du, act), _ = ffn_down_bwd(dh4, g1, u1, wd[1], "ffn_down_bwd1")
    full1 = {'down1': mm_tn(act, dh4, "dw_down1"), 'gate1': mm_tn(dg, hnf1, "dw_gate1"),
             'up1': mm_tn(du, hnf1, "dw_up1")}
    names1 = ['gate1', 'up1', 'down1']
    slabs1 = slabs_of(names1, full1)
    dh3, d_ffn_norm1, got1 = proj_rms_bwd([dg, du], [wg[1], wu[1]], h3, ffn_norm[1:2], dh4, 1, "ffn_up_bwd1",
                                          tm_pref=256, w_transposed=True, comm=give_half_comm(slabs1))
    sums1 = pair_sums_of(names1, slabs1, got1)
    d_pool, d_sgu = out_proj_bwd(dh3, c_w_out, [F32, F32], "out_proj_bwd1")
    full1['c_w_out'] = jnp.concatenate([mm_tn(pool, dh3, "dw_out1_pool"), mm_tn(sgu, dh3, "dw_out1_sgu")], axis=0)
    (dzp, d_w_pool, d_pool_scale), from_gate = pool_bwd(zp, w_pool_bf, pool_scale, d_pool, bl, seq, "pool_bwd",
                                                        comm=chip_scatter_comm(sums1[0:1]))
    (dzs, d_w_s, d_b_s_b, d_sgu_ln), from_up = sgu_bwd(zs, w_s, b_s_t, sgu_ln, d_sgu, bl, seq, "sgu_bwd",
                                                       comm=chip_scatter_comm(sums1[1:2]))
    full1['c_w_in'] = jnp.concatenate([mm_tn(hn1, dzp, "dw_in1_pool"), mm_tn(hn1, dzs, "dw_in1_sgu")], axis=1)
    names1b = ['c_w_out', 'c_w_in']
    slabs1b = slabs_of(names1b, full1)
    heavy_pack, heavy_layout = _pack([d_w_pool[None], d_w_s[None]])
    chips_down, pair1b = chip_scatter_comm(sums1[2:3]), give_half_comm(slabs1b, plain=[heavy_pack])
    dh2, d_mix_norm1, outs = proj_rms_bwd([dzp, dzs], [c_w_in[:, :POOL_WIDTH], c_w_in[:, POOL_WIDTH:]], h2,
                                          mix_norm[1:2], dh3, 1, "in_proj_bwd1", comm=chips_down + pair1b)
    from_down, got1b = chips_down.split(outs, pair1b)
    mine1 = chip_sums_of(names1, sums1, from_gate + from_up + from_down)
    sums1b = pair_sums_of(names1b, slabs1b, got1b[:2])
    heavy_pair = add_cast(heavy_pack[None], got1b[2][None], "pair_sum_heavy", dtype=F32)[0]

    join1, chips1b = swap_comm(mine1), chip_scatter_comm(sums1b, shared=heavy_pair)
    (dg, du, act), outs = ffn_down_bwd(dh2, g0, u0, wd[0], "ffn_down_bwd0", comm=join1 + chips1b)
    theirs1, from_chips1b = join1.split(outs, chips1b)
    pieces.update({k: (m, t) for k, m, t in zip(names1, mine1, theirs1)})
    mine1b = chip_sums_of(names1b, sums1b, from_chips1b[:2])
    heavy_chips = lax.dynamic_update_slice_in_dim(from_chips1b[2], heavy_pair[None], chip, axis=0)
    grads['c_w_pool'], grads['c_w_s'] = _unpack(sum_parts(heavy_chips, "heavy_sum"), heavy_layout)
    full0 = {'down0': mm_tn(act, dh2, "dw_down0"), 'gate0': mm_tn(dg, hnf0, "dw_gate0"),
             'up0': mm_tn(du, hnf0, "dw_up0")}
    names0 = ['gate0', 'up0', 'down0']
    slabs0 = slabs_of(names0, full0)
    join1b, pair0 = swap_comm(mine1b), give_half_comm(slabs0)
    dh1, d_ffn_norm0, outs = proj_rms_bwd([dg, du], [wg[0], wu[0]], h1, ffn_norm[0:1], dh2, 1, "ffn_up_bwd0",
                                          tm_pref=256, comm=join1b + pair0, w_transposed=True)
    theirs1b, got0 = join1b.split(outs, pair0)
    pieces.update({k: (m, t) for k, m, t in zip(names1b, mine1b, theirs1b)})
    sums0 = pair_sums_of(names0, slabs0, got0)

    d_attn, d_conv = out_proj_bwd(dh1, a_w_out, [BF16, F32], "out_proj_bwd0")
    full_o = {'a_w_out': jnp.concatenate([mm_tn(attn, dh1, "dw_out0_attn"), mm_tn(conv, dh1, "dw_out0_conv")], axis=0)}
    slabs_o = slabs_of(['a_w_out'], full_o)
    chips0, pair_o = chip_scatter_comm(sums0), give_half_comm(slabs_o)
    (dq, dkv, d_sinks_b), outs = attn_bwd(q, kv, sinks_b, d_attn, bl, seq, "attn_bwd", comm=chips0 + pair_o)
    from_chips0, got_o = chips0.split(outs, pair_o)
    mine0 = chip_sums_of(names0, sums0, from_chips0)
    sums_o = pair_sums_of(['a_w_out'], slabs_o, got_o)
    join0, chips_o = swap_comm(mine0), chip_scatter_comm(sums_o)
    (dcc, d_conv_taps, d_conv_vec), outs = conv_bwd(cc, conv_h1, conv_taps, conv_vec, d_conv, bl, seq, "conv_bwd",
                                                    comm=join0 + chips_o)
    theirs0, from_chips_o = join0.split(outs, chips_o)
    pieces.update({k: (m, t) for k, m, t in zip(names0, mine0, theirs0)})
    mine_o = chip_sums_of(['a_w_out'], sums_o, from_chips_o)
    kq, kk = ATTN_WIDTH, ATTN_WIDTH + 2 * KV_WIDTH
    grad_x, d_mix_norm0, _ = proj_rms_bwd([dq, dkv, dcc], [a_w_in_t[:kq], a_w_in_t[kq:kk], a_w_in_t[kk:]], x,
                                          mix_norm[0:1], dh1, 1, "in_proj_bwd0", w_transposed=True)
    dw_q, db_q = mm_tn(dq, hn0, "dw_in0_q", xsum=True)
    dw_kv, db_kv = mm_tn(dkv, hn0, "dw_in0_kv", xsum=True)
    (dw_c, db_c), theirs_o = mm_tn(dcc, hn0, "dw_in0_c", xsum=True, comm=swap_comm(mine_o))
    pieces['a_w_out'] = (mine_o[0], theirs_o[0])
    d_a_b_in = jnp.concatenate([db_q, db_kv, db_c], axis=0)
    slabs_i = slabs_of(['a_w_in'], {'a_w_in': jnp.concatenate([dw_q, dw_kv, dw_c], axis=0)})

    small_full = {
        'mix_norm': jnp.stack([d_mix_norm0, d_mix_norm1]), 'a_b_in': d_a_b_in[None], 'a_sinks': d_sinks_b[:, 0][None],
        'a_conv_w': d_conv_taps[:CONV_KERNEL][None], 'a_conv_b': d_conv_vec[0][None], 'a_cln_g': d_conv_vec[1][None],
        'a_cln_b': d_conv_vec[2][None], 'c_pool_scale': d_pool_scale[0][None],
        'c_sln_g': d_sgu_ln[0][None], 'c_sln_b': d_sgu_ln[1][None],
        'c_b_s': d_b_s_b[:, :, 0][None], 'ffn_norm': jnp.stack([d_ffn_norm0, d_ffn_norm1]),
        'final_norm': d_final_norm, 'loss': loss_local.reshape(1)}
    small_names = SMALL_REPL + SMALL_SHARD
    tail_names = [k for k in small_names if k in small_full] + ['loss']
    small_pack, small_layout = _pack([small_full[k] for k in tail_names])

    got_i, got_s = run_comm(give_half_comm(slabs_i, plain=[small_pack]), "tail_pair")
    sums_i = pair_sums_of(['a_w_in'], slabs_i, [got_i])
    small_pair = add_cast(small_pack[None], got_s[None], "pair_sum_small", dtype=F32)[0]
    outs = run_comm(chip_scatter_comm(sums_i, shared=small_pair), "tail_chips")
    mine_i = chip_sums_of(['a_w_in'], sums_i, outs[:1])
    small_chips = lax.dynamic_update_slice_in_dim(outs[1], small_pair[None], chip, axis=0)
    theirs_i = run_comm(swap_comm(mine_i), "tail_join")
    pieces['a_w_in'] = (mine_i[0], theirs_i[0])

    def whole(name):
        mine, theirs = pieces[name]
        return jnp.concatenate([jnp.where(ci == 0, mine, theirs), jnp.where(ci == 0, theirs, mine)], axis=0)

    for k in ('a_w_in', 'a_w_out', 'c_w_in', 'c_w_out'):
        grads[k] = whole(k)[None]
    for short, key in (('gate', 'ffn_w_gate'), ('up', 'ffn_w_up'), ('down', 'ffn_w_down')):
        grads[key] = jnp.stack([whole(short + '0'), whole(short + '1')])

    small_sum = sum_parts(small_chips, "small_sum")
    for k, g in zip(tail_names, _unpack(small_sum, small_layout)):
        if k in SMALL_SHARD:
            width = a[k].shape[-1]
            g = lax.dynamic_slice_in_dim(g, chip * width, width, axis=g.ndim - 1)
        grads[k] = g
    loss = grads.pop('loss')[0]

    delta, new_m, new_v = {}, {}, {}
    for k in BIG:
        if k in TRANSPOSED:
            flip = lambda t: jnp.swapaxes(t, 1, 2)
            d, m, v = adamw(flip(a[k]), grads[k], flip(a['m_' + k]), flip(a['v_' + k]), "adamw_" + k)
            grads[k], delta[k], new_m[k], new_v[k] = flip(grads[k]), flip(d), flip(m), flip(v)
        else:
            delta[k], new_m[k], new_v[k] = adamw(a[k], grads[k], a['m_' + k], a['v_' + k], "adamw_" + k)
    two_d = lambda t: t.reshape(1, -1) if t.ndim == 1 else t
    ds, ms, vs = adamw_many([two_d(a[k]) for k in small_names], [two_d(grads[k]) for k in small_names],
                            [two_d(a['m_' + k]) for k in small_names], [two_d(a['v_' + k]) for k in small_names],
                            "adamw_small")
    for k, dv, mv, vv in zip(small_names, ds, ms, vs):
        delta[k], new_m[k], new_v[k] = [t.reshape(a[k].shape) for t in (dv, mv, vv)]

    return (loss, grad_x.reshape(a['x'].shape), *[grads[k] for k in WEIGHTS], *[delta[k] for k in WEIGHTS],
            *[new_m[k] for k in WEIGHTS], *[new_v[k] for k in WEIGHTS])
```

```python
import functools

import jax
import jax.numpy as jnp
from jax import lax
from jax.experimental import pallas as pl
from jax.experimental.pallas import tpu as pltpu

F32 = jnp.float32
BF16 = jnp.bfloat16

D_MODEL = 1024
EPS = 1e-5
N_Q_HEADS, N_KV_HEADS, HEAD_DIM = 8, 2, 64
ATTN_BLOCK = 128
ATTN_WIDTH = N_Q_HEADS * HEAD_DIM
KV_WIDTH = N_KV_HEADS * HEAD_DIM
CONV_WIDTH = 512
CONV_KERNEL = 31
CONV_HALO = 32
POOL_WINDOWS = (2, 4, 8, 16)
POOL_WIDTH = 512
POOL_HALO = 16
SGU_WIDTH = 512
SGU_CHUNK = 128
D_FF = 2816
FF_CHUNK = 128
MXU_COLS = 256
LANES = 128
N_CHIPS = 4
N_DEV = 8

ADAM_LR, ADAM_B1, ADAM_B2, ADAM_EPS, ADAM_WD, ADAM_STEP = 0.001, 0.9, 0.999, 1e-08, 0.01, 10

VMEM_LIMIT = 56 * 2**20

WEIGHTS = ['mix_norm', 'a_w_in', 'a_b_in', 'a_sinks', 'a_conv_w', 'a_conv_b', 'a_cln_g', 'a_cln_b', 'a_w_out',
           'c_w_in', 'c_w_pool', 'c_pool_scale', 'c_sln_g', 'c_sln_b', 'c_w_s', 'c_b_s', 'c_w_out',
           'ffn_norm', 'ffn_w_gate', 'ffn_w_up', 'ffn_w_down', 'final_norm']
IN_NAMES = (['x'] + WEIGHTS + ['loss_target'] + ['m_' + n for n in WEIGHTS] + ['v_' + n for n in WEIGHTS])


def _params(*sem):
    return pltpu.CompilerParams(dimension_semantics=sem, vmem_limit_bytes=VMEM_LIMIT)


def _dot(a, b):
    return jnp.dot(a, b, preferred_element_type=F32)


def _dot_nt(a, b):
    return lax.dot_general(a, b, (((1,), (1,)), ((), ())), preferred_element_type=F32)


def _dot_tn(a, b):
    return lax.dot_general(a, b, (((0,), (0,)), ((), ())), preferred_element_type=F32)


def _sigmoid(v):
    return 0.5 * jnp.tanh(0.5 * v) + 0.5


def _row_tile(n, pref):
    t = min(n, pref)
    while n % t:
        t //= 2
    return t


def _col_tile(m, rows, budget=6 * 2**20):
    best = LANES
    for t in range(LANES, m + 1, LANES):
        if m % t == 0 and rows * t * 4 <= budget:
            best = t
    return best


class Comm:
    def __init__(self, ins, out_shapes, plan, count, aliases=None):
        self.ins, self.out_shapes, self.plan, self.count, self.aliases = ins, out_shapes, plan, count, aliases or {}

    def __add__(self, other):
        ni, no = len(self.ins), len(self.out_shapes)

        def plan(ins, outs, pos):
            return self.plan(ins[:ni], outs[:no], pos) + other.plan(ins[ni:], outs[no:], pos)

        aliases = dict(self.aliases)
        aliases.update({ni + i: no + o for i, o in other.aliases.items()})
        return Comm(list(self.ins) + list(other.ins), list(self.out_shapes) + list(other.out_shapes), plan,
                    self.count + other.count, aliases)

    def split(self, outs, other):
        return outs[:len(self.out_shapes)], outs[len(self.out_shapes):]


def _pcall(body, name, grid, in_specs, out_specs, out_shape, scratch_shapes, args, sem, comm=None):
    single = not isinstance(out_shape, (list, tuple))
    if single:
        out_specs, out_shape = [out_specs], [out_shape]
    if comm is None:
        res = pl.pallas_call(body, name=name, grid=grid, in_specs=in_specs, out_specs=list(out_specs),
                             out_shape=list(out_shape), scratch_shapes=list(scratch_shapes),
                             compiler_params=_params(*sem))(*args)
        return (res[0] if single else res), []
    na, nci, no, nco, ns = len(args), len(comm.ins), len(out_shape), len(comm.out_shapes), len(scratch_shapes)

    def wrapped(*refs):
        a_refs, ci_refs = refs[:na], refs[na:na + nci]
        o_refs, co_refs = refs[na + nci:na + nci + no], refs[na + nci + no:na + nci + no + nco]
        s_refs = refs[na + nci + no + nco:na + nci + no + nco + ns]
        send_sems, recv_sems = refs[-2], refs[-1]
        pos = _me()

        def copies():
            return [pltpu.make_async_remote_copy(src_ref=s, dst_ref=d, send_sem=send_sems.at[i],
                                                 recv_sem=recv_sems.at[i], device_id=_flip(pos, rel),
                                                 device_id_type=MESH)
                    for i, (rel, s, d) in enumerate(comm.plan(ci_refs, co_refs, pos))]

        first, last = None, None
        for d, size in enumerate(grid):
            f, l = pl.program_id(d) == 0, pl.program_id(d) == size - 1
            first = f if first is None else first & f
            last = l if last is None else last & l

        @pl.when(first)
        def _():
            for cp in copies():
                cp.start()

        body(*a_refs, *o_refs, *s_refs)

        @pl.when(last)
        def _():
            for cp in copies():
                cp.wait()

    res = pl.pallas_call(
        wrapped, name=name, grid=grid,
        in_specs=list(in_specs) + [ANY] * nci, out_specs=list(out_specs) + [ANY] * nco,
        out_shape=list(out_shape) + [jax.ShapeDtypeStruct(s, d) for s, d in comm.out_shapes],
        scratch_shapes=list(scratch_shapes) + [pltpu.SemaphoreType.DMA((comm.count,)),
                                               pltpu.SemaphoreType.DMA((comm.count,))],
        input_output_aliases={na + i: no + o for i, o in comm.aliases.items()},
        compiler_params=_params(*(["arbitrary"] * len(grid))),
    )(*args, *comm.ins)
    outs = res[:no]
    return (outs[0] if single else outs), list(res[no:])


def norm_inproj(x, gain, w, bias, splits, dtypes, name, comm=None, w_transposed=False):
    n = x.shape[0]
    m = w.shape[0] if w_transposed else w.shape[1]
    tm = _row_tile(n, 512)

    def body(x_ref, g_ref, w_ref, b_ref, hn_ref, *outs):
        xv = x_ref[...]
        r = lax.rsqrt(jnp.mean(xv * xv, axis=-1, keepdims=True) + EPS)
        hn = ((xv * r) * g_ref[...]).astype(BF16)
        hn_ref[...] = hn
        z = (_dot_nt if w_transposed else _dot)(hn, w_ref[...]) + b_ref[...]
        for o, (lo, hi) in zip(outs, splits):
            o[...] = z[:, lo:hi].astype(o.dtype)

    out_shape = [jax.ShapeDtypeStruct((n, D_MODEL), BF16)]
    out_specs = [pl.BlockSpec((tm, D_MODEL), lambda i: (i, 0))]
    for (lo, hi), dt in zip(splits, dtypes):
        out_shape.append(jax.ShapeDtypeStruct((n, hi - lo), dt))
        out_specs.append(pl.BlockSpec((tm, hi - lo), lambda i: (i, 0)))
    return _pcall(
        body, name, (n // tm,),
        [pl.BlockSpec((tm, D_MODEL), lambda i: (i, 0)),
         pl.BlockSpec((1, D_MODEL), lambda i: (0, 0)),
         pl.BlockSpec(w.shape, lambda i: (0, 0)),
         pl.BlockSpec((1, m), lambda i: (0, 0))],
        out_specs, out_shape, [], (x, gain, w, bias), ("parallel",), comm)


def out_proj(res, m1, m2, w, name, comm=None):
    n = res.shape[0]
    k1, k2 = m1.shape[1], m2.shape[1]
    assert k1 == k2
    tm = _row_tile(n, 512)

    def body(r_ref, a_ref, b_ref, w1_ref, w2_ref, o_ref):
        o_ref[...] = r_ref[...] + _dot(a_ref[...], w1_ref[...]) + _dot(b_ref[...], w2_ref[...])

    return _pcall(
        body, name, (n // tm,),
        [pl.BlockSpec((tm, D_MODEL), lambda i: (i, 0)),
         pl.BlockSpec((tm, k1), lambda i: (i, 0)),
         pl.BlockSpec((tm, k2), lambda i: (i, 0)),
         pl.BlockSpec((k1, D_MODEL), lambda i: (0, 0)),
         pl.BlockSpec((k2, D_MODEL), lambda i: (1, 0))],
        pl.BlockSpec((tm, D_MODEL), lambda i: (i, 0)),
        jax.ShapeDtypeStruct((n, D_MODEL), F32), [], (res, m1, m2, w, w), ("parallel",), comm)


def ffn_gate_up(h, gain, wg_t, wu_t, name, comm=None):
    n = h.shape[0]
    tm = _row_tile(n, 1024)
    th = D_FF // 2

    def body(h_ref, g_ref, wg_ref, wu_ref, hn_ref, go_ref, uo_ref):
        @pl.when(pl.program_id(1) == 0)
        def _():
            xv = h_ref[...]
            r = lax.rsqrt(jnp.mean(xv * xv, axis=-1, keepdims=True) + EPS)
            hn_ref[...] = ((xv * r) * g_ref[...]).astype(BF16)

        hn = hn_ref[...]
        go_ref[...] = _dot_nt(hn, wg_ref[...]).astype(BF16)
        uo_ref[...] = _dot_nt(hn, wu_ref[...]).astype(BF16)

    return _pcall(
        body, name, (n // tm, D_FF // th),
        [pl.BlockSpec((tm, D_MODEL), lambda i, j: (i, 0)),
         pl.BlockSpec((1, D_MODEL), lambda i, j: (0, 0)),
         pl.BlockSpec((th, D_MODEL), lambda i, j: (j, 0)),
         pl.BlockSpec((th, D_MODEL), lambda i, j: (j, 0))],
        [pl.BlockSpec((tm, D_MODEL), lambda i, j: (i, 0)),
         pl.BlockSpec((tm, th), lambda i, j: (i, j)),
         pl.BlockSpec((tm, th), lambda i, j: (i, j))],
        [jax.ShapeDtypeStruct((n, D_MODEL), BF16),
         jax.ShapeDtypeStruct((n, D_FF), BF16),
         jax.ShapeDtypeStruct((n, D_FF), BF16)],
        [], (h, gain, wg_t, wu_t), ("parallel", "arbitrary"), comm)


def ffn_down(h, g, u, wd, name, comm=None):
    n = h.shape[0]
    tm = _row_tile(n, 512)

    def body(h_ref, g_ref, u_ref, w_ref, o_ref, a_ref):
        for c0 in range(0, D_FF, FF_CHUNK):
            gv = g_ref[:, c0:c0 + FF_CHUNK]
            a_ref[:, c0:c0 + FF_CHUNK] = gv * _sigmoid(gv) * u_ref[:, c0:c0 + FF_CHUNK]
        o_ref[...] = h_ref[...] + _dot(a_ref[...], w_ref[...])

    return _pcall(
        body, name, (n // tm,),
        [pl.BlockSpec((tm, D_MODEL), lambda i: (i, 0)),
         pl.BlockSpec((tm, D_FF), lambda i: (i, 0)),
         pl.BlockSpec((tm, D_FF), lambda i: (i, 0)),
         pl.BlockSpec((D_FF, D_MODEL), lambda i: (0, 0))],
        pl.BlockSpec((tm, D_MODEL), lambda i: (i, 0)),
        jax.ShapeDtypeStruct((n, D_MODEL), F32),
        [pltpu.VMEM((tm, D_FF), BF16)], (h, g, u, wd), ("parallel",), comm)


def ffn_down_bwd(dh, g, u, wd, name, comm=None):
    n = dh.shape[0]
    tm = _row_tile(n, 512)

    def body(dh_ref, g_ref, u_ref, w_ref, dg_ref, du_ref, a_ref):
        dhb = dh_ref[...].astype(BF16)
        for c0 in range(0, D_FF, MXU_COLS):
            cols = slice(c0, c0 + MXU_COLS)
            da = _dot_nt(dhb, w_ref[cols, :]).astype(BF16)
            gv, uv = g_ref[:, cols], u_ref[:, cols]
            sg = _sigmoid(gv)
            act = gv * sg
            dg_ref[:, cols] = (da * uv) * (sg + act * (1.0 - sg))
            du_ref[:, cols] = da * act
            a_ref[:, cols] = act * uv

    spec_h = pl.BlockSpec((tm, D_FF), lambda i: (i, 0))
    return _pcall(
        body, name, (n // tm,),
        [pl.BlockSpec((tm, D_MODEL), lambda i: (i, 0)), spec_h, spec_h,
         pl.BlockSpec((D_FF, D_MODEL), lambda i: (0, 0))],
        [spec_h, spec_h, spec_h], [jax.ShapeDtypeStruct((n, D_FF), BF16)] * 3,
        [], (dh, g, u, wd), ("parallel",), comm)


def mm_tn(x, dy, name, xsum=False, comm=None):
    n, k = x.shape
    m = dy.shape[1]
    tk = _col_tile(k, m)
    tt = _row_tile(n, 1024)

    def body(x_ref, dy_ref, o_ref, *rest):
        xt_ref = rest[-1]
        t = pl.program_id(1)
        xv = x_ref[...]
        xt_ref[...] = xv.astype(BF16).T
        part = _dot(xt_ref[...], dy_ref[...].astype(BF16))

        @pl.when(t == 0)
        def _():
            o_ref[...] = part

        @pl.when(t > 0)
        def _():
            o_ref[...] += part

        if xsum:
            cs = jnp.broadcast_to(jnp.sum(xv.astype(F32), axis=0, keepdims=True), rest[0].shape)

            @pl.when(t == 0)
            def _():
                rest[0][...] = cs

            @pl.when(t > 0)
            def _():
                rest[0][...] += cs

    out_shape = [jax.ShapeDtypeStruct((k, m), F32)]
    out_specs = [pl.BlockSpec((tk, m), lambda j, t: (j, 0))]
    if xsum:
        out_shape.append(jax.ShapeDtypeStruct((8, k), F32))
        out_specs.append(pl.BlockSpec((8, tk), lambda j, t: (0, j)))
    res, comm_outs = _pcall(
        body, name, (k // tk, n // tt),
        [pl.BlockSpec((tt, tk), lambda j, t: (t, j)),
         pl.BlockSpec((tt, m), lambda j, t: (t, 0))],
        out_specs, out_shape, [pltpu.VMEM((tk, tt), BF16)], (x, dy), ("arbitrary", "arbitrary"), comm)
    res = (res[0], res[1][0]) if xsum else res[0]
    return res if comm is None else (res, comm_outs)


def out_proj_bwd(dh, w, dtypes, name):
    n = dh.shape[0]
    k = w.shape[0]
    half = k // 2
    tm = _row_tile(n, 512)

    def body(dh_ref, w_ref, a_ref, b_ref):
        dm = _dot_nt(dh_ref[...].astype(BF16), w_ref[...])
        a_ref[...] = dm[:, :half].astype(a_ref.dtype)
        b_ref[...] = dm[:, half:].astype(b_ref.dtype)

    return pl.pallas_call(
        body, name=name, grid=(n // tm,),
        in_specs=[pl.BlockSpec((tm, D_MODEL), lambda i: (i, 0)),
                  pl.BlockSpec((k, D_MODEL), lambda i: (0, 0))],
        out_specs=[pl.BlockSpec((tm, half), lambda i: (i, 0))] * 2,
        out_shape=[jax.ShapeDtypeStruct((n, half), dtypes[0]), jax.ShapeDtypeStruct((n, half), dtypes[1])],
        compiler_params=_params("parallel"),
    )(dh, w)


def proj_rms_bwd(dys, ws, h_in, gain, dres, nk, name, tm_pref=512, comm=None, w_transposed=False):
    n = h_in.shape[0]
    npair = len(dys)
    tm = _row_tile(n, tm_pref)
    tks = [dy.shape[1] // nk for dy in dys]
    mm = _dot if w_transposed else _dot_nt

    def body(*refs):
        dy_refs = refs[:npair]
        w_refs = refs[npair:2 * npair]
        h_ref, g_ref, dr_ref, o_ref, dg_ref, acc_ref = refs[2 * npair:]
        i, k = pl.program_id(0), pl.program_id(1)
        part = mm(dy_refs[0][...], w_refs[0][...])
        for p in range(1, npair):
            part = part + mm(dy_refs[p][...], w_refs[p][...])

        @pl.when(k == 0)
        def _():
            acc_ref[...] = part

        @pl.when(k > 0)
        def _():
            acc_ref[...] += part

        @pl.when(k == nk - 1)
        def _():
            dhn = acc_ref[...]
            xv = h_ref[...]
            r = lax.rsqrt(jnp.mean(xv * xv, axis=-1, keepdims=True) + EPS)
            xh = xv * r
            uv = dhn * g_ref[...]
            o_ref[...] = dr_ref[...] + r * (uv - xh * jnp.mean(uv * xh, axis=-1, keepdims=True))
            dgp = jnp.broadcast_to(jnp.sum(dhn * xh, axis=0, keepdims=True), dg_ref.shape)

            @pl.when(i == 0)
            def _():
                dg_ref[...] = dgp

            @pl.when(i > 0)
            def _():
                dg_ref[...] += dgp

    row = pl.BlockSpec((tm, D_MODEL), lambda i, k: (i, 0))
    in_specs = [pl.BlockSpec((tm, tk), lambda i, k: (i, k)) for tk in tks]
    if w_transposed:
        in_specs += [pl.BlockSpec((tk, D_MODEL), lambda i, k: (k, 0)) for tk in tks]
    else:
        in_specs += [pl.BlockSpec((D_MODEL, tk), lambda i, k: (0, k)) for tk in tks]
    in_specs += [row, pl.BlockSpec((1, D_MODEL), lambda i, k: (0, 0)), row]
    (dh, dgain), comm_outs = _pcall(
        body, name, (n // tm, nk), in_specs,
        [row, pl.BlockSpec((8, D_MODEL), lambda i, k: (0, 0))],
        [jax.ShapeDtypeStruct((n, D_MODEL), F32), jax.ShapeDtypeStruct((8, D_MODEL), F32)],
        [pltpu.VMEM((tm, D_MODEL), F32)], (*dys, *ws, h_in, gain, dres), ("arbitrary", "arbitrary"), comm)
    return dh, dgain[0], comm_outs


def loss_head(h, gain, target, name):
    n = h.shape[0]
    tm = _row_tile(n, 512)

    def body(h_ref, g_ref, t_ref, dh_ref, dg_ref, l_ref):
        i = pl.program_id(0)
        xv = h_ref[...]
        r = lax.rsqrt(jnp.mean(xv * xv, axis=-1, keepdims=True) + EPS)
        xh = xv * r
        err = xh * g_ref[...] - t_ref[...]
        dy = err * (1.0 / D_MODEL)
        uv = dy * g_ref[...]
        dh_ref[...] = r * (uv - xh * jnp.mean(uv * xh, axis=-1, keepdims=True))
        dgp = jnp.broadcast_to(jnp.sum(dy * xh, axis=0, keepdims=True), dg_ref.shape)
        lp = jnp.sum(jnp.sum(err * err, axis=-1, keepdims=True), axis=0, keepdims=True) * (0.5 / D_MODEL)
        lp = jnp.broadcast_to(lp, l_ref.shape)

        @pl.when(i == 0)
        def _():
            dg_ref[...] = dgp
            l_ref[...] = lp

        @pl.when(i > 0)
        def _():
            dg_ref[...] += dgp
            l_ref[...] += lp

    row = pl.BlockSpec((tm, D_MODEL), lambda i: (i, 0))
    dh, dg, l = pl.pallas_call(
        body, name=name, grid=(n // tm,),
        in_specs=[row, pl.BlockSpec((1, D_MODEL), lambda i: (0, 0)), row],
        out_specs=[row, pl.BlockSpec((8, D_MODEL), lambda i: (0, 0)), pl.BlockSpec((8, LANES), lambda i: (0, 0))],
        out_shape=[jax.ShapeDtypeStruct((n, D_MODEL), F32), jax.ShapeDtypeStruct((8, D_MODEL), F32),
                   jax.ShapeDtypeStruct((8, LANES), F32)],
        compiler_params=_params("arbitrary"),
    )(h, gain, target)
    return dh, dg[0], l[0, 0]


GROUP = N_Q_HEADS // N_KV_HEADS
GQ = GROUP * ATTN_BLOCK


def _attn_mask_t(n):
    r = lax.broadcasted_iota(jnp.int32, (2 * ATTN_BLOCK, GQ), 0)
    qi = lax.broadcasted_iota(jnp.int32, (2 * ATTN_BLOCK, GQ), 1) & (ATTN_BLOCK - 1)
    band = (r > qi) & (r <= qi + ATTN_BLOCK)
    return band & ((r >= ATTN_BLOCK) | (n > 0))


def _stack_heads(blk, kh):
    return jnp.concatenate([blk[:, (kh * GROUP + g) * HEAD_DIM:(kh * GROUP + g + 1) * HEAD_DIM]
                            for g in range(GROUP)], axis=0)


def _attn_probs_t(kk, qs, mask, sink):
    s = _dot_nt(kk, qs) * (HEAD_DIM ** -0.5)
    s = jnp.where(mask, s, -1e30)
    m = jnp.maximum(jnp.max(s, axis=0, keepdims=True), sink)
    p = jnp.exp(s - m)
    esink = jnp.exp(sink - m)
    inv = 1.0 / (jnp.sum(p, axis=0, keepdims=True) + esink)
    return p * inv, esink * inv


def attn_fwd(q, kv, sinks_t, nseq, seq, name, comm=None):
    nb = seq // ATTN_BLOCK

    def body(q_ref, kv_ref, s_ref, o_ref, kvp):
        kvp[0:ATTN_BLOCK, :] = jnp.zeros((ATTN_BLOCK, 2 * KV_WIDTH), BF16)
        kvp[ATTN_BLOCK:, :] = kv_ref[...]

        def blk(n, carry):
            st = pl.multiple_of(n * ATTN_BLOCK, ATTN_BLOCK)
            qb = q_ref[pl.ds(st, ATTN_BLOCK), :]
            kw = kvp[pl.ds(st, 2 * ATTN_BLOCK), :]
            mask = _attn_mask_t(n)
            for kh in range(N_KV_HEADS):
                kk = kw[:, kh * HEAD_DIM:(kh + 1) * HEAD_DIM]
                vv = kw[:, KV_WIDTH + kh * HEAD_DIM:KV_WIDTH + (kh + 1) * HEAD_DIM]
                probs, _ = _attn_probs_t(kk, _stack_heads(qb, kh), mask, s_ref[kh:kh + 1, :])
                ot = _dot_tn(vv, probs.astype(BF16))
                for pair in range(GROUP // 2):
                    two = jnp.concatenate([ot[:, (2 * pair) * ATTN_BLOCK:(2 * pair + 1) * ATTN_BLOCK],
                                           ot[:, (2 * pair + 1) * ATTN_BLOCK:(2 * pair + 2) * ATTN_BLOCK]], axis=0)
                    col = (kh * GROUP + 2 * pair) * HEAD_DIM
                    o_ref[pl.ds(st, ATTN_BLOCK), col:col + 2 * HEAD_DIM] = two.T.astype(o_ref.dtype)
            return carry

        lax.fori_loop(0, nb, blk, 0, unroll=4)

    return _pcall(
        body, name, (nseq,),
        [pl.BlockSpec((seq, ATTN_WIDTH), lambda b: (b, 0)),
         pl.BlockSpec((seq, 2 * KV_WIDTH), lambda b: (b, 0)),
         pl.BlockSpec((8, GQ), lambda b: (0, 0))],
        pl.BlockSpec((seq, ATTN_WIDTH), lambda b: (b, 0)),
        jax.ShapeDtypeStruct((nseq * seq, ATTN_WIDTH), BF16),
        [pltpu.VMEM((ATTN_BLOCK + seq, 2 * KV_WIDTH), BF16)], (q, kv, sinks_t), ("parallel",), comm)


def attn_bwd(q, kv, sinks_t, do, nseq, seq, name, comm=None):
    nb = seq // ATTN_BLOCK

    def body(q_ref, kv_ref, s_ref, do_ref, dq_ref, dkv_ref, ds_ref, kvp, dkvp, dsacc):
        @pl.when(pl.program_id(0) == 0)
        def _():
            dsacc[...] = jnp.zeros(dsacc.shape, F32)

        kvp[0:ATTN_BLOCK, :] = jnp.zeros((ATTN_BLOCK, 2 * KV_WIDTH), BF16)
        kvp[ATTN_BLOCK:, :] = kv_ref[...]
        dkvp[...] = jnp.zeros(dkvp.shape, F32)

        def blk(n, carry):
            st = pl.multiple_of(n * ATTN_BLOCK, ATTN_BLOCK)
            qb = q_ref[pl.ds(st, ATTN_BLOCK), :]
            dob = do_ref[pl.ds(st, ATTN_BLOCK), :]
            kw = kvp[pl.ds(st, 2 * ATTN_BLOCK), :]
            mask = _attn_mask_t(n)
            for kh in range(N_KV_HEADS):
                kk = kw[:, kh * HEAD_DIM:(kh + 1) * HEAD_DIM]
                vv = kw[:, KV_WIDTH + kh * HEAD_DIM:KV_WIDTH + (kh + 1) * HEAD_DIM]
                qs = _stack_heads(qb, kh)
                dos = _stack_heads(dob, kh)
                probs, psink = _attn_probs_t(kk, qs, mask, s_ref[kh:kh + 1, :])
                dp = _dot_nt(vv, dos)
                dv = _dot(probs.astype(BF16), dos)
                rowdot = jnp.sum(probs * dp, axis=0, keepdims=True)
                dsc = (probs * (dp - rowdot) * (HEAD_DIM ** -0.5)).astype(BF16)
                dsacc[kh:kh + 1, :] += -psink * rowdot
                dk = _dot(dsc, qs)
                dqs = _dot_tn(dsc, kk)
                for g in range(GROUP):
                    col = (kh * GROUP + g) * HEAD_DIM
                    dq_ref[pl.ds(st, ATTN_BLOCK), col:col + HEAD_DIM] = (
                        dqs[g * ATTN_BLOCK:(g + 1) * ATTN_BLOCK].astype(dq_ref.dtype))
                dkvp[pl.ds(st, 2 * ATTN_BLOCK), kh * HEAD_DIM:(kh + 1) * HEAD_DIM] += dk
                dkvp[pl.ds(st, 2 * ATTN_BLOCK), KV_WIDTH + kh * HEAD_DIM:KV_WIDTH + (kh + 1) * HEAD_DIM] += dv
            return carry

        lax.fori_loop(0, nb, blk, 0, unroll=2)
        dkv_ref[...] = dkvp[ATTN_BLOCK:, :].astype(dkv_ref.dtype)

        @pl.when(pl.program_id(0) == nseq - 1)
        def _():
            for kh in range(N_KV_HEADS):
                for g in range(GROUP):
                    tot = jnp.sum(dsacc[kh:kh + 1, g * ATTN_BLOCK:(g + 1) * ATTN_BLOCK], axis=1, keepdims=True)
                    ds_ref[kh * GROUP + g:kh * GROUP + g + 1, :] = jnp.broadcast_to(tot, (1, LANES))

    seq_q = pl.BlockSpec((seq, ATTN_WIDTH), lambda b: (b, 0))
    seq_kv = pl.BlockSpec((seq, 2 * KV_WIDTH), lambda b: (b, 0))
    return _pcall(
        body, name, (nseq,),
        [seq_q, seq_kv, pl.BlockSpec((8, GQ), lambda b: (0, 0)), seq_q],
        [seq_q, seq_kv, pl.BlockSpec((N_Q_HEADS, LANES), lambda b: (0, 0))],
        [jax.ShapeDtypeStruct((nseq * seq, ATTN_WIDTH), BF16),
         jax.ShapeDtypeStruct((nseq * seq, 2 * KV_WIDTH), BF16),
         jax.ShapeDtypeStruct((N_Q_HEADS, LANES), F32)],
        [pltpu.VMEM((ATTN_BLOCK + seq, 2 * KV_WIDTH), BF16),
         pltpu.VMEM((ATTN_BLOCK + seq, 2 * KV_WIDTH), F32),
         pltpu.VMEM((8, GQ), F32)], (q, kv, sinks_t, do), ("arbitrary",), comm)


CONV_T = 128


SUBLANES = 8


def _shifted_rows(win):
    phases = [win] + [pltpu.roll(win, s, 0) for s in range(1, SUBLANES)]

    def shifted(s):
        lo = CONV_HALO - SUBLANES * (s // SUBLANES)
        return phases[s % SUBLANES][lo:lo + CONV_T]

    return shifted


def _conv_taps(win, w_ref, lanes, init):
    shifted = _shifted_rows(win)
    acc = init
    for j in range(CONV_KERNEL):
        acc = acc + w_ref[j:j + 1, lanes] * shifted(CONV_KERNEL - 1 - j)
    return acc


def _conv_block(h0p, w_ref, vec_ref, st):
    cols = []
    for cs in range(CONV_WIDTH // LANES):
        lanes = slice(cs * LANES, (cs + 1) * LANES)
        win = h0p[pl.ds(st, CONV_T + CONV_HALO), lanes]
        init = jnp.broadcast_to(vec_ref[0:1, lanes], (CONV_T, LANES))
        cols.append(_conv_taps(win, w_ref, lanes, init))
    return jnp.concatenate(cols, axis=-1)


def _glu_store(c_ref, h0p, st):
    cb = c_ref[pl.ds(st, CONV_T), :]
    h0p[pl.ds(pl.multiple_of(st + CONV_HALO, CONV_HALO), CONV_T), :] = cb[:, :CONV_WIDTH] * _sigmoid(cb[:, CONV_WIDTH:])


def conv_fwd(c, w, vec, nseq, seq, name, comm=None):
    nb = seq // CONV_T

    def body(c_ref, w_ref, vec_ref, o_ref, h1_ref, h0p):
        h0p[0:CONV_HALO, :] = jnp.zeros((CONV_HALO, CONV_WIDTH), F32)

        def blk(n, carry):
            st = pl.multiple_of(n * CONV_T, CONV_T)
            _glu_store(c_ref, h0p, st)
            h1 = _conv_block(h0p, w_ref, vec_ref, st)
            h1_ref[pl.ds(st, CONV_T), :] = h1
            mu = jnp.mean(h1, axis=-1, keepdims=True)
            xc = h1 - mu
            rstd = lax.rsqrt(jnp.mean(xc * xc, axis=-1, keepdims=True) + EPS)
            y = xc * rstd * vec_ref[1:2, :] + vec_ref[2:3, :]
            o_ref[pl.ds(st, CONV_T), :] = (y * _sigmoid(y)).astype(o_ref.dtype)
            return carry

        lax.fori_loop(0, nb, blk, 0)

    return _pcall(
        body, name, (nseq,),
        [pl.BlockSpec((seq, 2 * CONV_WIDTH), lambda b: (b, 0)),
         pl.BlockSpec((32, CONV_WIDTH), lambda b: (0, 0)),
         pl.BlockSpec((8, CONV_WIDTH), lambda b: (0, 0))],
        [pl.BlockSpec((seq, CONV_WIDTH), lambda b: (b, 0))] * 2,
        [jax.ShapeDtypeStruct((nseq * seq, CONV_WIDTH), BF16), jax.ShapeDtypeStruct((nseq * seq, CONV_WIDTH), F32)],
        [pltpu.VMEM((CONV_HALO + seq, CONV_WIDTH), F32)], (c, w, vec), ("parallel",), comm)


def conv_bwd(c, h1_saved, w, vec, dout, nseq, seq, name, comm=None):
    nb = seq // CONV_T

    def body(c_ref, h1_ref, w_ref, vec_ref, do_ref, dc_ref, dw_ref, dvec_ref, h0p, dh1p, dwacc):
        @pl.when(pl.program_id(0) == 0)
        def _():
            dwacc[...] = jnp.zeros(dwacc.shape, F32)
            dvec_ref[...] = jnp.zeros(dvec_ref.shape, F32)

        h0p[0:CONV_HALO, :] = jnp.zeros((CONV_HALO, CONV_WIDTH), F32)
        dh1p[seq:seq + CONV_HALO, :] = jnp.zeros((CONV_HALO, CONV_WIDTH), F32)

        def pass_a(n, carry):
            st = pl.multiple_of(n * CONV_T, CONV_T)
            _glu_store(c_ref, h0p, st)
            h1 = h1_ref[pl.ds(st, CONV_T), :]
            mu = jnp.mean(h1, axis=-1, keepdims=True)
            xc = h1 - mu
            rstd = lax.rsqrt(jnp.mean(xc * xc, axis=-1, keepdims=True) + EPS)
            xh = xc * rstd
            y = xh * vec_ref[1:2, :] + vec_ref[2:3, :]
            sg = _sigmoid(y)
            dy = do_ref[pl.ds(st, CONV_T), :] * (sg * (1.0 + y * (1.0 - sg)))
            dvec_ref[1:2, :] += jnp.sum(dy * xh, axis=0, keepdims=True)
            dvec_ref[2:3, :] += jnp.sum(dy, axis=0, keepdims=True)
            dxh = dy * vec_ref[1:2, :]
            dh1 = rstd * (dxh - jnp.mean(dxh, axis=-1, keepdims=True)
                          - xh * jnp.mean(dxh * xh, axis=-1, keepdims=True))
            dvec_ref[0:1, :] += jnp.sum(dh1, axis=0, keepdims=True)
            dh1p[pl.ds(st, CONV_T), :] = dh1
            return carry

        lax.fori_loop(0, nb, pass_a, 0)

        def pass_b(n, carry):
            st = pl.multiple_of(n * CONV_T, CONV_T)
            cols = []
            for cs in range(CONV_WIDTH // LANES):
                lanes = slice(cs * LANES, (cs + 1) * LANES)
                wind = dh1p[pl.ds(st, CONV_T + CONV_HALO), lanes]
                winh = h0p[pl.ds(st, CONV_T + CONV_HALO), lanes]
                d1 = wind[0:CONV_T]
                shifted_d, shifted_h = _shifted_rows(wind), _shifted_rows(winh)
                acc = jnp.zeros((CONV_T, LANES), F32)
                for j in range(CONV_KERNEL):
                    acc = acc + w_ref[j:j + 1, lanes] * shifted_d(2 + j)
                    prod = d1 * shifted_h(CONV_KERNEL - 1 - j)
                    part = prod[0:8]
                    for r in range(8, CONV_T, 8):
                        part = part + prod[r:r + 8]
                    dwacc[8 * j:8 * j + 8, lanes] += part
                cols.append(acc)
            dh0 = jnp.concatenate(cols, axis=-1)
            cb = c_ref[pl.ds(st, CONV_T), :]
            av, gt = cb[:, :CONV_WIDTH], cb[:, CONV_WIDTH:]
            sg = _sigmoid(gt)
            dc_ref[pl.ds(st, CONV_T), :] = jnp.concatenate(
                [dh0 * sg, dh0 * av * sg * (1.0 - sg)], axis=-1).astype(dc_ref.dtype)
            return carry

        lax.fori_loop(0, nb, pass_b, 0)

        @pl.when(pl.program_id(0) == nseq - 1)
        def _():
            dw_ref[...] = jnp.zeros(dw_ref.shape, F32)
            for j in range(CONV_KERNEL):
                dw_ref[j:j + 1, :] = jnp.sum(dwacc[8 * j:8 * j + 8, :], axis=0, keepdims=True)

    return _pcall(
        body, name, (nseq,),
        [pl.BlockSpec((seq, 2 * CONV_WIDTH), lambda b: (b, 0)),
         pl.BlockSpec((seq, CONV_WIDTH), lambda b: (b, 0)),
         pl.BlockSpec((32, CONV_WIDTH), lambda b: (0, 0)),
         pl.BlockSpec((8, CONV_WIDTH), lambda b: (0, 0)),
         pl.BlockSpec((seq, CONV_WIDTH), lambda b: (b, 0))],
        [pl.BlockSpec((seq, 2 * CONV_WIDTH), lambda b: (b, 0)),
         pl.BlockSpec((32, CONV_WIDTH), lambda b: (0, 0)),
         pl.BlockSpec((8, CONV_WIDTH), lambda b: (0, 0))],
        [jax.ShapeDtypeStruct((nseq * seq, 2 * CONV_WIDTH), BF16),
         jax.ShapeDtypeStruct((32, CONV_WIDTH), F32),
         jax.ShapeDtypeStruct((8, CONV_WIDTH), F32)],
        [pltpu.VMEM((CONV_HALO + seq, CONV_WIDTH), F32),
         pltpu.VMEM((seq + CONV_HALO, CONV_WIDTH), F32),
         pltpu.VMEM((8 * 32, CONV_WIDTH), F32)], (c, h1_saved, w, vec, dout), ("arbitrary",), comm)


POOL_T = 128


def _pooled_block(zpp, st, grp):
    lanes = slice(grp * LANES, (grp + 1) * LANES)
    win = zpp[pl.ds(st, POOL_T + POOL_HALO), lanes]
    acc = win
    for lvl in range(grp + 1):
        acc = acc + pltpu.roll(acc, 1 << lvl, 0)
    t = st + lax.broadcasted_iota(jnp.int32, (POOL_T, 1), 0)
    inv = 1.0 / jnp.minimum(t + 1, POOL_WINDOWS[grp]).astype(F32)
    return acc[POOL_HALO:] * inv - win[POOL_HALO:], inv


def pool_fwd(zp, wp, scale, nseq, seq, name):
    nb = seq // POOL_T

    def body(z_ref, wp_ref, sc_ref, o_ref, zpp):
        zpp[0:POOL_HALO, :] = jnp.zeros((POOL_HALO, POOL_WIDTH), F32)
        zpp[POOL_HALO:, :] = z_ref[...]

        def blk(n, carry):
            st = pl.multiple_of(n * POOL_T, POOL_T)
            for grp in range(len(POOL_WINDOWS)):
                lanes = slice(grp * LANES, (grp + 1) * LANES)
                pooled, _ = _pooled_block(zpp, st, grp)
                o_ref[pl.ds(st, POOL_T), lanes] = (
                    _dot(pooled.astype(BF16), wp_ref[grp]) * sc_ref[0:1, lanes]).astype(o_ref.dtype)
            return carry

        lax.fori_loop(0, nb, blk, 0)

    return pl.pallas_call(
        body, name=name, grid=(nseq,),
        in_specs=[pl.BlockSpec((seq, POOL_WIDTH), lambda b: (b, 0)),
                  pl.BlockSpec((4, LANES, LANES), lambda b: (0, 0, 0)),
                  pl.BlockSpec((1, POOL_WIDTH), lambda b: (0, 0))],
        out_specs=pl.BlockSpec((seq, POOL_WIDTH), lambda b: (b, 0)),
        out_shape=jax.ShapeDtypeStruct((nseq * seq, POOL_WIDTH), BF16),
        scratch_shapes=[pltpu.VMEM((POOL_HALO + seq, POOL_WIDTH), F32)],
        compiler_params=_params("parallel"),
    )(zp, wp, scale)


def pool_bwd(zp, wp, scale, dout, nseq, seq, name, comm=None):
    nb = seq // POOL_T

    def body(z_ref, wp_ref, sc_ref, do_ref, dz_ref, dwp_ref, dsc_ref, zpp, dpcp, negd):
        @pl.when(pl.program_id(0) == 0)
        def _():
            dwp_ref[...] = jnp.zeros(dwp_ref.shape, F32)
            dsc_ref[...] = jnp.zeros(dsc_ref.shape, F32)

        zpp[0:POOL_HALO, :] = jnp.zeros((POOL_HALO, POOL_WIDTH), F32)
        zpp[POOL_HALO:, :] = z_ref[...]
        dpcp[seq:seq + POOL_HALO, :] = jnp.zeros((POOL_HALO, POOL_WIDTH), F32)

        def pass_a(n, carry):
            st = pl.multiple_of(n * POOL_T, POOL_T)
            for grp in range(len(POOL_WINDOWS)):
                lanes = slice(grp * LANES, (grp + 1) * LANES)
                pooled, inv = _pooled_block(zpp, st, grp)
                pb = pooled.astype(BF16)
                dob = do_ref[pl.ds(st, POOL_T), lanes]
                dsc_ref[0:1, lanes] += jnp.sum(dob * _dot(pb, wp_ref[grp]), axis=0, keepdims=True)
                dpm = (dob * sc_ref[0:1, lanes]).astype(BF16)
                dwp_ref[grp] += _dot_tn(pb, dpm)
                dpooled = _dot_nt(dpm, wp_ref[grp])
                negd[pl.ds(st, POOL_T), lanes] = -dpooled
                dpcp[pl.ds(st, POOL_T), lanes] = dpooled * inv
            return carry

        lax.fori_loop(0, nb, pass_a, 0)

        def pass_b(n, carry):
            st = pl.multiple_of(n * POOL_T, POOL_T)
            rows = POOL_T + POOL_HALO
            for grp in range(len(POOL_WINDOWS)):
                lanes = slice(grp * LANES, (grp + 1) * LANES)
                acc = dpcp[pl.ds(st, rows), lanes]
                for lvl in range(grp + 1):
                    acc = acc + pltpu.roll(acc, rows - (1 << lvl), 0)
                dz_ref[pl.ds(st, POOL_T), lanes] = (acc[0:POOL_T] + negd[pl.ds(st, POOL_T), lanes]).astype(dz_ref.dtype)
            return carry

        lax.fori_loop(0, nb, pass_b, 0)

    seq_spec = pl.BlockSpec((seq, POOL_WIDTH), lambda b: (b, 0))
    return _pcall(
        body, name, (nseq,),
        [seq_spec, pl.BlockSpec((4, LANES, LANES), lambda b: (0, 0, 0)),
         pl.BlockSpec((1, POOL_WIDTH), lambda b: (0, 0)), seq_spec],
        [seq_spec, pl.BlockSpec((4, LANES, LANES), lambda b: (0, 0, 0)),
         pl.BlockSpec((8, POOL_WIDTH), lambda b: (0, 0))],
        [jax.ShapeDtypeStruct((nseq * seq, POOL_WIDTH), BF16),
         jax.ShapeDtypeStruct((4, LANES, LANES), F32),
         jax.ShapeDtypeStruct((8, POOL_WIDTH), F32)],
        [pltpu.VMEM((POOL_HALO + seq, POOL_WIDTH), F32),
         pltpu.VMEM((seq + POOL_HALO, POOL_WIDTH), F32),
         pltpu.VMEM((seq, POOL_WIDTH), F32)], (zp, wp, scale, dout), ("arbitrary",), comm)


GELU_C0 = 0.7978845608028654
GELU_C1 = 0.044715


def _gelu(xv):
    return xv * (0.5 * (1.0 + jnp.tanh(GELU_C0 * (xv + GELU_C1 * (xv * xv * xv)))))


def _gelu_grad(xv):
    t = jnp.tanh(GELU_C0 * (xv + GELU_C1 * (xv * xv * xv)))
    return 0.5 * (1.0 + t) + 0.5 * xv * (1.0 - t * t) * (GELU_C0 * (1.0 + 3.0 * GELU_C1 * xv * xv))


def _tril():
    ti = lax.broadcasted_iota(jnp.int32, (SGU_CHUNK, SGU_CHUNK), 0)
    si = lax.broadcasted_iota(jnp.int32, (SGU_CHUNK, SGU_CHUNK), 1)
    return si <= ti


def sgu_fwd(zs, ws, bst, ln, nseq, seq, name):
    nc = seq // SGU_CHUNK

    def body(z_ref, ws_ref, bs_ref, ln_ref, o_ref):
        tril = _tril()

        def blk(n, carry):
            st = pl.multiple_of(n * SGU_CHUNK, SGU_CHUNK)
            ge = _gelu(z_ref[pl.ds(st, SGU_CHUNK), :])
            uu, vv = ge[:, :SGU_WIDTH], ge[:, SGU_WIDTH:]
            mu = jnp.mean(vv, axis=-1, keepdims=True)
            xc = vv - mu
            rstd = lax.rsqrt(jnp.mean(xc * xc, axis=-1, keepdims=True) + EPS)
            vn = (xc * rstd * ln_ref[0:1, :] + ln_ref[1:2, :]).astype(BF16)
            for g in range(4):
                lanes = slice(g * LANES, (g + 1) * LANES)
                wm = jnp.where(tril, ws_ref[g], 0.0).astype(BF16)
                mixed = _dot(wm, vn[:, lanes]) + bs_ref[:, g:g + 1]
                o_ref[pl.ds(st, SGU_CHUNK), lanes] = (uu[:, lanes] * mixed).astype(o_ref.dtype)
            return carry

        lax.fori_loop(0, nc, blk, 0)

    return pl.pallas_call(
        body, name=name, grid=(nseq,),
        in_specs=[pl.BlockSpec((seq, 2 * SGU_WIDTH), lambda b: (b, 0)),
                  pl.BlockSpec((4, LANES, LANES), lambda b: (0, 0, 0)),
                  pl.BlockSpec((SGU_CHUNK, 4), lambda b: (0, 0)),
                  pl.BlockSpec((8, SGU_WIDTH), lambda b: (0, 0))],
        out_specs=pl.BlockSpec((seq, SGU_WIDTH), lambda b: (b, 0)),
        out_shape=jax.ShapeDtypeStruct((nseq * seq, SGU_WIDTH), BF16),
        compiler_params=_params("parallel"),
    )(zs, ws, bst, ln)


def sgu_bwd(zs, ws, bst, ln, dout, nseq, seq, name, comm=None):
    nc = seq // SGU_CHUNK

    def body(z_ref, ws_ref, bs_ref, ln_ref, do_ref, dz_ref, dws_ref, dbs_ref, dln_ref):
        @pl.when(pl.program_id(0) == 0)
        def _():
            dws_ref[...] = jnp.zeros(dws_ref.shape, F32)
            dbs_ref[...] = jnp.zeros(dbs_ref.shape, F32)
            dln_ref[...] = jnp.zeros(dln_ref.shape, F32)

        tril = _tril()

        def blk(n, carry):
            st = pl.multiple_of(n * SGU_CHUNK, SGU_CHUNK)
            zv = z_ref[pl.ds(st, SGU_CHUNK), :]
            ge = _gelu(zv)
            uu, vv = ge[:, :SGU_WIDTH], ge[:, SGU_WIDTH:]
            mu = jnp.mean(vv, axis=-1, keepdims=True)
            xc = vv - mu
            rstd = lax.rsqrt(jnp.mean(xc * xc, axis=-1, keepdims=True) + EPS)
            xh = xc * rstd
            vn = (xh * ln_ref[0:1, :] + ln_ref[1:2, :]).astype(BF16)
            dob = do_ref[pl.ds(st, SGU_CHUNK), :]
            du_cols, dvn_cols = [], []
            for g in range(4):
                lanes = slice(g * LANES, (g + 1) * LANES)
                wm = jnp.where(tril, ws_ref[g], 0.0).astype(BF16)
                mixed = _dot(wm, vn[:, lanes]) + bs_ref[:, g:g + 1]
                du_cols.append(dob[:, lanes] * mixed)
                dmix = dob[:, lanes] * uu[:, lanes]
                dbs_ref[g] += jnp.broadcast_to(jnp.sum(dmix, axis=-1, keepdims=True), (SGU_CHUNK, LANES))
                dmb = dmix.astype(BF16)
                dws_ref[g] += jnp.where(tril, _dot_nt(dmb, vn[:, lanes]), 0.0)
                dvn_cols.append(_dot_tn(wm, dmb))
            dvn = jnp.concatenate(dvn_cols, axis=-1)
            dln_ref[0:1, :] += jnp.sum(dvn * xh, axis=0, keepdims=True)
            dln_ref[1:2, :] += jnp.sum(dvn, axis=0, keepdims=True)
            dxh = dvn * ln_ref[0:1, :]
            dv = rstd * (dxh - jnp.mean(dxh, axis=-1, keepdims=True)
                         - xh * jnp.mean(dxh * xh, axis=-1, keepdims=True))
            dge = jnp.concatenate(du_cols + [dv], axis=-1)
            dz_ref[pl.ds(st, SGU_CHUNK), :] = (dge * _gelu_grad(zv)).astype(dz_ref.dtype)
            return carry

        lax.fori_loop(0, nc, blk, 0)

    w_spec = pl.BlockSpec((4, LANES, LANES), lambda b: (0, 0, 0))
    ln_spec = pl.BlockSpec((8, SGU_WIDTH), lambda b: (0, 0))
    return _pcall(
        body, name, (nseq,),
        [pl.BlockSpec((seq, 2 * SGU_WIDTH), lambda b: (b, 0)), w_spec,
         pl.BlockSpec((SGU_CHUNK, 4), lambda b: (0, 0)), ln_spec,
         pl.BlockSpec((seq, SGU_WIDTH), lambda b: (b, 0))],
        [pl.BlockSpec((seq, 2 * SGU_WIDTH), lambda b: (b, 0)), w_spec, w_spec, ln_spec],
        [jax.ShapeDtypeStruct((nseq * seq, 2 * SGU_WIDTH), BF16),
         jax.ShapeDtypeStruct((4, LANES, LANES), F32),
         jax.ShapeDtypeStruct((4, LANES, LANES), F32),
         jax.ShapeDtypeStruct((8, SGU_WIDTH), F32)],
        [], (zs, ws, bst, ln, dout), ("arbitrary",), comm)


def _ew_rows(rows, cols, nbuf):
    t = _row_tile(rows, 1024)
    while t > 8 and t * cols * 4 * nbuf * 2 > 24 * 2**20:
        t //= 2
    return t


def adamw(w, g, m, v, name):
    layers, rows, cols = w.shape
    tr = _ew_rows(rows, cols, 7)

    def body(w_ref, g_ref, m_ref, v_ref, d_ref, mo_ref, vo_ref):
        gv = g_ref[...]
        mn = ADAM_B1 * m_ref[...] + (1.0 - ADAM_B1) * gv
        vn = ADAM_B2 * v_ref[...] + (1.0 - ADAM_B2) * (gv * gv)
        m_hat = mn / (1.0 - ADAM_B1 ** ADAM_STEP)
        v_hat = vn / (1.0 - ADAM_B2 ** ADAM_STEP)
        d_ref[...] = -ADAM_LR * (m_hat / (jnp.sqrt(v_hat) + ADAM_EPS) + ADAM_WD * w_ref[...])
        mo_ref[...] = mn
        vo_ref[...] = vn

    spec = pl.BlockSpec((1, tr, cols), lambda l, i: (l, i, 0))
    return pl.pallas_call(
        body, name=name, grid=(layers, rows // tr),
        in_specs=[spec] * 4, out_specs=[spec] * 3,
        out_shape=[jax.ShapeDtypeStruct(w.shape, F32)] * 3,
        compiler_params=_params("parallel", "parallel"),
    )(w, g, m, v)


def adamw_many(ws, gs, ms, vs, name):
    n = len(ws)

    def body(*refs):
        w_refs, g_refs, m_refs, v_refs = refs[:n], refs[n:2 * n], refs[2 * n:3 * n], refs[3 * n:4 * n]
        d_refs, mo_refs, vo_refs = refs[4 * n:5 * n], refs[5 * n:6 * n], refs[6 * n:7 * n]
        for i in range(n):
            gv = g_refs[i][...]
            mn = ADAM_B1 * m_refs[i][...] + (1.0 - ADAM_B1) * gv
            vn = ADAM_B2 * v_refs[i][...] + (1.0 - ADAM_B2) * (gv * gv)
            m_hat = mn / (1.0 - ADAM_B1 ** ADAM_STEP)
            v_hat = vn / (1.0 - ADAM_B2 ** ADAM_STEP)
            d_refs[i][...] = -ADAM_LR * (m_hat / (jnp.sqrt(v_hat) + ADAM_EPS) + ADAM_WD * w_refs[i][...])
            mo_refs[i][...] = mn
            vo_refs[i][...] = vn

    vmem = pl.BlockSpec(memory_space=pltpu.VMEM)
    shapes = [jax.ShapeDtypeStruct(w.shape, F32) for w in ws]
    res = pl.pallas_call(
        body, name=name, in_specs=[vmem] * (4 * n), out_specs=[vmem] * (3 * n), out_shape=shapes * 3,
        compiler_params=pltpu.CompilerParams(vmem_limit_bytes=VMEM_LIMIT),
    )(*ws, *gs, *ms, *vs)
    return res[:n], res[n:2 * n], res[2 * n:]


def add_cast(a, b, name, dtype=BF16):
    nslab, rows, cols = a.shape
    tr = _ew_rows(rows, cols, 3)

    def body(a_ref, b_ref, o_ref):
        o_ref[...] = (a_ref[...] + b_ref[...]).astype(dtype)

    spec = pl.BlockSpec((1, tr, cols), lambda k, i: (k, i, 0))
    return pl.pallas_call(
        body, name=name, grid=(nslab, rows // tr),
        in_specs=[spec, spec], out_specs=spec,
        out_shape=jax.ShapeDtypeStruct(a.shape, dtype),
        compiler_params=_params("parallel", "parallel"),
    )(a, b)


def pair_sum(t, got, core, name):
    nslab, h, cols = got.shape
    tr = _ew_rows(h, cols, 3)
    nb = h // tr

    def body(c_ref, a_ref, b_ref, o_ref):
        o_ref[...] = (a_ref[...] + b_ref[...]).astype(BF16)

    spec = pl.BlockSpec((1, tr, cols), lambda k, i, c: (k, i, 0))
    return pl.pallas_call(
        body, name=name,
        grid_spec=pltpu.PrefetchScalarGridSpec(
            num_scalar_prefetch=1, grid=(nslab, nb),
            in_specs=[pl.BlockSpec((1, tr, cols), lambda k, i, c: (k, c[0] * nb + i, 0)), spec],
            out_specs=spec),
        out_shape=jax.ShapeDtypeStruct(got.shape, BF16),
        compiler_params=_params("parallel", "parallel"),
    )(core, t, got)


def sum_parts(parts, name, first=None):
    npart, rows, cols = parts.shape
    tr = _ew_rows(rows, cols, npart + 2)

    def body(*refs):
        p_ref, o_ref = refs[-2], refs[-1]
        acc = p_ref[0].astype(F32) if first is None else refs[0][...].astype(F32) + p_ref[0].astype(F32)
        for j in range(1, npart):
            acc = acc + p_ref[j].astype(F32)
        o_ref[...] = acc

    row = pl.BlockSpec((tr, cols), lambda i: (i, 0))
    ins = [parts] if first is None else [first, parts]
    return pl.pallas_call(
        body, name=name, grid=(rows // tr,),
        in_specs=([] if first is None else [row]) + [pl.BlockSpec((npart, tr, cols), lambda i: (0, i, 0))],
        out_specs=row,
        out_shape=jax.ShapeDtypeStruct((rows, cols), F32),
        compiler_params=_params("parallel"),
    )(*ins)


ANY = pl.BlockSpec(memory_space=pl.ANY)
MESH = pl.DeviceIdType.MESH


def _me():
    return lax.axis_index("x"), lax.axis_index("y"), lax.axis_index("c")


def _flip(pos, rel):
    return tuple(1 - p if f else p for p, f in zip(pos, rel))


SIBLING = (0, 0, 1)
OTHER_CHIPS = ((1, 0, 0), (0, 1, 0), (1, 1, 0))


def _chip_of(pos, rel=(0, 0, 0)):
    px, py, _ = _flip(pos, rel)
    return 2 * px + py


def allgather_blocks(shards, name):
    nt = len(shards)
    hs = [s.shape[0] // 2 for s in shards]

    def body(*refs):
        ins, outs = refs[:nt], refs[nt:2 * nt]
        send_sems, recv_sems, loc_sems = refs[2 * nt:]
        pos = _me()
        x, y, c = pos

        def block_id(rel):
            px, py, pc = _flip(pos, rel)
            return 4 * px + 2 * py + pc

        def copy(t, k, block_rel, to_rel, src=None):
            dst = outs[t].at[block_id(block_rel)]
            return pltpu.make_async_remote_copy(
                src_ref=dst if src is None else src, dst_ref=dst,
                send_sem=send_sems.at[t * 7 + k], recv_sem=recv_sems.at[t * 7 + k],
                device_id=_flip(pos, to_rel), device_id_type=MESH)

        own = [ins[t].at[pl.ds(c * hs[t], hs[t])] for t in range(nt)]
        mine = [pltpu.make_async_copy(own[t], outs[t].at[block_id((0, 0, 0))], loc_sems.at[t]) for t in range(nt)]
        for cp in mine:
            cp.start()
        first = []
        for t in range(nt):
            first.append(copy(t, 0, (0, 0, 0), SIBLING, src=own[t]))
            first += [copy(t, 1 + j, (0, 0, 0), rel, src=own[t]) for j, rel in enumerate(OTHER_CHIPS)]
        for cp in first:
            cp.start()
        passed = []
        for j, rel in enumerate(OTHER_CHIPS):
            for t in range(nt):
                copy(t, 1 + j, rel, (0, 0, 0)).wait_recv()
                fwd = copy(t, 4 + j, rel, SIBLING)
                fwd.start()
                passed.append(fwd)
        for t in range(nt):
            copy(t, 0, SIBLING, (0, 0, 0)).wait_recv()
            for j, rel in enumerate(OTHER_CHIPS):
                copy(t, 4 + j, (rel[0], rel[1], 1), (0, 0, 0)).wait_recv()
        for cp in first + passed:
            cp.wait_send()
        for cp in mine:
            cp.wait()

    return pl.pallas_call(
        body, name=name,
        in_specs=[ANY] * nt, out_specs=[ANY] * nt,
        out_shape=[jax.ShapeDtypeStruct((N_DEV, h, s.shape[1]), s.dtype) for h, s in zip(hs, shards)],
        scratch_shapes=[pltpu.SemaphoreType.DMA((7 * nt,)), pltpu.SemaphoreType.DMA((7 * nt,)),
                        pltpu.SemaphoreType.DMA((nt,))],
    )(*shards)


def _block_id(pos, rel=(0, 0, 0)):
    px, py, pc = _flip(pos, rel)
    return 4 * px + 2 * py + pc


def gather_first_hop(shards):
    hs = [s.shape[0] // 2 for s in shards]

    def plan(ins, outs, pos):
        me = _block_id(pos)
        remote = []
        for i, o, h in zip(ins, outs, hs):
            own = i.at[pl.ds(pos[2] * h, h)]
            remote += [(rel, own, o.at[me]) for rel in (SIBLING,) + OTHER_CHIPS]
        return remote

    return Comm(shards, [((N_DEV, h, s.shape[1]), s.dtype) for h, s in zip(hs, shards)], plan, 4 * len(shards))


def gather_second_hop(gathered):
    def plan(ins, outs, pos):
        remote = []
        for i, o in zip(ins, outs):
            for rel in OTHER_CHIPS:
                blk = _block_id(pos, rel)
                remote.append((SIBLING, i.at[blk], o.at[blk]))
        return remote

    return Comm(gathered, [(g.shape, g.dtype) for g in gathered], plan, 3 * len(gathered),
                aliases={i: i for i in range(len(gathered))})


def swap_comm(xs):
    def plan(ins, outs, pos):
        return [(SIBLING, i, o) for i, o in zip(ins, outs)]

    return Comm(list(xs), [(v.shape, v.dtype) for v in xs], plan, len(xs))


def give_half_comm(ts, plain=()):
    nt = len(ts)

    def plan(ins, outs, pos):
        remote = []
        for i, o in zip(ins[:nt], outs[:nt]):
            h = o.shape[1]
            remote.append((SIBLING, i.at[:, pl.ds((1 - pos[2]) * h, h)], o))
        return remote + [(SIBLING, i, o) for i, o in zip(ins[nt:], outs[nt:])]

    shapes = [((t.shape[0], t.shape[1] // 2, t.shape[2]), t.dtype) for t in ts] + [(v.shape, v.dtype) for v in plain]
    return Comm(list(ts) + list(plain), shapes, plan, nt + len(plain))


def chip_scatter_comm(xs, shared=None):
    nx = len(xs)

    def plan(ins, outs, pos):
        me = _chip_of(pos)
        remote = []
        for i, o in zip(ins[:nx], outs[:nx]):
            remote += [(rel, i.at[_chip_of(pos, rel)], o.at[j]) for j, rel in enumerate(OTHER_CHIPS)]
        if shared is not None:
            remote += [(rel, ins[nx], outs[nx].at[me]) for rel in OTHER_CHIPS]
        return remote

    shapes = [((3,) + v.shape[1:], v.dtype) for v in xs]
    if shared is not None:
        shapes.append(((N_CHIPS,) + shared.shape, shared.dtype))
    return Comm(list(xs) + ([] if shared is None else [shared]), shapes, plan, 3 * nx + (0 if shared is None else 3))


def tail_reduce(t, small, name):
    nslab, h2, cols = t.shape
    h = h2 // 2
    rows = small.shape[0]

    def body(t_ref, small_ref, mine_ref, theirs_ref, ssum_ref,
             got_pair, sums, got_chips, small_got, small_pair, small_chips, send_sems, recv_sems):
        pos = _me()
        core = pos[2]
        me = _chip_of(pos)

        def copy(i, rel, src, dst):
            return pltpu.make_async_remote_copy(src_ref=src, dst_ref=dst, send_sem=send_sems.at[i],
                                                recv_sem=recv_sems.at[i], device_id=_flip(pos, rel),
                                                device_id_type=MESH)

        pair = [copy(0, SIBLING, t_ref.at[:, pl.ds(pl.multiple_of((1 - core) * h, SUBLANES), h)], got_pair),
                copy(1, SIBLING, small_ref, small_got)]
        for cp in pair:
            cp.start()
        for cp in pair:
            cp.wait()
        for k in range(nslab):
            sums[k] = (t_ref[k, pl.ds(pl.multiple_of(core * h, SUBLANES), h), :] + got_pair[k]).astype(BF16)
        small_pair[...] = small_ref[...] + small_got[...]

        chips = []
        for j, rel in enumerate(OTHER_CHIPS):
            chips.append(copy(2 + j, rel, sums.at[_chip_of(pos, rel)], got_chips.at[j]))
            chips.append(copy(5 + j, rel, small_pair, small_chips.at[me]))
        for cp in chips:
            cp.start()
        small_chips[me] = small_pair[...]
        for cp in chips:
            cp.wait()
        acc = sums[me].astype(F32)
        for j in range(len(OTHER_CHIPS)):
            acc = acc + got_chips[j].astype(F32)
        mine_ref[...] = acc
        tot = small_chips[0]
        for k in range(1, N_CHIPS):
            tot = tot + small_chips[k]
        ssum_ref[...] = tot

        join = copy(8, SIBLING, mine_ref, theirs_ref)
        join.start()
        join.wait()

    vmem = pl.BlockSpec(memory_space=pltpu.VMEM)
    return pl.pallas_call(
        body, name=name, in_specs=[vmem, vmem], out_specs=[vmem, vmem, vmem],
        out_shape=[jax.ShapeDtypeStruct((h, cols), F32), jax.ShapeDtypeStruct((h, cols), F32),
                   jax.ShapeDtypeStruct((rows, LANES), F32)],
        scratch_shapes=[pltpu.VMEM((nslab, h, cols), F32), pltpu.VMEM((nslab, h, cols), BF16),
                        pltpu.VMEM((3, h, cols), BF16), pltpu.VMEM((rows, LANES), F32),
                        pltpu.VMEM((rows, LANES), F32), pltpu.VMEM((N_CHIPS, rows, LANES), F32),
                        pltpu.SemaphoreType.DMA((9,)), pltpu.SemaphoreType.DMA((9,))],
        compiler_params=pltpu.CompilerParams(vmem_limit_bytes=VMEM_LIMIT),
    )(t, small)


PACK_ROWS = 256


def _pack(arrs):
    parts, layout = [], []
    row = 0
    for a in arrs:
        flat = a.reshape(-1).astype(F32)
        size = flat.shape[0]
        rows = -(-size // (8 * LANES)) * 8
        flat = jnp.pad(flat, (0, rows * LANES - size))
        parts.append(flat.reshape(rows, LANES))
        layout.append((row, rows, size, a.shape))
        row += rows
    if row % PACK_ROWS:
        parts.append(jnp.zeros((PACK_ROWS - row % PACK_ROWS, LANES), F32))
    return jnp.concatenate(parts, axis=0), layout


def _unpack(packed, layout):
    return [packed[r0:r0 + rows].reshape(-1)[:size].reshape(shape) for r0, rows, size, shape in layout]


SMALL_REPL = ['mix_norm', 'a_b_in', 'a_sinks', 'a_conv_b', 'a_cln_g', 'a_cln_b', 'c_w_pool', 'c_w_s', 'c_b_s',
              'ffn_norm', 'final_norm']
SMALL_SHARD = ['a_conv_w', 'c_pool_scale', 'c_sln_g', 'c_sln_b']
BIG = ['a_w_in', 'a_w_out', 'c_w_in', 'c_w_out', 'ffn_w_gate', 'ffn_w_up', 'ffn_w_down']
TRANSPOSED = ('a_w_in', 'ffn_w_gate', 'ffn_w_up')
BIG_COL_SHARDED = {'c_w_in'}


def _full_weight(name, g8):
    _, h, cols = g8.shape
    g4 = g8.reshape(N_CHIPS, 2 * h, cols)
    if name not in BIG_COL_SHARDED:
        return g4.reshape(-1, cols)
    return jnp.transpose(g4, (1, 0, 2)).reshape(2 * h, N_CHIPS * cols)


def _to_shard_major(name, f):
    if name not in BIG_COL_SHARDED:
        return f.reshape(N_CHIPS, f.shape[0] // N_CHIPS, f.shape[1])
    r, cfull = f.shape
    return jnp.transpose(f.reshape(r, N_CHIPS, cfull // N_CHIPS), (1, 0, 2))


def kernel(*args):
    a = dict(zip(IN_NAMES, args))
    bl, seq, _ = a['x'].shape
    n = bl * seq
    x = a['x'].reshape(n, D_MODEL)
    target = a['loss_target'].reshape(n, D_MODEL)
    xi, yi, ci = _me()
    chip = 2 * xi + yi

    shard = {'a_w_in': a['a_w_in'][0].T, 'a_w_out': a['a_w_out'][0], 'c_w_in': a['c_w_in'][0], 'c_w_out': a['c_w_out'][0]}
    for layer in range(2):
        shard['gate' + str(layer)] = a['ffn_w_gate'][layer].T
        shard['up' + str(layer)] = a['ffn_w_up'][layer].T
        shard['down' + str(layer)] = a['ffn_w_down'][layer]
    shard = {k: v.astype(BF16) for k, v in shard.items()}
    core = ci.astype(jnp.int32).reshape(1)
    block_id = 4 * xi + 2 * yi + ci

    def first_hop(*names):
        return gather_first_hop([shard[k] for k in names])

    def finish(name, g8):
        h = shard[name].shape[0] // 2
        own = lax.dynamic_slice_in_dim(shard[name], ci * h, h, axis=0)
        return _full_weight(name, lax.dynamic_update_slice_in_dim(g8, own[None], block_id, axis=0))

    a_w_in_t = _full_weight('a_w_in', allgather_blocks([shard['a_w_in']], "gather_a_w_in")[0])
    in0_width = a_w_in_t.shape[0]
    small_shard_pack, small_shard_layout = _pack([a[k] for k in SMALL_SHARD])
    hop_a = first_hop('a_w_out', 'c_w_out')
    hop_s = chip_scatter_comm([], shared=small_shard_pack)
    mix_norm, ffn_norm = a['mix_norm'], a['ffn_norm']
    (hn0, q, kv, cc), outs = norm_inproj(
        x, mix_norm[0:1], a_w_in_t, a['a_b_in'],
        [(0, ATTN_WIDTH), (ATTN_WIDTH, ATTN_WIDTH + 2 * KV_WIDTH), (ATTN_WIDTH + 2 * KV_WIDTH, in0_width)],
        [BF16, BF16, F32], "in_proj0", comm=hop_a + hop_s, w_transposed=True)
    got_a, (ss,) = hop_a.split(outs, hop_s)
    ss = lax.dynamic_update_slice_in_dim(ss, small_shard_pack[None], chip, axis=0)
    ss_full = []
    for r0, rows, size, shape in small_shard_layout:
        per_chip = ss[:, r0:r0 + rows].reshape(N_CHIPS, -1)[:, :size].reshape((N_CHIPS,) + shape)
        ss_full.append(jnp.concatenate([per_chip[k] for k in range(N_CHIPS)], axis=-1))
    a_conv_w, c_pool_scale, c_sln_g, c_sln_b = [v[0] for v in ss_full]

    conv_taps = jnp.pad(a_conv_w, ((0, 32 - CONV_KERNEL), (0, 0)))
    conv_vec = jnp.pad(jnp.stack([a['a_conv_b'][0], a['a_cln_g'][0], a['a_cln_b'][0]]), ((0, 5), (0, 0)))
    sinks_b = jnp.pad(jnp.repeat(a['a_sinks'][0].reshape(N_KV_HEADS, GROUP), ATTN_BLOCK, axis=1), ((0, 6), (0, 0)))
    w_pool_bf = a['c_w_pool'][0].astype(BF16)
    pool_scale = c_pool_scale.reshape(1, POOL_WIDTH)
    w_s = a['c_w_s'][0]
    b_s_t = a['c_b_s'][0].T
    sgu_ln = jnp.pad(jnp.stack([c_sln_g, c_sln_b]), ((0, 6), (0, 0)))
    final_norm = a['final_norm'].reshape(1, D_MODEL)

    hop_b, pass_a = first_hop('gate0', 'c_w_in'), gather_second_hop(got_a)
    attn, outs = attn_fwd(q, kv, sinks_b, bl, seq, "attn_fwd", comm=hop_b + pass_a)
    got_b, done = hop_b.split(outs, pass_a)
    a_w_out, c_w_out = finish('a_w_out', done[0]), finish('c_w_out', done[1])

    hop_c, pass_b = first_hop('up0', 'down0'), gather_second_hop(got_b)
    (conv, conv_h1), outs = conv_fwd(cc, conv_taps, conv_vec, bl, seq, "conv_fwd", comm=hop_c + pass_b)
    got_c, done = hop_c.split(outs, pass_b)
    wg0, c_w_in = finish('gate0', done[0]), finish('c_w_in', done[1])

    h1, done = out_proj(x, attn, conv, a_w_out, "out_proj0", comm=gather_second_hop(got_c))
    wu0, wd0 = finish('up0', done[0]), finish('down0', done[1])

    (hnf0, g0, u0), got_e = ffn_gate_up(h1, ffn_norm[0:1], wg0, wu0, "ffn_gate_up0",
                                        comm=first_hop('gate1', 'up1', 'down1'))

    h2, done = ffn_down(h1, g0, u0, wd0, "ffn_down0", comm=gather_second_hop(got_e))
    wg1, wu1, wd1 = finish('gate1', done[0]), finish('up1', done[1]), finish('down1', done[2])
    wg, wu, wd = [wg0, wg1], [wu0, wu1], [wd0, wd1]

    (hn1, zp, zs), _ = norm_inproj(
        h2, mix_norm[1:2], c_w_in, jnp.zeros((1, c_w_in.shape[1]), F32),
        [(0, POOL_WIDTH), (POOL_WIDTH, c_w_in.shape[1])], [F32, F32], "in_proj1")
    pool = pool_fwd(zp, w_pool_bf, pool_scale, bl, seq, "pool_fwd")
    sgu = sgu_fwd(zs, w_s, b_s_t, sgu_ln, bl, seq, "sgu_fwd")
    h3, _ = out_proj(h2, pool, sgu, c_w_out, "out_proj1")
    (hnf1, g1, u1), _ = ffn_gate_up(h3, ffn_norm[1:2], wg1, wu1, "ffn_gate_up1")
    h4, _ = ffn_down(h3, g1, u1, wd1, "ffn_down1")

    dh4, d_final_norm, loss_local = loss_head(h4, final_norm, target, "loss_head")

    grads = {}
    pieces = {}

    def slabs_of(names, fulls):
        return [_to_shard_major(k, fulls[k]) for k in names]

    def pair_sums_of(names, slabs, gots):
        return [pair_sum(t, gt, core, "pair_sum_" + k) for k, t, gt in zip(names, slabs, gots)]

    def chip_sums_of(names, sums, from_chips):
        own = [lax.dynamic_index_in_dim(p, chip, axis=0, keepdims=False) for p in sums]
        return [sum_parts(p, "chip_sum_" + k, first=o) for k, p, o in zip(names, from_chips, own)]

    (dg, du, act), _ = ffn_down_bwd(dh4, g1, u1, wd[1], "ffn_down_bwd1")
    full1 = {'down1': mm_tn(act, dh4, "dw_down1"), 'gate1': mm_tn(dg, hnf1, "dw_gate1"),
             'up1': mm_tn(du, hnf1, "dw_up1")}
    names1 = ['gate1', 'up1', 'down1']
    slabs1 = slabs_of(names1, full1)
    dh3, d_ffn_norm1, got1 = proj_rms_bwd([dg, du], [wg[1], wu[1]], h3, ffn_norm[1:2], dh4, 1, "ffn_up_bwd1",
                                          tm_pref=256, w_transposed=True, comm=give_half_comm(slabs1))
    sums1 = pair_sums_of(names1, slabs1, got1)
    d_pool, d_sgu = out_proj_bwd(dh3, c_w_out, [F32, F32], "out_proj_bwd1")
    full1['c_w_out'] = jnp.concatenate([mm_tn(pool, dh3, "dw_out1_pool"), mm_tn(sgu, dh3, "dw_out1_sgu")], axis=0)
    (dzp, d_w_pool, d_pool_scale), from_gate = pool_bwd(zp, w_pool_bf, pool_scale, d_pool, bl, seq, "pool_bwd",
                                                        comm=chip_scatter_comm(sums1[0:1]))
    (dzs, d_w_s, d_b_s_b, d_sgu_ln), from_up = sgu_bwd(zs, w_s, b_s_t, sgu_ln, d_sgu, bl, seq, "sgu_bwd",
                                                       comm=chip_scatter_comm(sums1[1:2]))
    full1['c_w_in'] = jnp.concatenate([mm_tn(hn1, dzp, "dw_in1_pool"), mm_tn(hn1, dzs, "dw_in1_sgu")], axis=1)
    names1b = ['c_w_out', 'c_w_in']
    slabs1b = slabs_of(names1b, full1)
    heavy_pack, heavy_layout = _pack([d_w_pool[None], d_w_s[None]])
    chips_down, pair1b = chip_scatter_comm(sums1[2:3]), give_half_comm(slabs1b, plain=[heavy_pack])
    dh2, d_mix_norm1, outs = proj_rms_bwd([dzp, dzs], [c_w_in[:, :POOL_WIDTH], c_w_in[:, POOL_WIDTH:]], h2,
                                          mix_norm[1:2], dh3, 1, "in_proj_bwd1", comm=chips_down + pair1b)
    from_down, got1b = chips_down.split(outs, pair1b)
    mine1 = chip_sums_of(names1, sums1, from_gate + from_up + from_down)
    sums1b = pair_sums_of(names1b, slabs1b, got1b[:2])
    heavy_pair = add_cast(heavy_pack[None], got1b[2][None], "pair_sum_heavy", dtype=F32)[0]

    join1, chips1b = swap_comm(mine1), chip_scatter_comm(sums1b, shared=heavy_pair)
    (dg, du, act), outs = ffn_down_bwd(dh2, g0, u0, wd[0], "ffn_down_bwd0", comm=join1 + chips1b)
    theirs1, from_chips1b = join1.split(outs, chips1b)
    pieces.update({k: (m, t) for k, m, t in zip(names1, mine1, theirs1)})
    mine1b = chip_sums_of(names1b, sums1b, from_chips1b[:2])
    heavy_chips = lax.dynamic_update_slice_in_dim(from_chips1b[2], heavy_pair[None], chip, axis=0)
    grads['c_w_pool'], grads['c_w_s'] = _unpack(sum_parts(heavy_chips, "heavy_sum"), heavy_layout)
    full0 = {'down0': mm_tn(act, dh2, "dw_down0"), 'gate0': mm_tn(dg, hnf0, "dw_gate0"),
             'up0': mm_tn(du, hnf0, "dw_up0")}
    names0 = ['gate0', 'up0', 'down0']
    slabs0 = slabs_of(names0, full0)
    join1b, pair0 = swap_comm(mine1b), give_half_comm(slabs0)
    dh1, d_ffn_norm0, outs = proj_rms_bwd([dg, du], [wg[0], wu[0]], h1, ffn_norm[0:1], dh2, 1, "ffn_up_bwd0",
                                          tm_pref=256, comm=join1b + pair0, w_transposed=True)
    theirs1b, got0 = join1b.split(outs, pair0)
    pieces.update({k: (m, t) for k, m, t in zip(names1b, mine1b, theirs1b)})
    sums0 = pair_sums_of(names0, slabs0, got0)

    d_attn, d_conv = out_proj_bwd(dh1, a_w_out, [BF16, F32], "out_proj_bwd0")
    full_o = {'a_w_out': jnp.concatenate([mm_tn(attn, dh1, "dw_out0_attn"), mm_tn(conv, dh1, "dw_out0_conv")], axis=0)}
    slabs_o = slabs_of(['a_w_out'], full_o)
    chips0, pair_o = chip_scatter_comm(sums0), give_half_comm(slabs_o)
    (dq, dkv, d_sinks_b), outs = attn_bwd(q, kv, sinks_b, d_attn, bl, seq, "attn_bwd", comm=chips0 + pair_o)
    from_chips0, got_o = chips0.split(outs, pair_o)
    mine0 = chip_sums_of(names0, sums0, from_chips0)
    sums_o = pair_sums_of(['a_w_out'], slabs_o, got_o)
    join0, chips_o = swap_comm(mine0), chip_scatter_comm(sums_o)
    (dcc, d_conv_taps, d_conv_vec), outs = conv_bwd(cc, conv_h1, conv_taps, conv_vec, d_conv, bl, seq, "conv_bwd",
                                                    comm=join0 + chips_o)
    theirs0, from_chips_o = join0.split(outs, chips_o)
    pieces.update({k: (m, t) for k, m, t in zip(names0, mine0, theirs0)})
    mine_o = chip_sums_of(['a_w_out'], sums_o, from_chips_o)
    kq, kk = ATTN_WIDTH, ATTN_WIDTH + 2 * KV_WIDTH
    grad_x, d_mix_norm0, _ = proj_rms_bwd([dq, dkv, dcc], [a_w_in_t[:kq], a_w_in_t[kq:kk], a_w_in_t[kk:]], x,
                                          mix_norm[0:1], dh1, 1, "in_proj_bwd0", w_transposed=True)
    dw_q, db_q = mm_tn(dq, hn0, "dw_in0_q", xsum=True)
    dw_kv, db_kv = mm_tn(dkv, hn0, "dw_in0_kv", xsum=True)
    (dw_c, db_c), theirs_o = mm_tn(dcc, hn0, "dw_in0_c", xsum=True, comm=swap_comm(mine_o))
    pieces['a_w_out'] = (mine_o[0], theirs_o[0])
    d_a_b_in = jnp.concatenate([db_q, db_kv, db_c], axis=0)
    slabs_i = slabs_of(['a_w_in'], {'a_w_in': jnp.concatenate([dw_q, dw_kv, dw_c], axis=0)})

    small_full = {
        'mix_norm': jnp.stack([d_mix_norm0, d_mix_norm1]), 'a_b_in': d_a_b_in[None], 'a_sinks': d_sinks_b[:, 0][None],
        'a_conv_w': d_conv_taps[:CONV_KERNEL][None], 'a_conv_b': d_conv_vec[0][None], 'a_cln_g': d_conv_vec[1][None],
        'a_cln_b': d_conv_vec[2][None], 'c_pool_scale': d_pool_scale[0][None],
        'c_sln_g': d_sgu_ln[0][None], 'c_sln_b': d_sgu_ln[1][None],
        'c_b_s': d_b_s_b[:, :, 0][None], 'ffn_norm': jnp.stack([d_ffn_norm0, d_ffn_norm1]),
        'final_norm': d_final_norm, 'loss': loss_local.reshape(1)}
    small_names = SMALL_REPL + SMALL_SHARD
    tail_names = [k for k in small_names if k in small_full] + ['loss']
    small_pack, small_layout = _pack([small_full[k] for k in tail_names])

    mine_i, theirs_i, small_sum = tail_reduce(slabs_i[0], small_pack, "tail_reduce")
    pieces['a_w_in'] = (mine_i, theirs_i)

    def whole(name):
        mine, theirs = pieces[name]
        return jnp.concatenate([jnp.where(ci == 0, mine, theirs), jnp.where(ci == 0, theirs, mine)], axis=0)

    for k in ('a_w_in', 'a_w_out', 'c_w_in', 'c_w_out'):
        grads[k] = whole(k)[None]
    for short, key in (('gate', 'ffn_w_gate'), ('up', 'ffn_w_up'), ('down', 'ffn_w_down')):
        grads[key] = jnp.stack([whole(short + '0'), whole(short + '1')])

    for k, g in zip(tail_names, _unpack(small_sum, small_layout)):
        if k in SMALL_SHARD:
            width = a[k].shape[-1]
            g = lax.dynamic_slice_in_dim(g, chip * width, width, axis=g.ndim - 1)
        grads[k] = g
    loss = grads.pop('loss')[0]

    delta, new_m, new_v = {}, {}, {}
    for k in BIG:
        if k in TRANSPOSED:
            flip = lambda t: jnp.swapaxes(t, 1, 2)
            d, m, v = adamw(flip(a[k]), grads[k], flip(a['m_' + k]), flip(a['v_' + k]), "adamw_" + k)
            grads[k], delta[k], new_m[k], new_v[k] = flip(grads[k]), flip(d), flip(m), flip(v)
        else:
            delta[k], new_m[k], new_v[k] = adamw(a[k], grads[k], a['m_' + k], a['v_' + k], "adamw_" + k)
    two_d = lambda t: t.reshape(1, -1) if t.ndim == 1 else t
    ds, ms, vs = adamw_many([two_d(a[k]) for k in small_names], [two_d(grads[k]) for k in small_names],
                            [two_d(a['m_' + k]) for k in small_names], [two_d(a['v_' + k]) for k in small_names],
                            "adamw_small")
    for k, dv, mv, vv in zip(small_names, ds, ms, vs):
        delta[k], new_m[k], new_v[k] = [t.reshape(a[k].shape) for t in (dv, mv, vv)]

    return (loss, grad_x.reshape(a['x'].shape), *[grads[k] for k in WEIGHTS], *[delta[k] for k in WEIGHTS],
            *[new_m[k] for k in WEIGHTS], *[new_v[k] for k in WEIGHTS])
```

```python
import functools

import jax
import jax.numpy as jnp
from jax import lax
from jax.experimental import pallas as pl
from jax.experimental.pallas import tpu as pltpu

F32 = jnp.float32
BF16 = jnp.bfloat16

D_MODEL = 1024
EPS = 1e-5
N_Q_HEADS, N_KV_HEADS, HEAD_DIM = 8, 2, 64
ATTN_BLOCK = 128
ATTN_WIDTH = N_Q_HEADS * HEAD_DIM
KV_WIDTH = N_KV_HEADS * HEAD_DIM
CONV_WIDTH = 512
CONV_KERNEL = 31
CONV_HALO = 32
POOL_WINDOWS = (2, 4, 8, 16)
POOL_WIDTH = 512
POOL_HALO = 16
SGU_WIDTH = 512
SGU_CHUNK = 128
D_FF = 2816
FF_CHUNK = 128
MXU_COLS = 256
FFN_AHEAD = 1
LANES = 128
N_CHIPS = 4
N_DEV = 8

ADAM_LR, ADAM_B1, ADAM_B2, ADAM_EPS, ADAM_WD, ADAM_STEP = 0.001, 0.9, 0.999, 1e-08, 0.01, 10

VMEM_LIMIT = 56 * 2**20

WEIGHTS = ['mix_norm', 'a_w_in', 'a_b_in', 'a_sinks', 'a_conv_w', 'a_conv_b', 'a_cln_g', 'a_cln_b', 'a_w_out',
           'c_w_in', 'c_w_pool', 'c_pool_scale', 'c_sln_g', 'c_sln_b', 'c_w_s', 'c_b_s', 'c_w_out',
           'ffn_norm', 'ffn_w_gate', 'ffn_w_up', 'ffn_w_down', 'final_norm']
IN_NAMES = (['x'] + WEIGHTS + ['loss_target'] + ['m_' + n for n in WEIGHTS] + ['v_' + n for n in WEIGHTS])


def _params(*sem):
    return pltpu.CompilerParams(dimension_semantics=sem, vmem_limit_bytes=VMEM_LIMIT)


def _dot(a, b):
    return jnp.dot(a, b, preferred_element_type=F32)


def _dot_nt(a, b):
    return lax.dot_general(a, b, (((1,), (1,)), ((), ())), preferred_element_type=F32)


def _dot_tn(a, b):
    return lax.dot_general(a, b, (((0,), (0,)), ((), ())), preferred_element_type=F32)


def _sigmoid(v):
    return 0.5 * jnp.tanh(0.5 * v) + 0.5


def _row_tile(n, pref):
    t = min(n, pref)
    while n % t:
        t //= 2
    return t


def _col_tile(m, rows, budget=6 * 2**20):
    best = LANES
    for t in range(LANES, m + 1, LANES):
        if m % t == 0 and rows * t * 4 <= budget:
            best = t
    return best


class Comm:
    def __init__(self, ins, out_shapes, plan, count, aliases=None):
        self.ins, self.out_shapes, self.plan, self.count, self.aliases = ins, out_shapes, plan, count, aliases or {}

    def __add__(self, other):
        ni, no = len(self.ins), len(self.out_shapes)

        def plan(ins, outs, pos):
            return self.plan(ins[:ni], outs[:no], pos) + other.plan(ins[ni:], outs[no:], pos)

        aliases = dict(self.aliases)
        aliases.update({ni + i: no + o for i, o in other.aliases.items()})
        return Comm(list(self.ins) + list(other.ins), list(self.out_shapes) + list(other.out_shapes), plan,
                    self.count + other.count, aliases)

    def split(self, outs, other):
        return outs[:len(self.out_shapes)], outs[len(self.out_shapes):]


def _pcall(body, name, grid, in_specs, out_specs, out_shape, scratch_shapes, args, sem, comm=None):
    single = not isinstance(out_shape, (list, tuple))
    if single:
        out_specs, out_shape = [out_specs], [out_shape]
    if comm is None:
        res = pl.pallas_call(body, name=name, grid=grid, in_specs=in_specs, out_specs=list(out_specs),
                             out_shape=list(out_shape), scratch_shapes=list(scratch_shapes),
                             compiler_params=_params(*sem))(*args)
        return (res[0] if single else res), []
    na, nci, no, nco, ns = len(args), len(comm.ins), len(out_shape), len(comm.out_shapes), len(scratch_shapes)

    def wrapped(*refs):
        a_refs, ci_refs = refs[:na], refs[na:na + nci]
        o_refs, co_refs = refs[na + nci:na + nci + no], refs[na + nci + no:na + nci + no + nco]
        s_refs = refs[na + nci + no + nco:na + nci + no + nco + ns]
        send_sems, recv_sems = refs[-2], refs[-1]
        pos = _me()

        def copies():
            return [pltpu.make_async_remote_copy(src_ref=s, dst_ref=d, send_sem=send_sems.at[i],
                                                 recv_sem=recv_sems.at[i], device_id=_flip(pos, rel),
                                                 device_id_type=MESH)
                    for i, (rel, s, d) in enumerate(comm.plan(ci_refs, co_refs, pos))]

        first, last = None, None
        for d, size in enumerate(grid):
            f, l = pl.program_id(d) == 0, pl.program_id(d) == size - 1
            first = f if first is None else first & f
            last = l if last is None else last & l

        @pl.when(first)
        def _():
            for cp in copies():
                cp.start()

        body(*a_refs, *o_refs, *s_refs)

        @pl.when(last)
        def _():
            for cp in copies():
                cp.wait()

    res = pl.pallas_call(
        wrapped, name=name, grid=grid,
        in_specs=list(in_specs) + [ANY] * nci, out_specs=list(out_specs) + [ANY] * nco,
        out_shape=list(out_shape) + [jax.ShapeDtypeStruct(s, d) for s, d in comm.out_shapes],
        scratch_shapes=list(scratch_shapes) + [pltpu.SemaphoreType.DMA((comm.count,)),
                                               pltpu.SemaphoreType.DMA((comm.count,))],
        input_output_aliases={na + i: no + o for i, o in comm.aliases.items()},
        compiler_params=_params(*(["arbitrary"] * len(grid))),
    )(*args, *comm.ins)
    outs = res[:no]
    return (outs[0] if single else outs), list(res[no:])


def norm_inproj(x, gain, w, bias, splits, dtypes, name, comm=None, w_transposed=False):
    n = x.shape[0]
    m = w.shape[0] if w_transposed else w.shape[1]
    tm = _row_tile(n, 1024)

    def body(x_ref, g_ref, w_ref, b_ref, hn_ref, *outs):
        xv = x_ref[...]
        r = lax.rsqrt(jnp.mean(xv * xv, axis=-1, keepdims=True) + EPS)
        hn = ((xv * r) * g_ref[...]).astype(BF16)
        hn_ref[...] = hn
        z = (_dot_nt if w_transposed else _dot)(hn, w_ref[...]) + b_ref[...]
        for o, (lo, hi) in zip(outs, splits):
            o[...] = z[:, lo:hi].astype(o.dtype)

    out_shape = [jax.ShapeDtypeStruct((n, D_MODEL), BF16)]
    out_specs = [pl.BlockSpec((tm, D_MODEL), lambda i: (i, 0))]
    for (lo, hi), dt in zip(splits, dtypes):
        out_shape.append(jax.ShapeDtypeStruct((n, hi - lo), dt))
        out_specs.append(pl.BlockSpec((tm, hi - lo), lambda i: (i, 0)))
    return _pcall(
        body, name, (n // tm,),
        [pl.BlockSpec((tm, D_MODEL), lambda i: (i, 0)),
         pl.BlockSpec((1, D_MODEL), lambda i: (0, 0)),
         pl.BlockSpec(w.shape, lambda i: (0, 0)),
         pl.BlockSpec((1, m), lambda i: (0, 0))],
        out_specs, out_shape, [], (x, gain, w, bias), ("parallel",), comm)


def out_proj(res, m1, m2, w, name, comm=None):
    n = res.shape[0]
    k1, k2 = m1.shape[1], m2.shape[1]
    assert k1 == k2
    tm = _row_tile(n, 1024)

    def body(r_ref, a_ref, b_ref, w1_ref, w2_ref, o_ref):
        o_ref[...] = r_ref[...] + _dot(a_ref[...], w1_ref[...]) + _dot(b_ref[...], w2_ref[...])

    return _pcall(
        body, name, (n // tm,),
        [pl.BlockSpec((tm, D_MODEL), lambda i: (i, 0)),
         pl.BlockSpec((tm, k1), lambda i: (i, 0)),
         pl.BlockSpec((tm, k2), lambda i: (i, 0)),
         pl.BlockSpec((k1, D_MODEL), lambda i: (0, 0)),
         pl.BlockSpec((k2, D_MODEL), lambda i: (1, 0))],
        pl.BlockSpec((tm, D_MODEL), lambda i: (i, 0)),
        jax.ShapeDtypeStruct((n, D_MODEL), F32), [], (res, m1, m2, w, w), ("parallel",), comm)


def ffn_gate_up(h, gain, wg_t, wu_t, name, comm=None):
    n = h.shape[0]
    tm = _row_tile(n, 1024)
    th = D_FF // 2

    def body(h_ref, g_ref, wg_ref, wu_ref, hn_ref, go_ref, uo_ref):
        @pl.when(pl.program_id(1) == 0)
        def _():
            xv = h_ref[...]
            r = lax.rsqrt(jnp.mean(xv * xv, axis=-1, keepdims=True) + EPS)
            hn_ref[...] = ((xv * r) * g_ref[...]).astype(BF16)

        hn = hn_ref[...]
        go_ref[...] = _dot_nt(hn, wg_ref[...]).astype(BF16)
        uo_ref[...] = _dot_nt(hn, wu_ref[...]).astype(BF16)

    return _pcall(
        body, name, (n // tm, D_FF // th),
        [pl.BlockSpec((tm, D_MODEL), lambda i, j: (i, 0)),
         pl.BlockSpec((1, D_MODEL), lambda i, j: (0, 0)),
         pl.BlockSpec((th, D_MODEL), lambda i, j: (j, 0)),
         pl.BlockSpec((th, D_MODEL), lambda i, j: (j, 0))],
        [pl.BlockSpec((tm, D_MODEL), lambda i, j: (i, 0)),
         pl.BlockSpec((tm, th), lambda i, j: (i, j)),
         pl.BlockSpec((tm, th), lambda i, j: (i, j))],
        [jax.ShapeDtypeStruct((n, D_MODEL), BF16),
         jax.ShapeDtypeStruct((n, D_FF), BF16),
         jax.ShapeDtypeStruct((n, D_FF), BF16)],
        [], (h, gain, wg_t, wu_t), ("parallel", "arbitrary"), comm)


def ffn_down(h, g, u, wd, name, comm=None):
    n = h.shape[0]
    tm = _row_tile(n, 512)

    def body(h_ref, g_ref, u_ref, w_ref, o_ref, a_ref):
        for c0 in range(0, D_FF, FF_CHUNK):
            gv = g_ref[:, c0:c0 + FF_CHUNK]
            a_ref[:, c0:c0 + FF_CHUNK] = gv * _sigmoid(gv) * u_ref[:, c0:c0 + FF_CHUNK]
        o_ref[...] = h_ref[...] + _dot(a_ref[...], w_ref[...])

    return _pcall(
        body, name, (n // tm,),
        [pl.BlockSpec((tm, D_MODEL), lambda i: (i, 0)),
         pl.BlockSpec((tm, D_FF), lambda i: (i, 0)),
         pl.BlockSpec((tm, D_FF), lambda i: (i, 0)),
         pl.BlockSpec((D_FF, D_MODEL), lambda i: (0, 0))],
        pl.BlockSpec((tm, D_MODEL), lambda i: (i, 0)),
        jax.ShapeDtypeStruct((n, D_MODEL), F32),
        [pltpu.VMEM((tm, D_FF), BF16)], (h, g, u, wd), ("parallel",), comm)


def ffn_down_bwd(dh, g, u, wd, name, comm=None):
    n = dh.shape[0]
    tm = _row_tile(n, 512)

    def body(dh_ref, g_ref, u_ref, w_ref, dg_ref, du_ref, a_ref):
        dhb = dh_ref[...].astype(BF16)
        chunks = [slice(c0, c0 + MXU_COLS) for c0 in range(0, D_FF, MXU_COLS)]
        ahead = [_dot_nt(dhb, w_ref[c, :]) for c in chunks[:FFN_AHEAD]]
        for i, cols in enumerate(chunks):
            da = ahead.pop(0).astype(BF16)
            if i + FFN_AHEAD < len(chunks):
                ahead.append(_dot_nt(dhb, w_ref[chunks[i + FFN_AHEAD], :]))
            gv, uv = g_ref[:, cols], u_ref[:, cols]
            sg = _sigmoid(gv)
            act = gv * sg
            dg_ref[:, cols] = (da * uv) * (sg + act * (1.0 - sg))
            du_ref[:, cols] = da * act
            a_ref[:, cols] = act * uv

    spec_h = pl.BlockSpec((tm, D_FF), lambda i: (i, 0))
    return _pcall(
        body, name, (n // tm,),
        [pl.BlockSpec((tm, D_MODEL), lambda i: (i, 0)), spec_h, spec_h,
         pl.BlockSpec((D_FF, D_MODEL), lambda i: (0, 0))],
        [spec_h, spec_h, spec_h], [jax.ShapeDtypeStruct((n, D_FF), BF16)] * 3,
        [], (dh, g, u, wd), ("parallel",), comm)


def mm_tn(x, dy, name, xsum=False, comm=None):
    n, k = x.shape
    m = dy.shape[1]
    tk = _col_tile(k, m)
    tt = _row_tile(n, 2048)

    def body(x_ref, dy_ref, o_ref, *rest):
        xt_ref = rest[-1]
        t = pl.program_id(1)
        xv = x_ref[...]
        xt_ref[...] = xv.astype(BF16).T
        part = _dot(xt_ref[...], dy_ref[...].astype(BF16))

        @pl.when(t == 0)
        def _():
            o_ref[...] = part

        @pl.when(t > 0)
        def _():
            o_ref[...] += part

        if xsum:
            cs = jnp.broadcast_to(jnp.sum(xv.astype(F32), axis=0, keepdims=True), rest[0].shape)

            @pl.when(t == 0)
            def _():
                rest[0][...] = cs

            @pl.when(t > 0)
            def _():
                rest[0][...] += cs

    out_shape = [jax.ShapeDtypeStruct((k, m), F32)]
    out_specs = [pl.BlockSpec((tk, m), lambda j, t: (j, 0))]
    if xsum:
        out_shape.append(jax.ShapeDtypeStruct((8, k), F32))
        out_specs.append(pl.BlockSpec((8, tk), lambda j, t: (0, j)))
    res, comm_outs = _pcall(
        body, name, (k // tk, n // tt),
        [pl.BlockSpec((tt, tk), lambda j, t: (t, j)),
         pl.BlockSpec((tt, m), lambda j, t: (t, 0))],
        out_specs, out_shape, [pltpu.VMEM((tk, tt), BF16)], (x, dy), ("arbitrary", "arbitrary"), comm)
    res = (res[0], res[1][0]) if xsum else res[0]
    return res if comm is None else (res, comm_outs)


def out_proj_bwd(dh, w, dtypes, name):
    n = dh.shape[0]
    k = w.shape[0]
    half = k // 2
    tm = _row_tile(n, 1024)

    def body(dh_ref, w_ref, a_ref, b_ref):
        dm = _dot_nt(dh_ref[...].astype(BF16), w_ref[...])
        a_ref[...] = dm[:, :half].astype(a_ref.dtype)
        b_ref[...] = dm[:, half:].astype(b_ref.dtype)

    return pl.pallas_call(
        body, name=name, grid=(n // tm,),
        in_specs=[pl.BlockSpec((tm, D_MODEL), lambda i: (i, 0)),
                  pl.BlockSpec((k, D_MODEL), lambda i: (0, 0))],
        out_specs=[pl.BlockSpec((tm, half), lambda i: (i, 0))] * 2,
        out_shape=[jax.ShapeDtypeStruct((n, half), dtypes[0]), jax.ShapeDtypeStruct((n, half), dtypes[1])],
        compiler_params=_params("parallel"),
    )(dh, w)


def proj_rms_bwd(dys, ws, h_in, gain, dres, nk, name, tm_pref=512, comm=None, w_transposed=False):
    n = h_in.shape[0]
    npair = len(dys)
    tm = _row_tile(n, tm_pref)
    tks = [dy.shape[1] // nk for dy in dys]
    mm = _dot if w_transposed else _dot_nt

    def body(*refs):
        dy_refs = refs[:npair]
        w_refs = refs[npair:2 * npair]
        h_ref, g_ref, dr_ref, o_ref, dg_ref, acc_ref = refs[2 * npair:]
        i, k = pl.program_id(0), pl.program_id(1)
        part = mm(dy_refs[0][...], w_refs[0][...])
        for p in range(1, npair):
            part = part + mm(dy_refs[p][...], w_refs[p][...])

        @pl.when(k == 0)
        def _():
            acc_ref[...] = part

        @pl.when(k > 0)
        def _():
            acc_ref[...] += part

        @pl.when(k == nk - 1)
        def _():
            dhn = acc_ref[...]
            xv = h_ref[...]
            r = lax.rsqrt(jnp.mean(xv * xv, axis=-1, keepdims=True) + EPS)
            xh = xv * r
            uv = dhn * g_ref[...]
            o_ref[...] = dr_ref[...] + r * (uv - xh * jnp.mean(uv * xh, axis=-1, keepdims=True))
            dgp = jnp.broadcast_to(jnp.sum(dhn * xh, axis=0, keepdims=True), dg_ref.shape)

            @pl.when(i == 0)
            def _():
                dg_ref[...] = dgp

            @pl.when(i > 0)
            def _():
                dg_ref[...] += dgp

    row = pl.BlockSpec((tm, D_MODEL), lambda i, k: (i, 0))
    in_specs = [pl.BlockSpec((tm, tk), lambda i, k: (i, k)) for tk in tks]
    once = dict(pipeline_mode=pl.Buffered(1)) if nk == 1 else {}
    if w_transposed:
        in_specs += [pl.BlockSpec((tk, D_MODEL), lambda i, k: (k, 0), **once) for tk in tks]
    else:
        in_specs += [pl.BlockSpec((D_MODEL, tk), lambda i, k: (0, k), **once) for tk in tks]
    in_specs += [row, pl.BlockSpec((1, D_MODEL), lambda i, k: (0, 0)), row]
    (dh, dgain), comm_outs = _pcall(
        body, name, (n // tm, nk), in_specs,
        [row, pl.BlockSpec((8, D_MODEL), lambda i, k: (0, 0))],
        [jax.ShapeDtypeStruct((n, D_MODEL), F32), jax.ShapeDtypeStruct((8, D_MODEL), F32)],
        [pltpu.VMEM((tm, D_MODEL), F32)], (*dys, *ws, h_in, gain, dres), ("arbitrary", "arbitrary"), comm)
    return dh, dgain[0], comm_outs


def loss_head(h, gain, target, name):
    n = h.shape[0]
    tm = _row_tile(n, 512)

    def body(h_ref, g_ref, t_ref, dh_ref, dg_ref, l_ref):
        i = pl.program_id(0)
        xv = h_ref[...]
        r = lax.rsqrt(jnp.mean(xv * xv, axis=-1, keepdims=True) + EPS)
        xh = xv * r
        err = xh * g_ref[...] - t_ref[...]
        dy = err * (1.0 / D_MODEL)
        uv = dy * g_ref[...]
        dh_ref[...] = r * (uv - xh * jnp.mean(uv * xh, axis=-1, keepdims=True))
        dgp = jnp.broadcast_to(jnp.sum(dy * xh, axis=0, keepdims=True), dg_ref.shape)
        lp = jnp.sum(jnp.sum(err * err, axis=-1, keepdims=True), axis=0, keepdims=True) * (0.5 / D_MODEL)
        lp = jnp.broadcast_to(lp, l_ref.shape)

        @pl.when(i == 0)
        def _():
            dg_ref[...] = dgp
            l_ref[...] = lp

        @pl.when(i > 0)
        def _():
            dg_ref[...] += dgp
            l_ref[...] += lp

    row = pl.BlockSpec((tm, D_MODEL), lambda i: (i, 0))
    dh, dg, l = pl.pallas_call(
        body, name=name, grid=(n // tm,),
        in_specs=[row, pl.BlockSpec((1, D_MODEL), lambda i: (0, 0)), row],
        out_specs=[row, pl.BlockSpec((8, D_MODEL), lambda i: (0, 0)), pl.BlockSpec((8, LANES), lambda i: (0, 0))],
        out_shape=[jax.ShapeDtypeStruct((n, D_MODEL), F32), jax.ShapeDtypeStruct((8, D_MODEL), F32),
                   jax.ShapeDtypeStruct((8, LANES), F32)],
        compiler_params=_params("arbitrary"),
    )(h, gain, target)
    return dh, dg[0], l[0, 0]


GROUP = N_Q_HEADS // N_KV_HEADS
GQ = GROUP * ATTN_BLOCK


def _attn_mask_t(n):
    r = lax.broadcasted_iota(jnp.int32, (2 * ATTN_BLOCK, GQ), 0)
    qi = lax.broadcasted_iota(jnp.int32, (2 * ATTN_BLOCK, GQ), 1) & (ATTN_BLOCK - 1)
    band = (r > qi) & (r <= qi + ATTN_BLOCK)
    return band & ((r >= ATTN_BLOCK) | (n > 0))


def _stack_heads(blk, kh):
    return jnp.concatenate([blk[:, (kh * GROUP + g) * HEAD_DIM:(kh * GROUP + g + 1) * HEAD_DIM]
                            for g in range(GROUP)], axis=0)


def _attn_probs_t(kk, qs, mask, sink):
    s = _dot_nt(kk, qs) * (HEAD_DIM ** -0.5)
    s = jnp.where(mask, s, -1e30)
    m = jnp.maximum(jnp.max(s, axis=0, keepdims=True), sink)
    p = jnp.exp(s - m)
    esink = jnp.exp(sink - m)
    inv = 1.0 / (jnp.sum(p, axis=0, keepdims=True) + esink)
    return p * inv, esink * inv


def attn_fwd(q, kv, sinks_t, nseq, seq, name, comm=None):
    nb = seq // ATTN_BLOCK

    def body(q_ref, kv_ref, s_ref, o_ref, kvp):
        kvp[0:ATTN_BLOCK, :] = jnp.zeros((ATTN_BLOCK, 2 * KV_WIDTH), BF16)
        kvp[ATTN_BLOCK:, :] = kv_ref[...]

        def blk(n, carry):
            st = pl.multiple_of(n * ATTN_BLOCK, ATTN_BLOCK)
            qb = q_ref[pl.ds(st, ATTN_BLOCK), :]
            kw = kvp[pl.ds(st, 2 * ATTN_BLOCK), :]
            mask = _attn_mask_t(n)
            for kh in range(N_KV_HEADS):
                kk = kw[:, kh * HEAD_DIM:(kh + 1) * HEAD_DIM]
                vv = kw[:, KV_WIDTH + kh * HEAD_DIM:KV_WIDTH + (kh + 1) * HEAD_DIM]
                probs, _ = _attn_probs_t(kk, _stack_heads(qb, kh), mask, s_ref[kh:kh + 1, :])
                ot = _dot_tn(vv, probs.astype(BF16))
                for pair in range(GROUP // 2):
                    two = jnp.concatenate([ot[:, (2 * pair) * ATTN_BLOCK:(2 * pair + 1) * ATTN_BLOCK],
                                           ot[:, (2 * pair + 1) * ATTN_BLOCK:(2 * pair + 2) * ATTN_BLOCK]], axis=0)
                    col = (kh * GROUP + 2 * pair) * HEAD_DIM
                    o_ref[pl.ds(st, ATTN_BLOCK), col:col + 2 * HEAD_DIM] = two.T.astype(o_ref.dtype)
            return carry

        lax.fori_loop(0, nb, blk, 0, unroll=4)

    return _pcall(
        body, name, (nseq,),
        [pl.BlockSpec((seq, ATTN_WIDTH), lambda b: (b, 0)),
         pl.BlockSpec((seq, 2 * KV_WIDTH), lambda b: (b, 0)),
         pl.BlockSpec((8, GQ), lambda b: (0, 0))],
        pl.BlockSpec((seq, ATTN_WIDTH), lambda b: (b, 0)),
        jax.ShapeDtypeStruct((nseq * seq, ATTN_WIDTH), BF16),
        [pltpu.VMEM((ATTN_BLOCK + seq, 2 * KV_WIDTH), BF16)], (q, kv, sinks_t), ("parallel",), comm)


def attn_bwd(q, kv, sinks_t, do, nseq, seq, name, comm=None):
    nb = seq // ATTN_BLOCK

    def body(q_ref, kv_ref, s_ref, do_ref, dq_ref, dkv_ref, ds_ref, kvp, dkvp, dsacc):
        @pl.when(pl.program_id(0) == 0)
        def _():
            dsacc[...] = jnp.zeros(dsacc.shape, F32)

        kvp[0:ATTN_BLOCK, :] = jnp.zeros((ATTN_BLOCK, 2 * KV_WIDTH), BF16)
        kvp[ATTN_BLOCK:, :] = kv_ref[...]
        dkvp[...] = jnp.zeros(dkvp.shape, F32)

        def blk(n, carry):
            st = pl.multiple_of(n * ATTN_BLOCK, ATTN_BLOCK)
            qb = q_ref[pl.ds(st, ATTN_BLOCK), :]
            dob = do_ref[pl.ds(st, ATTN_BLOCK), :]
            kw = kvp[pl.ds(st, 2 * ATTN_BLOCK), :]
            mask = _attn_mask_t(n)
            for kh in range(N_KV_HEADS):
                kk = kw[:, kh * HEAD_DIM:(kh + 1) * HEAD_DIM]
                vv = kw[:, KV_WIDTH + kh * HEAD_DIM:KV_WIDTH + (kh + 1) * HEAD_DIM]
                qs = _stack_heads(qb, kh)
                dos = _stack_heads(dob, kh)
                probs, psink = _attn_probs_t(kk, qs, mask, s_ref[kh:kh + 1, :])
                dp = _dot_nt(vv, dos)
                dv = _dot(probs.astype(BF16), dos)
                rowdot = jnp.sum(probs * dp, axis=0, keepdims=True)
                dsc = (probs * (dp - rowdot) * (HEAD_DIM ** -0.5)).astype(BF16)
                dsacc[kh:kh + 1, :] += -psink * rowdot
                dk = _dot(dsc, qs)
                dqs = _dot_tn(dsc, kk)
                for g in range(GROUP):
                    col = (kh * GROUP + g) * HEAD_DIM
                    dq_ref[pl.ds(st, ATTN_BLOCK), col:col + HEAD_DIM] = (
                        dqs[g * ATTN_BLOCK:(g + 1) * ATTN_BLOCK].astype(dq_ref.dtype))
                dkvp[pl.ds(st, 2 * ATTN_BLOCK), kh * HEAD_DIM:(kh + 1) * HEAD_DIM] += dk
                dkvp[pl.ds(st, 2 * ATTN_BLOCK), KV_WIDTH + kh * HEAD_DIM:KV_WIDTH + (kh + 1) * HEAD_DIM] += dv
            return carry

        lax.fori_loop(0, nb, blk, 0, unroll=2)
        dkv_ref[...] = dkvp[ATTN_BLOCK:, :].astype(dkv_ref.dtype)

        @pl.when(pl.program_id(0) == nseq - 1)
        def _():
            for kh in range(N_KV_HEADS):
                for g in range(GROUP):
                    tot = jnp.sum(dsacc[kh:kh + 1, g * ATTN_BLOCK:(g + 1) * ATTN_BLOCK], axis=1, keepdims=True)
                    ds_ref[kh * GROUP + g:kh * GROUP + g + 1, :] = jnp.broadcast_to(tot, (1, LANES))

    seq_q = pl.BlockSpec((seq, ATTN_WIDTH), lambda b: (b, 0))
    seq_kv = pl.BlockSpec((seq, 2 * KV_WIDTH), lambda b: (b, 0))
    return _pcall(
        body, name, (nseq,),
        [seq_q, seq_kv, pl.BlockSpec((8, GQ), lambda b: (0, 0)), seq_q],
        [seq_q, seq_kv, pl.BlockSpec((N_Q_HEADS, LANES), lambda b: (0, 0))],
        [jax.ShapeDtypeStruct((nseq * seq, ATTN_WIDTH), BF16),
         jax.ShapeDtypeStruct((nseq * seq, 2 * KV_WIDTH), BF16),
         jax.ShapeDtypeStruct((N_Q_HEADS, LANES), F32)],
        [pltpu.VMEM((ATTN_BLOCK + seq, 2 * KV_WIDTH), BF16),
         pltpu.VMEM((ATTN_BLOCK + seq, 2 * KV_WIDTH), F32),
         pltpu.VMEM((8, GQ), F32)], (q, kv, sinks_t, do), ("arbitrary",), comm)


CONV_T = 128


SUBLANES = 8


def _shifted_rows(win):
    phases = [win] + [pltpu.roll(win, s, 0) for s in range(1, SUBLANES)]

    def shifted(s):
        lo = CONV_HALO - SUBLANES * (s // SUBLANES)
        return phases[s % SUBLANES][lo:lo + CONV_T]

    return shifted


def _conv_taps(win, w_ref, lanes, init):
    shifted = _shifted_rows(win)
    acc = init
    for j in range(CONV_KERNEL):
        acc = acc + w_ref[j:j + 1, lanes] * shifted(CONV_KERNEL - 1 - j)
    return acc


def _conv_block(h0p, w_ref, vec_ref, st):
    cols = []
    for cs in range(CONV_WIDTH // LANES):
        lanes = slice(cs * LANES, (cs + 1) * LANES)
        win = h0p[pl.ds(st, CONV_T + CONV_HALO), lanes]
        init = jnp.broadcast_to(vec_ref[0:1, lanes], (CONV_T, LANES))
        cols.append(_conv_taps(win, w_ref, lanes, init))
    return jnp.concatenate(cols, axis=-1)


def _glu_store(c_ref, h0p, st):
    cb = c_ref[pl.ds(st, CONV_T), :]
    h0p[pl.ds(pl.multiple_of(st + CONV_HALO, CONV_HALO), CONV_T), :] = cb[:, :CONV_WIDTH] * _sigmoid(cb[:, CONV_WIDTH:])


def conv_fwd(c, w, vec, nseq, seq, name, comm=None):
    nb = seq // CONV_T

    def body(c_ref, w_ref, vec_ref, o_ref, h1_ref, h0p):
        h0p[0:CONV_HALO, :] = jnp.zeros((CONV_HALO, CONV_WIDTH), F32)

        def blk(n, carry):
            st = pl.multiple_of(n * CONV_T, CONV_T)
            _glu_store(c_ref, h0p, st)
            h1 = _conv_block(h0p, w_ref, vec_ref, st)
            h1_ref[pl.ds(st, CONV_T), :] = h1
            mu = jnp.mean(h1, axis=-1, keepdims=True)
            xc = h1 - mu
            rstd = lax.rsqrt(jnp.mean(xc * xc, axis=-1, keepdims=True) + EPS)
            y = xc * rstd * vec_ref[1:2, :] + vec_ref[2:3, :]
            o_ref[pl.ds(st, CONV_T), :] = (y * _sigmoid(y)).astype(o_ref.dtype)
            return carry

        lax.fori_loop(0, nb, blk, 0)

    return _pcall(
        body, name, (nseq,),
        [pl.BlockSpec((seq, 2 * CONV_WIDTH), lambda b: (b, 0)),
         pl.BlockSpec((32, CONV_WIDTH), lambda b: (0, 0)),
         pl.BlockSpec((8, CONV_WIDTH), lambda b: (0, 0))],
        [pl.BlockSpec((seq, CONV_WIDTH), lambda b: (b, 0))] * 2,
        [jax.ShapeDtypeStruct((nseq * seq, CONV_WIDTH), BF16), jax.ShapeDtypeStruct((nseq * seq, CONV_WIDTH), F32)],
        [pltpu.VMEM((CONV_HALO + seq, CONV_WIDTH), F32)], (c, w, vec), ("parallel",), comm)


def conv_bwd(c, h1_saved, w, vec, dout, nseq, seq, name, comm=None):
    nb = seq // CONV_T

    def body(c_ref, h1_ref, w_ref, vec_ref, do_ref, dc_ref, dw_ref, dvec_ref, h0p, dh1p, dwacc):
        @pl.when(pl.program_id(0) == 0)
        def _():
            dwacc[...] = jnp.zeros(dwacc.shape, F32)
            dvec_ref[...] = jnp.zeros(dvec_ref.shape, F32)

        h0p[0:CONV_HALO, :] = jnp.zeros((CONV_HALO, CONV_WIDTH), F32)
        dh1p[seq:seq + CONV_HALO, :] = jnp.zeros((CONV_HALO, CONV_WIDTH), F32)

        def pass_a(n, carry):
            st = pl.multiple_of(n * CONV_T, CONV_T)
            _glu_store(c_ref, h0p, st)
            h1 = h1_ref[pl.ds(st, CONV_T), :]
            mu = jnp.mean(h1, axis=-1, keepdims=True)
            xc = h1 - mu
            rstd = lax.rsqrt(jnp.mean(xc * xc, axis=-1, keepdims=True) + EPS)
            xh = xc * rstd
            y = xh * vec_ref[1:2, :] + vec_ref[2:3, :]
            sg = _sigmoid(y)
            dy = do_ref[pl.ds(st, CONV_T), :] * (sg * (1.0 + y * (1.0 - sg)))
            dvec_ref[1:2, :] += jnp.sum(dy * xh, axis=0, keepdims=True)
            dvec_ref[2:3, :] += jnp.sum(dy, axis=0, keepdims=True)
            dxh = dy * vec_ref[1:2, :]
            dh1 = rstd * (dxh - jnp.mean(dxh, axis=-1, keepdims=True)
                          - xh * jnp.mean(dxh * xh, axis=-1, keepdims=True))
            dvec_ref[0:1, :] += jnp.sum(dh1, axis=0, keepdims=True)
            dh1p[pl.ds(st, CONV_T), :] = dh1
            return carry

        lax.fori_loop(0, nb, pass_a, 0)

        def pass_b(n, carry):
            st = pl.multiple_of(n * CONV_T, CONV_T)
            cols = []
            for cs in range(CONV_WIDTH // LANES):
                lanes = slice(cs * LANES, (cs + 1) * LANES)
                wind = dh1p[pl.ds(st, CONV_T + CONV_HALO), lanes]
                winh = h0p[pl.ds(st, CONV_T + CONV_HALO), lanes]
                d1 = wind[0:CONV_T]
                shifted_d, shifted_h = _shifted_rows(wind), _shifted_rows(winh)
                acc = jnp.zeros((CONV_T, LANES), F32)
                for j in range(CONV_KERNEL):
                    acc = acc + w_ref[j:j + 1, lanes] * shifted_d(2 + j)
                    prod = d1 * shifted_h(CONV_KERNEL - 1 - j)
                    part = prod[0:8]
                    for r in range(8, CONV_T, 8):
                        part = part + prod[r:r + 8]
                    dwacc[8 * j:8 * j + 8, lanes] += part
                cols.append(acc)
            dh0 = jnp.concatenate(cols, axis=-1)
            cb = c_ref[pl.ds(st, CONV_T), :]
            av, gt = cb[:, :CONV_WIDTH], cb[:, CONV_WIDTH:]
            sg = _sigmoid(gt)
            dc_ref[pl.ds(st, CONV_T), :] = jnp.concatenate(
                [dh0 * sg, dh0 * av * sg * (1.0 - sg)], axis=-1).astype(dc_ref.dtype)
            return carry

        lax.fori_loop(0, nb, pass_b, 0)

        @pl.when(pl.program_id(0) == nseq - 1)
        def _():
            dw_ref[...] = jnp.zeros(dw_ref.shape, F32)
            for j in range(CONV_KERNEL):
                dw_ref[j:j + 1, :] = jnp.sum(dwacc[8 * j:8 * j + 8, :], axis=0, keepdims=True)

    return _pcall(
        body, name, (nseq,),
        [pl.BlockSpec((seq, 2 * CONV_WIDTH), lambda b: (b, 0)),
         pl.BlockSpec((seq, CONV_WIDTH), lambda b: (b, 0)),
         pl.BlockSpec((32, CONV_WIDTH), lambda b: (0, 0)),
         pl.BlockSpec((8, CONV_WIDTH), lambda b: (0, 0)),
         pl.BlockSpec((seq, CONV_WIDTH), lambda b: (b, 0))],
        [pl.BlockSpec((seq, 2 * CONV_WIDTH), lambda b: (b, 0)),
         pl.BlockSpec((32, CONV_WIDTH), lambda b: (0, 0)),
         pl.BlockSpec((8, CONV_WIDTH), lambda b: (0, 0))],
        [jax.ShapeDtypeStruct((nseq * seq, 2 * CONV_WIDTH), BF16),
         jax.ShapeDtypeStruct((32, CONV_WIDTH), F32),
         jax.ShapeDtypeStruct((8, CONV_WIDTH), F32)],
        [pltpu.VMEM((CONV_HALO + seq, CONV_WIDTH), F32),
         pltpu.VMEM((seq + CONV_HALO, CONV_WIDTH), F32),
         pltpu.VMEM((8 * 32, CONV_WIDTH), F32)], (c, h1_saved, w, vec, dout), ("arbitrary",), comm)


POOL_T = 128


def _pooled_block(zpp, st, grp):
    lanes = slice(grp * LANES, (grp + 1) * LANES)
    win = zpp[pl.ds(st, POOL_T + POOL_HALO), lanes]
    acc = win
    for lvl in range(grp + 1):
        acc = acc + pltpu.roll(acc, 1 << lvl, 0)
    t = st + lax.broadcasted_iota(jnp.int32, (POOL_T, 1), 0)
    inv = 1.0 / jnp.minimum(t + 1, POOL_WINDOWS[grp]).astype(F32)
    return acc[POOL_HALO:] * inv - win[POOL_HALO:], inv


def pool_fwd(zp, wp, scale, nseq, seq, name):
    nb = seq // POOL_T

    def body(z_ref, wp_ref, sc_ref, o_ref, zpp):
        zpp[0:POOL_HALO, :] = jnp.zeros((POOL_HALO, POOL_WIDTH), F32)
        zpp[POOL_HALO:, :] = z_ref[...]

        def blk(n, carry):
            st = pl.multiple_of(n * POOL_T, POOL_T)
            for grp in range(len(POOL_WINDOWS)):
                lanes = slice(grp * LANES, (grp + 1) * LANES)
                pooled, _ = _pooled_block(zpp, st, grp)
                o_ref[pl.ds(st, POOL_T), lanes] = (
                    _dot(pooled.astype(BF16), wp_ref[grp]) * sc_ref[0:1, lanes]).astype(o_ref.dtype)
            return carry

        lax.fori_loop(0, nb, blk, 0)

    return pl.pallas_call(
        body, name=name, grid=(nseq,),
        in_specs=[pl.BlockSpec((seq, POOL_WIDTH), lambda b: (b, 0)),
                  pl.BlockSpec((4, LANES, LANES), lambda b: (0, 0, 0)),
                  pl.BlockSpec((1, POOL_WIDTH), lambda b: (0, 0))],
        out_specs=pl.BlockSpec((seq, POOL_WIDTH), lambda b: (b, 0)),
        out_shape=jax.ShapeDtypeStruct((nseq * seq, POOL_WIDTH), BF16),
        scratch_shapes=[pltpu.VMEM((POOL_HALO + seq, POOL_WIDTH), F32)],
        compiler_params=_params("parallel"),
    )(zp, wp, scale)


def pool_bwd(zp, wp, scale, dout, nseq, seq, name, comm=None):
    nb = seq // POOL_T

    def body(z_ref, wp_ref, sc_ref, do_ref, dz_ref, dwp_ref, dsc_ref, zpp, dpcp, negd):
        @pl.when(pl.program_id(0) == 0)
        def _():
            dwp_ref[...] = jnp.zeros(dwp_ref.shape, F32)
            dsc_ref[...] = jnp.zeros(dsc_ref.shape, F32)

        zpp[0:POOL_HALO, :] = jnp.zeros((POOL_HALO, POOL_WIDTH), F32)
        zpp[POOL_HALO:, :] = z_ref[...]
        dpcp[seq:seq + POOL_HALO, :] = jnp.zeros((POOL_HALO, POOL_WIDTH), F32)

        def pass_a(n, carry):
            st = pl.multiple_of(n * POOL_T, POOL_T)
            for grp in range(len(POOL_WINDOWS)):
                lanes = slice(grp * LANES, (grp + 1) * LANES)
                pooled, inv = _pooled_block(zpp, st, grp)
                pb = pooled.astype(BF16)
                dob = do_ref[pl.ds(st, POOL_T), lanes]
                dsc_ref[0:1, lanes] += jnp.sum(dob * _dot(pb, wp_ref[grp]), axis=0, keepdims=True)
                dpm = (dob * sc_ref[0:1, lanes]).astype(BF16)
                dwp_ref[grp] += _dot_tn(pb, dpm)
                dpooled = _dot_nt(dpm, wp_ref[grp])
                negd[pl.ds(st, POOL_T), lanes] = -dpooled
                dpcp[pl.ds(st, POOL_T), lanes] = dpooled * inv
            return carry

        lax.fori_loop(0, nb, pass_a, 0)

        def pass_b(n, carry):
            st = pl.multiple_of(n * POOL_T, POOL_T)
            rows = POOL_T + POOL_HALO
            for grp in range(len(POOL_WINDOWS)):
                lanes = slice(grp * LANES, (grp + 1) * LANES)
                acc = dpcp[pl.ds(st, rows), lanes]
                for lvl in range(grp + 1):
                    acc = acc + pltpu.roll(acc, rows - (1 << lvl), 0)
                dz_ref[pl.ds(st, POOL_T), lanes] = (acc[0:POOL_T] + negd[pl.ds(st, POOL_T), lanes]).astype(dz_ref.dtype)
            return carry

        lax.fori_loop(0, nb, pass_b, 0)

    seq_spec = pl.BlockSpec((seq, POOL_WIDTH), lambda b: (b, 0))
    return _pcall(
        body, name, (nseq,),
        [seq_spec, pl.BlockSpec((4, LANES, LANES), lambda b: (0, 0, 0)),
         pl.BlockSpec((1, POOL_WIDTH), lambda b: (0, 0)), seq_spec],
        [seq_spec, pl.BlockSpec((4, LANES, LANES), lambda b: (0, 0, 0)),
         pl.BlockSpec((8, POOL_WIDTH), lambda b: (0, 0))],
        [jax.ShapeDtypeStruct((nseq * seq, POOL_WIDTH), BF16),
         jax.ShapeDtypeStruct((4, LANES, LANES), F32),
         jax.ShapeDtypeStruct((8, POOL_WIDTH), F32)],
        [pltpu.VMEM((POOL_HALO + seq, POOL_WIDTH), F32),
         pltpu.VMEM((seq + POOL_HALO, POOL_WIDTH), F32),
         pltpu.VMEM((seq, POOL_WIDTH), F32)], (zp, wp, scale, dout), ("arbitrary",), comm)


GELU_C0 = 0.7978845608028654
GELU_C1 = 0.044715


def _gelu(xv):
    return xv * (0.5 * (1.0 + jnp.tanh(GELU_C0 * (xv + GELU_C1 * (xv * xv * xv)))))


def _gelu_grad(xv):
    t = jnp.tanh(GELU_C0 * (xv + GELU_C1 * (xv * xv * xv)))
    return 0.5 * (1.0 + t) + 0.5 * xv * (1.0 - t * t) * (GELU_C0 * (1.0 + 3.0 * GELU_C1 * xv * xv))


def _tril():
    ti = lax.broadcasted_iota(jnp.int32, (SGU_CHUNK, SGU_CHUNK), 0)
    si = lax.broadcasted_iota(jnp.int32, (SGU_CHUNK, SGU_CHUNK), 1)
    return si <= ti


def sgu_fwd(zs, ws, bst, ln, nseq, seq, name):
    nc = seq // SGU_CHUNK

    def body(z_ref, ws_ref, bs_ref, ln_ref, o_ref):
        tril = _tril()

        def blk(n, carry):
            st = pl.multiple_of(n * SGU_CHUNK, SGU_CHUNK)
            ge = _gelu(z_ref[pl.ds(st, SGU_CHUNK), :])
            uu, vv = ge[:, :SGU_WIDTH], ge[:, SGU_WIDTH:]
            mu = jnp.mean(vv, axis=-1, keepdims=True)
            xc = vv - mu
            rstd = lax.rsqrt(jnp.mean(xc * xc, axis=-1, keepdims=True) + EPS)
            vn = (xc * rstd * ln_ref[0:1, :] + ln_ref[1:2, :]).astype(BF16)
            for g in range(4):
                lanes = slice(g * LANES, (g + 1) * LANES)
                wm = jnp.where(tril, ws_ref[g], 0.0).astype(BF16)
                mixed = _dot(wm, vn[:, lanes]) + bs_ref[:, g:g + 1]
                o_ref[pl.ds(st, SGU_CHUNK), lanes] = (uu[:, lanes] * mixed).astype(o_ref.dtype)
            return carry

        lax.fori_loop(0, nc, blk, 0)

    return pl.pallas_call(
        body, name=name, grid=(nseq,),
        in_specs=[pl.BlockSpec((seq, 2 * SGU_WIDTH), lambda b: (b, 0)),
                  pl.BlockSpec((4, LANES, LANES), lambda b: (0, 0, 0)),
                  pl.BlockSpec((SGU_CHUNK, 4), lambda b: (0, 0)),
                  pl.BlockSpec((8, SGU_WIDTH), lambda b: (0, 0))],
        out_specs=pl.BlockSpec((seq, SGU_WIDTH), lambda b: (b, 0)),
        out_shape=jax.ShapeDtypeStruct((nseq * seq, SGU_WIDTH), BF16),
        compiler_params=_params("parallel"),
    )(zs, ws, bst, ln)


def sgu_bwd(zs, ws, bst, ln, dout, nseq, seq, name, comm=None):
    nc = seq // SGU_CHUNK

    def body(z_ref, ws_ref, bs_ref, ln_ref, do_ref, dz_ref, dws_ref, dbs_ref, dln_ref):
        @pl.when(pl.program_id(0) == 0)
        def _():
            dws_ref[...] = jnp.zeros(dws_ref.shape, F32)
            dbs_ref[...] = jnp.zeros(dbs_ref.shape, F32)
            dln_ref[...] = jnp.zeros(dln_ref.shape, F32)

        tril = _tril()

        def blk(n, carry):
            st = pl.multiple_of(n * SGU_CHUNK, SGU_CHUNK)
            zv = z_ref[pl.ds(st, SGU_CHUNK), :]
            ge = _gelu(zv)
            uu, vv = ge[:, :SGU_WIDTH], ge[:, SGU_WIDTH:]
            mu = jnp.mean(vv, axis=-1, keepdims=True)
            xc = vv - mu
            rstd = lax.rsqrt(jnp.mean(xc * xc, axis=-1, keepdims=True) + EPS)
            xh = xc * rstd
            vn = (xh * ln_ref[0:1, :] + ln_ref[1:2, :]).astype(BF16)
            dob = do_ref[pl.ds(st, SGU_CHUNK), :]
            du_cols, dvn_cols = [], []
            for g in range(4):
                lanes = slice(g * LANES, (g + 1) * LANES)
                wm = jnp.where(tril, ws_ref[g], 0.0).astype(BF16)
                mixed = _dot(wm, vn[:, lanes]) + bs_ref[:, g:g + 1]
                du_cols.append(dob[:, lanes] * mixed)
                dmix = dob[:, lanes] * uu[:, lanes]
                dbs_ref[g] += jnp.broadcast_to(jnp.sum(dmix, axis=-1, keepdims=True), (SGU_CHUNK, LANES))
                dmb = dmix.astype(BF16)
                dws_ref[g] += jnp.where(tril, _dot_nt(dmb, vn[:, lanes]), 0.0)
                dvn_cols.append(_dot_tn(wm, dmb))
            dvn = jnp.concatenate(dvn_cols, axis=-1)
            dln_ref[0:1, :] += jnp.sum(dvn * xh, axis=0, keepdims=True)
            dln_ref[1:2, :] += jnp.sum(dvn, axis=0, keepdims=True)
            dxh = dvn * ln_ref[0:1, :]
            dv = rstd * (dxh - jnp.mean(dxh, axis=-1, keepdims=True)
                         - xh * jnp.mean(dxh * xh, axis=-1, keepdims=True))
            dge = jnp.concatenate(du_cols + [dv], axis=-1)
            dz_ref[pl.ds(st, SGU_CHUNK), :] = (dge * _gelu_grad(zv)).astype(dz_ref.dtype)
            return carry

        lax.fori_loop(0, nc, blk, 0)

    w_spec = pl.BlockSpec((4, LANES, LANES), lambda b: (0, 0, 0))
    ln_spec = pl.BlockSpec((8, SGU_WIDTH), lambda b: (0, 0))
    return _pcall(
        body, name, (nseq,),
        [pl.BlockSpec((seq, 2 * SGU_WIDTH), lambda b: (b, 0)), w_spec,
         pl.BlockSpec((SGU_CHUNK, 4), lambda b: (0, 0)), ln_spec,
         pl.BlockSpec((seq, SGU_WIDTH), lambda b: (b, 0))],
        [pl.BlockSpec((seq, 2 * SGU_WIDTH), lambda b: (b, 0)), w_spec, w_spec, ln_spec],
        [jax.ShapeDtypeStruct((nseq * seq, 2 * SGU_WIDTH), BF16),
         jax.ShapeDtypeStruct((4, LANES, LANES), F32),
         jax.ShapeDtypeStruct((4, LANES, LANES), F32),
         jax.ShapeDtypeStruct((8, SGU_WIDTH), F32)],
        [], (zs, ws, bst, ln, dout), ("arbitrary",), comm)


def _ew_rows(rows, cols, nbuf):
    t = _row_tile(rows, 1024)
    while t > 8 and t * cols * 4 * nbuf * 2 > 24 * 2**20:
        t //= 2
    return t


def adamw(w, g, m, v, name):
    layers, rows, cols = w.shape
    tr = _ew_rows(rows, cols, 7)

    def body(w_ref, g_ref, m_ref, v_ref, d_ref, mo_ref, vo_ref):
        gv = g_ref[...]
        mn = ADAM_B1 * m_ref[...] + (1.0 - ADAM_B1) * gv
        vn = ADAM_B2 * v_ref[...] + (1.0 - ADAM_B2) * (gv * gv)
        m_hat = mn / (1.0 - ADAM_B1 ** ADAM_STEP)
        v_hat = vn / (1.0 - ADAM_B2 ** ADAM_STEP)
        d_ref[...] = -ADAM_LR * (m_hat / (jnp.sqrt(v_hat) + ADAM_EPS) + ADAM_WD * w_ref[...])
        mo_ref[...] = mn
        vo_ref[...] = vn

    spec = pl.BlockSpec((1, tr, cols), lambda l, i: (l, i, 0))
    return pl.pallas_call(
        body, name=name, grid=(layers, rows // tr),
        in_specs=[spec] * 4, out_specs=[spec] * 3,
        out_shape=[jax.ShapeDtypeStruct(w.shape, F32)] * 3,
        compiler_params=_params("parallel", "parallel"),
    )(w, g, m, v)


def adamw_many(ws, gs, ms, vs, name):
    n = len(ws)

    def body(*refs):
        w_refs, g_refs, m_refs, v_refs = refs[:n], refs[n:2 * n], refs[2 * n:3 * n], refs[3 * n:4 * n]
        d_refs, mo_refs, vo_refs = refs[4 * n:5 * n], refs[5 * n:6 * n], refs[6 * n:7 * n]
        for i in range(n):
            gv = g_refs[i][...]
            mn = ADAM_B1 * m_refs[i][...] + (1.0 - ADAM_B1) * gv
            vn = ADAM_B2 * v_refs[i][...] + (1.0 - ADAM_B2) * (gv * gv)
            m_hat = mn / (1.0 - ADAM_B1 ** ADAM_STEP)
            v_hat = vn / (1.0 - ADAM_B2 ** ADAM_STEP)
            d_refs[i][...] = -ADAM_LR * (m_hat / (jnp.sqrt(v_hat) + ADAM_EPS) + ADAM_WD * w_refs[i][...])
            mo_refs[i][...] = mn
            vo_refs[i][...] = vn

    vmem = pl.BlockSpec(memory_space=pltpu.VMEM)
    shapes = [jax.ShapeDtypeStruct(w.shape, F32) for w in ws]
    res = pl.pallas_call(
        body, name=name, in_specs=[vmem] * (4 * n), out_specs=[vmem] * (3 * n), out_shape=shapes * 3,
        compiler_params=pltpu.CompilerParams(vmem_limit_bytes=VMEM_LIMIT),
    )(*ws, *gs, *ms, *vs)
    return res[:n], res[n:2 * n], res[2 * n:]


def add_cast(a, b, name, dtype=BF16):
    nslab, rows, cols = a.shape
    tr = _ew_rows(rows, cols, 3)

    def body(a_ref, b_ref, o_ref):
        o_ref[...] = (a_ref[...] + b_ref[...]).astype(dtype)

    spec = pl.BlockSpec((1, tr, cols), lambda k, i: (k, i, 0))
    return pl.pallas_call(
        body, name=name, grid=(nslab, rows // tr),
        in_specs=[spec, spec], out_specs=spec,
        out_shape=jax.ShapeDtypeStruct(a.shape, dtype),
        compiler_params=_params("parallel", "parallel"),
    )(a, b)


def pair_sum(t, got, core, name):
    nslab, h, cols = got.shape
    tr = _ew_rows(h, cols, 3)
    nb = h // tr

    def body(c_ref, a_ref, b_ref, o_ref):
        o_ref[...] = (a_ref[...] + b_ref[...]).astype(BF16)

    spec = pl.BlockSpec((1, tr, cols), lambda k, i, c: (k, i, 0))
    return pl.pallas_call(
        body, name=name,
        grid_spec=pltpu.PrefetchScalarGridSpec(
            num_scalar_prefetch=1, grid=(nslab, nb),
            in_specs=[pl.BlockSpec((1, tr, cols), lambda k, i, c: (k, c[0] * nb + i, 0)), spec],
            out_specs=spec),
        out_shape=jax.ShapeDtypeStruct(got.shape, BF16),
        compiler_params=_params("parallel", "parallel"),
    )(core, t, got)


def sum_parts(parts, name, first=None):
    npart, rows, cols = parts.shape
    tr = _ew_rows(rows, cols, npart + 2)

    def body(*refs):
        p_ref, o_ref = refs[-2], refs[-1]
        acc = p_ref[0].astype(F32) if first is None else refs[0][...].astype(F32) + p_ref[0].astype(F32)
        for j in range(1, npart):
            acc = acc + p_ref[j].astype(F32)
        o_ref[...] = acc

    row = pl.BlockSpec((tr, cols), lambda i: (i, 0))
    ins = [parts] if first is None else [first, parts]
    return pl.pallas_call(
        body, name=name, grid=(rows // tr,),
        in_specs=([] if first is None else [row]) + [pl.BlockSpec((npart, tr, cols), lambda i: (0, i, 0))],
        out_specs=row,
        out_shape=jax.ShapeDtypeStruct((rows, cols), F32),
        compiler_params=_params("parallel"),
    )(*ins)


ANY = pl.BlockSpec(memory_space=pl.ANY)
MESH = pl.DeviceIdType.MESH


def _me():
    return lax.axis_index("x"), lax.axis_index("y"), lax.axis_index("c")


def _flip(pos, rel):
    return tuple(1 - p if f else p for p, f in zip(pos, rel))


SIBLING = (0, 0, 1)
OTHER_CHIPS = ((1, 0, 0), (0, 1, 0), (1, 1, 0))


def _chip_of(pos, rel=(0, 0, 0)):
    px, py, _ = _flip(pos, rel)
    return 2 * px + py


def allgather_blocks(shards, name):
    nt = len(shards)
    hs = [s.shape[0] // 2 for s in shards]

    def body(*refs):
        ins, outs = refs[:nt], refs[nt:2 * nt]
        send_sems, recv_sems, loc_sems = refs[2 * nt:]
        pos = _me()
        x, y, c = pos

        def block_id(rel):
            px, py, pc = _flip(pos, rel)
            return 4 * px + 2 * py + pc

        def copy(t, k, block_rel, to_rel, src=None):
            dst = outs[t].at[block_id(block_rel)]
            return pltpu.make_async_remote_copy(
                src_ref=dst if src is None else src, dst_ref=dst,
                send_sem=send_sems.at[t * 7 + k], recv_sem=recv_sems.at[t * 7 + k],
                device_id=_flip(pos, to_rel), device_id_type=MESH)

        own = [ins[t].at[pl.ds(c * hs[t], hs[t])] for t in range(nt)]
        mine = [pltpu.make_async_copy(own[t], outs[t].at[block_id((0, 0, 0))], loc_sems.at[t]) for t in range(nt)]
        for cp in mine:
            cp.start()
        first = []
        for t in range(nt):
            first.append(copy(t, 0, (0, 0, 0), SIBLING, src=own[t]))
            first += [copy(t, 1 + j, (0, 0, 0), rel, src=own[t]) for j, rel in enumerate(OTHER_CHIPS)]
        for cp in first:
            cp.start()
        passed = []
        for j, rel in enumerate(OTHER_CHIPS):
            for t in range(nt):
                copy(t, 1 + j, rel, (0, 0, 0)).wait_recv()
                fwd = copy(t, 4 + j, rel, SIBLING)
                fwd.start()
                passed.append(fwd)
        for t in range(nt):
            copy(t, 0, SIBLING, (0, 0, 0)).wait_recv()
            for j, rel in enumerate(OTHER_CHIPS):
                copy(t, 4 + j, (rel[0], rel[1], 1), (0, 0, 0)).wait_recv()
        for cp in first + passed:
            cp.wait_send()
        for cp in mine:
            cp.wait()

    return pl.pallas_call(
        body, name=name,
        in_specs=[ANY] * nt, out_specs=[ANY] * nt,
        out_shape=[jax.ShapeDtypeStruct((N_DEV, h, s.shape[1]), s.dtype) for h, s in zip(hs, shards)],
        scratch_shapes=[pltpu.SemaphoreType.DMA((7 * nt,)), pltpu.SemaphoreType.DMA((7 * nt,)),
                        pltpu.SemaphoreType.DMA((nt,))],
    )(*shards)


def _block_id(pos, rel=(0, 0, 0)):
    px, py, pc = _flip(pos, rel)
    return 4 * px + 2 * py + pc


def gather_first_hop(shards):
    hs = [s.shape[0] // 2 for s in shards]

    def plan(ins, outs, pos):
        me = _block_id(pos)
        remote = []
        for i, o, h in zip(ins, outs, hs):
            own = i.at[pl.ds(pos[2] * h, h)]
            remote += [(rel, own, o.at[me]) for rel in (SIBLING,) + OTHER_CHIPS]
        return remote

    return Comm(shards, [((N_DEV, h, s.shape[1]), s.dtype) for h, s in zip(hs, shards)], plan, 4 * len(shards))


def gather_second_hop(gathered):
    def plan(ins, outs, pos):
        remote = []
        for i, o in zip(ins, outs):
            for rel in OTHER_CHIPS:
                blk = _block_id(pos, rel)
                remote.append((SIBLING, i.at[blk], o.at[blk]))
        return remote

    return Comm(gathered, [(g.shape, g.dtype) for g in gathered], plan, 3 * len(gathered),
                aliases={i: i for i in range(len(gathered))})


def swap_comm(xs):
    def plan(ins, outs, pos):
        return [(SIBLING, i, o) for i, o in zip(ins, outs)]

    return Comm(list(xs), [(v.shape, v.dtype) for v in xs], plan, len(xs))


def give_half_comm(ts, plain=()):
    nt = len(ts)

    def plan(ins, outs, pos):
        remote = []
        for i, o in zip(ins[:nt], outs[:nt]):
            h = o.shape[1]
            remote.append((SIBLING, i.at[:, pl.ds((1 - pos[2]) * h, h)], o))
        return remote + [(SIBLING, i, o) for i, o in zip(ins[nt:], outs[nt:])]

    shapes = [((t.shape[0], t.shape[1] // 2, t.shape[2]), t.dtype) for t in ts] + [(v.shape, v.dtype) for v in plain]
    return Comm(list(ts) + list(plain), shapes, plan, nt + len(plain))


def chip_scatter_comm(xs, shared=None):
    nx = len(xs)

    def plan(ins, outs, pos):
        me = _chip_of(pos)
        remote = []
        for i, o in zip(ins[:nx], outs[:nx]):
            remote += [(rel, i.at[_chip_of(pos, rel)], o.at[j]) for j, rel in enumerate(OTHER_CHIPS)]
        if shared is not None:
            remote += [(rel, ins[nx], outs[nx].at[me]) for rel in OTHER_CHIPS]
        return remote

    shapes = [((3,) + v.shape[1:], v.dtype) for v in xs]
    if shared is not None:
        shapes.append(((N_CHIPS,) + shared.shape, shared.dtype))
    return Comm(list(xs) + ([] if shared is None else [shared]), shapes, plan, 3 * nx + (0 if shared is None else 3))


def tail_reduce(t, small, name):
    nslab, h2, cols = t.shape
    h = h2 // 2
    rows = small.shape[0]

    def body(t_ref, small_ref, mine_ref, theirs_ref, ssum_ref,
             got_pair, sums, got_chips, small_got, small_pair, small_chips, send_sems, recv_sems):
        pos = _me()
        core = pos[2]
        me = _chip_of(pos)

        def copy(i, rel, src, dst):
            return pltpu.make_async_remote_copy(src_ref=src, dst_ref=dst, send_sem=send_sems.at[i],
                                                recv_sem=recv_sems.at[i], device_id=_flip(pos, rel),
                                                device_id_type=MESH)

        pair = [copy(0, SIBLING, t_ref.at[:, pl.ds(pl.multiple_of((1 - core) * h, SUBLANES), h)], got_pair),
                copy(1, SIBLING, small_ref, small_got)]
        for cp in pair:
            cp.start()
        for cp in pair:
            cp.wait()
        for k in range(nslab):
            sums[k] = (t_ref[k, pl.ds(pl.multiple_of(core * h, SUBLANES), h), :] + got_pair[k]).astype(BF16)
        small_pair[...] = small_ref[...] + small_got[...]

        chips = []
        for j, rel in enumerate(OTHER_CHIPS):
            chips.append(copy(2 + j, rel, sums.at[_chip_of(pos, rel)], got_chips.at[j]))
            chips.append(copy(5 + j, rel, small_pair, small_chips.at[me]))
        for cp in chips:
            cp.start()
        small_chips[me] = small_pair[...]
        for cp in chips:
            cp.wait()
        acc = sums[me].astype(F32)
        for j in range(len(OTHER_CHIPS)):
            acc = acc + got_chips[j].astype(F32)
        mine_ref[...] = acc
        tot = small_chips[0]
        for k in range(1, N_CHIPS):
            tot = tot + small_chips[k]
        ssum_ref[...] = tot

        join = copy(8, SIBLING, mine_ref, theirs_ref)
        join.start()
        join.wait()

    vmem = pl.BlockSpec(memory_space=pltpu.VMEM)
    return pl.pallas_call(
        body, name=name, in_specs=[vmem, vmem], out_specs=[vmem, vmem, vmem],
        out_shape=[jax.ShapeDtypeStruct((h, cols), F32), jax.ShapeDtypeStruct((h, cols), F32),
                   jax.ShapeDtypeStruct((rows, LANES), F32)],
        scratch_shapes=[pltpu.VMEM((nslab, h, cols), F32), pltpu.VMEM((nslab, h, cols), BF16),
                        pltpu.VMEM((3, h, cols), BF16), pltpu.VMEM((rows, LANES), F32),
                        pltpu.VMEM((rows, LANES), F32), pltpu.VMEM((N_CHIPS, rows, LANES), F32),
                        pltpu.SemaphoreType.DMA((9,)), pltpu.SemaphoreType.DMA((9,))],
        compiler_params=pltpu.CompilerParams(vmem_limit_bytes=VMEM_LIMIT),
    )(t, small)


PACK_ROWS = 256


def _pack(arrs):
    parts, layout = [], []
    row = 0
    for a in arrs:
        flat = a.reshape(-1).astype(F32)
        size = flat.shape[0]
        rows = -(-size // (8 * LANES)) * 8
        flat = jnp.pad(flat, (0, rows * LANES - size))
        parts.append(flat.reshape(rows, LANES))
        layout.append((row, rows, size, a.shape))
        row += rows
    if row % PACK_ROWS:
        parts.append(jnp.zeros((PACK_ROWS - row % PACK_ROWS, LANES), F32))
    return jnp.concatenate(parts, axis=0), layout


def _unpack(packed, layout):
    return [packed[r0:r0 + rows].reshape(-1)[:size].reshape(shape) for r0, rows, size, shape in layout]


SMALL_REPL = ['mix_norm', 'a_b_in', 'a_sinks', 'a_conv_b', 'a_cln_g', 'a_cln_b', 'c_w_pool', 'c_w_s', 'c_b_s',
              'ffn_norm', 'final_norm']
SMALL_SHARD = ['a_conv_w', 'c_pool_scale', 'c_sln_g', 'c_sln_b']
BIG = ['a_w_in', 'a_w_out', 'c_w_in', 'c_w_out', 'ffn_w_gate', 'ffn_w_up', 'ffn_w_down']
TRANSPOSED = ('a_w_in', 'ffn_w_gate', 'ffn_w_up')
BIG_COL_SHARDED = {'c_w_in'}


def _full_weight(name, g8):
    _, h, cols = g8.shape
    g4 = g8.reshape(N_CHIPS, 2 * h, cols)
    if name not in BIG_COL_SHARDED:
        return g4.reshape(-1, cols)
    return jnp.transpose(g4, (1, 0, 2)).reshape(2 * h, N_CHIPS * cols)


def _to_shard_major(name, f):
    if name not in BIG_COL_SHARDED:
        return f.reshape(N_CHIPS, f.shape[0] // N_CHIPS, f.shape[1])
    r, cfull = f.shape
    return jnp.transpose(f.reshape(r, N_CHIPS, cfull // N_CHIPS), (1, 0, 2))


def kernel(*args):
    a = dict(zip(IN_NAMES, args))
    bl, seq, _ = a['x'].shape
    n = bl * seq
    x = a['x'].reshape(n, D_MODEL)
    target = a['loss_target'].reshape(n, D_MODEL)
    xi, yi, ci = _me()
    chip = 2 * xi + yi

    shard = {'a_w_in': a['a_w_in'][0].T, 'a_w_out': a['a_w_out'][0], 'c_w_in': a['c_w_in'][0], 'c_w_out': a['c_w_out'][0]}
    for layer in range(2):
        shard['gate' + str(layer)] = a['ffn_w_gate'][layer].T
        shard['up' + str(layer)] = a['ffn_w_up'][layer].T
        shard['down' + str(layer)] = a['ffn_w_down'][layer]
    shard = {k: v.astype(BF16) for k, v in shard.items()}
    core = ci.astype(jnp.int32).reshape(1)
    block_id = 4 * xi + 2 * yi + ci

    def first_hop(*names):
        return gather_first_hop([shard[k] for k in names])

    def finish(name, g8):
        h = shard[name].shape[0] // 2
        own = lax.dynamic_slice_in_dim(shard[name], ci * h, h, axis=0)
        return _full_weight(name, lax.dynamic_update_slice_in_dim(g8, own[None], block_id, axis=0))

    a_w_in_t = _full_weight('a_w_in', allgather_blocks([shard['a_w_in']], "gather_a_w_in")[0])
    in0_width = a_w_in_t.shape[0]
    small_shard_pack, small_shard_layout = _pack([a[k] for k in SMALL_SHARD])
    hop_a = first_hop('a_w_out', 'c_w_out')
    hop_s = chip_scatter_comm([], shared=small_shard_pack)
    mix_norm, ffn_norm = a['mix_norm'], a['ffn_norm']
    (hn0, q, kv, cc), outs = norm_inproj(
        x, mix_norm[0:1], a_w_in_t, a['a_b_in'],
        [(0, ATTN_WIDTH), (ATTN_WIDTH, ATTN_WIDTH + 2 * KV_WIDTH), (ATTN_WIDTH + 2 * KV_WIDTH, in0_width)],
        [BF16, BF16, F32], "in_proj0", comm=hop_a + hop_s, w_transposed=True)
    got_a, (ss,) = hop_a.split(outs, hop_s)
    ss = lax.dynamic_update_slice_in_dim(ss, small_shard_pack[None], chip, axis=0)
    ss_full = []
    for r0, rows, size, shape in small_shard_layout:
        per_chip = ss[:, r0:r0 + rows].reshape(N_CHIPS, -1)[:, :size].reshape((N_CHIPS,) + shape)
        ss_full.append(jnp.concatenate([per_chip[k] for k in range(N_CHIPS)], axis=-1))
    a_conv_w, c_pool_scale, c_sln_g, c_sln_b = [v[0] for v in ss_full]

    conv_taps = jnp.pad(a_conv_w, ((0, 32 - CONV_KERNEL), (0, 0)))
    conv_vec = jnp.pad(jnp.stack([a['a_conv_b'][0], a['a_cln_g'][0], a['a_cln_b'][0]]), ((0, 5), (0, 0)))
    sinks_b = jnp.pad(jnp.repeat(a['a_sinks'][0].reshape(N_KV_HEADS, GROUP), ATTN_BLOCK, axis=1), ((0, 6), (0, 0)))
    w_pool_bf = a['c_w_pool'][0].astype(BF16)
    pool_scale = c_pool_scale.reshape(1, POOL_WIDTH)
    w_s = a['c_w_s'][0]
    b_s_t = a['c_b_s'][0].T
    sgu_ln = jnp.pad(jnp.stack([c_sln_g, c_sln_b]), ((0, 6), (0, 0)))
    final_norm = a['final_norm'].reshape(1, D_MODEL)

    hop_b, pass_a = first_hop('gate0', 'c_w_in'), gather_second_hop(got_a)
    attn, outs = attn_fwd(q, kv, sinks_b, bl, seq, "attn_fwd", comm=hop_b + pass_a)
    got_b, done = hop_b.split(outs, pass_a)
    a_w_out, c_w_out = finish('a_w_out', done[0]), finish('c_w_out', done[1])

    hop_c, pass_b = first_hop('up0', 'down0'), gather_second_hop(got_b)
    (conv, conv_h1), outs = conv_fwd(cc, conv_taps, conv_vec, bl, seq, "conv_fwd", comm=hop_c + pass_b)
    got_c, done = hop_c.split(outs, pass_b)
    wg0, c_w_in = finish('gate0', done[0]), finish('c_w_in', done[1])

    h1, done = out_proj(x, attn, conv, a_w_out, "out_proj0", comm=gather_second_hop(got_c))
    wu0, wd0 = finish('up0', done[0]), finish('down0', done[1])

    (hnf0, g0, u0), got_e = ffn_gate_up(h1, ffn_norm[0:1], wg0, wu0, "ffn_gate_up0",
                                        comm=first_hop('gate1', 'up1', 'down1'))

    h2, done = ffn_down(h1, g0, u0, wd0, "ffn_down0", comm=gather_second_hop(got_e))
    wg1, wu1, wd1 = finish('gate1', done[0]), finish('up1', done[1]), finish('down1', done[2])
    wg, wu, wd = [wg0, wg1], [wu0, wu1], [wd0, wd1]

    (hn1, zp, zs), _ = norm_inproj(
        h2, mix_norm[1:2], c_w_in, jnp.zeros((1, c_w_in.shape[1]), F32),
        [(0, POOL_WIDTH), (POOL_WIDTH, c_w_in.shape[1])], [F32, F32], "in_proj1")
    pool = pool_fwd(zp, w_pool_bf, pool_scale, bl, seq, "pool_fwd")
    sgu = sgu_fwd(zs, w_s, b_s_t, sgu_ln, bl, seq, "sgu_fwd")
    h3, _ = out_proj(h2, pool, sgu, c_w_out, "out_proj1")
    (hnf1, g1, u1), _ = ffn_gate_up(h3, ffn_norm[1:2], wg1, wu1, "ffn_gate_up1")
    h4, _ = ffn_down(h3, g1, u1, wd1, "ffn_down1")

    dh4, d_final_norm, loss_local = loss_head(h4, final_norm, target, "loss_head")

    grads = {}
    pieces = {}

    def slabs_of(names, fulls):
        return [_to_shard_major(k, fulls[k]) for k in names]

    def pair_sums_of(names, slabs, gots):
        return [pair_sum(t, gt, core, "pair_sum_" + k) for k, t, gt in zip(names, slabs, gots)]

    def chip_sums_of(names, sums, from_chips):
        own = [lax.dynamic_index_in_dim(p, chip, axis=0, keepdims=False) for p in sums]
        return [sum_parts(p, "chip_sum_" + k, first=o) for k, p, o in zip(names, from_chips, own)]

    (dg, du, act), _ = ffn_down_bwd(dh4, g1, u1, wd[1], "ffn_down_bwd1")
    full1 = {'down1': mm_tn(act, dh4, "dw_down1"), 'gate1': mm_tn(dg, hnf1, "dw_gate1"),
             'up1': mm_tn(du, hnf1, "dw_up1")}
    names1 = ['gate1', 'up1', 'down1']
    slabs1 = slabs_of(names1, full1)
    dh3, d_ffn_norm1, got1 = proj_rms_bwd([dg, du], [wg[1], wu[1]], h3, ffn_norm[1:2], dh4, 1, "ffn_up_bwd1",
                                          tm_pref=512, w_transposed=True, comm=give_half_comm(slabs1))
    sums1 = pair_sums_of(names1, slabs1, got1)
    d_pool, d_sgu = out_proj_bwd(dh3, c_w_out, [F32, F32], "out_proj_bwd1")
    full1['c_w_out'] = jnp.concatenate([mm_tn(pool, dh3, "dw_out1_pool"), mm_tn(sgu, dh3, "dw_out1_sgu")], axis=0)
    (dzp, d_w_pool, d_pool_scale), from_gate = pool_bwd(zp, w_pool_bf, pool_scale, d_pool, bl, seq, "pool_bwd",
                                                        comm=chip_scatter_comm(sums1[0:1]))
    (dzs, d_w_s, d_b_s_b, d_sgu_ln), from_up = sgu_bwd(zs, w_s, b_s_t, sgu_ln, d_sgu, bl, seq, "sgu_bwd",
                                                       comm=chip_scatter_comm(sums1[1:2]))
    full1['c_w_in'] = jnp.concatenate([mm_tn(hn1, dzp, "dw_in1_pool"), mm_tn(hn1, dzs, "dw_in1_sgu")], axis=1)
    names1b = ['c_w_out', 'c_w_in']
    slabs1b = slabs_of(names1b, full1)
    heavy_pack, heavy_layout = _pack([d_w_pool[None], d_w_s[None]])
    chips_down, pair1b = chip_scatter_comm(sums1[2:3]), give_half_comm(slabs1b, plain=[heavy_pack])
    dh2, d_mix_norm1, outs = proj_rms_bwd([dzp, dzs], [c_w_in[:, :POOL_WIDTH], c_w_in[:, POOL_WIDTH:]], h2,
                                          mix_norm[1:2], dh3, 1, "in_proj_bwd1", comm=chips_down + pair1b)
    from_down, got1b = chips_down.split(outs, pair1b)
    mine1 = chip_sums_of(names1, sums1, from_gate + from_up + from_down)
    sums1b = pair_sums_of(names1b, slabs1b, got1b[:2])
    heavy_pair = add_cast(heavy_pack[None], got1b[2][None], "pair_sum_heavy", dtype=F32)[0]

    join1, chips1b = swap_comm(mine1), chip_scatter_comm(sums1b, shared=heavy_pair)
    (dg, du, act), outs = ffn_down_bwd(dh2, g0, u0, wd[0], "ffn_down_bwd0", comm=join1 + chips1b)
    theirs1, from_chips1b = join1.split(outs, chips1b)
    pieces.update({k: (m, t) for k, m, t in zip(names1, mine1, theirs1)})
    mine1b = chip_sums_of(names1b, sums1b, from_chips1b[:2])
    heavy_chips = lax.dynamic_update_slice_in_dim(from_chips1b[2], heavy_pair[None], chip, axis=0)
    grads['c_w_pool'], grads['c_w_s'] = _unpack(sum_parts(heavy_chips, "heavy_sum"), heavy_layout)
    full0 = {'down0': mm_tn(act, dh2, "dw_down0"), 'gate0': mm_tn(dg, hnf0, "dw_gate0"),
             'up0': mm_tn(du, hnf0, "dw_up0")}
    names0 = ['gate0', 'up0', 'down0']
    slabs0 = slabs_of(names0, full0)
    join1b, pair0 = swap_comm(mine1b), give_half_comm(slabs0)
    dh1, d_ffn_norm0, outs = proj_rms_bwd([dg, du], [wg[0], wu[0]], h1, ffn_norm[0:1], dh2, 1, "ffn_up_bwd0",
                                          tm_pref=512, comm=join1b + pair0, w_transposed=True)
    theirs1b, got0 = join1b.split(outs, pair0)
    pieces.update({k: (m, t) for k, m, t in zip(names1b, mine1b, theirs1b)})
    sums0 = pair_sums_of(names0, slabs0, got0)

    d_attn, d_conv = out_proj_bwd(dh1, a_w_out, [BF16, F32], "out_proj_bwd0")
    full_o = {'a_w_out': jnp.concatenate([mm_tn(attn, dh1, "dw_out0_attn"), mm_tn(conv, dh1, "dw_out0_conv")], axis=0)}
    slabs_o = slabs_of(['a_w_out'], full_o)
    chips0, pair_o = chip_scatter_comm(sums0), give_half_comm(slabs_o)
    (dq, dkv, d_sinks_b), outs = attn_bwd(q, kv, sinks_b, d_attn, bl, seq, "attn_bwd", comm=chips0 + pair_o)
    from_chips0, got_o = chips0.split(outs, pair_o)
    mine0 = chip_sums_of(names0, sums0, from_chips0)
    sums_o = pair_sums_of(['a_w_out'], slabs_o, got_o)
    join0, chips_o = swap_comm(mine0), chip_scatter_comm(sums_o)
    (dcc, d_conv_taps, d_conv_vec), outs = conv_bwd(cc, conv_h1, conv_taps, conv_vec, d_conv, bl, seq, "conv_bwd",
                                                    comm=join0 + chips_o)
    theirs0, from_chips_o = join0.split(outs, chips_o)
    pieces.update({k: (m, t) for k, m, t in zip(names0, mine0, theirs0)})
    mine_o = chip_sums_of(['a_w_out'], sums_o, from_chips_o)
    kq, kk = ATTN_WIDTH, ATTN_WIDTH + 2 * KV_WIDTH
    grad_x, d_mix_norm0, _ = proj_rms_bwd([dq, dkv, dcc], [a_w_in_t[:kq], a_w_in_t[kq:kk], a_w_in_t[kk:]], x,
                                          mix_norm[0:1], dh1, 1, "in_proj_bwd0", w_transposed=True)
    dw_q, db_q = mm_tn(dq, hn0, "dw_in0_q", xsum=True)
    dw_kv, db_kv = mm_tn(dkv, hn0, "dw_in0_kv", xsum=True)
    (dw_c, db_c), theirs_o = mm_tn(dcc, hn0, "dw_in0_c", xsum=True, comm=swap_comm(mine_o))
    pieces['a_w_out'] = (mine_o[0], theirs_o[0])
    d_a_b_in = jnp.concatenate([db_q, db_kv, db_c], axis=0)
    slabs_i = slabs_of(['a_w_in'], {'a_w_in': jnp.concatenate([dw_q, dw_kv, dw_c], axis=0)})

    small_full = {
        'mix_norm': jnp.stack([d_mix_norm0, d_mix_norm1]), 'a_b_in': d_a_b_in[None], 'a_sinks': d_sinks_b[:, 0][None],
        'a_conv_w': d_conv_taps[:CONV_KERNEL][None], 'a_conv_b': d_conv_vec[0][None], 'a_cln_g': d_conv_vec[1][None],
        'a_cln_b': d_conv_vec[2][None], 'c_pool_scale': d_pool_scale[0][None],
        'c_sln_g': d_sgu_ln[0][None], 'c_sln_b': d_sgu_ln[1][None],
        'c_b_s': d_b_s_b[:, :, 0][None], 'ffn_norm': jnp.stack([d_ffn_norm0, d_ffn_norm1]),
        'final_norm': d_final_norm, 'loss': loss_local.reshape(1)}
    small_names = SMALL_REPL + SMALL_SHARD
    tail_names = [k for k in small_names if k in small_full] + ['loss']
    small_pack, small_layout = _pack([small_full[k] for k in tail_names])

    mine_i, theirs_i, small_sum = tail_reduce(slabs_i[0], small_pack, "tail_reduce")
    pieces['a_w_in'] = (mine_i, theirs_i)

    def whole(name):
        mine, theirs = pieces[name]
        return jnp.concatenate([jnp.where(ci == 0, mine, theirs), jnp.where(ci == 0, theirs, mine)], axis=0)

    for k in ('a_w_in', 'a_w_out', 'c_w_in', 'c_w_out'):
        grads[k] = whole(k)[None]
    for short, key in (('gate', 'ffn_w_gate'), ('up', 'ffn_w_up'), ('down', 'ffn_w_down')):
        grads[key] = jnp.stack([whole(short + '0'), whole(short + '1')])

    for k, g in zip(tail_names, _unpack(small_sum, small_layout)):
        if k in SMALL_SHARD:
            width = a[k].shape[-1]
            g = lax.dynamic_slice_in_dim(g, chip * width, width, axis=g.ndim - 1)
        grads[k] = g
    loss = grads.pop('loss')[0]

    delta, new_m, new_v = {}, {}, {}
    for k in BIG:
        if k in TRANSPOSED:
            flip = lambda t: jnp.swapaxes(t, 1, 2)
            d, m, v = adamw(flip(a[k]), grads[k], flip(a['m_' + k]), flip(a['v_' + k]), "adamw_" + k)
            grads[k], delta[k], new_m[k], new_v[k] = flip(grads[k]), flip(d), flip(m), flip(v)
        else:
            delta[k], new_m[k], new_v[k] = adamw(a[k], grads[k], a['m_' + k], a['v_' + k], "adamw_" + k)
    two_d = lambda t: t.reshape(1, -1) if t.ndim == 1 else t
    ds, ms, vs = adamw_many([two_d(a[k]) for k in small_names], [two_d(grads[k]) for k in small_names],
                            [two_d(a['m_' + k]) for k in small_names], [two_d(a['v_' + k]) for k in small_names],
                            "adamw_small")
    for k, dv, mv, vv in zip(small_names, ds, ms, vs):
        delta[k], new_m[k], new_v[k] = [t.reshape(a[k].shape) for t in (dv, mv, vv)]

    return (loss, grad_x.reshape(a['x'].shape), *[grads[k] for k in WEIGHTS], *[delta[k] for k in WEIGHTS],
            *[new_m[k] for k in WEIGHTS], *[new_v[k] for k in WEIGHTS])
```

```python
import functools

import jax
import jax.numpy as jnp
from jax import lax
from jax.experimental import pallas as pl
from jax.experimental.pallas import tpu as pltpu

F32 = jnp.float32
BF16 = jnp.bfloat16

D_MODEL = 1024
EPS = 1e-5
N_Q_HEADS, N_KV_HEADS, HEAD_DIM = 8, 2, 64
ATTN_BLOCK = 128
ATTN_WIDTH = N_Q_HEADS * HEAD_DIM
KV_WIDTH = N_KV_HEADS * HEAD_DIM
CONV_WIDTH = 512
CONV_KERNEL = 31
CONV_HALO = 32
POOL_WINDOWS = (2, 4, 8, 16)
POOL_WIDTH = 512
POOL_HALO = 16
SGU_WIDTH = 512
SGU_CHUNK = 128
D_FF = 2816
FF_CHUNK = 128
MXU_COLS = 256
FFN_AHEAD = 1
LANES = 128
N_CHIPS = 4
N_DEV = 8

ADAM_LR, ADAM_B1, ADAM_B2, ADAM_EPS, ADAM_WD, ADAM_STEP = 0.001, 0.9, 0.999, 1e-08, 0.01, 10

VMEM_LIMIT = 56 * 2**20

WEIGHTS = ['mix_norm', 'a_w_in', 'a_b_in', 'a_sinks', 'a_conv_w', 'a_conv_b', 'a_cln_g', 'a_cln_b', 'a_w_out',
           'c_w_in', 'c_w_pool', 'c_pool_scale', 'c_sln_g', 'c_sln_b', 'c_w_s', 'c_b_s', 'c_w_out',
           'ffn_norm', 'ffn_w_gate', 'ffn_w_up', 'ffn_w_down', 'final_norm']
IN_NAMES = (['x'] + WEIGHTS + ['loss_target'] + ['m_' + n for n in WEIGHTS] + ['v_' + n for n in WEIGHTS])


def _params(*sem):
    return pltpu.CompilerParams(dimension_semantics=sem, vmem_limit_bytes=VMEM_LIMIT)


def _dot(a, b):
    return jnp.dot(a, b, preferred_element_type=F32)


def _dot_nt(a, b):
    return lax.dot_general(a, b, (((1,), (1,)), ((), ())), preferred_element_type=F32)


def _dot_tn(a, b):
    return lax.dot_general(a, b, (((0,), (0,)), ((), ())), preferred_element_type=F32)


def _sigmoid(v):
    return 0.5 * jnp.tanh(0.5 * v) + 0.5


def _row_tile(n, pref):
    t = min(n, pref)
    while n % t:
        t //= 2
    return t


def _col_tile(m, rows, budget=6 * 2**20):
    best = LANES
    for t in range(LANES, m + 1, LANES):
        if m % t == 0 and rows * t * 4 <= budget:
            best = t
    return best


class Comm:
    def __init__(self, ins, out_shapes, plan, count, aliases=None):
        self.ins, self.out_shapes, self.plan, self.count, self.aliases = ins, out_shapes, plan, count, aliases or {}

    def __add__(self, other):
        ni, no = len(self.ins), len(self.out_shapes)

        def plan(ins, outs, pos):
            return self.plan(ins[:ni], outs[:no], pos) + other.plan(ins[ni:], outs[no:], pos)

        aliases = dict(self.aliases)
        aliases.update({ni + i: no + o for i, o in other.aliases.items()})
        return Comm(list(self.ins) + list(other.ins), list(self.out_shapes) + list(other.out_shapes), plan,
                    self.count + other.count, aliases)

    def split(self, outs, other):
        return outs[:len(self.out_shapes)], outs[len(self.out_shapes):]


def _pcall(body, name, grid, in_specs, out_specs, out_shape, scratch_shapes, args, sem, comm=None):
    single = not isinstance(out_shape, (list, tuple))
    if single:
        out_specs, out_shape = [out_specs], [out_shape]
    if comm is None:
        res = pl.pallas_call(body, name=name, grid=grid, in_specs=in_specs, out_specs=list(out_specs),
                             out_shape=list(out_shape), scratch_shapes=list(scratch_shapes),
                             compiler_params=_params(*sem))(*args)
        return (res[0] if single else res), []
    na, nci, no, nco, ns = len(args), len(comm.ins), len(out_shape), len(comm.out_shapes), len(scratch_shapes)

    def wrapped(*refs):
        a_refs, ci_refs = refs[:na], refs[na:na + nci]
        o_refs, co_refs = refs[na + nci:na + nci + no], refs[na + nci + no:na + nci + no + nco]
        s_refs = refs[na + nci + no + nco:na + nci + no + nco + ns]
        send_sems, recv_sems = refs[-2], refs[-1]
        pos = _me()

        def copies():
            return [pltpu.make_async_remote_copy(src_ref=s, dst_ref=d, send_sem=send_sems.at[i],
                                                 recv_sem=recv_sems.at[i], device_id=_flip(pos, rel),
                                                 device_id_type=MESH)
                    for i, (rel, s, d) in enumerate(comm.plan(ci_refs, co_refs, pos))]

        first, last = None, None
        for d, size in enumerate(grid):
            f, l = pl.program_id(d) == 0, pl.program_id(d) == size - 1
            first = f if first is None else first & f
            last = l if last is None else last & l

        @pl.when(first)
        def _():
            for cp in copies():
                cp.start()

        body(*a_refs, *o_refs, *s_refs)

        @pl.when(last)
        def _():
            for cp in copies():
                cp.wait()

    res = pl.pallas_call(
        wrapped, name=name, grid=grid,
        in_specs=list(in_specs) + [ANY] * nci, out_specs=list(out_specs) + [ANY] * nco,
        out_shape=list(out_shape) + [jax.ShapeDtypeStruct(s, d) for s, d in comm.out_shapes],
        scratch_shapes=list(scratch_shapes) + [pltpu.SemaphoreType.DMA((comm.count,)),
                                               pltpu.SemaphoreType.DMA((comm.count,))],
        input_output_aliases={na + i: no + o for i, o in comm.aliases.items()},
        compiler_params=_params(*(["arbitrary"] * len(grid))),
    )(*args, *comm.ins)
    outs = res[:no]
    return (outs[0] if single else outs), list(res[no:])


def norm_inproj(x, gain, w, bias, splits, dtypes, name, comm=None, w_transposed=False):
    n = x.shape[0]
    m = w.shape[0] if w_transposed else w.shape[1]
    tm = _row_tile(n, 1024)

    def body(x_ref, g_ref, w_ref, b_ref, hn_ref, *outs):
        xv = x_ref[...]
        r = lax.rsqrt(jnp.mean(xv * xv, axis=-1, keepdims=True) + EPS)
        hn = ((xv * r) * g_ref[...]).astype(BF16)
        hn_ref[...] = hn
        z = (_dot_nt if w_transposed else _dot)(hn, w_ref[...]) + b_ref[...]
        for o, (lo, hi) in zip(outs, splits):
            o[...] = z[:, lo:hi].astype(o.dtype)

    out_shape = [jax.ShapeDtypeStruct((n, D_MODEL), BF16)]
    out_specs = [pl.BlockSpec((tm, D_MODEL), lambda i: (i, 0))]
    for (lo, hi), dt in zip(splits, dtypes):
        out_shape.append(jax.ShapeDtypeStruct((n, hi - lo), dt))
        out_specs.append(pl.BlockSpec((tm, hi - lo), lambda i: (i, 0)))
    return _pcall(
        body, name, (n // tm,),
        [pl.BlockSpec((tm, D_MODEL), lambda i: (i, 0)),
         pl.BlockSpec((1, D_MODEL), lambda i: (0, 0)),
         pl.BlockSpec(w.shape, lambda i: (0, 0)),
         pl.BlockSpec((1, m), lambda i: (0, 0))],
        out_specs, out_shape, [], (x, gain, w, bias), ("parallel",), comm)


def out_proj(res, m1, m2, w, name, comm=None):
    n = res.shape[0]
    k1, k2 = m1.shape[1], m2.shape[1]
    assert k1 == k2
    tm = _row_tile(n, 1024)

    def body(r_ref, a_ref, b_ref, w1_ref, w2_ref, o_ref):
        o_ref[...] = r_ref[...] + _dot(a_ref[...], w1_ref[...]) + _dot(b_ref[...], w2_ref[...])

    return _pcall(
        body, name, (n // tm,),
        [pl.BlockSpec((tm, D_MODEL), lambda i: (i, 0)),
         pl.BlockSpec((tm, k1), lambda i: (i, 0)),
         pl.BlockSpec((tm, k2), lambda i: (i, 0)),
         pl.BlockSpec((k1, D_MODEL), lambda i: (0, 0)),
         pl.BlockSpec((k2, D_MODEL), lambda i: (1, 0))],
        pl.BlockSpec((tm, D_MODEL), lambda i: (i, 0)),
        jax.ShapeDtypeStruct((n, D_MODEL), F32), [], (res, m1, m2, w, w), ("parallel",), comm)


def ffn_gate_up(h, gain, wg_t, wu_t, name, comm=None):
    n = h.shape[0]
    tm = _row_tile(n, 512)
    th = D_FF

    def body(h_ref, g_ref, wg_ref, wu_ref, hn_ref, go_ref, uo_ref):
        @pl.when(pl.program_id(1) == 0)
        def _():
            xv = h_ref[...]
            r = lax.rsqrt(jnp.mean(xv * xv, axis=-1, keepdims=True) + EPS)
            hn_ref[...] = ((xv * r) * g_ref[...]).astype(BF16)

        hn = hn_ref[...]
        go_ref[...] = _dot_nt(hn, wg_ref[...]).astype(BF16)
        uo_ref[...] = _dot_nt(hn, wu_ref[...]).astype(BF16)

    return _pcall(
        body, name, (n // tm, D_FF // th),
        [pl.BlockSpec((tm, D_MODEL), lambda i, j: (i, 0)),
         pl.BlockSpec((1, D_MODEL), lambda i, j: (0, 0)),
         pl.BlockSpec((th, D_MODEL), lambda i, j: (j, 0), pipeline_mode=pl.Buffered(1)),
         pl.BlockSpec((th, D_MODEL), lambda i, j: (j, 0), pipeline_mode=pl.Buffered(1))],
        [pl.BlockSpec((tm, D_MODEL), lambda i, j: (i, 0)),
         pl.BlockSpec((tm, th), lambda i, j: (i, j)),
         pl.BlockSpec((tm, th), lambda i, j: (i, j))],
        [jax.ShapeDtypeStruct((n, D_MODEL), BF16),
         jax.ShapeDtypeStruct((n, D_FF), BF16),
         jax.ShapeDtypeStruct((n, D_FF), BF16)],
        [], (h, gain, wg_t, wu_t), ("parallel", "arbitrary"), comm)


def ffn_down(h, g, u, wd, name, comm=None):
    n = h.shape[0]
    tm = _row_tile(n, 1024)

    def body(h_ref, g_ref, u_ref, w_ref, o_ref, a_ref):
        for c0 in range(0, D_FF, FF_CHUNK):
            gv = g_ref[:, c0:c0 + FF_CHUNK]
            a_ref[:, c0:c0 + FF_CHUNK] = gv * _sigmoid(gv) * u_ref[:, c0:c0 + FF_CHUNK]
        o_ref[...] = h_ref[...] + _dot(a_ref[...], w_ref[...])

    return _pcall(
        body, name, (n // tm,),
        [pl.BlockSpec((tm, D_MODEL), lambda i: (i, 0)),
         pl.BlockSpec((tm, D_FF), lambda i: (i, 0)),
         pl.BlockSpec((tm, D_FF), lambda i: (i, 0)),
         pl.BlockSpec((D_FF, D_MODEL), lambda i: (0, 0), pipeline_mode=pl.Buffered(1))],
        pl.BlockSpec((tm, D_MODEL), lambda i: (i, 0)),
        jax.ShapeDtypeStruct((n, D_MODEL), F32),
        [pltpu.VMEM((tm, D_FF), BF16)], (h, g, u, wd), ("parallel",), comm)


def ffn_down_bwd(dh, g, u, wd, name, comm=None):
    n = dh.shape[0]
    tm = _row_tile(n, 512)

    def body(dh_ref, g_ref, u_ref, w_ref, dg_ref, du_ref, a_ref):
        dhb = dh_ref[...].astype(BF16)
        chunks = [slice(c0, c0 + MXU_COLS) for c0 in range(0, D_FF, MXU_COLS)]
        ahead = [_dot_nt(dhb, w_ref[c, :]) for c in chunks[:FFN_AHEAD]]
        for i, cols in enumerate(chunks):
            da = ahead.pop(0).astype(BF16)
            if i + FFN_AHEAD < len(chunks):
                ahead.append(_dot_nt(dhb, w_ref[chunks[i + FFN_AHEAD], :]))
            gv, uv = g_ref[:, cols], u_ref[:, cols]
            sg = _sigmoid(gv)
            act = gv * sg
            dg_ref[:, cols] = (da * uv) * (sg + act * (1.0 - sg))
            du_ref[:, cols] = da * act
            a_ref[:, cols] = act * uv

    spec_h = pl.BlockSpec((tm, D_FF), lambda i: (i, 0))
    return _pcall(
        body, name, (n // tm,),
        [pl.BlockSpec((tm, D_MODEL), lambda i: (i, 0)), spec_h, spec_h,
         pl.BlockSpec((D_FF, D_MODEL), lambda i: (0, 0))],
        [spec_h, spec_h, spec_h], [jax.ShapeDtypeStruct((n, D_FF), BF16)] * 3,
        [], (dh, g, u, wd), ("parallel",), comm)


def mm_tn(x, dy, name, xsum=False, comm=None):
    n, k = x.shape
    m = dy.shape[1]
    tk = _col_tile(k, m)
    tt = _row_tile(n, 2048)

    def body(x_ref, dy_ref, o_ref, *rest):
        xt_ref = rest[-1]
        t = pl.program_id(1)
        xv = x_ref[...]
        xt_ref[...] = xv.astype(BF16).T
        part = _dot(xt_ref[...], dy_ref[...].astype(BF16))

        @pl.when(t == 0)
        def _():
            o_ref[...] = part

        @pl.when(t > 0)
        def _():
            o_ref[...] += part

        if xsum:
            cs = jnp.broadcast_to(jnp.sum(xv.astype(F32), axis=0, keepdims=True), rest[0].shape)

            @pl.when(t == 0)
            def _():
                rest[0][...] = cs

            @pl.when(t > 0)
            def _():
                rest[0][...] += cs

    out_shape = [jax.ShapeDtypeStruct((k, m), F32)]
    out_specs = [pl.BlockSpec((tk, m), lambda j, t: (j, 0))]
    if xsum:
        out_shape.append(jax.ShapeDtypeStruct((8, k), F32))
        out_specs.append(pl.BlockSpec((8, tk), lambda j, t: (0, j)))
    res, comm_outs = _pcall(
        body, name, (k // tk, n // tt),
        [pl.BlockSpec((tt, tk), lambda j, t: (t, j)),
         pl.BlockSpec((tt, m), lambda j, t: (t, 0))],
        out_specs, out_shape, [pltpu.VMEM((tk, tt), BF16)], (x, dy), ("arbitrary", "arbitrary"), comm)
    res = (res[0], res[1][0]) if xsum else res[0]
    return res if comm is None else (res, comm_outs)


def out_proj_bwd(dh, w, dtypes, name):
    n = dh.shape[0]
    k = w.shape[0]
    half = k // 2
    tm = _row_tile(n, 1024)

    def body(dh_ref, w_ref, a_ref, b_ref):
        dm = _dot_nt(dh_ref[...].astype(BF16), w_ref[...])
        a_ref[...] = dm[:, :half].astype(a_ref.dtype)
        b_ref[...] = dm[:, half:].astype(b_ref.dtype)

    return pl.pallas_call(
        body, name=name, grid=(n // tm,),
        in_specs=[pl.BlockSpec((tm, D_MODEL), lambda i: (i, 0)),
                  pl.BlockSpec((k, D_MODEL), lambda i: (0, 0))],
        out_specs=[pl.BlockSpec((tm, half), lambda i: (i, 0))] * 2,
        out_shape=[jax.ShapeDtypeStruct((n, half), dtypes[0]), jax.ShapeDtypeStruct((n, half), dtypes[1])],
        compiler_params=_params("parallel"),
    )(dh, w)


def proj_rms_bwd(dys, ws, h_in, gain, dres, nk, name, tm_pref=512, comm=None, w_transposed=False):
    n = h_in.shape[0]
    npair = len(dys)
    tm = _row_tile(n, tm_pref)
    tks = [dy.shape[1] // nk for dy in dys]
    mm = _dot if w_transposed else _dot_nt

    def body(*refs):
        dy_refs = refs[:npair]
        w_refs = refs[npair:2 * npair]
        h_ref, g_ref, dr_ref, o_ref, dg_ref, acc_ref = refs[2 * npair:]
        i, k = pl.program_id(0), pl.program_id(1)
        part = mm(dy_refs[0][...], w_refs[0][...])
        for p in range(1, npair):
            part = part + mm(dy_refs[p][...], w_refs[p][...])

        @pl.when(k == 0)
        def _():
            acc_ref[...] = part

        @pl.when(k > 0)
        def _():
            acc_ref[...] += part

        @pl.when(k == nk - 1)
        def _():
            dhn = acc_ref[...]
            xv = h_ref[...]
            r = lax.rsqrt(jnp.mean(xv * xv, axis=-1, keepdims=True) + EPS)
            xh = xv * r
            uv = dhn * g_ref[...]
            o_ref[...] = dr_ref[...] + r * (uv - xh * jnp.mean(uv * xh, axis=-1, keepdims=True))
            dgp = jnp.broadcast_to(jnp.sum(dhn * xh, axis=0, keepdims=True), dg_ref.shape)

            @pl.when(i == 0)
            def _():
                dg_ref[...] = dgp

            @pl.when(i > 0)
            def _():
                dg_ref[...] += dgp

    row = pl.BlockSpec((tm, D_MODEL), lambda i, k: (i, 0))
    in_specs = [pl.BlockSpec((tm, tk), lambda i, k: (i, k)) for tk in tks]
    once = dict(pipeline_mode=pl.Buffered(1)) if nk == 1 else {}
    if w_transposed:
        in_specs += [pl.BlockSpec((tk, D_MODEL), lambda i, k: (k, 0), **once) for tk in tks]
    else:
        in_specs += [pl.BlockSpec((D_MODEL, tk), lambda i, k: (0, k), **once) for tk in tks]
    in_specs += [row, pl.BlockSpec((1, D_MODEL), lambda i, k: (0, 0)), row]
    (dh, dgain), comm_outs = _pcall(
        body, name, (n // tm, nk), in_specs,
        [row, pl.BlockSpec((8, D_MODEL), lambda i, k: (0, 0))],
        [jax.ShapeDtypeStruct((n, D_MODEL), F32), jax.ShapeDtypeStruct((8, D_MODEL), F32)],
        [pltpu.VMEM((tm, D_MODEL), F32)], (*dys, *ws, h_in, gain, dres), ("arbitrary", "arbitrary"), comm)
    return dh, dgain[0], comm_outs


def loss_head(h, gain, target, name):
    n = h.shape[0]
    tm = _row_tile(n, 512)

    def body(h_ref, g_ref, t_ref, dh_ref, dg_ref, l_ref):
        i = pl.program_id(0)
        xv = h_ref[...]
        r = lax.rsqrt(jnp.mean(xv * xv, axis=-1, keepdims=True) + EPS)
        xh = xv * r
        err = xh * g_ref[...] - t_ref[...]
        dy = err * (1.0 / D_MODEL)
        uv = dy * g_ref[...]
        dh_ref[...] = r * (uv - xh * jnp.mean(uv * xh, axis=-1, keepdims=True))
        dgp = jnp.broadcast_to(jnp.sum(dy * xh, axis=0, keepdims=True), dg_ref.shape)
        lp = jnp.sum(jnp.sum(err * err, axis=-1, keepdims=True), axis=0, keepdims=True) * (0.5 / D_MODEL)
        lp = jnp.broadcast_to(lp, l_ref.shape)

        @pl.when(i == 0)
        def _():
            dg_ref[...] = dgp
            l_ref[...] = lp

        @pl.when(i > 0)
        def _():
            dg_ref[...] += dgp
            l_ref[...] += lp

    row = pl.BlockSpec((tm, D_MODEL), lambda i: (i, 0))
    dh, dg, l = pl.pallas_call(
        body, name=name, grid=(n // tm,),
        in_specs=[row, pl.BlockSpec((1, D_MODEL), lambda i: (0, 0)), row],
        out_specs=[row, pl.BlockSpec((8, D_MODEL), lambda i: (0, 0)), pl.BlockSpec((8, LANES), lambda i: (0, 0))],
        out_shape=[jax.ShapeDtypeStruct((n, D_MODEL), F32), jax.ShapeDtypeStruct((8, D_MODEL), F32),
                   jax.ShapeDtypeStruct((8, LANES), F32)],
        compiler_params=_params("arbitrary"),
    )(h, gain, target)
    return dh, dg[0], l[0, 0]


GROUP = N_Q_HEADS // N_KV_HEADS
GQ = GROUP * ATTN_BLOCK


def _attn_mask_t(n):
    r = lax.broadcasted_iota(jnp.int32, (2 * ATTN_BLOCK, GQ), 0)
    qi = lax.broadcasted_iota(jnp.int32, (2 * ATTN_BLOCK, GQ), 1) & (ATTN_BLOCK - 1)
    band = (r > qi) & (r <= qi + ATTN_BLOCK)
    return band & ((r >= ATTN_BLOCK) | (n > 0))


def _stack_heads(blk, kh):
    return jnp.concatenate([blk[:, (kh * GROUP + g) * HEAD_DIM:(kh * GROUP + g + 1) * HEAD_DIM]
                            for g in range(GROUP)], axis=0)


def _attn_probs_t(kk, qs, mask, sink):
    s = _dot_nt(kk, qs) * (HEAD_DIM ** -0.5)
    s = jnp.where(mask, s, -1e30)
    m = jnp.maximum(jnp.max(s, axis=0, keepdims=True), sink)
    p = jnp.exp(s - m)
    esink = jnp.exp(sink - m)
    inv = 1.0 / (jnp.sum(p, axis=0, keepdims=True) + esink)
    return p * inv, esink * inv


def attn_fwd(q, kv, sinks_t, nseq, seq, name, comm=None):
    nb = seq // ATTN_BLOCK

    def body(q_ref, kv_ref, s_ref, o_ref, kvp):
        kvp[0:ATTN_BLOCK, :] = jnp.zeros((ATTN_BLOCK, 2 * KV_WIDTH), BF16)
        kvp[ATTN_BLOCK:, :] = kv_ref[...]

        def blk(n, carry):
            st = pl.multiple_of(n * ATTN_BLOCK, ATTN_BLOCK)
            qb = q_ref[pl.ds(st, ATTN_BLOCK), :]
            kw = kvp[pl.ds(st, 2 * ATTN_BLOCK), :]
            mask = _attn_mask_t(n)
            for kh in range(N_KV_HEADS):
                kk = kw[:, kh * HEAD_DIM:(kh + 1) * HEAD_DIM]
                vv = kw[:, KV_WIDTH + kh * HEAD_DIM:KV_WIDTH + (kh + 1) * HEAD_DIM]
                probs, _ = _attn_probs_t(kk, _stack_heads(qb, kh), mask, s_ref[kh:kh + 1, :])
                ot = _dot_tn(vv, probs.astype(BF16))
                for pair in range(GROUP // 2):
                    two = jnp.concatenate([ot[:, (2 * pair) * ATTN_BLOCK:(2 * pair + 1) * ATTN_BLOCK],
                                           ot[:, (2 * pair + 1) * ATTN_BLOCK:(2 * pair + 2) * ATTN_BLOCK]], axis=0)
                    col = (kh * GROUP + 2 * pair) * HEAD_DIM
                    o_ref[pl.ds(st, ATTN_BLOCK), col:col + 2 * HEAD_DIM] = two.T.astype(o_ref.dtype)
            return carry

        lax.fori_loop(0, nb, blk, 0, unroll=4)

    return _pcall(
        body, name, (nseq,),
        [pl.BlockSpec((seq, ATTN_WIDTH), lambda b: (b, 0)),
         pl.BlockSpec((seq, 2 * KV_WIDTH), lambda b: (b, 0)),
         pl.BlockSpec((8, GQ), lambda b: (0, 0))],
        pl.BlockSpec((seq, ATTN_WIDTH), lambda b: (b, 0)),
        jax.ShapeDtypeStruct((nseq * seq, ATTN_WIDTH), BF16),
        [pltpu.VMEM((ATTN_BLOCK + seq, 2 * KV_WIDTH), BF16)], (q, kv, sinks_t), ("parallel",), comm)


def attn_bwd(q, kv, sinks_t, do, nseq, seq, name, comm=None):
    nb = seq // ATTN_BLOCK

    def body(q_ref, kv_ref, s_ref, do_ref, dq_ref, dkv_ref, ds_ref, kvp, dkvp, dsacc):
        @pl.when(pl.program_id(0) == 0)
        def _():
            dsacc[...] = jnp.zeros(dsacc.shape, F32)

        kvp[0:ATTN_BLOCK, :] = jnp.zeros((ATTN_BLOCK, 2 * KV_WIDTH), BF16)
        kvp[ATTN_BLOCK:, :] = kv_ref[...]
        dkvp[...] = jnp.zeros(dkvp.shape, F32)

        def blk(n, carry):
            st = pl.multiple_of(n * ATTN_BLOCK, ATTN_BLOCK)
            qb = q_ref[pl.ds(st, ATTN_BLOCK), :]
            dob = do_ref[pl.ds(st, ATTN_BLOCK), :]
            kw = kvp[pl.ds(st, 2 * ATTN_BLOCK), :]
            mask = _attn_mask_t(n)
            for kh in range(N_KV_HEADS):
                kk = kw[:, kh * HEAD_DIM:(kh + 1) * HEAD_DIM]
                vv = kw[:, KV_WIDTH + kh * HEAD_DIM:KV_WIDTH + (kh + 1) * HEAD_DIM]
                qs = _stack_heads(qb, kh)
                dos = _stack_heads(dob, kh)
                probs, psink = _attn_probs_t(kk, qs, mask, s_ref[kh:kh + 1, :])
                dp = _dot_nt(vv, dos)
                dv = _dot(probs.astype(BF16), dos)
                rowdot = jnp.sum(probs * dp, axis=0, keepdims=True)
                dsc = (probs * (dp - rowdot) * (HEAD_DIM ** -0.5)).astype(BF16)
                dsacc[kh:kh + 1, :] += -psink * rowdot
                dk = _dot(dsc, qs)
                dqs = _dot_tn(dsc, kk)
                for g in range(GROUP):
                    col = (kh * GROUP + g) * HEAD_DIM
                    dq_ref[pl.ds(st, ATTN_BLOCK), col:col + HEAD_DIM] = (
                        dqs[g * ATTN_BLOCK:(g + 1) * ATTN_BLOCK].astype(dq_ref.dtype))
                dkvp[pl.ds(st, 2 * ATTN_BLOCK), kh * HEAD_DIM:(kh + 1) * HEAD_DIM] += dk
                dkvp[pl.ds(st, 2 * ATTN_BLOCK), KV_WIDTH + kh * HEAD_DIM:KV_WIDTH + (kh + 1) * HEAD_DIM] += dv
            return carry

        lax.fori_loop(0, nb, blk, 0, unroll=2)
        dkv_ref[...] = dkvp[ATTN_BLOCK:, :].astype(dkv_ref.dtype)

        @pl.when(pl.program_id(0) == nseq - 1)
        def _():
            for kh in range(N_KV_HEADS):
                for g in range(GROUP):
                    tot = jnp.sum(dsacc[kh:kh + 1, g * ATTN_BLOCK:(g + 1) * ATTN_BLOCK], axis=1, keepdims=True)
                    ds_ref[kh * GROUP + g:kh * GROUP + g + 1, :] = jnp.broadcast_to(tot, (1, LANES))

    seq_q = pl.BlockSpec((seq, ATTN_WIDTH), lambda b: (b, 0))
    seq_kv = pl.BlockSpec((seq, 2 * KV_WIDTH), lambda b: (b, 0))
    return _pcall(
        body, name, (nseq,),
        [seq_q, seq_kv, pl.BlockSpec((8, GQ), lambda b: (0, 0)), seq_q],
        [seq_q, seq_kv, pl.BlockSpec((N_Q_HEADS, LANES), lambda b: (0, 0))],
        [jax.ShapeDtypeStruct((nseq * seq, ATTN_WIDTH), BF16),
         jax.ShapeDtypeStruct((nseq * seq, 2 * KV_WIDTH), BF16),
         jax.ShapeDtypeStruct((N_Q_HEADS, LANES), F32)],
        [pltpu.VMEM((ATTN_BLOCK + seq, 2 * KV_WIDTH), BF16),
         pltpu.VMEM((ATTN_BLOCK + seq, 2 * KV_WIDTH), F32),
         pltpu.VMEM((8, GQ), F32)], (q, kv, sinks_t, do), ("arbitrary",), comm)


CONV_T = 128


SUBLANES = 8


def _shifted_rows(win):
    phases = [win] + [pltpu.roll(win, s, 0) for s in range(1, SUBLANES)]

    def shifted(s):
        lo = CONV_HALO - SUBLANES * (s // SUBLANES)
        return phases[s % SUBLANES][lo:lo + CONV_T]

    return shifted


def _conv_taps(win, w_ref, lanes, init):
    shifted = _shifted_rows(win)
    acc = init
    for j in range(CONV_KERNEL):
        acc = acc + w_ref[j:j + 1, lanes] * shifted(CONV_KERNEL - 1 - j)
    return acc


def _conv_block(h0p, w_ref, vec_ref, st):
    cols = []
    for cs in range(CONV_WIDTH // LANES):
        lanes = slice(cs * LANES, (cs + 1) * LANES)
        win = h0p[pl.ds(st, CONV_T + CONV_HALO), lanes]
        init = jnp.broadcast_to(vec_ref[0:1, lanes], (CONV_T, LANES))
        cols.append(_conv_taps(win, w_ref, lanes, init))
    return jnp.concatenate(cols, axis=-1)


def _glu_store(c_ref, h0p, st):
    cb = c_ref[pl.ds(st, CONV_T), :]
    h0p[pl.ds(pl.multiple_of(st + CONV_HALO, CONV_HALO), CONV_T), :] = cb[:, :CONV_WIDTH] * _sigmoid(cb[:, CONV_WIDTH:])


def conv_fwd(c, w, vec, nseq, seq, name, comm=None):
    nb = seq // CONV_T

    def body(c_ref, w_ref, vec_ref, o_ref, h1_ref, h0p):
        h0p[0:CONV_HALO, :] = jnp.zeros((CONV_HALO, CONV_WIDTH), F32)

        def blk(n, carry):
            st = pl.multiple_of(n * CONV_T, CONV_T)
            _glu_store(c_ref, h0p, st)
            h1 = _conv_block(h0p, w_ref, vec_ref, st)
            h1_ref[pl.ds(st, CONV_T), :] = h1
            mu = jnp.mean(h1, axis=-1, keepdims=True)
            xc = h1 - mu
            rstd = lax.rsqrt(jnp.mean(xc * xc, axis=-1, keepdims=True) + EPS)
            y = xc * rstd * vec_ref[1:2, :] + vec_ref[2:3, :]
            o_ref[pl.ds(st, CONV_T), :] = (y * _sigmoid(y)).astype(o_ref.dtype)
            return carry

        lax.fori_loop(0, nb, blk, 0)

    return _pcall(
        body, name, (nseq,),
        [pl.BlockSpec((seq, 2 * CONV_WIDTH), lambda b: (b, 0)),
         pl.BlockSpec((32, CONV_WIDTH), lambda b: (0, 0)),
         pl.BlockSpec((8, CONV_WIDTH), lambda b: (0, 0))],
        [pl.BlockSpec((seq, CONV_WIDTH), lambda b: (b, 0))] * 2,
        [jax.ShapeDtypeStruct((nseq * seq, CONV_WIDTH), BF16), jax.ShapeDtypeStruct((nseq * seq, CONV_WIDTH), F32)],
        [pltpu.VMEM((CONV_HALO + seq, CONV_WIDTH), F32)], (c, w, vec), ("parallel",), comm)


def conv_bwd(c, h1_saved, w, vec, dout, nseq, seq, name, comm=None):
    nb = seq // CONV_T

    def body(c_ref, h1_ref, w_ref, vec_ref, do_ref, dc_ref, dw_ref, dvec_ref, h0p, dh1p, dwacc):
        @pl.when(pl.program_id(0) == 0)
        def _():
            dwacc[...] = jnp.zeros(dwacc.shape, F32)
            dvec_ref[...] = jnp.zeros(dvec_ref.shape, F32)

        h0p[0:CONV_HALO, :] = jnp.zeros((CONV_HALO, CONV_WIDTH), F32)
        dh1p[seq:seq + CONV_HALO, :] = jnp.zeros((CONV_HALO, CONV_WIDTH), F32)

        def pass_a(n, carry):
            st = pl.multiple_of(n * CONV_T, CONV_T)
            _glu_store(c_ref, h0p, st)
            h1 = h1_ref[pl.ds(st, CONV_T), :]
            mu = jnp.mean(h1, axis=-1, keepdims=True)
            xc = h1 - mu
            rstd = lax.rsqrt(jnp.mean(xc * xc, axis=-1, keepdims=True) + EPS)
            xh = xc * rstd
            y = xh * vec_ref[1:2, :] + vec_ref[2:3, :]
            sg = _sigmoid(y)
            dy = do_ref[pl.ds(st, CONV_T), :] * (sg * (1.0 + y * (1.0 - sg)))
            dvec_ref[1:2, :] += jnp.sum(dy * xh, axis=0, keepdims=True)
            dvec_ref[2:3, :] += jnp.sum(dy, axis=0, keepdims=True)
            dxh = dy * vec_ref[1:2, :]
            dh1 = rstd * (dxh - jnp.mean(dxh, axis=-1, keepdims=True)
                          - xh * jnp.mean(dxh * xh, axis=-1, keepdims=True))
            dvec_ref[0:1, :] += jnp.sum(dh1, axis=0, keepdims=True)
            dh1p[pl.ds(st, CONV_T), :] = dh1
            return carry

        lax.fori_loop(0, nb, pass_a, 0)

        def pass_b(n, carry):
            st = pl.multiple_of(n * CONV_T, CONV_T)
            cols = []
            for cs in range(CONV_WIDTH // LANES):
                lanes = slice(cs * LANES, (cs + 1) * LANES)
                wind = dh1p[pl.ds(st, CONV_T + CONV_HALO), lanes]
                winh = h0p[pl.ds(st, CONV_T + CONV_HALO), lanes]
                d1 = wind[0:CONV_T]
                shifted_d, shifted_h = _shifted_rows(wind), _shifted_rows(winh)
                acc = jnp.zeros((CONV_T, LANES), F32)
                for j in range(CONV_KERNEL):
                    acc = acc + w_ref[j:j + 1, lanes] * shifted_d(2 + j)
                    prod = d1 * shifted_h(CONV_KERNEL - 1 - j)
                    part = prod[0:8]
                    for r in range(8, CONV_T, 8):
                        part = part + prod[r:r + 8]
                    dwacc[8 * j:8 * j + 8, lanes] += part
                cols.append(acc)
            dh0 = jnp.concatenate(cols, axis=-1)
            cb = c_ref[pl.ds(st, CONV_T), :]
            av, gt = cb[:, :CONV_WIDTH], cb[:, CONV_WIDTH:]
            sg = _sigmoid(gt)
            dc_ref[pl.ds(st, CONV_T), :] = jnp.concatenate(
                [dh0 * sg, dh0 * av * sg * (1.0 - sg)], axis=-1).astype(dc_ref.dtype)
            return carry

        lax.fori_loop(0, nb, pass_b, 0)

        @pl.when(pl.program_id(0) == nseq - 1)
        def _():
            dw_ref[...] = jnp.zeros(dw_ref.shape, F32)
            for j in range(CONV_KERNEL):
                dw_ref[j:j + 1, :] = jnp.sum(dwacc[8 * j:8 * j + 8, :], axis=0, keepdims=True)

    return _pcall(
        body, name, (nseq,),
        [pl.BlockSpec((seq, 2 * CONV_WIDTH), lambda b: (b, 0)),
         pl.BlockSpec((seq, CONV_WIDTH), lambda b: (b, 0)),
         pl.BlockSpec((32, CONV_WIDTH), lambda b: (0, 0)),
         pl.BlockSpec((8, CONV_WIDTH), lambda b: (0, 0)),
         pl.BlockSpec((seq, CONV_WIDTH), lambda b: (b, 0))],
        [pl.BlockSpec((seq, 2 * CONV_WIDTH), lambda b: (b, 0)),
         pl.BlockSpec((32, CONV_WIDTH), lambda b: (0, 0)),
         pl.BlockSpec((8, CONV_WIDTH), lambda b: (0, 0))],
        [jax.ShapeDtypeStruct((nseq * seq, 2 * CONV_WIDTH), BF16),
         jax.ShapeDtypeStruct((32, CONV_WIDTH), F32),
         jax.ShapeDtypeStruct((8, CONV_WIDTH), F32)],
        [pltpu.VMEM((CONV_HALO + seq, CONV_WIDTH), F32),
         pltpu.VMEM((seq + CONV_HALO, CONV_WIDTH), F32),
         pltpu.VMEM((8 * 32, CONV_WIDTH), F32)], (c, h1_saved, w, vec, dout), ("arbitrary",), comm)


POOL_T = 128


def _pooled_block(zpp, st, grp):
    lanes = slice(grp * LANES, (grp + 1) * LANES)
    win = zpp[pl.ds(st, POOL_T + POOL_HALO), lanes]
    acc = win
    for lvl in range(grp + 1):
        acc = acc + pltpu.roll(acc, 1 << lvl, 0)
    t = st + lax.broadcasted_iota(jnp.int32, (POOL_T, 1), 0)
    inv = 1.0 / jnp.minimum(t + 1, POOL_WINDOWS[grp]).astype(F32)
    return acc[POOL_HALO:] * inv - win[POOL_HALO:], inv


def pool_fwd(zp, wp, scale, nseq, seq, name):
    nb = seq // POOL_T

    def body(z_ref, wp_ref, sc_ref, o_ref, zpp):
        zpp[0:POOL_HALO, :] = jnp.zeros((POOL_HALO, POOL_WIDTH), F32)
        zpp[POOL_HALO:, :] = z_ref[...]

        def blk(n, carry):
            st = pl.multiple_of(n * POOL_T, POOL_T)
            for grp in range(len(POOL_WINDOWS)):
                lanes = slice(grp * LANES, (grp + 1) * LANES)
                pooled, _ = _pooled_block(zpp, st, grp)
                o_ref[pl.ds(st, POOL_T), lanes] = (
                    _dot(pooled.astype(BF16), wp_ref[grp]) * sc_ref[0:1, lanes]).astype(o_ref.dtype)
            return carry

        lax.fori_loop(0, nb, blk, 0)

    return pl.pallas_call(
        body, name=name, grid=(nseq,),
        in_specs=[pl.BlockSpec((seq, POOL_WIDTH), lambda b: (b, 0)),
                  pl.BlockSpec((4, LANES, LANES), lambda b: (0, 0, 0)),
                  pl.BlockSpec((1, POOL_WIDTH), lambda b: (0, 0))],
        out_specs=pl.BlockSpec((seq, POOL_WIDTH), lambda b: (b, 0)),
        out_shape=jax.ShapeDtypeStruct((nseq * seq, POOL_WIDTH), BF16),
        scratch_shapes=[pltpu.VMEM((POOL_HALO + seq, POOL_WIDTH), F32)],
        compiler_params=_params("parallel"),
    )(zp, wp, scale)


def pool_bwd(zp, wp, scale, dout, nseq, seq, name, comm=None):
    nb = seq // POOL_T

    def body(z_ref, wp_ref, sc_ref, do_ref, dz_ref, dwp_ref, dsc_ref, zpp, dpcp, negd):
        @pl.when(pl.program_id(0) == 0)
        def _():
            dwp_ref[...] = jnp.zeros(dwp_ref.shape, F32)
            dsc_ref[...] = jnp.zeros(dsc_ref.shape, F32)

        zpp[0:POOL_HALO, :] = jnp.zeros((POOL_HALO, POOL_WIDTH), F32)
        zpp[POOL_HALO:, :] = z_ref[...]
        dpcp[seq:seq + POOL_HALO, :] = jnp.zeros((POOL_HALO, POOL_WIDTH), F32)

        def pass_a(n, carry):
            st = pl.multiple_of(n * POOL_T, POOL_T)
            for grp in range(len(POOL_WINDOWS)):
                lanes = slice(grp * LANES, (grp + 1) * LANES)
                pooled, inv = _pooled_block(zpp, st, grp)
                pb = pooled.astype(BF16)
                dob = do_ref[pl.ds(st, POOL_T), lanes]
                dsc_ref[0:1, lanes] += jnp.sum(dob * _dot(pb, wp_ref[grp]), axis=0, keepdims=True)
                dpm = (dob * sc_ref[0:1, lanes]).astype(BF16)
                dwp_ref[grp] += _dot_tn(pb, dpm)
                dpooled = _dot_nt(dpm, wp_ref[grp])
                negd[pl.ds(st, POOL_T), lanes] = -dpooled
                dpcp[pl.ds(st, POOL_T), lanes] = dpooled * inv
            return carry

        lax.fori_loop(0, nb, pass_a, 0)

        def pass_b(n, carry):
            st = pl.multiple_of(n * POOL_T, POOL_T)
            rows = POOL_T + POOL_HALO
            for grp in range(len(POOL_WINDOWS)):
                lanes = slice(grp * LANES, (grp + 1) * LANES)
                acc = dpcp[pl.ds(st, rows), lanes]
                for lvl in range(grp + 1):
                    acc = acc + pltpu.roll(acc, rows - (1 << lvl), 0)
                dz_ref[pl.ds(st, POOL_T), lanes] = (acc[0:POOL_T] + negd[pl.ds(st, POOL_T), lanes]).astype(dz_ref.dtype)
            return carry

        lax.fori_loop(0, nb, pass_b, 0)

    seq_spec = pl.BlockSpec((seq, POOL_WIDTH), lambda b: (b, 0))
    return _pcall(
        body, name, (nseq,),
        [seq_spec, pl.BlockSpec((4, LANES, LANES), lambda b: (0, 0, 0)),
         pl.BlockSpec((1, POOL_WIDTH), lambda b: (0, 0)), seq_spec],
        [seq_spec, pl.BlockSpec((4, LANES, LANES), lambda b: (0, 0, 0)),
         pl.BlockSpec((8, POOL_WIDTH), lambda b: (0, 0))],
        [jax.ShapeDtypeStruct((nseq * seq, POOL_WIDTH), BF16),
         jax.ShapeDtypeStruct((4, LANES, LANES), F32),
         jax.ShapeDtypeStruct((8, POOL_WIDTH), F32)],
        [pltpu.VMEM((POOL_HALO + seq, POOL_WIDTH), F32),
         pltpu.VMEM((seq + POOL_HALO, POOL_WIDTH), F32),
         pltpu.VMEM((seq, POOL_WIDTH), F32)], (zp, wp, scale, dout), ("arbitrary",), comm)


GELU_C0 = 0.7978845608028654
GELU_C1 = 0.044715


def _gelu(xv):
    return xv * (0.5 * (1.0 + jnp.tanh(GELU_C0 * (xv + GELU_C1 * (xv * xv * xv)))))


def _gelu_grad(xv):
    t = jnp.tanh(GELU_C0 * (xv + GELU_C1 * (xv * xv * xv)))
    return 0.5 * (1.0 + t) + 0.5 * xv * (1.0 - t * t) * (GELU_C0 * (1.0 + 3.0 * GELU_C1 * xv * xv))


def _tril():
    ti = lax.broadcasted_iota(jnp.int32, (SGU_CHUNK, SGU_CHUNK), 0)
    si = lax.broadcasted_iota(jnp.int32, (SGU_CHUNK, SGU_CHUNK), 1)
    return si <= ti


def sgu_fwd(zs, ws, bst, ln, nseq, seq, name):
    nc = seq // SGU_CHUNK

    def body(z_ref, ws_ref, bs_ref, ln_ref, o_ref):
        tril = _tril()

        def blk(n, carry):
            st = pl.multiple_of(n * SGU_CHUNK, SGU_CHUNK)
            ge = _gelu(z_ref[pl.ds(st, SGU_CHUNK), :])
            uu, vv = ge[:, :SGU_WIDTH], ge[:, SGU_WIDTH:]
            mu = jnp.mean(vv, axis=-1, keepdims=True)
            xc = vv - mu
            rstd = lax.rsqrt(jnp.mean(xc * xc, axis=-1, keepdims=True) + EPS)
            vn = (xc * rstd * ln_ref[0:1, :] + ln_ref[1:2, :]).astype(BF16)
            for g in range(4):
                lanes = slice(g * LANES, (g + 1) * LANES)
                wm = jnp.where(tril, ws_ref[g], 0.0).astype(BF16)
                mixed = _dot(wm, vn[:, lanes]) + bs_ref[:, g:g + 1]
                o_ref[pl.ds(st, SGU_CHUNK), lanes] = (uu[:, lanes] * mixed).astype(o_ref.dtype)
            return carry

        lax.fori_loop(0, nc, blk, 0)

    return pl.pallas_call(
        body, name=name, grid=(nseq,),
        in_specs=[pl.BlockSpec((seq, 2 * SGU_WIDTH), lambda b: (b, 0)),
                  pl.BlockSpec((4, LANES, LANES), lambda b: (0, 0, 0)),
                  pl.BlockSpec((SGU_CHUNK, 4), lambda b: (0, 0)),
                  pl.BlockSpec((8, SGU_WIDTH), lambda b: (0, 0))],
        out_specs=pl.BlockSpec((seq, SGU_WIDTH), lambda b: (b, 0)),
        out_shape=jax.ShapeDtypeStruct((nseq * seq, SGU_WIDTH), BF16),
        compiler_params=_params("parallel"),
    )(zs, ws, bst, ln)


def sgu_bwd(zs, ws, bst, ln, dout, nseq, seq, name, comm=None):
    nc = seq // SGU_CHUNK

    def body(z_ref, ws_ref, bs_ref, ln_ref, do_ref, dz_ref, dws_ref, dbs_ref, dln_ref):
        @pl.when(pl.program_id(0) == 0)
        def _():
            dws_ref[...] = jnp.zeros(dws_ref.shape, F32)
            dbs_ref[...] = jnp.zeros(dbs_ref.shape, F32)
            dln_ref[...] = jnp.zeros(dln_ref.shape, F32)

        tril = _tril()

        def blk(n, carry):
            st = pl.multiple_of(n * SGU_CHUNK, SGU_CHUNK)
            zv = z_ref[pl.ds(st, SGU_CHUNK), :]
            ge = _gelu(zv)
            uu, vv = ge[:, :SGU_WIDTH], ge[:, SGU_WIDTH:]
            mu = jnp.mean(vv, axis=-1, keepdims=True)
            xc = vv - mu
            rstd = lax.rsqrt(jnp.mean(xc * xc, axis=-1, keepdims=True) + EPS)
            xh = xc * rstd
            vn = (xh * ln_ref[0:1, :] + ln_ref[1:2, :]).astype(BF16)
            dob = do_ref[pl.ds(st, SGU_CHUNK), :]
            du_cols, dvn_cols = [], []
            for g in range(4):
                lanes = slice(g * LANES, (g + 1) * LANES)
                wm = jnp.where(tril, ws_ref[g], 0.0).astype(BF16)
                mixed = _dot(wm, vn[:, lanes]) + bs_ref[:, g:g + 1]
                du_cols.append(dob[:, lanes] * mixed)
                dmix = dob[:, lanes] * uu[:, lanes]
                dbs_ref[g] += jnp.broadcast_to(jnp.sum(dmix, axis=-1, keepdims=True), (SGU_CHUNK, LANES))
                dmb = dmix.astype(BF16)
                dws_ref[g] += jnp.where(tril, _dot_nt(dmb, vn[:, lanes]), 0.0)
                dvn_cols.append(_dot_tn(wm, dmb))
            dvn = jnp.concatenate(dvn_cols, axis=-1)
            dln_ref[0:1, :] += jnp.sum(dvn * xh, axis=0, keepdims=True)
            dln_ref[1:2, :] += jnp.sum(dvn, axis=0, keepdims=True)
            dxh = dvn * ln_ref[0:1, :]
            dv = rstd * (dxh - jnp.mean(dxh, axis=-1, keepdims=True)
                         - xh * jnp.mean(dxh * xh, axis=-1, keepdims=True))
            dge = jnp.concatenate(du_cols + [dv], axis=-1)
            dz_ref[pl.ds(st, SGU_CHUNK), :] = (dge * _gelu_grad(zv)).astype(dz_ref.dtype)
            return carry

        lax.fori_loop(0, nc, blk, 0)

    w_spec = pl.BlockSpec((4, LANES, LANES), lambda b: (0, 0, 0))
    ln_spec = pl.BlockSpec((8, SGU_WIDTH), lambda b: (0, 0))
    return _pcall(
        body, name, (nseq,),
        [pl.BlockSpec((seq, 2 * SGU_WIDTH), lambda b: (b, 0)), w_spec,
         pl.BlockSpec((SGU_CHUNK, 4), lambda b: (0, 0)), ln_spec,
         pl.BlockSpec((seq, SGU_WIDTH), lambda b: (b, 0))],
        [pl.BlockSpec((seq, 2 * SGU_WIDTH), lambda b: (b, 0)), w_spec, w_spec, ln_spec],
        [jax.ShapeDtypeStruct((nseq * seq, 2 * SGU_WIDTH), BF16),
         jax.ShapeDtypeStruct((4, LANES, LANES), F32),
         jax.ShapeDtypeStruct((4, LANES, LANES), F32),
         jax.ShapeDtypeStruct((8, SGU_WIDTH), F32)],
        [], (zs, ws, bst, ln, dout), ("arbitrary",), comm)


def _ew_rows(rows, cols, nbuf):
    t = _row_tile(rows, 1024)
    while t > 8 and t * cols * 4 * nbuf * 2 > 24 * 2**20:
        t //= 2
    return t


def adamw(w, g, m, v, name):
    layers, rows, cols = w.shape
    tr = _ew_rows(rows, cols, 7)

    def body(w_ref, g_ref, m_ref, v_ref, d_ref, mo_ref, vo_ref):
        gv = g_ref[...]
        mn = ADAM_B1 * m_ref[...] + (1.0 - ADAM_B1) * gv
        vn = ADAM_B2 * v_ref[...] + (1.0 - ADAM_B2) * (gv * gv)
        m_hat = mn / (1.0 - ADAM_B1 ** ADAM_STEP)
        v_hat = vn / (1.0 - ADAM_B2 ** ADAM_STEP)
        d_ref[...] = -ADAM_LR * (m_hat / (jnp.sqrt(v_hat) + ADAM_EPS) + ADAM_WD * w_ref[...])
        mo_ref[...] = mn
        vo_ref[...] = vn

    spec = pl.BlockSpec((1, tr, cols), lambda l, i: (l, i, 0))
    return pl.pallas_call(
        body, name=name, grid=(layers, rows // tr),
        in_specs=[spec] * 4, out_specs=[spec] * 3,
        out_shape=[jax.ShapeDtypeStruct(w.shape, F32)] * 3,
        compiler_params=_params("parallel", "parallel"),
    )(w, g, m, v)


def adamw_many(ws, gs, ms, vs, name):
    n = len(ws)

    def body(*refs):
        w_refs, g_refs, m_refs, v_refs = refs[:n], refs[n:2 * n], refs[2 * n:3 * n], refs[3 * n:4 * n]
        d_refs, mo_refs, vo_refs = refs[4 * n:5 * n], refs[5 * n:6 * n], refs[6 * n:7 * n]
        for i in range(n):
            gv = g_refs[i][...]
            mn = ADAM_B1 * m_refs[i][...] + (1.0 - ADAM_B1) * gv
            vn = ADAM_B2 * v_refs[i][...] + (1.0 - ADAM_B2) * (gv * gv)
            m_hat = mn / (1.0 - ADAM_B1 ** ADAM_STEP)
            v_hat = vn / (1.0 - ADAM_B2 ** ADAM_STEP)
            d_refs[i][...] = -ADAM_LR * (m_hat / (jnp.sqrt(v_hat) + ADAM_EPS) + ADAM_WD * w_refs[i][...])
            mo_refs[i][...] = mn
            vo_refs[i][...] = vn

    vmem = pl.BlockSpec(memory_space=pltpu.VMEM)
    shapes = [jax.ShapeDtypeStruct(w.shape, F32) for w in ws]
    res = pl.pallas_call(
        body, name=name, in_specs=[vmem] * (4 * n), out_specs=[vmem] * (3 * n), out_shape=shapes * 3,
        compiler_params=pltpu.CompilerParams(vmem_limit_bytes=VMEM_LIMIT),
    )(*ws, *gs, *ms, *vs)
    return res[:n], res[n:2 * n], res[2 * n:]


def add_cast(a, b, name, dtype=BF16):
    nslab, rows, cols = a.shape
    tr = _ew_rows(rows, cols, 3)

    def body(a_ref, b_ref, o_ref):
        o_ref[...] = (a_ref[...] + b_ref[...]).astype(dtype)

    spec = pl.BlockSpec((1, tr, cols), lambda k, i: (k, i, 0))
    return pl.pallas_call(
        body, name=name, grid=(nslab, rows // tr),
        in_specs=[spec, spec], out_specs=spec,
        out_shape=jax.ShapeDtypeStruct(a.shape, dtype),
        compiler_params=_params("parallel", "parallel"),
    )(a, b)


def pair_sum(t, got, core, name):
    nslab, h, cols = got.shape
    tr = _ew_rows(h, cols, 3)
    nb = h // tr

    def body(c_ref, a_ref, b_ref, o_ref):
        o_ref[...] = (a_ref[...] + b_ref[...]).astype(BF16)

    spec = pl.BlockSpec((1, tr, cols), lambda k, i, c: (k, i, 0))
    return pl.pallas_call(
        body, name=name,
        grid_spec=pltpu.PrefetchScalarGridSpec(
            num_scalar_prefetch=1, grid=(nslab, nb),
            in_specs=[pl.BlockSpec((1, tr, cols), lambda k, i, c: (k, c[0] * nb + i, 0)), spec],
            out_specs=spec),
        out_shape=jax.ShapeDtypeStruct(got.shape, BF16),
        compiler_params=_params("parallel", "parallel"),
    )(core, t, got)


def sum_parts(parts, name, first=None):
    npart, rows, cols = parts.shape
    tr = _ew_rows(rows, cols, npart + 2)

    def body(*refs):
        p_ref, o_ref = refs[-2], refs[-1]
        acc = p_ref[0].astype(F32) if first is None else refs[0][...].astype(F32) + p_ref[0].astype(F32)
        for j in range(1, npart):
            acc = acc + p_ref[j].astype(F32)
        o_ref[...] = acc

    row = pl.BlockSpec((tr, cols), lambda i: (i, 0))
    ins = [parts] if first is None else [first, parts]
    return pl.pallas_call(
        body, name=name, grid=(rows // tr,),
        in_specs=([] if first is None else [row]) + [pl.BlockSpec((npart, tr, cols), lambda i: (0, i, 0))],
        out_specs=row,
        out_shape=jax.ShapeDtypeStruct((rows, cols), F32),
        compiler_params=_params("parallel"),
    )(*ins)


ANY = pl.BlockSpec(memory_space=pl.ANY)
MESH = pl.DeviceIdType.MESH


def _me():
    return lax.axis_index("x"), lax.axis_index("y"), lax.axis_index("c")


def _flip(pos, rel):
    return tuple(1 - p if f else p for p, f in zip(pos, rel))


SIBLING = (0, 0, 1)
OTHER_CHIPS = ((1, 0, 0), (0, 1, 0), (1, 1, 0))


def _chip_of(pos, rel=(0, 0, 0)):
    px, py, _ = _flip(pos, rel)
    return 2 * px + py


def allgather_blocks(shards, name):
    nt = len(shards)
    hs = [s.shape[0] // 2 for s in shards]

    def body(*refs):
        ins, outs = refs[:nt], refs[nt:2 * nt]
        send_sems, recv_sems, loc_sems = refs[2 * nt:]
        pos = _me()
        x, y, c = pos

        def block_id(rel):
            px, py, pc = _flip(pos, rel)
            return 4 * px + 2 * py + pc

        def copy(t, k, block_rel, to_rel, src=None):
            dst = outs[t].at[block_id(block_rel)]
            return pltpu.make_async_remote_copy(
                src_ref=dst if src is None else src, dst_ref=dst,
                send_sem=send_sems.at[t * 7 + k], recv_sem=recv_sems.at[t * 7 + k],
                device_id=_flip(pos, to_rel), device_id_type=MESH)

        own = [ins[t].at[pl.ds(c * hs[t], hs[t])] for t in range(nt)]
        mine = [pltpu.make_async_copy(own[t], outs[t].at[block_id((0, 0, 0))], loc_sems.at[t]) for t in range(nt)]
        for cp in mine:
            cp.start()
        first = []
        for t in range(nt):
            first.append(copy(t, 0, (0, 0, 0), SIBLING, src=own[t]))
            first += [copy(t, 1 + j, (0, 0, 0), rel, src=own[t]) for j, rel in enumerate(OTHER_CHIPS)]
        for cp in first:
            cp.start()
        passed = []
        for j, rel in enumerate(OTHER_CHIPS):
            for t in range(nt):
                copy(t, 1 + j, rel, (0, 0, 0)).wait_recv()
                fwd = copy(t, 4 + j, rel, SIBLING)
                fwd.start()
                passed.append(fwd)
        for t in range(nt):
            copy(t, 0, SIBLING, (0, 0, 0)).wait_recv()
            for j, rel in enumerate(OTHER_CHIPS):
                copy(t, 4 + j, (rel[0], rel[1], 1), (0, 0, 0)).wait_recv()
        for cp in first + passed:
            cp.wait_send()
        for cp in mine:
            cp.wait()

    return pl.pallas_call(
        body, name=name,
        in_specs=[ANY] * nt, out_specs=[ANY] * nt,
        out_shape=[jax.ShapeDtypeStruct((N_DEV, h, s.shape[1]), s.dtype) for h, s in zip(hs, shards)],
        scratch_shapes=[pltpu.SemaphoreType.DMA((7 * nt,)), pltpu.SemaphoreType.DMA((7 * nt,)),
                        pltpu.SemaphoreType.DMA((nt,))],
    )(*shards)


def _block_id(pos, rel=(0, 0, 0)):
    px, py, pc = _flip(pos, rel)
    return 4 * px + 2 * py + pc


def gather_first_hop(shards):
    hs = [s.shape[0] // 2 for s in shards]

    def plan(ins, outs, pos):
        me = _block_id(pos)
        remote = []
        for i, o, h in zip(ins, outs, hs):
            own = i.at[pl.ds(pos[2] * h, h)]
            remote += [(rel, own, o.at[me]) for rel in (SIBLING,) + OTHER_CHIPS]
        return remote

    return Comm(shards, [((N_DEV, h, s.shape[1]), s.dtype) for h, s in zip(hs, shards)], plan, 4 * len(shards))


def gather_second_hop(gathered):
    def plan(ins, outs, pos):
        remote = []
        for i, o in zip(ins, outs):
            for rel in OTHER_CHIPS:
                blk = _block_id(pos, rel)
                remote.append((SIBLING, i.at[blk], o.at[blk]))
        return remote

    return Comm(gathered, [(g.shape, g.dtype) for g in gathered], plan, 3 * len(gathered),
                aliases={i: i for i in range(len(gathered))})


def swap_comm(xs):
    def plan(ins, outs, pos):
        return [(SIBLING, i, o) for i, o in zip(ins, outs)]

    return Comm(list(xs), [(v.shape, v.dtype) for v in xs], plan, len(xs))


def give_half_comm(ts, plain=()):
    nt = len(ts)

    def plan(ins, outs, pos):
        remote = []
        for i, o in zip(ins[:nt], outs[:nt]):
            h = o.shape[1]
            remote.append((SIBLING, i.at[:, pl.ds((1 - pos[2]) * h, h)], o))
        return remote + [(SIBLING, i, o) for i, o in zip(ins[nt:], outs[nt:])]

    shapes = [((t.shape[0], t.shape[1] // 2, t.shape[2]), t.dtype) for t in ts] + [(v.shape, v.dtype) for v in plain]
    return Comm(list(ts) + list(plain), shapes, plan, nt + len(plain))


def chip_scatter_comm(xs, shared=None):
    nx = len(xs)

    def plan(ins, outs, pos):
        me = _chip_of(pos)
        remote = []
        for i, o in zip(ins[:nx], outs[:nx]):
            remote += [(rel, i.at[_chip_of(pos, rel)], o.at[j]) for j, rel in enumerate(OTHER_CHIPS)]
        if shared is not None:
            remote += [(rel, ins[nx], outs[nx].at[me]) for rel in OTHER_CHIPS]
        return remote

    shapes = [((3,) + v.shape[1:], v.dtype) for v in xs]
    if shared is not None:
        shapes.append(((N_CHIPS,) + shared.shape, shared.dtype))
    return Comm(list(xs) + ([] if shared is None else [shared]), shapes, plan, 3 * nx + (0 if shared is None else 3))


def tail_reduce(t, small, name):
    nslab, h2, cols = t.shape
    h = h2 // 2
    rows = small.shape[0]

    def body(t_ref, small_ref, mine_ref, theirs_ref, ssum_ref,
             got_pair, sums, got_chips, small_got, small_pair, small_chips, send_sems, recv_sems):
        pos = _me()
        core = pos[2]
        me = _chip_of(pos)

        def copy(i, rel, src, dst):
            return pltpu.make_async_remote_copy(src_ref=src, dst_ref=dst, send_sem=send_sems.at[i],
                                                recv_sem=recv_sems.at[i], device_id=_flip(pos, rel),
                                                device_id_type=MESH)

        pair = [copy(0, SIBLING, t_ref.at[:, pl.ds(pl.multiple_of((1 - core) * h, SUBLANES), h)], got_pair),
                copy(1, SIBLING, small_ref, small_got)]
        for cp in pair:
            cp.start()
        for cp in pair:
            cp.wait()
        for k in range(nslab):
            sums[k] = (t_ref[k, pl.ds(pl.multiple_of(core * h, SUBLANES), h), :] + got_pair[k]).astype(BF16)
        small_pair[...] = small_ref[...] + small_got[...]

        chips = []
        for j, rel in enumerate(OTHER_CHIPS):
            chips.append(copy(2 + j, rel, sums.at[_chip_of(pos, rel)], got_chips.at[j]))
            chips.append(copy(5 + j, rel, small_pair, small_chips.at[me]))
        for cp in chips:
            cp.start()
        small_chips[me] = small_pair[...]
        for cp in chips:
            cp.wait()
        acc = sums[me].astype(F32)
        for j in range(len(OTHER_CHIPS)):
            acc = acc + got_chips[j].astype(F32)
        mine_ref[...] = acc
        tot = small_chips[0]
        for k in range(1, N_CHIPS):
            tot = tot + small_chips[k]
        ssum_ref[...] = tot

        join = copy(8, SIBLING, mine_ref, theirs_ref)
        join.start()
        join.wait()

    vmem = pl.BlockSpec(memory_space=pltpu.VMEM)
    return pl.pallas_call(
        body, name=name, in_specs=[vmem, vmem], out_specs=[vmem, vmem, vmem],
        out_shape=[jax.ShapeDtypeStruct((h, cols), F32), jax.ShapeDtypeStruct((h, cols), F32),
                   jax.ShapeDtypeStruct((rows, LANES), F32)],
        scratch_shapes=[pltpu.VMEM((nslab, h, cols), F32), pltpu.VMEM((nslab, h, cols), BF16),
                        pltpu.VMEM((3, h, cols), BF16), pltpu.VMEM((rows, LANES), F32),
                        pltpu.VMEM((rows, LANES), F32), pltpu.VMEM((N_CHIPS, rows, LANES), F32),
                        pltpu.SemaphoreType.DMA((9,)), pltpu.SemaphoreType.DMA((9,))],
        compiler_params=pltpu.CompilerParams(vmem_limit_bytes=VMEM_LIMIT),
    )(t, small)


PACK_ROWS = 256


def _pack(arrs):
    parts, layout = [], []
    row = 0
    for a in arrs:
        flat = a.reshape(-1).astype(F32)
        size = flat.shape[0]
        rows = -(-size // (8 * LANES)) * 8
        flat = jnp.pad(flat, (0, rows * LANES - size))
        parts.append(flat.reshape(rows, LANES))
        layout.append((row, rows, size, a.shape))
        row += rows
    if row % PACK_ROWS:
        parts.append(jnp.zeros((PACK_ROWS - row % PACK_ROWS, LANES), F32))
    return jnp.concatenate(parts, axis=0), layout


def _unpack(packed, layout):
    return [packed[r0:r0 + rows].reshape(-1)[:size].reshape(shape) for r0, rows, size, shape in layout]


SMALL_REPL = ['mix_norm', 'a_b_in', 'a_sinks', 'a_conv_b', 'a_cln_g', 'a_cln_b', 'c_w_pool', 'c_w_s', 'c_b_s',
              'ffn_norm', 'final_norm']
SMALL_SHARD = ['a_conv_w', 'c_pool_scale', 'c_sln_g', 'c_sln_b']
BIG = ['a_w_in', 'a_w_out', 'c_w_in', 'c_w_out', 'ffn_w_gate', 'ffn_w_up', 'ffn_w_down']
TRANSPOSED = ('a_w_in', 'ffn_w_gate', 'ffn_w_up')
BIG_COL_SHARDED = {'c_w_in'}


def _full_weight(name, g8):
    _, h, cols = g8.shape
    g4 = g8.reshape(N_CHIPS, 2 * h, cols)
    if name not in BIG_COL_SHARDED:
        return g4.reshape(-1, cols)
    return jnp.transpose(g4, (1, 0, 2)).reshape(2 * h, N_CHIPS * cols)


def _to_shard_major(name, f):
    if name not in BIG_COL_SHARDED:
        return f.reshape(N_CHIPS, f.shape[0] // N_CHIPS, f.shape[1])
    r, cfull = f.shape
    return jnp.transpose(f.reshape(r, N_CHIPS, cfull // N_CHIPS), (1, 0, 2))


def kernel(*args):
    a = dict(zip(IN_NAMES, args))
    bl, seq, _ = a['x'].shape
    n = bl * seq
    x = a['x'].reshape(n, D_MODEL)
    target = a['loss_target'].reshape(n, D_MODEL)
    xi, yi, ci = _me()
    chip = 2 * xi + yi

    shard = {'a_w_in': a['a_w_in'][0].T, 'a_w_out': a['a_w_out'][0], 'c_w_in': a['c_w_in'][0], 'c_w_out': a['c_w_out'][0]}
    for layer in range(2):
        shard['gate' + str(layer)] = a['ffn_w_gate'][layer].T
        shard['up' + str(layer)] = a['ffn_w_up'][layer].T
        shard['down' + str(layer)] = a['ffn_w_down'][layer]
    shard = {k: v.astype(BF16) for k, v in shard.items()}
    core = ci.astype(jnp.int32).reshape(1)
    block_id = 4 * xi + 2 * yi + ci

    def first_hop(*names):
        return gather_first_hop([shard[k] for k in names])

    def finish(name, g8):
        h = shard[name].shape[0] // 2
        own = lax.dynamic_slice_in_dim(shard[name], ci * h, h, axis=0)
        return _full_weight(name, lax.dynamic_update_slice_in_dim(g8, own[None], block_id, axis=0))

    a_w_in_t = _full_weight('a_w_in', allgather_blocks([shard['a_w_in']], "gather_a_w_in")[0])
    in0_width = a_w_in_t.shape[0]
    small_shard_pack, small_shard_layout = _pack([a[k] for k in SMALL_SHARD])
    hop_a = first_hop('a_w_out', 'c_w_out')
    hop_s = chip_scatter_comm([], shared=small_shard_pack)
    mix_norm, ffn_norm = a['mix_norm'], a['ffn_norm']
    (hn0, q, kv, cc), outs = norm_inproj(
        x, mix_norm[0:1], a_w_in_t, a['a_b_in'],
        [(0, ATTN_WIDTH), (ATTN_WIDTH, ATTN_WIDTH + 2 * KV_WIDTH), (ATTN_WIDTH + 2 * KV_WIDTH, in0_width)],
        [BF16, BF16, F32], "in_proj0", comm=hop_a + hop_s, w_transposed=True)
    got_a, (ss,) = hop_a.split(outs, hop_s)
    ss = lax.dynamic_update_slice_in_dim(ss, small_shard_pack[None], chip, axis=0)
    ss_full = []
    for r0, rows, size, shape in small_shard_layout:
        per_chip = ss[:, r0:r0 + rows].reshape(N_CHIPS, -1)[:, :size].reshape((N_CHIPS,) + shape)
        ss_full.append(jnp.concatenate([per_chip[k] for k in range(N_CHIPS)], axis=-1))
    a_conv_w, c_pool_scale, c_sln_g, c_sln_b = [v[0] for v in ss_full]

    conv_taps = jnp.pad(a_conv_w, ((0, 32 - CONV_KERNEL), (0, 0)))
    conv_vec = jnp.pad(jnp.stack([a['a_conv_b'][0], a['a_cln_g'][0], a['a_cln_b'][0]]), ((0, 5), (0, 0)))
    sinks_b = jnp.pad(jnp.repeat(a['a_sinks'][0].reshape(N_KV_HEADS, GROUP), ATTN_BLOCK, axis=1), ((0, 6), (0, 0)))
    w_pool_bf = a['c_w_pool'][0].astype(BF16)
    pool_scale = c_pool_scale.reshape(1, POOL_WIDTH)
    w_s = a['c_w_s'][0]
    b_s_t = a['c_b_s'][0].T
    sgu_ln = jnp.pad(jnp.stack([c_sln_g, c_sln_b]), ((0, 6), (0, 0)))
    final_norm = a['final_norm'].reshape(1, D_MODEL)

    hop_b, pass_a = first_hop('gate0', 'c_w_in'), gather_second_hop(got_a)
    attn, outs = attn_fwd(q, kv, sinks_b, bl, seq, "attn_fwd", comm=hop_b + pass_a)
    got_b, done = hop_b.split(outs, pass_a)
    a_w_out, c_w_out = finish('a_w_out', done[0]), finish('c_w_out', done[1])

    hop_c, pass_b = first_hop('up0', 'down0'), gather_second_hop(got_b)
    (conv, conv_h1), outs = conv_fwd(cc, conv_taps, conv_vec, bl, seq, "conv_fwd", comm=hop_c + pass_b)
    got_c, done = hop_c.split(outs, pass_b)
    wg0, c_w_in = finish('gate0', done[0]), finish('c_w_in', done[1])

    h1, done = out_proj(x, attn, conv, a_w_out, "out_proj0", comm=gather_second_hop(got_c))
    wu0, wd0 = finish('up0', done[0]), finish('down0', done[1])

    (hnf0, g0, u0), got_e = ffn_gate_up(h1, ffn_norm[0:1], wg0, wu0, "ffn_gate_up0",
                                        comm=first_hop('gate1', 'up1', 'down1'))

    h2, done = ffn_down(h1, g0, u0, wd0, "ffn_down0", comm=gather_second_hop(got_e))
    wg1, wu1, wd1 = finish('gate1', done[0]), finish('up1', done[1]), finish('down1', done[2])
    wg, wu, wd = [wg0, wg1], [wu0, wu1], [wd0, wd1]

    (hn1, zp, zs), _ = norm_inproj(
        h2, mix_norm[1:2], c_w_in, jnp.zeros((1, c_w_in.shape[1]), F32),
        [(0, POOL_WIDTH), (POOL_WIDTH, c_w_in.shape[1])], [F32, F32], "in_proj1")
    pool = pool_fwd(zp, w_pool_bf, pool_scale, bl, seq, "pool_fwd")
    sgu = sgu_fwd(zs, w_s, b_s_t, sgu_ln, bl, seq, "sgu_fwd")
    h3, _ = out_proj(h2, pool, sgu, c_w_out, "out_proj1")
    (hnf1, g1, u1), _ = ffn_gate_up(h3, ffn_norm[1:2], wg1, wu1, "ffn_gate_up1")
    h4, _ = ffn_down(h3, g1, u1, wd1, "ffn_down1")

    dh4, d_final_norm, loss_local = loss_head(h4, final_norm, target, "loss_head")

    grads = {}
    pieces = {}

    def slabs_of(names, fulls):
        return [_to_shard_major(k, fulls[k]) for k in names]

    def pair_sums_of(names, slabs, gots):
        return [pair_sum(t, gt, core, "pair_sum_" + k) for k, t, gt in zip(names, slabs, gots)]

    def chip_sums_of(names, sums, from_chips):
        own = [lax.dynamic_index_in_dim(p, chip, axis=0, keepdims=False) for p in sums]
        return [sum_parts(p, "chip_sum_" + k, first=o) for k, p, o in zip(names, from_chips, own)]

    (dg, du, act), _ = ffn_down_bwd(dh4, g1, u1, wd[1], "ffn_down_bwd1")
    full1 = {'down1': mm_tn(act, dh4, "dw_down1"), 'gate1': mm_tn(dg, hnf1, "dw_gate1"),
             'up1': mm_tn(du, hnf1, "dw_up1")}
    names1 = ['gate1', 'up1', 'down1']
    slabs1 = slabs_of(names1, full1)
    dh3, d_ffn_norm1, got1 = proj_rms_bwd([dg, du], [wg[1], wu[1]], h3, ffn_norm[1:2], dh4, 1, "ffn_up_bwd1",
                                          tm_pref=512, w_transposed=True, comm=give_half_comm(slabs1))
    sums1 = pair_sums_of(names1, slabs1, got1)
    d_pool, d_sgu = out_proj_bwd(dh3, c_w_out, [F32, F32], "out_proj_bwd1")
    full1['c_w_out'] = jnp.concatenate([mm_tn(pool, dh3, "dw_out1_pool"), mm_tn(sgu, dh3, "dw_out1_sgu")], axis=0)
    (dzp, d_w_pool, d_pool_scale), from_gate = pool_bwd(zp, w_pool_bf, pool_scale, d_pool, bl, seq, "pool_bwd",
                                                        comm=chip_scatter_comm(sums1[0:1]))
    (dzs, d_w_s, d_b_s_b, d_sgu_ln), from_up = sgu_bwd(zs, w_s, b_s_t, sgu_ln, d_sgu, bl, seq, "sgu_bwd",
                                                       comm=chip_scatter_comm(sums1[1:2]))
    full1['c_w_in'] = jnp.concatenate([mm_tn(hn1, dzp, "dw_in1_pool"), mm_tn(hn1, dzs, "dw_in1_sgu")], axis=1)
    names1b = ['c_w_out', 'c_w_in']
    slabs1b = slabs_of(names1b, full1)
    heavy_pack, heavy_layout = _pack([d_w_pool[None], d_w_s[None]])
    chips_down, pair1b = chip_scatter_comm(sums1[2:3]), give_half_comm(slabs1b, plain=[heavy_pack])
    dh2, d_mix_norm1, outs = proj_rms_bwd([dzp, dzs], [c_w_in[:, :POOL_WIDTH], c_w_in[:, POOL_WIDTH:]], h2,
                                          mix_norm[1:2], dh3, 1, "in_proj_bwd1", comm=chips_down + pair1b)
    from_down, got1b = chips_down.split(outs, pair1b)
    mine1 = chip_sums_of(names1, sums1, from_gate + from_up + from_down)
    sums1b = pair_sums_of(names1b, slabs1b, got1b[:2])
    heavy_pair = add_cast(heavy_pack[None], got1b[2][None], "pair_sum_heavy", dtype=F32)[0]

    join1, chips1b = swap_comm(mine1), chip_scatter_comm(sums1b, shared=heavy_pair)
    (dg, du, act), outs = ffn_down_bwd(dh2, g0, u0, wd[0], "ffn_down_bwd0", comm=join1 + chips1b)
    theirs1, from_chips1b = join1.split(outs, chips1b)
    pieces.update({k: (m, t) for k, m, t in zip(names1, mine1, theirs1)})
    mine1b = chip_sums_of(names1b, sums1b, from_chips1b[:2])
    heavy_chips = lax.dynamic_update_slice_in_dim(from_chips1b[2], heavy_pair[None], chip, axis=0)
    grads['c_w_pool'], grads['c_w_s'] = _unpack(sum_parts(heavy_chips, "heavy_sum"), heavy_layout)
    full0 = {'down0': mm_tn(act, dh2, "dw_down0"), 'gate0': mm_tn(dg, hnf0, "dw_gate0"),
             'up0': mm_tn(du, hnf0, "dw_up0")}
    names0 = ['gate0', 'up0', 'down0']
    slabs0 = slabs_of(names0, full0)
    join1b, pair0 = swap_comm(mine1b), give_half_comm(slabs0)
    dh1, d_ffn_norm0, outs = proj_rms_bwd([dg, du], [wg[0], wu[0]], h1, ffn_norm[0:1], dh2, 1, "ffn_up_bwd0",
                                          tm_pref=512, comm=join1b + pair0, w_transposed=True)
    theirs1b, got0 = join1b.split(outs, pair0)
    pieces.update({k: (m, t) for k, m, t in zip(names1b, mine1b, theirs1b)})
    sums0 = pair_sums_of(names0, slabs0, got0)

    d_attn, d_conv = out_proj_bwd(dh1, a_w_out, [BF16, F32], "out_proj_bwd0")
    full_o = {'a_w_out': jnp.concatenate([mm_tn(attn, dh1, "dw_out0_attn"), mm_tn(conv, dh1, "dw_out0_conv")], axis=0)}
    slabs_o = slabs_of(['a_w_out'], full_o)
    chips0, pair_o = chip_scatter_comm(sums0), give_half_comm(slabs_o)
    (dq, dkv, d_sinks_b), outs = attn_bwd(q, kv, sinks_b, d_attn, bl, seq, "attn_bwd", comm=chips0 + pair_o)
    from_chips0, got_o = chips0.split(outs, pair_o)
    mine0 = chip_sums_of(names0, sums0, from_chips0)
    sums_o = pair_sums_of(['a_w_out'], slabs_o, got_o)
    join0, chips_o = swap_comm(mine0), chip_scatter_comm(sums_o)
    (dcc, d_conv_taps, d_conv_vec), outs = conv_bwd(cc, conv_h1, conv_taps, conv_vec, d_conv, bl, seq, "conv_bwd",
                                                    comm=join0 + chips_o)
    theirs0, from_chips_o = join0.split(outs, chips_o)
    pieces.update({k: (m, t) for k, m, t in zip(names0, mine0, theirs0)})
    mine_o = chip_sums_of(['a_w_out'], sums_o, from_chips_o)
    kq, kk = ATTN_WIDTH, ATTN_WIDTH + 2 * KV_WIDTH
    grad_x, d_mix_norm0, _ = proj_rms_bwd([dq, dkv, dcc], [a_w_in_t[:kq], a_w_in_t[kq:kk], a_w_in_t[kk:]], x,
                                          mix_norm[0:1], dh1, 1, "in_proj_bwd0", w_transposed=True)
    dw_q, db_q = mm_tn(dq, hn0, "dw_in0_q", xsum=True)
    dw_kv, db_kv = mm_tn(dkv, hn0, "dw_in0_kv", xsum=True)
    (dw_c, db_c), theirs_o = mm_tn(dcc, hn0, "dw_in0_c", xsum=True, comm=swap_comm(mine_o))
    pieces['a_w_out'] = (mine_o[0], theirs_o[0])
    d_a_b_in = jnp.concatenate([db_q, db_kv, db_c], axis=0)
    slabs_i = slabs_of(['a_w_in'], {'a_w_in': jnp.concatenate([dw_q, dw_kv, dw_c], axis=0)})

    small_full = {
        'mix_norm': jnp.stack([d_mix_norm0, d_mix_norm1]), 'a_b_in': d_a_b_in[None], 'a_sinks': d_sinks_b[:, 0][None],
        'a_conv_w': d_conv_taps[:CONV_KERNEL][None], 'a_conv_b': d_conv_vec[0][None], 'a_cln_g': d_conv_vec[1][None],
        'a_cln_b': d_conv_vec[2][None], 'c_pool_scale': d_pool_scale[0][None],
        'c_sln_g': d_sgu_ln[0][None], 'c_sln_b': d_sgu_ln[1][None],
        'c_b_s': d_b_s_b[:, :, 0][None], 'ffn_norm': jnp.stack([d_ffn_norm0, d_ffn_norm1]),
        'final_norm': d_final_norm, 'loss': loss_local.reshape(1)}
    small_names = SMALL_REPL + SMALL_SHARD
    tail_names = [k for k in small_names if k in small_full] + ['loss']
    small_pack, small_layout = _pack([small_full[k] for k in tail_names])

    mine_i, theirs_i, small_sum = tail_reduce(slabs_i[0], small_pack, "tail_reduce")
    pieces['a_w_in'] = (mine_i, theirs_i)

    def whole(name):
        mine, theirs = pieces[name]
        return jnp.concatenate([jnp.where(ci == 0, mine, theirs), jnp.where(ci == 0, theirs, mine)], axis=0)

    for k in ('a_w_in', 'a_w_out', 'c_w_in', 'c_w_out'):
        grads[k] = whole(k)[None]
    for short, key in (('gate', 'ffn_w_gate'), ('up', 'ffn_w_up'), ('down', 'ffn_w_down')):
        grads[key] = jnp.stack([whole(short + '0'), whole(short + '1')])

    for k, g in zip(tail_names, _unpack(small_sum, small_layout)):
        if k in SMALL_SHARD:
            width = a[k].shape[-1]
            g = lax.dynamic_slice_in_dim(g, chip * width, width, axis=g.ndim - 1)
        grads[k] = g
    loss = grads.pop('loss')[0]

    delta, new_m, new_v = {}, {}, {}
    for k in BIG:
        if k in TRANSPOSED:
            flip = lambda t: jnp.swapaxes(t, 1, 2)
            d, m, v = adamw(flip(a[k]), grads[k], flip(a['m_' + k]), flip(a['v_' + k]), "adamw_" + k)
            grads[k], delta[k], new_m[k], new_v[k] = flip(grads[k]), flip(d), flip(m), flip(v)
        else:
            delta[k], new_m[k], new_v[k] = adamw(a[k], grads[k], a['m_' + k], a['v_' + k], "adamw_" + k)
    two_d = lambda t: t.reshape(1, -1) if t.ndim == 1 else t
    ds, ms, vs = adamw_many([two_d(a[k]) for k in small_names], [two_d(grads[k]) for k in small_names],
                            [two_d(a['m_' + k]) for k in small_names], [two_d(a['v_' + k]) for k in small_names],
                            "adamw_small")
    for k, dv, mv, vv in zip(small_names, ds, ms, vs):
        delta[k], new_m[k], new_v[k] = [t.reshape(a[k].shape) for t in (dv, mv, vv)]

    return (loss, grad_x.reshape(a['x'].shape), *[grads[k] for k in WEIGHTS], *[delta[k] for k in WEIGHTS],
            *[new_m[k] for k in WEIGHTS], *[new_v[k] for k in WEIGHTS])
```

```python
import functools

import jax
import jax.numpy as jnp
from jax import lax
from jax.experimental import pallas as pl
from jax.experimental.pallas import tpu as pltpu

F32 = jnp.float32
BF16 = jnp.bfloat16

D_MODEL = 1024
EPS = 1e-5
N_Q_HEADS, N_KV_HEADS, HEAD_DIM = 8, 2, 64
ATTN_BLOCK = 128
ATTN_WIDTH = N_Q_HEADS * HEAD_DIM
KV_WIDTH = N_KV_HEADS * HEAD_DIM
CONV_WIDTH = 512
CONV_KERNEL = 31
CONV_HALO = 32
POOL_WINDOWS = (2, 4, 8, 16)
POOL_WIDTH = 512
POOL_HALO = 16
SGU_WIDTH = 512
SGU_CHUNK = 128
D_FF = 2816
FF_CHUNK = 128
MXU_COLS = 256
FFN_AHEAD = 1
LANES = 128
N_CHIPS = 4
N_DEV = 8

ADAM_LR, ADAM_B1, ADAM_B2, ADAM_EPS, ADAM_WD, ADAM_STEP = 0.001, 0.9, 0.999, 1e-08, 0.01, 10

VMEM_LIMIT = 56 * 2**20

WEIGHTS = ['mix_norm', 'a_w_in', 'a_b_in', 'a_sinks', 'a_conv_w', 'a_conv_b', 'a_cln_g', 'a_cln_b', 'a_w_out',
           'c_w_in', 'c_w_pool', 'c_pool_scale', 'c_sln_g', 'c_sln_b', 'c_w_s', 'c_b_s', 'c_w_out',
           'ffn_norm', 'ffn_w_gate', 'ffn_w_up', 'ffn_w_down', 'final_norm']
IN_NAMES = (['x'] + WEIGHTS + ['loss_target'] + ['m_' + n for n in WEIGHTS] + ['v_' + n for n in WEIGHTS])


def _params(*sem):
    return pltpu.CompilerParams(dimension_semantics=sem, vmem_limit_bytes=VMEM_LIMIT)


def _dot(a, b):
    return jnp.dot(a, b, preferred_element_type=F32)


def _dot_nt(a, b):
    return lax.dot_general(a, b, (((1,), (1,)), ((), ())), preferred_element_type=F32)


def _dot_tn(a, b):
    return lax.dot_general(a, b, (((0,), (0,)), ((), ())), preferred_element_type=F32)


def _sigmoid(v):
    return 0.5 * jnp.tanh(0.5 * v) + 0.5


def _row_tile(n, pref):
    t = min(n, pref)
    while n % t:
        t //= 2
    return t


def _col_tile(m, rows, budget=6 * 2**20):
    best = LANES
    for t in range(LANES, m + 1, LANES):
        if m % t == 0 and rows * t * 4 <= budget:
            best = t
    return best


class Comm:
    def __init__(self, ins, out_shapes, plan, count, aliases=None):
        self.ins, self.out_shapes, self.plan, self.count, self.aliases = ins, out_shapes, plan, count, aliases or {}

    def __add__(self, other):
        ni, no = len(self.ins), len(self.out_shapes)

        def plan(ins, outs, pos):
            return self.plan(ins[:ni], outs[:no], pos) + other.plan(ins[ni:], outs[no:], pos)

        aliases = dict(self.aliases)
        aliases.update({ni + i: no + o for i, o in other.aliases.items()})
        return Comm(list(self.ins) + list(other.ins), list(self.out_shapes) + list(other.out_shapes), plan,
                    self.count + other.count, aliases)

    def split(self, outs, other):
        return outs[:len(self.out_shapes)], outs[len(self.out_shapes):]


def _pcall(body, name, grid, in_specs, out_specs, out_shape, scratch_shapes, args, sem, comm=None):
    single = not isinstance(out_shape, (list, tuple))
    if single:
        out_specs, out_shape = [out_specs], [out_shape]
    if comm is None:
        res = pl.pallas_call(body, name=name, grid=grid, in_specs=in_specs, out_specs=list(out_specs),
                             out_shape=list(out_shape), scratch_shapes=list(scratch_shapes),
                             compiler_params=_params(*sem))(*args)
        return (res[0] if single else res), []
    na, nci, no, nco, ns = len(args), len(comm.ins), len(out_shape), len(comm.out_shapes), len(scratch_shapes)

    def wrapped(*refs):
        a_refs, ci_refs = refs[:na], refs[na:na + nci]
        o_refs, co_refs = refs[na + nci:na + nci + no], refs[na + nci + no:na + nci + no + nco]
        s_refs = refs[na + nci + no + nco:na + nci + no + nco + ns]
        send_sems, recv_sems = refs[-2], refs[-1]
        pos = _me()

        def copies():
            return [pltpu.make_async_remote_copy(src_ref=s, dst_ref=d, send_sem=send_sems.at[i],
                                                 recv_sem=recv_sems.at[i], device_id=_flip(pos, rel),
                                                 device_id_type=MESH)
                    for i, (rel, s, d) in enumerate(comm.plan(ci_refs, co_refs, pos))]

        first, last = None, None
        for d, size in enumerate(grid):
            f, l = pl.program_id(d) == 0, pl.program_id(d) == size - 1
            first = f if first is None else first & f
            last = l if last is None else last & l

        @pl.when(first)
        def _():
            for cp in copies():
                cp.start()

        body(*a_refs, *o_refs, *s_refs)

        @pl.when(last)
        def _():
            for cp in copies():
                cp.wait()

    res = pl.pallas_call(
        wrapped, name=name, grid=grid,
        in_specs=list(in_specs) + [ANY] * nci, out_specs=list(out_specs) + [ANY] * nco,
        out_shape=list(out_shape) + [jax.ShapeDtypeStruct(s, d) for s, d in comm.out_shapes],
        scratch_shapes=list(scratch_shapes) + [pltpu.SemaphoreType.DMA((comm.count,)),
                                               pltpu.SemaphoreType.DMA((comm.count,))],
        input_output_aliases={na + i: no + o for i, o in comm.aliases.items()},
        compiler_params=_params(*(["arbitrary"] * len(grid))),
    )(*args, *comm.ins)
    outs = res[:no]
    return (outs[0] if single else outs), list(res[no:])


def norm_inproj(x, gain, w, bias, splits, dtypes, name, comm=None, w_transposed=False):
    n = x.shape[0]
    m = w.shape[0] if w_transposed else w.shape[1]
    tm = _row_tile(n, 1024)

    def body(x_ref, g_ref, w_ref, b_ref, hn_ref, *outs):
        xv = x_ref[...]
        r = lax.rsqrt(jnp.mean(xv * xv, axis=-1, keepdims=True) + EPS)
        hn = ((xv * r) * g_ref[...]).astype(BF16)
        hn_ref[...] = hn
        z = (_dot_nt if w_transposed else _dot)(hn, w_ref[...]) + b_ref[...]
        for o, (lo, hi) in zip(outs, splits):
            o[...] = z[:, lo:hi].astype(o.dtype)

    out_shape = [jax.ShapeDtypeStruct((n, D_MODEL), BF16)]
    out_specs = [pl.BlockSpec((tm, D_MODEL), lambda i: (i, 0))]
    for (lo, hi), dt in zip(splits, dtypes):
        out_shape.append(jax.ShapeDtypeStruct((n, hi - lo), dt))
        out_specs.append(pl.BlockSpec((tm, hi - lo), lambda i: (i, 0)))
    return _pcall(
        body, name, (n // tm,),
        [pl.BlockSpec((tm, D_MODEL), lambda i: (i, 0)),
         pl.BlockSpec((1, D_MODEL), lambda i: (0, 0)),
         pl.BlockSpec(w.shape, lambda i: (0, 0)),
         pl.BlockSpec((1, m), lambda i: (0, 0))],
        out_specs, out_shape, [], (x, gain, w, bias), ("parallel",), comm)


def out_proj(res, m1, m2, w, name, comm=None):
    n = res.shape[0]
    k1, k2 = m1.shape[1], m2.shape[1]
    assert k1 == k2
    tm = _row_tile(n, 1024)

    def body(r_ref, a_ref, b_ref, w1_ref, w2_ref, o_ref):
        o_ref[...] = r_ref[...] + _dot(a_ref[...], w1_ref[...]) + _dot(b_ref[...], w2_ref[...])

    return _pcall(
        body, name, (n // tm,),
        [pl.BlockSpec((tm, D_MODEL), lambda i: (i, 0)),
         pl.BlockSpec((tm, k1), lambda i: (i, 0)),
         pl.BlockSpec((tm, k2), lambda i: (i, 0)),
         pl.BlockSpec((k1, D_MODEL), lambda i: (0, 0)),
         pl.BlockSpec((k2, D_MODEL), lambda i: (1, 0))],
        pl.BlockSpec((tm, D_MODEL), lambda i: (i, 0)),
        jax.ShapeDtypeStruct((n, D_MODEL), F32), [], (res, m1, m2, w, w), ("parallel",), comm)


def ffn_gate_up(h, gain, wg_t, wu_t, name, comm=None):
    n = h.shape[0]
    tm = _row_tile(n, 512)
    th = D_FF

    def body(h_ref, g_ref, wg_ref, wu_ref, hn_ref, go_ref, uo_ref):
        @pl.when(pl.program_id(1) == 0)
        def _():
            xv = h_ref[...]
            r = lax.rsqrt(jnp.mean(xv * xv, axis=-1, keepdims=True) + EPS)
            hn_ref[...] = ((xv * r) * g_ref[...]).astype(BF16)

        hn = hn_ref[...]
        go_ref[...] = _dot_nt(hn, wg_ref[...]).astype(BF16)
        uo_ref[...] = _dot_nt(hn, wu_ref[...]).astype(BF16)

    return _pcall(
        body, name, (n // tm, D_FF // th),
        [pl.BlockSpec((tm, D_MODEL), lambda i, j: (i, 0)),
         pl.BlockSpec((1, D_MODEL), lambda i, j: (0, 0)),
         pl.BlockSpec((th, D_MODEL), lambda i, j: (j, 0), pipeline_mode=pl.Buffered(1)),
         pl.BlockSpec((th, D_MODEL), lambda i, j: (j, 0), pipeline_mode=pl.Buffered(1))],
        [pl.BlockSpec((tm, D_MODEL), lambda i, j: (i, 0)),
         pl.BlockSpec((tm, th), lambda i, j: (i, j)),
         pl.BlockSpec((tm, th), lambda i, j: (i, j))],
        [jax.ShapeDtypeStruct((n, D_MODEL), BF16),
         jax.ShapeDtypeStruct((n, D_FF), BF16),
         jax.ShapeDtypeStruct((n, D_FF), BF16)],
        [], (h, gain, wg_t, wu_t), ("parallel", "arbitrary"), comm)


def ffn_down(h, g, u, wd, name, comm=None):
    n = h.shape[0]
    tm = _row_tile(n, 1024)

    def body(h_ref, g_ref, u_ref, w_ref, o_ref, a_ref):
        for c0 in range(0, D_FF, FF_CHUNK):
            gv = g_ref[:, c0:c0 + FF_CHUNK]
            a_ref[:, c0:c0 + FF_CHUNK] = gv * _sigmoid(gv) * u_ref[:, c0:c0 + FF_CHUNK]
        o_ref[...] = h_ref[...] + _dot(a_ref[...], w_ref[...])

    return _pcall(
        body, name, (n // tm,),
        [pl.BlockSpec((tm, D_MODEL), lambda i: (i, 0)),
         pl.BlockSpec((tm, D_FF), lambda i: (i, 0)),
         pl.BlockSpec((tm, D_FF), lambda i: (i, 0)),
         pl.BlockSpec((D_FF, D_MODEL), lambda i: (0, 0), pipeline_mode=pl.Buffered(1))],
        pl.BlockSpec((tm, D_MODEL), lambda i: (i, 0)),
        jax.ShapeDtypeStruct((n, D_MODEL), F32),
        [pltpu.VMEM((tm, D_FF), BF16)], (h, g, u, wd), ("parallel",), comm)


def ffn_down_bwd(dh, g, u, wd, name, comm=None):
    n = dh.shape[0]
    tm = _row_tile(n, 512)

    def body(dh_ref, g_ref, u_ref, w_ref, dg_ref, du_ref, a_ref):
        dhb = dh_ref[...].astype(BF16)
        chunks = [slice(c0, c0 + MXU_COLS) for c0 in range(0, D_FF, MXU_COLS)]
        ahead = [_dot_nt(dhb, w_ref[c, :]) for c in chunks[:FFN_AHEAD]]
        for i, cols in enumerate(chunks):
            da = ahead.pop(0).astype(BF16)
            if i + FFN_AHEAD < len(chunks):
                ahead.append(_dot_nt(dhb, w_ref[chunks[i + FFN_AHEAD], :]))
            gv, uv = g_ref[:, cols], u_ref[:, cols]
            sg = _sigmoid(gv)
            act = gv * sg
            dg_ref[:, cols] = (da * uv) * (sg + act * (1.0 - sg))
            du_ref[:, cols] = da * act
            a_ref[:, cols] = act * uv

    spec_h = pl.BlockSpec((tm, D_FF), lambda i: (i, 0))
    return _pcall(
        body, name, (n // tm,),
        [pl.BlockSpec((tm, D_MODEL), lambda i: (i, 0)), spec_h, spec_h,
         pl.BlockSpec((D_FF, D_MODEL), lambda i: (0, 0))],
        [spec_h, spec_h, spec_h], [jax.ShapeDtypeStruct((n, D_FF), BF16)] * 3,
        [], (dh, g, u, wd), ("parallel",), comm)


def mm_tn(x, dy, name, xsum=False, comm=None):
    n, k = x.shape
    m = dy.shape[1]
    tk = _col_tile(k, m)
    tt = _row_tile(n, 2048)

    def body(x_ref, dy_ref, o_ref, *rest):
        xt_ref = rest[-1]
        t = pl.program_id(1)
        xv = x_ref[...]
        xt_ref[...] = xv.astype(BF16).T
        part = _dot(xt_ref[...], dy_ref[...].astype(BF16))

        @pl.when(t == 0)
        def _():
            o_ref[...] = part

        @pl.when(t > 0)
        def _():
            o_ref[...] += part

        if xsum:
            cs = jnp.broadcast_to(jnp.sum(xv.astype(F32), axis=0, keepdims=True), rest[0].shape)

            @pl.when(t == 0)
            def _():
                rest[0][...] = cs

            @pl.when(t > 0)
            def _():
                rest[0][...] += cs

    out_shape = [jax.ShapeDtypeStruct((k, m), F32)]
    out_specs = [pl.BlockSpec((tk, m), lambda j, t: (j, 0))]
    if xsum:
        out_shape.append(jax.ShapeDtypeStruct((8, k), F32))
        out_specs.append(pl.BlockSpec((8, tk), lambda j, t: (0, j)))
    res, comm_outs = _pcall(
        body, name, (k // tk, n // tt),
        [pl.BlockSpec((tt, tk), lambda j, t: (t, j)),
         pl.BlockSpec((tt, m), lambda j, t: (t, 0))],
        out_specs, out_shape, [pltpu.VMEM((tk, tt), BF16)], (x, dy), ("arbitrary", "arbitrary"), comm)
    res = (res[0], res[1][0]) if xsum else res[0]
    return res if comm is None else (res, comm_outs)


def out_proj_bwd(dh, w, dtypes, name):
    n = dh.shape[0]
    k = w.shape[0]
    half = k // 2
    tm = _row_tile(n, 1024)

    def body(dh_ref, w_ref, a_ref, b_ref):
        dm = _dot_nt(dh_ref[...].astype(BF16), w_ref[...])
        a_ref[...] = dm[:, :half].astype(a_ref.dtype)
        b_ref[...] = dm[:, half:].astype(b_ref.dtype)

    return pl.pallas_call(
        body, name=name, grid=(n // tm,),
        in_specs=[pl.BlockSpec((tm, D_MODEL), lambda i: (i, 0)),
                  pl.BlockSpec((k, D_MODEL), lambda i: (0, 0))],
        out_specs=[pl.BlockSpec((tm, half), lambda i: (i, 0))] * 2,
        out_shape=[jax.ShapeDtypeStruct((n, half), dtypes[0]), jax.ShapeDtypeStruct((n, half), dtypes[1])],
        compiler_params=_params("parallel"),
    )(dh, w)


def proj_rms_bwd(dys, ws, h_in, gain, dres, nk, name, tm_pref=512, comm=None, w_transposed=False):
    n = h_in.shape[0]
    npair = len(dys)
    tm = _row_tile(n, tm_pref)
    tks = [dy.shape[1] // nk for dy in dys]
    mm = _dot if w_transposed else _dot_nt

    def body(*refs):
        dy_refs = refs[:npair]
        w_refs = refs[npair:2 * npair]
        h_ref, g_ref, dr_ref, o_ref, dg_ref, acc_ref = refs[2 * npair:]
        i, k = pl.program_id(0), pl.program_id(1)
        part = mm(dy_refs[0][...], w_refs[0][...])
        for p in range(1, npair):
            part = part + mm(dy_refs[p][...], w_refs[p][...])

        @pl.when(k == 0)
        def _():
            acc_ref[...] = part

        @pl.when(k > 0)
        def _():
            acc_ref[...] += part

        @pl.when(k == nk - 1)
        def _():
            dhn = acc_ref[...]
            xv = h_ref[...]
            r = lax.rsqrt(jnp.mean(xv * xv, axis=-1, keepdims=True) + EPS)
            xh = xv * r
            uv = dhn * g_ref[...]
            o_ref[...] = dr_ref[...] + r * (uv - xh * jnp.mean(uv * xh, axis=-1, keepdims=True))
            dgp = jnp.broadcast_to(jnp.sum(dhn * xh, axis=0, keepdims=True), dg_ref.shape)

            @pl.when(i == 0)
            def _():
                dg_ref[...] = dgp

            @pl.when(i > 0)
            def _():
                dg_ref[...] += dgp

    row = pl.BlockSpec((tm, D_MODEL), lambda i, k: (i, 0))
    in_specs = [pl.BlockSpec((tm, tk), lambda i, k: (i, k)) for tk in tks]
    once = dict(pipeline_mode=pl.Buffered(1)) if nk == 1 else {}
    if w_transposed:
        in_specs += [pl.BlockSpec((tk, D_MODEL), lambda i, k: (k, 0), **once) for tk in tks]
    else:
        in_specs += [pl.BlockSpec((D_MODEL, tk), lambda i, k: (0, k), **once) for tk in tks]
    in_specs += [row, pl.BlockSpec((1, D_MODEL), lambda i, k: (0, 0)), row]
    (dh, dgain), comm_outs = _pcall(
        body, name, (n // tm, nk), in_specs,
        [row, pl.BlockSpec((8, D_MODEL), lambda i, k: (0, 0))],
        [jax.ShapeDtypeStruct((n, D_MODEL), F32), jax.ShapeDtypeStruct((8, D_MODEL), F32)],
        [pltpu.VMEM((tm, D_MODEL), F32)], (*dys, *ws, h_in, gain, dres), ("arbitrary", "arbitrary"), comm)
    return dh, dgain[0], comm_outs


def loss_head(h, gain, target, name):
    n = h.shape[0]
    tm = _row_tile(n, 512)

    def body(h_ref, g_ref, t_ref, dh_ref, dg_ref, l_ref):
        i = pl.program_id(0)
        xv = h_ref[...]
        r = lax.rsqrt(jnp.mean(xv * xv, axis=-1, keepdims=True) + EPS)
        xh = xv * r
        err = xh * g_ref[...] - t_ref[...]
        dy = err * (1.0 / D_MODEL)
        uv = dy * g_ref[...]
        dh_ref[...] = r * (uv - xh * jnp.mean(uv * xh, axis=-1, keepdims=True))
        dgp = jnp.broadcast_to(jnp.sum(dy * xh, axis=0, keepdims=True), dg_ref.shape)
        lp = jnp.sum(jnp.sum(err * err, axis=-1, keepdims=True), axis=0, keepdims=True) * (0.5 / D_MODEL)
        lp = jnp.broadcast_to(lp, l_ref.shape)

        @pl.when(i == 0)
        def _():
            dg_ref[...] = dgp
            l_ref[...] = lp

        @pl.when(i > 0)
        def _():
            dg_ref[...] += dgp
            l_ref[...] += lp

    row = pl.BlockSpec((tm, D_MODEL), lambda i: (i, 0))
    dh, dg, l = pl.pallas_call(
        body, name=name, grid=(n // tm,),
        in_specs=[row, pl.BlockSpec((1, D_MODEL), lambda i: (0, 0)), row],
        out_specs=[row, pl.BlockSpec((8, D_MODEL), lambda i: (0, 0)), pl.BlockSpec((8, LANES), lambda i: (0, 0))],
        out_shape=[jax.ShapeDtypeStruct((n, D_MODEL), F32), jax.ShapeDtypeStruct((8, D_MODEL), F32),
                   jax.ShapeDtypeStruct((8, LANES), F32)],
        compiler_params=_params("arbitrary"),
    )(h, gain, target)
    return dh, dg[0], l[0, 0]


GROUP = N_Q_HEADS // N_KV_HEADS
GQ = GROUP * ATTN_BLOCK


def _attn_mask_t(n):
    r = lax.broadcasted_iota(jnp.int32, (2 * ATTN_BLOCK, GQ), 0)
    qi = lax.broadcasted_iota(jnp.int32, (2 * ATTN_BLOCK, GQ), 1) & (ATTN_BLOCK - 1)
    band = (r > qi) & (r <= qi + ATTN_BLOCK)
    return band & ((r >= ATTN_BLOCK) | (n > 0))


def _stack_heads(blk, kh):
    return jnp.concatenate([blk[:, (kh * GROUP + g) * HEAD_DIM:(kh * GROUP + g + 1) * HEAD_DIM]
                            for g in range(GROUP)], axis=0)


def _attn_probs_t(kk, qs, mask, sink):
    s = _dot_nt(kk, qs) * (HEAD_DIM ** -0.5)
    s = jnp.where(mask, s, -1e30)
    m = jnp.maximum(jnp.max(s, axis=0, keepdims=True), sink)
    p = jnp.exp(s - m)
    esink = jnp.exp(sink - m)
    inv = 1.0 / (jnp.sum(p, axis=0, keepdims=True) + esink)
    return p * inv, esink * inv


def attn_fwd(q, kv, sinks_t, nseq, seq, name, comm=None):
    nb = seq // ATTN_BLOCK

    def body(q_ref, kv_ref, s_ref, o_ref, kvp):
        kvp[0:ATTN_BLOCK, :] = jnp.zeros((ATTN_BLOCK, 2 * KV_WIDTH), BF16)
        kvp[ATTN_BLOCK:, :] = kv_ref[...]

        def blk(n, carry):
            st = pl.multiple_of(n * ATTN_BLOCK, ATTN_BLOCK)
            qb = q_ref[pl.ds(st, ATTN_BLOCK), :]
            kw = kvp[pl.ds(st, 2 * ATTN_BLOCK), :]
            mask = _attn_mask_t(n)
            for kh in range(N_KV_HEADS):
                kk = kw[:, kh * HEAD_DIM:(kh + 1) * HEAD_DIM]
                vv = kw[:, KV_WIDTH + kh * HEAD_DIM:KV_WIDTH + (kh + 1) * HEAD_DIM]
                probs, _ = _attn_probs_t(kk, _stack_heads(qb, kh), mask, s_ref[kh:kh + 1, :])
                ot = _dot_tn(vv, probs.astype(BF16))
                for pair in range(GROUP // 2):
                    two = jnp.concatenate([ot[:, (2 * pair) * ATTN_BLOCK:(2 * pair + 1) * ATTN_BLOCK],
                                           ot[:, (2 * pair + 1) * ATTN_BLOCK:(2 * pair + 2) * ATTN_BLOCK]], axis=0)
                    col = (kh * GROUP + 2 * pair) * HEAD_DIM
                    o_ref[pl.ds(st, ATTN_BLOCK), col:col + 2 * HEAD_DIM] = two.T.astype(o_ref.dtype)
            return carry

        lax.fori_loop(0, nb, blk, 0, unroll=4)

    return _pcall(
        body, name, (nseq,),
        [pl.BlockSpec((seq, ATTN_WIDTH), lambda b: (b, 0)),
         pl.BlockSpec((seq, 2 * KV_WIDTH), lambda b: (b, 0)),
         pl.BlockSpec((8, GQ), lambda b: (0, 0))],
        pl.BlockSpec((seq, ATTN_WIDTH), lambda b: (b, 0)),
        jax.ShapeDtypeStruct((nseq * seq, ATTN_WIDTH), BF16),
        [pltpu.VMEM((ATTN_BLOCK + seq, 2 * KV_WIDTH), BF16)], (q, kv, sinks_t), ("parallel",), comm)


def attn_bwd(q, kv, sinks_t, do, nseq, seq, name, comm=None):
    nb = seq // ATTN_BLOCK

    def body(q_ref, kv_ref, s_ref, do_ref, dq_ref, dkv_ref, ds_ref, kvp, dkvp, dsacc):
        @pl.when(pl.program_id(0) == 0)
        def _():
            dsacc[...] = jnp.zeros(dsacc.shape, F32)

        kvp[0:ATTN_BLOCK, :] = jnp.zeros((ATTN_BLOCK, 2 * KV_WIDTH), BF16)
        kvp[ATTN_BLOCK:, :] = kv_ref[...]
        dkvp[...] = jnp.zeros(dkvp.shape, F32)

        def blk(n, carry):
            st = pl.multiple_of(n * ATTN_BLOCK, ATTN_BLOCK)
            qb = q_ref[pl.ds(st, ATTN_BLOCK), :]
            dob = do_ref[pl.ds(st, ATTN_BLOCK), :]
            kw = kvp[pl.ds(st, 2 * ATTN_BLOCK), :]
            mask = _attn_mask_t(n)
            for kh in range(N_KV_HEADS):
                kk = kw[:, kh * HEAD_DIM:(kh + 1) * HEAD_DIM]
                vv = kw[:, KV_WIDTH + kh * HEAD_DIM:KV_WIDTH + (kh + 1) * HEAD_DIM]
                qs = _stack_heads(qb, kh)
                dos = _stack_heads(dob, kh)
                probs, psink = _attn_probs_t(kk, qs, mask, s_ref[kh:kh + 1, :])
                dp = _dot_nt(vv, dos)
                dv = _dot(probs.astype(BF16), dos)
                rowdot = jnp.sum(probs * dp, axis=0, keepdims=True)
                dsc = (probs * (dp - rowdot) * (HEAD_DIM ** -0.5)).astype(BF16)
                dsacc[kh:kh + 1, :] += -psink * rowdot
                dk = _dot(dsc, qs)
                dqs = _dot_tn(dsc, kk)
                for g in range(GROUP):
                    col = (kh * GROUP + g) * HEAD_DIM
                    dq_ref[pl.ds(st, ATTN_BLOCK), col:col + HEAD_DIM] = (
                        dqs[g * ATTN_BLOCK:(g + 1) * ATTN_BLOCK].astype(dq_ref.dtype))
                dkvp[pl.ds(st, 2 * ATTN_BLOCK), kh * HEAD_DIM:(kh + 1) * HEAD_DIM] += dk
                dkvp[pl.ds(st, 2 * ATTN_BLOCK), KV_WIDTH + kh * HEAD_DIM:KV_WIDTH + (kh + 1) * HEAD_DIM] += dv
            return carry

        lax.fori_loop(0, nb, blk, 0, unroll=2)
        dkv_ref[...] = dkvp[ATTN_BLOCK:, :].astype(dkv_ref.dtype)

        @pl.when(pl.program_id(0) == nseq - 1)
        def _():
            for kh in range(N_KV_HEADS):
                for g in range(GROUP):
                    tot = jnp.sum(dsacc[kh:kh + 1, g * ATTN_BLOCK:(g + 1) * ATTN_BLOCK], axis=1, keepdims=True)
                    ds_ref[kh * GROUP + g:kh * GROUP + g + 1, :] = jnp.broadcast_to(tot, (1, LANES))

    seq_q = pl.BlockSpec((seq, ATTN_WIDTH), lambda b: (b, 0))
    seq_kv = pl.BlockSpec((seq, 2 * KV_WIDTH), lambda b: (b, 0))
    return _pcall(
        body, name, (nseq,),
        [seq_q, seq_kv, pl.BlockSpec((8, GQ), lambda b: (0, 0)), seq_q],
        [seq_q, seq_kv, pl.BlockSpec((N_Q_HEADS, LANES), lambda b: (0, 0))],
        [jax.ShapeDtypeStruct((nseq * seq, ATTN_WIDTH), BF16),
         jax.ShapeDtypeStruct((nseq * seq, 2 * KV_WIDTH), BF16),
         jax.ShapeDtypeStruct((N_Q_HEADS, LANES), F32)],
        [pltpu.VMEM((ATTN_BLOCK + seq, 2 * KV_WIDTH), BF16),
         pltpu.VMEM((ATTN_BLOCK + seq, 2 * KV_WIDTH), F32),
         pltpu.VMEM((8, GQ), F32)], (q, kv, sinks_t, do), ("arbitrary",), comm)


CONV_T = 128


SUBLANES = 8


def _shifted_rows(win):
    phases = [win] + [pltpu.roll(win, s, 0) for s in range(1, SUBLANES)]

    def shifted(s):
        lo = CONV_HALO - SUBLANES * (s // SUBLANES)
        return phases[s % SUBLANES][lo:lo + CONV_T]

    return shifted


def _conv_taps(win, w_ref, lanes, init):
    shifted = _shifted_rows(win)
    acc = init
    for j in range(CONV_KERNEL):
        acc = acc + w_ref[j:j + 1, lanes] * shifted(CONV_KERNEL - 1 - j)
    return acc


def _conv_block(h0p, w_ref, vec_ref, st):
    cols = []
    for cs in range(CONV_WIDTH // LANES):
        lanes = slice(cs * LANES, (cs + 1) * LANES)
        win = h0p[pl.ds(st, CONV_T + CONV_HALO), lanes]
        init = jnp.broadcast_to(vec_ref[0:1, lanes], (CONV_T, LANES))
        cols.append(_conv_taps(win, w_ref, lanes, init))
    return jnp.concatenate(cols, axis=-1)


def _glu_store(c_ref, h0p, st):
    cb = c_ref[pl.ds(st, CONV_T), :]
    h0p[pl.ds(pl.multiple_of(st + CONV_HALO, CONV_HALO), CONV_T), :] = cb[:, :CONV_WIDTH] * _sigmoid(cb[:, CONV_WIDTH:])


def conv_fwd(c, w, vec, nseq, seq, name, comm=None):
    nb = seq // CONV_T

    def body(c_ref, w_ref, vec_ref, o_ref, h1_ref, h0p):
        h0p[0:CONV_HALO, :] = jnp.zeros((CONV_HALO, CONV_WIDTH), F32)

        def blk(n, carry):
            st = pl.multiple_of(n * CONV_T, CONV_T)
            _glu_store(c_ref, h0p, st)
            h1 = _conv_block(h0p, w_ref, vec_ref, st)
            h1_ref[pl.ds(st, CONV_T), :] = h1
            mu = jnp.mean(h1, axis=-1, keepdims=True)
            xc = h1 - mu
            rstd = lax.rsqrt(jnp.mean(xc * xc, axis=-1, keepdims=True) + EPS)
            y = xc * rstd * vec_ref[1:2, :] + vec_ref[2:3, :]
            o_ref[pl.ds(st, CONV_T), :] = (y * _sigmoid(y)).astype(o_ref.dtype)
            return carry

        lax.fori_loop(0, nb, blk, 0)

    return _pcall(
        body, name, (nseq,),
        [pl.BlockSpec((seq, 2 * CONV_WIDTH), lambda b: (b, 0)),
         pl.BlockSpec((32, CONV_WIDTH), lambda b: (0, 0)),
         pl.BlockSpec((8, CONV_WIDTH), lambda b: (0, 0))],
        [pl.BlockSpec((seq, CONV_WIDTH), lambda b: (b, 0))] * 2,
        [jax.ShapeDtypeStruct((nseq * seq, CONV_WIDTH), BF16), jax.ShapeDtypeStruct((nseq * seq, CONV_WIDTH), F32)],
        [pltpu.VMEM((CONV_HALO + seq, CONV_WIDTH), F32)], (c, w, vec), ("parallel",), comm)


def conv_bwd(c, h1_saved, w, vec, dout, nseq, seq, name, comm=None):
    nb = seq // CONV_T

    def body(c_ref, h1_ref, w_ref, vec_ref, do_ref, dc_ref, dw_ref, dvec_ref, h0p, dh1p, dwacc):
        @pl.when(pl.program_id(0) == 0)
        def _():
            dwacc[...] = jnp.zeros(dwacc.shape, F32)
            dvec_ref[...] = jnp.zeros(dvec_ref.shape, F32)

        h0p[0:CONV_HALO, :] = jnp.zeros((CONV_HALO, CONV_WIDTH), F32)
        dh1p[seq:seq + CONV_HALO, :] = jnp.zeros((CONV_HALO, CONV_WIDTH), F32)

        def pass_a(n, carry):
            st = pl.multiple_of(n * CONV_T, CONV_T)
            _glu_store(c_ref, h0p, st)
            h1 = h1_ref[pl.ds(st, CONV_T), :]
            mu = jnp.mean(h1, axis=-1, keepdims=True)
            xc = h1 - mu
            rstd = lax.rsqrt(jnp.mean(xc * xc, axis=-1, keepdims=True) + EPS)
            xh = xc * rstd
            y = xh * vec_ref[1:2, :] + vec_ref[2:3, :]
            sg = _sigmoid(y)
            dy = do_ref[pl.ds(st, CONV_T), :] * (sg * (1.0 + y * (1.0 - sg)))
            dvec_ref[1:2, :] += jnp.sum(dy * xh, axis=0, keepdims=True)
            dvec_ref[2:3, :] += jnp.sum(dy, axis=0, keepdims=True)
            dxh = dy * vec_ref[1:2, :]
            dh1 = rstd * (dxh - jnp.mean(dxh, axis=-1, keepdims=True)
                          - xh * jnp.mean(dxh * xh, axis=-1, keepdims=True))
            dvec_ref[0:1, :] += jnp.sum(dh1, axis=0, keepdims=True)
            dh1p[pl.ds(st, CONV_T), :] = dh1
            return carry

        lax.fori_loop(0, nb, pass_a, 0)

        def pass_b(n, carry):
            st = pl.multiple_of(n * CONV_T, CONV_T)
            cols = []
            for cs in range(CONV_WIDTH // LANES):
                lanes = slice(cs * LANES, (cs + 1) * LANES)
                wind = dh1p[pl.ds(st, CONV_T + CONV_HALO), lanes]
                winh = h0p[pl.ds(st, CONV_T + CONV_HALO), lanes]
                d1 = wind[0:CONV_T]
                shifted_d, shifted_h = _shifted_rows(wind), _shifted_rows(winh)
                acc = jnp.zeros((CONV_T, LANES), F32)
                for j in range(CONV_KERNEL):
                    acc = acc + w_ref[j:j + 1, lanes] * shifted_d(2 + j)
                    prod = d1 * shifted_h(CONV_KERNEL - 1 - j)
                    part = prod[0:8]
                    for r in range(8, CONV_T, 8):
                        part = part + prod[r:r + 8]
                    dwacc[8 * j:8 * j + 8, lanes] += part
                cols.append(acc)
            dh0 = jnp.concatenate(cols, axis=-1)
            cb = c_ref[pl.ds(st, CONV_T), :]
            av, gt = cb[:, :CONV_WIDTH], cb[:, CONV_WIDTH:]
            sg = _sigmoid(gt)
            dc_ref[pl.ds(st, CONV_T), :] = jnp.concatenate(
                [dh0 * sg, dh0 * av * sg * (1.0 - sg)], axis=-1).astype(dc_ref.dtype)
            return carry

        lax.fori_loop(0, nb, pass_b, 0)

        @pl.when(pl.program_id(0) == nseq - 1)
        def _():
            dw_ref[...] = jnp.zeros(dw_ref.shape, F32)
            for j in range(CONV_KERNEL):
                dw_ref[j:j + 1, :] = jnp.sum(dwacc[8 * j:8 * j + 8, :], axis=0, keepdims=True)

    return _pcall(
        body, name, (nseq,),
        [pl.BlockSpec((seq, 2 * CONV_WIDTH), lambda b: (b, 0)),
         pl.BlockSpec((seq, CONV_WIDTH), lambda b: (b, 0)),
         pl.BlockSpec((32, CONV_WIDTH), lambda b: (0, 0)),
         pl.BlockSpec((8, CONV_WIDTH), lambda b: (0, 0)),
         pl.BlockSpec((seq, CONV_WIDTH), lambda b: (b, 0))],
        [pl.BlockSpec((seq, 2 * CONV_WIDTH), lambda b: (b, 0)),
         pl.BlockSpec((32, CONV_WIDTH), lambda b: (0, 0)),
         pl.BlockSpec((8, CONV_WIDTH), lambda b: (0, 0))],
        [jax.ShapeDtypeStruct((nseq * seq, 2 * CONV_WIDTH), BF16),
         jax.ShapeDtypeStruct((32, CONV_WIDTH), F32),
         jax.ShapeDtypeStruct((8, CONV_WIDTH), F32)],
        [pltpu.VMEM((CONV_HALO + seq, CONV_WIDTH), F32),
         pltpu.VMEM((seq + CONV_HALO, CONV_WIDTH), F32),
         pltpu.VMEM((8 * 32, CONV_WIDTH), F32)], (c, h1_saved, w, vec, dout), ("arbitrary",), comm)


POOL_T = 128


def _pooled_block(zpp, st, grp):
    lanes = slice(grp * LANES, (grp + 1) * LANES)
    win = zpp[pl.ds(st, POOL_T + POOL_HALO), lanes]
    acc = win
    for lvl in range(grp + 1):
        acc = acc + pltpu.roll(acc, 1 << lvl, 0)
    t = st + lax.broadcasted_iota(jnp.int32, (POOL_T, 1), 0)
    inv = 1.0 / jnp.minimum(t + 1, POOL_WINDOWS[grp]).astype(F32)
    return acc[POOL_HALO:] * inv - win[POOL_HALO:], inv


def pool_fwd(zp, wp, scale, nseq, seq, name):
    nb = seq // POOL_T

    def body(z_ref, wp_ref, sc_ref, o_ref, zpp):
        zpp[0:POOL_HALO, :] = jnp.zeros((POOL_HALO, POOL_WIDTH), F32)
        zpp[POOL_HALO:, :] = z_ref[...]

        def blk(n, carry):
            st = pl.multiple_of(n * POOL_T, POOL_T)
            for grp in range(len(POOL_WINDOWS)):
                lanes = slice(grp * LANES, (grp + 1) * LANES)
                pooled, _ = _pooled_block(zpp, st, grp)
                o_ref[pl.ds(st, POOL_T), lanes] = (
                    _dot(pooled.astype(BF16), wp_ref[grp]) * sc_ref[0:1, lanes]).astype(o_ref.dtype)
            return carry

        lax.fori_loop(0, nb, blk, 0)

    return pl.pallas_call(
        body, name=name, grid=(nseq,),
        in_specs=[pl.BlockSpec((seq, POOL_WIDTH), lambda b: (b, 0)),
                  pl.BlockSpec((4, LANES, LANES), lambda b: (0, 0, 0)),
                  pl.BlockSpec((1, POOL_WIDTH), lambda b: (0, 0))],
        out_specs=pl.BlockSpec((seq, POOL_WIDTH), lambda b: (b, 0)),
        out_shape=jax.ShapeDtypeStruct((nseq * seq, POOL_WIDTH), BF16),
        scratch_shapes=[pltpu.VMEM((POOL_HALO + seq, POOL_WIDTH), F32)],
        compiler_params=_params("parallel"),
    )(zp, wp, scale)


def pool_bwd(zp, wp, scale, dout, nseq, seq, name, comm=None):
    nb = seq // POOL_T

    def body(z_ref, wp_ref, sc_ref, do_ref, dz_ref, dwp_ref, dsc_ref, zpp, dpcp, negd):
        @pl.when(pl.program_id(0) == 0)
        def _():
            dwp_ref[...] = jnp.zeros(dwp_ref.shape, F32)
            dsc_ref[...] = jnp.zeros(dsc_ref.shape, F32)

        zpp[0:POOL_HALO, :] = jnp.zeros((POOL_HALO, POOL_WIDTH), F32)
        zpp[POOL_HALO:, :] = z_ref[...]
        dpcp[seq:seq + POOL_HALO, :] = jnp.zeros((POOL_HALO, POOL_WIDTH), F32)

        def pass_a(n, carry):
            st = pl.multiple_of(n * POOL_T, POOL_T)
            for grp in range(len(POOL_WINDOWS)):
                lanes = slice(grp * LANES, (grp + 1) * LANES)
                pooled, inv = _pooled_block(zpp, st, grp)
                pb = pooled.astype(BF16)
                dob = do_ref[pl.ds(st, POOL_T), lanes]
                dsc_ref[0:1, lanes] += jnp.sum(dob * _dot(pb, wp_ref[grp]), axis=0, keepdims=True)
                dpm = (dob * sc_ref[0:1, lanes]).astype(BF16)
                dwp_ref[grp] += _dot_tn(pb, dpm)
                dpooled = _dot_nt(dpm, wp_ref[grp])
                negd[pl.ds(st, POOL_T), lanes] = -dpooled
                dpcp[pl.ds(st, POOL_T), lanes] = dpooled * inv
            return carry

        lax.fori_loop(0, nb, pass_a, 0)

        def pass_b(n, carry):
            st = pl.multiple_of(n * POOL_T, POOL_T)
            rows = POOL_T + POOL_HALO
            for grp in range(len(POOL_WINDOWS)):
                lanes = slice(grp * LANES, (grp + 1) * LANES)
                acc = dpcp[pl.ds(st, rows), lanes]
                for lvl in range(grp + 1):
                    acc = acc + pltpu.roll(acc, rows - (1 << lvl), 0)
                dz_ref[pl.ds(st, POOL_T), lanes] = (acc[0:POOL_T] + negd[pl.ds(st, POOL_T), lanes]).astype(dz_ref.dtype)
            return carry

        lax.fori_loop(0, nb, pass_b, 0)

    seq_spec = pl.BlockSpec((seq, POOL_WIDTH), lambda b: (b, 0))
    return _pcall(
        body, name, (nseq,),
        [seq_spec, pl.BlockSpec((4, LANES, LANES), lambda b: (0, 0, 0)),
         pl.BlockSpec((1, POOL_WIDTH), lambda b: (0, 0)), seq_spec],
        [seq_spec, pl.BlockSpec((4, LANES, LANES), lambda b: (0, 0, 0)),
         pl.BlockSpec((8, POOL_WIDTH), lambda b: (0, 0))],
        [jax.ShapeDtypeStruct((nseq * seq, POOL_WIDTH), BF16),
         jax.ShapeDtypeStruct((4, LANES, LANES), F32),
         jax.ShapeDtypeStruct((8, POOL_WIDTH), F32)],
        [pltpu.VMEM((POOL_HALO + seq, POOL_WIDTH), F32),
         pltpu.VMEM((seq + POOL_HALO, POOL_WIDTH), F32),
         pltpu.VMEM((seq, POOL_WIDTH), F32)], (zp, wp, scale, dout), ("arbitrary",), comm)


GELU_C0 = 0.7978845608028654
GELU_C1 = 0.044715


def _gelu(xv):
    return xv * (0.5 * (1.0 + jnp.tanh(GELU_C0 * (xv + GELU_C1 * (xv * xv * xv)))))


def _gelu_grad(xv):
    t = jnp.tanh(GELU_C0 * (xv + GELU_C1 * (xv * xv * xv)))
    return 0.5 * (1.0 + t) + 0.5 * xv * (1.0 - t * t) * (GELU_C0 * (1.0 + 3.0 * GELU_C1 * xv * xv))


def _tril():
    ti = lax.broadcasted_iota(jnp.int32, (SGU_CHUNK, SGU_CHUNK), 0)
    si = lax.broadcasted_iota(jnp.int32, (SGU_CHUNK, SGU_CHUNK), 1)
    return si <= ti


def sgu_fwd(zs, ws, bst, ln, nseq, seq, name):
    nc = seq // SGU_CHUNK

    def body(z_ref, ws_ref, bs_ref, ln_ref, o_ref):
        tril = _tril()

        def blk(n, carry):
            st = pl.multiple_of(n * SGU_CHUNK, SGU_CHUNK)
            ge = _gelu(z_ref[pl.ds(st, SGU_CHUNK), :])
            uu, vv = ge[:, :SGU_WIDTH], ge[:, SGU_WIDTH:]
            mu = jnp.mean(vv, axis=-1, keepdims=True)
            xc = vv - mu
            rstd = lax.rsqrt(jnp.mean(xc * xc, axis=-1, keepdims=True) + EPS)
            vn = (xc * rstd * ln_ref[0:1, :] + ln_ref[1:2, :]).astype(BF16)
            for g in range(4):
                lanes = slice(g * LANES, (g + 1) * LANES)
                wm = jnp.where(tril, ws_ref[g], 0.0).astype(BF16)
                mixed = _dot(wm, vn[:, lanes]) + bs_ref[:, g:g + 1]
                o_ref[pl.ds(st, SGU_CHUNK), lanes] = (uu[:, lanes] * mixed).astype(o_ref.dtype)
            return carry

        lax.fori_loop(0, nc, blk, 0)

    return pl.pallas_call(
        body, name=name, grid=(nseq,),
        in_specs=[pl.BlockSpec((seq, 2 * SGU_WIDTH), lambda b: (b, 0)),
                  pl.BlockSpec((4, LANES, LANES), lambda b: (0, 0, 0)),
                  pl.BlockSpec((SGU_CHUNK, 4), lambda b: (0, 0)),
                  pl.BlockSpec((8, SGU_WIDTH), lambda b: (0, 0))],
        out_specs=pl.BlockSpec((seq, SGU_WIDTH), lambda b: (b, 0)),
        out_shape=jax.ShapeDtypeStruct((nseq * seq, SGU_WIDTH), BF16),
        compiler_params=_params("parallel"),
    )(zs, ws, bst, ln)


def sgu_bwd(zs, ws, bst, ln, dout, nseq, seq, name, comm=None):
    nc = seq // SGU_CHUNK

    def body(z_ref, ws_ref, bs_ref, ln_ref, do_ref, dz_ref, dws_ref, dbs_ref, dln_ref):
        @pl.when(pl.program_id(0) == 0)
        def _():
            dws_ref[...] = jnp.zeros(dws_ref.shape, F32)
            dbs_ref[...] = jnp.zeros(dbs_ref.shape, F32)
            dln_ref[...] = jnp.zeros(dln_ref.shape, F32)

        tril = _tril()

        def blk(n, carry):
            st = pl.multiple_of(n * SGU_CHUNK, SGU_CHUNK)
            zv = z_ref[pl.ds(st, SGU_CHUNK), :]
            ge = _gelu(zv)
            uu, vv = ge[:, :SGU_WIDTH], ge[:, SGU_WIDTH:]
            mu = jnp.mean(vv, axis=-1, keepdims=True)
            xc = vv - mu
            rstd = lax.rsqrt(jnp.mean(xc * xc, axis=-1, keepdims=True) + EPS)
            xh = xc * rstd
            vn = (xh * ln_ref[0:1, :] + ln_ref[1:2, :]).astype(BF16)
            dob = do_ref[pl.ds(st, SGU_CHUNK), :]
            du_cols, dvn_cols = [], []
            for g in range(4):
                lanes = slice(g * LANES, (g + 1) * LANES)
                wm = jnp.where(tril, ws_ref[g], 0.0).astype(BF16)
                mixed = _dot(wm, vn[:, lanes]) + bs_ref[:, g:g + 1]
                du_cols.append(dob[:, lanes] * mixed)
                dmix = dob[:, lanes] * uu[:, lanes]
                dbs_ref[g] += jnp.broadcast_to(jnp.sum(dmix, axis=-1, keepdims=True), (SGU_CHUNK, LANES))
                dmb = dmix.astype(BF16)
                dws_ref[g] += jnp.where(tril, _dot_nt(dmb, vn[:, lanes]), 0.0)
                dvn_cols.append(_dot_tn(wm, dmb))
            dvn = jnp.concatenate(dvn_cols, axis=-1)
            dln_ref[0:1, :] += jnp.sum(dvn * xh, axis=0, keepdims=True)
            dln_ref[1:2, :] += jnp.sum(dvn, axis=0, keepdims=True)
            dxh = dvn * ln_ref[0:1, :]
            dv = rstd * (dxh - jnp.mean(dxh, axis=-1, keepdims=True)
                         - xh * jnp.mean(dxh * xh, axis=-1, keepdims=True))
            dge = jnp.concatenate(du_cols + [dv], axis=-1)
            dz_ref[pl.ds(st, SGU_CHUNK), :] = (dge * _gelu_grad(zv)).astype(dz_ref.dtype)
            return carry

        lax.fori_loop(0, nc, blk, 0)

    w_spec = pl.BlockSpec((4, LANES, LANES), lambda b: (0, 0, 0))
    ln_spec = pl.BlockSpec((8, SGU_WIDTH), lambda b: (0, 0))
    return _pcall(
        body, name, (nseq,),
        [pl.BlockSpec((seq, 2 * SGU_WIDTH), lambda b: (b, 0)), w_spec,
         pl.BlockSpec((SGU_CHUNK, 4), lambda b: (0, 0)), ln_spec,
         pl.BlockSpec((seq, SGU_WIDTH), lambda b: (b, 0))],
        [pl.BlockSpec((seq, 2 * SGU_WIDTH), lambda b: (b, 0)), w_spec, w_spec, ln_spec],
        [jax.ShapeDtypeStruct((nseq * seq, 2 * SGU_WIDTH), BF16),
         jax.ShapeDtypeStruct((4, LANES, LANES), F32),
         jax.ShapeDtypeStruct((4, LANES, LANES), F32),
         jax.ShapeDtypeStruct((8, SGU_WIDTH), F32)],
        [], (zs, ws, bst, ln, dout), ("arbitrary",), comm)


def _ew_rows(rows, cols, nbuf):
    t = _row_tile(rows, 1024)
    while t > 8 and t * cols * 4 * nbuf * 2 > 24 * 2**20:
        t //= 2
    return t


def adamw(w, g, m, v, name):
    layers, rows, cols = w.shape
    tr = _ew_rows(rows, cols, 7)

    def body(w_ref, g_ref, m_ref, v_ref, d_ref, mo_ref, vo_ref):
        gv = g_ref[...]
        mn = ADAM_B1 * m_ref[...] + (1.0 - ADAM_B1) * gv
        vn = ADAM_B2 * v_ref[...] + (1.0 - ADAM_B2) * (gv * gv)
        m_hat = mn / (1.0 - ADAM_B1 ** ADAM_STEP)
        v_hat = vn / (1.0 - ADAM_B2 ** ADAM_STEP)
        d_ref[...] = -ADAM_LR * (m_hat / (jnp.sqrt(v_hat) + ADAM_EPS) + ADAM_WD * w_ref[...])
        mo_ref[...] = mn
        vo_ref[...] = vn

    spec = pl.BlockSpec((1, tr, cols), lambda l, i: (l, i, 0))
    return pl.pallas_call(
        body, name=name, grid=(layers, rows // tr),
        in_specs=[spec] * 4, out_specs=[spec] * 3,
        out_shape=[jax.ShapeDtypeStruct(w.shape, F32)] * 3,
        compiler_params=_params("parallel", "parallel"),
    )(w, g, m, v)


def adamw_many(ws, gs, ms, vs, name):
    n = len(ws)

    def body(*refs):
        w_refs, g_refs, m_refs, v_refs = refs[:n], refs[n:2 * n], refs[2 * n:3 * n], refs[3 * n:4 * n]
        d_refs, mo_refs, vo_refs = refs[4 * n:5 * n], refs[5 * n:6 * n], refs[6 * n:7 * n]
        for i in range(n):
            gv = g_refs[i][...]
            mn = ADAM_B1 * m_refs[i][...] + (1.0 - ADAM_B1) * gv
            vn = ADAM_B2 * v_refs[i][...] + (1.0 - ADAM_B2) * (gv * gv)
            m_hat = mn / (1.0 - ADAM_B1 ** ADAM_STEP)
            v_hat = vn / (1.0 - ADAM_B2 ** ADAM_STEP)
            d_refs[i][...] = -ADAM_LR * (m_hat / (jnp.sqrt(v_hat) + ADAM_EPS) + ADAM_WD * w_refs[i][...])
            mo_refs[i][...] = mn
            vo_refs[i][...] = vn

    vmem = pl.BlockSpec(memory_space=pltpu.VMEM)
    shapes = [jax.ShapeDtypeStruct(w.shape, F32) for w in ws]
    res = pl.pallas_call(
        body, name=name, in_specs=[vmem] * (4 * n), out_specs=[vmem] * (3 * n), out_shape=shapes * 3,
        compiler_params=pltpu.CompilerParams(vmem_limit_bytes=VMEM_LIMIT),
    )(*ws, *gs, *ms, *vs)
    return res[:n], res[n:2 * n], res[2 * n:]


def add_cast(a, b, name, dtype=BF16):
    nslab, rows, cols = a.shape
    tr = _ew_rows(rows, cols, 3)

    def body(a_ref, b_ref, o_ref):
        o_ref[...] = (a_ref[...] + b_ref[...]).astype(dtype)

    spec = pl.BlockSpec((1, tr, cols), lambda k, i: (k, i, 0))
    return pl.pallas_call(
        body, name=name, grid=(nslab, rows // tr),
        in_specs=[spec, spec], out_specs=spec,
        out_shape=jax.ShapeDtypeStruct(a.shape, dtype),
        compiler_params=_params("parallel", "parallel"),
    )(a, b)


def pair_sum(t, got, core, name):
    nslab, h, cols = got.shape
    tr = _ew_rows(h, cols, 3)
    nb = h // tr

    def body(c_ref, a_ref, b_ref, o_ref):
        o_ref[...] = (a_ref[...] + b_ref[...]).astype(BF16)

    spec = pl.BlockSpec((1, tr, cols), lambda k, i, c: (k, i, 0))
    return pl.pallas_call(
        body, name=name,
        grid_spec=pltpu.PrefetchScalarGridSpec(
            num_scalar_prefetch=1, grid=(nslab, nb),
            in_specs=[pl.BlockSpec((1, tr, cols), lambda k, i, c: (k, c[0] * nb + i, 0)), spec],
            out_specs=spec),
        out_shape=jax.ShapeDtypeStruct(got.shape, BF16),
        compiler_params=_params("parallel", "parallel"),
    )(core, t, got)


def chip_sum(own, parts, core, name):
    npart, h, cols = parts.shape
    tr = _ew_rows(h, cols, npart + 2)
    nb = h // tr

    def body(c_ref, own_ref, p_ref, o_ref):
        acc = own_ref[...].astype(F32)
        for j in range(npart):
            acc = acc + p_ref[j].astype(F32)
        o_ref[...] = acc

    return pl.pallas_call(
        body, name=name,
        grid_spec=pltpu.PrefetchScalarGridSpec(
            num_scalar_prefetch=1, grid=(nb,),
            in_specs=[pl.BlockSpec((tr, cols), lambda i, c: (i, 0)),
                      pl.BlockSpec((npart, tr, cols), lambda i, c: (0, i, 0))],
            out_specs=pl.BlockSpec((tr, cols), lambda i, c: (c[0] * nb + i, 0))),
        out_shape=jax.ShapeDtypeStruct((2 * h, cols), F32),
        compiler_params=_params("parallel"),
    )(core, own, parts)


def sum_parts(parts, name, first=None):
    npart, rows, cols = parts.shape
    tr = _ew_rows(rows, cols, npart + 2)

    def body(*refs):
        p_ref, o_ref = refs[-2], refs[-1]
        acc = p_ref[0].astype(F32) if first is None else refs[0][...].astype(F32) + p_ref[0].astype(F32)
        for j in range(1, npart):
            acc = acc + p_ref[j].astype(F32)
        o_ref[...] = acc

    row = pl.BlockSpec((tr, cols), lambda i: (i, 0))
    ins = [parts] if first is None else [first, parts]
    return pl.pallas_call(
        body, name=name, grid=(rows // tr,),
        in_specs=([] if first is None else [row]) + [pl.BlockSpec((npart, tr, cols), lambda i: (0, i, 0))],
        out_specs=row,
        out_shape=jax.ShapeDtypeStruct((rows, cols), F32),
        compiler_params=_params("parallel"),
    )(*ins)


ANY = pl.BlockSpec(memory_space=pl.ANY)
MESH = pl.DeviceIdType.MESH


def _me():
    return lax.axis_index("x"), lax.axis_index("y"), lax.axis_index("c")


def _flip(pos, rel):
    return tuple(1 - p if f else p for p, f in zip(pos, rel))


SIBLING = (0, 0, 1)
OTHER_CHIPS = ((1, 0, 0), (0, 1, 0), (1, 1, 0))


def _chip_of(pos, rel=(0, 0, 0)):
    px, py, _ = _flip(pos, rel)
    return 2 * px + py


def allgather_blocks(shards, name):
    nt = len(shards)
    hs = [s.shape[0] // 2 for s in shards]

    def body(*refs):
        ins, outs = refs[:nt], refs[nt:2 * nt]
        send_sems, recv_sems, loc_sems = refs[2 * nt:]
        pos = _me()
        x, y, c = pos

        def block_id(rel):
            px, py, pc = _flip(pos, rel)
            return 4 * px + 2 * py + pc

        def copy(t, k, block_rel, to_rel, src=None):
            dst = outs[t].at[block_id(block_rel)]
            return pltpu.make_async_remote_copy(
                src_ref=dst if src is None else src, dst_ref=dst,
                send_sem=send_sems.at[t * 7 + k], recv_sem=recv_sems.at[t * 7 + k],
                device_id=_flip(pos, to_rel), device_id_type=MESH)

        own = [ins[t].at[pl.ds(c * hs[t], hs[t])] for t in range(nt)]
        mine = [pltpu.make_async_copy(own[t], outs[t].at[block_id((0, 0, 0))], loc_sems.at[t]) for t in range(nt)]
        for cp in mine:
            cp.start()
        first = []
        for t in range(nt):
            first.append(copy(t, 0, (0, 0, 0), SIBLING, src=own[t]))
            first += [copy(t, 1 + j, (0, 0, 0), rel, src=own[t]) for j, rel in enumerate(OTHER_CHIPS)]
        for cp in first:
            cp.start()
        passed = []
        for j, rel in enumerate(OTHER_CHIPS):
            for t in range(nt):
                copy(t, 1 + j, rel, (0, 0, 0)).wait_recv()
                fwd = copy(t, 4 + j, rel, SIBLING)
                fwd.start()
                passed.append(fwd)
        for t in range(nt):
            copy(t, 0, SIBLING, (0, 0, 0)).wait_recv()
            for j, rel in enumerate(OTHER_CHIPS):
                copy(t, 4 + j, (rel[0], rel[1], 1), (0, 0, 0)).wait_recv()
        for cp in first + passed:
            cp.wait_send()
        for cp in mine:
            cp.wait()

    return pl.pallas_call(
        body, name=name,
        in_specs=[ANY] * nt, out_specs=[ANY] * nt,
        out_shape=[jax.ShapeDtypeStruct((N_DEV, h, s.shape[1]), s.dtype) for h, s in zip(hs, shards)],
        scratch_shapes=[pltpu.SemaphoreType.DMA((7 * nt,)), pltpu.SemaphoreType.DMA((7 * nt,)),
                        pltpu.SemaphoreType.DMA((nt,))],
    )(*shards)


def _block_id(pos, rel=(0, 0, 0)):
    px, py, pc = _flip(pos, rel)
    return 4 * px + 2 * py + pc


def gather_first_hop(shards):
    hs = [s.shape[0] // 2 for s in shards]

    def plan(ins, outs, pos):
        me = _block_id(pos)
        remote = []
        for i, o, h in zip(ins, outs, hs):
            own = i.at[pl.ds(pos[2] * h, h)]
            remote += [(rel, own, o.at[me]) for rel in (SIBLING,) + OTHER_CHIPS]
        return remote

    return Comm(shards, [((N_DEV, h, s.shape[1]), s.dtype) for h, s in zip(hs, shards)], plan, 4 * len(shards))


def gather_second_hop(gathered):
    def plan(ins, outs, pos):
        remote = []
        for i, o in zip(ins, outs):
            for rel in OTHER_CHIPS:
                blk = _block_id(pos, rel)
                remote.append((SIBLING, i.at[blk], o.at[blk]))
        return remote

    return Comm(gathered, [(g.shape, g.dtype) for g in gathered], plan, 3 * len(gathered),
                aliases={i: i for i in range(len(gathered))})


def join_comm(bufs):
    def plan(ins, outs, pos):
        remote = []
        for i, o in zip(ins, outs):
            h = i.shape[0] // 2
            rows = pl.ds(pl.multiple_of(pos[2] * h, SUBLANES), h)
            remote.append((SIBLING, i.at[rows], o.at[rows]))
        return remote

    return Comm(list(bufs), [(b.shape, b.dtype) for b in bufs], plan, len(bufs),
                aliases={i: i for i in range(len(bufs))})


def give_half_comm(ts, plain=()):
    nt = len(ts)

    def plan(ins, outs, pos):
        remote = []
        for i, o in zip(ins[:nt], outs[:nt]):
            h = o.shape[1]
            remote.append((SIBLING, i.at[:, pl.ds((1 - pos[2]) * h, h)], o))
        return remote + [(SIBLING, i, o) for i, o in zip(ins[nt:], outs[nt:])]

    shapes = [((t.shape[0], t.shape[1] // 2, t.shape[2]), t.dtype) for t in ts] + [(v.shape, v.dtype) for v in plain]
    return Comm(list(ts) + list(plain), shapes, plan, nt + len(plain))


def chip_scatter_comm(xs, shared=None):
    nx = len(xs)

    def plan(ins, outs, pos):
        me = _chip_of(pos)
        remote = []
        for i, o in zip(ins[:nx], outs[:nx]):
            remote += [(rel, i.at[_chip_of(pos, rel)], o.at[j]) for j, rel in enumerate(OTHER_CHIPS)]
        if shared is not None:
            remote += [(rel, ins[nx], outs[nx].at[me]) for rel in OTHER_CHIPS]
        return remote

    shapes = [((3,) + v.shape[1:], v.dtype) for v in xs]
    if shared is not None:
        shapes.append(((N_CHIPS,) + shared.shape, shared.dtype))
    return Comm(list(xs) + ([] if shared is None else [shared]), shapes, plan, 3 * nx + (0 if shared is None else 3))


def tail_reduce(t, small, name):
    nslab, h2, cols = t.shape
    h = h2 // 2
    rows = small.shape[0]

    def body(t_ref, small_ref, full_ref, ssum_ref,
             got_pair, sums, got_chips, small_got, small_pair, small_chips, send_sems, recv_sems):
        pos = _me()
        core = pos[2]
        me = _chip_of(pos)

        def copy(i, rel, src, dst):
            return pltpu.make_async_remote_copy(src_ref=src, dst_ref=dst, send_sem=send_sems.at[i],
                                                recv_sem=recv_sems.at[i], device_id=_flip(pos, rel),
                                                device_id_type=MESH)

        pair = [copy(0, SIBLING, t_ref.at[:, pl.ds(pl.multiple_of((1 - core) * h, SUBLANES), h)], got_pair),
                copy(1, SIBLING, small_ref, small_got)]
        for cp in pair:
            cp.start()
        for cp in pair:
            cp.wait()
        for k in range(nslab):
            sums[k] = (t_ref[k, pl.ds(pl.multiple_of(core * h, SUBLANES), h), :] + got_pair[k]).astype(BF16)
        small_pair[...] = small_ref[...] + small_got[...]

        chips = []
        for j, rel in enumerate(OTHER_CHIPS):
            chips.append(copy(2 + j, rel, sums.at[_chip_of(pos, rel)], got_chips.at[j]))
            chips.append(copy(5 + j, rel, small_pair, small_chips.at[me]))
        for cp in chips:
            cp.start()
        small_chips[me] = small_pair[...]
        for cp in chips:
            cp.wait()
        acc = sums[me].astype(F32)
        for j in range(len(OTHER_CHIPS)):
            acc = acc + got_chips[j].astype(F32)
        mine = full_ref.at[pl.ds(pl.multiple_of(core * h, SUBLANES), h)]
        mine[...] = acc
        tot = small_chips[0]
        for k in range(1, N_CHIPS):
            tot = tot + small_chips[k]
        ssum_ref[...] = tot

        join = copy(8, SIBLING, mine, mine)
        join.start()
        join.wait()

    vmem = pl.BlockSpec(memory_space=pltpu.VMEM)
    return pl.pallas_call(
        body, name=name, in_specs=[vmem, vmem], out_specs=[vmem, vmem],
        out_shape=[jax.ShapeDtypeStruct((h2, cols), F32), jax.ShapeDtypeStruct((rows, LANES), F32)],
        scratch_shapes=[pltpu.VMEM((nslab, h, cols), F32), pltpu.VMEM((nslab, h, cols), BF16),
                        pltpu.VMEM((3, h, cols), BF16), pltpu.VMEM((rows, LANES), F32),
                        pltpu.VMEM((rows, LANES), F32), pltpu.VMEM((N_CHIPS, rows, LANES), F32),
                        pltpu.SemaphoreType.DMA((9,)), pltpu.SemaphoreType.DMA((9,))],
        compiler_params=pltpu.CompilerParams(vmem_limit_bytes=VMEM_LIMIT),
    )(t, small)


PACK_ROWS = 256


def _pack(arrs):
    parts, layout = [], []
    row = 0
    for a in arrs:
        flat = a.reshape(-1).astype(F32)
        size = flat.shape[0]
        rows = -(-size // (8 * LANES)) * 8
        flat = jnp.pad(flat, (0, rows * LANES - size))
        parts.append(flat.reshape(rows, LANES))
        layout.append((row, rows, size, a.shape))
        row += rows
    if row % PACK_ROWS:
        parts.append(jnp.zeros((PACK_ROWS - row % PACK_ROWS, LANES), F32))
    return jnp.concatenate(parts, axis=0), layout


def _unpack(packed, layout):
    return [packed[r0:r0 + rows].reshape(-1)[:size].reshape(shape) for r0, rows, size, shape in layout]


SMALL_REPL = ['mix_norm', 'a_b_in', 'a_sinks', 'a_conv_b', 'a_cln_g', 'a_cln_b', 'c_w_pool', 'c_w_s', 'c_b_s',
              'ffn_norm', 'final_norm']
SMALL_SHARD = ['a_conv_w', 'c_pool_scale', 'c_sln_g', 'c_sln_b']
BIG = ['a_w_in', 'a_w_out', 'c_w_in', 'c_w_out', 'ffn_w_gate', 'ffn_w_up', 'ffn_w_down']
TRANSPOSED = ('a_w_in', 'ffn_w_gate', 'ffn_w_up')
BIG_COL_SHARDED = {'c_w_in'}


def _full_weight(name, g8):
    _, h, cols = g8.shape
    g4 = g8.reshape(N_CHIPS, 2 * h, cols)
    if name not in BIG_COL_SHARDED:
        return g4.reshape(-1, cols)
    return jnp.transpose(g4, (1, 0, 2)).reshape(2 * h, N_CHIPS * cols)


def _to_shard_major(name, f):
    if name not in BIG_COL_SHARDED:
        return f.reshape(N_CHIPS, f.shape[0] // N_CHIPS, f.shape[1])
    r, cfull = f.shape
    return jnp.transpose(f.reshape(r, N_CHIPS, cfull // N_CHIPS), (1, 0, 2))


def kernel(*args):
    a = dict(zip(IN_NAMES, args))
    bl, seq, _ = a['x'].shape
    n = bl * seq
    x = a['x'].reshape(n, D_MODEL)
    target = a['loss_target'].reshape(n, D_MODEL)
    xi, yi, ci = _me()
    chip = 2 * xi + yi

    shard = {'a_w_in': a['a_w_in'][0].T, 'a_w_out': a['a_w_out'][0], 'c_w_in': a['c_w_in'][0], 'c_w_out': a['c_w_out'][0]}
    for layer in range(2):
        shard['gate' + str(layer)] = a['ffn_w_gate'][layer].T
        shard['up' + str(layer)] = a['ffn_w_up'][layer].T
        shard['down' + str(layer)] = a['ffn_w_down'][layer]
    shard = {k: v.astype(BF16) for k, v in shard.items()}
    core = ci.astype(jnp.int32).reshape(1)
    block_id = 4 * xi + 2 * yi + ci

    def first_hop(*names):
        return gather_first_hop([shard[k] for k in names])

    def finish(name, g8):
        h = shard[name].shape[0] // 2
        own = lax.dynamic_slice_in_dim(shard[name], ci * h, h, axis=0)
        return _full_weight(name, lax.dynamic_update_slice_in_dim(g8, own[None], block_id, axis=0))

    a_w_in_t = _full_weight('a_w_in', allgather_blocks([shard['a_w_in']], "gather_a_w_in")[0])
    in0_width = a_w_in_t.shape[0]
    small_shard_pack, small_shard_layout = _pack([a[k] for k in SMALL_SHARD])
    hop_a = first_hop('a_w_out', 'c_w_out')
    hop_s = chip_scatter_comm([], shared=small_shard_pack)
    mix_norm, ffn_norm = a['mix_norm'], a['ffn_norm']
    (hn0, q, kv, cc), outs = norm_inproj(
        x, mix_norm[0:1], a_w_in_t, a['a_b_in'],
        [(0, ATTN_WIDTH), (ATTN_WIDTH, ATTN_WIDTH + 2 * KV_WIDTH), (ATTN_WIDTH + 2 * KV_WIDTH, in0_width)],
        [BF16, BF16, F32], "in_proj0", comm=hop_a + hop_s, w_transposed=True)
    got_a, (ss,) = hop_a.split(outs, hop_s)
    ss = lax.dynamic_update_slice_in_dim(ss, small_shard_pack[None], chip, axis=0)
    ss_full = []
    for r0, rows, size, shape in small_shard_layout:
        per_chip = ss[:, r0:r0 + rows].reshape(N_CHIPS, -1)[:, :size].reshape((N_CHIPS,) + shape)
        ss_full.append(jnp.concatenate([per_chip[k] for k in range(N_CHIPS)], axis=-1))
    a_conv_w, c_pool_scale, c_sln_g, c_sln_b = [v[0] for v in ss_full]

    conv_taps = jnp.pad(a_conv_w, ((0, 32 - CONV_KERNEL), (0, 0)))
    conv_vec = jnp.pad(jnp.stack([a['a_conv_b'][0], a['a_cln_g'][0], a['a_cln_b'][0]]), ((0, 5), (0, 0)))
    sinks_b = jnp.pad(jnp.repeat(a['a_sinks'][0].reshape(N_KV_HEADS, GROUP), ATTN_BLOCK, axis=1), ((0, 6), (0, 0)))
    w_pool_bf = a['c_w_pool'][0].astype(BF16)
    pool_scale = c_pool_scale.reshape(1, POOL_WIDTH)
    w_s = a['c_w_s'][0]
    b_s_t = a['c_b_s'][0].T
    sgu_ln = jnp.pad(jnp.stack([c_sln_g, c_sln_b]), ((0, 6), (0, 0)))
    final_norm = a['final_norm'].reshape(1, D_MODEL)

    hop_b, pass_a = first_hop('gate0', 'c_w_in'), gather_second_hop(got_a)
    attn, outs = attn_fwd(q, kv, sinks_b, bl, seq, "attn_fwd", comm=hop_b + pass_a)
    got_b, done = hop_b.split(outs, pass_a)
    a_w_out, c_w_out = finish('a_w_out', done[0]), finish('c_w_out', done[1])

    hop_c, pass_b = first_hop('up0', 'down0'), gather_second_hop(got_b)
    (conv, conv_h1), outs = conv_fwd(cc, conv_taps, conv_vec, bl, seq, "conv_fwd", comm=hop_c + pass_b)
    got_c, done = hop_c.split(outs, pass_b)
    wg0, c_w_in = finish('gate0', done[0]), finish('c_w_in', done[1])

    h1, done = out_proj(x, attn, conv, a_w_out, "out_proj0", comm=gather_second_hop(got_c))
    wu0, wd0 = finish('up0', done[0]), finish('down0', done[1])

    (hnf0, g0, u0), got_e = ffn_gate_up(h1, ffn_norm[0:1], wg0, wu0, "ffn_gate_up0",
                                        comm=first_hop('gate1', 'up1', 'down1'))

    h2, done = ffn_down(h1, g0, u0, wd0, "ffn_down0", comm=gather_second_hop(got_e))
    wg1, wu1, wd1 = finish('gate1', done[0]), finish('up1', done[1]), finish('down1', done[2])
    wg, wu, wd = [wg0, wg1], [wu0, wu1], [wd0, wd1]

    (hn1, zp, zs), _ = norm_inproj(
        h2, mix_norm[1:2], c_w_in, jnp.zeros((1, c_w_in.shape[1]), F32),
        [(0, POOL_WIDTH), (POOL_WIDTH, c_w_in.shape[1])], [F32, F32], "in_proj1")
    pool = pool_fwd(zp, w_pool_bf, pool_scale, bl, seq, "pool_fwd")
    sgu = sgu_fwd(zs, w_s, b_s_t, sgu_ln, bl, seq, "sgu_fwd")
    h3, _ = out_proj(h2, pool, sgu, c_w_out, "out_proj1")
    (hnf1, g1, u1), _ = ffn_gate_up(h3, ffn_norm[1:2], wg1, wu1, "ffn_gate_up1")
    h4, _ = ffn_down(h3, g1, u1, wd1, "ffn_down1")

    dh4, d_final_norm, loss_local = loss_head(h4, final_norm, target, "loss_head")

    grads = {}
    pieces = {}

    def slabs_of(names, fulls):
        return [_to_shard_major(k, fulls[k]) for k in names]

    def pair_sums_of(names, slabs, gots):
        return [pair_sum(t, gt, core, "pair_sum_" + k) for k, t, gt in zip(names, slabs, gots)]

    def chip_sums_of(names, sums, from_chips):
        own = [lax.dynamic_index_in_dim(p, chip, axis=0, keepdims=False) for p in sums]
        return [chip_sum(o, p, core, "chip_sum_" + k) for k, p, o in zip(names, from_chips, own)]

    (dg, du, act), _ = ffn_down_bwd(dh4, g1, u1, wd[1], "ffn_down_bwd1")
    full1 = {'down1': mm_tn(act, dh4, "dw_down1"), 'gate1': mm_tn(dg, hnf1, "dw_gate1"),
             'up1': mm_tn(du, hnf1, "dw_up1")}
    names1 = ['gate1', 'up1', 'down1']
    slabs1 = slabs_of(names1, full1)
    dh3, d_ffn_norm1, got1 = proj_rms_bwd([dg, du], [wg[1], wu[1]], h3, ffn_norm[1:2], dh4, 1, "ffn_up_bwd1",
                                          tm_pref=512, w_transposed=True, comm=give_half_comm(slabs1))
    sums1 = pair_sums_of(names1, slabs1, got1)
    d_pool, d_sgu = out_proj_bwd(dh3, c_w_out, [F32, F32], "out_proj_bwd1")
    full1['c_w_out'] = jnp.concatenate([mm_tn(pool, dh3, "dw_out1_pool"), mm_tn(sgu, dh3, "dw_out1_sgu")], axis=0)
    (dzp, d_w_pool, d_pool_scale), from_gate = pool_bwd(zp, w_pool_bf, pool_scale, d_pool, bl, seq, "pool_bwd",
                                                        comm=chip_scatter_comm(sums1[0:1]))
    (dzs, d_w_s, d_b_s_b, d_sgu_ln), from_up = sgu_bwd(zs, w_s, b_s_t, sgu_ln, d_sgu, bl, seq, "sgu_bwd",
                                                       comm=chip_scatter_comm(sums1[1:2]))
    full1['c_w_in'] = jnp.concatenate([mm_tn(hn1, dzp, "dw_in1_pool"), mm_tn(hn1, dzs, "dw_in1_sgu")], axis=1)
    names1b = ['c_w_out', 'c_w_in']
    slabs1b = slabs_of(names1b, full1)
    heavy_pack, heavy_layout = _pack([d_w_pool[None], d_w_s[None]])
    chips_down, pair1b = chip_scatter_comm(sums1[2:3]), give_half_comm(slabs1b, plain=[heavy_pack])
    dh2, d_mix_norm1, outs = proj_rms_bwd([dzp, dzs], [c_w_in[:, :POOL_WIDTH], c_w_in[:, POOL_WIDTH:]], h2,
                                          mix_norm[1:2], dh3, 1, "in_proj_bwd1", comm=chips_down + pair1b)
    from_down, got1b = chips_down.split(outs, pair1b)
    mine1 = chip_sums_of(names1, sums1, from_gate + from_up + from_down)
    sums1b = pair_sums_of(names1b, slabs1b, got1b[:2])
    heavy_pair = add_cast(heavy_pack[None], got1b[2][None], "pair_sum_heavy", dtype=F32)[0]

    join1, chips1b = join_comm(mine1), chip_scatter_comm(sums1b, shared=heavy_pair)
    (dg, du, act), outs = ffn_down_bwd(dh2, g0, u0, wd[0], "ffn_down_bwd0", comm=join1 + chips1b)
    whole1, from_chips1b = join1.split(outs, chips1b)
    pieces.update(dict(zip(names1, whole1)))
    mine1b = chip_sums_of(names1b, sums1b, from_chips1b[:2])
    heavy_chips = lax.dynamic_update_slice_in_dim(from_chips1b[2], heavy_pair[None], chip, axis=0)
    grads['c_w_pool'], grads['c_w_s'] = _unpack(sum_parts(heavy_chips, "heavy_sum"), heavy_layout)
    full0 = {'down0': mm_tn(act, dh2, "dw_down0"), 'gate0': mm_tn(dg, hnf0, "dw_gate0"),
             'up0': mm_tn(du, hnf0, "dw_up0")}
    names0 = ['gate0', 'up0', 'down0']
    slabs0 = slabs_of(names0, full0)
    join1b, pair0 = join_comm(mine1b), give_half_comm(slabs0)
    dh1, d_ffn_norm0, outs = proj_rms_bwd([dg, du], [wg[0], wu[0]], h1, ffn_norm[0:1], dh2, 1, "ffn_up_bwd0",
                                          tm_pref=512, comm=join1b + pair0, w_transposed=True)
    whole1b, got0 = join1b.split(outs, pair0)
    pieces.update(dict(zip(names1b, whole1b)))
    sums0 = pair_sums_of(names0, slabs0, got0)

    d_attn, d_conv = out_proj_bwd(dh1, a_w_out, [BF16, F32], "out_proj_bwd0")
    full_o = {'a_w_out': jnp.concatenate([mm_tn(attn, dh1, "dw_out0_attn"), mm_tn(conv, dh1, "dw_out0_conv")], axis=0)}
    slabs_o = slabs_of(['a_w_out'], full_o)
    chips0, pair_o = chip_scatter_comm(sums0), give_half_comm(slabs_o)
    (dq, dkv, d_sinks_b), outs = attn_bwd(q, kv, sinks_b, d_attn, bl, seq, "attn_bwd", comm=chips0 + pair_o)
    from_chips0, got_o = chips0.split(outs, pair_o)
    mine0 = chip_sums_of(names0, sums0, from_chips0)
    sums_o = pair_sums_of(['a_w_out'], slabs_o, got_o)
    join0, chips_o = join_comm(mine0), chip_scatter_comm(sums_o)
    (dcc, d_conv_taps, d_conv_vec), outs = conv_bwd(cc, conv_h1, conv_taps, conv_vec, d_conv, bl, seq, "conv_bwd",
                                                    comm=join0 + chips_o)
    whole0, from_chips_o = join0.split(outs, chips_o)
    pieces.update(dict(zip(names0, whole0)))
    mine_o = chip_sums_of(['a_w_out'], sums_o, from_chips_o)
    kq, kk = ATTN_WIDTH, ATTN_WIDTH + 2 * KV_WIDTH
    grad_x, d_mix_norm0, _ = proj_rms_bwd([dq, dkv, dcc], [a_w_in_t[:kq], a_w_in_t[kq:kk], a_w_in_t[kk:]], x,
                                          mix_norm[0:1], dh1, 1, "in_proj_bwd0", w_transposed=True)
    dw_q, db_q = mm_tn(dq, hn0, "dw_in0_q", xsum=True)
    dw_kv, db_kv = mm_tn(dkv, hn0, "dw_in0_kv", xsum=True)
    (dw_c, db_c), whole_o = mm_tn(dcc, hn0, "dw_in0_c", xsum=True, comm=join_comm(mine_o))
    pieces['a_w_out'] = whole_o[0]
    d_a_b_in = jnp.concatenate([db_q, db_kv, db_c], axis=0)
    slabs_i = slabs_of(['a_w_in'], {'a_w_in': jnp.concatenate([dw_q, dw_kv, dw_c], axis=0)})

    small_full = {
        'mix_norm': jnp.stack([d_mix_norm0, d_mix_norm1]), 'a_b_in': d_a_b_in[None], 'a_sinks': d_sinks_b[:, 0][None],
        'a_conv_w': d_conv_taps[:CONV_KERNEL][None], 'a_conv_b': d_conv_vec[0][None], 'a_cln_g': d_conv_vec[1][None],
        'a_cln_b': d_conv_vec[2][None], 'c_pool_scale': d_pool_scale[0][None],
        'c_sln_g': d_sgu_ln[0][None], 'c_sln_b': d_sgu_ln[1][None],
        'c_b_s': d_b_s_b[:, :, 0][None], 'ffn_norm': jnp.stack([d_ffn_norm0, d_ffn_norm1]),
        'final_norm': d_final_norm, 'loss': loss_local.reshape(1)}
    small_names = SMALL_REPL + SMALL_SHARD
    tail_names = [k for k in small_names if k in small_full] + ['loss']
    small_pack, small_layout = _pack([small_full[k] for k in tail_names])

    pieces['a_w_in'], small_sum = tail_reduce(slabs_i[0], small_pack, "tail_reduce")

    for k in ('a_w_in', 'a_w_out', 'c_w_in', 'c_w_out'):
        grads[k] = pieces[k][None]
    for short, key in (('gate', 'ffn_w_gate'), ('up', 'ffn_w_up'), ('down', 'ffn_w_down')):
        grads[key] = jnp.stack([pieces[short + '0'], pieces[short + '1']])

    for k, g in zip(tail_names, _unpack(small_sum, small_layout)):
        if k in SMALL_SHARD:
            width = a[k].shape[-1]
            g = lax.dynamic_slice_in_dim(g, chip * width, width, axis=g.ndim - 1)
        grads[k] = g
    loss = grads.pop('loss')[0]

    delta, new_m, new_v = {}, {}, {}
    for k in BIG:
        if k in TRANSPOSED:
            flip = lambda t: jnp.swapaxes(t, 1, 2)
            d, m, v = adamw(flip(a[k]), grads[k], flip(a['m_' + k]), flip(a['v_' + k]), "adamw_" + k)
            grads[k], delta[k], new_m[k], new_v[k] = flip(grads[k]), flip(d), flip(m), flip(v)
        else:
            delta[k], new_m[k], new_v[k] = adamw(a[k], grads[k], a['m_' + k], a['v_' + k], "adamw_" + k)
    two_d = lambda t: t.reshape(1, -1) if t.ndim == 1 else t
    ds, ms, vs = adamw_many([two_d(a[k]) for k in small_names], [two_d(grads[k]) for k in small_names],
                            [two_d(a['m_' + k]) for k in small_names], [two_d(a['v_' + k]) for k in small_names],
                            "adamw_small")
    for k, dv, mv, vv in zip(small_names, ds, ms, vs):
        delta[k], new_m[k], new_v[k] = [t.reshape(a[k].shape) for t in (dv, mv, vv)]

    return (loss, grad_x.reshape(a['x'].shape), *[grads[k] for k in WEIGHTS], *[delta[k] for k in WEIGHTS],
            *[new_m[k] for k in WEIGHTS], *[new_v[k] for k in WEIGHTS])
```

```python
import functools

import jax
import jax.numpy as jnp
from jax import lax
from jax.experimental import pallas as pl
from jax.experimental.pallas import tpu as pltpu

F32 = jnp.float32
BF16 = jnp.bfloat16

D_MODEL = 1024
EPS = 1e-5
N_Q_HEADS, N_KV_HEADS, HEAD_DIM = 8, 2, 64
ATTN_BLOCK = 128
ATTN_WIDTH = N_Q_HEADS * HEAD_DIM
KV_WIDTH = N_KV_HEADS * HEAD_DIM
CONV_WIDTH = 512
CONV_KERNEL = 31
CONV_HALO = 32
POOL_WINDOWS = (2, 4, 8, 16)
POOL_WIDTH = 512
POOL_HALO = 16
SGU_WIDTH = 512
SGU_CHUNK = 128
D_FF = 2816
FF_CHUNK = 128
MXU_COLS = 256
FFN_AHEAD = 1
LANES = 128
N_CHIPS = 4
N_DEV = 8

ADAM_LR, ADAM_B1, ADAM_B2, ADAM_EPS, ADAM_WD, ADAM_STEP = 0.001, 0.9, 0.999, 1e-08, 0.01, 10

VMEM_LIMIT = 56 * 2**20

WEIGHTS = ['mix_norm', 'a_w_in', 'a_b_in', 'a_sinks', 'a_conv_w', 'a_conv_b', 'a_cln_g', 'a_cln_b', 'a_w_out',
           'c_w_in', 'c_w_pool', 'c_pool_scale', 'c_sln_g', 'c_sln_b', 'c_w_s', 'c_b_s', 'c_w_out',
           'ffn_norm', 'ffn_w_gate', 'ffn_w_up', 'ffn_w_down', 'final_norm']
IN_NAMES = (['x'] + WEIGHTS + ['loss_target'] + ['m_' + n for n in WEIGHTS] + ['v_' + n for n in WEIGHTS])


def _params(*sem):
    return pltpu.CompilerParams(dimension_semantics=sem, vmem_limit_bytes=VMEM_LIMIT)


def _dot(a, b):
    return jnp.dot(a, b, preferred_element_type=F32)


def _dot_nt(a, b):
    return lax.dot_general(a, b, (((1,), (1,)), ((), ())), preferred_element_type=F32)


def _dot_tn(a, b):
    return lax.dot_general(a, b, (((0,), (0,)), ((), ())), preferred_element_type=F32)


def _sigmoid(v):
    return 0.5 * jnp.tanh(0.5 * v) + 0.5


def _row_tile(n, pref):
    t = min(n, pref)
    while n % t:
        t //= 2
    return t


def _col_tile(m, rows, budget=6 * 2**20):
    best = LANES
    for t in range(LANES, m + 1, LANES):
        if m % t == 0 and rows * t * 4 <= budget:
            best = t
    return best


class Comm:
    def __init__(self, ins, out_shapes, plan, count, aliases=None):
        self.ins, self.out_shapes, self.plan, self.count, self.aliases = ins, out_shapes, plan, count, aliases or {}

    def __add__(self, other):
        ni, no = len(self.ins), len(self.out_shapes)

        def plan(ins, outs, pos):
            return self.plan(ins[:ni], outs[:no], pos) + other.plan(ins[ni:], outs[no:], pos)

        aliases = dict(self.aliases)
        aliases.update({ni + i: no + o for i, o in other.aliases.items()})
        return Comm(list(self.ins) + list(other.ins), list(self.out_shapes) + list(other.out_shapes), plan,
                    self.count + other.count, aliases)

    def split(self, outs, other):
        return outs[:len(self.out_shapes)], outs[len(self.out_shapes):]


def _pcall(body, name, grid, in_specs, out_specs, out_shape, scratch_shapes, args, sem, comm=None):
    single = not isinstance(out_shape, (list, tuple))
    if single:
        out_specs, out_shape = [out_specs], [out_shape]
    if comm is None:
        res = pl.pallas_call(body, name=name, grid=grid, in_specs=in_specs, out_specs=list(out_specs),
                             out_shape=list(out_shape), scratch_shapes=list(scratch_shapes),
                             compiler_params=_params(*sem))(*args)
        return (res[0] if single else res), []
    na, nci, no, nco, ns = len(args), len(comm.ins), len(out_shape), len(comm.out_shapes), len(scratch_shapes)

    def wrapped(*refs):
        a_refs, ci_refs = refs[:na], refs[na:na + nci]
        o_refs, co_refs = refs[na + nci:na + nci + no], refs[na + nci + no:na + nci + no + nco]
        s_refs = refs[na + nci + no + nco:na + nci + no + nco + ns]
        send_sems, recv_sems = refs[-2], refs[-1]
        pos = _me()

        def copies():
            return [pltpu.make_async_remote_copy(src_ref=s, dst_ref=d, send_sem=send_sems.at[i],
                                                 recv_sem=recv_sems.at[i], device_id=_flip(pos, rel),
                                                 device_id_type=MESH)
                    for i, (rel, s, d) in enumerate(comm.plan(ci_refs, co_refs, pos))]

        first, last = None, None
        for d, size in enumerate(grid):
            f, l = pl.program_id(d) == 0, pl.program_id(d) == size - 1
            first = f if first is None else first & f
            last = l if last is None else last & l

        @pl.when(first)
        def _():
            for cp in copies():
                cp.start()

        body(*a_refs, *o_refs, *s_refs)

        @pl.when(last)
        def _():
            for cp in copies():
                cp.wait()

    res = pl.pallas_call(
        wrapped, name=name, grid=grid,
        in_specs=list(in_specs) + [ANY] * nci, out_specs=list(out_specs) + [ANY] * nco,
        out_shape=list(out_shape) + [jax.ShapeDtypeStruct(s, d) for s, d in comm.out_shapes],
        scratch_shapes=list(scratch_shapes) + [pltpu.SemaphoreType.DMA((comm.count,)),
                                               pltpu.SemaphoreType.DMA((comm.count,))],
        input_output_aliases={na + i: no + o for i, o in comm.aliases.items()},
        compiler_params=_params(*(["arbitrary"] * len(grid))),
    )(*args, *comm.ins)
    outs = res[:no]
    return (outs[0] if single else outs), list(res[no:])


def norm_inproj(x, gain, w, bias, splits, dtypes, name, comm=None, w_transposed=False):
    n = x.shape[0]
    m = w.shape[0] if w_transposed else w.shape[1]
    tm = _row_tile(n, 1024)

    def body(x_ref, g_ref, w_ref, b_ref, hn_ref, *outs):
        xv = x_ref[...]
        r = lax.rsqrt(jnp.mean(xv * xv, axis=-1, keepdims=True) + EPS)
        hn = ((xv * r) * g_ref[...]).astype(BF16)
        hn_ref[...] = hn
        z = (_dot_nt if w_transposed else _dot)(hn, w_ref[...]) + b_ref[...]
        for o, (lo, hi) in zip(outs, splits):
            o[...] = z[:, lo:hi].astype(o.dtype)

    out_shape = [jax.ShapeDtypeStruct((n, D_MODEL), BF16)]
    out_specs = [pl.BlockSpec((tm, D_MODEL), lambda i: (i, 0))]
    for (lo, hi), dt in zip(splits, dtypes):
        out_shape.append(jax.ShapeDtypeStruct((n, hi - lo), dt))
        out_specs.append(pl.BlockSpec((tm, hi - lo), lambda i: (i, 0)))
    return _pcall(
        body, name, (n // tm,),
        [pl.BlockSpec((tm, D_MODEL), lambda i: (i, 0)),
         pl.BlockSpec((1, D_MODEL), lambda i: (0, 0)),
         pl.BlockSpec(w.shape, lambda i: (0, 0)),
         pl.BlockSpec((1, m), lambda i: (0, 0))],
        out_specs, out_shape, [], (x, gain, w, bias), ("parallel",), comm)


def out_proj(res, m1, m2, w, name, comm=None):
    n = res.shape[0]
    k1, k2 = m1.shape[1], m2.shape[1]
    assert k1 == k2
    tm = _row_tile(n, 1024)

    def body(r_ref, a_ref, b_ref, w1_ref, w2_ref, o_ref):
        o_ref[...] = r_ref[...] + _dot(a_ref[...], w1_ref[...]) + _dot(b_ref[...], w2_ref[...])

    return _pcall(
        body, name, (n // tm,),
        [pl.BlockSpec((tm, D_MODEL), lambda i: (i, 0)),
         pl.BlockSpec((tm, k1), lambda i: (i, 0)),
         pl.BlockSpec((tm, k2), lambda i: (i, 0)),
         pl.BlockSpec((k1, D_MODEL), lambda i: (0, 0)),
         pl.BlockSpec((k2, D_MODEL), lambda i: (1, 0))],
        pl.BlockSpec((tm, D_MODEL), lambda i: (i, 0)),
        jax.ShapeDtypeStruct((n, D_MODEL), F32), [], (res, m1, m2, w, w), ("parallel",), comm)


def ffn_gate_up(h, gain, wg_t, wu_t, name, comm=None):
    n = h.shape[0]
    tm = _row_tile(n, 512)
    th = D_FF

    def body(h_ref, g_ref, wg_ref, wu_ref, hn_ref, go_ref, uo_ref):
        @pl.when(pl.program_id(1) == 0)
        def _():
            xv = h_ref[...]
            r = lax.rsqrt(jnp.mean(xv * xv, axis=-1, keepdims=True) + EPS)
            hn_ref[...] = ((xv * r) * g_ref[...]).astype(BF16)

        hn = hn_ref[...]
        go_ref[...] = _dot_nt(hn, wg_ref[...]).astype(BF16)
        uo_ref[...] = _dot_nt(hn, wu_ref[...]).astype(BF16)

    return _pcall(
        body, name, (n // tm, D_FF // th),
        [pl.BlockSpec((tm, D_MODEL), lambda i, j: (i, 0)),
         pl.BlockSpec((1, D_MODEL), lambda i, j: (0, 0)),
         pl.BlockSpec((th, D_MODEL), lambda i, j: (j, 0), pipeline_mode=pl.Buffered(1)),
         pl.BlockSpec((th, D_MODEL), lambda i, j: (j, 0), pipeline_mode=pl.Buffered(1))],
        [pl.BlockSpec((tm, D_MODEL), lambda i, j: (i, 0)),
         pl.BlockSpec((tm, th), lambda i, j: (i, j)),
         pl.BlockSpec((tm, th), lambda i, j: (i, j))],
        [jax.ShapeDtypeStruct((n, D_MODEL), BF16),
         jax.ShapeDtypeStruct((n, D_FF), BF16),
         jax.ShapeDtypeStruct((n, D_FF), BF16)],
        [], (h, gain, wg_t, wu_t), ("parallel", "arbitrary"), comm)


def ffn_down(h, g, u, wd, name, comm=None):
    n = h.shape[0]
    tm = _row_tile(n, 1024)

    def body(h_ref, g_ref, u_ref, w_ref, o_ref, a_ref):
        for c0 in range(0, D_FF, FF_CHUNK):
            gv = g_ref[:, c0:c0 + FF_CHUNK]
            a_ref[:, c0:c0 + FF_CHUNK] = gv * _sigmoid(gv) * u_ref[:, c0:c0 + FF_CHUNK]
        o_ref[...] = h_ref[...] + _dot(a_ref[...], w_ref[...])

    return _pcall(
        body, name, (n // tm,),
        [pl.BlockSpec((tm, D_MODEL), lambda i: (i, 0)),
         pl.BlockSpec((tm, D_FF), lambda i: (i, 0)),
         pl.BlockSpec((tm, D_FF), lambda i: (i, 0)),
         pl.BlockSpec((D_FF, D_MODEL), lambda i: (0, 0), pipeline_mode=pl.Buffered(1))],
        pl.BlockSpec((tm, D_MODEL), lambda i: (i, 0)),
        jax.ShapeDtypeStruct((n, D_MODEL), F32),
        [pltpu.VMEM((tm, D_FF), BF16)], (h, g, u, wd), ("parallel",), comm)


def ffn_down_bwd(dh, g, u, wd, name, comm=None):
    n = dh.shape[0]
    tm = _row_tile(n, 512)

    def body(dh_ref, g_ref, u_ref, w_ref, dg_ref, du_ref, a_ref):
        dhb = dh_ref[...].astype(BF16)
        chunks = [slice(c0, c0 + MXU_COLS) for c0 in range(0, D_FF, MXU_COLS)]
        ahead = [_dot_nt(dhb, w_ref[c, :]) for c in chunks[:FFN_AHEAD]]
        for i, cols in enumerate(chunks):
            da = ahead.pop(0).astype(BF16)
            if i + FFN_AHEAD < len(chunks):
                ahead.append(_dot_nt(dhb, w_ref[chunks[i + FFN_AHEAD], :]))
            gv, uv = g_ref[:, cols], u_ref[:, cols]
            sg = _sigmoid(gv)
            act = gv * sg
            dg_ref[:, cols] = (da * uv) * (sg + act * (1.0 - sg))
            du_ref[:, cols] = da * act
            a_ref[:, cols] = act * uv

    spec_h = pl.BlockSpec((tm, D_FF), lambda i: (i, 0))
    return _pcall(
        body, name, (n // tm,),
        [pl.BlockSpec((tm, D_MODEL), lambda i: (i, 0)), spec_h, spec_h,
         pl.BlockSpec((D_FF, D_MODEL), lambda i: (0, 0))],
        [spec_h, spec_h, spec_h], [jax.ShapeDtypeStruct((n, D_FF), BF16)] * 3,
        [], (dh, g, u, wd), ("parallel",), comm)


def mm_tn(x, dy, name, xsum=False, comm=None):
    n, k = x.shape
    m = dy.shape[1]
    tk = _col_tile(k, m)
    tt = _row_tile(n, 2048)

    def body(x_ref, dy_ref, o_ref, *rest):
        xt_ref = rest[-1]
        t = pl.program_id(1)
        xv = x_ref[...]
        xt_ref[...] = xv.astype(BF16).T
        part = _dot(xt_ref[...], dy_ref[...].astype(BF16))

        @pl.when(t == 0)
        def _():
            o_ref[...] = part

        @pl.when(t > 0)
        def _():
            o_ref[...] += part

        if xsum:
            cs = jnp.broadcast_to(jnp.sum(xv.astype(F32), axis=0, keepdims=True), rest[0].shape)

            @pl.when(t == 0)
            def _():
                rest[0][...] = cs

            @pl.when(t > 0)
            def _():
                rest[0][...] += cs

    out_shape = [jax.ShapeDtypeStruct((k, m), F32)]
    out_specs = [pl.BlockSpec((tk, m), lambda j, t: (j, 0))]
    if xsum:
        out_shape.append(jax.ShapeDtypeStruct((8, k), F32))
        out_specs.append(pl.BlockSpec((8, tk), lambda j, t: (0, j)))
    res, comm_outs = _pcall(
        body, name, (k // tk, n // tt),
        [pl.BlockSpec((tt, tk), lambda j, t: (t, j)),
         pl.BlockSpec((tt, m), lambda j, t: (t, 0))],
        out_specs, out_shape, [pltpu.VMEM((tk, tt), BF16)], (x, dy), ("arbitrary", "arbitrary"), comm)
    res = (res[0], res[1][0]) if xsum else res[0]
    return res if comm is None else (res, comm_outs)


def out_proj_bwd(dh, w, dtypes, name):
    n = dh.shape[0]
    k = w.shape[0]
    half = k // 2
    tm = _row_tile(n, 1024)

    def body(dh_ref, w_ref, a_ref, b_ref):
        dm = _dot_nt(dh_ref[...].astype(BF16), w_ref[...])
        a_ref[...] = dm[:, :half].astype(a_ref.dtype)
        b_ref[...] = dm[:, half:].astype(b_ref.dtype)

    return pl.pallas_call(
        body, name=name, grid=(n // tm,),
        in_specs=[pl.BlockSpec((tm, D_MODEL), lambda i: (i, 0)),
                  pl.BlockSpec((k, D_MODEL), lambda i: (0, 0))],
        out_specs=[pl.BlockSpec((tm, half), lambda i: (i, 0))] * 2,
        out_shape=[jax.ShapeDtypeStruct((n, half), dtypes[0]), jax.ShapeDtypeStruct((n, half), dtypes[1])],
        compiler_params=_params("parallel"),
    )(dh, w)


def proj_rms_bwd(dys, ws, h_in, gain, dres, nk, name, tm_pref=512, comm=None, w_transposed=False):
    n = h_in.shape[0]
    npair = len(dys)
    tm = _row_tile(n, tm_pref)
    tks = [dy.shape[1] // nk for dy in dys]
    mm = _dot if w_transposed else _dot_nt

    def body(*refs):
        dy_refs = refs[:npair]
        w_refs = refs[npair:2 * npair]
        h_ref, g_ref, dr_ref, o_ref, dg_ref, acc_ref = refs[2 * npair:]
        i, k = pl.program_id(0), pl.program_id(1)
        part = mm(dy_refs[0][...], w_refs[0][...])
        for p in range(1, npair):
            part = part + mm(dy_refs[p][...], w_refs[p][...])

        @pl.when(k == 0)
        def _():
            acc_ref[...] = part

        @pl.when(k > 0)
        def _():
            acc_ref[...] += part

        @pl.when(k == nk - 1)
        def _():
            dhn = acc_ref[...]
            xv = h_ref[...]
            r = lax.rsqrt(jnp.mean(xv * xv, axis=-1, keepdims=True) + EPS)
            xh = xv * r
            uv = dhn * g_ref[...]
            o_ref[...] = dr_ref[...] + r * (uv - xh * jnp.mean(uv * xh, axis=-1, keepdims=True))
            dgp = jnp.broadcast_to(jnp.sum(dhn * xh, axis=0, keepdims=True), dg_ref.shape)

            @pl.when(i == 0)
            def _():
                dg_ref[...] = dgp

            @pl.when(i > 0)
            def _():
                dg_ref[...] += dgp

    row = pl.BlockSpec((tm, D_MODEL), lambda i, k: (i, 0))
    in_specs = [pl.BlockSpec((tm, tk), lambda i, k: (i, k)) for tk in tks]
    once = dict(pipeline_mode=pl.Buffered(1)) if nk == 1 else {}
    if w_transposed:
        in_specs += [pl.BlockSpec((tk, D_MODEL), lambda i, k: (k, 0), **once) for tk in tks]
    else:
        in_specs += [pl.BlockSpec((D_MODEL, tk), lambda i, k: (0, k), **once) for tk in tks]
    in_specs += [row, pl.BlockSpec((1, D_MODEL), lambda i, k: (0, 0)), row]
    (dh, dgain), comm_outs = _pcall(
        body, name, (n // tm, nk), in_specs,
        [row, pl.BlockSpec((8, D_MODEL), lambda i, k: (0, 0))],
        [jax.ShapeDtypeStruct((n, D_MODEL), F32), jax.ShapeDtypeStruct((8, D_MODEL), F32)],
        [pltpu.VMEM((tm, D_MODEL), F32)], (*dys, *ws, h_in, gain, dres), ("arbitrary", "arbitrary"), comm)
    return dh, dgain[0], comm_outs


def loss_head(h, gain, target, name):
    n = h.shape[0]
    tm = _row_tile(n, 512)

    def body(h_ref, g_ref, t_ref, dh_ref, dg_ref, l_ref):
        i = pl.program_id(0)
        xv = h_ref[...]
        r = lax.rsqrt(jnp.mean(xv * xv, axis=-1, keepdims=True) + EPS)
        xh = xv * r
        err = xh * g_ref[...] - t_ref[...]
        dy = err * (1.0 / D_MODEL)
        uv = dy * g_ref[...]
        dh_ref[...] = r * (uv - xh * jnp.mean(uv * xh, axis=-1, keepdims=True))
        dgp = jnp.broadcast_to(jnp.sum(dy * xh, axis=0, keepdims=True), dg_ref.shape)
        lp = jnp.sum(jnp.sum(err * err, axis=-1, keepdims=True), axis=0, keepdims=True) * (0.5 / D_MODEL)
        lp = jnp.broadcast_to(lp, l_ref.shape)

        @pl.when(i == 0)
        def _():
            dg_ref[...] = dgp
            l_ref[...] = lp

        @pl.when(i > 0)
        def _():
            dg_ref[...] += dgp
            l_ref[...] += lp

    row = pl.BlockSpec((tm, D_MODEL), lambda i: (i, 0))
    dh, dg, l = pl.pallas_call(
        body, name=name, grid=(n // tm,),
        in_specs=[row, pl.BlockSpec((1, D_MODEL), lambda i: (0, 0)), row],
        out_specs=[row, pl.BlockSpec((8, D_MODEL), lambda i: (0, 0)), pl.BlockSpec((8, LANES), lambda i: (0, 0))],
        out_shape=[jax.ShapeDtypeStruct((n, D_MODEL), F32), jax.ShapeDtypeStruct((8, D_MODEL), F32),
                   jax.ShapeDtypeStruct((8, LANES), F32)],
        compiler_params=_params("arbitrary"),
    )(h, gain, target)
    return dh, dg[0], l[0, 0]


GROUP = N_Q_HEADS // N_KV_HEADS
GQ = GROUP * ATTN_BLOCK


def _attn_mask_t(n):
    r = lax.broadcasted_iota(jnp.int32, (2 * ATTN_BLOCK, GQ), 0)
    qi = lax.broadcasted_iota(jnp.int32, (2 * ATTN_BLOCK, GQ), 1) & (ATTN_BLOCK - 1)
    band = (r > qi) & (r <= qi + ATTN_BLOCK)
    return band & ((r >= ATTN_BLOCK) | (n > 0))


def _stack_heads(blk, kh):
    return jnp.concatenate([blk[:, (kh * GROUP + g) * HEAD_DIM:(kh * GROUP + g + 1) * HEAD_DIM]
                            for g in range(GROUP)], axis=0)


def _attn_probs_t(kk, qs, mask, sink):
    s = _dot_nt(kk, qs) * (HEAD_DIM ** -0.5)
    s = jnp.where(mask, s, -1e30)
    m = jnp.maximum(jnp.max(s, axis=0, keepdims=True), sink)
    p = jnp.exp(s - m)
    esink = jnp.exp(sink - m)
    inv = 1.0 / (jnp.sum(p, axis=0, keepdims=True) + esink)
    return p * inv, esink * inv


def attn_fwd(q, kv, sinks_t, nseq, seq, name, comm=None):
    nb = seq // ATTN_BLOCK

    def body(q_ref, kv_ref, s_ref, o_ref, kvp):
        kvp[0:ATTN_BLOCK, :] = jnp.zeros((ATTN_BLOCK, 2 * KV_WIDTH), BF16)
        kvp[ATTN_BLOCK:, :] = kv_ref[...]

        def blk(n, carry):
            st = pl.multiple_of(n * ATTN_BLOCK, ATTN_BLOCK)
            qb = q_ref[pl.ds(st, ATTN_BLOCK), :]
            kw = kvp[pl.ds(st, 2 * ATTN_BLOCK), :]
            mask = _attn_mask_t(n)
            for kh in range(N_KV_HEADS):
                kk = kw[:, kh * HEAD_DIM:(kh + 1) * HEAD_DIM]
                vv = kw[:, KV_WIDTH + kh * HEAD_DIM:KV_WIDTH + (kh + 1) * HEAD_DIM]
                probs, _ = _attn_probs_t(kk, _stack_heads(qb, kh), mask, s_ref[kh:kh + 1, :])
                ot = _dot_tn(vv, probs.astype(BF16))
                for pair in range(GROUP // 2):
                    two = jnp.concatenate([ot[:, (2 * pair) * ATTN_BLOCK:(2 * pair + 1) * ATTN_BLOCK],
                                           ot[:, (2 * pair + 1) * ATTN_BLOCK:(2 * pair + 2) * ATTN_BLOCK]], axis=0)
                    col = (kh * GROUP + 2 * pair) * HEAD_DIM
                    o_ref[pl.ds(st, ATTN_BLOCK), col:col + 2 * HEAD_DIM] = two.T.astype(o_ref.dtype)
            return carry

        lax.fori_loop(0, nb, blk, 0, unroll=4)

    return _pcall(
        body, name, (nseq,),
        [pl.BlockSpec((seq, ATTN_WIDTH), lambda b: (b, 0)),
         pl.BlockSpec((seq, 2 * KV_WIDTH), lambda b: (b, 0)),
         pl.BlockSpec((8, GQ), lambda b: (0, 0))],
        pl.BlockSpec((seq, ATTN_WIDTH), lambda b: (b, 0)),
        jax.ShapeDtypeStruct((nseq * seq, ATTN_WIDTH), BF16),
        [pltpu.VMEM((ATTN_BLOCK + seq, 2 * KV_WIDTH), BF16)], (q, kv, sinks_t), ("parallel",), comm)


def attn_bwd(q, kv, sinks_t, do, nseq, seq, name, comm=None):
    nb = seq // ATTN_BLOCK

    def body(q_ref, kv_ref, s_ref, do_ref, dq_ref, dkv_ref, ds_ref, kvp, dkvp, dsacc):
        @pl.when(pl.program_id(0) == 0)
        def _():
            dsacc[...] = jnp.zeros(dsacc.shape, F32)

        kvp[0:ATTN_BLOCK, :] = jnp.zeros((ATTN_BLOCK, 2 * KV_WIDTH), BF16)
        kvp[ATTN_BLOCK:, :] = kv_ref[...]
        dkvp[...] = jnp.zeros(dkvp.shape, F32)

        def blk(n, carry):
            st = pl.multiple_of(n * ATTN_BLOCK, ATTN_BLOCK)
            qb = q_ref[pl.ds(st, ATTN_BLOCK), :]
            dob = do_ref[pl.ds(st, ATTN_BLOCK), :]
            kw = kvp[pl.ds(st, 2 * ATTN_BLOCK), :]
            mask = _attn_mask_t(n)
            for kh in range(N_KV_HEADS):
                kk = kw[:, kh * HEAD_DIM:(kh + 1) * HEAD_DIM]
                vv = kw[:, KV_WIDTH + kh * HEAD_DIM:KV_WIDTH + (kh + 1) * HEAD_DIM]
                qs = _stack_heads(qb, kh)
                dos = _stack_heads(dob, kh)
                probs, psink = _attn_probs_t(kk, qs, mask, s_ref[kh:kh + 1, :])
                dp = _dot_nt(vv, dos)
                dv = _dot(probs.astype(BF16), dos)
                rowdot = jnp.sum(probs * dp, axis=0, keepdims=True)
                dsc = (probs * (dp - rowdot) * (HEAD_DIM ** -0.5)).astype(BF16)
                dsacc[kh:kh + 1, :] += -psink * rowdot
                dk = _dot(dsc, qs)
                dqs = _dot_tn(dsc, kk)
                for g in range(GROUP):
                    col = (kh * GROUP + g) * HEAD_DIM
                    dq_ref[pl.ds(st, ATTN_BLOCK), col:col + HEAD_DIM] = (
                        dqs[g * ATTN_BLOCK:(g + 1) * ATTN_BLOCK].astype(dq_ref.dtype))
                dkvp[pl.ds(st, 2 * ATTN_BLOCK), kh * HEAD_DIM:(kh + 1) * HEAD_DIM] += dk
                dkvp[pl.ds(st, 2 * ATTN_BLOCK), KV_WIDTH + kh * HEAD_DIM:KV_WIDTH + (kh + 1) * HEAD_DIM] += dv
            return carry

        lax.fori_loop(0, nb, blk, 0, unroll=2)
        dkv_ref[...] = dkvp[ATTN_BLOCK:, :].astype(dkv_ref.dtype)

        @pl.when(pl.program_id(0) == nseq - 1)
        def _():
            for kh in range(N_KV_HEADS):
                for g in range(GROUP):
                    tot = jnp.sum(dsacc[kh:kh + 1, g * ATTN_BLOCK:(g + 1) * ATTN_BLOCK], axis=1, keepdims=True)
                    ds_ref[kh * GROUP + g:kh * GROUP + g + 1, :] = jnp.broadcast_to(tot, (1, LANES))

    seq_q = pl.BlockSpec((seq, ATTN_WIDTH), lambda b: (b, 0))
    seq_kv = pl.BlockSpec((seq, 2 * KV_WIDTH), lambda b: (b, 0))
    return _pcall(
        body, name, (nseq,),
        [seq_q, seq_kv, pl.BlockSpec((8, GQ), lambda b: (0, 0)), seq_q],
        [seq_q, seq_kv, pl.BlockSpec((N_Q_HEADS, LANES), lambda b: (0, 0))],
        [jax.ShapeDtypeStruct((nseq * seq, ATTN_WIDTH), BF16),
         jax.ShapeDtypeStruct((nseq * seq, 2 * KV_WIDTH), BF16),
         jax.ShapeDtypeStruct((N_Q_HEADS, LANES), F32)],
        [pltpu.VMEM((ATTN_BLOCK + seq, 2 * KV_WIDTH), BF16),
         pltpu.VMEM((ATTN_BLOCK + seq, 2 * KV_WIDTH), F32),
         pltpu.VMEM((8, GQ), F32)], (q, kv, sinks_t, do), ("arbitrary",), comm)


CONV_T = 128


SUBLANES = 8


def _shifted_rows(win):
    phases = [win] + [pltpu.roll(win, s, 0) for s in range(1, SUBLANES)]

    def shifted(s):
        lo = CONV_HALO - SUBLANES * (s // SUBLANES)
        return phases[s % SUBLANES][lo:lo + CONV_T]

    return shifted


def _conv_taps(win, w_ref, lanes, init):
    shifted = _shifted_rows(win)
    acc = init
    for j in range(CONV_KERNEL):
        acc = acc + w_ref[j:j + 1, lanes] * shifted(CONV_KERNEL - 1 - j)
    return acc


def _conv_block(h0p, w_ref, vec_ref, st):
    cols = []
    for cs in range(CONV_WIDTH // LANES):
        lanes = slice(cs * LANES, (cs + 1) * LANES)
        win = h0p[pl.ds(st, CONV_T + CONV_HALO), lanes]
        init = jnp.broadcast_to(vec_ref[0:1, lanes], (CONV_T, LANES))
        cols.append(_conv_taps(win, w_ref, lanes, init))
    return jnp.concatenate(cols, axis=-1)


def _glu_store(c_ref, h0p, st):
    cb = c_ref[pl.ds(st, CONV_T), :]
    h0p[pl.ds(pl.multiple_of(st + CONV_HALO, CONV_HALO), CONV_T), :] = cb[:, :CONV_WIDTH] * _sigmoid(cb[:, CONV_WIDTH:])


def conv_fwd(c, w, vec, nseq, seq, name, comm=None):
    nb = seq // CONV_T

    def body(c_ref, w_ref, vec_ref, o_ref, h1_ref, h0p):
        h0p[0:CONV_HALO, :] = jnp.zeros((CONV_HALO, CONV_WIDTH), F32)

        def blk(n, carry):
            st = pl.multiple_of(n * CONV_T, CONV_T)
            _glu_store(c_ref, h0p, st)
            h1 = _conv_block(h0p, w_ref, vec_ref, st)
            h1_ref[pl.ds(st, CONV_T), :] = h1
            mu = jnp.mean(h1, axis=-1, keepdims=True)
            xc = h1 - mu
            rstd = lax.rsqrt(jnp.mean(xc * xc, axis=-1, keepdims=True) + EPS)
            y = xc * rstd * vec_ref[1:2, :] + vec_ref[2:3, :]
            o_ref[pl.ds(st, CONV_T), :] = (y * _sigmoid(y)).astype(o_ref.dtype)
            return carry

        lax.fori_loop(0, nb, blk, 0)

    return _pcall(
        body, name, (nseq,),
        [pl.BlockSpec((seq, 2 * CONV_WIDTH), lambda b: (b, 0)),
         pl.BlockSpec((32, CONV_WIDTH), lambda b: (0, 0)),
         pl.BlockSpec((8, CONV_WIDTH), lambda b: (0, 0))],
        [pl.BlockSpec((seq, CONV_WIDTH), lambda b: (b, 0))] * 2,
        [jax.ShapeDtypeStruct((nseq * seq, CONV_WIDTH), BF16), jax.ShapeDtypeStruct((nseq * seq, CONV_WIDTH), F32)],
        [pltpu.VMEM((CONV_HALO + seq, CONV_WIDTH), F32)], (c, w, vec), ("parallel",), comm)


def conv_bwd(c, h1_saved, w, vec, dout, nseq, seq, name, comm=None):
    nb = seq // CONV_T

    def body(c_ref, h1_ref, w_ref, vec_ref, do_ref, dc_ref, dw_ref, dvec_ref, h0p, dh1p, dwacc):
        @pl.when(pl.program_id(0) == 0)
        def _():
            dwacc[...] = jnp.zeros(dwacc.shape, F32)
            dvec_ref[...] = jnp.zeros(dvec_ref.shape, F32)

        h0p[0:CONV_HALO, :] = jnp.zeros((CONV_HALO, CONV_WIDTH), F32)
        dh1p[seq:seq + CONV_HALO, :] = jnp.zeros((CONV_HALO, CONV_WIDTH), F32)

        def pass_a(n, carry):
            st = pl.multiple_of(n * CONV_T, CONV_T)
            _glu_store(c_ref, h0p, st)
            h1 = h1_ref[pl.ds(st, CONV_T), :]
            mu = jnp.mean(h1, axis=-1, keepdims=True)
            xc = h1 - mu
            rstd = lax.rsqrt(jnp.mean(xc * xc, axis=-1, keepdims=True) + EPS)
            xh = xc * rstd
            y = xh * vec_ref[1:2, :] + vec_ref[2:3, :]
            sg = _sigmoid(y)
            dy = do_ref[pl.ds(st, CONV_T), :] * (sg * (1.0 + y * (1.0 - sg)))
            dvec_ref[1:2, :] += jnp.sum(dy * xh, axis=0, keepdims=True)
            dvec_ref[2:3, :] += jnp.sum(dy, axis=0, keepdims=True)
            dxh = dy * vec_ref[1:2, :]
            dh1 = rstd * (dxh - jnp.mean(dxh, axis=-1, keepdims=True)
                          - xh * jnp.mean(dxh * xh, axis=-1, keepdims=True))
            dvec_ref[0:1, :] += jnp.sum(dh1, axis=0, keepdims=True)
            dh1p[pl.ds(st, CONV_T), :] = dh1
            return carry

        lax.fori_loop(0, nb, pass_a, 0)

        def pass_b(n, carry):
            st = pl.multiple_of(n * CONV_T, CONV_T)
            cols = []
            for cs in range(CONV_WIDTH // LANES):
                lanes = slice(cs * LANES, (cs + 1) * LANES)
                wind = dh1p[pl.ds(st, CONV_T + CONV_HALO), lanes]
                winh = h0p[pl.ds(st, CONV_T + CONV_HALO), lanes]
                d1 = wind[0:CONV_T]
                shifted_d, shifted_h = _shifted_rows(wind), _shifted_rows(winh)
                acc = jnp.zeros((CONV_T, LANES), F32)
                for j in range(CONV_KERNEL):
                    acc = acc + w_ref[j:j + 1, lanes] * shifted_d(2 + j)
                    prod = d1 * shifted_h(CONV_KERNEL - 1 - j)
                    part = prod[0:8]
                    for r in range(8, CONV_T, 8):
                        part = part + prod[r:r + 8]
                    dwacc[8 * j:8 * j + 8, lanes] += part
                cols.append(acc)
            dh0 = jnp.concatenate(cols, axis=-1)
            cb = c_ref[pl.ds(st, CONV_T), :]
            av, gt = cb[:, :CONV_WIDTH], cb[:, CONV_WIDTH:]
            sg = _sigmoid(gt)
            dc_ref[pl.ds(st, CONV_T), :] = jnp.concatenate(
                [dh0 * sg, dh0 * av * sg * (1.0 - sg)], axis=-1).astype(dc_ref.dtype)
            return carry

        lax.fori_loop(0, nb, pass_b, 0)

        @pl.when(pl.program_id(0) == nseq - 1)
        def _():
            dw_ref[...] = jnp.zeros(dw_ref.shape, F32)
            for j in range(CONV_KERNEL):
                dw_ref[j:j + 1, :] = jnp.sum(dwacc[8 * j:8 * j + 8, :], axis=0, keepdims=True)

    return _pcall(
        body, name, (nseq,),
        [pl.BlockSpec((seq, 2 * CONV_WIDTH), lambda b: (b, 0)),
         pl.BlockSpec((seq, CONV_WIDTH), lambda b: (b, 0)),
         pl.BlockSpec((32, CONV_WIDTH), lambda b: (0, 0)),
         pl.BlockSpec((8, CONV_WIDTH), lambda b: (0, 0)),
         pl.BlockSpec((seq, CONV_WIDTH), lambda b: (b, 0))],
        [pl.BlockSpec((seq, 2 * CONV_WIDTH), lambda b: (b, 0)),
         pl.BlockSpec((32, CONV_WIDTH), lambda b: (0, 0)),
         pl.BlockSpec((8, CONV_WIDTH), lambda b: (0, 0))],
        [jax.ShapeDtypeStruct((nseq * seq, 2 * CONV_WIDTH), BF16),
         jax.ShapeDtypeStruct((32, CONV_WIDTH), F32),
         jax.ShapeDtypeStruct((8, CONV_WIDTH), F32)],
        [pltpu.VMEM((CONV_HALO + seq, CONV_WIDTH), F32),
         pltpu.VMEM((seq + CONV_HALO, CONV_WIDTH), F32),
         pltpu.VMEM((8 * 32, CONV_WIDTH), F32)], (c, h1_saved, w, vec, dout), ("arbitrary",), comm)


POOL_T = 128


def _pooled_block(zpp, st, grp):
    lanes = slice(grp * LANES, (grp + 1) * LANES)
    win = zpp[pl.ds(st, POOL_T + POOL_HALO), lanes]
    acc = win
    for lvl in range(grp + 1):
        acc = acc + pltpu.roll(acc, 1 << lvl, 0)
    t = st + lax.broadcasted_iota(jnp.int32, (POOL_T, 1), 0)
    inv = 1.0 / jnp.minimum(t + 1, POOL_WINDOWS[grp]).astype(F32)
    return acc[POOL_HALO:] * inv - win[POOL_HALO:], inv


def pool_fwd(zp, wp, scale, nseq, seq, name):
    nb = seq // POOL_T

    def body(z_ref, wp_ref, sc_ref, o_ref, zpp):
        zpp[0:POOL_HALO, :] = jnp.zeros((POOL_HALO, POOL_WIDTH), F32)
        zpp[POOL_HALO:, :] = z_ref[...]

        def blk(n, carry):
            st = pl.multiple_of(n * POOL_T, POOL_T)
            for grp in range(len(POOL_WINDOWS)):
                lanes = slice(grp * LANES, (grp + 1) * LANES)
                pooled, _ = _pooled_block(zpp, st, grp)
                o_ref[pl.ds(st, POOL_T), lanes] = (
                    _dot(pooled.astype(BF16), wp_ref[grp]) * sc_ref[0:1, lanes]).astype(o_ref.dtype)
            return carry

        lax.fori_loop(0, nb, blk, 0)

    return pl.pallas_call(
        body, name=name, grid=(nseq,),
        in_specs=[pl.BlockSpec((seq, POOL_WIDTH), lambda b: (b, 0)),
                  pl.BlockSpec((4, LANES, LANES), lambda b: (0, 0, 0)),
                  pl.BlockSpec((1, POOL_WIDTH), lambda b: (0, 0))],
        out_specs=pl.BlockSpec((seq, POOL_WIDTH), lambda b: (b, 0)),
        out_shape=jax.ShapeDtypeStruct((nseq * seq, POOL_WIDTH), BF16),
        scratch_shapes=[pltpu.VMEM((POOL_HALO + seq, POOL_WIDTH), F32)],
        compiler_params=_params("parallel"),
    )(zp, wp, scale)


def pool_bwd(zp, wp, scale, dout, nseq, seq, name, comm=None):
    nb = seq // POOL_T

    def body(z_ref, wp_ref, sc_ref, do_ref, dz_ref, dwp_ref, dsc_ref, zpp, dpcp, negd):
        @pl.when(pl.program_id(0) == 0)
        def _():
            dwp_ref[...] = jnp.zeros(dwp_ref.shape, F32)
            dsc_ref[...] = jnp.zeros(dsc_ref.shape, F32)

        zpp[0:POOL_HALO, :] = jnp.zeros((POOL_HALO, POOL_WIDTH), F32)
        zpp[POOL_HALO:, :] = z_ref[...]
        dpcp[seq:seq + POOL_HALO, :] = jnp.zeros((POOL_HALO, POOL_WIDTH), F32)

        def pass_a(n, carry):
            st = pl.multiple_of(n * POOL_T, POOL_T)
            for grp in range(len(POOL_WINDOWS)):
                lanes = slice(grp * LANES, (grp + 1) * LANES)
                pooled, inv = _pooled_block(zpp, st, grp)
                pb = pooled.astype(BF16)
                dob = do_ref[pl.ds(st, POOL_T), lanes]
                dsc_ref[0:1, lanes] += jnp.sum(dob * _dot(pb, wp_ref[grp]), axis=0, keepdims=True)
                dpm = (dob * sc_ref[0:1, lanes]).astype(BF16)
                dwp_ref[grp] += _dot_tn(pb, dpm)
                dpooled = _dot_nt(dpm, wp_ref[grp])
                negd[pl.ds(st, POOL_T), lanes] = -dpooled
                dpcp[pl.ds(st, POOL_T), lanes] = dpooled * inv
            return carry

        lax.fori_loop(0, nb, pass_a, 0)

        def pass_b(n, carry):
            st = pl.multiple_of(n * POOL_T, POOL_T)
            rows = POOL_T + POOL_HALO
            for grp in range(len(POOL_WINDOWS)):
                lanes = slice(grp * LANES, (grp + 1) * LANES)
                acc = dpcp[pl.ds(st, rows), lanes]
                for lvl in range(grp + 1):
                    acc = acc + pltpu.roll(acc, rows - (1 << lvl), 0)
                dz_ref[pl.ds(st, POOL_T), lanes] = (acc[0:POOL_T] + negd[pl.ds(st, POOL_T), lanes]).astype(dz_ref.dtype)
            return carry

        lax.fori_loop(0, nb, pass_b, 0)

    seq_spec = pl.BlockSpec((seq, POOL_WIDTH), lambda b: (b, 0))
    return _pcall(
        body, name, (nseq,),
        [seq_spec, pl.BlockSpec((4, LANES, LANES), lambda b: (0, 0, 0)),
         pl.BlockSpec((1, POOL_WIDTH), lambda b: (0, 0)), seq_spec],
        [seq_spec, pl.BlockSpec((4, LANES, LANES), lambda b: (0, 0, 0)),
         pl.BlockSpec((8, POOL_WIDTH), lambda b: (0, 0))],
        [jax.ShapeDtypeStruct((nseq * seq, POOL_WIDTH), BF16),
         jax.ShapeDtypeStruct((4, LANES, LANES), F32),
         jax.ShapeDtypeStruct((8, POOL_WIDTH), F32)],
        [pltpu.VMEM((POOL_HALO + seq, POOL_WIDTH), F32),
         pltpu.VMEM((seq + POOL_HALO, POOL_WIDTH), F32),
         pltpu.VMEM((seq, POOL_WIDTH), F32)], (zp, wp, scale, dout), ("arbitrary",), comm)


GELU_C0 = 0.7978845608028654
GELU_C1 = 0.044715


def _gelu(xv):
    return xv * (0.5 * (1.0 + jnp.tanh(GELU_C0 * (xv + GELU_C1 * (xv * xv * xv)))))


def _gelu_grad(xv):
    t = jnp.tanh(GELU_C0 * (xv + GELU_C1 * (xv * xv * xv)))
    return 0.5 * (1.0 + t) + 0.5 * xv * (1.0 - t * t) * (GELU_C0 * (1.0 + 3.0 * GELU_C1 * xv * xv))


def _tril():
    ti = lax.broadcasted_iota(jnp.int32, (SGU_CHUNK, SGU_CHUNK), 0)
    si = lax.broadcasted_iota(jnp.int32, (SGU_CHUNK, SGU_CHUNK), 1)
    return si <= ti


def sgu_fwd(zs, ws, bst, ln, nseq, seq, name):
    nc = seq // SGU_CHUNK

    def body(z_ref, ws_ref, bs_ref, ln_ref, o_ref):
        tril = _tril()

        def blk(n, carry):
            st = pl.multiple_of(n * SGU_CHUNK, SGU_CHUNK)
            ge = _gelu(z_ref[pl.ds(st, SGU_CHUNK), :])
            uu, vv = ge[:, :SGU_WIDTH], ge[:, SGU_WIDTH:]
            mu = jnp.mean(vv, axis=-1, keepdims=True)
            xc = vv - mu
            rstd = lax.rsqrt(jnp.mean(xc * xc, axis=-1, keepdims=True) + EPS)
            vn = (xc * rstd * ln_ref[0:1, :] + ln_ref[1:2, :]).astype(BF16)
            for g in range(4):
                lanes = slice(g * LANES, (g + 1) * LANES)
                wm = jnp.where(tril, ws_ref[g], 0.0).astype(BF16)
                mixed = _dot(wm, vn[:, lanes]) + bs_ref[:, g:g + 1]
                o_ref[pl.ds(st, SGU_CHUNK), lanes] = (uu[:, lanes] * mixed).astype(o_ref.dtype)
            return carry

        lax.fori_loop(0, nc, blk, 0)

    return pl.pallas_call(
        body, name=name, grid=(nseq,),
        in_specs=[pl.BlockSpec((seq, 2 * SGU_WIDTH), lambda b: (b, 0)),
                  pl.BlockSpec((4, LANES, LANES), lambda b: (0, 0, 0)),
                  pl.BlockSpec((SGU_CHUNK, 4), lambda b: (0, 0)),
                  pl.BlockSpec((8, SGU_WIDTH), lambda b: (0, 0))],
        out_specs=pl.BlockSpec((seq, SGU_WIDTH), lambda b: (b, 0)),
        out_shape=jax.ShapeDtypeStruct((nseq * seq, SGU_WIDTH), BF16),
        compiler_params=_params("parallel"),
    )(zs, ws, bst, ln)


def sgu_bwd(zs, ws, bst, ln, dout, nseq, seq, name, comm=None):
    nc = seq // SGU_CHUNK

    def body(z_ref, ws_ref, bs_ref, ln_ref, do_ref, dz_ref, dws_ref, dbs_ref, dln_ref):
        @pl.when(pl.program_id(0) == 0)
        def _():
            dws_ref[...] = jnp.zeros(dws_ref.shape, F32)
            dbs_ref[...] = jnp.zeros(dbs_ref.shape, F32)
            dln_ref[...] = jnp.zeros(dln_ref.shape, F32)

        tril = _tril()

        def blk(n, carry):
            st = pl.multiple_of(n * SGU_CHUNK, SGU_CHUNK)
            zv = z_ref[pl.ds(st, SGU_CHUNK), :]
            ge = _gelu(zv)
            uu, vv = ge[:, :SGU_WIDTH], ge[:, SGU_WIDTH:]
            mu = jnp.mean(vv, axis=-1, keepdims=True)
            xc = vv - mu
            rstd = lax.rsqrt(jnp.mean(xc * xc, axis=-1, keepdims=True) + EPS)
            xh = xc * rstd
            vn = (xh * ln_ref[0:1, :] + ln_ref[1:2, :]).astype(BF16)
            dob = do_ref[pl.ds(st, SGU_CHUNK), :]
            du_cols, dvn_cols = [], []
            for g in range(4):
                lanes = slice(g * LANES, (g + 1) * LANES)
                wm = jnp.where(tril, ws_ref[g], 0.0).astype(BF16)
                mixed = _dot(wm, vn[:, lanes]) + bs_ref[:, g:g + 1]
                du_cols.append(dob[:, lanes] * mixed)
                dmix = dob[:, lanes] * uu[:, lanes]
                dbs_ref[g] += jnp.broadcast_to(jnp.sum(dmix, axis=-1, keepdims=True), (SGU_CHUNK, LANES))
                dmb = dmix.astype(BF16)
                dws_ref[g] += jnp.where(tril, _dot_nt(dmb, vn[:, lanes]), 0.0)
                dvn_cols.append(_dot_tn(wm, dmb))
            dvn = jnp.concatenate(dvn_cols, axis=-1)
            dln_ref[0:1, :] += jnp.sum(dvn * xh, axis=0, keepdims=True)
            dln_ref[1:2, :] += jnp.sum(dvn, axis=0, keepdims=True)
            dxh = dvn * ln_ref[0:1, :]
            dv = rstd * (dxh - jnp.mean(dxh, axis=-1, keepdims=True)
                         - xh * jnp.mean(dxh * xh, axis=-1, keepdims=True))
            dge = jnp.concatenate(du_cols + [dv], axis=-1)
            dz_ref[pl.ds(st, SGU_CHUNK), :] = (dge * _gelu_grad(zv)).astype(dz_ref.dtype)
            return carry

        lax.fori_loop(0, nc, blk, 0)

    w_spec = pl.BlockSpec((4, LANES, LANES), lambda b: (0, 0, 0))
    ln_spec = pl.BlockSpec((8, SGU_WIDTH), lambda b: (0, 0))
    return _pcall(
        body, name, (nseq,),
        [pl.BlockSpec((seq, 2 * SGU_WIDTH), lambda b: (b, 0)), w_spec,
         pl.BlockSpec((SGU_CHUNK, 4), lambda b: (0, 0)), ln_spec,
         pl.BlockSpec((seq, SGU_WIDTH), lambda b: (b, 0))],
        [pl.BlockSpec((seq, 2 * SGU_WIDTH), lambda b: (b, 0)), w_spec, w_spec, ln_spec],
        [jax.ShapeDtypeStruct((nseq * seq, 2 * SGU_WIDTH), BF16),
         jax.ShapeDtypeStruct((4, LANES, LANES), F32),
         jax.ShapeDtypeStruct((4, LANES, LANES), F32),
         jax.ShapeDtypeStruct((8, SGU_WIDTH), F32)],
        [], (zs, ws, bst, ln, dout), ("arbitrary",), comm)


def _ew_rows(rows, cols, nbuf):
    t = _row_tile(rows, 1024)
    while t > 8 and t * cols * 4 * nbuf * 2 > 24 * 2**20:
        t //= 2
    return t


def adamw(w, g, m, v, name):
    layers, rows, cols = w.shape
    tr = _ew_rows(rows, cols, 7)

    def body(w_ref, g_ref, m_ref, v_ref, d_ref, mo_ref, vo_ref):
        gv = g_ref[...]
        mn = ADAM_B1 * m_ref[...] + (1.0 - ADAM_B1) * gv
        vn = ADAM_B2 * v_ref[...] + (1.0 - ADAM_B2) * (gv * gv)
        m_hat = mn / (1.0 - ADAM_B1 ** ADAM_STEP)
        v_hat = vn / (1.0 - ADAM_B2 ** ADAM_STEP)
        d_ref[...] = -ADAM_LR * (m_hat / (jnp.sqrt(v_hat) + ADAM_EPS) + ADAM_WD * w_ref[...])
        mo_ref[...] = mn
        vo_ref[...] = vn

    spec = pl.BlockSpec((1, tr, cols), lambda l, i: (l, i, 0))
    return pl.pallas_call(
        body, name=name, grid=(layers, rows // tr),
        in_specs=[spec] * 4, out_specs=[spec] * 3,
        out_shape=[jax.ShapeDtypeStruct(w.shape, F32)] * 3,
        compiler_params=_params("parallel", "parallel"),
    )(w, g, m, v)


def adamw_many(ws, gs, ms, vs, name):
    n = len(ws)

    def body(*refs):
        w_refs, g_refs, m_refs, v_refs = refs[:n], refs[n:2 * n], refs[2 * n:3 * n], refs[3 * n:4 * n]
        d_refs, mo_refs, vo_refs = refs[4 * n:5 * n], refs[5 * n:6 * n], refs[6 * n:7 * n]
        for i in range(n):
            gv = g_refs[i][...]
            mn = ADAM_B1 * m_refs[i][...] + (1.0 - ADAM_B1) * gv
            vn = ADAM_B2 * v_refs[i][...] + (1.0 - ADAM_B2) * (gv * gv)
            m_hat = mn / (1.0 - ADAM_B1 ** ADAM_STEP)
            v_hat = vn / (1.0 - ADAM_B2 ** ADAM_STEP)
            d_refs[i][...] = -ADAM_LR * (m_hat / (jnp.sqrt(v_hat) + ADAM_EPS) + ADAM_WD * w_refs[i][...])
            mo_refs[i][...] = mn
            vo_refs[i][...] = vn

    vmem = pl.BlockSpec(memory_space=pltpu.VMEM)
    shapes = [jax.ShapeDtypeStruct(w.shape, F32) for w in ws]
    res = pl.pallas_call(
        body, name=name, in_specs=[vmem] * (4 * n), out_specs=[vmem] * (3 * n), out_shape=shapes * 3,
        compiler_params=pltpu.CompilerParams(vmem_limit_bytes=VMEM_LIMIT),
    )(*ws, *gs, *ms, *vs)
    return res[:n], res[n:2 * n], res[2 * n:]


def add_cast(a, b, name, dtype=BF16):
    nslab, rows, cols = a.shape
    tr = _ew_rows(rows, cols, 3)

    def body(a_ref, b_ref, o_ref):
        o_ref[...] = (a_ref[...] + b_ref[...]).astype(dtype)

    spec = pl.BlockSpec((1, tr, cols), lambda k, i: (k, i, 0))
    return pl.pallas_call(
        body, name=name, grid=(nslab, rows // tr),
        in_specs=[spec, spec], out_specs=spec,
        out_shape=jax.ShapeDtypeStruct(a.shape, dtype),
        compiler_params=_params("parallel", "parallel"),
    )(a, b)


def pair_sum(t, got, core, name):
    nslab, h, cols = got.shape
    tr = _ew_rows(h, cols, 3)
    nb = h // tr

    def body(c_ref, a_ref, b_ref, o_ref):
        o_ref[...] = (a_ref[...] + b_ref[...]).astype(BF16)

    spec = pl.BlockSpec((1, tr, cols), lambda k, i, c: (k, i, 0))
    return pl.pallas_call(
        body, name=name,
        grid_spec=pltpu.PrefetchScalarGridSpec(
            num_scalar_prefetch=1, grid=(nslab, nb),
            in_specs=[pl.BlockSpec((1, tr, cols), lambda k, i, c: (k, c[0] * nb + i, 0)), spec],
            out_specs=spec),
        out_shape=jax.ShapeDtypeStruct(got.shape, BF16),
        compiler_params=_params("parallel", "parallel"),
    )(core, t, got)


def chip_sum(sums, parts, place, name):
    npart, h, cols = parts.shape
    tr = _ew_rows(h, cols, npart + 2)
    nb = h // tr

    def body(c_ref, own_ref, p_ref, o_ref):
        acc = own_ref[0].astype(F32)
        for j in range(npart):
            acc = acc + p_ref[j].astype(F32)
        o_ref[...] = acc

    return pl.pallas_call(
        body, name=name,
        grid_spec=pltpu.PrefetchScalarGridSpec(
            num_scalar_prefetch=1, grid=(nb,),
            in_specs=[pl.BlockSpec((1, tr, cols), lambda i, c: (c[1], i, 0)),
                      pl.BlockSpec((npart, tr, cols), lambda i, c: (0, i, 0))],
            out_specs=pl.BlockSpec((tr, cols), lambda i, c: (c[0] * nb + i, 0))),
        out_shape=jax.ShapeDtypeStruct((2 * h, cols), F32),
        compiler_params=_params("parallel"),
    )(place, sums, parts)


def sum_parts(parts, name, first=None):
    npart, rows, cols = parts.shape
    tr = _ew_rows(rows, cols, npart + 2)

    def body(*refs):
        p_ref, o_ref = refs[-2], refs[-1]
        acc = p_ref[0].astype(F32) if first is None else refs[0][...].astype(F32) + p_ref[0].astype(F32)
        for j in range(1, npart):
            acc = acc + p_ref[j].astype(F32)
        o_ref[...] = acc

    row = pl.BlockSpec((tr, cols), lambda i: (i, 0))
    ins = [parts] if first is None else [first, parts]
    return pl.pallas_call(
        body, name=name, grid=(rows // tr,),
        in_specs=([] if first is None else [row]) + [pl.BlockSpec((npart, tr, cols), lambda i: (0, i, 0))],
        out_specs=row,
        out_shape=jax.ShapeDtypeStruct((rows, cols), F32),
        compiler_params=_params("parallel"),
    )(*ins)


ANY = pl.BlockSpec(memory_space=pl.ANY)
MESH = pl.DeviceIdType.MESH


def _me():
    return lax.axis_index("x"), lax.axis_index("y"), lax.axis_index("c")


def _flip(pos, rel):
    return tuple(1 - p if f else p for p, f in zip(pos, rel))


SIBLING = (0, 0, 1)
OTHER_CHIPS = ((1, 0, 0), (0, 1, 0), (1, 1, 0))


def _chip_of(pos, rel=(0, 0, 0)):
    px, py, _ = _flip(pos, rel)
    return 2 * px + py


def allgather_blocks(shards, name):
    nt = len(shards)
    hs = [s.shape[0] // 2 for s in shards]

    def body(*refs):
        ins, outs = refs[:nt], refs[nt:2 * nt]
        send_sems, recv_sems, loc_sems = refs[2 * nt:]
        pos = _me()
        x, y, c = pos

        def block_id(rel):
            px, py, pc = _flip(pos, rel)
            return 4 * px + 2 * py + pc

        def copy(t, k, block_rel, to_rel, src=None):
            dst = outs[t].at[block_id(block_rel)]
            return pltpu.make_async_remote_copy(
                src_ref=dst if src is None else src, dst_ref=dst,
                send_sem=send_sems.at[t * 7 + k], recv_sem=recv_sems.at[t * 7 + k],
                device_id=_flip(pos, to_rel), device_id_type=MESH)

        own = [ins[t].at[pl.ds(c * hs[t], hs[t])] for t in range(nt)]
        mine = [pltpu.make_async_copy(own[t], outs[t].at[block_id((0, 0, 0))], loc_sems.at[t]) for t in range(nt)]
        for cp in mine:
            cp.start()
        first = []
        for t in range(nt):
            first.append(copy(t, 0, (0, 0, 0), SIBLING, src=own[t]))
            first += [copy(t, 1 + j, (0, 0, 0), rel, src=own[t]) for j, rel in enumerate(OTHER_CHIPS)]
        for cp in first:
            cp.start()
        passed = []
        for j, rel in enumerate(OTHER_CHIPS):
            for t in range(nt):
                copy(t, 1 + j, rel, (0, 0, 0)).wait_recv()
                fwd = copy(t, 4 + j, rel, SIBLING)
                fwd.start()
                passed.append(fwd)
        for t in range(nt):
            copy(t, 0, SIBLING, (0, 0, 0)).wait_recv()
            for j, rel in enumerate(OTHER_CHIPS):
                copy(t, 4 + j, (rel[0], rel[1], 1), (0, 0, 0)).wait_recv()
        for cp in first + passed:
            cp.wait_send()
        for cp in mine:
            cp.wait()

    return pl.pallas_call(
        body, name=name,
        in_specs=[ANY] * nt, out_specs=[ANY] * nt,
        out_shape=[jax.ShapeDtypeStruct((N_DEV, h, s.shape[1]), s.dtype) for h, s in zip(hs, shards)],
        scratch_shapes=[pltpu.SemaphoreType.DMA((7 * nt,)), pltpu.SemaphoreType.DMA((7 * nt,)),
                        pltpu.SemaphoreType.DMA((nt,))],
    )(*shards)


def _block_id(pos, rel=(0, 0, 0)):
    px, py, pc = _flip(pos, rel)
    return 4 * px + 2 * py + pc


def gather_first_hop(shards):
    hs = [s.shape[0] // 2 for s in shards]

    def plan(ins, outs, pos):
        me = _block_id(pos)
        remote = []
        for i, o, h in zip(ins, outs, hs):
            own = i.at[pl.ds(pos[2] * h, h)]
            remote += [(rel, own, o.at[me]) for rel in (SIBLING,) + OTHER_CHIPS]
        return remote

    return Comm(shards, [((N_DEV, h, s.shape[1]), s.dtype) for h, s in zip(hs, shards)], plan, 4 * len(shards))


def gather_second_hop(gathered):
    def plan(ins, outs, pos):
        remote = []
        for i, o in zip(ins, outs):
            for rel in OTHER_CHIPS:
                blk = _block_id(pos, rel)
                remote.append((SIBLING, i.at[blk], o.at[blk]))
        return remote

    return Comm(gathered, [(g.shape, g.dtype) for g in gathered], plan, 3 * len(gathered),
                aliases={i: i for i in range(len(gathered))})


def join_comm(bufs):
    def plan(ins, outs, pos):
        remote = []
        for i, o in zip(ins, outs):
            h = i.shape[0] // 2
            rows = pl.ds(pl.multiple_of(pos[2] * h, SUBLANES), h)
            remote.append((SIBLING, i.at[rows], o.at[rows]))
        return remote

    return Comm(list(bufs), [(b.shape, b.dtype) for b in bufs], plan, len(bufs),
                aliases={i: i for i in range(len(bufs))})


def give_half_comm(ts, plain=()):
    nt = len(ts)

    def plan(ins, outs, pos):
        remote = []
        for i, o in zip(ins[:nt], outs[:nt]):
            h = o.shape[1]
            remote.append((SIBLING, i.at[:, pl.ds((1 - pos[2]) * h, h)], o))
        return remote + [(SIBLING, i, o) for i, o in zip(ins[nt:], outs[nt:])]

    shapes = [((t.shape[0], t.shape[1] // 2, t.shape[2]), t.dtype) for t in ts] + [(v.shape, v.dtype) for v in plain]
    return Comm(list(ts) + list(plain), shapes, plan, nt + len(plain))


def chip_scatter_comm(xs, shared=None):
    nx = len(xs)

    def plan(ins, outs, pos):
        me = _chip_of(pos)
        remote = []
        for i, o in zip(ins[:nx], outs[:nx]):
            remote += [(rel, i.at[_chip_of(pos, rel)], o.at[j]) for j, rel in enumerate(OTHER_CHIPS)]
        if shared is not None:
            remote += [(rel, ins[nx], outs[nx].at[me]) for rel in OTHER_CHIPS]
        return remote

    shapes = [((3,) + v.shape[1:], v.dtype) for v in xs]
    if shared is not None:
        shapes.append(((N_CHIPS,) + shared.shape, shared.dtype))
    return Comm(list(xs) + ([] if shared is None else [shared]), shapes, plan, 3 * nx + (0 if shared is None else 3))


def tail_reduce(t, small, name):
    nslab, h2, cols = t.shape
    h = h2 // 2
    rows = small.shape[0]

    def body(t_ref, small_ref, full_ref, ssum_ref,
             got_pair, sums, got_chips, small_got, small_pair, small_chips, send_sems, recv_sems):
        pos = _me()
        core = pos[2]
        me = _chip_of(pos)

        def copy(i, rel, src, dst):
            return pltpu.make_async_remote_copy(src_ref=src, dst_ref=dst, send_sem=send_sems.at[i],
                                                recv_sem=recv_sems.at[i], device_id=_flip(pos, rel),
                                                device_id_type=MESH)

        pair = [copy(0, SIBLING, t_ref.at[:, pl.ds(pl.multiple_of((1 - core) * h, SUBLANES), h)], got_pair),
                copy(1, SIBLING, small_ref, small_got)]
        for cp in pair:
            cp.start()
        for cp in pair:
            cp.wait()
        for k in range(nslab):
            sums[k] = (t_ref[k, pl.ds(pl.multiple_of(core * h, SUBLANES), h), :] + got_pair[k]).astype(BF16)
        small_pair[...] = small_ref[...] + small_got[...]

        chips = []
        for j, rel in enumerate(OTHER_CHIPS):
            chips.append(copy(2 + j, rel, sums.at[_chip_of(pos, rel)], got_chips.at[j]))
            chips.append(copy(5 + j, rel, small_pair, small_chips.at[me]))
        for cp in chips:
            cp.start()
        small_chips[me] = small_pair[...]
        for cp in chips:
            cp.wait()
        acc = sums[me].astype(F32)
        for j in range(len(OTHER_CHIPS)):
            acc = acc + got_chips[j].astype(F32)
        mine = full_ref.at[pl.ds(pl.multiple_of(core * h, SUBLANES), h)]
        mine[...] = acc
        tot = small_chips[0]
        for k in range(1, N_CHIPS):
            tot = tot + small_chips[k]
        ssum_ref[...] = tot

        join = copy(8, SIBLING, mine, mine)
        join.start()
        join.wait()

    vmem = pl.BlockSpec(memory_space=pltpu.VMEM)
    return pl.pallas_call(
        body, name=name, in_specs=[vmem, vmem], out_specs=[vmem, vmem],
        out_shape=[jax.ShapeDtypeStruct((h2, cols), F32), jax.ShapeDtypeStruct((rows, LANES), F32)],
        scratch_shapes=[pltpu.VMEM((nslab, h, cols), F32), pltpu.VMEM((nslab, h, cols), BF16),
                        pltpu.VMEM((3, h, cols), BF16), pltpu.VMEM((rows, LANES), F32),
                        pltpu.VMEM((rows, LANES), F32), pltpu.VMEM((N_CHIPS, rows, LANES), F32),
                        pltpu.SemaphoreType.DMA((9,)), pltpu.SemaphoreType.DMA((9,))],
        compiler_params=pltpu.CompilerParams(vmem_limit_bytes=VMEM_LIMIT),
    )(t, small)


PACK_ROWS = 256


def _pack(arrs):
    parts, layout = [], []
    row = 0
    for a in arrs:
        flat = a.reshape(-1).astype(F32)
        size = flat.shape[0]
        rows = -(-size // (8 * LANES)) * 8
        flat = jnp.pad(flat, (0, rows * LANES - size))
        parts.append(flat.reshape(rows, LANES))
        layout.append((row, rows, size, a.shape))
        row += rows
    if row % PACK_ROWS:
        parts.append(jnp.zeros((PACK_ROWS - row % PACK_ROWS, LANES), F32))
    return jnp.concatenate(parts, axis=0), layout


def _unpack(packed, layout):
    return [packed[r0:r0 + rows].reshape(-1)[:size].reshape(shape) for r0, rows, size, shape in layout]


SMALL_REPL = ['mix_norm', 'a_b_in', 'a_sinks', 'a_conv_b', 'a_cln_g', 'a_cln_b', 'c_w_pool', 'c_w_s', 'c_b_s',
              'ffn_norm', 'final_norm']
SMALL_SHARD = ['a_conv_w', 'c_pool_scale', 'c_sln_g', 'c_sln_b']
BIG = ['a_w_in', 'a_w_out', 'c_w_in', 'c_w_out', 'ffn_w_gate', 'ffn_w_up', 'ffn_w_down']
TRANSPOSED = ('a_w_in', 'ffn_w_gate', 'ffn_w_up')
BIG_COL_SHARDED = {'c_w_in'}


def _full_weight(name, g8):
    _, h, cols = g8.shape
    g4 = g8.reshape(N_CHIPS, 2 * h, cols)
    if name not in BIG_COL_SHARDED:
        return g4.reshape(-1, cols)
    return jnp.transpose(g4, (1, 0, 2)).reshape(2 * h, N_CHIPS * cols)


def _to_shard_major(name, f):
    if name not in BIG_COL_SHARDED:
        return f.reshape(N_CHIPS, f.shape[0] // N_CHIPS, f.shape[1])
    r, cfull = f.shape
    return jnp.transpose(f.reshape(r, N_CHIPS, cfull // N_CHIPS), (1, 0, 2))


def kernel(*args):
    a = dict(zip(IN_NAMES, args))
    bl, seq, _ = a['x'].shape
    n = bl * seq
    x = a['x'].reshape(n, D_MODEL)
    target = a['loss_target'].reshape(n, D_MODEL)
    xi, yi, ci = _me()
    chip = 2 * xi + yi

    shard = {'a_w_in': a['a_w_in'][0].T, 'a_w_out': a['a_w_out'][0], 'c_w_in': a['c_w_in'][0], 'c_w_out': a['c_w_out'][0]}
    for layer in range(2):
        shard['gate' + str(layer)] = a['ffn_w_gate'][layer].T
        shard['up' + str(layer)] = a['ffn_w_up'][layer].T
        shard['down' + str(layer)] = a['ffn_w_down'][layer]
    shard = {k: v.astype(BF16) for k, v in shard.items()}
    core = ci.astype(jnp.int32).reshape(1)
    place = jnp.stack([ci, chip]).astype(jnp.int32)
    block_id = 4 * xi + 2 * yi + ci

    def first_hop(*names):
        return gather_first_hop([shard[k] for k in names])

    def finish(name, g8):
        h = shard[name].shape[0] // 2
        own = lax.dynamic_slice_in_dim(shard[name], ci * h, h, axis=0)
        return _full_weight(name, lax.dynamic_update_slice_in_dim(g8, own[None], block_id, axis=0))

    a_w_in_t = _full_weight('a_w_in', allgather_blocks([shard['a_w_in']], "gather_a_w_in")[0])
    in0_width = a_w_in_t.shape[0]
    small_shard_pack, small_shard_layout = _pack([a[k] for k in SMALL_SHARD])
    hop_a = first_hop('a_w_out', 'c_w_out')
    hop_s = chip_scatter_comm([], shared=small_shard_pack)
    mix_norm, ffn_norm = a['mix_norm'], a['ffn_norm']
    (hn0, q, kv, cc), outs = norm_inproj(
        x, mix_norm[0:1], a_w_in_t, a['a_b_in'],
        [(0, ATTN_WIDTH), (ATTN_WIDTH, ATTN_WIDTH + 2 * KV_WIDTH), (ATTN_WIDTH + 2 * KV_WIDTH, in0_width)],
        [BF16, BF16, F32], "in_proj0", comm=hop_a + hop_s, w_transposed=True)
    got_a, (ss,) = hop_a.split(outs, hop_s)
    ss = lax.dynamic_update_slice_in_dim(ss, small_shard_pack[None], chip, axis=0)
    ss_full = []
    for r0, rows, size, shape in small_shard_layout:
        per_chip = ss[:, r0:r0 + rows].reshape(N_CHIPS, -1)[:, :size].reshape((N_CHIPS,) + shape)
        ss_full.append(jnp.concatenate([per_chip[k] for k in range(N_CHIPS)], axis=-1))
    a_conv_w, c_pool_scale, c_sln_g, c_sln_b = [v[0] for v in ss_full]

    conv_taps = jnp.pad(a_conv_w, ((0, 32 - CONV_KERNEL), (0, 0)))
    conv_vec = jnp.pad(jnp.stack([a['a_conv_b'][0], a['a_cln_g'][0], a['a_cln_b'][0]]), ((0, 5), (0, 0)))
    sinks_b = jnp.pad(jnp.repeat(a['a_sinks'][0].reshape(N_KV_HEADS, GROUP), ATTN_BLOCK, axis=1), ((0, 6), (0, 0)))
    w_pool_bf = a['c_w_pool'][0].astype(BF16)
    pool_scale = c_pool_scale.reshape(1, POOL_WIDTH)
    w_s = a['c_w_s'][0]
    b_s_t = a['c_b_s'][0].T
    sgu_ln = jnp.pad(jnp.stack([c_sln_g, c_sln_b]), ((0, 6), (0, 0)))
    final_norm = a['final_norm'].reshape(1, D_MODEL)

    hop_b, pass_a = first_hop('gate0', 'c_w_in'), gather_second_hop(got_a)
    attn, outs = attn_fwd(q, kv, sinks_b, bl, seq, "attn_fwd", comm=hop_b + pass_a)
    got_b, done = hop_b.split(outs, pass_a)
    a_w_out, c_w_out = finish('a_w_out', done[0]), finish('c_w_out', done[1])

    hop_c, pass_b = first_hop('up0', 'down0'), gather_second_hop(got_b)
    (conv, conv_h1), outs = conv_fwd(cc, conv_taps, conv_vec, bl, seq, "conv_fwd", comm=hop_c + pass_b)
    got_c, done = hop_c.split(outs, pass_b)
    wg0, c_w_in = finish('gate0', done[0]), finish('c_w_in', done[1])

    h1, done = out_proj(x, attn, conv, a_w_out, "out_proj0", comm=gather_second_hop(got_c))
    wu0, wd0 = finish('up0', done[0]), finish('down0', done[1])

    (hnf0, g0, u0), got_e = ffn_gate_up(h1, ffn_norm[0:1], wg0, wu0, "ffn_gate_up0",
                                        comm=first_hop('gate1', 'up1', 'down1'))

    h2, done = ffn_down(h1, g0, u0, wd0, "ffn_down0", comm=gather_second_hop(got_e))
    wg1, wu1, wd1 = finish('gate1', done[0]), finish('up1', done[1]), finish('down1', done[2])
    wg, wu, wd = [wg0, wg1], [wu0, wu1], [wd0, wd1]

    (hn1, zp, zs), _ = norm_inproj(
        h2, mix_norm[1:2], c_w_in, jnp.zeros((1, c_w_in.shape[1]), F32),
        [(0, POOL_WIDTH), (POOL_WIDTH, c_w_in.shape[1])], [F32, F32], "in_proj1")
    pool = pool_fwd(zp, w_pool_bf, pool_scale, bl, seq, "pool_fwd")
    sgu = sgu_fwd(zs, w_s, b_s_t, sgu_ln, bl, seq, "sgu_fwd")
    h3, _ = out_proj(h2, pool, sgu, c_w_out, "out_proj1")
    (hnf1, g1, u1), _ = ffn_gate_up(h3, ffn_norm[1:2], wg1, wu1, "ffn_gate_up1")
    h4, _ = ffn_down(h3, g1, u1, wd1, "ffn_down1")

    dh4, d_final_norm, loss_local = loss_head(h4, final_norm, target, "loss_head")

    grads = {}
    pieces = {}

    def slabs_of(names, fulls):
        return [_to_shard_major(k, fulls[k]) for k in names]

    def pair_sums_of(names, slabs, gots):
        return [pair_sum(t, gt, core, "pair_sum_" + k) for k, t, gt in zip(names, slabs, gots)]

    def chip_sums_of(names, sums, from_chips):
        return [chip_sum(s, p, place, "chip_sum_" + k) for k, p, s in zip(names, from_chips, sums)]

    (dg, du, act), _ = ffn_down_bwd(dh4, g1, u1, wd[1], "ffn_down_bwd1")
    full1 = {'down1': mm_tn(act, dh4, "dw_down1"), 'gate1': mm_tn(dg, hnf1, "dw_gate1"),
             'up1': mm_tn(du, hnf1, "dw_up1")}
    names1 = ['gate1', 'up1', 'down1']
    slabs1 = slabs_of(names1, full1)
    dh3, d_ffn_norm1, got1 = proj_rms_bwd([dg, du], [wg[1], wu[1]], h3, ffn_norm[1:2], dh4, 1, "ffn_up_bwd1",
                                          tm_pref=512, w_transposed=True, comm=give_half_comm(slabs1))
    sums1 = pair_sums_of(names1, slabs1, got1)
    d_pool, d_sgu = out_proj_bwd(dh3, c_w_out, [F32, F32], "out_proj_bwd1")
    full1['c_w_out'] = jnp.concatenate([mm_tn(pool, dh3, "dw_out1_pool"), mm_tn(sgu, dh3, "dw_out1_sgu")], axis=0)
    (dzp, d_w_pool, d_pool_scale), from_gate = pool_bwd(zp, w_pool_bf, pool_scale, d_pool, bl, seq, "pool_bwd",
                                                        comm=chip_scatter_comm(sums1[0:1]))
    (dzs, d_w_s, d_b_s_b, d_sgu_ln), from_up = sgu_bwd(zs, w_s, b_s_t, sgu_ln, d_sgu, bl, seq, "sgu_bwd",
                                                       comm=chip_scatter_comm(sums1[1:2]))
    full1['c_w_in'] = jnp.concatenate([mm_tn(hn1, dzp, "dw_in1_pool"), mm_tn(hn1, dzs, "dw_in1_sgu")], axis=1)
    names1b = ['c_w_out', 'c_w_in']
    slabs1b = slabs_of(names1b, full1)
    heavy_pack, heavy_layout = _pack([d_w_pool[None], d_w_s[None]])
    chips_down, pair1b = chip_scatter_comm(sums1[2:3]), give_half_comm(slabs1b, plain=[heavy_pack])
    dh2, d_mix_norm1, outs = proj_rms_bwd([dzp, dzs], [c_w_in[:, :POOL_WIDTH], c_w_in[:, POOL_WIDTH:]], h2,
                                          mix_norm[1:2], dh3, 1, "in_proj_bwd1", comm=chips_down + pair1b)
    from_down, got1b = chips_down.split(outs, pair1b)
    mine1 = chip_sums_of(names1, sums1, from_gate + from_up + from_down)
    sums1b = pair_sums_of(names1b, slabs1b, got1b[:2])
    heavy_pair = add_cast(heavy_pack[None], got1b[2][None], "pair_sum_heavy", dtype=F32)[0]

    join1, chips1b = join_comm(mine1), chip_scatter_comm(sums1b, shared=heavy_pair)
    (dg, du, act), outs = ffn_down_bwd(dh2, g0, u0, wd[0], "ffn_down_bwd0", comm=join1 + chips1b)
    whole1, from_chips1b = join1.split(outs, chips1b)
    pieces.update(dict(zip(names1, whole1)))
    mine1b = chip_sums_of(names1b, sums1b, from_chips1b[:2])
    heavy_chips = lax.dynamic_update_slice_in_dim(from_chips1b[2], heavy_pair[None], chip, axis=0)
    grads['c_w_pool'], grads['c_w_s'] = _unpack(sum_parts(heavy_chips, "heavy_sum"), heavy_layout)
    full0 = {'down0': mm_tn(act, dh2, "dw_down0"), 'gate0': mm_tn(dg, hnf0, "dw_gate0"),
             'up0': mm_tn(du, hnf0, "dw_up0")}
    names0 = ['gate0', 'up0', 'down0']
    slabs0 = slabs_of(names0, full0)
    join1b, pair0 = join_comm(mine1b), give_half_comm(slabs0)
    dh1, d_ffn_norm0, outs = proj_rms_bwd([dg, du], [wg[0], wu[0]], h1, ffn_norm[0:1], dh2, 1, "ffn_up_bwd0",
                                          tm_pref=512, comm=join1b + pair0, w_transposed=True)
    whole1b, got0 = join1b.split(outs, pair0)
    pieces.update(dict(zip(names1b, whole1b)))
    sums0 = pair_sums_of(names0, slabs0, got0)

    d_attn, d_conv = out_proj_bwd(dh1, a_w_out, [BF16, F32], "out_proj_bwd0")
    full_o = {'a_w_out': jnp.concatenate([mm_tn(attn, dh1, "dw_out0_attn"), mm_tn(conv, dh1, "dw_out0_conv")], axis=0)}
    slabs_o = slabs_of(['a_w_out'], full_o)
    chips0, pair_o = chip_scatter_comm(sums0), give_half_comm(slabs_o)
    (dq, dkv, d_sinks_b), outs = attn_bwd(q, kv, sinks_b, d_attn, bl, seq, "attn_bwd", comm=chips0 + pair_o)
    from_chips0, got_o = chips0.split(outs, pair_o)
    mine0 = chip_sums_of(names0, sums0, from_chips0)
    sums_o = pair_sums_of(['a_w_out'], slabs_o, got_o)
    join0, chips_o = join_comm(mine0), chip_scatter_comm(sums_o)
    (dcc, d_conv_taps, d_conv_vec), outs = conv_bwd(cc, conv_h1, conv_taps, conv_vec, d_conv, bl, seq, "conv_bwd",
                                                    comm=join0 + chips_o)
    whole0, from_chips_o = join0.split(outs, chips_o)
    pieces.update(dict(zip(names0, whole0)))
    mine_o = chip_sums_of(['a_w_out'], sums_o, from_chips_o)
    kq, kk = ATTN_WIDTH, ATTN_WIDTH + 2 * KV_WIDTH
    grad_x, d_mix_norm0, _ = proj_rms_bwd([dq, dkv, dcc], [a_w_in_t[:kq], a_w_in_t[kq:kk], a_w_in_t[kk:]], x,
                                          mix_norm[0:1], dh1, 1, "in_proj_bwd0", w_transposed=True)
    dw_q, db_q = mm_tn(dq, hn0, "dw_in0_q", xsum=True)
    dw_kv, db_kv = mm_tn(dkv, hn0, "dw_in0_kv", xsum=True)
    (dw_c, db_c), whole_o = mm_tn(dcc, hn0, "dw_in0_c", xsum=True, comm=join_comm(mine_o))
    pieces['a_w_out'] = whole_o[0]
    d_a_b_in = jnp.concatenate([db_q, db_kv, db_c], axis=0)
    slabs_i = slabs_of(['a_w_in'], {'a_w_in': jnp.concatenate([dw_q, dw_kv, dw_c], axis=0)})

    small_full = {
        'mix_norm': jnp.stack([d_mix_norm0, d_mix_norm1]), 'a_b_in': d_a_b_in[None], 'a_sinks': d_sinks_b[:, 0][None],
        'a_conv_w': d_conv_taps[:CONV_KERNEL][None], 'a_conv_b': d_conv_vec[0][None], 'a_cln_g': d_conv_vec[1][None],
        'a_cln_b': d_conv_vec[2][None], 'c_pool_scale': d_pool_scale[0][None],
        'c_sln_g': d_sgu_ln[0][None], 'c_sln_b': d_sgu_ln[1][None],
        'c_b_s': d_b_s_b[:, :, 0][None], 'ffn_norm': jnp.stack([d_ffn_norm0, d_ffn_norm1]),
        'final_norm': d_final_norm, 'loss': loss_local.reshape(1)}
    small_names = SMALL_REPL + SMALL_SHARD
    tail_names = [k for k in small_names if k in small_full] + ['loss']
    small_pack, small_layout = _pack([small_full[k] for k in tail_names])

    pieces['a_w_in'], small_sum = tail_reduce(slabs_i[0], small_pack, "tail_reduce")

    for k in ('a_w_in', 'a_w_out', 'c_w_in', 'c_w_out'):
        grads[k] = pieces[k][None]
    for short, key in (('gate', 'ffn_w_gate'), ('up', 'ffn_w_up'), ('down', 'ffn_w_down')):
        grads[key] = jnp.stack([pieces[short + '0'], pieces[short + '1']])

    for k, g in zip(tail_names, _unpack(small_sum, small_layout)):
        if k in SMALL_SHARD:
            width = a[k].shape[-1]
            g = lax.dynamic_slice_in_dim(g, chip * width, width, axis=g.ndim - 1)
        grads[k] = g
    loss = grads.pop('loss')[0]

    delta, new_m, new_v = {}, {}, {}
    for k in BIG:
        if k in TRANSPOSED:
            flip = lambda t: jnp.swapaxes(t, 1, 2)
            d, m, v = adamw(flip(a[k]), grads[k], flip(a['m_' + k]), flip(a['v_' + k]), "adamw_" + k)
            grads[k], delta[k], new_m[k], new_v[k] = flip(grads[k]), flip(d), flip(m), flip(v)
        else:
            delta[k], new_m[k], new_v[k] = adamw(a[k], grads[k], a['m_' + k], a['v_' + k], "adamw_" + k)
    two_d = lambda t: t.reshape(1, -1) if t.ndim == 1 else t
    ds, ms, vs = adamw_many([two_d(a[k]) for k in small_names], [two_d(grads[k]) for k in small_names],
                            [two_d(a['m_' + k]) for k in small_names], [two_d(a['v_' + k]) for k in small_names],
                            "adamw_small")
    for k, dv, mv, vv in zip(small_names, ds, ms, vs):
        delta[k], new_m[k], new_v[k] = [t.reshape(a[k].shape) for t in (dv, mv, vv)]

    return (loss, grad_x.reshape(a['x'].shape), *[grads[k] for k in WEIGHTS], *[delta[k] for k in WEIGHTS],
            *[new_m[k] for k in WEIGHTS], *[new_v[k] for k in WEIGHTS])
```

```python
import functools

import jax
import jax.numpy as jnp
from jax import lax
from jax.experimental import pallas as pl
from jax.experimental.pallas import tpu as pltpu

F32 = jnp.float32
BF16 = jnp.bfloat16

D_MODEL = 1024
EPS = 1e-5
N_Q_HEADS, N_KV_HEADS, HEAD_DIM = 8, 2, 64
ATTN_BLOCK = 128
ATTN_WIDTH = N_Q_HEADS * HEAD_DIM
KV_WIDTH = N_KV_HEADS * HEAD_DIM
CONV_WIDTH = 512
CONV_KERNEL = 31
CONV_HALO = 32
POOL_WINDOWS = (2, 4, 8, 16)
POOL_WIDTH = 512
POOL_HALO = 16
SGU_WIDTH = 512
SGU_CHUNK = 128
D_FF = 2816
FF_CHUNK = 128
MXU_COLS = 256
FFN_AHEAD = 1
LANES = 128
N_CHIPS = 4
N_DEV = 8

ADAM_LR, ADAM_B1, ADAM_B2, ADAM_EPS, ADAM_WD, ADAM_STEP = 0.001, 0.9, 0.999, 1e-08, 0.01, 10

VMEM_LIMIT = 56 * 2**20

WEIGHTS = ['mix_norm', 'a_w_in', 'a_b_in', 'a_sinks', 'a_conv_w', 'a_conv_b', 'a_cln_g', 'a_cln_b', 'a_w_out',
           'c_w_in', 'c_w_pool', 'c_pool_scale', 'c_sln_g', 'c_sln_b', 'c_w_s', 'c_b_s', 'c_w_out',
           'ffn_norm', 'ffn_w_gate', 'ffn_w_up', 'ffn_w_down', 'final_norm']
IN_NAMES = (['x'] + WEIGHTS + ['loss_target'] + ['m_' + n for n in WEIGHTS] + ['v_' + n for n in WEIGHTS])


def _params(*sem):
    return pltpu.CompilerParams(dimension_semantics=sem, vmem_limit_bytes=VMEM_LIMIT)


def _dot(a, b):
    return jnp.dot(a, b, preferred_element_type=F32)


def _dot_nt(a, b):
    return lax.dot_general(a, b, (((1,), (1,)), ((), ())), preferred_element_type=F32)


def _dot_tn(a, b):
    return lax.dot_general(a, b, (((0,), (0,)), ((), ())), preferred_element_type=F32)


def _sigmoid(v):
    return 0.5 * jnp.tanh(0.5 * v) + 0.5


def _row_tile(n, pref):
    t = min(n, pref)
    while n % t:
        t //= 2
    return t


def _col_tile(m, rows, budget=6 * 2**20):
    best = LANES
    for t in range(LANES, m + 1, LANES):
        if m % t == 0 and rows * t * 4 <= budget:
            best = t
    return best


class Comm:
    def __init__(self, ins, out_shapes, plan, count, aliases=None):
        self.ins, self.out_shapes, self.plan, self.count, self.aliases = ins, out_shapes, plan, count, aliases or {}

    def __add__(self, other):
        ni, no = len(self.ins), len(self.out_shapes)

        def plan(ins, outs, pos):
            return self.plan(ins[:ni], outs[:no], pos) + other.plan(ins[ni:], outs[no:], pos)

        aliases = dict(self.aliases)
        aliases.update({ni + i: no + o for i, o in other.aliases.items()})
        return Comm(list(self.ins) + list(other.ins), list(self.out_shapes) + list(other.out_shapes), plan,
                    self.count + other.count, aliases)

    def split(self, outs, other):
        return outs[:len(self.out_shapes)], outs[len(self.out_shapes):]


def _pcall(body, name, grid, in_specs, out_specs, out_shape, scratch_shapes, args, sem, comm=None):
    single = not isinstance(out_shape, (list, tuple))
    if single:
        out_specs, out_shape = [out_specs], [out_shape]
    if comm is None:
        res = pl.pallas_call(body, name=name, grid=grid, in_specs=in_specs, out_specs=list(out_specs),
                             out_shape=list(out_shape), scratch_shapes=list(scratch_shapes),
                             compiler_params=_params(*sem))(*args)
        return (res[0] if single else res), []
    na, nci, no, nco, ns = len(args), len(comm.ins), len(out_shape), len(comm.out_shapes), len(scratch_shapes)

    def wrapped(*refs):
        a_refs, ci_refs = refs[:na], refs[na:na + nci]
        o_refs, co_refs = refs[na + nci:na + nci + no], refs[na + nci + no:na + nci + no + nco]
        s_refs = refs[na + nci + no + nco:na + nci + no + nco + ns]
        send_sems, recv_sems = refs[-2], refs[-1]
        pos = _me()

        def copies():
            return [pltpu.make_async_remote_copy(src_ref=s, dst_ref=d, send_sem=send_sems.at[i],
                                                 recv_sem=recv_sems.at[i], device_id=_flip(pos, rel),
                                                 device_id_type=MESH)
                    for i, (rel, s, d) in enumerate(comm.plan(ci_refs, co_refs, pos))]

        first, last = None, None
        for d, size in enumerate(grid):
            f, l = pl.program_id(d) == 0, pl.program_id(d) == size - 1
            first = f if first is None else first & f
            last = l if last is None else last & l

        @pl.when(first)
        def _():
            for cp in copies():
                cp.start()

        body(*a_refs, *o_refs, *s_refs)

        @pl.when(last)
        def _():
            for cp in copies():
                cp.wait()

    res = pl.pallas_call(
        wrapped, name=name, grid=grid,
        in_specs=list(in_specs) + [ANY] * nci, out_specs=list(out_specs) + [ANY] * nco,
        out_shape=list(out_shape) + [jax.ShapeDtypeStruct(s, d) for s, d in comm.out_shapes],
        scratch_shapes=list(scratch_shapes) + [pltpu.SemaphoreType.DMA((comm.count,)),
                                               pltpu.SemaphoreType.DMA((comm.count,))],
        input_output_aliases={na + i: no + o for i, o in comm.aliases.items()},
        compiler_params=_params(*(["arbitrary"] * len(grid))),
    )(*args, *comm.ins)
    outs = res[:no]
    return (outs[0] if single else outs), list(res[no:])


def norm_inproj(x, gain, w, bias, splits, dtypes, name, comm=None, w_transposed=False):
    n = x.shape[0]
    m = w.shape[0] if w_transposed else w.shape[1]
    tm = _row_tile(n, 1024)

    def body(x_ref, g_ref, w_ref, b_ref, hn_ref, *outs):
        xv = x_ref[...]
        r = lax.rsqrt(jnp.mean(xv * xv, axis=-1, keepdims=True) + EPS)
        hn = ((xv * r) * g_ref[...]).astype(BF16)
        hn_ref[...] = hn
        z = (_dot_nt if w_transposed else _dot)(hn, w_ref[...]) + b_ref[...]
        for o, (lo, hi) in zip(outs, splits):
            o[...] = z[:, lo:hi].astype(o.dtype)

    out_shape = [jax.ShapeDtypeStruct((n, D_MODEL), BF16)]
    out_specs = [pl.BlockSpec((tm, D_MODEL), lambda i: (i, 0))]
    for (lo, hi), dt in zip(splits, dtypes):
        out_shape.append(jax.ShapeDtypeStruct((n, hi - lo), dt))
        out_specs.append(pl.BlockSpec((tm, hi - lo), lambda i: (i, 0)))
    return _pcall(
        body, name, (n // tm,),
        [pl.BlockSpec((tm, D_MODEL), lambda i: (i, 0)),
         pl.BlockSpec((1, D_MODEL), lambda i: (0, 0)),
         pl.BlockSpec(w.shape, lambda i: (0, 0)),
         pl.BlockSpec((1, m), lambda i: (0, 0))],
        out_specs, out_shape, [], (x, gain, w, bias), ("parallel",), comm)


def out_proj(res, m1, m2, w, name, comm=None):
    n = res.shape[0]
    k1, k2 = m1.shape[1], m2.shape[1]
    assert k1 == k2
    tm = _row_tile(n, 1024)

    def body(r_ref, a_ref, b_ref, w1_ref, w2_ref, o_ref):
        o_ref[...] = r_ref[...] + _dot(a_ref[...], w1_ref[...]) + _dot(b_ref[...], w2_ref[...])

    return _pcall(
        body, name, (n // tm,),
        [pl.BlockSpec((tm, D_MODEL), lambda i: (i, 0)),
         pl.BlockSpec((tm, k1), lambda i: (i, 0)),
         pl.BlockSpec((tm, k2), lambda i: (i, 0)),
         pl.BlockSpec((k1, D_MODEL), lambda i: (0, 0)),
         pl.BlockSpec((k2, D_MODEL), lambda i: (1, 0))],
        pl.BlockSpec((tm, D_MODEL), lambda i: (i, 0)),
        jax.ShapeDtypeStruct((n, D_MODEL), F32), [], (res, m1, m2, w, w), ("parallel",), comm)


def ffn_gate_up(h, gain, wg_t, wu_t, name, comm=None):
    n = h.shape[0]
    tm = _row_tile(n, 512)
    th = D_FF

    def body(h_ref, g_ref, wg_ref, wu_ref, hn_ref, go_ref, uo_ref):
        @pl.when(pl.program_id(1) == 0)
        def _():
            xv = h_ref[...]
            r = lax.rsqrt(jnp.mean(xv * xv, axis=-1, keepdims=True) + EPS)
            hn_ref[...] = ((xv * r) * g_ref[...]).astype(BF16)

        hn = hn_ref[...]
        go_ref[...] = _dot_nt(hn, wg_ref[...]).astype(BF16)
        uo_ref[...] = _dot_nt(hn, wu_ref[...]).astype(BF16)

    return _pcall(
        body, name, (n // tm, D_FF // th),
        [pl.BlockSpec((tm, D_MODEL), lambda i, j: (i, 0)),
         pl.BlockSpec((1, D_MODEL), lambda i, j: (0, 0)),
         pl.BlockSpec((th, D_MODEL), lambda i, j: (j, 0), pipeline_mode=pl.Buffered(1)),
         pl.BlockSpec((th, D_MODEL), lambda i, j: (j, 0), pipeline_mode=pl.Buffered(1))],
        [pl.BlockSpec((tm, D_MODEL), lambda i, j: (i, 0)),
         pl.BlockSpec((tm, th), lambda i, j: (i, j)),
         pl.BlockSpec((tm, th), lambda i, j: (i, j))],
        [jax.ShapeDtypeStruct((n, D_MODEL), BF16),
         jax.ShapeDtypeStruct((n, D_FF), BF16),
         jax.ShapeDtypeStruct((n, D_FF), BF16)],
        [], (h, gain, wg_t, wu_t), ("parallel", "arbitrary"), comm)


def ffn_down(h, g, u, wd, name, comm=None):
    n = h.shape[0]
    tm = _row_tile(n, 1024)

    def body(h_ref, g_ref, u_ref, w_ref, o_ref, a_ref):
        for c0 in range(0, D_FF, FF_CHUNK):
            gv = g_ref[:, c0:c0 + FF_CHUNK]
            a_ref[:, c0:c0 + FF_CHUNK] = gv * _sigmoid(gv) * u_ref[:, c0:c0 + FF_CHUNK]
        o_ref[...] = h_ref[...] + _dot(a_ref[...], w_ref[...])

    return _pcall(
        body, name, (n // tm,),
        [pl.BlockSpec((tm, D_MODEL), lambda i: (i, 0)),
         pl.BlockSpec((tm, D_FF), lambda i: (i, 0)),
         pl.BlockSpec((tm, D_FF), lambda i: (i, 0)),
         pl.BlockSpec((D_FF, D_MODEL), lambda i: (0, 0), pipeline_mode=pl.Buffered(1))],
        pl.BlockSpec((tm, D_MODEL), lambda i: (i, 0)),
        jax.ShapeDtypeStruct((n, D_MODEL), F32),
        [pltpu.VMEM((tm, D_FF), BF16)], (h, g, u, wd), ("parallel",), comm)


def ffn_down_bwd(dh, g, u, wd, name, comm=None):
    n = dh.shape[0]
    tm = _row_tile(n, 512)

    def body(dh_ref, g_ref, u_ref, w_ref, dg_ref, du_ref, a_ref):
        dhb = dh_ref[...].astype(BF16)
        chunks = [slice(c0, c0 + MXU_COLS) for c0 in range(0, D_FF, MXU_COLS)]
        ahead = [_dot_nt(dhb, w_ref[c, :]) for c in chunks[:FFN_AHEAD]]
        for i, cols in enumerate(chunks):
            da = ahead.pop(0).astype(BF16)
            if i + FFN_AHEAD < len(chunks):
                ahead.append(_dot_nt(dhb, w_ref[chunks[i + FFN_AHEAD], :]))
            gv, uv = g_ref[:, cols], u_ref[:, cols]
            sg = _sigmoid(gv)
            act = gv * sg
            dg_ref[:, cols] = (da * uv) * (sg + act * (1.0 - sg))
            du_ref[:, cols] = da * act
            a_ref[:, cols] = act * uv

    spec_h = pl.BlockSpec((tm, D_FF), lambda i: (i, 0))
    return _pcall(
        body, name, (n // tm,),
        [pl.BlockSpec((tm, D_MODEL), lambda i: (i, 0)), spec_h, spec_h,
         pl.BlockSpec((D_FF, D_MODEL), lambda i: (0, 0))],
        [spec_h, spec_h, spec_h], [jax.ShapeDtypeStruct((n, D_FF), BF16)] * 3,
        [], (dh, g, u, wd), ("parallel",), comm)


def mm_tn(x, dy, name, xsum=False, comm=None):
    n, k = x.shape
    m = dy.shape[1]
    tk = _col_tile(k, m)
    tt = _row_tile(n, 2048)

    def body(x_ref, dy_ref, o_ref, *rest):
        xt_ref = rest[-1]
        t = pl.program_id(1)
        xv = x_ref[...]
        xt_ref[...] = xv.astype(BF16).T
        part = _dot(xt_ref[...], dy_ref[...].astype(BF16))

        @pl.when(t == 0)
        def _():
            o_ref[...] = part

        @pl.when(t > 0)
        def _():
            o_ref[...] += part

        if xsum:
            cs = jnp.broadcast_to(jnp.sum(xv.astype(F32), axis=0, keepdims=True), rest[0].shape)

            @pl.when(t == 0)
            def _():
                rest[0][...] = cs

            @pl.when(t > 0)
            def _():
                rest[0][...] += cs

    out_shape = [jax.ShapeDtypeStruct((k, m), F32)]
    out_specs = [pl.BlockSpec((tk, m), lambda j, t: (j, 0))]
    if xsum:
        out_shape.append(jax.ShapeDtypeStruct((8, k), F32))
        out_specs.append(pl.BlockSpec((8, tk), lambda j, t: (0, j)))
    res, comm_outs = _pcall(
        body, name, (k // tk, n // tt),
        [pl.BlockSpec((tt, tk), lambda j, t: (t, j)),
         pl.BlockSpec((tt, m), lambda j, t: (t, 0))],
        out_specs, out_shape, [pltpu.VMEM((tk, tt), BF16)], (x, dy), ("arbitrary", "arbitrary"), comm)
    res = (res[0], res[1][0]) if xsum else res[0]
    return res if comm is None else (res, comm_outs)


def out_proj_bwd(dh, w, dtypes, name):
    n = dh.shape[0]
    k = w.shape[0]
    half = k // 2
    tm = _row_tile(n, 1024)

    def body(dh_ref, w_ref, a_ref, b_ref):
        dm = _dot_nt(dh_ref[...].astype(BF16), w_ref[...])
        a_ref[...] = dm[:, :half].astype(a_ref.dtype)
        b_ref[...] = dm[:, half:].astype(b_ref.dtype)

    return pl.pallas_call(
        body, name=name, grid=(n // tm,),
        in_specs=[pl.BlockSpec((tm, D_MODEL), lambda i: (i, 0)),
                  pl.BlockSpec((k, D_MODEL), lambda i: (0, 0))],
        out_specs=[pl.BlockSpec((tm, half), lambda i: (i, 0))] * 2,
        out_shape=[jax.ShapeDtypeStruct((n, half), dtypes[0]), jax.ShapeDtypeStruct((n, half), dtypes[1])],
        compiler_params=_params("parallel"),
    )(dh, w)


def proj_rms_bwd(dys, ws, h_in, gain, dres, nk, name, tm_pref=512, comm=None, w_transposed=False):
    n = h_in.shape[0]
    npair = len(dys)
    tm = _row_tile(n, tm_pref)
    tks = [dy.shape[1] // nk for dy in dys]
    mm = _dot if w_transposed else _dot_nt

    def body(*refs):
        dy_refs = refs[:npair]
        w_refs = refs[npair:2 * npair]
        h_ref, g_ref, dr_ref, o_ref, dg_ref, acc_ref = refs[2 * npair:]
        i, k = pl.program_id(0), pl.program_id(1)
        part = mm(dy_refs[0][...], w_refs[0][...])
        for p in range(1, npair):
            part = part + mm(dy_refs[p][...], w_refs[p][...])

        @pl.when(k == 0)
        def _():
            acc_ref[...] = part

        @pl.when(k > 0)
        def _():
            acc_ref[...] += part

        @pl.when(k == nk - 1)
        def _():
            dhn = acc_ref[...]
            xv = h_ref[...]
            r = lax.rsqrt(jnp.mean(xv * xv, axis=-1, keepdims=True) + EPS)
            xh = xv * r
            uv = dhn * g_ref[...]
            o_ref[...] = dr_ref[...] + r * (uv - xh * jnp.mean(uv * xh, axis=-1, keepdims=True))
            dgp = jnp.broadcast_to(jnp.sum(dhn * xh, axis=0, keepdims=True), dg_ref.shape)

            @pl.when(i == 0)
            def _():
                dg_ref[...] = dgp

            @pl.when(i > 0)
            def _():
                dg_ref[...] += dgp

    row = pl.BlockSpec((tm, D_MODEL), lambda i, k: (i, 0))
    in_specs = [pl.BlockSpec((tm, tk), lambda i, k: (i, k)) for tk in tks]
    once = dict(pipeline_mode=pl.Buffered(1)) if nk == 1 else {}
    if w_transposed:
        in_specs += [pl.BlockSpec((tk, D_MODEL), lambda i, k: (k, 0), **once) for tk in tks]
    else:
        in_specs += [pl.BlockSpec((D_MODEL, tk), lambda i, k: (0, k), **once) for tk in tks]
    in_specs += [row, pl.BlockSpec((1, D_MODEL), lambda i, k: (0, 0)), row]
    (dh, dgain), comm_outs = _pcall(
        body, name, (n // tm, nk), in_specs,
        [row, pl.BlockSpec((8, D_MODEL), lambda i, k: (0, 0))],
        [jax.ShapeDtypeStruct((n, D_MODEL), F32), jax.ShapeDtypeStruct((8, D_MODEL), F32)],
        [pltpu.VMEM((tm, D_MODEL), F32)], (*dys, *ws, h_in, gain, dres), ("arbitrary", "arbitrary"), comm)
    return dh, dgain[0], comm_outs


def loss_head(h, gain, target, name):
    n = h.shape[0]
    tm = _row_tile(n, 512)

    def body(h_ref, g_ref, t_ref, dh_ref, dg_ref, l_ref):
        i = pl.program_id(0)
        xv = h_ref[...]
        r = lax.rsqrt(jnp.mean(xv * xv, axis=-1, keepdims=True) + EPS)
        xh = xv * r
        err = xh * g_ref[...] - t_ref[...]
        dy = err * (1.0 / D_MODEL)
        uv = dy * g_ref[...]
        dh_ref[...] = r * (uv - xh * jnp.mean(uv * xh, axis=-1, keepdims=True))
        dgp = jnp.broadcast_to(jnp.sum(dy * xh, axis=0, keepdims=True), dg_ref.shape)
        lp = jnp.sum(jnp.sum(err * err, axis=-1, keepdims=True), axis=0, keepdims=True) * (0.5 / D_MODEL)
        lp = jnp.broadcast_to(lp, l_ref.shape)

        @pl.when(i == 0)
        def _():
            dg_ref[...] = dgp
            l_ref[...] = lp

        @pl.when(i > 0)
        def _():
            dg_ref[...] += dgp
            l_ref[...] += lp

    row = pl.BlockSpec((tm, D_MODEL), lambda i: (i, 0))
    dh, dg, l = pl.pallas_call(
        body, name=name, grid=(n // tm,),
        in_specs=[row, pl.BlockSpec((1, D_MODEL), lambda i: (0, 0)), row],
        out_specs=[row, pl.BlockSpec((8, D_MODEL), lambda i: (0, 0)), pl.BlockSpec((8, LANES), lambda i: (0, 0))],
        out_shape=[jax.ShapeDtypeStruct((n, D_MODEL), F32), jax.ShapeDtypeStruct((8, D_MODEL), F32),
                   jax.ShapeDtypeStruct((8, LANES), F32)],
        compiler_params=_params("arbitrary"),
    )(h, gain, target)
    return dh, dg[0], l[0, 0]


GROUP = N_Q_HEADS // N_KV_HEADS
GQ = GROUP * ATTN_BLOCK


def _attn_mask_t(n):
    r = lax.broadcasted_iota(jnp.int32, (2 * ATTN_BLOCK, GQ), 0)
    qi = lax.broadcasted_iota(jnp.int32, (2 * ATTN_BLOCK, GQ), 1) & (ATTN_BLOCK - 1)
    band = (r > qi) & (r <= qi + ATTN_BLOCK)
    return band & ((r >= ATTN_BLOCK) | (n > 0))


def _stack_heads(blk, kh):
    return jnp.concatenate([blk[:, (kh * GROUP + g) * HEAD_DIM:(kh * GROUP + g + 1) * HEAD_DIM]
                            for g in range(GROUP)], axis=0)


def _attn_probs_t(kk, qs, mask, sink):
    s = _dot_nt(kk, qs) * (HEAD_DIM ** -0.5)
    s = jnp.where(mask, s, -1e30)
    m = jnp.maximum(jnp.max(s, axis=0, keepdims=True), sink)
    p = jnp.exp(s - m)
    esink = jnp.exp(sink - m)
    inv = 1.0 / (jnp.sum(p, axis=0, keepdims=True) + esink)
    return p * inv, esink * inv


def attn_fwd(q, kv, sinks_t, nseq, seq, name, comm=None):
    nb = seq // ATTN_BLOCK

    def body(q_ref, kv_ref, s_ref, o_ref, kvp):
        kvp[0:ATTN_BLOCK, :] = jnp.zeros((ATTN_BLOCK, 2 * KV_WIDTH), BF16)
        kvp[ATTN_BLOCK:, :] = kv_ref[...]

        def blk(n, carry):
            st = pl.multiple_of(n * ATTN_BLOCK, ATTN_BLOCK)
            qb = q_ref[pl.ds(st, ATTN_BLOCK), :]
            kw = kvp[pl.ds(st, 2 * ATTN_BLOCK), :]
            mask = _attn_mask_t(n)
            for kh in range(N_KV_HEADS):
                kk = kw[:, kh * HEAD_DIM:(kh + 1) * HEAD_DIM]
                vv = kw[:, KV_WIDTH + kh * HEAD_DIM:KV_WIDTH + (kh + 1) * HEAD_DIM]
                probs, _ = _attn_probs_t(kk, _stack_heads(qb, kh), mask, s_ref[kh:kh + 1, :])
                ot = _dot_tn(vv, probs.astype(BF16))
                for pair in range(GROUP // 2):
                    two = jnp.concatenate([ot[:, (2 * pair) * ATTN_BLOCK:(2 * pair + 1) * ATTN_BLOCK],
                                           ot[:, (2 * pair + 1) * ATTN_BLOCK:(2 * pair + 2) * ATTN_BLOCK]], axis=0)
                    col = (kh * GROUP + 2 * pair) * HEAD_DIM
                    o_ref[pl.ds(st, ATTN_BLOCK), col:col + 2 * HEAD_DIM] = two.T.astype(o_ref.dtype)
            return carry

        lax.fori_loop(0, nb, blk, 0, unroll=4)

    return _pcall(
        body, name, (nseq,),
        [pl.BlockSpec((seq, ATTN_WIDTH), lambda b: (b, 0)),
         pl.BlockSpec((seq, 2 * KV_WIDTH), lambda b: (b, 0)),
         pl.BlockSpec((8, GQ), lambda b: (0, 0))],
        pl.BlockSpec((seq, ATTN_WIDTH), lambda b: (b, 0)),
        jax.ShapeDtypeStruct((nseq * seq, ATTN_WIDTH), BF16),
        [pltpu.VMEM((ATTN_BLOCK + seq, 2 * KV_WIDTH), BF16)], (q, kv, sinks_t), ("parallel",), comm)


def attn_bwd(q, kv, sinks_t, do, nseq, seq, name, comm=None):
    nb = seq // ATTN_BLOCK

    def body(q_ref, kv_ref, s_ref, do_ref, dq_ref, dkv_ref, ds_ref, kvp, dkvp, dsacc):
        @pl.when(pl.program_id(0) == 0)
        def _():
            dsacc[...] = jnp.zeros(dsacc.shape, F32)

        kvp[0:ATTN_BLOCK, :] = jnp.zeros((ATTN_BLOCK, 2 * KV_WIDTH), BF16)
        kvp[ATTN_BLOCK:, :] = kv_ref[...]
        dkvp[...] = jnp.zeros(dkvp.shape, F32)

        def blk(n, carry):
            st = pl.multiple_of(n * ATTN_BLOCK, ATTN_BLOCK)
            qb = q_ref[pl.ds(st, ATTN_BLOCK), :]
            dob = do_ref[pl.ds(st, ATTN_BLOCK), :]
            kw = kvp[pl.ds(st, 2 * ATTN_BLOCK), :]
            mask = _attn_mask_t(n)
            for kh in range(N_KV_HEADS):
                kk = kw[:, kh * HEAD_DIM:(kh + 1) * HEAD_DIM]
                vv = kw[:, KV_WIDTH + kh * HEAD_DIM:KV_WIDTH + (kh + 1) * HEAD_DIM]
                qs = _stack_heads(qb, kh)
                dos = _stack_heads(dob, kh)
                probs, psink = _attn_probs_t(kk, qs, mask, s_ref[kh:kh + 1, :])
                dp = _dot_nt(vv, dos)
                dv = _dot(probs.astype(BF16), dos)
                rowdot = jnp.sum(probs * dp, axis=0, keepdims=True)
                dsc = (probs * (dp - rowdot) * (HEAD_DIM ** -0.5)).astype(BF16)
                dsacc[kh:kh + 1, :] += -psink * rowdot
                dk = _dot(dsc, qs)
                dqs = _dot_tn(dsc, kk)
                for g in range(GROUP):
                    col = (kh * GROUP + g) * HEAD_DIM
                    dq_ref[pl.ds(st, ATTN_BLOCK), col:col + HEAD_DIM] = (
                        dqs[g * ATTN_BLOCK:(g + 1) * ATTN_BLOCK].astype(dq_ref.dtype))
                dkvp[pl.ds(st, 2 * ATTN_BLOCK), kh * HEAD_DIM:(kh + 1) * HEAD_DIM] += dk
                dkvp[pl.ds(st, 2 * ATTN_BLOCK), KV_WIDTH + kh * HEAD_DIM:KV_WIDTH + (kh + 1) * HEAD_DIM] += dv
            return carry

        lax.fori_loop(0, nb, blk, 0, unroll=2)
        dkv_ref[...] = dkvp[ATTN_BLOCK:, :].astype(dkv_ref.dtype)

        @pl.when(pl.program_id(0) == nseq - 1)
        def _():
            for kh in range(N_KV_HEADS):
                for g in range(GROUP):
                    tot = jnp.sum(dsacc[kh:kh + 1, g * ATTN_BLOCK:(g + 1) * ATTN_BLOCK], axis=1, keepdims=True)
                    ds_ref[kh * GROUP + g:kh * GROUP + g + 1, :] = jnp.broadcast_to(tot, (1, LANES))

    seq_q = pl.BlockSpec((seq, ATTN_WIDTH), lambda b: (b, 0))
    seq_kv = pl.BlockSpec((seq, 2 * KV_WIDTH), lambda b: (b, 0))
    return _pcall(
        body, name, (nseq,),
        [seq_q, seq_kv, pl.BlockSpec((8, GQ), lambda b: (0, 0)), seq_q],
        [seq_q, seq_kv, pl.BlockSpec((N_Q_HEADS, LANES), lambda b: (0, 0))],
        [jax.ShapeDtypeStruct((nseq * seq, ATTN_WIDTH), BF16),
         jax.ShapeDtypeStruct((nseq * seq, 2 * KV_WIDTH), BF16),
         jax.ShapeDtypeStruct((N_Q_HEADS, LANES), F32)],
        [pltpu.VMEM((ATTN_BLOCK + seq, 2 * KV_WIDTH), BF16),
         pltpu.VMEM((ATTN_BLOCK + seq, 2 * KV_WIDTH), F32),
         pltpu.VMEM((8, GQ), F32)], (q, kv, sinks_t, do), ("arbitrary",), comm)


CONV_T = 128


SUBLANES = 8


def _shifted_rows(win):
    phases = [win] + [pltpu.roll(win, s, 0) for s in range(1, SUBLANES)]

    def shifted(s):
        lo = CONV_HALO - SUBLANES * (s // SUBLANES)
        return phases[s % SUBLANES][lo:lo + CONV_T]

    return shifted


def _conv_taps(win, w_ref, lanes, init):
    shifted = _shifted_rows(win)
    acc = init
    for j in range(CONV_KERNEL):
        acc = acc + w_ref[j:j + 1, lanes] * shifted(CONV_KERNEL - 1 - j)
    return acc


def _conv_block(h0p, w_ref, vec_ref, st):
    cols = []
    for cs in range(CONV_WIDTH // LANES):
        lanes = slice(cs * LANES, (cs + 1) * LANES)
        win = h0p[pl.ds(st, CONV_T + CONV_HALO), lanes]
        init = jnp.broadcast_to(vec_ref[0:1, lanes], (CONV_T, LANES))
        cols.append(_conv_taps(win, w_ref, lanes, init))
    return jnp.concatenate(cols, axis=-1)


def _glu_store(c_ref, h0p, st):
    cb = c_ref[pl.ds(st, CONV_T), :]
    h0p[pl.ds(pl.multiple_of(st + CONV_HALO, CONV_HALO), CONV_T), :] = cb[:, :CONV_WIDTH] * _sigmoid(cb[:, CONV_WIDTH:])


def conv_fwd(c, w, vec, nseq, seq, name, comm=None):
    nb = seq // CONV_T

    def body(c_ref, w_ref, vec_ref, o_ref, h1_ref, h0p):
        h0p[0:CONV_HALO, :] = jnp.zeros((CONV_HALO, CONV_WIDTH), F32)

        def blk(n, carry):
            st = pl.multiple_of(n * CONV_T, CONV_T)
            _glu_store(c_ref, h0p, st)
            h1 = _conv_block(h0p, w_ref, vec_ref, st)
            h1_ref[pl.ds(st, CONV_T), :] = h1
            mu = jnp.mean(h1, axis=-1, keepdims=True)
            xc = h1 - mu
            rstd = lax.rsqrt(jnp.mean(xc * xc, axis=-1, keepdims=True) + EPS)
            y = xc * rstd * vec_ref[1:2, :] + vec_ref[2:3, :]
            o_ref[pl.ds(st, CONV_T), :] = (y * _sigmoid(y)).astype(o_ref.dtype)
            return carry

        lax.fori_loop(0, nb, blk, 0)

    return _pcall(
        body, name, (nseq,),
        [pl.BlockSpec((seq, 2 * CONV_WIDTH), lambda b: (b, 0)),
         pl.BlockSpec((32, CONV_WIDTH), lambda b: (0, 0)),
         pl.BlockSpec((8, CONV_WIDTH), lambda b: (0, 0))],
        [pl.BlockSpec((seq, CONV_WIDTH), lambda b: (b, 0))] * 2,
        [jax.ShapeDtypeStruct((nseq * seq, CONV_WIDTH), BF16), jax.ShapeDtypeStruct((nseq * seq, CONV_WIDTH), F32)],
        [pltpu.VMEM((CONV_HALO + seq, CONV_WIDTH), F32)], (c, w, vec), ("parallel",), comm)


def conv_bwd(c, h1_saved, w, vec, dout, nseq, seq, name, comm=None):
    nb = seq // CONV_T

    def body(c_ref, h1_ref, w_ref, vec_ref, do_ref, dc_ref, dw_ref, dvec_ref, h0p, dh1p, dwacc):
        @pl.when(pl.program_id(0) == 0)
        def _():
            dwacc[...] = jnp.zeros(dwacc.shape, F32)
            dvec_ref[...] = jnp.zeros(dvec_ref.shape, F32)

        h0p[0:CONV_HALO, :] = jnp.zeros((CONV_HALO, CONV_WIDTH), F32)
        dh1p[seq:seq + CONV_HALO, :] = jnp.zeros((CONV_HALO, CONV_WIDTH), F32)

        def pass_a(n, carry):
            st = pl.multiple_of(n * CONV_T, CONV_T)
            _glu_store(c_ref, h0p, st)
            h1 = h1_ref[pl.ds(st, CONV_T), :]
            mu = jnp.mean(h1, axis=-1, keepdims=True)
            xc = h1 - mu
            rstd = lax.rsqrt(jnp.mean(xc * xc, axis=-1, keepdims=True) + EPS)
            xh = xc * rstd
            y = xh * vec_ref[1:2, :] + vec_ref[2:3, :]
            sg = _sigmoid(y)
            dy = do_ref[pl.ds(st, CONV_T), :] * (sg * (1.0 + y * (1.0 - sg)))
            dvec_ref[1:2, :] += jnp.sum(dy * xh, axis=0, keepdims=True)
            dvec_ref[2:3, :] += jnp.sum(dy, axis=0, keepdims=True)
            dxh = dy * vec_ref[1:2, :]
            dh1 = rstd * (dxh - jnp.mean(dxh, axis=-1, keepdims=True)
                          - xh * jnp.mean(dxh * xh, axis=-1, keepdims=True))
            dvec_ref[0:1, :] += jnp.sum(dh1, axis=0, keepdims=True)
            dh1p[pl.ds(st, CONV_T), :] = dh1
            return carry

        lax.fori_loop(0, nb, pass_a, 0)

        def pass_b(n, carry):
            st = pl.multiple_of(n * CONV_T, CONV_T)
            cols = []
            for cs in range(CONV_WIDTH // LANES):
                lanes = slice(cs * LANES, (cs + 1) * LANES)
                wind = dh1p[pl.ds(st, CONV_T + CONV_HALO), lanes]
                winh = h0p[pl.ds(st, CONV_T + CONV_HALO), lanes]
                d1 = wind[0:CONV_T]
                shifted_d, shifted_h = _shifted_rows(wind), _shifted_rows(winh)
                acc = jnp.zeros((CONV_T, LANES), F32)
                for j in range(CONV_KERNEL):
                    acc = acc + w_ref[j:j + 1, lanes] * shifted_d(2 + j)
                    prod = d1 * shifted_h(CONV_KERNEL - 1 - j)
                    part = prod[0:8]
                    for r in range(8, CONV_T, 8):
                        part = part + prod[r:r + 8]
                    dwacc[8 * j:8 * j + 8, lanes] += part
                cols.append(acc)
            dh0 = jnp.concatenate(cols, axis=-1)
            cb = c_ref[pl.ds(st, CONV_T), :]
            av, gt = cb[:, :CONV_WIDTH], cb[:, CONV_WIDTH:]
            sg = _sigmoid(gt)
            dc_ref[pl.ds(st, CONV_T), :] = jnp.concatenate(
                [dh0 * sg, dh0 * av * sg * (1.0 - sg)], axis=-1).astype(dc_ref.dtype)
            return carry

        lax.fori_loop(0, nb, pass_b, 0)

        @pl.when(pl.program_id(0) == nseq - 1)
        def _():
            dw_ref[...] = jnp.zeros(dw_ref.shape, F32)
            for j in range(CONV_KERNEL):
                dw_ref[j:j + 1, :] = jnp.sum(dwacc[8 * j:8 * j + 8, :], axis=0, keepdims=True)

    return _pcall(
        body, name, (nseq,),
        [pl.BlockSpec((seq, 2 * CONV_WIDTH), lambda b: (b, 0)),
         pl.BlockSpec((seq, CONV_WIDTH), lambda b: (b, 0)),
         pl.BlockSpec((32, CONV_WIDTH), lambda b: (0, 0)),
         pl.BlockSpec((8, CONV_WIDTH), lambda b: (0, 0)),
         pl.BlockSpec((seq, CONV_WIDTH), lambda b: (b, 0))],
        [pl.BlockSpec((seq, 2 * CONV_WIDTH), lambda b: (b, 0)),
         pl.BlockSpec((32, CONV_WIDTH), lambda b: (0, 0)),
         pl.BlockSpec((8, CONV_WIDTH), lambda b: (0, 0))],
        [jax.ShapeDtypeStruct((nseq * seq, 2 * CONV_WIDTH), BF16),
         jax.ShapeDtypeStruct((32, CONV_WIDTH), F32),
         jax.ShapeDtypeStruct((8, CONV_WIDTH), F32)],
        [pltpu.VMEM((CONV_HALO + seq, CONV_WIDTH), F32),
         pltpu.VMEM((seq + CONV_HALO, CONV_WIDTH), F32),
         pltpu.VMEM((8 * 32, CONV_WIDTH), F32)], (c, h1_saved, w, vec, dout), ("arbitrary",), comm)


POOL_T = 128


def _pooled_block(zpp, st, grp):
    lanes = slice(grp * LANES, (grp + 1) * LANES)
    win = zpp[pl.ds(st, POOL_T + POOL_HALO), lanes]
    acc = win
    for lvl in range(grp + 1):
        acc = acc + pltpu.roll(acc, 1 << lvl, 0)
    t = st + lax.broadcasted_iota(jnp.int32, (POOL_T, 1), 0)
    inv = 1.0 / jnp.minimum(t + 1, POOL_WINDOWS[grp]).astype(F32)
    return acc[POOL_HALO:] * inv - win[POOL_HALO:], inv


def pool_fwd(zp, wp, scale, nseq, seq, name):
    nb = seq // POOL_T

    def body(z_ref, wp_ref, sc_ref, o_ref, zpp):
        zpp[0:POOL_HALO, :] = jnp.zeros((POOL_HALO, POOL_WIDTH), F32)
        zpp[POOL_HALO:, :] = z_ref[...]

        def blk(n, carry):
            st = pl.multiple_of(n * POOL_T, POOL_T)
            for grp in range(len(POOL_WINDOWS)):
                lanes = slice(grp * LANES, (grp + 1) * LANES)
                pooled, _ = _pooled_block(zpp, st, grp)
                o_ref[pl.ds(st, POOL_T), lanes] = (
                    _dot(pooled.astype(BF16), wp_ref[grp]) * sc_ref[0:1, lanes]).astype(o_ref.dtype)
            return carry

        lax.fori_loop(0, nb, blk, 0, unroll=2)

    return pl.pallas_call(
        body, name=name, grid=(nseq,),
        in_specs=[pl.BlockSpec((seq, POOL_WIDTH), lambda b: (b, 0)),
                  pl.BlockSpec((4, LANES, LANES), lambda b: (0, 0, 0)),
                  pl.BlockSpec((1, POOL_WIDTH), lambda b: (0, 0))],
        out_specs=pl.BlockSpec((seq, POOL_WIDTH), lambda b: (b, 0)),
        out_shape=jax.ShapeDtypeStruct((nseq * seq, POOL_WIDTH), BF16),
        scratch_shapes=[pltpu.VMEM((POOL_HALO + seq, POOL_WIDTH), F32)],
        compiler_params=_params("parallel"),
    )(zp, wp, scale)


def pool_bwd(zp, wp, scale, dout, nseq, seq, name, comm=None):
    nb = seq // POOL_T

    def body(z_ref, wp_ref, sc_ref, do_ref, dz_ref, dwp_ref, dsc_ref, zpp, dpcp, negd):
        @pl.when(pl.program_id(0) == 0)
        def _():
            dwp_ref[...] = jnp.zeros(dwp_ref.shape, F32)
            dsc_ref[...] = jnp.zeros(dsc_ref.shape, F32)

        zpp[0:POOL_HALO, :] = jnp.zeros((POOL_HALO, POOL_WIDTH), F32)
        zpp[POOL_HALO:, :] = z_ref[...]
        dpcp[seq:seq + POOL_HALO, :] = jnp.zeros((POOL_HALO, POOL_WIDTH), F32)

        def pass_a(n, carry):
            st = pl.multiple_of(n * POOL_T, POOL_T)
            for grp in range(len(POOL_WINDOWS)):
                lanes = slice(grp * LANES, (grp + 1) * LANES)
                pooled, inv = _pooled_block(zpp, st, grp)
                pb = pooled.astype(BF16)
                dob = do_ref[pl.ds(st, POOL_T), lanes]
                dsc_ref[0:1, lanes] += jnp.sum(dob * _dot(pb, wp_ref[grp]), axis=0, keepdims=True)
                dpm = (dob * sc_ref[0:1, lanes]).astype(BF16)
                dwp_ref[grp] += _dot_tn(pb, dpm)
                dpooled = _dot_nt(dpm, wp_ref[grp])
                negd[pl.ds(st, POOL_T), lanes] = -dpooled
                dpcp[pl.ds(st, POOL_T), lanes] = dpooled * inv
            return carry

        lax.fori_loop(0, nb, pass_a, 0, unroll=2)

        def pass_b(n, carry):
            st = pl.multiple_of(n * POOL_T, POOL_T)
            rows = POOL_T + POOL_HALO
            for grp in range(len(POOL_WINDOWS)):
                lanes = slice(grp * LANES, (grp + 1) * LANES)
                acc = dpcp[pl.ds(st, rows), lanes]
                for lvl in range(grp + 1):
                    acc = acc + pltpu.roll(acc, rows - (1 << lvl), 0)
                dz_ref[pl.ds(st, POOL_T), lanes] = (acc[0:POOL_T] + negd[pl.ds(st, POOL_T), lanes]).astype(dz_ref.dtype)
            return carry

        lax.fori_loop(0, nb, pass_b, 0, unroll=2)

    seq_spec = pl.BlockSpec((seq, POOL_WIDTH), lambda b: (b, 0))
    return _pcall(
        body, name, (nseq,),
        [seq_spec, pl.BlockSpec((4, LANES, LANES), lambda b: (0, 0, 0)),
         pl.BlockSpec((1, POOL_WIDTH), lambda b: (0, 0)), seq_spec],
        [seq_spec, pl.BlockSpec((4, LANES, LANES), lambda b: (0, 0, 0)),
         pl.BlockSpec((8, POOL_WIDTH), lambda b: (0, 0))],
        [jax.ShapeDtypeStruct((nseq * seq, POOL_WIDTH), BF16),
         jax.ShapeDtypeStruct((4, LANES, LANES), F32),
         jax.ShapeDtypeStruct((8, POOL_WIDTH), F32)],
        [pltpu.VMEM((POOL_HALO + seq, POOL_WIDTH), F32),
         pltpu.VMEM((seq + POOL_HALO, POOL_WIDTH), F32),
         pltpu.VMEM((seq, POOL_WIDTH), F32)], (zp, wp, scale, dout), ("arbitrary",), comm)


GELU_C0 = 0.7978845608028654
GELU_C1 = 0.044715


def _gelu(xv):
    return xv * (0.5 * (1.0 + jnp.tanh(GELU_C0 * (xv + GELU_C1 * (xv * xv * xv)))))


def _gelu_grad(xv):
    t = jnp.tanh(GELU_C0 * (xv + GELU_C1 * (xv * xv * xv)))
    return 0.5 * (1.0 + t) + 0.5 * xv * (1.0 - t * t) * (GELU_C0 * (1.0 + 3.0 * GELU_C1 * xv * xv))


def _tril():
    ti = lax.broadcasted_iota(jnp.int32, (SGU_CHUNK, SGU_CHUNK), 0)
    si = lax.broadcasted_iota(jnp.int32, (SGU_CHUNK, SGU_CHUNK), 1)
    return si <= ti


def sgu_fwd(zs, ws, bst, ln, nseq, seq, name):
    nc = seq // SGU_CHUNK

    def body(z_ref, ws_ref, bs_ref, ln_ref, o_ref):
        tril = _tril()

        def blk(n, carry):
            st = pl.multiple_of(n * SGU_CHUNK, SGU_CHUNK)
            ge = _gelu(z_ref[pl.ds(st, SGU_CHUNK), :])
            uu, vv = ge[:, :SGU_WIDTH], ge[:, SGU_WIDTH:]
            mu = jnp.mean(vv, axis=-1, keepdims=True)
            xc = vv - mu
            rstd = lax.rsqrt(jnp.mean(xc * xc, axis=-1, keepdims=True) + EPS)
            vn = (xc * rstd * ln_ref[0:1, :] + ln_ref[1:2, :]).astype(BF16)
            for g in range(4):
                lanes = slice(g * LANES, (g + 1) * LANES)
                wm = jnp.where(tril, ws_ref[g], 0.0).astype(BF16)
                mixed = _dot(wm, vn[:, lanes]) + bs_ref[:, g:g + 1]
                o_ref[pl.ds(st, SGU_CHUNK), lanes] = (uu[:, lanes] * mixed).astype(o_ref.dtype)
            return carry

        lax.fori_loop(0, nc, blk, 0, unroll=2)

    return pl.pallas_call(
        body, name=name, grid=(nseq,),
        in_specs=[pl.BlockSpec((seq, 2 * SGU_WIDTH), lambda b: (b, 0)),
                  pl.BlockSpec((4, LANES, LANES), lambda b: (0, 0, 0)),
                  pl.BlockSpec((SGU_CHUNK, 4), lambda b: (0, 0)),
                  pl.BlockSpec((8, SGU_WIDTH), lambda b: (0, 0))],
        out_specs=pl.BlockSpec((seq, SGU_WIDTH), lambda b: (b, 0)),
        out_shape=jax.ShapeDtypeStruct((nseq * seq, SGU_WIDTH), BF16),
        compiler_params=_params("parallel"),
    )(zs, ws, bst, ln)


def sgu_bwd(zs, ws, bst, ln, dout, nseq, seq, name, comm=None):
    nc = seq // SGU_CHUNK

    def body(z_ref, ws_ref, bs_ref, ln_ref, do_ref, dz_ref, dws_ref, dbs_ref, dln_ref):
        @pl.when(pl.program_id(0) == 0)
        def _():
            dws_ref[...] = jnp.zeros(dws_ref.shape, F32)
            dbs_ref[...] = jnp.zeros(dbs_ref.shape, F32)
            dln_ref[...] = jnp.zeros(dln_ref.shape, F32)

        tril = _tril()

        def blk(n, carry):
            st = pl.multiple_of(n * SGU_CHUNK, SGU_CHUNK)
            zv = z_ref[pl.ds(st, SGU_CHUNK), :]
            ge = _gelu(zv)
            uu, vv = ge[:, :SGU_WIDTH], ge[:, SGU_WIDTH:]
            mu = jnp.mean(vv, axis=-1, keepdims=True)
            xc = vv - mu
            rstd = lax.rsqrt(jnp.mean(xc * xc, axis=-1, keepdims=True) + EPS)
            xh = xc * rstd
            vn = (xh * ln_ref[0:1, :] + ln_ref[1:2, :]).astype(BF16)
            dob = do_ref[pl.ds(st, SGU_CHUNK), :]
            du_cols, dvn_cols = [], []
            for g in range(4):
                lanes = slice(g * LANES, (g + 1) * LANES)
                wm = jnp.where(tril, ws_ref[g], 0.0).astype(BF16)
                mixed = _dot(wm, vn[:, lanes]) + bs_ref[:, g:g + 1]
                du_cols.append(dob[:, lanes] * mixed)
                dmix = dob[:, lanes] * uu[:, lanes]
                dbs_ref[g] += jnp.broadcast_to(jnp.sum(dmix, axis=-1, keepdims=True), (SGU_CHUNK, LANES))
                dmb = dmix.astype(BF16)
                dws_ref[g] += jnp.where(tril, _dot_nt(dmb, vn[:, lanes]), 0.0)
                dvn_cols.append(_dot_tn(wm, dmb))
            dvn = jnp.concatenate(dvn_cols, axis=-1)
            dln_ref[0:1, :] += jnp.sum(dvn * xh, axis=0, keepdims=True)
            dln_ref[1:2, :] += jnp.sum(dvn, axis=0, keepdims=True)
            dxh = dvn * ln_ref[0:1, :]
            dv = rstd * (dxh - jnp.mean(dxh, axis=-1, keepdims=True)
                         - xh * jnp.mean(dxh * xh, axis=-1, keepdims=True))
            dge = jnp.concatenate(du_cols + [dv], axis=-1)
            dz_ref[pl.ds(st, SGU_CHUNK), :] = (dge * _gelu_grad(zv)).astype(dz_ref.dtype)
            return carry

        lax.fori_loop(0, nc, blk, 0, unroll=2)

    w_spec = pl.BlockSpec((4, LANES, LANES), lambda b: (0, 0, 0))
    ln_spec = pl.BlockSpec((8, SGU_WIDTH), lambda b: (0, 0))
    return _pcall(
        body, name, (nseq,),
        [pl.BlockSpec((seq, 2 * SGU_WIDTH), lambda b: (b, 0)), w_spec,
         pl.BlockSpec((SGU_CHUNK, 4), lambda b: (0, 0)), ln_spec,
         pl.BlockSpec((seq, SGU_WIDTH), lambda b: (b, 0))],
        [pl.BlockSpec((seq, 2 * SGU_WIDTH), lambda b: (b, 0)), w_spec, w_spec, ln_spec],
        [jax.ShapeDtypeStruct((nseq * seq, 2 * SGU_WIDTH), BF16),
         jax.ShapeDtypeStruct((4, LANES, LANES), F32),
         jax.ShapeDtypeStruct((4, LANES, LANES), F32),
         jax.ShapeDtypeStruct((8, SGU_WIDTH), F32)],
        [], (zs, ws, bst, ln, dout), ("arbitrary",), comm)


def _ew_rows(rows, cols, nbuf):
    t = _row_tile(rows, 1024)
    while t > 8 and t * cols * 4 * nbuf * 2 > 24 * 2**20:
        t //= 2
    return t


def adamw(w, g, m, v, name):
    layers, rows, cols = w.shape
    tr = _ew_rows(rows, cols, 7)

    def body(w_ref, g_ref, m_ref, v_ref, d_ref, mo_ref, vo_ref):
        gv = g_ref[...]
        mn = ADAM_B1 * m_ref[...] + (1.0 - ADAM_B1) * gv
        vn = ADAM_B2 * v_ref[...] + (1.0 - ADAM_B2) * (gv * gv)
        m_hat = mn / (1.0 - ADAM_B1 ** ADAM_STEP)
        v_hat = vn / (1.0 - ADAM_B2 ** ADAM_STEP)
        d_ref[...] = -ADAM_LR * (m_hat / (jnp.sqrt(v_hat) + ADAM_EPS) + ADAM_WD * w_ref[...])
        mo_ref[...] = mn
        vo_ref[...] = vn

    spec = pl.BlockSpec((1, tr, cols), lambda l, i: (l, i, 0))
    return pl.pallas_call(
        body, name=name, grid=(layers, rows // tr),
        in_specs=[spec] * 4, out_specs=[spec] * 3,
        out_shape=[jax.ShapeDtypeStruct(w.shape, F32)] * 3,
        compiler_params=_params("parallel", "parallel"),
    )(w, g, m, v)


def adamw_many(ws, gs, ms, vs, name):
    n = len(ws)

    def body(*refs):
        w_refs, g_refs, m_refs, v_refs = refs[:n], refs[n:2 * n], refs[2 * n:3 * n], refs[3 * n:4 * n]
        d_refs, mo_refs, vo_refs = refs[4 * n:5 * n], refs[5 * n:6 * n], refs[6 * n:7 * n]
        for i in range(n):
            gv = g_refs[i][...]
            mn = ADAM_B1 * m_refs[i][...] + (1.0 - ADAM_B1) * gv
            vn = ADAM_B2 * v_refs[i][...] + (1.0 - ADAM_B2) * (gv * gv)
            m_hat = mn / (1.0 - ADAM_B1 ** ADAM_STEP)
            v_hat = vn / (1.0 - ADAM_B2 ** ADAM_STEP)
            d_refs[i][...] = -ADAM_LR * (m_hat / (jnp.sqrt(v_hat) + ADAM_EPS) + ADAM_WD * w_refs[i][...])
            mo_refs[i][...] = mn
            vo_refs[i][...] = vn

    vmem = pl.BlockSpec(memory_space=pltpu.VMEM)
    shapes = [jax.ShapeDtypeStruct(w.shape, F32) for w in ws]
    res = pl.pallas_call(
        body, name=name, in_specs=[vmem] * (4 * n), out_specs=[vmem] * (3 * n), out_shape=shapes * 3,
        compiler_params=pltpu.CompilerParams(vmem_limit_bytes=VMEM_LIMIT),
    )(*ws, *gs, *ms, *vs)
    return res[:n], res[n:2 * n], res[2 * n:]


def add_cast(a, b, name, dtype=BF16):
    nslab, rows, cols = a.shape
    tr = _ew_rows(rows, cols, 3)

    def body(a_ref, b_ref, o_ref):
        o_ref[...] = (a_ref[...] + b_ref[...]).astype(dtype)

    spec = pl.BlockSpec((1, tr, cols), lambda k, i: (k, i, 0))
    return pl.pallas_call(
        body, name=name, grid=(nslab, rows // tr),
        in_specs=[spec, spec], out_specs=spec,
        out_shape=jax.ShapeDtypeStruct(a.shape, dtype),
        compiler_params=_params("parallel", "parallel"),
    )(a, b)


def pair_sum(t, got, core, name):
    nslab, h, cols = got.shape
    tr = _ew_rows(h, cols, 3)
    nb = h // tr

    def body(c_ref, a_ref, b_ref, o_ref):
        o_ref[...] = (a_ref[...] + b_ref[...]).astype(BF16)

    spec = pl.BlockSpec((1, tr, cols), lambda k, i, c: (k, i, 0))
    return pl.pallas_call(
        body, name=name,
        grid_spec=pltpu.PrefetchScalarGridSpec(
            num_scalar_prefetch=1, grid=(nslab, nb),
            in_specs=[pl.BlockSpec((1, tr, cols), lambda k, i, c: (k, c[0] * nb + i, 0)), spec],
            out_specs=spec),
        out_shape=jax.ShapeDtypeStruct(got.shape, BF16),
        compiler_params=_params("parallel", "parallel"),
    )(core, t, got)


def chip_sum(sums, parts, place, name):
    npart, h, cols = parts.shape
    tr = _ew_rows(h, cols, npart + 2)
    nb = h // tr

    def body(c_ref, own_ref, p_ref, o_ref):
        acc = own_ref[0].astype(F32)
        for j in range(npart):
            acc = acc + p_ref[j].astype(F32)
        o_ref[...] = acc

    return pl.pallas_call(
        body, name=name,
        grid_spec=pltpu.PrefetchScalarGridSpec(
            num_scalar_prefetch=1, grid=(nb,),
            in_specs=[pl.BlockSpec((1, tr, cols), lambda i, c: (c[1], i, 0)),
                      pl.BlockSpec((npart, tr, cols), lambda i, c: (0, i, 0))],
            out_specs=pl.BlockSpec((tr, cols), lambda i, c: (c[0] * nb + i, 0))),
        out_shape=jax.ShapeDtypeStruct((2 * h, cols), F32),
        compiler_params=_params("parallel"),
    )(place, sums, parts)


def sum_parts(parts, name, first=None):
    npart, rows, cols = parts.shape
    tr = _ew_rows(rows, cols, npart + 2)

    def body(*refs):
        p_ref, o_ref = refs[-2], refs[-1]
        acc = p_ref[0].astype(F32) if first is None else refs[0][...].astype(F32) + p_ref[0].astype(F32)
        for j in range(1, npart):
            acc = acc + p_ref[j].astype(F32)
        o_ref[...] = acc

    row = pl.BlockSpec((tr, cols), lambda i: (i, 0))
    ins = [parts] if first is None else [first, parts]
    return pl.pallas_call(
        body, name=name, grid=(rows // tr,),
        in_specs=([] if first is None else [row]) + [pl.BlockSpec((npart, tr, cols), lambda i: (0, i, 0))],
        out_specs=row,
        out_shape=jax.ShapeDtypeStruct((rows, cols), F32),
        compiler_params=_params("parallel"),
    )(*ins)


ANY = pl.BlockSpec(memory_space=pl.ANY)
MESH = pl.DeviceIdType.MESH


def _me():
    return lax.axis_index("x"), lax.axis_index("y"), lax.axis_index("c")


def _flip(pos, rel):
    return tuple(1 - p if f else p for p, f in zip(pos, rel))


SIBLING = (0, 0, 1)
OTHER_CHIPS = ((1, 0, 0), (0, 1, 0), (1, 1, 0))


def _chip_of(pos, rel=(0, 0, 0)):
    px, py, _ = _flip(pos, rel)
    return 2 * px + py


def allgather_blocks(shards, name):
    nt = len(shards)
    hs = [s.shape[0] // 2 for s in shards]

    def body(*refs):
        ins, outs = refs[:nt], refs[nt:2 * nt]
        send_sems, recv_sems, loc_sems = refs[2 * nt:]
        pos = _me()
        x, y, c = pos

        def block_id(rel):
            px, py, pc = _flip(pos, rel)
            return 4 * px + 2 * py + pc

        def copy(t, k, block_rel, to_rel, src=None):
            dst = outs[t].at[block_id(block_rel)]
            return pltpu.make_async_remote_copy(
                src_ref=dst if src is None else src, dst_ref=dst,
                send_sem=send_sems.at[t * 7 + k], recv_sem=recv_sems.at[t * 7 + k],
                device_id=_flip(pos, to_rel), device_id_type=MESH)

        own = [ins[t].at[pl.ds(c * hs[t], hs[t])] for t in range(nt)]
        mine = [pltpu.make_async_copy(own[t], outs[t].at[block_id((0, 0, 0))], loc_sems.at[t]) for t in range(nt)]
        for cp in mine:
            cp.start()
        first = []
        for t in range(nt):
            first.append(copy(t, 0, (0, 0, 0), SIBLING, src=own[t]))
            first += [copy(t, 1 + j, (0, 0, 0), rel, src=own[t]) for j, rel in enumerate(OTHER_CHIPS)]
        for cp in first:
            cp.start()
        passed = []
        for j, rel in enumerate(OTHER_CHIPS):
            for t in range(nt):
                copy(t, 1 + j, rel, (0, 0, 0)).wait_recv()
                fwd = copy(t, 4 + j, rel, SIBLING)
                fwd.start()
                passed.append(fwd)
        for t in range(nt):
            copy(t, 0, SIBLING, (0, 0, 0)).wait_recv()
            for j, rel in enumerate(OTHER_CHIPS):
                copy(t, 4 + j, (rel[0], rel[1], 1), (0, 0, 0)).wait_recv()
        for cp in first + passed:
            cp.wait_send()
        for cp in mine:
            cp.wait()

    return pl.pallas_call(
        body, name=name,
        in_specs=[ANY] * nt, out_specs=[ANY] * nt,
        out_shape=[jax.ShapeDtypeStruct((N_DEV, h, s.shape[1]), s.dtype) for h, s in zip(hs, shards)],
        scratch_shapes=[pltpu.SemaphoreType.DMA((7 * nt,)), pltpu.SemaphoreType.DMA((7 * nt,)),
                        pltpu.SemaphoreType.DMA((nt,))],
    )(*shards)


def _block_id(pos, rel=(0, 0, 0)):
    px, py, pc = _flip(pos, rel)
    return 4 * px + 2 * py + pc


def gather_first_hop(shards):
    hs = [s.shape[0] // 2 for s in shards]

    def plan(ins, outs, pos):
        me = _block_id(pos)
        remote = []
        for i, o, h in zip(ins, outs, hs):
            own = i.at[pl.ds(pos[2] * h, h)]
            remote += [(rel, own, o.at[me]) for rel in (SIBLING,) + OTHER_CHIPS]
        return remote

    return Comm(shards, [((N_DEV, h, s.shape[1]), s.dtype) for h, s in zip(hs, shards)], plan, 4 * len(shards))


def gather_second_hop(gathered):
    def plan(ins, outs, pos):
        remote = []
        for i, o in zip(ins, outs):
            for rel in OTHER_CHIPS:
                blk = _block_id(pos, rel)
                remote.append((SIBLING, i.at[blk], o.at[blk]))
        return remote

    return Comm(gathered, [(g.shape, g.dtype) for g in gathered], plan, 3 * len(gathered),
                aliases={i: i for i in range(len(gathered))})


def join_comm(bufs):
    def plan(ins, outs, pos):
        remote = []
        for i, o in zip(ins, outs):
            h = i.shape[0] // 2
            rows = pl.ds(pl.multiple_of(pos[2] * h, SUBLANES), h)
            remote.append((SIBLING, i.at[rows], o.at[rows]))
        return remote

    return Comm(list(bufs), [(b.shape, b.dtype) for b in bufs], plan, len(bufs),
                aliases={i: i for i in range(len(bufs))})


def give_half_comm(ts, plain=()):
    nt = len(ts)

    def plan(ins, outs, pos):
        remote = []
        for i, o in zip(ins[:nt], outs[:nt]):
            h = o.shape[1]
            remote.append((SIBLING, i.at[:, pl.ds((1 - pos[2]) * h, h)], o))
        return remote + [(SIBLING, i, o) for i, o in zip(ins[nt:], outs[nt:])]

    shapes = [((t.shape[0], t.shape[1] // 2, t.shape[2]), t.dtype) for t in ts] + [(v.shape, v.dtype) for v in plain]
    return Comm(list(ts) + list(plain), shapes, plan, nt + len(plain))


def chip_scatter_comm(xs, shared=None):
    nx = len(xs)

    def plan(ins, outs, pos):
        me = _chip_of(pos)
        remote = []
        for i, o in zip(ins[:nx], outs[:nx]):
            remote += [(rel, i.at[_chip_of(pos, rel)], o.at[j]) for j, rel in enumerate(OTHER_CHIPS)]
        if shared is not None:
            remote += [(rel, ins[nx], outs[nx].at[me]) for rel in OTHER_CHIPS]
        return remote

    shapes = [((3,) + v.shape[1:], v.dtype) for v in xs]
    if shared is not None:
        shapes.append(((N_CHIPS,) + shared.shape, shared.dtype))
    return Comm(list(xs) + ([] if shared is None else [shared]), shapes, plan, 3 * nx + (0 if shared is None else 3))


def tail_reduce(t, small, name):
    nslab, h2, cols = t.shape
    h = h2 // 2
    rows = small.shape[0]

    def body(t_ref, small_ref, full_ref, ssum_ref,
             got_pair, sums, got_chips, small_got, small_pair, small_chips, send_sems, recv_sems):
        pos = _me()
        core = pos[2]
        me = _chip_of(pos)

        def copy(i, rel, src, dst):
            return pltpu.make_async_remote_copy(src_ref=src, dst_ref=dst, send_sem=send_sems.at[i],
                                                recv_sem=recv_sems.at[i], device_id=_flip(pos, rel),
                                                device_id_type=MESH)

        pair = [copy(0, SIBLING, t_ref.at[:, pl.ds(pl.multiple_of((1 - core) * h, SUBLANES), h)], got_pair),
                copy(1, SIBLING, small_ref, small_got)]
        for cp in pair:
            cp.start()
        for cp in pair:
            cp.wait()
        for k in range(nslab):
            sums[k] = (t_ref[k, pl.ds(pl.multiple_of(core * h, SUBLANES), h), :] + got_pair[k]).astype(BF16)
        small_pair[...] = small_ref[...] + small_got[...]

        chips = []
        for j, rel in enumerate(OTHER_CHIPS):
            chips.append(copy(2 + j, rel, sums.at[_chip_of(pos, rel)], got_chips.at[j]))
            chips.append(copy(5 + j, rel, small_pair, small_chips.at[me]))
        for cp in chips:
            cp.start()
        small_chips[me] = small_pair[...]
        for cp in chips:
            cp.wait()
        acc = sums[me].astype(F32)
        for j in range(len(OTHER_CHIPS)):
            acc = acc + got_chips[j].astype(F32)
        mine = full_ref.at[pl.ds(pl.multiple_of(core * h, SUBLANES), h)]
        mine[...] = acc
        tot = small_chips[0]
        for k in range(1, N_CHIPS):
            tot = tot + small_chips[k]
        ssum_ref[...] = tot

        join = copy(8, SIBLING, mine, mine)
        join.start()
        join.wait()

    vmem = pl.BlockSpec(memory_space=pltpu.VMEM)
    return pl.pallas_call(
        body, name=name, in_specs=[vmem, vmem], out_specs=[vmem, vmem],
        out_shape=[jax.ShapeDtypeStruct((h2, cols), F32), jax.ShapeDtypeStruct((rows, LANES), F32)],
        scratch_shapes=[pltpu.VMEM((nslab, h, cols), F32), pltpu.VMEM((nslab, h, cols), BF16),
                        pltpu.VMEM((3, h, cols), BF16), pltpu.VMEM((rows, LANES), F32),
                        pltpu.VMEM((rows, LANES), F32), pltpu.VMEM((N_CHIPS, rows, LANES), F32),
                        pltpu.SemaphoreType.DMA((9,)), pltpu.SemaphoreType.DMA((9,))],
        compiler_params=pltpu.CompilerParams(vmem_limit_bytes=VMEM_LIMIT),
    )(t, small)


PACK_ROWS = 256


def _pack(arrs):
    parts, layout = [], []
    row = 0
    for a in arrs:
        flat = a.reshape(-1).astype(F32)
        size = flat.shape[0]
        rows = -(-size // (8 * LANES)) * 8
        flat = jnp.pad(flat, (0, rows * LANES - size))
        parts.append(flat.reshape(rows, LANES))
        layout.append((row, rows, size, a.shape))
        row += rows
    if row % PACK_ROWS:
        parts.append(jnp.zeros((PACK_ROWS - row % PACK_ROWS, LANES), F32))
    return jnp.concatenate(parts, axis=0), layout


def _unpack(packed, layout):
    return [packed[r0:r0 + rows].reshape(-1)[:size].reshape(shape) for r0, rows, size, shape in layout]


SMALL_REPL = ['mix_norm', 'a_b_in', 'a_sinks', 'a_conv_b', 'a_cln_g', 'a_cln_b', 'c_w_pool', 'c_w_s', 'c_b_s',
              'ffn_norm', 'final_norm']
SMALL_SHARD = ['a_conv_w', 'c_pool_scale', 'c_sln_g', 'c_sln_b']
BIG = ['a_w_in', 'a_w_out', 'c_w_in', 'c_w_out', 'ffn_w_gate', 'ffn_w_up', 'ffn_w_down']
TRANSPOSED = ('a_w_in', 'ffn_w_gate', 'ffn_w_up')
BIG_COL_SHARDED = {'c_w_in'}


def _full_weight(name, g8):
    _, h, cols = g8.shape
    g4 = g8.reshape(N_CHIPS, 2 * h, cols)
    if name not in BIG_COL_SHARDED:
        return g4.reshape(-1, cols)
    return jnp.transpose(g4, (1, 0, 2)).reshape(2 * h, N_CHIPS * cols)


def _to_shard_major(name, f):
    if name not in BIG_COL_SHARDED:
        return f.reshape(N_CHIPS, f.shape[0] // N_CHIPS, f.shape[1])
    r, cfull = f.shape
    return jnp.transpose(f.reshape(r, N_CHIPS, cfull // N_CHIPS), (1, 0, 2))


def kernel(*args):
    a = dict(zip(IN_NAMES, args))
    bl, seq, _ = a['x'].shape
    n = bl * seq
    x = a['x'].reshape(n, D_MODEL)
    target = a['loss_target'].reshape(n, D_MODEL)
    xi, yi, ci = _me()
    chip = 2 * xi + yi

    shard = {'a_w_in': a['a_w_in'][0].T, 'a_w_out': a['a_w_out'][0], 'c_w_in': a['c_w_in'][0], 'c_w_out': a['c_w_out'][0]}
    for layer in range(2):
        shard['gate' + str(layer)] = a['ffn_w_gate'][layer].T
        shard['up' + str(layer)] = a['ffn_w_up'][layer].T
        shard['down' + str(layer)] = a['ffn_w_down'][layer]
    shard = {k: v.astype(BF16) for k, v in shard.items()}
    core = ci.astype(jnp.int32).reshape(1)
    place = jnp.stack([ci, chip]).astype(jnp.int32)
    block_id = 4 * xi + 2 * yi + ci

    def first_hop(*names):
        return gather_first_hop([shard[k] for k in names])

    def finish(name, g8):
        h = shard[name].shape[0] // 2
        own = lax.dynamic_slice_in_dim(shard[name], ci * h, h, axis=0)
        return _full_weight(name, lax.dynamic_update_slice_in_dim(g8, own[None], block_id, axis=0))

    a_w_in_t = _full_weight('a_w_in', allgather_blocks([shard['a_w_in']], "gather_a_w_in")[0])
    in0_width = a_w_in_t.shape[0]
    small_shard_pack, small_shard_layout = _pack([a[k] for k in SMALL_SHARD])
    hop_a = first_hop('a_w_out', 'c_w_out')
    hop_s = chip_scatter_comm([], shared=small_shard_pack)
    mix_norm, ffn_norm = a['mix_norm'], a['ffn_norm']
    (hn0, q, kv, cc), outs = norm_inproj(
        x, mix_norm[0:1], a_w_in_t, a['a_b_in'],
        [(0, ATTN_WIDTH), (ATTN_WIDTH, ATTN_WIDTH + 2 * KV_WIDTH), (ATTN_WIDTH + 2 * KV_WIDTH, in0_width)],
        [BF16, BF16, F32], "in_proj0", comm=hop_a + hop_s, w_transposed=True)
    got_a, (ss,) = hop_a.split(outs, hop_s)
    ss = lax.dynamic_update_slice_in_dim(ss, small_shard_pack[None], chip, axis=0)
    ss_full = []
    for r0, rows, size, shape in small_shard_layout:
        per_chip = ss[:, r0:r0 + rows].reshape(N_CHIPS, -1)[:, :size].reshape((N_CHIPS,) + shape)
        ss_full.append(jnp.concatenate([per_chip[k] for k in range(N_CHIPS)], axis=-1))
    a_conv_w, c_pool_scale, c_sln_g, c_sln_b = [v[0] for v in ss_full]

    conv_taps = jnp.pad(a_conv_w, ((0, 32 - CONV_KERNEL), (0, 0)))
    conv_vec = jnp.pad(jnp.stack([a['a_conv_b'][0], a['a_cln_g'][0], a['a_cln_b'][0]]), ((0, 5), (0, 0)))
    sinks_b = jnp.pad(jnp.repeat(a['a_sinks'][0].reshape(N_KV_HEADS, GROUP), ATTN_BLOCK, axis=1), ((0, 6), (0, 0)))
    w_pool_bf = a['c_w_pool'][0].astype(BF16)
    pool_scale = c_pool_scale.reshape(1, POOL_WIDTH)
    w_s = a['c_w_s'][0]
    b_s_t = a['c_b_s'][0].T
    sgu_ln = jnp.pad(jnp.stack([c_sln_g, c_sln_b]), ((0, 6), (0, 0)))
    final_norm = a['final_norm'].reshape(1, D_MODEL)

    hop_b, pass_a = first_hop('gate0', 'c_w_in'), gather_second_hop(got_a)
    attn, outs = attn_fwd(q, kv, sinks_b, bl, seq, "attn_fwd", comm=hop_b + pass_a)
    got_b, done = hop_b.split(outs, pass_a)
    a_w_out, c_w_out = finish('a_w_out', done[0]), finish('c_w_out', done[1])

    hop_c, pass_b = first_hop('up0', 'down0'), gather_second_hop(got_b)
    (conv, conv_h1), outs = conv_fwd(cc, conv_taps, conv_vec, bl, seq, "conv_fwd", comm=hop_c + pass_b)
    got_c, done = hop_c.split(outs, pass_b)
    wg0, c_w_in = finish('gate0', done[0]), finish('c_w_in', done[1])

    h1, done = out_proj(x, attn, conv, a_w_out, "out_proj0", comm=gather_second_hop(got_c))
    wu0, wd0 = finish('up0', done[0]), finish('down0', done[1])

    (hnf0, g0, u0), got_e = ffn_gate_up(h1, ffn_norm[0:1], wg0, wu0, "ffn_gate_up0",
                                        comm=first_hop('gate1', 'up1', 'down1'))

    h2, done = ffn_down(h1, g0, u0, wd0, "ffn_down0", comm=gather_second_hop(got_e))
    wg1, wu1, wd1 = finish('gate1', done[0]), finish('up1', done[1]), finish('down1', done[2])
    wg, wu, wd = [wg0, wg1], [wu0, wu1], [wd0, wd1]

    (hn1, zp, zs), _ = norm_inproj(
        h2, mix_norm[1:2], c_w_in, jnp.zeros((1, c_w_in.shape[1]), F32),
        [(0, POOL_WIDTH), (POOL_WIDTH, c_w_in.shape[1])], [F32, F32], "in_proj1")
    pool = pool_fwd(zp, w_pool_bf, pool_scale, bl, seq, "pool_fwd")
    sgu = sgu_fwd(zs, w_s, b_s_t, sgu_ln, bl, seq, "sgu_fwd")
    h3, _ = out_proj(h2, pool, sgu, c_w_out, "out_proj1")
    (hnf1, g1, u1), _ = ffn_gate_up(h3, ffn_norm[1:2], wg1, wu1, "ffn_gate_up1")
    h4, _ = ffn_down(h3, g1, u1, wd1, "ffn_down1")

    dh4, d_final_norm, loss_local = loss_head(h4, final_norm, target, "loss_head")

    grads = {}
    pieces = {}

    def slabs_of(names, fulls):
        return [_to_shard_major(k, fulls[k]) for k in names]

    def pair_sums_of(names, slabs, gots):
        return [pair_sum(t, gt, core, "pair_sum_" + k) for k, t, gt in zip(names, slabs, gots)]

    def chip_sums_of(names, sums, from_chips):
        return [chip_sum(s, p, place, "chip_sum_" + k) for k, p, s in zip(names, from_chips, sums)]

    (dg, du, act), _ = ffn_down_bwd(dh4, g1, u1, wd[1], "ffn_down_bwd1")
    full1 = {'down1': mm_tn(act, dh4, "dw_down1"), 'gate1': mm_tn(dg, hnf1, "dw_gate1"),
             'up1': mm_tn(du, hnf1, "dw_up1")}
    names1 = ['gate1', 'up1', 'down1']
    slabs1 = slabs_of(names1, full1)
    dh3, d_ffn_norm1, got1 = proj_rms_bwd([dg, du], [wg[1], wu[1]], h3, ffn_norm[1:2], dh4, 1, "ffn_up_bwd1",
                                          tm_pref=512, w_transposed=True, comm=give_half_comm(slabs1))
    sums1 = pair_sums_of(names1, slabs1, got1)
    d_pool, d_sgu = out_proj_bwd(dh3, c_w_out, [F32, F32], "out_proj_bwd1")
    full1['c_w_out'] = jnp.concatenate([mm_tn(pool, dh3, "dw_out1_pool"), mm_tn(sgu, dh3, "dw_out1_sgu")], axis=0)
    (dzp, d_w_pool, d_pool_scale), from_gate = pool_bwd(zp, w_pool_bf, pool_scale, d_pool, bl, seq, "pool_bwd",
                                                        comm=chip_scatter_comm(sums1[0:1]))
    (dzs, d_w_s, d_b_s_b, d_sgu_ln), from_up = sgu_bwd(zs, w_s, b_s_t, sgu_ln, d_sgu, bl, seq, "sgu_bwd",
                                                       comm=chip_scatter_comm(sums1[1:2]))
    full1['c_w_in'] = jnp.concatenate([mm_tn(hn1, dzp, "dw_in1_pool"), mm_tn(hn1, dzs, "dw_in1_sgu")], axis=1)
    names1b = ['c_w_out', 'c_w_in']
    slabs1b = slabs_of(names1b, full1)
    heavy_pack, heavy_layout = _pack([d_w_pool[None], d_w_s[None]])
    chips_down, pair1b = chip_scatter_comm(sums1[2:3]), give_half_comm(slabs1b, plain=[heavy_pack])
    dh2, d_mix_norm1, outs = proj_rms_bwd([dzp, dzs], [c_w_in[:, :POOL_WIDTH], c_w_in[:, POOL_WIDTH:]], h2,
                                          mix_norm[1:2], dh3, 1, "in_proj_bwd1", comm=chips_down + pair1b)
    from_down, got1b = chips_down.split(outs, pair1b)
    mine1 = chip_sums_of(names1, sums1, from_gate + from_up + from_down)
    sums1b = pair_sums_of(names1b, slabs1b, got1b[:2])
    heavy_pair = add_cast(heavy_pack[None], got1b[2][None], "pair_sum_heavy", dtype=F32)[0]

    join1, chips1b = join_comm(mine1), chip_scatter_comm(sums1b, shared=heavy_pair)
    (dg, du, act), outs = ffn_down_bwd(dh2, g0, u0, wd[0], "ffn_down_bwd0", comm=join1 + chips1b)
    whole1, from_chips1b = join1.split(outs, chips1b)
    pieces.update(dict(zip(names1, whole1)))
    mine1b = chip_sums_of(names1b, sums1b, from_chips1b[:2])
    heavy_chips = lax.dynamic_update_slice_in_dim(from_chips1b[2], heavy_pair[None], chip, axis=0)
    grads['c_w_pool'], grads['c_w_s'] = _unpack(sum_parts(heavy_chips, "heavy_sum"), heavy_layout)
    full0 = {'down0': mm_tn(act, dh2, "dw_down0"), 'gate0': mm_tn(dg, hnf0, "dw_gate0"),
             'up0': mm_tn(du, hnf0, "dw_up0")}
    names0 = ['gate0', 'up0', 'down0']
    slabs0 = slabs_of(names0, full0)
    join1b, pair0 = join_comm(mine1b), give_half_comm(slabs0)
    dh1, d_ffn_norm0, outs = proj_rms_bwd([dg, du], [wg[0], wu[0]], h1, ffn_norm[0:1], dh2, 1, "ffn_up_bwd0",
                                          tm_pref=512, comm=join1b + pair0, w_transposed=True)
    whole1b, got0 = join1b.split(outs, pair0)
    pieces.update(dict(zip(names1b, whole1b)))
    sums0 = pair_sums_of(names0, slabs0, got0)

    d_attn, d_conv = out_proj_bwd(dh1, a_w_out, [BF16, F32], "out_proj_bwd0")
    full_o = {'a_w_out': jnp.concatenate([mm_tn(attn, dh1, "dw_out0_attn"), mm_tn(conv, dh1, "dw_out0_conv")], axis=0)}
    slabs_o = slabs_of(['a_w_out'], full_o)
    chips0, pair_o = chip_scatter_comm(sums0), give_half_comm(slabs_o)
    (dq, dkv, d_sinks_b), outs = attn_bwd(q, kv, sinks_b, d_attn, bl, seq, "attn_bwd", comm=chips0 + pair_o)
    from_chips0, got_o = chips0.split(outs, pair_o)
    mine0 = chip_sums_of(names0, sums0, from_chips0)
    sums_o = pair_sums_of(['a_w_out'], slabs_o, got_o)
    join0, chips_o = join_comm(mine0), chip_scatter_comm(sums_o)
    (dcc, d_conv_taps, d_conv_vec), outs = conv_bwd(cc, conv_h1, conv_taps, conv_vec, d_conv, bl, seq, "conv_bwd",
                                                    comm=join0 + chips_o)
    whole0, from_chips_o = join0.split(outs, chips_o)
    pieces.update(dict(zip(names0, whole0)))
    mine_o = chip_sums_of(['a_w_out'], sums_o, from_chips_o)
    kq, kk = ATTN_WIDTH, ATTN_WIDTH + 2 * KV_WIDTH
    grad_x, d_mix_norm0, _ = proj_rms_bwd([dq, dkv, dcc], [a_w_in_t[:kq], a_w_in_t[kq:kk], a_w_in_t[kk:]], x,
                                          mix_norm[0:1], dh1, 1, "in_proj_bwd0", w_transposed=True)
    dw_q, db_q = mm_tn(dq, hn0, "dw_in0_q", xsum=True)
    dw_kv, db_kv = mm_tn(dkv, hn0, "dw_in0_kv", xsum=True)
    (dw_c, db_c), whole_o = mm_tn(dcc, hn0, "dw_in0_c", xsum=True, comm=join_comm(mine_o))
    pieces['a_w_out'] = whole_o[0]
    d_a_b_in = jnp.concatenate([db_q, db_kv, db_c], axis=0)
    slabs_i = slabs_of(['a_w_in'], {'a_w_in': jnp.concatenate([dw_q, dw_kv, dw_c], axis=0)})

    small_full = {
        'mix_norm': jnp.stack([d_mix_norm0, d_mix_norm1]), 'a_b_in': d_a_b_in[None], 'a_sinks': d_sinks_b[:, 0][None],
        'a_conv_w': d_conv_taps[:CONV_KERNEL][None], 'a_conv_b': d_conv_vec[0][None], 'a_cln_g': d_conv_vec[1][None],
        'a_cln_b': d_conv_vec[2][None], 'c_pool_scale': d_pool_scale[0][None],
        'c_sln_g': d_sgu_ln[0][None], 'c_sln_b': d_sgu_ln[1][None],
        'c_b_s': d_b_s_b[:, :, 0][None], 'ffn_norm': jnp.stack([d_ffn_norm0, d_ffn_norm1]),
        'final_norm': d_final_norm, 'loss': loss_local.reshape(1)}
    small_names = SMALL_REPL + SMALL_SHARD
    tail_names = [k for k in small_names if k in small_full] + ['loss']
    small_pack, small_layout = _pack([small_full[k] for k in tail_names])

    pieces['a_w_in'], small_sum = tail_reduce(slabs_i[0], small_pack, "tail_reduce")

    for k in ('a_w_in', 'a_w_out', 'c_w_in', 'c_w_out'):
        grads[k] = pieces[k][None]
    for short, key in (('gate', 'ffn_w_gate'), ('up', 'ffn_w_up'), ('down', 'ffn_w_down')):
        grads[key] = jnp.stack([pieces[short + '0'], pieces[short + '1']])

    for k, g in zip(tail_names, _unpack(small_sum, small_layout)):
        if k in SMALL_SHARD:
            width = a[k].shape[-1]
            g = lax.dynamic_slice_in_dim(g, chip * width, width, axis=g.ndim - 1)
        grads[k] = g
    loss = grads.pop('loss')[0]

    delta, new_m, new_v = {}, {}, {}
    for k in BIG:
        if k in TRANSPOSED:
            flip = lambda t: jnp.swapaxes(t, 1, 2)
            d, m, v = adamw(flip(a[k]), grads[k], flip(a['m_' + k]), flip(a['v_' + k]), "adamw_" + k)
            grads[k], delta[k], new_m[k], new_v[k] = flip(grads[k]), flip(d), flip(m), flip(v)
        else:
            delta[k], new_m[k], new_v[k] = adamw(a[k], grads[k], a['m_' + k], a['v_' + k], "adamw_" + k)
    two_d = lambda t: t.reshape(1, -1) if t.ndim == 1 else t
    ds, ms, vs = adamw_many([two_d(a[k]) for k in small_names], [two_d(grads[k]) for k in small_names],
                            [two_d(a['m_' + k]) for k in small_names], [two_d(a['v_' + k]) for k in small_names],
                            "adamw_small")
    for k, dv, mv, vv in zip(small_names, ds, ms, vs):
        delta[k], new_m[k], new_v[k] = [t.reshape(a[k].shape) for t in (dv, mv, vv)]

    return (loss, grad_x.reshape(a['x'].shape), *[grads[k] for k in WEIGHTS], *[delta[k] for k in WEIGHTS],
            *[new_m[k] for k in WEIGHTS], *[new_v[k] for k in WEIGHTS])
```

```python
import functools

import jax
import jax.numpy as jnp
from jax import lax
from jax.experimental import pallas as pl
from jax.experimental.pallas import tpu as pltpu

F32 = jnp.float32
BF16 = jnp.bfloat16

D_MODEL = 1024
EPS = 1e-5
N_Q_HEADS, N_KV_HEADS, HEAD_DIM = 8, 2, 64
ATTN_BLOCK = 128
ATTN_WIDTH = N_Q_HEADS * HEAD_DIM
KV_WIDTH = N_KV_HEADS * HEAD_DIM
CONV_WIDTH = 512
CONV_KERNEL = 31
CONV_HALO = 32
POOL_WINDOWS = (2, 4, 8, 16)
POOL_WIDTH = 512
POOL_HALO = 16
SGU_WIDTH = 512
SGU_CHUNK = 128
D_FF = 2816
FF_CHUNK = 128
MXU_COLS = 256
FFN_AHEAD = 1
LANES = 128
N_CHIPS = 4
N_DEV = 8

ADAM_LR, ADAM_B1, ADAM_B2, ADAM_EPS, ADAM_WD, ADAM_STEP = 0.001, 0.9, 0.999, 1e-08, 0.01, 10

VMEM_LIMIT = 56 * 2**20

WEIGHTS = ['mix_norm', 'a_w_in', 'a_b_in', 'a_sinks', 'a_conv_w', 'a_conv_b', 'a_cln_g', 'a_cln_b', 'a_w_out',
           'c_w_in', 'c_w_pool', 'c_pool_scale', 'c_sln_g', 'c_sln_b', 'c_w_s', 'c_b_s', 'c_w_out',
           'ffn_norm', 'ffn_w_gate', 'ffn_w_up', 'ffn_w_down', 'final_norm']
IN_NAMES = (['x'] + WEIGHTS + ['loss_target'] + ['m_' + n for n in WEIGHTS] + ['v_' + n for n in WEIGHTS])


def _params(*sem):
    return pltpu.CompilerParams(dimension_semantics=sem, vmem_limit_bytes=VMEM_LIMIT)


def _dot(a, b):
    return jnp.dot(a, b, preferred_element_type=F32)


def _dot_nt(a, b):
    return lax.dot_general(a, b, (((1,), (1,)), ((), ())), preferred_element_type=F32)


def _dot_tn(a, b):
    return lax.dot_general(a, b, (((0,), (0,)), ((), ())), preferred_element_type=F32)


def _sigmoid(v):
    return 0.5 * jnp.tanh(0.5 * v) + 0.5


def _row_tile(n, pref):
    t = min(n, pref)
    while n % t:
        t //= 2
    return t


def _col_tile(m, rows, budget=6 * 2**20):
    best = LANES
    for t in range(LANES, m + 1, LANES):
        if m % t == 0 and rows * t * 4 <= budget:
            best = t
    return best


class Comm:
    def __init__(self, ins, out_shapes, plan, count, aliases=None):
        self.ins, self.out_shapes, self.plan, self.count, self.aliases = ins, out_shapes, plan, count, aliases or {}

    def __add__(self, other):
        ni, no = len(self.ins), len(self.out_shapes)

        def plan(ins, outs, pos):
            return self.plan(ins[:ni], outs[:no], pos) + other.plan(ins[ni:], outs[no:], pos)

        aliases = dict(self.aliases)
        aliases.update({ni + i: no + o for i, o in other.aliases.items()})
        return Comm(list(self.ins) + list(other.ins), list(self.out_shapes) + list(other.out_shapes), plan,
                    self.count + other.count, aliases)

    def split(self, outs, other):
        return outs[:len(self.out_shapes)], outs[len(self.out_shapes):]


def _pcall(body, name, grid, in_specs, out_specs, out_shape, scratch_shapes, args, sem, comm=None):
    single = not isinstance(out_shape, (list, tuple))
    if single:
        out_specs, out_shape = [out_specs], [out_shape]
    if comm is None:
        res = pl.pallas_call(body, name=name, grid=grid, in_specs=in_specs, out_specs=list(out_specs),
                             out_shape=list(out_shape), scratch_shapes=list(scratch_shapes),
                             compiler_params=_params(*sem))(*args)
        return (res[0] if single else res), []
    na, nci, no, nco, ns = len(args), len(comm.ins), len(out_shape), len(comm.out_shapes), len(scratch_shapes)

    def wrapped(*refs):
        a_refs, ci_refs = refs[:na], refs[na:na + nci]
        o_refs, co_refs = refs[na + nci:na + nci + no], refs[na + nci + no:na + nci + no + nco]
        s_refs = refs[na + nci + no + nco:na + nci + no + nco + ns]
        send_sems, recv_sems = refs[-2], refs[-1]
        pos = _me()

        def copies():
            return [pltpu.make_async_remote_copy(src_ref=s, dst_ref=d, send_sem=send_sems.at[i],
                                                 recv_sem=recv_sems.at[i], device_id=_flip(pos, rel),
                                                 device_id_type=MESH)
                    for i, (rel, s, d) in enumerate(comm.plan(ci_refs, co_refs, pos))]

        first, last = None, None
        for d, size in enumerate(grid):
            f, l = pl.program_id(d) == 0, pl.program_id(d) == size - 1
            first = f if first is None else first & f
            last = l if last is None else last & l

        @pl.when(first)
        def _():
            for cp in copies():
                cp.start()

        body(*a_refs, *o_refs, *s_refs)

        @pl.when(last)
        def _():
            for cp in copies():
                cp.wait()

    res = pl.pallas_call(
        wrapped, name=name, grid=grid,
        in_specs=list(in_specs) + [ANY] * nci, out_specs=list(out_specs) + [ANY] * nco,
        out_shape=list(out_shape) + [jax.ShapeDtypeStruct(s, d) for s, d in comm.out_shapes],
        scratch_shapes=list(scratch_shapes) + [pltpu.SemaphoreType.DMA((comm.count,)),
                                               pltpu.SemaphoreType.DMA((comm.count,))],
        input_output_aliases={na + i: no + o for i, o in comm.aliases.items()},
        compiler_params=_params(*(["arbitrary"] * len(grid))),
    )(*args, *comm.ins)
    outs = res[:no]
    return (outs[0] if single else outs), list(res[no:])


def norm_inproj(x, gain, w, bias, splits, dtypes, name, comm=None, w_transposed=False):
    n = x.shape[0]
    m = w.shape[0] if w_transposed else w.shape[1]
    tm = _row_tile(n, 1024)

    def body(x_ref, g_ref, w_ref, b_ref, hn_ref, *outs):
        xv = x_ref[...]
        r = lax.rsqrt(jnp.mean(xv * xv, axis=-1, keepdims=True) + EPS)
        hn = ((xv * r) * g_ref[...]).astype(BF16)
        hn_ref[...] = hn
        z = (_dot_nt if w_transposed else _dot)(hn, w_ref[...]) + b_ref[...]
        for o, (lo, hi) in zip(outs, splits):
            o[...] = z[:, lo:hi].astype(o.dtype)

    out_shape = [jax.ShapeDtypeStruct((n, D_MODEL), BF16)]
    out_specs = [pl.BlockSpec((tm, D_MODEL), lambda i: (i, 0))]
    for (lo, hi), dt in zip(splits, dtypes):
        out_shape.append(jax.ShapeDtypeStruct((n, hi - lo), dt))
        out_specs.append(pl.BlockSpec((tm, hi - lo), lambda i: (i, 0)))
    return _pcall(
        body, name, (n // tm,),
        [pl.BlockSpec((tm, D_MODEL), lambda i: (i, 0)),
         pl.BlockSpec((1, D_MODEL), lambda i: (0, 0)),
         pl.BlockSpec(w.shape, lambda i: (0, 0)),
         pl.BlockSpec((1, m), lambda i: (0, 0))],
        out_specs, out_shape, [], (x, gain, w, bias), ("parallel",), comm)


def out_proj(res, m1, m2, w, name, comm=None):
    n = res.shape[0]
    k1, k2 = m1.shape[1], m2.shape[1]
    assert k1 == k2
    tm = _row_tile(n, 1024)

    def body(r_ref, a_ref, b_ref, w1_ref, w2_ref, o_ref):
        o_ref[...] = r_ref[...] + _dot(a_ref[...], w1_ref[...]) + _dot(b_ref[...], w2_ref[...])

    return _pcall(
        body, name, (n // tm,),
        [pl.BlockSpec((tm, D_MODEL), lambda i: (i, 0)),
         pl.BlockSpec((tm, k1), lambda i: (i, 0)),
         pl.BlockSpec((tm, k2), lambda i: (i, 0)),
         pl.BlockSpec((k1, D_MODEL), lambda i: (0, 0)),
         pl.BlockSpec((k2, D_MODEL), lambda i: (1, 0))],
        pl.BlockSpec((tm, D_MODEL), lambda i: (i, 0)),
        jax.ShapeDtypeStruct((n, D_MODEL), F32), [], (res, m1, m2, w, w), ("parallel",), comm)


def ffn_gate_up(h, gain, wg_t, wu_t, name, comm=None):
    n = h.shape[0]
    tm = _row_tile(n, 512)
    th = D_FF

    def body(h_ref, g_ref, wg_ref, wu_ref, hn_ref, go_ref, uo_ref):
        @pl.when(pl.program_id(1) == 0)
        def _():
            xv = h_ref[...]
            r = lax.rsqrt(jnp.mean(xv * xv, axis=-1, keepdims=True) + EPS)
            hn_ref[...] = ((xv * r) * g_ref[...]).astype(BF16)

        hn = hn_ref[...]
        go_ref[...] = _dot_nt(hn, wg_ref[...]).astype(BF16)
        uo_ref[...] = _dot_nt(hn, wu_ref[...]).astype(BF16)

    return _pcall(
        body, name, (n // tm, D_FF // th),
        [pl.BlockSpec((tm, D_MODEL), lambda i, j: (i, 0)),
         pl.BlockSpec((1, D_MODEL), lambda i, j: (0, 0)),
         pl.BlockSpec((th, D_MODEL), lambda i, j: (j, 0), pipeline_mode=pl.Buffered(1)),
         pl.BlockSpec((th, D_MODEL), lambda i, j: (j, 0), pipeline_mode=pl.Buffered(1))],
        [pl.BlockSpec((tm, D_MODEL), lambda i, j: (i, 0)),
         pl.BlockSpec((tm, th), lambda i, j: (i, j)),
         pl.BlockSpec((tm, th), lambda i, j: (i, j))],
        [jax.ShapeDtypeStruct((n, D_MODEL), BF16),
         jax.ShapeDtypeStruct((n, D_FF), BF16),
         jax.ShapeDtypeStruct((n, D_FF), BF16)],
        [], (h, gain, wg_t, wu_t), ("parallel", "arbitrary"), comm)


def ffn_down(h, g, u, wd, name, comm=None):
    n = h.shape[0]
    tm = _row_tile(n, 1024)

    def body(h_ref, g_ref, u_ref, w_ref, o_ref, a_ref):
        for c0 in range(0, D_FF, FF_CHUNK):
            gv = g_ref[:, c0:c0 + FF_CHUNK]
            a_ref[:, c0:c0 + FF_CHUNK] = gv * _sigmoid(gv) * u_ref[:, c0:c0 + FF_CHUNK]
        o_ref[...] = h_ref[...] + _dot(a_ref[...], w_ref[...])

    return _pcall(
        body, name, (n // tm,),
        [pl.BlockSpec((tm, D_MODEL), lambda i: (i, 0)),
         pl.BlockSpec((tm, D_FF), lambda i: (i, 0)),
         pl.BlockSpec((tm, D_FF), lambda i: (i, 0)),
         pl.BlockSpec((D_FF, D_MODEL), lambda i: (0, 0), pipeline_mode=pl.Buffered(1))],
        pl.BlockSpec((tm, D_MODEL), lambda i: (i, 0)),
        jax.ShapeDtypeStruct((n, D_MODEL), F32),
        [pltpu.VMEM((tm, D_FF), BF16)], (h, g, u, wd), ("parallel",), comm)


def ffn_down_bwd(dh, g, u, wd, name, comm=None):
    n = dh.shape[0]
    tm = _row_tile(n, 512)

    def body(dh_ref, g_ref, u_ref, w_ref, dg_ref, du_ref, a_ref):
        dhb = dh_ref[...].astype(BF16)
        chunks = [slice(c0, c0 + MXU_COLS) for c0 in range(0, D_FF, MXU_COLS)]
        ahead = [_dot_nt(dhb, w_ref[c, :]) for c in chunks[:FFN_AHEAD]]
        for i, cols in enumerate(chunks):
            da = ahead.pop(0).astype(BF16)
            if i + FFN_AHEAD < len(chunks):
                ahead.append(_dot_nt(dhb, w_ref[chunks[i + FFN_AHEAD], :]))
            gv, uv = g_ref[:, cols], u_ref[:, cols]
            sg = _sigmoid(gv)
            act = gv * sg
            dg_ref[:, cols] = (da * uv) * (sg + act * (1.0 - sg))
            du_ref[:, cols] = da * act
            a_ref[:, cols] = act * uv

    spec_h = pl.BlockSpec((tm, D_FF), lambda i: (i, 0))
    return _pcall(
        body, name, (n // tm,),
        [pl.BlockSpec((tm, D_MODEL), lambda i: (i, 0)), spec_h, spec_h,
         pl.BlockSpec((D_FF, D_MODEL), lambda i: (0, 0))],
        [spec_h, spec_h, spec_h], [jax.ShapeDtypeStruct((n, D_FF), BF16)] * 3,
        [], (dh, g, u, wd), ("parallel",), comm)


def mm_tn(x, dy, name, xsum=False, comm=None):
    n, k = x.shape
    m = dy.shape[1]
    tk = _col_tile(k, m)
    tt = _row_tile(n, 2048)

    def body(x_ref, dy_ref, o_ref, *rest):
        xt_ref = rest[-1]
        t = pl.program_id(1)
        xv = x_ref[...]
        xt_ref[...] = xv.astype(BF16).T
        part = _dot(xt_ref[...], dy_ref[...].astype(BF16))

        @pl.when(t == 0)
        def _():
            o_ref[...] = part

        @pl.when(t > 0)
        def _():
            o_ref[...] += part

        if xsum:
            cs = jnp.broadcast_to(jnp.sum(xv.astype(F32), axis=0, keepdims=True), rest[0].shape)

            @pl.when(t == 0)
            def _():
                rest[0][...] = cs

            @pl.when(t > 0)
            def _():
                rest[0][...] += cs

    out_shape = [jax.ShapeDtypeStruct((k, m), F32)]
    out_specs = [pl.BlockSpec((tk, m), lambda j, t: (j, 0))]
    if xsum:
        out_shape.append(jax.ShapeDtypeStruct((8, k), F32))
        out_specs.append(pl.BlockSpec((8, tk), lambda j, t: (0, j)))
    res, comm_outs = _pcall(
        body, name, (k // tk, n // tt),
        [pl.BlockSpec((tt, tk), lambda j, t: (t, j)),
         pl.BlockSpec((tt, m), lambda j, t: (t, 0))],
        out_specs, out_shape, [pltpu.VMEM((tk, tt), BF16)], (x, dy), ("arbitrary", "arbitrary"), comm)
    res = (res[0], res[1][0]) if xsum else res[0]
    return res if comm is None else (res, comm_outs)


def out_proj_bwd(dh, w, dtypes, name):
    n = dh.shape[0]
    k = w.shape[0]
    half = k // 2
    tm = _row_tile(n, 1024)

    def body(dh_ref, w_ref, a_ref, b_ref):
        dm = _dot_nt(dh_ref[...].astype(BF16), w_ref[...])
        a_ref[...] = dm[:, :half].astype(a_ref.dtype)
        b_ref[...] = dm[:, half:].astype(b_ref.dtype)

    return pl.pallas_call(
        body, name=name, grid=(n // tm,),
        in_specs=[pl.BlockSpec((tm, D_MODEL), lambda i: (i, 0)),
                  pl.BlockSpec((k, D_MODEL), lambda i: (0, 0))],
        out_specs=[pl.BlockSpec((tm, half), lambda i: (i, 0))] * 2,
        out_shape=[jax.ShapeDtypeStruct((n, half), dtypes[0]), jax.ShapeDtypeStruct((n, half), dtypes[1])],
        compiler_params=_params("parallel"),
    )(dh, w)


def proj_rms_bwd(dys, ws, h_in, gain, dres, nk, name, tm_pref=512, comm=None, w_transposed=False):
    n = h_in.shape[0]
    npair = len(dys)
    tm = _row_tile(n, tm_pref)
    tks = [dy.shape[1] // nk for dy in dys]
    mm = _dot if w_transposed else _dot_nt

    def body(*refs):
        dy_refs = refs[:npair]
        w_refs = refs[npair:2 * npair]
        h_ref, g_ref, dr_ref, o_ref, dg_ref, acc_ref = refs[2 * npair:]
        i, k = pl.program_id(0), pl.program_id(1)
        part = mm(dy_refs[0][...], w_refs[0][...])
        for p in range(1, npair):
            part = part + mm(dy_refs[p][...], w_refs[p][...])

        @pl.when(k == 0)
        def _():
            acc_ref[...] = part

        @pl.when(k > 0)
        def _():
            acc_ref[...] += part

        @pl.when(k == nk - 1)
        def _():
            dhn = acc_ref[...]
            xv = h_ref[...]
            r = lax.rsqrt(jnp.mean(xv * xv, axis=-1, keepdims=True) + EPS)
            xh = xv * r
            uv = dhn * g_ref[...]
            o_ref[...] = dr_ref[...] + r * (uv - xh * jnp.mean(uv * xh, axis=-1, keepdims=True))
            dgp = jnp.broadcast_to(jnp.sum(dhn * xh, axis=0, keepdims=True), dg_ref.shape)

            @pl.when(i == 0)
            def _():
                dg_ref[...] = dgp

            @pl.when(i > 0)
            def _():
                dg_ref[...] += dgp

    row = pl.BlockSpec((tm, D_MODEL), lambda i, k: (i, 0))
    in_specs = [pl.BlockSpec((tm, tk), lambda i, k: (i, k)) for tk in tks]
    once = dict(pipeline_mode=pl.Buffered(1)) if nk == 1 else {}
    if w_transposed:
        in_specs += [pl.BlockSpec((tk, D_MODEL), lambda i, k: (k, 0), **once) for tk in tks]
    else:
        in_specs += [pl.BlockSpec((D_MODEL, tk), lambda i, k: (0, k), **once) for tk in tks]
    in_specs += [row, pl.BlockSpec((1, D_MODEL), lambda i, k: (0, 0)), row]
    (dh, dgain), comm_outs = _pcall(
        body, name, (n // tm, nk), in_specs,
        [row, pl.BlockSpec((8, D_MODEL), lambda i, k: (0, 0))],
        [jax.ShapeDtypeStruct((n, D_MODEL), F32), jax.ShapeDtypeStruct((8, D_MODEL), F32)],
        [pltpu.VMEM((tm, D_MODEL), F32)], (*dys, *ws, h_in, gain, dres), ("arbitrary", "arbitrary"), comm)
    return dh, dgain[0], comm_outs


def loss_head(h, gain, target, name):
    n = h.shape[0]
    tm = _row_tile(n, 512)

    def body(h_ref, g_ref, t_ref, dh_ref, dg_ref, l_ref):
        i = pl.program_id(0)
        xv = h_ref[...]
        r = lax.rsqrt(jnp.mean(xv * xv, axis=-1, keepdims=True) + EPS)
        xh = xv * r
        err = xh * g_ref[...] - t_ref[...]
        dy = err * (1.0 / D_MODEL)
        uv = dy * g_ref[...]
        dh_ref[...] = r * (uv - xh * jnp.mean(uv * xh, axis=-1, keepdims=True))
        dgp = jnp.broadcast_to(jnp.sum(dy * xh, axis=0, keepdims=True), dg_ref.shape)
        lp = jnp.sum(jnp.sum(err * err, axis=-1, keepdims=True), axis=0, keepdims=True) * (0.5 / D_MODEL)
        lp = jnp.broadcast_to(lp, l_ref.shape)

        @pl.when(i == 0)
        def _():
            dg_ref[...] = dgp
            l_ref[...] = lp

        @pl.when(i > 0)
        def _():
            dg_ref[...] += dgp
            l_ref[...] += lp

    row = pl.BlockSpec((tm, D_MODEL), lambda i: (i, 0))
    dh, dg, l = pl.pallas_call(
        body, name=name, grid=(n // tm,),
        in_specs=[row, pl.BlockSpec((1, D_MODEL), lambda i: (0, 0)), row],
        out_specs=[row, pl.BlockSpec((8, D_MODEL), lambda i: (0, 0)), pl.BlockSpec((8, LANES), lambda i: (0, 0))],
        out_shape=[jax.ShapeDtypeStruct((n, D_MODEL), F32), jax.ShapeDtypeStruct((8, D_MODEL), F32),
                   jax.ShapeDtypeStruct((8, LANES), F32)],
        compiler_params=_params("arbitrary"),
    )(h, gain, target)
    return dh, dg[0], l[0, 0]


GROUP = N_Q_HEADS // N_KV_HEADS
GQ = GROUP * ATTN_BLOCK


def _attn_mask_t(n):
    r = lax.broadcasted_iota(jnp.int32, (2 * ATTN_BLOCK, GQ), 0)
    qi = lax.broadcasted_iota(jnp.int32, (2 * ATTN_BLOCK, GQ), 1) & (ATTN_BLOCK - 1)
    band = (r > qi) & (r <= qi + ATTN_BLOCK)
    return band & ((r >= ATTN_BLOCK) | (n > 0))


def _stack_heads(blk, kh):
    return jnp.concatenate([blk[:, (kh * GROUP + g) * HEAD_DIM:(kh * GROUP + g + 1) * HEAD_DIM]
                            for g in range(GROUP)], axis=0)


def _attn_probs_t(kk, qs, mask, sink):
    s = _dot_nt(kk, qs) * (HEAD_DIM ** -0.5)
    s = jnp.where(mask, s, -1e30)
    m = jnp.maximum(jnp.max(s, axis=0, keepdims=True), sink)
    p = jnp.exp(s - m)
    esink = jnp.exp(sink - m)
    inv = 1.0 / (jnp.sum(p, axis=0, keepdims=True) + esink)
    return p * inv, esink * inv


def attn_fwd(q, kv, sinks_t, nseq, seq, name, comm=None):
    nb = seq // ATTN_BLOCK

    def body(q_ref, kv_ref, s_ref, o_ref, kvp):
        kvp[0:ATTN_BLOCK, :] = jnp.zeros((ATTN_BLOCK, 2 * KV_WIDTH), BF16)
        kvp[ATTN_BLOCK:, :] = kv_ref[...]

        def blk(n, carry):
            st = pl.multiple_of(n * ATTN_BLOCK, ATTN_BLOCK)
            qb = q_ref[pl.ds(st, ATTN_BLOCK), :]
            kw = kvp[pl.ds(st, 2 * ATTN_BLOCK), :]
            mask = _attn_mask_t(n)
            for kh in range(N_KV_HEADS):
                kk = kw[:, kh * HEAD_DIM:(kh + 1) * HEAD_DIM]
                vv = kw[:, KV_WIDTH + kh * HEAD_DIM:KV_WIDTH + (kh + 1) * HEAD_DIM]
                probs, _ = _attn_probs_t(kk, _stack_heads(qb, kh), mask, s_ref[kh:kh + 1, :])
                ot = _dot_tn(vv, probs.astype(BF16))
                for pair in range(GROUP // 2):
                    two = jnp.concatenate([ot[:, (2 * pair) * ATTN_BLOCK:(2 * pair + 1) * ATTN_BLOCK],
                                           ot[:, (2 * pair + 1) * ATTN_BLOCK:(2 * pair + 2) * ATTN_BLOCK]], axis=0)
                    col = (kh * GROUP + 2 * pair) * HEAD_DIM
                    o_ref[pl.ds(st, ATTN_BLOCK), col:col + 2 * HEAD_DIM] = two.T.astype(o_ref.dtype)
            return carry

        lax.fori_loop(0, nb, blk, 0, unroll=4)

    return _pcall(
        body, name, (nseq,),
        [pl.BlockSpec((seq, ATTN_WIDTH), lambda b: (b, 0)),
         pl.BlockSpec((seq, 2 * KV_WIDTH), lambda b: (b, 0)),
         pl.BlockSpec((8, GQ), lambda b: (0, 0))],
        pl.BlockSpec((seq, ATTN_WIDTH), lambda b: (b, 0)),
        jax.ShapeDtypeStruct((nseq * seq, ATTN_WIDTH), BF16),
        [pltpu.VMEM((ATTN_BLOCK + seq, 2 * KV_WIDTH), BF16)], (q, kv, sinks_t), ("parallel",), comm)


def attn_bwd(q, kv, sinks_t, do, nseq, seq, name, comm=None):
    nb = seq // ATTN_BLOCK

    def body(q_ref, kv_ref, s_ref, do_ref, dq_ref, dkv_ref, ds_ref, kvp, dkvp, dsacc):
        @pl.when(pl.program_id(0) == 0)
        def _():
            dsacc[...] = jnp.zeros(dsacc.shape, F32)

        kvp[0:ATTN_BLOCK, :] = jnp.zeros((ATTN_BLOCK, 2 * KV_WIDTH), BF16)
        kvp[ATTN_BLOCK:, :] = kv_ref[...]
        dkvp[...] = jnp.zeros(dkvp.shape, F32)

        def blk(n, carry):
            st = pl.multiple_of(n * ATTN_BLOCK, ATTN_BLOCK)
            qb = q_ref[pl.ds(st, ATTN_BLOCK), :]
            dob = do_ref[pl.ds(st, ATTN_BLOCK), :]
            kw = kvp[pl.ds(st, 2 * ATTN_BLOCK), :]
            mask = _attn_mask_t(n)
            for kh in range(N_KV_HEADS):
                kk = kw[:, kh * HEAD_DIM:(kh + 1) * HEAD_DIM]
                vv = kw[:, KV_WIDTH + kh * HEAD_DIM:KV_WIDTH + (kh + 1) * HEAD_DIM]
                qs = _stack_heads(qb, kh)
                dos = _stack_heads(dob, kh)
                probs, psink = _attn_probs_t(kk, qs, mask, s_ref[kh:kh + 1, :])
                dp = _dot_nt(vv, dos)
                dv = _dot(probs.astype(BF16), dos)
                rowdot = jnp.sum(probs * dp, axis=0, keepdims=True)
                dsc = (probs * (dp - rowdot) * (HEAD_DIM ** -0.5)).astype(BF16)
                dsacc[kh:kh + 1, :] += -psink * rowdot
                dk = _dot(dsc, qs)
                dqs = _dot_tn(dsc, kk)
                for g in range(GROUP):
                    col = (kh * GROUP + g) * HEAD_DIM
                    dq_ref[pl.ds(st, ATTN_BLOCK), col:col + HEAD_DIM] = (
                        dqs[g * ATTN_BLOCK:(g + 1) * ATTN_BLOCK].astype(dq_ref.dtype))
                dkvp[pl.ds(st, 2 * ATTN_BLOCK), kh * HEAD_DIM:(kh + 1) * HEAD_DIM] += dk
                dkvp[pl.ds(st, 2 * ATTN_BLOCK), KV_WIDTH + kh * HEAD_DIM:KV_WIDTH + (kh + 1) * HEAD_DIM] += dv
            return carry

        lax.fori_loop(0, nb, blk, 0, unroll=2)
        dkv_ref[...] = dkvp[ATTN_BLOCK:, :].astype(dkv_ref.dtype)

        @pl.when(pl.program_id(0) == nseq - 1)
        def _():
            for kh in range(N_KV_HEADS):
                for g in range(GROUP):
                    tot = jnp.sum(dsacc[kh:kh + 1, g * ATTN_BLOCK:(g + 1) * ATTN_BLOCK], axis=1, keepdims=True)
                    ds_ref[kh * GROUP + g:kh * GROUP + g + 1, :] = jnp.broadcast_to(tot, (1, LANES))

    seq_q = pl.BlockSpec((seq, ATTN_WIDTH), lambda b: (b, 0))
    seq_kv = pl.BlockSpec((seq, 2 * KV_WIDTH), lambda b: (b, 0))
    return _pcall(
        body, name, (nseq,),
        [seq_q, seq_kv, pl.BlockSpec((8, GQ), lambda b: (0, 0)), seq_q],
        [seq_q, seq_kv, pl.BlockSpec((N_Q_HEADS, LANES), lambda b: (0, 0))],
        [jax.ShapeDtypeStruct((nseq * seq, ATTN_WIDTH), BF16),
         jax.ShapeDtypeStruct((nseq * seq, 2 * KV_WIDTH), BF16),
         jax.ShapeDtypeStruct((N_Q_HEADS, LANES), F32)],
        [pltpu.VMEM((ATTN_BLOCK + seq, 2 * KV_WIDTH), BF16),
         pltpu.VMEM((ATTN_BLOCK + seq, 2 * KV_WIDTH), F32),
         pltpu.VMEM((8, GQ), F32)], (q, kv, sinks_t, do), ("arbitrary",), comm)


CONV_T = 128


SUBLANES = 8


def _shifted_rows(win):
    phases = [win] + [pltpu.roll(win, s, 0) for s in range(1, SUBLANES)]

    def shifted(s):
        lo = CONV_HALO - SUBLANES * (s // SUBLANES)
        return phases[s % SUBLANES][lo:lo + CONV_T]

    return shifted


def _conv_taps(win, w_ref, lanes, init):
    shifted = _shifted_rows(win)
    acc = init
    for j in range(CONV_KERNEL):
        acc = acc + w_ref[j:j + 1, lanes] * shifted(CONV_KERNEL - 1 - j)
    return acc


def _conv_block(h0p, w_ref, vec_ref, st):
    cols = []
    for cs in range(CONV_WIDTH // LANES):
        lanes = slice(cs * LANES, (cs + 1) * LANES)
        win = h0p[pl.ds(st, CONV_T + CONV_HALO), lanes]
        init = jnp.broadcast_to(vec_ref[0:1, lanes], (CONV_T, LANES))
        cols.append(_conv_taps(win, w_ref, lanes, init))
    return jnp.concatenate(cols, axis=-1)


def _glu_store(c_ref, h0p, st):
    cb = c_ref[pl.ds(st, CONV_T), :]
    h0p[pl.ds(pl.multiple_of(st + CONV_HALO, CONV_HALO), CONV_T), :] = cb[:, :CONV_WIDTH] * _sigmoid(cb[:, CONV_WIDTH:])


def conv_fwd(c, w, vec, nseq, seq, name, comm=None):
    nb = seq // CONV_T

    def body(c_ref, w_ref, vec_ref, o_ref, h1_ref, h0p):
        h0p[0:CONV_HALO, :] = jnp.zeros((CONV_HALO, CONV_WIDTH), F32)

        def blk(n, carry):
            st = pl.multiple_of(n * CONV_T, CONV_T)
            _glu_store(c_ref, h0p, st)
            h1 = _conv_block(h0p, w_ref, vec_ref, st)
            h1_ref[pl.ds(st, CONV_T), :] = h1
            mu = jnp.mean(h1, axis=-1, keepdims=True)
            xc = h1 - mu
            rstd = lax.rsqrt(jnp.mean(xc * xc, axis=-1, keepdims=True) + EPS)
            y = xc * rstd * vec_ref[1:2, :] + vec_ref[2:3, :]
            o_ref[pl.ds(st, CONV_T), :] = (y * _sigmoid(y)).astype(o_ref.dtype)
            return carry

        lax.fori_loop(0, nb, blk, 0)

    return _pcall(
        body, name, (nseq,),
        [pl.BlockSpec((seq, 2 * CONV_WIDTH), lambda b: (b, 0)),
         pl.BlockSpec((32, CONV_WIDTH), lambda b: (0, 0)),
         pl.BlockSpec((8, CONV_WIDTH), lambda b: (0, 0))],
        [pl.BlockSpec((seq, CONV_WIDTH), lambda b: (b, 0))] * 2,
        [jax.ShapeDtypeStruct((nseq * seq, CONV_WIDTH), BF16), jax.ShapeDtypeStruct((nseq * seq, CONV_WIDTH), F32)],
        [pltpu.VMEM((CONV_HALO + seq, CONV_WIDTH), F32)], (c, w, vec), ("parallel",), comm)


def conv_bwd(c, h1_saved, w, vec, dout, nseq, seq, name, comm=None):
    nb = seq // CONV_T

    def body(c_ref, h1_ref, w_ref, vec_ref, do_ref, dc_ref, dw_ref, dvec_ref, h0p, dh1p, dwacc):
        @pl.when(pl.program_id(0) == 0)
        def _():
            dwacc[...] = jnp.zeros(dwacc.shape, F32)
            dvec_ref[...] = jnp.zeros(dvec_ref.shape, F32)

        h0p[0:CONV_HALO, :] = jnp.zeros((CONV_HALO, CONV_WIDTH), F32)
        dh1p[seq:seq + CONV_HALO, :] = jnp.zeros((CONV_HALO, CONV_WIDTH), F32)

        def pass_a(n, carry):
            st = pl.multiple_of(n * CONV_T, CONV_T)
            _glu_store(c_ref, h0p, st)
            h1 = h1_ref[pl.ds(st, CONV_T), :]
            mu = jnp.mean(h1, axis=-1, keepdims=True)
            xc = h1 - mu
            rstd = lax.rsqrt(jnp.mean(xc * xc, axis=-1, keepdims=True) + EPS)
            xh = xc * rstd
            y = xh * vec_ref[1:2, :] + vec_ref[2:3, :]
            sg = _sigmoid(y)
            dy = do_ref[pl.ds(st, CONV_T), :] * (sg * (1.0 + y * (1.0 - sg)))
            dvec_ref[1:2, :] += jnp.sum(dy * xh, axis=0, keepdims=True)
            dvec_ref[2:3, :] += jnp.sum(dy, axis=0, keepdims=True)
            dxh = dy * vec_ref[1:2, :]
            dh1 = rstd * (dxh - jnp.mean(dxh, axis=-1, keepdims=True)
                          - xh * jnp.mean(dxh * xh, axis=-1, keepdims=True))
            dvec_ref[0:1, :] += jnp.sum(dh1, axis=0, keepdims=True)
            dh1p[pl.ds(st, CONV_T), :] = dh1
            return carry

        lax.fori_loop(0, nb, pass_a, 0)

        def pass_b(n, carry):
            st = pl.multiple_of(n * CONV_T, CONV_T)
            cols = []
            for cs in range(CONV_WIDTH // LANES):
                lanes = slice(cs * LANES, (cs + 1) * LANES)
                wind = dh1p[pl.ds(st, CONV_T + CONV_HALO), lanes]
                winh = h0p[pl.ds(st, CONV_T + CONV_HALO), lanes]
                d1 = wind[0:CONV_T]
                shifted_d, shifted_h = _shifted_rows(wind), _shifted_rows(winh)
                acc = jnp.zeros((CONV_T, LANES), F32)
                for j in range(CONV_KERNEL):
                    acc = acc + w_ref[j:j + 1, lanes] * shifted_d(2 + j)
                    prod = d1 * shifted_h(CONV_KERNEL - 1 - j)
                    part = prod[0:8]
                    for r in range(8, CONV_T, 8):
                        part = part + prod[r:r + 8]
                    dwacc[8 * j:8 * j + 8, lanes] += part
                cols.append(acc)
            dh0 = jnp.concatenate(cols, axis=-1)
            cb = c_ref[pl.ds(st, CONV_T), :]
            av, gt = cb[:, :CONV_WIDTH], cb[:, CONV_WIDTH:]
            sg = _sigmoid(gt)
            dc_ref[pl.ds(st, CONV_T), :] = jnp.concatenate(
                [dh0 * sg, dh0 * av * sg * (1.0 - sg)], axis=-1).astype(dc_ref.dtype)
            return carry

        lax.fori_loop(0, nb, pass_b, 0)

        @pl.when(pl.program_id(0) == nseq - 1)
        def _():
            dw_ref[...] = jnp.zeros(dw_ref.shape, F32)
            for j in range(CONV_KERNEL):
                dw_ref[j:j + 1, :] = jnp.sum(dwacc[8 * j:8 * j + 8, :], axis=0, keepdims=True)

    return _pcall(
        body, name, (nseq,),
        [pl.BlockSpec((seq, 2 * CONV_WIDTH), lambda b: (b, 0)),
         pl.BlockSpec((seq, CONV_WIDTH), lambda b: (b, 0)),
         pl.BlockSpec((32, CONV_WIDTH), lambda b: (0, 0)),
         pl.BlockSpec((8, CONV_WIDTH), lambda b: (0, 0)),
         pl.BlockSpec((seq, CONV_WIDTH), lambda b: (b, 0))],
        [pl.BlockSpec((seq, 2 * CONV_WIDTH), lambda b: (b, 0)),
         pl.BlockSpec((32, CONV_WIDTH), lambda b: (0, 0)),
         pl.BlockSpec((8, CONV_WIDTH), lambda b: (0, 0))],
        [jax.ShapeDtypeStruct((nseq * seq, 2 * CONV_WIDTH), BF16),
         jax.ShapeDtypeStruct((32, CONV_WIDTH), F32),
         jax.ShapeDtypeStruct((8, CONV_WIDTH), F32)],
        [pltpu.VMEM((CONV_HALO + seq, CONV_WIDTH), F32),
         pltpu.VMEM((seq + CONV_HALO, CONV_WIDTH), F32),
         pltpu.VMEM((8 * 32, CONV_WIDTH), F32)], (c, h1_saved, w, vec, dout), ("arbitrary",), comm)


POOL_T = 128


def _pooled_block(zpp, st, grp):
    lanes = slice(grp * LANES, (grp + 1) * LANES)
    win = zpp[pl.ds(st, POOL_T + POOL_HALO), lanes]
    acc = win
    for lvl in range(grp + 1):
        acc = acc + pltpu.roll(acc, 1 << lvl, 0)
    t = st + lax.broadcasted_iota(jnp.int32, (POOL_T, 1), 0)
    inv = 1.0 / jnp.minimum(t + 1, POOL_WINDOWS[grp]).astype(F32)
    return acc[POOL_HALO:] * inv - win[POOL_HALO:], inv


def pool_fwd(zp, wp, scale, nseq, seq, name):
    nb = seq // POOL_T

    def body(z_ref, wp_ref, sc_ref, o_ref, zpp):
        zpp[0:POOL_HALO, :] = jnp.zeros((POOL_HALO, POOL_WIDTH), F32)
        zpp[POOL_HALO:, :] = z_ref[...]

        def blk(n, carry):
            st = pl.multiple_of(n * POOL_T, POOL_T)
            for grp in range(len(POOL_WINDOWS)):
                lanes = slice(grp * LANES, (grp + 1) * LANES)
                pooled, _ = _pooled_block(zpp, st, grp)
                o_ref[pl.ds(st, POOL_T), lanes] = (
                    _dot(pooled.astype(BF16), wp_ref[grp]) * sc_ref[0:1, lanes]).astype(o_ref.dtype)
            return carry

        lax.fori_loop(0, nb, blk, 0, unroll=2)

    return pl.pallas_call(
        body, name=name, grid=(nseq,),
        in_specs=[pl.BlockSpec((seq, POOL_WIDTH), lambda b: (b, 0)),
                  pl.BlockSpec((4, LANES, LANES), lambda b: (0, 0, 0)),
                  pl.BlockSpec((1, POOL_WIDTH), lambda b: (0, 0))],
        out_specs=pl.BlockSpec((seq, POOL_WIDTH), lambda b: (b, 0)),
        out_shape=jax.ShapeDtypeStruct((nseq * seq, POOL_WIDTH), BF16),
        scratch_shapes=[pltpu.VMEM((POOL_HALO + seq, POOL_WIDTH), F32)],
        compiler_params=_params("parallel"),
    )(zp, wp, scale)


def pool_bwd(zp, wp, scale, dout, nseq, seq, name, comm=None):
    nb = seq // POOL_T

    def body(z_ref, wp_ref, sc_ref, do_ref, dz_ref, dwp_ref, dsc_ref, zpp, dpcp, negd):
        @pl.when(pl.program_id(0) == 0)
        def _():
            dwp_ref[...] = jnp.zeros(dwp_ref.shape, F32)
            dsc_ref[...] = jnp.zeros(dsc_ref.shape, F32)

        zpp[0:POOL_HALO, :] = jnp.zeros((POOL_HALO, POOL_WIDTH), F32)
        zpp[POOL_HALO:, :] = z_ref[...]
        dpcp[seq:seq + POOL_HALO, :] = jnp.zeros((POOL_HALO, POOL_WIDTH), F32)

        def pass_a(n, carry):
            st = pl.multiple_of(n * POOL_T, POOL_T)
            for grp in range(len(POOL_WINDOWS)):
                lanes = slice(grp * LANES, (grp + 1) * LANES)
                pooled, inv = _pooled_block(zpp, st, grp)
                pb = pooled.astype(BF16)
                dob = do_ref[pl.ds(st, POOL_T), lanes]
                dsc_ref[0:1, lanes] += jnp.sum(dob * _dot(pb, wp_ref[grp]), axis=0, keepdims=True)
                dpm = (dob * sc_ref[0:1, lanes]).astype(BF16)
                dwp_ref[grp] += _dot_tn(pb, dpm)
                dpooled = _dot_nt(dpm, wp_ref[grp])
                negd[pl.ds(st, POOL_T), lanes] = -dpooled
                dpcp[pl.ds(st, POOL_T), lanes] = dpooled * inv
            return carry

        lax.fori_loop(0, nb, pass_a, 0, unroll=2)

        def pass_b(n, carry):
            st = pl.multiple_of(n * POOL_T, POOL_T)
            rows = POOL_T + POOL_HALO
            for grp in range(len(POOL_WINDOWS)):
                lanes = slice(grp * LANES, (grp + 1) * LANES)
                acc = dpcp[pl.ds(st, rows), lanes]
                for lvl in range(grp + 1):
                    acc = acc + pltpu.roll(acc, rows - (1 << lvl), 0)
                dz_ref[pl.ds(st, POOL_T), lanes] = (acc[0:POOL_T] + negd[pl.ds(st, POOL_T), lanes]).astype(dz_ref.dtype)
            return carry

        lax.fori_loop(0, nb, pass_b, 0, unroll=2)

    seq_spec = pl.BlockSpec((seq, POOL_WIDTH), lambda b: (b, 0))
    return _pcall(
        body, name, (nseq,),
        [seq_spec, pl.BlockSpec((4, LANES, LANES), lambda b: (0, 0, 0)),
         pl.BlockSpec((1, POOL_WIDTH), lambda b: (0, 0)), seq_spec],
        [seq_spec, pl.BlockSpec((4, LANES, LANES), lambda b: (0, 0, 0)),
         pl.BlockSpec((8, POOL_WIDTH), lambda b: (0, 0))],
        [jax.ShapeDtypeStruct((nseq * seq, POOL_WIDTH), BF16),
         jax.ShapeDtypeStruct((4, LANES, LANES), F32),
         jax.ShapeDtypeStruct((8, POOL_WIDTH), F32)],
        [pltpu.VMEM((POOL_HALO + seq, POOL_WIDTH), F32),
         pltpu.VMEM((seq + POOL_HALO, POOL_WIDTH), F32),
         pltpu.VMEM((seq, POOL_WIDTH), F32)], (zp, wp, scale, dout), ("arbitrary",), comm)


GELU_C0 = 0.7978845608028654
GELU_C1 = 0.044715


def _gelu(xv):
    return xv * (0.5 * (1.0 + jnp.tanh(GELU_C0 * (xv + GELU_C1 * (xv * xv * xv)))))


def _gelu_grad(xv):
    t = jnp.tanh(GELU_C0 * (xv + GELU_C1 * (xv * xv * xv)))
    return 0.5 * (1.0 + t) + 0.5 * xv * (1.0 - t * t) * (GELU_C0 * (1.0 + 3.0 * GELU_C1 * xv * xv))


def _tril():
    ti = lax.broadcasted_iota(jnp.int32, (SGU_CHUNK, SGU_CHUNK), 0)
    si = lax.broadcasted_iota(jnp.int32, (SGU_CHUNK, SGU_CHUNK), 1)
    return si <= ti


def sgu_fwd(zs, ws, bst, ln, nseq, seq, name):
    nc = seq // SGU_CHUNK

    def body(z_ref, ws_ref, bs_ref, ln_ref, o_ref):
        tril = _tril()

        def blk(n, carry):
            st = pl.multiple_of(n * SGU_CHUNK, SGU_CHUNK)
            ge = _gelu(z_ref[pl.ds(st, SGU_CHUNK), :])
            uu, vv = ge[:, :SGU_WIDTH], ge[:, SGU_WIDTH:]
            mu = jnp.mean(vv, axis=-1, keepdims=True)
            xc = vv - mu
            rstd = lax.rsqrt(jnp.mean(xc * xc, axis=-1, keepdims=True) + EPS)
            vn = (xc * rstd * ln_ref[0:1, :] + ln_ref[1:2, :]).astype(BF16)
            for g in range(4):
                lanes = slice(g * LANES, (g + 1) * LANES)
                wm = jnp.where(tril, ws_ref[g], 0.0).astype(BF16)
                mixed = _dot(wm, vn[:, lanes]) + bs_ref[:, g:g + 1]
                o_ref[pl.ds(st, SGU_CHUNK), lanes] = (uu[:, lanes] * mixed).astype(o_ref.dtype)
            return carry

        lax.fori_loop(0, nc, blk, 0, unroll=4)

    return pl.pallas_call(
        body, name=name, grid=(nseq,),
        in_specs=[pl.BlockSpec((seq, 2 * SGU_WIDTH), lambda b: (b, 0)),
                  pl.BlockSpec((4, LANES, LANES), lambda b: (0, 0, 0)),
                  pl.BlockSpec((SGU_CHUNK, 4), lambda b: (0, 0)),
                  pl.BlockSpec((8, SGU_WIDTH), lambda b: (0, 0))],
        out_specs=pl.BlockSpec((seq, SGU_WIDTH), lambda b: (b, 0)),
        out_shape=jax.ShapeDtypeStruct((nseq * seq, SGU_WIDTH), BF16),
        compiler_params=_params("parallel"),
    )(zs, ws, bst, ln)


def sgu_bwd(zs, ws, bst, ln, dout, nseq, seq, name, comm=None):
    nc = seq // SGU_CHUNK

    def body(z_ref, ws_ref, bs_ref, ln_ref, do_ref, dz_ref, dws_ref, dbs_ref, dln_ref):
        @pl.when(pl.program_id(0) == 0)
        def _():
            dws_ref[...] = jnp.zeros(dws_ref.shape, F32)
            dbs_ref[...] = jnp.zeros(dbs_ref.shape, F32)
            dln_ref[...] = jnp.zeros(dln_ref.shape, F32)

        tril = _tril()

        def blk(n, carry):
            st = pl.multiple_of(n * SGU_CHUNK, SGU_CHUNK)
            zv = z_ref[pl.ds(st, SGU_CHUNK), :]
            ge = _gelu(zv)
            uu, vv = ge[:, :SGU_WIDTH], ge[:, SGU_WIDTH:]
            mu = jnp.mean(vv, axis=-1, keepdims=True)
            xc = vv - mu
            rstd = lax.rsqrt(jnp.mean(xc * xc, axis=-1, keepdims=True) + EPS)
            xh = xc * rstd
            vn = (xh * ln_ref[0:1, :] + ln_ref[1:2, :]).astype(BF16)
            dob = do_ref[pl.ds(st, SGU_CHUNK), :]
            du_cols, dvn_cols = [], []
            for g in range(4):
                lanes = slice(g * LANES, (g + 1) * LANES)
                wm = jnp.where(tril, ws_ref[g], 0.0).astype(BF16)
                mixed = _dot(wm, vn[:, lanes]) + bs_ref[:, g:g + 1]
                du_cols.append(dob[:, lanes] * mixed)
                dmix = dob[:, lanes] * uu[:, lanes]
                dbs_ref[g] += jnp.broadcast_to(jnp.sum(dmix, axis=-1, keepdims=True), (SGU_CHUNK, LANES))
                dmb = dmix.astype(BF16)
                dws_ref[g] += jnp.where(tril, _dot_nt(dmb, vn[:, lanes]), 0.0)
                dvn_cols.append(_dot_tn(wm, dmb))
            dvn = jnp.concatenate(dvn_cols, axis=-1)
            dln_ref[0:1, :] += jnp.sum(dvn * xh, axis=0, keepdims=True)
            dln_ref[1:2, :] += jnp.sum(dvn, axis=0, keepdims=True)
            dxh = dvn * ln_ref[0:1, :]
            dv = rstd * (dxh - jnp.mean(dxh, axis=-1, keepdims=True)
                         - xh * jnp.mean(dxh * xh, axis=-1, keepdims=True))
            dge = jnp.concatenate(du_cols + [dv], axis=-1)
            dz_ref[pl.ds(st, SGU_CHUNK), :] = (dge * _gelu_grad(zv)).astype(dz_ref.dtype)
            return carry

        lax.fori_loop(0, nc, blk, 0, unroll=4)

    w_spec = pl.BlockSpec((4, LANES, LANES), lambda b: (0, 0, 0))
    ln_spec = pl.BlockSpec((8, SGU_WIDTH), lambda b: (0, 0))
    return _pcall(
        body, name, (nseq,),
        [pl.BlockSpec((seq, 2 * SGU_WIDTH), lambda b: (b, 0)), w_spec,
         pl.BlockSpec((SGU_CHUNK, 4), lambda b: (0, 0)), ln_spec,
         pl.BlockSpec((seq, SGU_WIDTH), lambda b: (b, 0))],
        [pl.BlockSpec((seq, 2 * SGU_WIDTH), lambda b: (b, 0)), w_spec, w_spec, ln_spec],
        [jax.ShapeDtypeStruct((nseq * seq, 2 * SGU_WIDTH), BF16),
         jax.ShapeDtypeStruct((4, LANES, LANES), F32),
         jax.ShapeDtypeStruct((4, LANES, LANES), F32),
         jax.ShapeDtypeStruct((8, SGU_WIDTH), F32)],
        [], (zs, ws, bst, ln, dout), ("arbitrary",), comm)


def _ew_rows(rows, cols, nbuf):
    t = _row_tile(rows, 1024)
    while t > 8 and t * cols * 4 * nbuf * 2 > 24 * 2**20:
        t //= 2
    return t


def adamw(w, g, m, v, name):
    layers, rows, cols = w.shape
    tr = _ew_rows(rows, cols, 7)

    def body(w_ref, g_ref, m_ref, v_ref, d_ref, mo_ref, vo_ref):
        gv = g_ref[...]
        mn = ADAM_B1 * m_ref[...] + (1.0 - ADAM_B1) * gv
        vn = ADAM_B2 * v_ref[...] + (1.0 - ADAM_B2) * (gv * gv)
        m_hat = mn / (1.0 - ADAM_B1 ** ADAM_STEP)
        v_hat = vn / (1.0 - ADAM_B2 ** ADAM_STEP)
        d_ref[...] = -ADAM_LR * (m_hat / (jnp.sqrt(v_hat) + ADAM_EPS) + ADAM_WD * w_ref[...])
        mo_ref[...] = mn
        vo_ref[...] = vn

    spec = pl.BlockSpec((1, tr, cols), lambda l, i: (l, i, 0))
    return pl.pallas_call(
        body, name=name, grid=(layers, rows // tr),
        in_specs=[spec] * 4, out_specs=[spec] * 3,
        out_shape=[jax.ShapeDtypeStruct(w.shape, F32)] * 3,
        compiler_params=_params("parallel", "parallel"),
    )(w, g, m, v)


def adamw_many(ws, gs, ms, vs, name):
    n = len(ws)

    def body(*refs):
        w_refs, g_refs, m_refs, v_refs = refs[:n], refs[n:2 * n], refs[2 * n:3 * n], refs[3 * n:4 * n]
        d_refs, mo_refs, vo_refs = refs[4 * n:5 * n], refs[5 * n:6 * n], refs[6 * n:7 * n]
        for i in range(n):
            gv = g_refs[i][...]
            mn = ADAM_B1 * m_refs[i][...] + (1.0 - ADAM_B1) * gv
            vn = ADAM_B2 * v_refs[i][...] + (1.0 - ADAM_B2) * (gv * gv)
            m_hat = mn / (1.0 - ADAM_B1 ** ADAM_STEP)
            v_hat = vn / (1.0 - ADAM_B2 ** ADAM_STEP)
            d_refs[i][...] = -ADAM_LR * (m_hat / (jnp.sqrt(v_hat) + ADAM_EPS) + ADAM_WD * w_refs[i][...])
            mo_refs[i][...] = mn
            vo_refs[i][...] = vn

    vmem = pl.BlockSpec(memory_space=pltpu.VMEM)
    shapes = [jax.ShapeDtypeStruct(w.shape, F32) for w in ws]
    res = pl.pallas_call(
        body, name=name, in_specs=[vmem] * (4 * n), out_specs=[vmem] * (3 * n), out_shape=shapes * 3,
        compiler_params=pltpu.CompilerParams(vmem_limit_bytes=VMEM_LIMIT),
    )(*ws, *gs, *ms, *vs)
    return res[:n], res[n:2 * n], res[2 * n:]


def add_cast(a, b, name, dtype=BF16):
    nslab, rows, cols = a.shape
    tr = _ew_rows(rows, cols, 3)

    def body(a_ref, b_ref, o_ref):
        o_ref[...] = (a_ref[...] + b_ref[...]).astype(dtype)

    spec = pl.BlockSpec((1, tr, cols), lambda k, i: (k, i, 0))
    return pl.pallas_call(
        body, name=name, grid=(nslab, rows // tr),
        in_specs=[spec, spec], out_specs=spec,
        out_shape=jax.ShapeDtypeStruct(a.shape, dtype),
        compiler_params=_params("parallel", "parallel"),
    )(a, b)


def pair_sum(t, got, core, name):
    nslab, h, cols = got.shape
    tr = _ew_rows(h, cols, 3)
    nb = h // tr

    def body(c_ref, a_ref, b_ref, o_ref):
        o_ref[...] = (a_ref[...] + b_ref[...]).astype(BF16)

    spec = pl.BlockSpec((1, tr, cols), lambda k, i, c: (k, i, 0))
    return pl.pallas_call(
        body, name=name,
        grid_spec=pltpu.PrefetchScalarGridSpec(
            num_scalar_prefetch=1, grid=(nslab, nb),
            in_specs=[pl.BlockSpec((1, tr, cols), lambda k, i, c: (k, c[0] * nb + i, 0)), spec],
            out_specs=spec),
        out_shape=jax.ShapeDtypeStruct(got.shape, BF16),
        compiler_params=_params("parallel", "parallel"),
    )(core, t, got)


def chip_sum(sums, parts, place, name):
    npart, h, cols = parts.shape
    tr = _ew_rows(h, cols, npart + 2)
    nb = h // tr

    def body(c_ref, own_ref, p_ref, o_ref):
        acc = own_ref[0].astype(F32)
        for j in range(npart):
            acc = acc + p_ref[j].astype(F32)
        o_ref[...] = acc

    return pl.pallas_call(
        body, name=name,
        grid_spec=pltpu.PrefetchScalarGridSpec(
            num_scalar_prefetch=1, grid=(nb,),
            in_specs=[pl.BlockSpec((1, tr, cols), lambda i, c: (c[1], i, 0)),
                      pl.BlockSpec((npart, tr, cols), lambda i, c: (0, i, 0))],
            out_specs=pl.BlockSpec((tr, cols), lambda i, c: (c[0] * nb + i, 0))),
        out_shape=jax.ShapeDtypeStruct((2 * h, cols), F32),
        compiler_params=_params("parallel"),
    )(place, sums, parts)


def sum_parts(parts, name, first=None):
    npart, rows, cols = parts.shape
    tr = _ew_rows(rows, cols, npart + 2)

    def body(*refs):
        p_ref, o_ref = refs[-2], refs[-1]
        acc = p_ref[0].astype(F32) if first is None else refs[0][...].astype(F32) + p_ref[0].astype(F32)
        for j in range(1, npart):
            acc = acc + p_ref[j].astype(F32)
        o_ref[...] = acc

    row = pl.BlockSpec((tr, cols), lambda i: (i, 0))
    ins = [parts] if first is None else [first, parts]
    return pl.pallas_call(
        body, name=name, grid=(rows // tr,),
        in_specs=([] if first is None else [row]) + [pl.BlockSpec((npart, tr, cols), lambda i: (0, i, 0))],
        out_specs=row,
        out_shape=jax.ShapeDtypeStruct((rows, cols), F32),
        compiler_params=_params("parallel"),
    )(*ins)


ANY = pl.BlockSpec(memory_space=pl.ANY)
MESH = pl.DeviceIdType.MESH


def _me():
    return lax.axis_index("x"), lax.axis_index("y"), lax.axis_index("c")


def _flip(pos, rel):
    return tuple(1 - p if f else p for p, f in zip(pos, rel))


SIBLING = (0, 0, 1)
OTHER_CHIPS = ((1, 0, 0), (0, 1, 0), (1, 1, 0))


def _chip_of(pos, rel=(0, 0, 0)):
    px, py, _ = _flip(pos, rel)
    return 2 * px + py


def allgather_blocks(shards, name):
    nt = len(shards)
    hs = [s.shape[0] // 2 for s in shards]

    def body(*refs):
        ins, outs = refs[:nt], refs[nt:2 * nt]
        send_sems, recv_sems, loc_sems = refs[2 * nt:]
        pos = _me()
        x, y, c = pos

        def block_id(rel):
            px, py, pc = _flip(pos, rel)
            return 4 * px + 2 * py + pc

        def copy(t, k, block_rel, to_rel, src=None):
            dst = outs[t].at[block_id(block_rel)]
            return pltpu.make_async_remote_copy(
                src_ref=dst if src is None else src, dst_ref=dst,
                send_sem=send_sems.at[t * 7 + k], recv_sem=recv_sems.at[t * 7 + k],
                device_id=_flip(pos, to_rel), device_id_type=MESH)

        own = [ins[t].at[pl.ds(c * hs[t], hs[t])] for t in range(nt)]
        mine = [pltpu.make_async_copy(own[t], outs[t].at[block_id((0, 0, 0))], loc_sems.at[t]) for t in range(nt)]
        for cp in mine:
            cp.start()
        first = []
        for t in range(nt):
            first.append(copy(t, 0, (0, 0, 0), SIBLING, src=own[t]))
            first += [copy(t, 1 + j, (0, 0, 0), rel, src=own[t]) for j, rel in enumerate(OTHER_CHIPS)]
        for cp in first:
            cp.start()
        passed = []
        for j, rel in enumerate(OTHER_CHIPS):
            for t in range(nt):
                copy(t, 1 + j, rel, (0, 0, 0)).wait_recv()
                fwd = copy(t, 4 + j, rel, SIBLING)
                fwd.start()
                passed.append(fwd)
        for t in range(nt):
            copy(t, 0, SIBLING, (0, 0, 0)).wait_recv()
            for j, rel in enumerate(OTHER_CHIPS):
                copy(t, 4 + j, (rel[0], rel[1], 1), (0, 0, 0)).wait_recv()
        for cp in first + passed:
            cp.wait_send()
        for cp in mine:
            cp.wait()

    return pl.pallas_call(
        body, name=name,
        in_specs=[ANY] * nt, out_specs=[ANY] * nt,
        out_shape=[jax.ShapeDtypeStruct((N_DEV, h, s.shape[1]), s.dtype) for h, s in zip(hs, shards)],
        scratch_shapes=[pltpu.SemaphoreType.DMA((7 * nt,)), pltpu.SemaphoreType.DMA((7 * nt,)),
                        pltpu.SemaphoreType.DMA((nt,))],
    )(*shards)


def _block_id(pos, rel=(0, 0, 0)):
    px, py, pc = _flip(pos, rel)
    return 4 * px + 2 * py + pc


def gather_first_hop(shards):
    hs = [s.shape[0] // 2 for s in shards]

    def plan(ins, outs, pos):
        me = _block_id(pos)
        remote = []
        for i, o, h in zip(ins, outs, hs):
            own = i.at[pl.ds(pos[2] * h, h)]
            remote += [(rel, own, o.at[me]) for rel in (SIBLING,) + OTHER_CHIPS]
        return remote

    return Comm(shards, [((N_DEV, h, s.shape[1]), s.dtype) for h, s in zip(hs, shards)], plan, 4 * len(shards))


def gather_second_hop(gathered):
    def plan(ins, outs, pos):
        remote = []
        for i, o in zip(ins, outs):
            for rel in OTHER_CHIPS:
                blk = _block_id(pos, rel)
                remote.append((SIBLING, i.at[blk], o.at[blk]))
        return remote

    return Comm(gathered, [(g.shape, g.dtype) for g in gathered], plan, 3 * len(gathered),
                aliases={i: i for i in range(len(gathered))})


def join_comm(bufs):
    def plan(ins, outs, pos):
        remote = []
        for i, o in zip(ins, outs):
            h = i.shape[0] // 2
            rows = pl.ds(pl.multiple_of(pos[2] * h, SUBLANES), h)
            remote.append((SIBLING, i.at[rows], o.at[rows]))
        return remote

    return Comm(list(bufs), [(b.shape, b.dtype) for b in bufs], plan, len(bufs),
                aliases={i: i for i in range(len(bufs))})


def give_half_comm(ts, plain=()):
    nt = len(ts)

    def plan(ins, outs, pos):
        remote = []
        for i, o in zip(ins[:nt], outs[:nt]):
            h = o.shape[1]
            remote.append((SIBLING, i.at[:, pl.ds((1 - pos[2]) * h, h)], o))
        return remote + [(SIBLING, i, o) for i, o in zip(ins[nt:], outs[nt:])]

    shapes = [((t.shape[0], t.shape[1] // 2, t.shape[2]), t.dtype) for t in ts] + [(v.shape, v.dtype) for v in plain]
    return Comm(list(ts) + list(plain), shapes, plan, nt + len(plain))


def chip_scatter_comm(xs, shared=None):
    nx = len(xs)

    def plan(ins, outs, pos):
        me = _chip_of(pos)
        remote = []
        for i, o in zip(ins[:nx], outs[:nx]):
            remote += [(rel, i.at[_chip_of(pos, rel)], o.at[j]) for j, rel in enumerate(OTHER_CHIPS)]
        if shared is not None:
            remote += [(rel, ins[nx], outs[nx].at[me]) for rel in OTHER_CHIPS]
        return remote

    shapes = [((3,) + v.shape[1:], v.dtype) for v in xs]
    if shared is not None:
        shapes.append(((N_CHIPS,) + shared.shape, shared.dtype))
    return Comm(list(xs) + ([] if shared is None else [shared]), shapes, plan, 3 * nx + (0 if shared is None else 3))


def tail_reduce(t, small, name):
    nslab, h2, cols = t.shape
    h = h2 // 2
    rows = small.shape[0]

    def body(t_ref, small_ref, full_ref, ssum_ref,
             got_pair, sums, got_chips, small_got, small_pair, small_chips, send_sems, recv_sems):
        pos = _me()
        core = pos[2]
        me = _chip_of(pos)

        def copy(i, rel, src, dst):
            return pltpu.make_async_remote_copy(src_ref=src, dst_ref=dst, send_sem=send_sems.at[i],
                                                recv_sem=recv_sems.at[i], device_id=_flip(pos, rel),
                                                device_id_type=MESH)

        pair = [copy(0, SIBLING, t_ref.at[:, pl.ds(pl.multiple_of((1 - core) * h, SUBLANES), h)], got_pair),
                copy(1, SIBLING, small_ref, small_got)]
        for cp in pair:
            cp.start()
        for cp in pair:
            cp.wait()
        for k in range(nslab):
            sums[k] = (t_ref[k, pl.ds(pl.multiple_of(core * h, SUBLANES), h), :] + got_pair[k]).astype(BF16)
        small_pair[...] = small_ref[...] + small_got[...]

        chips = []
        for j, rel in enumerate(OTHER_CHIPS):
            chips.append(copy(2 + j, rel, sums.at[_chip_of(pos, rel)], got_chips.at[j]))
            chips.append(copy(5 + j, rel, small_pair, small_chips.at[me]))
        for cp in chips:
            cp.start()
        small_chips[me] = small_pair[...]
        for cp in chips:
            cp.wait()
        acc = sums[me].astype(F32)
        for j in range(len(OTHER_CHIPS)):
            acc = acc + got_chips[j].astype(F32)
        mine = full_ref.at[pl.ds(pl.multiple_of(core * h, SUBLANES), h)]
        mine[...] = acc
        tot = small_chips[0]
        for k in range(1, N_CHIPS):
            tot = tot + small_chips[k]
        ssum_ref[...] = tot

        join = copy(8, SIBLING, mine, mine)
        join.start()
        join.wait()

    vmem = pl.BlockSpec(memory_space=pltpu.VMEM)
    return pl.pallas_call(
        body, name=name, in_specs=[vmem, vmem], out_specs=[vmem, vmem],
        out_shape=[jax.ShapeDtypeStruct((h2, cols), F32), jax.ShapeDtypeStruct((rows, LANES), F32)],
        scratch_shapes=[pltpu.VMEM((nslab, h, cols), F32), pltpu.VMEM((nslab, h, cols), BF16),
                        pltpu.VMEM((3, h, cols), BF16), pltpu.VMEM((rows, LANES), F32),
                        pltpu.VMEM((rows, LANES), F32), pltpu.VMEM((N_CHIPS, rows, LANES), F32),
                        pltpu.SemaphoreType.DMA((9,)), pltpu.SemaphoreType.DMA((9,))],
        compiler_params=pltpu.CompilerParams(vmem_limit_bytes=VMEM_LIMIT),
    )(t, small)


PACK_ROWS = 256


def _pack(arrs):
    parts, layout = [], []
    row = 0
    for a in arrs:
        flat = a.reshape(-1).astype(F32)
        size = flat.shape[0]
        rows = -(-size // (8 * LANES)) * 8
        flat = jnp.pad(flat, (0, rows * LANES - size))
        parts.append(flat.reshape(rows, LANES))
        layout.append((row, rows, size, a.shape))
        row += rows
    if row % PACK_ROWS:
        parts.append(jnp.zeros((PACK_ROWS - row % PACK_ROWS, LANES), F32))
    return jnp.concatenate(parts, axis=0), layout


def _unpack(packed, layout):
    return [packed[r0:r0 + rows].reshape(-1)[:size].reshape(shape) for r0, rows, size, shape in layout]


SMALL_REPL = ['mix_norm', 'a_b_in', 'a_sinks', 'a_conv_b', 'a_cln_g', 'a_cln_b', 'c_w_pool', 'c_w_s', 'c_b_s',
              'ffn_norm', 'final_norm']
SMALL_SHARD = ['a_conv_w', 'c_pool_scale', 'c_sln_g', 'c_sln_b']
BIG = ['a_w_in', 'a_w_out', 'c_w_in', 'c_w_out', 'ffn_w_gate', 'ffn_w_up', 'ffn_w_down']
TRANSPOSED = ('a_w_in', 'ffn_w_gate', 'ffn_w_up')
BIG_COL_SHARDED = {'c_w_in'}


def _full_weight(name, g8):
    _, h, cols = g8.shape
    g4 = g8.reshape(N_CHIPS, 2 * h, cols)
    if name not in BIG_COL_SHARDED:
        return g4.reshape(-1, cols)
    return jnp.transpose(g4, (1, 0, 2)).reshape(2 * h, N_CHIPS * cols)


def _to_shard_major(name, f):
    if name not in BIG_COL_SHARDED:
        return f.reshape(N_CHIPS, f.shape[0] // N_CHIPS, f.shape[1])
    r, cfull = f.shape
    return jnp.transpose(f.reshape(r, N_CHIPS, cfull // N_CHIPS), (1, 0, 2))


def kernel(*args):
    a = dict(zip(IN_NAMES, args))
    bl, seq, _ = a['x'].shape
    n = bl * seq
    x = a['x'].reshape(n, D_MODEL)
    target = a['loss_target'].reshape(n, D_MODEL)
    xi, yi, ci = _me()
    chip = 2 * xi + yi

    shard = {'a_w_in': a['a_w_in'][0].T, 'a_w_out': a['a_w_out'][0], 'c_w_in': a['c_w_in'][0], 'c_w_out': a['c_w_out'][0]}
    for layer in range(2):
        shard['gate' + str(layer)] = a['ffn_w_gate'][layer].T
        shard['up' + str(layer)] = a['ffn_w_up'][layer].T
        shard['down' + str(layer)] = a['ffn_w_down'][layer]
    shard = {k: v.astype(BF16) for k, v in shard.items()}
    core = ci.astype(jnp.int32).reshape(1)
    place = jnp.stack([ci, chip]).astype(jnp.int32)
    block_id = 4 * xi + 2 * yi + ci

    def first_hop(*names):
        return gather_first_hop([shard[k] for k in names])

    def finish(name, g8):
        h = shard[name].shape[0] // 2
        own = lax.dynamic_slice_in_dim(shard[name], ci * h, h, axis=0)
        return _full_weight(name, lax.dynamic_update_slice_in_dim(g8, own[None], block_id, axis=0))

    a_w_in_t = _full_weight('a_w_in', allgather_blocks([shard['a_w_in']], "gather_a_w_in")[0])
    in0_width = a_w_in_t.shape[0]
    small_shard_pack, small_shard_layout = _pack([a[k] for k in SMALL_SHARD])
    hop_a = first_hop('a_w_out', 'c_w_out')
    hop_s = chip_scatter_comm([], shared=small_shard_pack)
    mix_norm, ffn_norm = a['mix_norm'], a['ffn_norm']
    (hn0, q, kv, cc), outs = norm_inproj(
        x, mix_norm[0:1], a_w_in_t, a['a_b_in'],
        [(0, ATTN_WIDTH), (ATTN_WIDTH, ATTN_WIDTH + 2 * KV_WIDTH), (ATTN_WIDTH + 2 * KV_WIDTH, in0_width)],
        [BF16, BF16, F32], "in_proj0", comm=hop_a + hop_s, w_transposed=True)
    got_a, (ss,) = hop_a.split(outs, hop_s)
    ss = lax.dynamic_update_slice_in_dim(ss, small_shard_pack[None], chip, axis=0)
    ss_full = []
    for r0, rows, size, shape in small_shard_layout:
        per_chip = ss[:, r0:r0 + rows].reshape(N_CHIPS, -1)[:, :size].reshape((N_CHIPS,) + shape)
        ss_full.append(jnp.concatenate([per_chip[k] for k in range(N_CHIPS)], axis=-1))
    a_conv_w, c_pool_scale, c_sln_g, c_sln_b = [v[0] for v in ss_full]

    conv_taps = jnp.pad(a_conv_w, ((0, 32 - CONV_KERNEL), (0, 0)))
    conv_vec = jnp.pad(jnp.stack([a['a_conv_b'][0], a['a_cln_g'][0], a['a_cln_b'][0]]), ((0, 5), (0, 0)))
    sinks_b = jnp.pad(jnp.repeat(a['a_sinks'][0].reshape(N_KV_HEADS, GROUP), ATTN_BLOCK, axis=1), ((0, 6), (0, 0)))
    w_pool_bf = a['c_w_pool'][0].astype(BF16)
    pool_scale = c_pool_scale.reshape(1, POOL_WIDTH)
    w_s = a['c_w_s'][0]
    b_s_t = a['c_b_s'][0].T
    sgu_ln = jnp.pad(jnp.stack([c_sln_g, c_sln_b]), ((0, 6), (0, 0)))
    final_norm = a['final_norm'].reshape(1, D_MODEL)

    hop_b, pass_a = first_hop('gate0', 'c_w_in'), gather_second_hop(got_a)
    attn, outs = attn_fwd(q, kv, sinks_b, bl, seq, "attn_fwd", comm=hop_b + pass_a)
    got_b, done = hop_b.split(outs, pass_a)
    a_w_out, c_w_out = finish('a_w_out', done[0]), finish('c_w_out', done[1])

    hop_c, pass_b = first_hop('up0', 'down0'), gather_second_hop(got_b)
    (conv, conv_h1), outs = conv_fwd(cc, conv_taps, conv_vec, bl, seq, "conv_fwd", comm=hop_c + pass_b)
    got_c, done = hop_c.split(outs, pass_b)
    wg0, c_w_in = finish('gate0', done[0]), finish('c_w_in', done[1])

    h1, done = out_proj(x, attn, conv, a_w_out, "out_proj0", comm=gather_second_hop(got_c))
    wu0, wd0 = finish('up0', done[0]), finish('down0', done[1])

    (hnf0, g0, u0), got_e = ffn_gate_up(h1, ffn_norm[0:1], wg0, wu0, "ffn_gate_up0",
                                        comm=first_hop('gate1', 'up1', 'down1'))

    h2, done = ffn_down(h1, g0, u0, wd0, "ffn_down0", comm=gather_second_hop(got_e))
    wg1, wu1, wd1 = finish('gate1', done[0]), finish('up1', done[1]), finish('down1', done[2])
    wg, wu, wd = [wg0, wg1], [wu0, wu1], [wd0, wd1]

    (hn1, zp, zs), _ = norm_inproj(
        h2, mix_norm[1:2], c_w_in, jnp.zeros((1, c_w_in.shape[1]), F32),
        [(0, POOL_WIDTH), (POOL_WIDTH, c_w_in.shape[1])], [F32, F32], "in_proj1")
    pool = pool_fwd(zp, w_pool_bf, pool_scale, bl, seq, "pool_fwd")
    sgu = sgu_fwd(zs, w_s, b_s_t, sgu_ln, bl, seq, "sgu_fwd")
    h3, _ = out_proj(h2, pool, sgu, c_w_out, "out_proj1")
    (hnf1, g1, u1), _ = ffn_gate_up(h3, ffn_norm[1:2], wg1, wu1, "ffn_gate_up1")
    h4, _ = ffn_down(h3, g1, u1, wd1, "ffn_down1")

    dh4, d_final_norm, loss_local = loss_head(h4, final_norm, target, "loss_head")

    grads = {}
    pieces = {}

    def slabs_of(names, fulls):
        return [_to_shard_major(k, fulls[k]) for k in names]

    def pair_sums_of(names, slabs, gots):
        return [pair_sum(t, gt, core, "pair_sum_" + k) for k, t, gt in zip(names, slabs, gots)]

    def chip_sums_of(names, sums, from_chips):
        return [chip_sum(s, p, place, "chip_sum_" + k) for k, p, s in zip(names, from_chips, sums)]

    (dg, du, act), _ = ffn_down_bwd(dh4, g1, u1, wd[1], "ffn_down_bwd1")
    full1 = {'down1': mm_tn(act, dh4, "dw_down1"), 'gate1': mm_tn(dg, hnf1, "dw_gate1"),
             'up1': mm_tn(du, hnf1, "dw_up1")}
    names1 = ['gate1', 'up1', 'down1']
    slabs1 = slabs_of(names1, full1)
    dh3, d_ffn_norm1, got1 = proj_rms_bwd([dg, du], [wg[1], wu[1]], h3, ffn_norm[1:2], dh4, 1, "ffn_up_bwd1",
                                          tm_pref=512, w_transposed=True, comm=give_half_comm(slabs1))
    sums1 = pair_sums_of(names1, slabs1, got1)
    d_pool, d_sgu = out_proj_bwd(dh3, c_w_out, [F32, F32], "out_proj_bwd1")
    full1['c_w_out'] = jnp.concatenate([mm_tn(pool, dh3, "dw_out1_pool"), mm_tn(sgu, dh3, "dw_out1_sgu")], axis=0)
    (dzp, d_w_pool, d_pool_scale), from_gate = pool_bwd(zp, w_pool_bf, pool_scale, d_pool, bl, seq, "pool_bwd",
                                                        comm=chip_scatter_comm(sums1[0:1]))
    (dzs, d_w_s, d_b_s_b, d_sgu_ln), from_up = sgu_bwd(zs, w_s, b_s_t, sgu_ln, d_sgu, bl, seq, "sgu_bwd",
                                                       comm=chip_scatter_comm(sums1[1:2]))
    full1['c_w_in'] = jnp.concatenate([mm_tn(hn1, dzp, "dw_in1_pool"), mm_tn(hn1, dzs, "dw_in1_sgu")], axis=1)
    names1b = ['c_w_out', 'c_w_in']
    slabs1b = slabs_of(names1b, full1)
    heavy_pack, heavy_layout = _pack([d_w_pool[None], d_w_s[None]])
    chips_down, pair1b = chip_scatter_comm(sums1[2:3]), give_half_comm(slabs1b, plain=[heavy_pack])
    dh2, d_mix_norm1, outs = proj_rms_bwd([dzp, dzs], [c_w_in[:, :POOL_WIDTH], c_w_in[:, POOL_WIDTH:]], h2,
                                          mix_norm[1:2], dh3, 1, "in_proj_bwd1", comm=chips_down + pair1b)
    from_down, got1b = chips_down.split(outs, pair1b)
    mine1 = chip_sums_of(names1, sums1, from_gate + from_up + from_down)
    sums1b = pair_sums_of(names1b, slabs1b, got1b[:2])
    heavy_pair = add_cast(heavy_pack[None], got1b[2][None], "pair_sum_heavy", dtype=F32)[0]

    join1, chips1b = join_comm(mine1), chip_scatter_comm(sums1b, shared=heavy_pair)
    (dg, du, act), outs = ffn_down_bwd(dh2, g0, u0, wd[0], "ffn_down_bwd0", comm=join1 + chips1b)
    whole1, from_chips1b = join1.split(outs, chips1b)
    pieces.update(dict(zip(names1, whole1)))
    mine1b = chip_sums_of(names1b, sums1b, from_chips1b[:2])
    heavy_chips = lax.dynamic_update_slice_in_dim(from_chips1b[2], heavy_pair[None], chip, axis=0)
    grads['c_w_pool'], grads['c_w_s'] = _unpack(sum_parts(heavy_chips, "heavy_sum"), heavy_layout)
    full0 = {'down0': mm_tn(act, dh2, "dw_down0"), 'gate0': mm_tn(dg, hnf0, "dw_gate0"),
             'up0': mm_tn(du, hnf0, "dw_up0")}
    names0 = ['gate0', 'up0', 'down0']
    slabs0 = slabs_of(names0, full0)
    join1b, pair0 = join_comm(mine1b), give_half_comm(slabs0)
    dh1, d_ffn_norm0, outs = proj_rms_bwd([dg, du], [wg[0], wu[0]], h1, ffn_norm[0:1], dh2, 1, "ffn_up_bwd0",
                                          tm_pref=512, comm=join1b + pair0, w_transposed=True)
    whole1b, got0 = join1b.split(outs, pair0)
    pieces.update(dict(zip(names1b, whole1b)))
    sums0 = pair_sums_of(names0, slabs0, got0)

    d_attn, d_conv = out_proj_bwd(dh1, a_w_out, [BF16, F32], "out_proj_bwd0")
    full_o = {'a_w_out': jnp.concatenate([mm_tn(attn, dh1, "dw_out0_attn"), mm_tn(conv, dh1, "dw_out0_conv")], axis=0)}
    slabs_o = slabs_of(['a_w_out'], full_o)
    chips0, pair_o = chip_scatter_comm(sums0), give_half_comm(slabs_o)
    (dq, dkv, d_sinks_b), outs = attn_bwd(q, kv, sinks_b, d_attn, bl, seq, "attn_bwd", comm=chips0 + pair_o)
    from_chips0, got_o = chips0.split(outs, pair_o)
    mine0 = chip_sums_of(names0, sums0, from_chips0)
    sums_o = pair_sums_of(['a_w_out'], slabs_o, got_o)
    join0, chips_o = join_comm(mine0), chip_scatter_comm(sums_o)
    (dcc, d_conv_taps, d_conv_vec), outs = conv_bwd(cc, conv_h1, conv_taps, conv_vec, d_conv, bl, seq, "conv_bwd",
                                                    comm=join0 + chips_o)
    whole0, from_chips_o = join0.split(outs, chips_o)
    pieces.update(dict(zip(names0, whole0)))
    mine_o = chip_sums_of(['a_w_out'], sums_o, from_chips_o)
    kq, kk = ATTN_WIDTH, ATTN_WIDTH + 2 * KV_WIDTH
    grad_x, d_mix_norm0, _ = proj_rms_bwd([dq, dkv, dcc], [a_w_in_t[:kq], a_w_in_t[kq:kk], a_w_in_t[kk:]], x,
                                          mix_norm[0:1], dh1, 1, "in_proj_bwd0", w_transposed=True)
    dw_q, db_q = mm_tn(dq, hn0, "dw_in0_q", xsum=True)
    dw_kv, db_kv = mm_tn(dkv, hn0, "dw_in0_kv", xsum=True)
    (dw_c, db_c), whole_o = mm_tn(dcc, hn0, "dw_in0_c", xsum=True, comm=join_comm(mine_o))
    pieces['a_w_out'] = whole_o[0]
    d_a_b_in = jnp.concatenate([db_q, db_kv, db_c], axis=0)
    slabs_i = slabs_of(['a_w_in'], {'a_w_in': jnp.concatenate([dw_q, dw_kv, dw_c], axis=0)})

    small_full = {
        'mix_norm': jnp.stack([d_mix_norm0, d_mix_norm1]), 'a_b_in': d_a_b_in[None], 'a_sinks': d_sinks_b[:, 0][None],
        'a_conv_w': d_conv_taps[:CONV_KERNEL][None], 'a_conv_b': d_conv_vec[0][None], 'a_cln_g': d_conv_vec[1][None],
        'a_cln_b': d_conv_vec[2][None], 'c_pool_scale': d_pool_scale[0][None],
        'c_sln_g': d_sgu_ln[0][None], 'c_sln_b': d_sgu_ln[1][None],
        'c_b_s': d_b_s_b[:, :, 0][None], 'ffn_norm': jnp.stack([d_ffn_norm0, d_ffn_norm1]),
        'final_norm': d_final_norm, 'loss': loss_local.reshape(1)}
    small_names = SMALL_REPL + SMALL_SHARD
    tail_names = [k for k in small_names if k in small_full] + ['loss']
    small_pack, small_layout = _pack([small_full[k] for k in tail_names])

    pieces['a_w_in'], small_sum = tail_reduce(slabs_i[0], small_pack, "tail_reduce")

    for k in ('a_w_in', 'a_w_out', 'c_w_in', 'c_w_out'):
        grads[k] = pieces[k][None]
    for short, key in (('gate', 'ffn_w_gate'), ('up', 'ffn_w_up'), ('down', 'ffn_w_down')):
        grads[key] = jnp.stack([pieces[short + '0'], pieces[short + '1']])

    for k, g in zip(tail_names, _unpack(small_sum, small_layout)):
        if k in SMALL_SHARD:
            width = a[k].shape[-1]
            g = lax.dynamic_slice_in_dim(g, chip * width, width, axis=g.ndim - 1)
        grads[k] = g
    loss = grads.pop('loss')[0]

    delta, new_m, new_v = {}, {}, {}
    for k in BIG:
        if k in TRANSPOSED:
            flip = lambda t: jnp.swapaxes(t, 1, 2)
            d, m, v = adamw(flip(a[k]), grads[k], flip(a['m_' + k]), flip(a['v_' + k]), "adamw_" + k)
            grads[k], delta[k], new_m[k], new_v[k] = flip(grads[k]), flip(d), flip(m), flip(v)
        else:
            delta[k], new_m[k], new_v[k] = adamw(a[k], grads[k], a['m_' + k], a['v_' + k], "adamw_" + k)
    two_d = lambda t: t.reshape(1, -1) if t.ndim == 1 else t
    ds, ms, vs = adamw_many([two_d(a[k]) for k in small_names], [two_d(grads[k]) for k in small_names],
                            [two_d(a['m_' + k]) for k in small_names], [two_d(a['v_' + k]) for k in small_names],
                            "adamw_small")
    for k, dv, mv, vv in zip(small_names, ds, ms, vs):
        delta[k], new_m[k], new_v[k] = [t.reshape(a[k].shape) for t in (dv, mv, vv)]

    return (loss, grad_x.reshape(a['x'].shape), *[grads[k] for k in WEIGHTS], *[delta[k] for k in WEIGHTS],
            *[new_m[k] for k in WEIGHTS], *[new_v[k] for k in WEIGHTS])
```
